```python
import math
import jax, jax.numpy as jnp
from jax import lax
import numpy as np

D_MODEL = 1024
BATCH = 8
SEQ = 8192
DEPTH = 1

MEM_LEN = 256
SSD_EXPAND = 2
SSD_D_INNER = SSD_EXPAND * D_MODEL
SSD_HEAD_DIM = 64
SSD_HEADS = SSD_D_INNER // SSD_HEAD_DIM
SSD_GROUPS = 4
SSD_HEADS_PER_GROUP = SSD_HEADS // SSD_GROUPS
SSD_STATE = 128
SSD_CONV = 4
SSD_CHUNK = 128
SSD_CONV_DIM = SSD_D_INNER + 2 * SSD_GROUPS * SSD_STATE
GMLP_WIDTH = D_MODEL
GMLP_CHUNK = 128
GMLP_GROUPS = 8
GMLP_GROUP_DIM = GMLP_WIDTH // GMLP_GROUPS
MEM_HEADS = 4
MEM_HEAD_DIM = 64
MEM_WIDTH = MEM_HEADS * MEM_HEAD_DIM
D_FF = ((8 * D_MODEL // 3 + 127) // 128) * 128
N_BRANCH = 3
EPS = 1e-6

_IN_SIZES = (SSD_D_INNER, SSD_CONV_DIM, SSD_HEADS, 2 * GMLP_WIDTH, MEM_WIDTH, N_BRANCH * D_MODEL)
IN_WIDTH = sum(_IN_SIZES)
_IN_SPLITS = tuple(sum(_IN_SIZES[:i + 1]) for i in range(len(_IN_SIZES) - 1))

kernel_name = "hybrid_ssd_gmlp_memory_gated_block"


def rmsnorm(x, g):
    xf = x.astype(jnp.float32)
    y = xf * lax.rsqrt(jnp.mean(xf * xf, axis=-1, keepdims=True) + EPS)
    return (y * g.astype(jnp.float32)).astype(x.dtype)


def swiglu(x, w_gate, w_up, w_down):
    return (jax.nn.silu(x @ w_gate) * (x @ w_up)) @ w_down


def causal_dwconv(x, w, b):
    k, c = w.shape
    y = lax.conv_general_dilated(
        x, w[:, None, :].astype(x.dtype), window_strides=(1,), padding=[(k - 1, 0)],
        dimension_numbers=("NWC", "WIO", "NWC"), feature_group_count=c)
    return y + b.astype(x.dtype)


def ssd_chunked(xh, dt, a, bm, cm):
    bsz, s, h, p = xh.shape
    g, n = bm.shape[-2:]
    r = h // g
    l = SSD_CHUNK
    c = s // l
    x = (xh.astype(jnp.float32) * dt[..., None]).reshape(bsz, c, l, g, r, p)
    bm = bm.astype(jnp.float32).reshape(bsz, c, l, g, n)
    cm = cm.astype(jnp.float32).reshape(bsz, c, l, g, n)
    a_cs = jnp.cumsum((dt * a).reshape(bsz, c, l, g, r), axis=2)
    causal = jnp.tril(jnp.ones((l, l), dtype=bool))[:, :, None, None]
    seg = a_cs[:, :, :, None] - a_cs[:, :, None, :]
    decay = jnp.exp(jnp.where(causal, seg, -jnp.inf))
    cb = jnp.einsum("bclgn,bcsgn->bclsg", cm, bm)
    y_diag = jnp.einsum("bclsgr,bcsgrp->bclgrp", cb[..., None] * decay, x)
    decay_to_end = jnp.exp(a_cs[:, :, -1:] - a_cs)
    chunk_states = jnp.einsum("bclgn,bclgrp->bcgrpn", bm, x * decay_to_end[..., None])
    chunk_decay = jnp.exp(a_cs[:, :, -1])

    def step(state, inp):
        dec, st = inp
        return state * dec[..., None, None] + st, state

    init = jnp.zeros((bsz, g, r, p, n), jnp.float32)
    _, prev = lax.scan(step, init, (jnp.moveaxis(chunk_decay, 1, 0), jnp.moveaxis(chunk_states, 1, 0)))
    prev = jnp.moveaxis(prev, 0, 1)
    y_off = jnp.einsum("bclgn,bcgrpn->bclgrp", cm, prev) * jnp.exp(a_cs)[..., None]
    return (y_diag + y_off).reshape(bsz, s, h, p)


def ssd_branch(z, xbc, dt_raw, conv_w, conv_b, dt_bias, a_log, d_skip, norm_g):
    bsz, s, _ = z.shape
    xbc = jax.nn.silu(causal_dwconv(xbc, conv_w, conv_b))
    xs, bm, cm = jnp.split(xbc, (SSD_D_INNER, SSD_D_INNER + SSD_GROUPS * SSD_STATE), axis=-1)
    xh = xs.reshape(bsz, s, SSD_HEADS, SSD_HEAD_DIM)
    bm = bm.reshape(bsz, s, SSD_GROUPS, SSD_STATE)
    cm = cm.reshape(bsz, s, SSD_GROUPS, SSD_STATE)
    dt = jax.nn.softplus(dt_raw.astype(jnp.float32) + dt_bias.astype(jnp.float32))
    a = -jnp.exp(a_log.astype(jnp.float32))
    y = ssd_chunked(xh, dt, a, bm, cm) + d_skip.astype(jnp.float32)[:, None] * xh.astype(jnp.float32)
    yg = (y.reshape(bsz, s, SSD_D_INNER) * jax.nn.silu(z.astype(jnp.float32))).reshape(bsz, s, SSD_GROUPS, -1)
    yg = yg * lax.rsqrt(jnp.mean(yg * yg, axis=-1, keepdims=True) + EPS)
    y = yg.reshape(bsz, s, SSD_D_INNER) * norm_g.astype(jnp.float32)
    return y.astype(z.dtype)


def gmlp_branch(uv, v_norm_g, w_s, b_s):
    bsz, s, _ = uv.shape
    u, v = jnp.split(jax.nn.gelu(uv, approximate=False), 2, axis=-1)
    v = rmsnorm(v, v_norm_g)
    vb = v.reshape(bsz, s // GMLP_CHUNK, GMLP_CHUNK, GMLP_GROUPS, GMLP_GROUP_DIM)
    mask = jnp.tril(jnp.ones((GMLP_CHUNK, GMLP_CHUNK), dtype=bool))
    ws = jnp.where(mask[None], w_s, 0).astype(v.dtype)
    mixed = jnp.einsum("gts,bcsgd->bctgd", ws, vb) + b_s.T.astype(v.dtype)[None, None, :, :, None]
    return u * mixed.reshape(bsz, s, GMLP_WIDTH)


def memory_attention(q, mem_n, w_mem_kv):
    bsz, s, _ = q.shape
    k, v = jnp.split(mem_n @ w_mem_kv, 2, axis=-1)
    q = q.reshape(bsz, s, MEM_HEADS, MEM_HEAD_DIM)
    k = k.reshape(bsz, -1, MEM_HEADS, MEM_HEAD_DIM)
    v = v.reshape(bsz, -1, MEM_HEADS, MEM_HEAD_DIM)
    scores = jnp.einsum("bshd,bmhd->bhsm", q, k).astype(jnp.float32) * (1.0 / math.sqrt(MEM_HEAD_DIM))
    probs = jax.nn.softmax(scores, axis=-1).astype(v.dtype)
    return jnp.einsum("bhsm,bmhd->bshd", probs, v).reshape(bsz, s, MEM_WIDTH)


def _fwd_setup_inputs(seed: int = 0) -> dict:
    key = jax.random.key(seed)
    ks = iter(jax.random.split(key, 40))

    def nrm(shape, scale):
        return jax.random.normal(next(ks), shape, jnp.float32) * scale

    def gain(shape):
        return 1.0 + nrm(shape, 0.02)

    d = DEPTH
    dt0 = jnp.exp(jax.random.uniform(next(ks), (d, SSD_HEADS), jnp.float32, math.log(1e-3), math.log(1e-1)))
    dt_bias = dt0 + jnp.log(-jnp.expm1(-dt0))
    a_log = jnp.log(jax.random.uniform(next(ks), (d, SSD_HEADS), jnp.float32, 1.0, 16.0))
    return {
        "x": nrm((BATCH, SEQ, D_MODEL), 1.0),
        "mem": nrm((BATCH, MEM_LEN, D_MODEL), 1.0),
        "ffn1_norm": gain((d, D_MODEL)),
        "ffn1_w_gate": nrm((d, D_MODEL, D_FF), D_MODEL ** -0.5),
        "ffn1_w_up": nrm((d, D_MODEL, D_FF), D_MODEL ** -0.5),
        "ffn1_w_down": nrm((d, D_FF, D_MODEL), D_FF ** -0.5),
        "mix_norm": gain((d, D_MODEL)),
        "mem_norm": gain((d, D_MODEL)),
        "w_in": nrm((d, D_MODEL, IN_WIDTH), D_MODEL ** -0.5),
        "ssd_conv_w": nrm((d, SSD_CONV, SSD_CONV_DIM), SSD_CONV ** -0.5),
        "ssd_conv_b": nrm((d, SSD_CONV_DIM), 0.01),
        "ssd_dt_bias": dt_bias,
        "ssd_a_log": a_log,
        "ssd_d": gain((d, SSD_HEADS)),
        "ssd_norm": gain((d, SSD_D_INNER)),
        "gmlp_v_norm": gain((d, GMLP_WIDTH)),
        "gmlp_w_s": nrm((d, GMLP_GROUPS, GMLP_CHUNK, GMLP_CHUNK), GMLP_CHUNK ** -0.5),
        "gmlp_b_s": gain((d, GMLP_GROUPS, GMLP_CHUNK)),
        "w_mem_kv": nrm((d, D_MODEL, 2 * MEM_WIDTH), D_MODEL ** -0.5),
        "w_branch_ssd": nrm((d, SSD_D_INNER, D_MODEL), SSD_D_INNER ** -0.5),
        "w_branch_gmlp": nrm((d, GMLP_WIDTH, D_MODEL), GMLP_WIDTH ** -0.5),
        "w_branch_mem": nrm((d, MEM_WIDTH, D_MODEL), MEM_WIDTH ** -0.5),
        "w_out": nrm((d, D_MODEL, D_MODEL), D_MODEL ** -0.5),
        "ffn2_norm": gain((d, D_MODEL)),
        "ffn2_w_gate": nrm((d, D_MODEL, D_FF), D_MODEL ** -0.5),
        "ffn2_w_up": nrm((d, D_MODEL, D_FF), D_MODEL ** -0.5),
        "ffn2_w_down": nrm((d, D_FF, D_MODEL), D_FF ** -0.5),
        "final_norm": gain((D_MODEL,)),
    }


def _fwd_reference(x, mem, ffn1_norm, ffn1_w_gate, ffn1_w_up, ffn1_w_down, mix_norm, mem_norm, w_in,
              ssd_conv_w, ssd_conv_b, ssd_dt_bias, ssd_a_log, ssd_d, ssd_norm,
              gmlp_v_norm, gmlp_w_s, gmlp_b_s, w_mem_kv,
              w_branch_ssd, w_branch_gmlp, w_branch_mem, w_out,
              ffn2_norm, ffn2_w_gate, ffn2_w_up, ffn2_w_down, final_norm):
    h = x
    for layer in range(DEPTH):
        h = h + 0.5 * swiglu(rmsnorm(h, ffn1_norm[layer]), ffn1_w_gate[layer], ffn1_w_up[layer], ffn1_w_down[layer])
        n = rmsnorm(h, mix_norm[layer])
        z, xbc, dt_raw, uv, q_mem, gate_logits = jnp.split(n @ w_in[layer], _IN_SPLITS, axis=-1)
        y_ssd = ssd_branch(z, xbc, dt_raw, ssd_conv_w[layer], ssd_conv_b[layer], ssd_dt_bias[layer],
                           ssd_a_log[layer], ssd_d[layer], ssd_norm[layer])
        y_gmlp = gmlp_branch(uv, gmlp_v_norm[layer], gmlp_w_s[layer], gmlp_b_s[layer])
        y_mem = memory_attention(q_mem, rmsnorm(mem, mem_norm[layer]), w_mem_kv[layer])
        g_ssd, g_gmlp, g_mem = jnp.split(jax.nn.sigmoid(gate_logits), N_BRANCH, axis=-1)
        merged = (g_ssd * (y_ssd @ w_branch_ssd[layer])
                  + g_gmlp * (y_gmlp @ w_branch_gmlp[layer])
                  + g_mem * (y_mem @ w_branch_mem[layer]))
        h = h + merged @ w_out[layer]
        h = h + 0.5 * swiglu(rmsnorm(h, ffn2_norm[layer]), ffn2_w_gate[layer], ffn2_w_up[layer], ffn2_w_down[layer])
    return rmsnorm(h, final_norm)


import jax as _jax
import jax.numpy as _jnp

TWIN_FORMAT = 'train_step'
FWD_PARAMS = ['x', 'mem', 'ffn1_norm', 'ffn1_w_gate', 'ffn1_w_up', 'ffn1_w_down', 'mix_norm', 'mem_norm', 'w_in', 'ssd_conv_w', 'ssd_conv_b', 'ssd_dt_bias', 'ssd_a_log', 'ssd_d', 'ssd_norm', 'gmlp_v_norm', 'gmlp_w_s', 'gmlp_b_s', 'w_mem_kv', 'w_branch_ssd', 'w_branch_gmlp', 'w_branch_mem', 'w_out', 'ffn2_norm', 'ffn2_w_gate', 'ffn2_w_up', 'ffn2_w_down', 'final_norm']
TWIN_WEIGHTS = ['ffn1_norm', 'ffn1_w_gate', 'ffn1_w_up', 'ffn1_w_down', 'mix_norm', 'mem_norm', 'w_in', 'ssd_conv_w', 'ssd_conv_b', 'ssd_dt_bias', 'ssd_a_log', 'ssd_d', 'ssd_norm', 'gmlp_v_norm', 'gmlp_w_s', 'gmlp_b_s', 'w_mem_kv', 'w_branch_ssd', 'w_branch_gmlp', 'w_branch_mem', 'w_out', 'ffn2_norm', 'ffn2_w_gate', 'ffn2_w_up', 'ffn2_w_down', 'final_norm']
TWIN_DIFF_INPUT = 'x'
TWIN_INPUTS = ['x', 'mem', 'ffn1_norm', 'ffn1_w_gate', 'ffn1_w_up', 'ffn1_w_down', 'mix_norm', 'mem_norm', 'w_in', 'ssd_conv_w', 'ssd_conv_b', 'ssd_dt_bias', 'ssd_a_log', 'ssd_d', 'ssd_norm', 'gmlp_v_norm', 'gmlp_w_s', 'gmlp_b_s', 'w_mem_kv', 'w_branch_ssd', 'w_branch_gmlp', 'w_branch_mem', 'w_out', 'ffn2_norm', 'ffn2_w_gate', 'ffn2_w_up', 'ffn2_w_down', 'final_norm', 'loss_target', 'm_ffn1_norm', 'm_ffn1_w_gate', 'm_ffn1_w_up', 'm_ffn1_w_down', 'm_mix_norm', 'm_mem_norm', 'm_w_in', 'm_ssd_conv_w', 'm_ssd_conv_b', 'm_ssd_dt_bias', 'm_ssd_a_log', 'm_ssd_d', 'm_ssd_norm', 'm_gmlp_v_norm', 'm_gmlp_w_s', 'm_gmlp_b_s', 'm_w_mem_kv', 'm_w_branch_ssd', 'm_w_branch_gmlp', 'm_w_branch_mem', 'm_w_out', 'm_ffn2_norm', 'm_ffn2_w_gate', 'm_ffn2_w_up', 'm_ffn2_w_down', 'm_final_norm', 'v_ffn1_norm', 'v_ffn1_w_gate', 'v_ffn1_w_up', 'v_ffn1_w_down', 'v_mix_norm', 'v_mem_norm', 'v_w_in', 'v_ssd_conv_w', 'v_ssd_conv_b', 'v_ssd_dt_bias', 'v_ssd_a_log', 'v_ssd_d', 'v_ssd_norm', 'v_gmlp_v_norm', 'v_gmlp_w_s', 'v_gmlp_b_s', 'v_w_mem_kv', 'v_w_branch_ssd', 'v_w_branch_gmlp', 'v_w_branch_mem', 'v_w_out', 'v_ffn2_norm', 'v_ffn2_w_gate', 'v_ffn2_w_up', 'v_ffn2_w_down', 'v_final_norm']
TWIN_OUTPUTS = ['loss', 'grad_x', 'grad_ffn1_norm', 'grad_ffn1_w_gate', 'grad_ffn1_w_up', 'grad_ffn1_w_down', 'grad_mix_norm', 'grad_mem_norm', 'grad_w_in', 'grad_ssd_conv_w', 'grad_ssd_conv_b', 'grad_ssd_dt_bias', 'grad_ssd_a_log', 'grad_ssd_d', 'grad_ssd_norm', 'grad_gmlp_v_norm', 'grad_gmlp_w_s', 'grad_gmlp_b_s', 'grad_w_mem_kv', 'grad_w_branch_ssd', 'grad_w_branch_gmlp', 'grad_w_branch_mem', 'grad_w_out', 'grad_ffn2_norm', 'grad_ffn2_w_gate', 'grad_ffn2_w_up', 'grad_ffn2_w_down', 'grad_final_norm', 'delta_ffn1_norm', 'delta_ffn1_w_gate', 'delta_ffn1_w_up', 'delta_ffn1_w_down', 'delta_mix_norm', 'delta_mem_norm', 'delta_w_in', 'delta_ssd_conv_w', 'delta_ssd_conv_b', 'delta_ssd_dt_bias', 'delta_ssd_a_log', 'delta_ssd_d', 'delta_ssd_norm', 'delta_gmlp_v_norm', 'delta_gmlp_w_s', 'delta_gmlp_b_s', 'delta_w_mem_kv', 'delta_w_branch_ssd', 'delta_w_branch_gmlp', 'delta_w_branch_mem', 'delta_w_out', 'delta_ffn2_norm', 'delta_ffn2_w_gate', 'delta_ffn2_w_up', 'delta_ffn2_w_down', 'delta_final_norm', 'new_m_ffn1_norm', 'new_m_ffn1_w_gate', 'new_m_ffn1_w_up', 'new_m_ffn1_w_down', 'new_m_mix_norm', 'new_m_mem_norm', 'new_m_w_in', 'new_m_ssd_conv_w', 'new_m_ssd_conv_b', 'new_m_ssd_dt_bias', 'new_m_ssd_a_log', 'new_m_ssd_d', 'new_m_ssd_norm', 'new_m_gmlp_v_norm', 'new_m_gmlp_w_s', 'new_m_gmlp_b_s', 'new_m_w_mem_kv', 'new_m_w_branch_ssd', 'new_m_w_branch_gmlp', 'new_m_w_branch_mem', 'new_m_w_out', 'new_m_ffn2_norm', 'new_m_ffn2_w_gate', 'new_m_ffn2_w_up', 'new_m_ffn2_w_down', 'new_m_final_norm', 'new_v_ffn1_norm', 'new_v_ffn1_w_gate', 'new_v_ffn1_w_up', 'new_v_ffn1_w_down', 'new_v_mix_norm', 'new_v_mem_norm', 'new_v_w_in', 'new_v_ssd_conv_w', 'new_v_ssd_conv_b', 'new_v_ssd_dt_bias', 'new_v_ssd_a_log', 'new_v_ssd_d', 'new_v_ssd_norm', 'new_v_gmlp_v_norm', 'new_v_gmlp_w_s', 'new_v_gmlp_b_s', 'new_v_w_mem_kv', 'new_v_w_branch_ssd', 'new_v_w_branch_gmlp', 'new_v_w_branch_mem', 'new_v_w_out', 'new_v_ffn2_norm', 'new_v_ffn2_w_gate', 'new_v_ffn2_w_up', 'new_v_ffn2_w_down', 'new_v_final_norm']
TWIN_LEAF_KINDS = {'loss': 'loss', 'grad_x': 'grad_x', 'grad_ffn1_norm': 'grad_w', 'grad_ffn1_w_gate': 'grad_w', 'grad_ffn1_w_up': 'grad_w', 'grad_ffn1_w_down': 'grad_w', 'grad_mix_norm': 'grad_w', 'grad_mem_norm': 'grad_w', 'grad_w_in': 'grad_w', 'grad_ssd_conv_w': 'grad_w', 'grad_ssd_conv_b': 'grad_w', 'grad_ssd_dt_bias': 'grad_w', 'grad_ssd_a_log': 'grad_w', 'grad_ssd_d': 'grad_w', 'grad_ssd_norm': 'grad_w', 'grad_gmlp_v_norm': 'grad_w', 'grad_gmlp_w_s': 'grad_w', 'grad_gmlp_b_s': 'grad_w', 'grad_w_mem_kv': 'grad_w', 'grad_w_branch_ssd': 'grad_w', 'grad_w_branch_gmlp': 'grad_w', 'grad_w_branch_mem': 'grad_w', 'grad_w_out': 'grad_w', 'grad_ffn2_norm': 'grad_w', 'grad_ffn2_w_gate': 'grad_w', 'grad_ffn2_w_up': 'grad_w', 'grad_ffn2_w_down': 'grad_w', 'grad_final_norm': 'grad_w', 'delta_ffn1_norm': 'delta_w', 'delta_ffn1_w_gate': 'delta_w', 'delta_ffn1_w_up': 'delta_w', 'delta_ffn1_w_down': 'delta_w', 'delta_mix_norm': 'delta_w', 'delta_mem_norm': 'delta_w', 'delta_w_in': 'delta_w', 'delta_ssd_conv_w': 'delta_w', 'delta_ssd_conv_b': 'delta_w', 'delta_ssd_dt_bias': 'delta_w', 'delta_ssd_a_log': 'delta_w', 'delta_ssd_d': 'delta_w', 'delta_ssd_norm': 'delta_w', 'delta_gmlp_v_norm': 'delta_w', 'delta_gmlp_w_s': 'delta_w', 'delta_gmlp_b_s': 'delta_w', 'delta_w_mem_kv': 'delta_w', 'delta_w_branch_ssd': 'delta_w', 'delta_w_branch_gmlp': 'delta_w', 'delta_w_branch_mem': 'delta_w', 'delta_w_out': 'delta_w', 'delta_ffn2_norm': 'delta_w', 'delta_ffn2_w_gate': 'delta_w', 'delta_ffn2_w_up': 'delta_w', 'delta_ffn2_w_down': 'delta_w', 'delta_final_norm': 'delta_w', 'new_m_ffn1_norm': 'new_m', 'new_m_ffn1_w_gate': 'new_m', 'new_m_ffn1_w_up': 'new_m', 'new_m_ffn1_w_down': 'new_m', 'new_m_mix_norm': 'new_m', 'new_m_mem_norm': 'new_m', 'new_m_w_in': 'new_m', 'new_m_ssd_conv_w': 'new_m', 'new_m_ssd_conv_b': 'new_m', 'new_m_ssd_dt_bias': 'new_m', 'new_m_ssd_a_log': 'new_m', 'new_m_ssd_d': 'new_m', 'new_m_ssd_norm': 'new_m', 'new_m_gmlp_v_norm': 'new_m', 'new_m_gmlp_w_s': 'new_m', 'new_m_gmlp_b_s': 'new_m', 'new_m_w_mem_kv': 'new_m', 'new_m_w_branch_ssd': 'new_m', 'new_m_w_branch_gmlp': 'new_m', 'new_m_w_branch_mem': 'new_m', 'new_m_w_out': 'new_m', 'new_m_ffn2_norm': 'new_m', 'new_m_ffn2_w_gate': 'new_m', 'new_m_ffn2_w_up': 'new_m', 'new_m_ffn2_w_down': 'new_m', 'new_m_final_norm': 'new_m', 'new_v_ffn1_norm': 'new_v', 'new_v_ffn1_w_gate': 'new_v', 'new_v_ffn1_w_up': 'new_v', 'new_v_ffn1_w_down': 'new_v', 'new_v_mix_norm': 'new_v', 'new_v_mem_norm': 'new_v', 'new_v_w_in': 'new_v', 'new_v_ssd_conv_w': 'new_v', 'new_v_ssd_conv_b': 'new_v', 'new_v_ssd_dt_bias': 'new_v', 'new_v_ssd_a_log': 'new_v', 'new_v_ssd_d': 'new_v', 'new_v_ssd_norm': 'new_v', 'new_v_gmlp_v_norm': 'new_v', 'new_v_gmlp_w_s': 'new_v', 'new_v_gmlp_b_s': 'new_v', 'new_v_w_mem_kv': 'new_v', 'new_v_w_branch_ssd': 'new_v', 'new_v_w_branch_gmlp': 'new_v', 'new_v_w_branch_mem': 'new_v', 'new_v_w_out': 'new_v', 'new_v_ffn2_norm': 'new_v', 'new_v_ffn2_w_gate': 'new_v', 'new_v_ffn2_w_up': 'new_v', 'new_v_ffn2_w_down': 'new_v', 'new_v_final_norm': 'new_v'}


def _forward(args):
    return _fwd_reference(*[args[k] for k in FWD_PARAMS])


def _output_shape():
    def fwd():
        inp = _fwd_setup_inputs(0)
        return _fwd_reference(*[inp[k] for k in FWD_PARAMS])
    out = _jax.eval_shape(fwd)
    return out.shape, out.dtype

N_MICROBATCH = 1
ADAM_LR = 0.001
ADAM_B1 = 0.9
ADAM_B2 = 0.999
ADAM_EPS = 1e-08
ADAM_WD = 0.01
ADAM_STEP = 10
PER_EXAMPLE_BATCH_AXIS = {'x': 0, 'mem': 0, 'loss_target': 0}
SHARED_INPUTS = []
_WEIGHT_DTYPES = {'ffn1_norm': _jnp.float32, 'ffn1_w_gate': _jnp.float32, 'ffn1_w_up': _jnp.float32, 'ffn1_w_down': _jnp.float32, 'mix_norm': _jnp.float32, 'mem_norm': _jnp.float32, 'w_in': _jnp.float32, 'ssd_conv_w': _jnp.float32, 'ssd_conv_b': _jnp.float32, 'ssd_dt_bias': _jnp.float32, 'ssd_a_log': _jnp.float32, 'ssd_d': _jnp.float32, 'ssd_norm': _jnp.float32, 'gmlp_v_norm': _jnp.float32, 'gmlp_w_s': _jnp.float32, 'gmlp_b_s': _jnp.float32, 'w_mem_kv': _jnp.float32, 'w_branch_ssd': _jnp.float32, 'w_branch_gmlp': _jnp.float32, 'w_branch_mem': _jnp.float32, 'w_out': _jnp.float32, 'ffn2_norm': _jnp.float32, 'ffn2_w_gate': _jnp.float32, 'ffn2_w_up': _jnp.float32, 'ffn2_w_down': _jnp.float32, 'final_norm': _jnp.float32}
MOMENT_SCALE = {'ffn1_norm': 1.240623e-01, 'ffn1_w_gate': 5.294033e-02, 'ffn1_w_up': 5.123772e-02, 'ffn1_w_down': 8.486709e-02, 'mix_norm': 2.110418e-01, 'mem_norm': 1.976049e-02, 'w_in': 6.447966e-02, 'ssd_conv_w': 6.853861e-02, 'ssd_conv_b': 1.110271e-01, 'ssd_dt_bias': 2.956324e-01, 'ssd_a_log': 1.748371e-01, 'ssd_d': 5.478028e-01, 'ssd_norm': 8.393237e-02, 'gmlp_v_norm': 5.157590e-02, 'gmlp_w_s': 5.093670e-02, 'gmlp_b_s': 7.070488e-02, 'w_mem_kv': 2.387603e-02, 'w_branch_ssd': 1.126286e-01, 'w_branch_gmlp': 9.057683e-02, 'w_branch_mem': 1.185468e-02, 'w_out': 1.465042e-01, 'ffn2_norm': 8.603155e-02, 'ffn2_w_gate': 3.606592e-02, 'ffn2_w_up': 3.518848e-02, 'ffn2_w_down': 5.827748e-02, 'final_norm': 6.400031e+01}


def _to_microbatches(a, axis):
    t = _jnp.moveaxis(a, axis, 0)
    t = t.reshape((N_MICROBATCH, t.shape[0] // N_MICROBATCH) + t.shape[1:])
    return _jnp.moveaxis(t, 1, axis + 1)


def setup_inputs(seed: int = 0) -> dict:
    inp = _fwd_setup_inputs(seed)
    key = _jax.random.fold_in(_jax.random.key(seed), 7919)
    shape, _ = _output_shape()
    out = dict(inp)
    out["loss_target"] = _jax.random.normal(_jax.random.fold_in(key, 0), shape, _jnp.float32)
    for i, name in enumerate(TWIN_WEIGHTS):
        w = inp[name].astype(_jnp.float32)
        if MOMENT_SCALE is None:
            s = _jnp.sqrt(_jnp.mean(_jnp.square(w)) + 1e-30)
        else:
            s = MOMENT_SCALE[name]
        km, kv = _jax.random.split(_jax.random.fold_in(key, i + 1))
        out[name] = w
        out["m_" + name] = s * _jax.random.normal(km, w.shape, _jnp.float32)
        out["v_" + name] = (s * s) * _jax.random.uniform(kv, w.shape, _jnp.float32, 0.5, 1.5)
    if N_MICROBATCH > 1:
        for name, axis in PER_EXAMPLE_BATCH_AXIS.items():
            out[name] = _to_microbatches(out[name], axis)
    return {'x': out['x'], 'mem': out['mem'], 'ffn1_norm': out['ffn1_norm'], 'ffn1_w_gate': out['ffn1_w_gate'], 'ffn1_w_up': out['ffn1_w_up'], 'ffn1_w_down': out['ffn1_w_down'], 'mix_norm': out['mix_norm'], 'mem_norm': out['mem_norm'], 'w_in': out['w_in'], 'ssd_conv_w': out['ssd_conv_w'], 'ssd_conv_b': out['ssd_conv_b'], 'ssd_dt_bias': out['ssd_dt_bias'], 'ssd_a_log': out['ssd_a_log'], 'ssd_d': out['ssd_d'], 'ssd_norm': out['ssd_norm'], 'gmlp_v_norm': out['gmlp_v_norm'], 'gmlp_w_s': out['gmlp_w_s'], 'gmlp_b_s': out['gmlp_b_s'], 'w_mem_kv': out['w_mem_kv'], 'w_branch_ssd': out['w_branch_ssd'], 'w_branch_gmlp': out['w_branch_gmlp'], 'w_branch_mem': out['w_branch_mem'], 'w_out': out['w_out'], 'ffn2_norm': out['ffn2_norm'], 'ffn2_w_gate': out['ffn2_w_gate'], 'ffn2_w_up': out['ffn2_w_up'], 'ffn2_w_down': out['ffn2_w_down'], 'final_norm': out['final_norm'], 'loss_target': out['loss_target'], 'm_ffn1_norm': out['m_ffn1_norm'], 'm_ffn1_w_gate': out['m_ffn1_w_gate'], 'm_ffn1_w_up': out['m_ffn1_w_up'], 'm_ffn1_w_down': out['m_ffn1_w_down'], 'm_mix_norm': out['m_mix_norm'], 'm_mem_norm': out['m_mem_norm'], 'm_w_in': out['m_w_in'], 'm_ssd_conv_w': out['m_ssd_conv_w'], 'm_ssd_conv_b': out['m_ssd_conv_b'], 'm_ssd_dt_bias': out['m_ssd_dt_bias'], 'm_ssd_a_log': out['m_ssd_a_log'], 'm_ssd_d': out['m_ssd_d'], 'm_ssd_norm': out['m_ssd_norm'], 'm_gmlp_v_norm': out['m_gmlp_v_norm'], 'm_gmlp_w_s': out['m_gmlp_w_s'], 'm_gmlp_b_s': out['m_gmlp_b_s'], 'm_w_mem_kv': out['m_w_mem_kv'], 'm_w_branch_ssd': out['m_w_branch_ssd'], 'm_w_branch_gmlp': out['m_w_branch_gmlp'], 'm_w_branch_mem': out['m_w_branch_mem'], 'm_w_out': out['m_w_out'], 'm_ffn2_norm': out['m_ffn2_norm'], 'm_ffn2_w_gate': out['m_ffn2_w_gate'], 'm_ffn2_w_up': out['m_ffn2_w_up'], 'm_ffn2_w_down': out['m_ffn2_w_down'], 'm_final_norm': out['m_final_norm'], 'v_ffn1_norm': out['v_ffn1_norm'], 'v_ffn1_w_gate': out['v_ffn1_w_gate'], 'v_ffn1_w_up': out['v_ffn1_w_up'], 'v_ffn1_w_down': out['v_ffn1_w_down'], 'v_mix_norm': out['v_mix_norm'], 'v_mem_norm': out['v_mem_norm'], 'v_w_in': out['v_w_in'], 'v_ssd_conv_w': out['v_ssd_conv_w'], 'v_ssd_conv_b': out['v_ssd_conv_b'], 'v_ssd_dt_bias': out['v_ssd_dt_bias'], 'v_ssd_a_log': out['v_ssd_a_log'], 'v_ssd_d': out['v_ssd_d'], 'v_ssd_norm': out['v_ssd_norm'], 'v_gmlp_v_norm': out['v_gmlp_v_norm'], 'v_gmlp_w_s': out['v_gmlp_w_s'], 'v_gmlp_b_s': out['v_gmlp_b_s'], 'v_w_mem_kv': out['v_w_mem_kv'], 'v_w_branch_ssd': out['v_w_branch_ssd'], 'v_w_branch_gmlp': out['v_w_branch_gmlp'], 'v_w_branch_mem': out['v_w_branch_mem'], 'v_w_out': out['v_w_out'], 'v_ffn2_norm': out['v_ffn2_norm'], 'v_ffn2_w_gate': out['v_ffn2_w_gate'], 'v_ffn2_w_up': out['v_ffn2_w_up'], 'v_ffn2_w_down': out['v_ffn2_w_down'], 'v_final_norm': out['v_final_norm']}


def _loss(weights, diff, rest, loss_target):
    with _jax.named_scope("forward"):
        args = {**rest, TWIN_DIFF_INPUT: diff, **{k: w.astype(_WEIGHT_DTYPES[k]) for k, w in weights.items()}}
        y = _forward(args)
    with _jax.named_scope("loss_head"):
        err = _jnp.square(y.astype(_jnp.float32) - loss_target)
        return 0.5 * _jnp.sum(_jnp.mean(err, axis=-1)) if err.ndim else 0.5 * err


def _adamw(w, g, m, v):
    m = ADAM_B1 * m + (1.0 - ADAM_B1) * g
    v = ADAM_B2 * v + (1.0 - ADAM_B2) * _jnp.square(g)
    m_hat = m / (1.0 - ADAM_B1 ** ADAM_STEP)
    v_hat = v / (1.0 - ADAM_B2 ** ADAM_STEP)
    delta = -ADAM_LR * (m_hat / (_jnp.sqrt(v_hat) + ADAM_EPS) + ADAM_WD * w)
    return delta, m, v


def reference(x, mem, ffn1_norm, ffn1_w_gate, ffn1_w_up, ffn1_w_down, mix_norm, mem_norm, w_in, ssd_conv_w, ssd_conv_b, ssd_dt_bias, ssd_a_log, ssd_d, ssd_norm, gmlp_v_norm, gmlp_w_s, gmlp_b_s, w_mem_kv, w_branch_ssd, w_branch_gmlp, w_branch_mem, w_out, ffn2_norm, ffn2_w_gate, ffn2_w_up, ffn2_w_down, final_norm, loss_target, m_ffn1_norm, m_ffn1_w_gate, m_ffn1_w_up, m_ffn1_w_down, m_mix_norm, m_mem_norm, m_w_in, m_ssd_conv_w, m_ssd_conv_b, m_ssd_dt_bias, m_ssd_a_log, m_ssd_d, m_ssd_norm, m_gmlp_v_norm, m_gmlp_w_s, m_gmlp_b_s, m_w_mem_kv, m_w_branch_ssd, m_w_branch_gmlp, m_w_branch_mem, m_w_out, m_ffn2_norm, m_ffn2_w_gate, m_ffn2_w_up, m_ffn2_w_down, m_final_norm, v_ffn1_norm, v_ffn1_w_gate, v_ffn1_w_up, v_ffn1_w_down, v_mix_norm, v_mem_norm, v_w_in, v_ssd_conv_w, v_ssd_conv_b, v_ssd_dt_bias, v_ssd_a_log, v_ssd_d, v_ssd_norm, v_gmlp_v_norm, v_gmlp_w_s, v_gmlp_b_s, v_w_mem_kv, v_w_branch_ssd, v_w_branch_gmlp, v_w_branch_mem, v_w_out, v_ffn2_norm, v_ffn2_w_gate, v_ffn2_w_up, v_ffn2_w_down, v_final_norm):
    given = dict(x=x, mem=mem, ffn1_norm=ffn1_norm, ffn1_w_gate=ffn1_w_gate, ffn1_w_up=ffn1_w_up, ffn1_w_down=ffn1_w_down, mix_norm=mix_norm, mem_norm=mem_norm, w_in=w_in, ssd_conv_w=ssd_conv_w, ssd_conv_b=ssd_conv_b, ssd_dt_bias=ssd_dt_bias, ssd_a_log=ssd_a_log, ssd_d=ssd_d, ssd_norm=ssd_norm, gmlp_v_norm=gmlp_v_norm, gmlp_w_s=gmlp_w_s, gmlp_b_s=gmlp_b_s, w_mem_kv=w_mem_kv, w_branch_ssd=w_branch_ssd, w_branch_gmlp=w_branch_gmlp, w_branch_mem=w_branch_mem, w_out=w_out, ffn2_norm=ffn2_norm, ffn2_w_gate=ffn2_w_gate, ffn2_w_up=ffn2_w_up, ffn2_w_down=ffn2_w_down, final_norm=final_norm, loss_target=loss_target, m_ffn1_norm=m_ffn1_norm, m_ffn1_w_gate=m_ffn1_w_gate, m_ffn1_w_up=m_ffn1_w_up, m_ffn1_w_down=m_ffn1_w_down, m_mix_norm=m_mix_norm, m_mem_norm=m_mem_norm, m_w_in=m_w_in, m_ssd_conv_w=m_ssd_conv_w, m_ssd_conv_b=m_ssd_conv_b, m_ssd_dt_bias=m_ssd_dt_bias, m_ssd_a_log=m_ssd_a_log, m_ssd_d=m_ssd_d, m_ssd_norm=m_ssd_norm, m_gmlp_v_norm=m_gmlp_v_norm, m_gmlp_w_s=m_gmlp_w_s, m_gmlp_b_s=m_gmlp_b_s, m_w_mem_kv=m_w_mem_kv, m_w_branch_ssd=m_w_branch_ssd, m_w_branch_gmlp=m_w_branch_gmlp, m_w_branch_mem=m_w_branch_mem, m_w_out=m_w_out, m_ffn2_norm=m_ffn2_norm, m_ffn2_w_gate=m_ffn2_w_gate, m_ffn2_w_up=m_ffn2_w_up, m_ffn2_w_down=m_ffn2_w_down, m_final_norm=m_final_norm, v_ffn1_norm=v_ffn1_norm, v_ffn1_w_gate=v_ffn1_w_gate, v_ffn1_w_up=v_ffn1_w_up, v_ffn1_w_down=v_ffn1_w_down, v_mix_norm=v_mix_norm, v_mem_norm=v_mem_norm, v_w_in=v_w_in, v_ssd_conv_w=v_ssd_conv_w, v_ssd_conv_b=v_ssd_conv_b, v_ssd_dt_bias=v_ssd_dt_bias, v_ssd_a_log=v_ssd_a_log, v_ssd_d=v_ssd_d, v_ssd_norm=v_ssd_norm, v_gmlp_v_norm=v_gmlp_v_norm, v_gmlp_w_s=v_gmlp_w_s, v_gmlp_b_s=v_gmlp_b_s, v_w_mem_kv=v_w_mem_kv, v_w_branch_ssd=v_w_branch_ssd, v_w_branch_gmlp=v_w_branch_gmlp, v_w_branch_mem=v_w_branch_mem, v_w_out=v_w_out, v_ffn2_norm=v_ffn2_norm, v_ffn2_w_gate=v_ffn2_w_gate, v_ffn2_w_up=v_ffn2_w_up, v_ffn2_w_down=v_ffn2_w_down, v_final_norm=v_final_norm)
    weights = {n: given[n] for n in TWIN_WEIGHTS}
    shared = {n: given[n] for n in SHARED_INPUTS}
    per_example = {n: given[n] for n in ['x', 'mem']}
    grad_fn = _jax.value_and_grad(_loss, argnums=(0, 1))

    def one_microbatch(ex, loss_target):
        ex = dict(ex)
        diff = ex.pop(TWIN_DIFF_INPUT)
        return grad_fn(weights, diff, {**shared, **ex}, loss_target)

    if N_MICROBATCH == 1:
        loss, (grad_w, grad_x) = one_microbatch(per_example, given["loss_target"])
    else:
        def body(carry, xs):
            loss_sum, grad_sum = carry
            l_k, (gw_k, gx_k) = one_microbatch(xs[0], xs[1])
            with _jax.named_scope("update"):
                return (loss_sum + l_k, _jax.tree.map(_jnp.add, grad_sum, gw_k)), gx_k

        init = (_jnp.zeros((), _jnp.float32), _jax.tree.map(_jnp.zeros_like, weights))
        (loss, grad_w), grad_x = _jax.lax.scan(body, init, (per_example, given["loss_target"]))
    with _jax.named_scope("update"):
        delta_w, new_m, new_v = {}, {}, {}
        for n in TWIN_WEIGHTS:
            delta_w[n], new_m[n], new_v[n] = _adamw(weights[n], grad_w[n], given["m_" + n], given["v_" + n])
    return (loss, grad_x, *[grad_w[n] for n in TWIN_WEIGHTS], *[delta_w[n] for n in TWIN_WEIGHTS],
            *[new_m[n] for n in TWIN_WEIGHTS], *[new_v[n] for n in TWIN_WEIGHTS])
```

```python
import functools
import math

import jax
import jax.numpy as jnp
from jax import lax
from jax.experimental import pallas as pl
from jax.experimental.pallas import tpu as pltpu

F32, BF16 = jnp.float32, jnp.bfloat16
HI = lax.Precision.HIGHEST
MESH = pl.DeviceIdType.MESH

D_MODEL = 1024
D_FF = 2816
MEM_LEN = 256
SSD_INNER = 2048
SSD_HEADS = 32
SSD_GROUPS = 4
SSD_STATE = 128
SSD_CONV = 4
CHUNK = 128
XBC = SSD_INNER + 2 * SSD_GROUPS * SSD_STATE
GMLP_W = 1024
GMLP_GROUPS = 8
MEM_W = 256
MEM_HEADS = 4
EPS = 1e-6
IN_WIDTH = 10528
IN_Z, IN_XBC, IN_DT, IN_UV, IN_Q, IN_GL = 0, 2048, 5120, 5152, 7200, 7456
P_Z, P_UV, P_XBC, P_GL, P_Q, P_W = 0, 2048, 4096, 7168, 10240, 10752
P_USED = 10496

ADAM_LR, ADAM_B1, ADAM_B2, ADAM_EPS, ADAM_WD, ADAM_STEP = 0.001, 0.9, 0.999, 1e-08, 0.01, 10

V7X_VMEM_LIMIT = 56 * 1024 * 1024
N_SHARD = 4
LANES = 1024


def _cparams(*sem):
    return pltpu.CompilerParams(dimension_semantics=sem, vmem_limit_bytes=V7X_VMEM_LIMIT)


def _sigmoid(x):
    return 1.0 / (1.0 + jnp.exp(-x))


def _row_tile(t):
    return min(512, t)


_DIMS = {"nn": (((1,), (0,)), ((), ())), "nt": (((1,), (1,)), ((), ())), "tn": (((0,), (0,)), ((), ()))}


def _matmul(a, b, *, mode, out_dtype, tm, tn, tk, name, scale=1.0, addend=None):
    if mode == "tn":
        k_dim, m_dim = a.shape
    else:
        m_dim, k_dim = a.shape
    n_dim = b.shape[0] if mode == "nt" else b.shape[1]
    tm, tn, tk = min(tm, m_dim), min(tn, n_dim), min(tk, k_dim)
    assert m_dim % tm == 0 and n_dim % tn == 0 and k_dim % tk == 0, (name, a.shape, b.shape, tm, tn, tk)
    ni, nj, nk = m_dim // tm, n_dim // tn, k_dim // tk
    a_spec = pl.BlockSpec((tk, tm), lambda j, i, k: (k, i)) if mode == "tn" else pl.BlockSpec((tm, tk), lambda j, i, k: (i, k))
    b_spec = pl.BlockSpec((tn, tk), lambda j, i, k: (j, k)) if mode == "nt" else pl.BlockSpec((tk, tn), lambda j, i, k: (k, j))
    o_spec = pl.BlockSpec((tm, tn), lambda j, i, k: (i, j))
    dims = _DIMS[mode]
    has_add = addend is not None

    def body(*refs):
        if has_add:
            a_ref, b_ref, r_ref, o_ref, acc_ref = refs
        else:
            a_ref, b_ref, o_ref, acc_ref = refs
        k = pl.program_id(2)

        @pl.when(k == 0)
        def _():
            acc_ref[...] = jnp.zeros_like(acc_ref)

        acc_ref[...] += lax.dot_general(a_ref[...].astype(BF16), b_ref[...].astype(BF16), dims, preferred_element_type=F32)

        @pl.when(k == nk - 1)
        def _():
            r = acc_ref[...] * scale
            if has_add:
                r = r + r_ref[...].astype(F32)
            o_ref[...] = r.astype(o_ref.dtype)

    in_specs = [a_spec, b_spec] + ([o_spec] if has_add else [])
    args = (a, b) + ((addend,) if has_add else ())
    return pl.pallas_call(
        body, name=name, grid=(nj, ni, nk), in_specs=in_specs, out_specs=o_spec,
        out_shape=jax.ShapeDtypeStruct((m_dim, n_dim), out_dtype),
        scratch_shapes=[pltpu.VMEM((tm, tn), F32)],
        compiler_params=_cparams("parallel", "parallel", "arbitrary"),
    )(*args)


def _rms_fwd(x, gain, name):
    t, d = x.shape
    tm = _row_tile(t)

    def body(x_ref, g_ref, o_ref):
        xv = x_ref[...]
        r = lax.rsqrt(jnp.mean(xv * xv, axis=-1, keepdims=True) + EPS)
        o_ref[...] = (xv * r * g_ref[...]).astype(o_ref.dtype)

    return pl.pallas_call(
        body, name=name, grid=(t // tm,),
        in_specs=[pl.BlockSpec((tm, d), lambda i: (i, 0)), pl.BlockSpec((1, d), lambda i: (0, 0))],
        out_specs=pl.BlockSpec((tm, d), lambda i: (i, 0)),
        out_shape=jax.ShapeDtypeStruct((t, d), BF16), compiler_params=_cparams("parallel"),
    )(x, gain)


def _rms_bwd(x, gain, dn, dres, name):
    t, d = x.shape
    tm = _row_tile(t)
    has_res = dres is not None

    def body(*refs):
        if has_res:
            x_ref, g_ref, dn_ref, r_ref, dx_ref, dg_ref = refs
        else:
            x_ref, g_ref, dn_ref, dx_ref, dg_ref = refs

        @pl.when(pl.program_id(0) == 0)
        def _():
            dg_ref[...] = jnp.zeros_like(dg_ref)

        xv = x_ref[...]
        r = lax.rsqrt(jnp.mean(xv * xv, axis=-1, keepdims=True) + EPS)
        xh = xv * r
        dnv = dn_ref[...].astype(F32)
        dg_ref[...] += jnp.sum(dnv * xh, axis=0, keepdims=True)
        dxh = dnv * g_ref[...]
        dx = r * (dxh - xh * jnp.mean(dxh * xh, axis=-1, keepdims=True))
        if has_res:
            dx = dx + r_ref[...]
        dx_ref[...] = dx

    row = pl.BlockSpec((tm, d), lambda i: (i, 0))
    vec = pl.BlockSpec((1, d), lambda i: (0, 0))
    in_specs = [row, vec, row] + ([row] if has_res else [])
    args = (x, gain, dn) + ((dres,) if has_res else ())
    return pl.pallas_call(
        body, name=name, grid=(t // tm,), in_specs=in_specs, out_specs=[row, vec],
        out_shape=[jax.ShapeDtypeStruct((t, d), F32), jax.ShapeDtypeStruct((1, d), F32)],
        compiler_params=_cparams("arbitrary"),
    )(*args)


def _loss_head(h, gain, target, name):
    t, d = h.shape
    tm = _row_tile(t)

    def body(h_ref, g_ref, t_ref, dh_ref, dg_ref, l_ref):
        @pl.when(pl.program_id(0) == 0)
        def _():
            dg_ref[...] = jnp.zeros_like(dg_ref)
            l_ref[...] = jnp.zeros_like(l_ref)

        xv = h_ref[...]
        g = g_ref[...]
        r = lax.rsqrt(jnp.mean(xv * xv, axis=-1, keepdims=True) + EPS)
        xh = xv * r
        err = xh * g - t_ref[...]
        l_ref[...] += 0.5 * jnp.sum(jnp.mean(err * err, axis=-1, keepdims=True), axis=0, keepdims=True)
        dy = err * (1.0 / d)
        dg_ref[...] += jnp.sum(dy * xh, axis=0, keepdims=True)
        dxh = dy * g
        dh_ref[...] = r * (dxh - xh * jnp.mean(dxh * xh, axis=-1, keepdims=True))

    row = pl.BlockSpec((tm, d), lambda i: (i, 0))
    vec = pl.BlockSpec((1, d), lambda i: (0, 0))
    return pl.pallas_call(
        body, name=name, grid=(t // tm,), in_specs=[row, vec, row],
        out_specs=[row, vec, pl.BlockSpec((1, 128), lambda i: (0, 0))],
        out_shape=[jax.ShapeDtypeStruct((t, d), F32), jax.ShapeDtypeStruct((1, d), F32), jax.ShapeDtypeStruct((1, 128), F32)],
        compiler_params=_cparams("arbitrary"),
    )(h, gain, target)


FF_TILE = 1408


def _ffn_fwd(n, x, wg, wu, wd, name):
    t, d = x.shape
    tm, tn = _row_tile(t), FF_TILE
    nj = D_FF // tn

    def body(n_ref, x_ref, wg_ref, wu_ref, wd_ref, h_ref, g_ref, u_ref, acc_ref):
        j = pl.program_id(1)

        @pl.when(j == 0)
        def _():
            acc_ref[...] = jnp.zeros_like(acc_ref)

        nb = n_ref[...]
        g = jnp.dot(nb, wg_ref[...], preferred_element_type=F32)
        u = jnp.dot(nb, wu_ref[...], preferred_element_type=F32)
        g_ref[...] = g.astype(BF16)
        u_ref[...] = u.astype(BF16)
        a = g * _sigmoid(g) * u
        acc_ref[...] += jnp.dot(a.astype(BF16), wd_ref[...], preferred_element_type=F32)

        @pl.when(j == nj - 1)
        def _():
            h_ref[...] = x_ref[...] + 0.5 * acc_ref[...]

    row = pl.BlockSpec((tm, d), lambda i, j: (i, 0))
    act = pl.BlockSpec((tm, tn), lambda i, j: (i, j))
    return pl.pallas_call(
        body, name=name, grid=(t // tm, nj),
        in_specs=[row, row, pl.BlockSpec((d, tn), lambda i, j: (0, j)), pl.BlockSpec((d, tn), lambda i, j: (0, j)),
                  pl.BlockSpec((tn, d), lambda i, j: (j, 0))],
        out_specs=[row, act, act],
        out_shape=[jax.ShapeDtypeStruct((t, d), F32), jax.ShapeDtypeStruct((t, D_FF), BF16), jax.ShapeDtypeStruct((t, D_FF), BF16)],
        scratch_shapes=[pltpu.VMEM((tm, d), F32)], compiler_params=_cparams("parallel", "arbitrary"),
    )(n, x, wg, wu, wd)


def _ffn_bwd_act(dh, g, u, wg, wu, wd, name):
    t, d = dh.shape
    tm, tn = _row_tile(t), FF_TILE
    nj = D_FF // tn

    def body(dh_ref, g_ref, u_ref, wg_ref, wu_ref, wd_ref, dn_ref, dg_ref, du_ref, a_ref, acc_ref):
        j = pl.program_id(1)

        @pl.when(j == 0)
        def _():
            acc_ref[...] = jnp.zeros_like(acc_ref)

        dhb = (0.5 * dh_ref[...]).astype(BF16)
        da = lax.dot_general(dhb, wd_ref[...], _DIMS["nt"], preferred_element_type=F32)
        gv = g_ref[...].astype(F32)
        uv = u_ref[...].astype(F32)
        sg = _sigmoid(gv)
        s = gv * sg
        dg = (da * uv * (sg * (1.0 + gv * (1.0 - sg)))).astype(BF16)
        du = (da * s).astype(BF16)
        dg_ref[...] = dg
        du_ref[...] = du
        a_ref[...] = (s * uv).astype(BF16)
        acc_ref[...] += (lax.dot_general(dg, wg_ref[...], _DIMS["nt"], preferred_element_type=F32)
                         + lax.dot_general(du, wu_ref[...], _DIMS["nt"], preferred_element_type=F32))

        @pl.when(j == nj - 1)
        def _():
            dn_ref[...] = acc_ref[...]

    row = pl.BlockSpec((tm, d), lambda i, j: (i, 0))
    act = pl.BlockSpec((tm, tn), lambda i, j: (i, j))
    return pl.pallas_call(
        body, name=name, grid=(t // tm, nj),
        in_specs=[row, act, act, pl.BlockSpec((d, tn), lambda i, j: (0, j)), pl.BlockSpec((d, tn), lambda i, j: (0, j)),
                  pl.BlockSpec((tn, d), lambda i, j: (j, 0))],
        out_specs=[row, act, act, act],
        out_shape=[jax.ShapeDtypeStruct((t, d), F32)] + [jax.ShapeDtypeStruct((t, D_FF), BF16)] * 3,
        scratch_shapes=[pltpu.VMEM((tm, d), F32)], compiler_params=_cparams("parallel", "arbitrary"),
    )(dh, g, u, wg, wu, wd)


def _ffn_forward(x, gain, wg, wu, wd, tag):
    n = _rms_fwd(x, gain, f"{tag}_norm")
    h, g, u = _ffn_fwd(n, x, wg, wu, wd, f"{tag}_fwd")
    return h, (n, g, u)


def _ffn_backward(dh, x, gain, wg, wu, wd, saved, tag):
    n, g, u = saved
    dn, dg, du, a = _ffn_bwd_act(dh, g, u, wg, wu, wd, f"{tag}_bwd_act")
    kw = dict(mode="tn", out_dtype=F32, tm=1024, tn=FF_TILE, tk=1024)
    d_wg = _matmul(n, dg, name=f"{tag}_dwg", **kw)
    d_wu = _matmul(n, du, name=f"{tag}_dwu", **kw)
    d_wd = _matmul(a, dh, mode="tn", out_dtype=F32, tm=FF_TILE, tn=1024, tk=1024, scale=0.5, name=f"{tag}_dwd")
    dx, dgain = _rms_bwd(x, gain, dn, dh, f"{tag}_norm_bwd")
    return dx, dict(norm=dgain, w_gate=d_wg, w_up=d_wu, w_down=d_wd)


CONV_COLS = 512
HALO = 8


def _conv_fwd(p, w, b, name):
    t = p.shape[0]
    tm = _row_tile(t)
    c0 = P_XBC // CONV_COLS

    def body(x_ref, halo_ref, w_ref, b_ref, o_ref, s_ref):
        i = pl.program_id(1)
        s_ref[0:HALO, :] = jnp.where(i > 0, halo_ref[...].astype(F32), 0.0)
        s_ref[HALO:HALO + tm, :] = x_ref[...].astype(F32)
        wv = w_ref[...]
        acc = b_ref[...] + wv[0:1, :] * s_ref[HALO - 3:HALO - 3 + tm, :]
        for k in range(1, SSD_CONV):
            acc = acc + wv[k:k + 1, :] * s_ref[HALO - 3 + k:HALO - 3 + k + tm, :]
        o_ref[...] = (acc * _sigmoid(acc)).astype(o_ref.dtype)

    return pl.pallas_call(
        body, name=name, grid=(XBC // CONV_COLS, t // tm),
        in_specs=[pl.BlockSpec((tm, CONV_COLS), lambda j, i: (i, c0 + j)),
                  pl.BlockSpec((HALO, CONV_COLS), lambda j, i: (jnp.maximum(i * (tm // HALO) - 1, 0), c0 + j)),
                  pl.BlockSpec((SSD_CONV, CONV_COLS), lambda j, i: (0, j)), pl.BlockSpec((1, CONV_COLS), lambda j, i: (0, j))],
        out_specs=pl.BlockSpec((tm, CONV_COLS), lambda j, i: (i, j)),
        out_shape=jax.ShapeDtypeStruct((t, XBC), BF16),
        scratch_shapes=[pltpu.VMEM((tm + HALO, CONV_COLS), F32)], compiler_params=_cparams("parallel", "parallel"),
    )(p, p, w, b)


def _conv_bwd_act(p, dy, w, b, col0, name):
    t, cols = dy.shape
    tm = _row_tile(t)
    c0 = (P_XBC + col0) // CONV_COLS
    w0 = col0 // CONV_COLS

    def body(x_ref, halo_ref, dy_ref, w_ref, b_ref, da_ref, dw_ref, db_ref, s_ref):
        i = pl.program_id(1)

        @pl.when(i == 0)
        def _():
            dw_ref[...] = jnp.zeros_like(dw_ref)
            db_ref[...] = jnp.zeros_like(db_ref)

        s_ref[0:HALO, :] = jnp.where(i > 0, halo_ref[...].astype(F32), 0.0)
        s_ref[HALO:HALO + tm, :] = x_ref[...].astype(F32)
        wv = w_ref[...]
        acc = b_ref[...] + wv[0:1, :] * s_ref[HALO - 3:HALO - 3 + tm, :]
        for k in range(1, SSD_CONV):
            acc = acc + wv[k:k + 1, :] * s_ref[HALO - 3 + k:HALO - 3 + k + tm, :]
        sg = _sigmoid(acc)
        dacc = dy_ref[...].astype(F32) * (sg * (1.0 + acc * (1.0 - sg)))
        da_ref[...] = dacc.astype(BF16)
        db_ref[...] += jnp.sum(dacc, axis=0, keepdims=True)
        for k in range(SSD_CONV):
            dw_ref[k:k + 1, :] += jnp.sum(dacc * s_ref[HALO - 3 + k:HALO - 3 + k + tm, :], axis=0, keepdims=True)

    return pl.pallas_call(
        body, name=name, grid=(cols // CONV_COLS, t // tm),
        in_specs=[pl.BlockSpec((tm, CONV_COLS), lambda j, i: (i, c0 + j)),
                  pl.BlockSpec((HALO, CONV_COLS), lambda j, i: (jnp.maximum(i * (tm // HALO) - 1, 0), c0 + j)),
                  pl.BlockSpec((tm, CONV_COLS), lambda j, i: (i, j)),
                  pl.BlockSpec((SSD_CONV, CONV_COLS), lambda j, i: (0, w0 + j)), pl.BlockSpec((1, CONV_COLS), lambda j, i: (0, w0 + j))],
        out_specs=[pl.BlockSpec((tm, CONV_COLS), lambda j, i: (i, j)), pl.BlockSpec((SSD_CONV, CONV_COLS), lambda j, i: (0, j)),
                   pl.BlockSpec((1, CONV_COLS), lambda j, i: (0, j))],
        out_shape=[jax.ShapeDtypeStruct((t, cols), BF16), jax.ShapeDtypeStruct((SSD_CONV, cols), F32), jax.ShapeDtypeStruct((1, cols), F32)],
        scratch_shapes=[pltpu.VMEM((tm + HALO, CONV_COLS), F32)], compiler_params=_cparams("parallel", "arbitrary"),
    )(p, p, dy, w, b)


def _conv_bwd_dx(dacc, w, col0, name):
    t, cols = dacc.shape
    tm = _row_tile(t)
    nt = t // tm
    w0 = col0 // CONV_COLS

    def body(d_ref, halo_ref, w_ref, o_ref, s_ref):
        i = pl.program_id(1)
        s_ref[0:tm, :] = d_ref[...].astype(F32)
        s_ref[tm:tm + HALO, :] = jnp.where(i < nt - 1, halo_ref[...].astype(F32), 0.0)
        wv = w_ref[...]
        acc = wv[3:4, :] * s_ref[0:tm, :]
        for k in range(SSD_CONV - 1):
            acc = acc + wv[k:k + 1, :] * s_ref[3 - k:3 - k + tm, :]
        o_ref[...] = acc.astype(o_ref.dtype)

    return pl.pallas_call(
        body, name=name, grid=(cols // CONV_COLS, nt),
        in_specs=[pl.BlockSpec((tm, CONV_COLS), lambda j, i: (i, j)),
                  pl.BlockSpec((HALO, CONV_COLS), lambda j, i: (jnp.minimum((i + 1) * (tm // HALO), t // HALO - 1), j)),
                  pl.BlockSpec((SSD_CONV, CONV_COLS), lambda j, i: (0, w0 + j))],
        out_specs=pl.BlockSpec((tm, CONV_COLS), lambda j, i: (i, j)),
        out_shape=jax.ShapeDtypeStruct((t, cols), BF16),
        scratch_shapes=[pltpu.VMEM((tm + HALO, CONV_COLS), F32)], compiler_params=_cparams("parallel", "parallel"),
    )(dacc, dacc, w)


GROUP_COLS = SSD_INNER // SSD_GROUPS
PAIRS = GROUP_COLS // 128
HEADS_PER_GROUP = SSD_HEADS // SSD_GROUPS


def _softplus(x):
    return jnp.maximum(x, 0.0) + jnp.log1p(jnp.exp(-jnp.abs(x)))


def _ssd_common(dtw_ref, dtt_ref, bw_ref, bt_ref, aw_ref, at_ref):
    l = CHUNK
    dt = _softplus(dtw_ref[0] + bw_ref[0])
    dtt = _softplus(dtt_ref[0] + bt_ref[0])
    a = -jnp.exp(aw_ref[0])
    at = -jnp.exp(at_ref[0])
    rowi = lax.broadcasted_iota(jnp.int32, (l, l), 0)
    coli = lax.broadcasted_iota(jnp.int32, (l, l), 1)
    tri = rowi >= coli
    lower = tri.astype(F32)
    upper = (rowi <= coli).astype(F32)
    acs = jnp.dot(lower, dt * a, precision=HI, preferred_element_type=F32)
    acst = jnp.dot(dtt * at, upper, precision=HI, preferred_element_type=F32)
    return dt, a, acs, acst, tri, upper


def _pair_bc(w, lo, p):
    return jnp.where(lo, w[:, 2 * p:2 * p + 1], w[:, 2 * p + 1:2 * p + 2])


def _ssd_specs(t):
    nc = t // CHUNK
    return nc, dict(
        xs=lambda cm: pl.BlockSpec((CHUNK, GROUP_COLS), lambda g, c: (cm(c), g)),
        bm=lambda cm: pl.BlockSpec((CHUNK, SSD_STATE), lambda g, c: (cm(c), SSD_INNER // SSD_STATE + g)),
        cmat=lambda cm: pl.BlockSpec((CHUNK, SSD_STATE), lambda g, c: (cm(c), SSD_INNER // SSD_STATE + SSD_GROUPS + g)),
        dtw=lambda cm: pl.BlockSpec((1, CHUNK, 128), lambda g, c: (g, cm(c), 0)),
        dtt=lambda cm: pl.BlockSpec((1, HEADS_PER_GROUP, CHUNK), lambda g, c: (g, 0, cm(c))),
        wide=lambda cm: pl.BlockSpec((1, 1, 128), lambda g, c: (g, 0, 0)),
        tall=lambda cm: pl.BlockSpec((1, HEADS_PER_GROUP, 1), lambda g, c: (g, 0, 0)),
        grp=lambda cm: pl.BlockSpec((CHUNK, GROUP_COLS), lambda g, c: (cm(c), g)),
        vec=lambda cm: pl.BlockSpec((1, GROUP_COLS), lambda g, c: (0, g)),
        state=lambda cm: pl.BlockSpec((1, 1, PAIRS, SSD_STATE, 128), lambda g, c: (g, cm(c), 0, 0, 0)),
    )


def _ssd_fwd(xc, p, hv, norm_g, name):
    t = xc.shape[0]
    nc, sp = _ssd_specs(t)
    ident = lambda c: c

    def body(xs_ref, b_ref, c_ref, dtw_ref, dtt_ref, bw_ref, bt_ref, aw_ref, at_ref, dk_ref, z_ref, ng_ref,
             y_ref, ys_ref, h_ref, st_ref):
        @pl.when(pl.program_id(1) == 0)
        def _():
            st_ref[...] = jnp.zeros_like(st_ref)

        dt, a, acs, acst, tri, _ = _ssd_common(dtw_ref, dtt_ref, bw_ref, bt_ref, aw_ref, at_ref)
        ecs = jnp.exp(acs)
        alast = acs[CHUNK - 1:CHUNK, :]
        bmat, cmat = b_ref[...], c_ref[...]
        cb = lax.dot_general(cmat, bmat, _DIMS["nt"], preferred_element_type=F32)
        lo = lax.broadcasted_iota(jnp.int32, (1, 128), 1) < 64
        dskip = dk_ref[0]
        for pi in range(PAIRS):
            x = xs_ref[:, pi * 128:(pi + 1) * 128].astype(F32)
            xdt = x * _pair_bc(dt, lo, pi)
            ydiag = jnp.zeros((CHUNK, 128), F32)
            for r, mask in ((2 * pi, lo), (2 * pi + 1, jnp.logical_not(lo))):
                lam = jnp.exp(jnp.where(tri, acs[:, r:r + 1] - acst[r:r + 1, :], -1e30))
                m = (cb * lam).astype(BF16)
                ydiag = ydiag + jnp.dot(m, jnp.where(mask, xdt, 0.0).astype(BF16), preferred_element_type=F32)
            ht = st_ref[pi]
            h_ref[0, 0, pi] = ht
            yoff = jnp.dot(cmat, ht.astype(BF16), preferred_element_type=F32) * _pair_bc(ecs, lo, pi)
            y_ref[:, pi * 128:(pi + 1) * 128] = (ydiag + yoff + _pair_bc(dskip, lo, pi) * x).astype(y_ref.dtype)
            alp = _pair_bc(alast, lo, pi)
            e = jnp.exp(alp - _pair_bc(acs, lo, pi))
            st = lax.dot_general(bmat, (xdt * e).astype(BF16), _DIMS["tn"], preferred_element_type=F32)
            st_ref[pi] = ht * jnp.exp(alp) + st
        zf = z_ref[...].astype(F32)
        yg = y_ref[...].astype(F32) * (zf * _sigmoid(zf))
        rstd = lax.rsqrt(jnp.mean(yg * yg, axis=-1, keepdims=True) + EPS)
        ys_ref[...] = (yg * rstd * ng_ref[...]).astype(ys_ref.dtype)

    ins = ["xs", "bm", "cmat", "dtw", "dtt", "wide", "tall", "wide", "tall", "wide", "grp", "vec"]
    return pl.pallas_call(
        body, name=name, grid=(SSD_GROUPS, nc),
        in_specs=[sp[k](ident) for k in ins],
        out_specs=[sp["grp"](ident), sp["grp"](ident), sp["state"](ident)],
        out_shape=[jax.ShapeDtypeStruct((t, SSD_INNER), BF16), jax.ShapeDtypeStruct((t, SSD_INNER), BF16),
                   jax.ShapeDtypeStruct((SSD_GROUPS, nc, PAIRS, SSD_STATE, 128), F32)],
        scratch_shapes=[pltpu.VMEM((PAIRS, SSD_STATE, 128), F32)], compiler_params=_cparams("parallel", "arbitrary"),
    )(xc, xc, xc, hv["dtw"], hv["dtt"], hv["bias_w"], hv["bias_t"], hv["alog_w"], hv["alog_t"], hv["dskip_w"], p, norm_g)


def _ssd_bwd(xc, p, hv, norm_g, y, dys, states, name):
    t = xc.shape[0]
    nc, sp = _ssd_specs(t)
    rev = lambda c: nc - 1 - c

    def body(xs_ref, b_ref, c_ref, dtw_ref, dtt_ref, bw_ref, bt_ref, aw_ref, at_ref, dk_ref, z_ref, ng_ref,
             y_ref, dys_ref, h_ref,
             dxs_ref, db_ref, dc_ref, dz_ref, ddt_ref, hsum_ref, dng_ref, dst_ref):
        @pl.when(pl.program_id(1) == 0)
        def _():
            dst_ref[...] = jnp.zeros_like(dst_ref)
            hsum_ref[...] = jnp.zeros_like(hsum_ref)
            dng_ref[...] = jnp.zeros_like(dng_ref)

        dt, a, acs, acst, tri, upper = _ssd_common(dtw_ref, dtt_ref, bw_ref, bt_ref, aw_ref, at_ref)
        ecs = jnp.exp(acs)
        alast = acs[CHUNK - 1:CHUNK, :]
        bmat, cmat = b_ref[...], c_ref[...]
        cb = lax.dot_general(cmat, bmat, _DIMS["nt"], preferred_element_type=F32)
        lane = lax.broadcasted_iota(jnp.int32, (1, 128), 1)
        lo = lane < 64
        dskip = dk_ref[0]

        zf = z_ref[...].astype(F32)
        sg = _sigmoid(zf)
        sz = zf * sg
        yv = y_ref[...].astype(F32)
        yg = yv * sz
        rstd = lax.rsqrt(jnp.mean(yg * yg, axis=-1, keepdims=True) + EPS)
        yhat = yg * rstd
        dysv = dys_ref[...].astype(F32)
        dng_ref[...] += jnp.sum(dysv * yhat, axis=0, keepdims=True)
        dyh = dysv * ng_ref[...]
        dyg = rstd * (dyh - yhat * jnp.mean(dyh * yhat, axis=-1, keepdims=True))
        dz_ref[...] = (dyg * yv * (sg * (1.0 + zf * (1.0 - sg)))).astype(dz_ref.dtype)
        dy_all = dyg * sz

        sel_r = lax.broadcasted_iota(jnp.int32, (128, 128), 0)
        sel_c = lax.broadcasted_iota(jnp.int32, (128, 128), 1)
        dal = jnp.zeros((CHUNK, 128), F32)
        ddtm = jnp.zeros((CHUNK, 128), F32)
        dalast = jnp.zeros((8, 128), F32)
        ddsk = jnp.zeros((8, 128), F32)
        dcb = jnp.zeros((CHUNK, CHUNK), F32)
        qcol = jnp.zeros((8, CHUNK), F32)
        sub8 = lax.broadcasted_iota(jnp.int32, (8, CHUNK), 0)
        dc_acc = jnp.zeros((CHUNK, SSD_STATE), F32)
        db_acc = jnp.zeros((CHUNK, SSD_STATE), F32)
        for pi in range(PAIRS):
            sel = (sel_c == 2 * pi + (sel_r >= 64).astype(jnp.int32)).astype(F32)

            def hsum(v, sel=sel):
                return jnp.dot(v, sel, precision=HI, preferred_element_type=F32)

            dyp = dy_all[:, pi * 128:(pi + 1) * 128]
            x = xs_ref[:, pi * 128:(pi + 1) * 128].astype(F32)
            dtp = _pair_bc(dt, lo, pi)
            xdt = x * dtp
            dxdt = jnp.zeros((CHUNK, 128), F32)
            for r, mask in ((2 * pi, lo), (2 * pi + 1, jnp.logical_not(lo))):
                lam = jnp.exp(jnp.where(tri, acs[:, r:r + 1] - acst[r:r + 1, :], -1e30))
                m32 = cb * lam
                m = m32.astype(BF16)
                dyr = jnp.where(mask, dyp, 0.0).astype(BF16)
                xr = jnp.where(mask, xdt, 0.0).astype(BF16)
                dm = lax.dot_general(dyr, xr, _DIMS["nt"], preferred_element_type=F32)
                dcb = dcb + dm * lam
                q = dm * m32
                dal = dal + jnp.sum(q, axis=1, keepdims=True) * (lane == r).astype(F32)
                qcol = qcol + jnp.where(sub8 == r, jnp.sum(q, axis=0, keepdims=True), 0.0)
                dxdt = dxdt + lax.dot_general(m, dyr, _DIMS["tn"], preferred_element_type=F32)
            ht = h_ref[0, 0, pi]
            htb = ht.astype(BF16)
            ecp = _pair_bc(ecs, lo, pi)
            yoff = jnp.dot(cmat, htb, preferred_element_type=F32) * ecp
            dg = (dyp * ecp).astype(BF16)
            dc_acc = dc_acc + lax.dot_general(dg, htb, _DIMS["nt"], preferred_element_type=F32)
            dht = lax.dot_general(cmat, dg, _DIMS["tn"], preferred_element_type=F32)
            dal = dal + hsum(dyp * yoff)
            dhn = dst_ref[pi]
            dhnb = dhn.astype(BF16)
            alp = _pair_bc(alast, lo, pi)
            e = jnp.exp(alp - _pair_bc(acs, lo, pi))
            xe = xdt * e
            db_acc = db_acc + lax.dot_general(xe.astype(BF16), dhnb, _DIMS["nt"], preferred_element_type=F32)
            dxe = jnp.dot(bmat, dhnb, preferred_element_type=F32)
            dxdt = dxdt + dxe * e
            tt = hsum(dxe * xe)
            dal = dal - tt
            dec = jnp.exp(alp)
            dalast = dalast + jnp.sum(tt, axis=0, keepdims=True) + hsum(
                jnp.broadcast_to(jnp.sum(dhn * ht, axis=0, keepdims=True) * dec, (8, 128)))
            dst_ref[pi] = dht + dhn * dec
            dxs_ref[:, pi * 128:(pi + 1) * 128] = (_pair_bc(dskip, lo, pi) * dyp + dxdt * dtp).astype(dxs_ref.dtype)
            ddtm = ddtm + hsum(dxdt * x)
            ddsk = ddsk + hsum(jnp.broadcast_to(jnp.sum(dyp * x, axis=0, keepdims=True), (8, 128)))
        rowi = lax.broadcasted_iota(jnp.int32, (CHUNK, 128), 0)
        qcol_w = lax.dot_general(jnp.concatenate([qcol, jnp.zeros((CHUNK - 8, CHUNK), F32)], axis=0), (sel_r == sel_c).astype(F32),
                                 _DIMS["tn"], precision=HI, preferred_element_type=F32)
        dal = dal - qcol_w + jnp.where(rowi == CHUNK - 1, dalast[0:1, :], 0.0)
        dda = jnp.dot(upper, dal, precision=HI, preferred_element_type=F32)
        ddt = ddtm + dda * a
        ddraw = ddt * _sigmoid(dtw_ref[0] + bw_ref[0])
        ddt_ref[0] = ddraw
        hsum_ref[0, 0:1, :] += jnp.sum(ddraw, axis=0, keepdims=True)
        hsum_ref[0, 1:2, :] += jnp.sum(dda * dt, axis=0, keepdims=True) * a
        hsum_ref[0, 2:3, :] += ddsk[0:1, :]
        dcbb = dcb.astype(BF16)
        dc_ref[...] = (jnp.dot(dcbb, bmat, preferred_element_type=F32) + dc_acc).astype(dc_ref.dtype)
        db_ref[...] = (lax.dot_general(dcbb, cmat, _DIMS["tn"], preferred_element_type=F32) + db_acc).astype(db_ref.dtype)

    ins = ["xs", "bm", "cmat", "dtw", "dtt", "wide", "tall", "wide", "tall", "wide", "grp", "vec", "grp", "grp", "state"]
    col = lambda: pl.BlockSpec((CHUNK, SSD_STATE), lambda g, c: (rev(c), g))
    return pl.pallas_call(
        body, name=name, grid=(SSD_GROUPS, nc),
        in_specs=[sp[k](rev) for k in ins],
        out_specs=[sp["grp"](rev), col(), col(), sp["grp"](rev), sp["dtw"](rev),
                   pl.BlockSpec((1, 8, 128), lambda g, c: (g, 0, 0)), sp["vec"](rev)],
        out_shape=[jax.ShapeDtypeStruct((t, SSD_INNER), BF16), jax.ShapeDtypeStruct((t, SSD_GROUPS * SSD_STATE), BF16),
                   jax.ShapeDtypeStruct((t, SSD_GROUPS * SSD_STATE), BF16), jax.ShapeDtypeStruct((t, SSD_INNER), BF16),
                   jax.ShapeDtypeStruct((SSD_GROUPS, t, 128), F32), jax.ShapeDtypeStruct((SSD_GROUPS, 8, 128), F32),
                   jax.ShapeDtypeStruct((1, SSD_INNER), F32)],
        scratch_shapes=[pltpu.VMEM((PAIRS, SSD_STATE, 128), F32)], compiler_params=_cparams("parallel", "arbitrary"),
    )(xc, xc, xc, hv["dtw"], hv["dtt"], hv["bias_w"], hv["bias_t"], hv["alog_w"], hv["alog_t"], hv["dskip_w"], p, norm_g,
      y, dys, states)


def _wide(v):
    return jnp.pad(v.reshape(SSD_GROUPS, 1, HEADS_PER_GROUP), ((0, 0), (0, 0), (0, 128 - HEADS_PER_GROUP)))


def _head_inputs(dt_raw, dt_bias, a_log, d_skip):
    t = dt_raw.shape[0]
    g = dt_raw[:, :SSD_HEADS].reshape(t, SSD_GROUPS, HEADS_PER_GROUP)
    return dict(
        dtw=jnp.pad(jnp.transpose(g, (1, 0, 2)), ((0, 0), (0, 0), (0, 128 - HEADS_PER_GROUP))),
        dtt=jnp.transpose(g, (1, 2, 0)),
        bias_w=_wide(dt_bias), bias_t=dt_bias.reshape(SSD_GROUPS, HEADS_PER_GROUP, 1),
        alog_w=_wide(a_log), alog_t=a_log.reshape(SSD_GROUPS, HEADS_PER_GROUP, 1),
        dskip_w=_wide(d_skip),
    )


def _gelu(x):
    return 0.5 * x * (1.0 + lax.erf(x * (1.0 / math.sqrt(2.0))))


def _gelu_grad(x):
    return 0.5 * (1.0 + lax.erf(x * (1.0 / math.sqrt(2.0)))) + x * jnp.exp(-0.5 * x * x) * (1.0 / math.sqrt(2.0 * math.pi))


def _tril_mask():
    r = lax.broadcasted_iota(jnp.int32, (CHUNK, CHUNK), 0)
    c = lax.broadcasted_iota(jnp.int32, (CHUNK, CHUNK), 1)
    return r >= c


def _gmlp_fwd(p, v_gain, w_s, b_col, name):
    t = p.shape[0]
    tm = _row_tile(t)
    u0 = P_UV // GMLP_W

    def body(u_ref, v_ref, gn_ref, ws_ref, bs_ref, o_ref):
        v = _gelu(v_ref[...].astype(F32))
        v = (v * lax.rsqrt(jnp.mean(v * v, axis=-1, keepdims=True) + EPS) * gn_ref[...]).astype(BF16)
        tril = _tril_mask()
        wm = [jnp.where(tril, ws_ref[g], 0.0).astype(BF16) for g in range(GMLP_GROUPS)]
        for k in range(tm // CHUNK):
            rows = slice(k * CHUNK, (k + 1) * CHUNK)
            for g in range(GMLP_GROUPS):
                cols = slice(g * 128, (g + 1) * 128)
                mixed = jnp.dot(wm[g], v[rows, cols], preferred_element_type=F32) + bs_ref[g]
                o_ref[rows, cols] = (_gelu(u_ref[rows, cols].astype(F32)) * mixed).astype(o_ref.dtype)

    return pl.pallas_call(
        body, name=name, grid=(t // tm,),
        in_specs=[pl.BlockSpec((tm, GMLP_W), lambda i: (i, u0)), pl.BlockSpec((tm, GMLP_W), lambda i: (i, u0 + 1)),
                  pl.BlockSpec((1, GMLP_W), lambda i: (0, 0)), pl.BlockSpec((GMLP_GROUPS, CHUNK, CHUNK), lambda i: (0, 0, 0)),
                  pl.BlockSpec((GMLP_GROUPS, CHUNK, 1), lambda i: (0, 0, 0))],
        out_specs=pl.BlockSpec((tm, GMLP_W), lambda i: (i, 0)),
        out_shape=jax.ShapeDtypeStruct((t, GMLP_W), BF16), compiler_params=_cparams("parallel"),
    )(p, p, v_gain, w_s, b_col)


def _gmlp_bwd(p, dy, v_gain, w_s, b_col, name):
    t = p.shape[0]
    tm = _row_tile(t)
    u0 = P_UV // GMLP_W

    def body(u_ref, v_ref, dy_ref, gn_ref, ws_ref, bs_ref, duv_ref, dws_ref, dbs_ref, dgn_ref, dvn_ref):
        @pl.when(pl.program_id(0) == 0)
        def _():
            dws_ref[...] = jnp.zeros_like(dws_ref)
            dbs_ref[...] = jnp.zeros_like(dbs_ref)
            dgn_ref[...] = jnp.zeros_like(dgn_ref)

        vraw = v_ref[...].astype(F32)
        va = _gelu(vraw)
        rstd = lax.rsqrt(jnp.mean(va * va, axis=-1, keepdims=True) + EPS)
        vhat = va * rstd
        gain = gn_ref[...]
        vn = (vhat * gain).astype(BF16)
        tril = _tril_mask()
        wm = [jnp.where(tril, ws_ref[g], 0.0).astype(BF16) for g in range(GMLP_GROUPS)]
        for k in range(tm // CHUNK):
            rows = slice(k * CHUNK, (k + 1) * CHUNK)
            for g in range(GMLP_GROUPS):
                cols = slice(g * 128, (g + 1) * 128)
                uraw = u_ref[rows, cols].astype(F32)
                vb = vn[rows, cols]
                mixed = jnp.dot(wm[g], vb, preferred_element_type=F32) + bs_ref[g]
                dyb = dy_ref[rows, cols].astype(F32)
                duv_ref[rows, cols] = (dyb * mixed * _gelu_grad(uraw)).astype(duv_ref.dtype)
                dmix = dyb * _gelu(uraw)
                dmb = dmix.astype(BF16)
                dws_ref[g] += jnp.where(tril, lax.dot_general(dmb, vb, _DIMS["nt"], preferred_element_type=F32), 0.0)
                dbs_ref[g] += jnp.sum(dmix, axis=1, keepdims=True)
                dvn_ref[rows, cols] = lax.dot_general(wm[g], dmb, _DIMS["tn"], preferred_element_type=F32)
        dvn = dvn_ref[...]
        dgn_ref[...] += jnp.sum(dvn * vhat, axis=0, keepdims=True)
        dvh = dvn * gain
        dva = rstd * (dvh - vhat * jnp.mean(dvh * vhat, axis=-1, keepdims=True))
        duv_ref[:, GMLP_W:2 * GMLP_W] = (dva * _gelu_grad(vraw)).astype(duv_ref.dtype)

    return pl.pallas_call(
        body, name=name, grid=(t // tm,),
        in_specs=[pl.BlockSpec((tm, GMLP_W), lambda i: (i, u0)), pl.BlockSpec((tm, GMLP_W), lambda i: (i, u0 + 1)),
                  pl.BlockSpec((tm, GMLP_W), lambda i: (i, 0)),
                  pl.BlockSpec((1, GMLP_W), lambda i: (0, 0)), pl.BlockSpec((GMLP_GROUPS, CHUNK, CHUNK), lambda i: (0, 0, 0)),
                  pl.BlockSpec((GMLP_GROUPS, CHUNK, 1), lambda i: (0, 0, 0))],
        out_specs=[pl.BlockSpec((tm, 2 * GMLP_W), lambda i: (i, 0)), pl.BlockSpec((GMLP_GROUPS, CHUNK, CHUNK), lambda i: (0, 0, 0)),
                   pl.BlockSpec((GMLP_GROUPS, CHUNK, 1), lambda i: (0, 0, 0)), pl.BlockSpec((1, GMLP_W), lambda i: (0, 0))],
        out_shape=[jax.ShapeDtypeStruct((t, 2 * GMLP_W), BF16), jax.ShapeDtypeStruct((GMLP_GROUPS, CHUNK, CHUNK), F32),
                   jax.ShapeDtypeStruct((GMLP_GROUPS, CHUNK, 1), F32), jax.ShapeDtypeStruct((1, GMLP_W), F32)],
        scratch_shapes=[pltpu.VMEM((tm, GMLP_W), F32)], compiler_params=_cparams("arbitrary"),
    )(p, p, dy, v_gain, w_s, b_col)


def _head_masks():
    lane = lax.broadcasted_iota(jnp.int32, (1, MEM_W), 1)
    return [(lane >= h * 64) & (lane < (h + 1) * 64) for h in range(MEM_HEADS)]


def _mem_fwd(p, kv, name):
    t = p.shape[0]
    tm = _row_tile(t)
    q0 = P_Q // MEM_W

    def body(q_ref, kv_ref, o_ref):
        q = q_ref[...]
        k = kv_ref[:, 0:MEM_W].astype(BF16)
        v = kv_ref[:, MEM_W:2 * MEM_W].astype(BF16)
        out = jnp.zeros((tm, MEM_W), F32)
        for mask in _head_masks():
            s = lax.dot_general(jnp.where(mask, q, 0), k, _DIMS["nt"], preferred_element_type=F32) * 0.125
            e = jnp.exp(s - jnp.max(s, axis=-1, keepdims=True))
            pr = (e / jnp.sum(e, axis=-1, keepdims=True)).astype(BF16)
            out = out + jnp.where(mask, jnp.dot(pr, v, preferred_element_type=F32), 0.0)
        o_ref[...] = out.astype(o_ref.dtype)

    return pl.pallas_call(
        body, name=name, grid=(t // tm,),
        in_specs=[pl.BlockSpec((tm, MEM_W), lambda i: (i, q0)), pl.BlockSpec((MEM_LEN, 2 * MEM_W), lambda i: (0, 0))],
        out_specs=pl.BlockSpec((tm, MEM_W), lambda i: (i, 0)),
        out_shape=jax.ShapeDtypeStruct((t, MEM_W), BF16), compiler_params=_cparams("parallel"),
    )(p, kv)


def _mem_bwd(p, kv, dy, name):
    t = p.shape[0]
    tm = _row_tile(t)
    q0 = P_Q // MEM_W

    def body(q_ref, kv_ref, dy_ref, dq_ref, dkv_ref):
        @pl.when(pl.program_id(0) == 0)
        def _():
            dkv_ref[...] = jnp.zeros_like(dkv_ref)

        q = q_ref[...]
        dy = dy_ref[...]
        k = kv_ref[:, 0:MEM_W].astype(BF16)
        v = kv_ref[:, MEM_W:2 * MEM_W].astype(BF16)
        dq = jnp.zeros((tm, MEM_W), F32)
        dk = jnp.zeros((MEM_LEN, MEM_W), F32)
        dv = jnp.zeros((MEM_LEN, MEM_W), F32)
        for mask in _head_masks():
            qh = jnp.where(mask, q, 0)
            dyh = jnp.where(mask, dy, 0)
            s = lax.dot_general(qh, k, _DIMS["nt"], preferred_element_type=F32) * 0.125
            e = jnp.exp(s - jnp.max(s, axis=-1, keepdims=True))
            pr = e / jnp.sum(e, axis=-1, keepdims=True)
            prb = pr.astype(BF16)
            dp = lax.dot_general(dyh, v, _DIMS["nt"], preferred_element_type=F32)
            ds = (pr * (dp - jnp.sum(dp * pr, axis=-1, keepdims=True)) * 0.125).astype(BF16)
            dq = dq + jnp.where(mask, jnp.dot(ds, k, preferred_element_type=F32), 0.0)
            dk = dk + lax.dot_general(ds, qh, _DIMS["tn"], preferred_element_type=F32)
            dv = dv + lax.dot_general(prb, dyh, _DIMS["tn"], preferred_element_type=F32)
        dq_ref[...] = dq.astype(dq_ref.dtype)
        dkv_ref[:, 0:MEM_W] += dk
        dkv_ref[:, MEM_W:2 * MEM_W] += dv

    return pl.pallas_call(
        body, name=name, grid=(t // tm,),
        in_specs=[pl.BlockSpec((tm, MEM_W), lambda i: (i, q0)), pl.BlockSpec((MEM_LEN, 2 * MEM_W), lambda i: (0, 0)),
                  pl.BlockSpec((tm, MEM_W), lambda i: (i, 0))],
        out_specs=[pl.BlockSpec((tm, MEM_W), lambda i: (i, 0)), pl.BlockSpec((MEM_LEN, 2 * MEM_W), lambda i: (0, 0))],
        out_shape=[jax.ShapeDtypeStruct((t, MEM_W), BF16), jax.ShapeDtypeStruct((MEM_LEN, 2 * MEM_W), F32)],
        compiler_params=_cparams("arbitrary"),
    )(p, kv, dy)


def _merge_fwd(p, b_ssd, b_gmlp, b_mem, name):
    t = p.shape[0]
    tm = _row_tile(t)
    g0 = P_GL // D_MODEL

    def body(g1, g2, g3, b1, b2, b3, o_ref):
        acc = _sigmoid(g1[...].astype(F32)) * b1[...].astype(F32)
        acc = acc + _sigmoid(g2[...].astype(F32)) * b2[...].astype(F32)
        acc = acc + _sigmoid(g3[...].astype(F32)) * b3[...].astype(F32)
        o_ref[...] = acc.astype(o_ref.dtype)

    row = pl.BlockSpec((tm, D_MODEL), lambda i: (i, 0))
    return pl.pallas_call(
        body, name=name, grid=(t // tm,),
        in_specs=[pl.BlockSpec((tm, D_MODEL), lambda i, k=k: (i, g0 + k)) for k in range(3)] + [row] * 3,
        out_specs=row, out_shape=jax.ShapeDtypeStruct((t, D_MODEL), BF16), compiler_params=_cparams("parallel"),
    )(p, p, p, b_ssd, b_gmlp, b_mem)


def _merge_bwd(p, dm, b_ssd, b_gmlp, b_mem, name):
    t = p.shape[0]
    tm = _row_tile(t)
    g0 = P_GL // D_MODEL

    def body(g1, g2, g3, dm_ref, b1, b2, b3, d1, d2, d3, dgl_ref):
        dmv = dm_ref[...].astype(F32)
        for k, (g_ref, b_ref, d_ref) in enumerate(((g1, b1, d1), (g2, b2, d2), (g3, b3, d3))):
            sg = _sigmoid(g_ref[...].astype(F32))
            d_ref[...] = (dmv * sg).astype(d_ref.dtype)
            dgl_ref[:, k * D_MODEL:(k + 1) * D_MODEL] = (dmv * b_ref[...].astype(F32) * sg * (1.0 - sg)).astype(dgl_ref.dtype)

    row = pl.BlockSpec((tm, D_MODEL), lambda i: (i, 0))
    return pl.pallas_call(
        body, name=name, grid=(t // tm,),
        in_specs=[pl.BlockSpec((tm, D_MODEL), lambda i, k=k: (i, g0 + k)) for k in range(3)] + [row] * 4,
        out_specs=[row, row, row, pl.BlockSpec((tm, 3 * D_MODEL), lambda i: (i, 0))],
        out_shape=[jax.ShapeDtypeStruct((t, D_MODEL), BF16)] * 3 + [jax.ShapeDtypeStruct((t, 3 * D_MODEL), BF16)],
        compiler_params=_cparams("parallel"),
    )(p, p, p, dm, b_ssd, b_gmlp, b_mem)


def _local_step(x, mem, target, w):
    t = x.shape[0]
    mm = functools.partial(_matmul, tk=1024)

    h1, ffn1_saved = _ffn_forward(x, w["ffn1_norm"], w["ffn1_w_gate"], w["ffn1_w_up"], w["ffn1_w_down"], "ffn1")
    n2 = _rms_fwd(h1, w["mix_norm"], "mix_norm")
    p = mm(n2, w["w_in_p"], mode="nn", out_dtype=BF16, tm=512, tn=1536, name="in_proj")
    dt_raw = mm(n2, w["w_dt"], mode="nn", out_dtype=F32, tm=512, tn=128, name="dt_proj")
    hv = _head_inputs(dt_raw, w["ssd_dt_bias"], w["ssd_a_log"], w["ssd_d"])
    xc = _conv_fwd(p, w["ssd_conv_w"], w["ssd_conv_b"], "conv_fwd")
    y_ssd_raw, y_ssd, states = _ssd_fwd(xc, p, hv, w["ssd_norm"], "ssd_fwd")
    b_col = w["gmlp_b_s"].reshape(GMLP_GROUPS, CHUNK, 1)
    y_gmlp = _gmlp_fwd(p, w["gmlp_v_norm"], w["gmlp_w_s"], b_col, "gmlp_fwd")
    mem_n = _rms_fwd(mem, w["mem_norm"], "mem_norm")
    kv = mm(mem_n, w["w_mem_kv"], mode="nn", out_dtype=F32, tm=256, tn=512, name="mem_kv")
    y_mem = _mem_fwd(p, kv, "mem_fwd")
    b_ssd = mm(y_ssd, w["w_branch_ssd"], mode="nn", out_dtype=BF16, tm=512, tn=1024, name="branch_ssd")
    b_gmlp = mm(y_gmlp, w["w_branch_gmlp"], mode="nn", out_dtype=BF16, tm=512, tn=1024, name="branch_gmlp")
    b_mem = mm(y_mem, w["w_branch_mem"], mode="nn", out_dtype=BF16, tm=512, tn=1024, name="branch_mem")
    merged = _merge_fwd(p, b_ssd, b_gmlp, b_mem, "merge_fwd")
    h2 = mm(merged, w["w_out"], mode="nn", out_dtype=F32, tm=512, tn=1024, addend=h1, name="out_proj")
    h3, ffn2_saved = _ffn_forward(h2, w["ffn2_norm"], w["ffn2_w_gate"], w["ffn2_w_up"], w["ffn2_w_down"], "ffn2")
    dh3, d_final, loss = _loss_head(h3, w["final_norm"], target, "loss_head")

    g = {"final_norm": d_final}
    dh2, gf = _ffn_backward(dh3, h2, w["ffn2_norm"], w["ffn2_w_gate"], w["ffn2_w_up"], w["ffn2_w_down"], ffn2_saved, "ffn2")
    g.update({f"ffn2_{k}": v for k, v in gf.items()})
    dmerged = mm(dh2, w["w_out"], mode="nt", out_dtype=BF16, tm=512, tn=1024, name="out_proj_dx")
    g["w_out"] = mm(merged, dh2, mode="tn", out_dtype=F32, tm=1024, tn=1024, name="out_proj_dw")
    db_ssd, db_gmlp, db_mem, dgl = _merge_bwd(p, dmerged, b_ssd, b_gmlp, b_mem, "merge_bwd")
    dy_ssd = mm(db_ssd, w["w_branch_ssd"], mode="nt", out_dtype=BF16, tm=512, tn=2048, name="branch_ssd_dx")
    dy_gmlp = mm(db_gmlp, w["w_branch_gmlp"], mode="nt", out_dtype=BF16, tm=512, tn=1024, name="branch_gmlp_dx")
    dy_mem = mm(db_mem, w["w_branch_mem"], mode="nt", out_dtype=BF16, tm=512, tn=256, name="branch_mem_dx")
    g["w_branch_ssd"] = mm(y_ssd, db_ssd, mode="tn", out_dtype=F32, tm=1024, tn=1024, name="branch_ssd_dw")
    g["w_branch_gmlp"] = mm(y_gmlp, db_gmlp, mode="tn", out_dtype=F32, tm=1024, tn=1024, name="branch_gmlp_dw")
    g["w_branch_mem"] = mm(y_mem, db_mem, mode="tn", out_dtype=F32, tm=256, tn=1024, name="branch_mem_dw")
    dq, dkv = _mem_bwd(p, kv, dy_mem, "mem_bwd")
    g["w_mem_kv"] = mm(mem_n, dkv, mode="tn", out_dtype=F32, tm=1024, tn=512, tk=256, name="mem_kv_dw")
    dmem_n = mm(dkv, w["w_mem_kv"], mode="nt", out_dtype=F32, tm=256, tn=1024, tk=512, name="mem_kv_dx")
    _, g["mem_norm"] = _rms_bwd(mem, w["mem_norm"], dmem_n, None, "mem_norm_bwd")
    duv, d_ws, d_bs, g["gmlp_v_norm"] = _gmlp_bwd(p, dy_gmlp, w["gmlp_v_norm"], w["gmlp_w_s"], b_col, "gmlp_bwd")
    g["gmlp_w_s"] = d_ws
    g["gmlp_b_s"] = d_bs.reshape(GMLP_GROUPS, CHUNK)
    dxs, d_bm, d_cm, dz, ddt_w, hsums, g["ssd_norm"] = _ssd_bwd(xc, p, hv, w["ssd_norm"], y_ssd_raw, dy_ssd, states, "ssd_bwd")
    heads = hsums[:, :, :HEADS_PER_GROUP]
    g["ssd_dt_bias"] = heads[:, 0, :].reshape(1, SSD_HEADS)
    g["ssd_a_log"] = heads[:, 1, :].reshape(1, SSD_HEADS)
    g["ssd_d"] = heads[:, 2, :].reshape(1, SSD_HEADS)
    ddt = jnp.transpose(ddt_w[:, :, :HEADS_PER_GROUP], (1, 0, 2)).reshape(t, SSD_HEADS)
    ddt = jnp.pad(ddt, ((0, 0), (0, 128 - SSD_HEADS))).astype(BF16)
    parts, dws, dbs = [], [], []
    for dyc, col0, tag in ((dxs, 0, "x"), (d_bm, SSD_INNER, "b"), (d_cm, SSD_INNER + SSD_GROUPS * SSD_STATE, "c")):
        dacc, dw_c, db_c = _conv_bwd_act(p, dyc, w["ssd_conv_w"], w["ssd_conv_b"], col0, f"conv_bwd_act_{tag}")
        parts.append(_conv_bwd_dx(dacc, w["ssd_conv_w"], col0, f"conv_bwd_dx_{tag}"))
        dws.append(dw_c)
        dbs.append(db_c)
    g["ssd_conv_w"] = jnp.concatenate(dws, axis=1)
    g["ssd_conv_b"] = jnp.concatenate(dbs, axis=1)
    dp = jnp.concatenate([dz, duv] + parts + [dgl, dq, jnp.zeros((t, P_W - P_USED), BF16)], axis=1)
    dn2 = mm(dp, w["w_in_p"], mode="nt", out_dtype=F32, tm=512, tn=1024, tk=1536, name="in_proj_dx")
    dn2 = _matmul(ddt, w["w_dt"], mode="nt", out_dtype=F32, tm=512, tn=1024, tk=128, addend=dn2, name="dt_proj_dx")
    d_win_p = mm(n2, dp, mode="tn", out_dtype=F32, tm=1024, tn=1536, name="in_proj_dw")
    d_wdt = mm(n2, ddt, mode="tn", out_dtype=F32, tm=1024, tn=128, name="dt_proj_dw")
    sl = lambda a, o, n: a[:, o:o + n]
    g["w_in"] = jnp.concatenate([sl(d_win_p, P_Z, 2048), sl(d_win_p, P_XBC, XBC), d_wdt[:, :SSD_HEADS], sl(d_win_p, P_UV, 2048),
                                 sl(d_win_p, P_Q, MEM_W), sl(d_win_p, P_GL, 3 * D_MODEL)], axis=1)
    dh1, g["mix_norm"] = _rms_bwd(h1, w["mix_norm"], dn2, dh2, "mix_norm_bwd")
    dx, gf = _ffn_backward(dh1, x, w["ffn1_norm"], w["ffn1_w_gate"], w["ffn1_w_up"], w["ffn1_w_down"], ffn1_saved, "ffn1")
    g.update({f"ffn1_{k}": v for k, v in gf.items()})
    return loss, dx, g


def _split_w_in(w_in):
    sl = lambda o, n: w_in[:, o:o + n]
    w_p = jnp.concatenate([sl(IN_Z, 2048), sl(IN_UV, 2048), sl(IN_XBC, XBC), sl(IN_GL, 3 * D_MODEL), sl(IN_Q, MEM_W),
                           jnp.zeros((D_MODEL, P_W - P_USED), w_in.dtype)], axis=1)
    w_dt = jnp.pad(sl(IN_DT, SSD_HEADS), ((0, 0), (0, 128 - SSD_HEADS)))
    return w_p, w_dt


def _pick_tile(rows, cap=512):
    best = None
    for tile in range(8, min(rows, cap) + 1, 8):
        if rows % tile == 0:
            best = tile
    return best if best is not None else rows


def _adamw(w, g, m, v, name):
    rows, lanes = w.shape
    tile = _pick_tile(rows)
    c1 = 1.0 / (1.0 - ADAM_B1 ** ADAM_STEP)
    c2 = 1.0 / (1.0 - ADAM_B2 ** ADAM_STEP)

    def body(w_ref, g_ref, m_ref, v_ref, d_ref, nm_ref, nv_ref):
        gv = g_ref[...]
        nm = ADAM_B1 * m_ref[...] + (1.0 - ADAM_B1) * gv
        nv = ADAM_B2 * v_ref[...] + (1.0 - ADAM_B2) * (gv * gv)
        nm_ref[...] = nm
        nv_ref[...] = nv
        d_ref[...] = -ADAM_LR * ((nm * c1) / (jnp.sqrt(nv * c2) + ADAM_EPS) + ADAM_WD * w_ref[...])

    blk = pl.BlockSpec((tile, lanes), lambda i: (i, 0))
    return pl.pallas_call(
        body, name=name, grid=(rows // tile,), in_specs=[blk] * 4, out_specs=[blk] * 3,
        out_shape=[jax.ShapeDtypeStruct((rows, lanes), F32)] * 3, compiler_params=_cparams("parallel"),
    )(w, g, m, v)


HBM = pl.BlockSpec(memory_space=pltpu.HBM)


def _place():
    x, y, c = lax.axis_index("x"), lax.axis_index("y"), lax.axis_index("c")
    chips = [(1 - x, y), (x, 1 - y), (1 - x, 1 - y)]
    return x, y, c, chips


def _gather_weights(packed):
    _, rh, lanes = packed.shape

    def body(in_ref, out_ref, send_sems, recv_sems, local_sem):
        x, y, c, chips = _place()
        me, sibling = (x, y, c), (x, y, 1 - c)

        def copy(k, src, dst, to):
            return pltpu.make_async_remote_copy(src_ref=src, dst_ref=dst, send_sem=send_sems.at[k], recv_sem=recv_sems.at[k],
                                                device_id=to, device_id_type=MESH)

        mine = pltpu.make_async_copy(in_ref, out_ref.at[2 * x + y], local_sem)
        mine.start()
        first = [copy(j, in_ref.at[c], out_ref.at[2 * x + y, c], (*chip, c)) for j, chip in enumerate(chips)]
        for cp in first:
            cp.start()
        passed = []
        for j, (cx, cy) in enumerate(chips):
            landed = out_ref.at[2 * cx + cy, c]
            copy(j, landed, landed, me).wait_recv()
            fwd = copy(3 + j, landed, landed, sibling)
            fwd.start()
            passed.append(fwd)
        for j, (cx, cy) in enumerate(chips):
            other = out_ref.at[2 * cx + cy, 1 - c]
            copy(3 + j, other, other, me).wait_recv()
        for cp in first + passed:
            cp.wait_send()
        mine.wait()

    return pl.pallas_call(
        body, name="gather_weights", out_shape=jax.ShapeDtypeStruct((N_SHARD, 2, rh, lanes), packed.dtype),
        in_specs=[HBM], out_specs=HBM,
        scratch_shapes=[pltpu.SemaphoreType.DMA((6,)), pltpu.SemaphoreType.DMA((6,)), pltpu.SemaphoreType.DMA],
    )(packed)


def _rs_swap(gp):
    _, ns, rh, lanes = gp.shape

    def body(in_ref, out_ref, send_sem, recv_sem):
        x, y, c, _ = _place()
        cp = pltpu.make_async_remote_copy(src_ref=in_ref.at[1 - c], dst_ref=out_ref, send_sem=send_sem, recv_sem=recv_sem,
                                          device_id=(x, y, 1 - c), device_id_type=MESH)
        cp.start()
        cp.wait_send()
        cp.wait_recv()

    return pl.pallas_call(
        body, name="rs_swap", out_shape=jax.ShapeDtypeStruct((ns, rh, lanes), gp.dtype), in_specs=[HBM], out_specs=HBM,
        scratch_shapes=[pltpu.SemaphoreType.DMA, pltpu.SemaphoreType.DMA],
    )(gp)


def _rs_tile(rh):
    return _pick_tile(rh, cap=512)


def _rs_add(gp, recv, c):
    _, ns, rh, lanes = gp.shape
    tile = _rs_tile(rh)

    def body(c_ref, a_ref, b_ref, o_ref):
        o_ref[...] = (a_ref[...].astype(F32) + b_ref[...].astype(F32)).astype(o_ref.dtype)

    return pl.pallas_call(
        body, name="rs_add", out_shape=jax.ShapeDtypeStruct((ns, rh, lanes), gp.dtype),
        grid_spec=pltpu.PrefetchScalarGridSpec(
            num_scalar_prefetch=1, grid=(ns, rh // tile),
            in_specs=[pl.BlockSpec((None, None, tile, lanes), lambda s, i, c_ref: (c_ref[0], s, i, 0)),
                      pl.BlockSpec((None, tile, lanes), lambda s, i, c_ref: (s, i, 0))],
            out_specs=pl.BlockSpec((None, tile, lanes), lambda s, i, c_ref: (s, i, 0))),
        compiler_params=_cparams("parallel", "parallel"),
    )(c, gp, recv)


def _rs_scatter(pa):
    ns, rh, lanes = pa.shape

    def body(in_ref, out_ref, send_sems, recv_sems):
        x, y, c, chips = _place()
        cps = [pltpu.make_async_remote_copy(src_ref=in_ref.at[2 * cx + cy], dst_ref=out_ref.at[j], send_sem=send_sems.at[j],
                                            recv_sem=recv_sems.at[j], device_id=(cx, cy, c), device_id_type=MESH)
               for j, (cx, cy) in enumerate(chips)]
        for cp in cps:
            cp.start()
        for cp in cps:
            cp.wait_recv()
        for cp in cps:
            cp.wait_send()

    return pl.pallas_call(
        body, name="rs_scatter", out_shape=jax.ShapeDtypeStruct((ns - 1, rh, lanes), pa.dtype), in_specs=[HBM], out_specs=HBM,
        scratch_shapes=[pltpu.SemaphoreType.DMA((3,)), pltpu.SemaphoreType.DMA((3,))],
    )(pa)


def _rs_sum(pa, recv, chip):
    ns, rh, lanes = pa.shape
    tile = _rs_tile(rh)

    def body(chip_ref, a_ref, r_ref, o_ref):
        acc = a_ref[...].astype(F32)
        for j in range(ns - 1):
            acc = acc + r_ref[j].astype(F32)
        o_ref[...] = acc

    return pl.pallas_call(
        body, name="rs_sum", out_shape=jax.ShapeDtypeStruct((rh, lanes), F32),
        grid_spec=pltpu.PrefetchScalarGridSpec(
            num_scalar_prefetch=1, grid=(rh // tile,),
            in_specs=[pl.BlockSpec((None, tile, lanes), lambda i, chip_ref: (chip_ref[0], i, 0)),
                      pl.BlockSpec((ns - 1, tile, lanes), lambda i, chip_ref: (0, i, 0))],
            out_specs=pl.BlockSpec((tile, lanes), lambda i, chip_ref: (i, 0))),
        compiler_params=_cparams("parallel"),
    )(chip, pa, recv)


def _rs_share(gs):
    rh, lanes = gs.shape

    def body(in_ref, out_ref, send_sem, recv_sem, local_sem):
        x, y, c, _ = _place()
        mine = pltpu.make_async_copy(in_ref, out_ref.at[c], local_sem)
        mine.start()
        cp = pltpu.make_async_remote_copy(src_ref=in_ref, dst_ref=out_ref.at[c], send_sem=send_sem, recv_sem=recv_sem,
                                          device_id=(x, y, 1 - c), device_id_type=MESH)
        cp.start()
        other = out_ref.at[1 - c]
        pltpu.make_async_remote_copy(src_ref=in_ref, dst_ref=other, send_sem=send_sem, recv_sem=recv_sem,
                                     device_id=(x, y, 1 - c), device_id_type=MESH).wait_recv()
        cp.wait_send()
        mine.wait()

    return pl.pallas_call(
        body, name="rs_share", out_shape=jax.ShapeDtypeStruct((2, rh, lanes), gs.dtype), in_specs=[HBM], out_specs=HBM,
        scratch_shapes=[pltpu.SemaphoreType.DMA, pltpu.SemaphoreType.DMA, pltpu.SemaphoreType.DMA],
    )(gs)


N_DEV = 8
SMALL_ROWS = 160


def _allreduce_small(v):
    m_per, n = v.shape

    def body(x_ref, out_ref, all_ref, send_sems, recv_sems, local_sem):
        x, y, c, chips = _place()
        me, sibling = (x, y, c), (x, y, 1 - c)

        def rows(px, py, pc):
            return all_ref.at[pl.ds((4 * px + 2 * py + pc) * m_per, m_per), :]

        def copy(k, block, to, src=None):
            return pltpu.make_async_remote_copy(src_ref=rows(*block) if src is None else src, dst_ref=rows(*block),
                                                send_sem=send_sems.at[k], recv_sem=recv_sems.at[k], device_id=to, device_id_type=MESH)

        mine = pltpu.make_async_copy(x_ref, rows(*me), local_sem)
        mine.start()
        first = [copy(0, me, sibling, src=x_ref)]
        first += [copy(1 + j, me, (*chip, c), src=x_ref) for j, chip in enumerate(chips)]
        for cp in first:
            cp.start()
        passed = [copy(4 + j, (*chip, c), sibling) for j, chip in enumerate(chips)]
        for j, chip in enumerate(chips):
            copy(1 + j, (*chip, c), me).wait_recv()
            passed[j].start()
        copy(0, sibling, me).wait_recv()
        for j, chip in enumerate(chips):
            copy(4 + j, (*chip, 1 - c), me).wait_recv()
        for cp in first + passed:
            cp.wait_send()
        mine.wait()
        step = 32
        for r in range(0, m_per, step):
            acc = all_ref[r:r + step, :]
            for d in range(1, N_DEV):
                acc = acc + all_ref[d * m_per + r:d * m_per + r + step, :]
            out_ref[r:r + step, :] = acc

    vm = pl.BlockSpec(memory_space=pltpu.VMEM)
    return pl.pallas_call(
        body, name="allreduce_small", out_shape=jax.ShapeDtypeStruct((m_per, n), v.dtype), in_specs=[vm], out_specs=vm,
        scratch_shapes=[pltpu.VMEM((N_DEV * m_per, n), v.dtype), pltpu.SemaphoreType.DMA((7,)), pltpu.SemaphoreType.DMA((7,)),
                        pltpu.SemaphoreType.DMA],
        compiler_params=pltpu.CompilerParams(vmem_limit_bytes=V7X_VMEM_LIMIT),
    )(v)


BIG = [("ffn1_w_gate", (D_MODEL, D_FF), 1), ("ffn1_w_up", (D_MODEL, D_FF), 1), ("ffn1_w_down", (D_FF, D_MODEL), 0),
       ("ffn2_w_gate", (D_MODEL, D_FF), 1), ("ffn2_w_up", (D_MODEL, D_FF), 1), ("ffn2_w_down", (D_FF, D_MODEL), 0),
       ("w_in", (D_MODEL, IN_WIDTH), 1), ("w_mem_kv", (D_MODEL, 2 * MEM_W), 0), ("w_branch_ssd", (SSD_INNER, D_MODEL), 0),
       ("w_branch_gmlp", (GMLP_W, D_MODEL), 0), ("w_branch_mem", (MEM_W, D_MODEL), 1), ("w_out", (D_MODEL, D_MODEL), 0)]
BIG_ROWS = [shape[0] * shape[1] // N_SHARD // LANES for _, shape, _ in BIG]
CONV_W_ROWS = 8
PACK_ROWS = 8096
assert sum(BIG_ROWS) + CONV_W_ROWS <= PACK_ROWS and (PACK_ROWS // 2) % 16 == 0

SMALL = [("ffn1_norm", 1), ("mix_norm", 1), ("mem_norm", 1), ("ssd_conv_b", 3), ("heads", 1), ("ssd_norm", 2),
         ("gmlp_v_norm", 1), ("gmlp_w_s", 128), ("gmlp_b_s", 1), ("ffn2_norm", 1), ("final_norm", 1), ("ssd_conv_w", 12)]
assert sum(n for _, n in SMALL) <= SMALL_ROWS
HEAD_VECS = ("ssd_dt_bias", "ssd_a_log", "ssd_d")


def _pack_small(vals, loss=None):
    parts = []
    for name, nrows in SMALL:
        if name == "heads":
            row = jnp.concatenate([vals[k].reshape(-1) for k in HEAD_VECS]
                                  + [jnp.zeros((1,), F32) if loss is None else loss.reshape(1)])
            parts.append(jnp.pad(row, (0, LANES - row.shape[0])).reshape(1, LANES))
        elif name in vals:
            parts.append(vals[name].reshape(nrows, LANES))
        else:
            parts.append(jnp.zeros((nrows, LANES), F32))
    buf = jnp.concatenate(parts, axis=0)
    return jnp.pad(buf, ((0, SMALL_ROWS - buf.shape[0]), (0, 0)))


def _unpack_small(buf):
    out, r = {}, 0
    for name, nrows in SMALL:
        blk = buf[r:r + nrows]
        r += nrows
        if name == "heads":
            for i, k in enumerate(HEAD_VECS):
                out[k] = blk[0, i * SSD_HEADS:(i + 1) * SSD_HEADS]
            out["loss"] = blk[0, 3 * SSD_HEADS]
        else:
            out[name] = blk
    return out


def _shard_rows(full, axis):
    a, b = full.shape
    if axis == 1:
        full = jnp.transpose(full.reshape(a, N_SHARD, b // N_SHARD), (1, 0, 2))
    return full.reshape(N_SHARD, a * b // N_SHARD // LANES, LANES)


def _unshard_rows(rows, shape, axis):
    a, b = shape
    if axis == 1:
        return jnp.transpose(rows.reshape(N_SHARD, a, b // N_SHARD), (1, 0, 2)).reshape(a, b)
    return rows.reshape(a, b)


def kernel(x, mem, ffn1_norm, ffn1_w_gate, ffn1_w_up, ffn1_w_down, mix_norm, mem_norm, w_in, ssd_conv_w, ssd_conv_b, ssd_dt_bias, ssd_a_log, ssd_d, ssd_norm, gmlp_v_norm, gmlp_w_s, gmlp_b_s, w_mem_kv, w_branch_ssd, w_branch_gmlp, w_branch_mem, w_out, ffn2_norm, ffn2_w_gate, ffn2_w_up, ffn2_w_down, final_norm, loss_target, m_ffn1_norm, m_ffn1_w_gate, m_ffn1_w_up, m_ffn1_w_down, m_mix_norm, m_mem_norm, m_w_in, m_ssd_conv_w, m_ssd_conv_b, m_ssd_dt_bias, m_ssd_a_log, m_ssd_d, m_ssd_norm, m_gmlp_v_norm, m_gmlp_w_s, m_gmlp_b_s, m_w_mem_kv, m_w_branch_ssd, m_w_branch_gmlp, m_w_branch_mem, m_w_out, m_ffn2_norm, m_ffn2_w_gate, m_ffn2_w_up, m_ffn2_w_down, m_final_norm, v_ffn1_norm, v_ffn1_w_gate, v_ffn1_w_up, v_ffn1_w_down, v_mix_norm, v_mem_norm, v_w_in, v_ssd_conv_w, v_ssd_conv_b, v_ssd_dt_bias, v_ssd_a_log, v_ssd_d, v_ssd_norm, v_gmlp_v_norm, v_gmlp_w_s, v_gmlp_b_s, v_w_mem_kv, v_w_branch_ssd, v_w_branch_gmlp, v_w_branch_mem, v_w_out, v_ffn2_norm, v_ffn2_w_gate, v_ffn2_w_up, v_ffn2_w_down, v_final_norm):
    given = dict(x=x, mem=mem, ffn1_norm=ffn1_norm, ffn1_w_gate=ffn1_w_gate, ffn1_w_up=ffn1_w_up, ffn1_w_down=ffn1_w_down, mix_norm=mix_norm, mem_norm=mem_norm, w_in=w_in, ssd_conv_w=ssd_conv_w, ssd_conv_b=ssd_conv_b, ssd_dt_bias=ssd_dt_bias, ssd_a_log=ssd_a_log, ssd_d=ssd_d, ssd_norm=ssd_norm, gmlp_v_norm=gmlp_v_norm, gmlp_w_s=gmlp_w_s, gmlp_b_s=gmlp_b_s, w_mem_kv=w_mem_kv, w_branch_ssd=w_branch_ssd, w_branch_gmlp=w_branch_gmlp, w_branch_mem=w_branch_mem, w_out=w_out, ffn2_norm=ffn2_norm, ffn2_w_gate=ffn2_w_gate, ffn2_w_up=ffn2_w_up, ffn2_w_down=ffn2_w_down, final_norm=final_norm, loss_target=loss_target, m_ffn1_norm=m_ffn1_norm, m_ffn1_w_gate=m_ffn1_w_gate, m_ffn1_w_up=m_ffn1_w_up, m_ffn1_w_down=m_ffn1_w_down, m_mix_norm=m_mix_norm, m_mem_norm=m_mem_norm, m_w_in=m_w_in, m_ssd_conv_w=m_ssd_conv_w, m_ssd_conv_b=m_ssd_conv_b, m_ssd_dt_bias=m_ssd_dt_bias, m_ssd_a_log=m_ssd_a_log, m_ssd_d=m_ssd_d, m_ssd_norm=m_ssd_norm, m_gmlp_v_norm=m_gmlp_v_norm, m_gmlp_w_s=m_gmlp_w_s, m_gmlp_b_s=m_gmlp_b_s, m_w_mem_kv=m_w_mem_kv, m_w_branch_ssd=m_w_branch_ssd, m_w_branch_gmlp=m_w_branch_gmlp, m_w_branch_mem=m_w_branch_mem, m_w_out=m_w_out, m_ffn2_norm=m_ffn2_norm, m_ffn2_w_gate=m_ffn2_w_gate, m_ffn2_w_up=m_ffn2_w_up, m_ffn2_w_down=m_ffn2_w_down, m_final_norm=m_final_norm, v_ffn1_norm=v_ffn1_norm, v_ffn1_w_gate=v_ffn1_w_gate, v_ffn1_w_up=v_ffn1_w_up, v_ffn1_w_down=v_ffn1_w_down, v_mix_norm=v_mix_norm, v_mem_norm=v_mem_norm, v_w_in=v_w_in, v_ssd_conv_w=v_ssd_conv_w, v_ssd_conv_b=v_ssd_conv_b, v_ssd_dt_bias=v_ssd_dt_bias, v_ssd_a_log=v_ssd_a_log, v_ssd_d=v_ssd_d, v_ssd_norm=v_ssd_norm, v_gmlp_v_norm=v_gmlp_v_norm, v_gmlp_w_s=v_gmlp_w_s, v_gmlp_b_s=v_gmlp_b_s, v_w_mem_kv=v_w_mem_kv, v_w_branch_ssd=v_w_branch_ssd, v_w_branch_gmlp=v_w_branch_gmlp, v_w_branch_mem=v_w_branch_mem, v_w_out=v_w_out, v_ffn2_norm=v_ffn2_norm, v_ffn2_w_gate=v_ffn2_w_gate, v_ffn2_w_up=v_ffn2_w_up, v_ffn2_w_down=v_ffn2_w_down, v_final_norm=v_final_norm)
    weights = [n for n in given if n not in ("x", "mem", "loss_target") and not n.startswith(("m_", "v_"))]
    xi, yi, ci = lax.axis_index("x"), lax.axis_index("y"), lax.axis_index("c")
    chip = (2 * xi + yi).astype(jnp.int32)
    rh = PACK_ROWS // 2

    local_rows = [given[name].astype(BF16).reshape(rows, LANES) for (name, _, _), rows in zip(BIG, BIG_ROWS)]
    conv_pairs = lax.bitcast_convert_type(given["ssd_conv_w"], BF16).reshape(-1)
    local_rows.append(jnp.pad(conv_pairs, (0, CONV_W_ROWS * LANES - conv_pairs.shape[0])).reshape(CONV_W_ROWS, LANES))
    packed = jnp.concatenate(local_rows, axis=0)
    packed = jnp.pad(packed, ((0, PACK_ROWS - packed.shape[0]), (0, 0))).reshape(2, rh, LANES)
    gathered = _gather_weights(packed).reshape(N_SHARD, PACK_ROWS, LANES)
    w, r = {}, 0
    for (name, shape, axis), rows in zip(BIG, BIG_ROWS):
        w[name] = _unshard_rows(gathered[:, r:r + rows], shape, axis)
        r += rows
    conv_cols = XBC // N_SHARD
    conv_w = lax.bitcast_convert_type(gathered[:, r:r + CONV_W_ROWS].reshape(N_SHARD, -1)[:, :SSD_CONV * conv_cols * 2]
                                      .reshape(N_SHARD, SSD_CONV, conv_cols, 2), F32)
    w["ssd_conv_w"] = jnp.transpose(conv_w, (1, 0, 2)).reshape(SSD_CONV, XBC)
    w["w_in_p"], w["w_dt"] = _split_w_in(w.pop("w_in"))
    for name in ("ffn1_norm", "mix_norm", "mem_norm", "ssd_conv_b", "ssd_norm", "gmlp_v_norm", "ffn2_norm", "final_norm"):
        w[name] = given[name].reshape(1, -1)
    for name in HEAD_VECS:
        w[name] = given[name].reshape(-1)
    w["gmlp_w_s"] = given["gmlp_w_s"][0]
    w["gmlp_b_s"] = given["gmlp_b_s"][0]

    loss_part, grad_x, g = _local_step(x[0], mem[0], loss_target[0], w)

    gp = jnp.concatenate([_shard_rows(g[name], axis).astype(BF16) for name, _, axis in BIG], axis=1)
    gp = jnp.pad(gp, ((0, 0), (0, PACK_ROWS - gp.shape[1]), (0, 0))).reshape(N_SHARD, 2, rh, LANES)
    gp = jnp.transpose(gp, (1, 0, 2, 3))
    recv_a = _rs_swap(gp)
    pa = _rs_add(gp, recv_a, ci.astype(jnp.int32).reshape(1))
    recv_b = _rs_scatter(pa)
    gsum = _rs_sum(pa, recv_b, chip.reshape(1))
    gred = _rs_share(gsum).reshape(PACK_ROWS, LANES)

    small_vals = {k: g[k] for k, _ in SMALL if k != "heads"}
    small_vals.update({k: g[k] for k in HEAD_VECS})
    red = _unpack_small(_allreduce_small(_pack_small(small_vals, loss=loss_part[0, 0])))

    grads, deltas, new_m, new_v = {}, {}, {}, {}
    r = 0
    for (name, _, _), rows in zip(BIG, BIG_ROWS):
        local_shape = given[name].shape
        gl = gred[r:r + rows]
        r += rows
        d, nm, nv = _adamw(given[name].reshape(rows, LANES), gl, given["m_" + name].reshape(rows, LANES),
                           given["v_" + name].reshape(rows, LANES), f"adamw_{name}")
        grads[name], deltas[name], new_m[name], new_v[name] = (a.reshape(local_shape) for a in (gl, d, nm, nv))
    conv_g = lax.dynamic_slice_in_dim(red["ssd_conv_w"].reshape(SSD_CONV, XBC), chip * conv_cols, conv_cols, axis=1)
    pad8 = lambda a: jnp.pad(a.reshape(-1, LANES), ((0, 8 - SSD_CONV * conv_cols // LANES), (0, 0)))
    d, nm, nv = _adamw(pad8(given["ssd_conv_w"]), pad8(conv_g), pad8(given["m_ssd_conv_w"]), pad8(given["v_ssd_conv_w"]), "adamw_conv_w")
    unpad = lambda a: a[:SSD_CONV * conv_cols // LANES].reshape(given["ssd_conv_w"].shape)
    grads["ssd_conv_w"], deltas["ssd_conv_w"], new_m["ssd_conv_w"], new_v["ssd_conv_w"] = (
        conv_g.reshape(given["ssd_conv_w"].shape), unpad(d), unpad(nm), unpad(nv))
    small_names = [k for k, _ in SMALL if k not in ("heads", "ssd_conv_w")] + list(HEAD_VECS)
    pack_of = lambda prefix: _pack_small({k: given[prefix + k] for k in small_names})
    g_small = _pack_small({k: red[k] for k in small_names})
    d, nm, nv = (_unpack_small(a) for a in _adamw(pack_of(""), g_small, pack_of("m_"), pack_of("v_"), "adamw_small"))
    for k in small_names:
        shape = given[k].shape
        grads[k], deltas[k], new_m[k], new_v[k] = (a[k].reshape(shape) for a in (red, d, nm, nv))

    return (red["loss"], grad_x[None], *[grads[n] for n in weights], *[deltas[n] for n in weights],
            *[new_m[n] for n in weights], *[new_v[n] for n in weights])
```

```python
import functools
import math

import jax
import jax.numpy as jnp
from jax import lax
from jax.experimental import pallas as pl
from jax.experimental.pallas import tpu as pltpu

F32, BF16 = jnp.float32, jnp.bfloat16
HI = lax.Precision.HIGHEST
MESH = pl.DeviceIdType.MESH

D_MODEL = 1024
D_FF = 2816
MEM_LEN = 256
SSD_INNER = 2048
SSD_HEADS = 32
SSD_GROUPS = 4
SSD_STATE = 128
SSD_CONV = 4
CHUNK = 128
XBC = SSD_INNER + 2 * SSD_GROUPS * SSD_STATE
GMLP_W = 1024
GMLP_GROUPS = 8
MEM_W = 256
MEM_HEADS = 4
EPS = 1e-6
IN_WIDTH = 10528
IN_Z, IN_XBC, IN_DT, IN_UV, IN_Q, IN_GL = 0, 2048, 5120, 5152, 7200, 7456
P_Z, P_UV, P_XBC, P_GL, P_Q, P_W = 0, 2048, 4096, 7168, 10240, 10752
P_USED = 10496

ADAM_LR, ADAM_B1, ADAM_B2, ADAM_EPS, ADAM_WD, ADAM_STEP = 0.001, 0.9, 0.999, 1e-08, 0.01, 10

V7X_VMEM_LIMIT = 56 * 1024 * 1024
N_SHARD = 4
LANES = 1024


def _cparams(*sem):
    return pltpu.CompilerParams(dimension_semantics=sem, vmem_limit_bytes=V7X_VMEM_LIMIT)


def _sigmoid(x):
    return 1.0 / (1.0 + jnp.exp(-x))


def _row_tile(t):
    return min(512, t)


_DIMS = {"nn": (((1,), (0,)), ((), ())), "nt": (((1,), (1,)), ((), ())), "tn": (((0,), (0,)), ((), ()))}


def _matmul(a, b, *, mode, out_dtype, tm, tn, tk, name, scale=1.0, addend=None):
    if mode == "tn":
        k_dim, m_dim = a.shape
    else:
        m_dim, k_dim = a.shape
    n_dim = b.shape[0] if mode == "nt" else b.shape[1]
    tm, tn, tk = min(tm, m_dim), min(tn, n_dim), min(tk, k_dim)
    assert m_dim % tm == 0 and n_dim % tn == 0 and k_dim % tk == 0, (name, a.shape, b.shape, tm, tn, tk)
    ni, nj, nk = m_dim // tm, n_dim // tn, k_dim // tk
    a_spec = pl.BlockSpec((tk, tm), lambda j, i, k: (k, i)) if mode == "tn" else pl.BlockSpec((tm, tk), lambda j, i, k: (i, k))
    b_spec = pl.BlockSpec((tn, tk), lambda j, i, k: (j, k)) if mode == "nt" else pl.BlockSpec((tk, tn), lambda j, i, k: (k, j))
    o_spec = pl.BlockSpec((tm, tn), lambda j, i, k: (i, j))
    dims = _DIMS[mode]
    has_add = addend is not None

    def body(*refs):
        if has_add:
            a_ref, b_ref, r_ref, o_ref, acc_ref = refs
        else:
            a_ref, b_ref, o_ref, acc_ref = refs
        k = pl.program_id(2)

        @pl.when(k == 0)
        def _():
            acc_ref[...] = jnp.zeros_like(acc_ref)

        acc_ref[...] += lax.dot_general(a_ref[...].astype(BF16), b_ref[...].astype(BF16), dims, preferred_element_type=F32)

        @pl.when(k == nk - 1)
        def _():
            r = acc_ref[...] * scale
            if has_add:
                r = r + r_ref[...].astype(F32)
            o_ref[...] = r.astype(o_ref.dtype)

    in_specs = [a_spec, b_spec] + ([o_spec] if has_add else [])
    args = (a, b) + ((addend,) if has_add else ())
    return pl.pallas_call(
        body, name=name, grid=(nj, ni, nk), in_specs=in_specs, out_specs=o_spec,
        out_shape=jax.ShapeDtypeStruct((m_dim, n_dim), out_dtype),
        scratch_shapes=[pltpu.VMEM((tm, tn), F32)],
        compiler_params=_cparams("parallel", "parallel", "arbitrary"),
    )(*args)


ROW_STRIP = 16


def _strips(tm, fn, init=None, rb=ROW_STRIP):
    def step(i, carry):
        return fn(pl.ds(pl.multiple_of(i * rb, rb), rb), carry)
    return lax.fori_loop(0, tm // rb, step, init, unroll=2)


def _rms_fwd(x, gain, name):
    t, d = x.shape
    tm = _row_tile(t)

    def body(x_ref, g_ref, o_ref):
        g = g_ref[...]

        def strip(rows, carry):
            xv = x_ref[rows, :]
            r = lax.rsqrt(jnp.mean(xv * xv, axis=-1, keepdims=True) + EPS)
            o_ref[rows, :] = (xv * r * g).astype(o_ref.dtype)
            return carry

        _strips(tm, strip, 0)

    return pl.pallas_call(
        body, name=name, grid=(t // tm,),
        in_specs=[pl.BlockSpec((tm, d), lambda i: (i, 0)), pl.BlockSpec((1, d), lambda i: (0, 0))],
        out_specs=pl.BlockSpec((tm, d), lambda i: (i, 0)),
        out_shape=jax.ShapeDtypeStruct((t, d), BF16), compiler_params=_cparams("parallel"),
    )(x, gain)


def _rms_bwd(x, gain, dn, dres, name):
    t, d = x.shape
    tm = _row_tile(t)
    has_res = dres is not None

    def body(*refs):
        if has_res:
            x_ref, g_ref, dn_ref, r_ref, dx_ref, dg_ref = refs
        else:
            x_ref, g_ref, dn_ref, dx_ref, dg_ref = refs

        @pl.when(pl.program_id(0) == 0)
        def _():
            dg_ref[...] = jnp.zeros_like(dg_ref)

        g = g_ref[...]

        def strip(rows, acc):
            xv = x_ref[rows, :]
            r = lax.rsqrt(jnp.mean(xv * xv, axis=-1, keepdims=True) + EPS)
            xh = xv * r
            dnv = dn_ref[rows, :].astype(F32)
            dxh = dnv * g
            dx = r * (dxh - xh * jnp.mean(dxh * xh, axis=-1, keepdims=True))
            if has_res:
                dx = dx + r_ref[rows, :]
            dx_ref[rows, :] = dx
            return acc + dnv * xh

        acc = _strips(tm, strip, jnp.zeros((ROW_STRIP, d), F32))
        dg_ref[...] += jnp.sum(acc, axis=0, keepdims=True)

    row = pl.BlockSpec((tm, d), lambda i: (i, 0))
    vec = pl.BlockSpec((1, d), lambda i: (0, 0))
    in_specs = [row, vec, row] + ([row] if has_res else [])
    args = (x, gain, dn) + ((dres,) if has_res else ())
    return pl.pallas_call(
        body, name=name, grid=(t // tm,), in_specs=in_specs, out_specs=[row, vec],
        out_shape=[jax.ShapeDtypeStruct((t, d), F32), jax.ShapeDtypeStruct((1, d), F32)],
        compiler_params=_cparams("arbitrary"),
    )(*args)


def _loss_head(h, gain, target, name):
    t, d = h.shape
    tm = _row_tile(t)

    def body(h_ref, g_ref, t_ref, dh_ref, dg_ref, l_ref):
        @pl.when(pl.program_id(0) == 0)
        def _():
            dg_ref[...] = jnp.zeros_like(dg_ref)
            l_ref[...] = jnp.zeros_like(l_ref)

        g = g_ref[...]

        def strip(rows, carry):
            acc_g, acc_l = carry
            xv = h_ref[rows, :]
            r = lax.rsqrt(jnp.mean(xv * xv, axis=-1, keepdims=True) + EPS)
            xh = xv * r
            err = xh * g - t_ref[rows, :]
            dy = err * (1.0 / d)
            dxh = dy * g
            dh_ref[rows, :] = r * (dxh - xh * jnp.mean(dxh * xh, axis=-1, keepdims=True))
            return acc_g + dy * xh, acc_l + err * err

        zero = jnp.zeros((ROW_STRIP, d), F32)
        acc_g, acc_l = _strips(tm, strip, (zero, zero))
        dg_ref[...] += jnp.sum(acc_g, axis=0, keepdims=True)
        l_ref[...] += (0.5 / d) * jnp.sum(jnp.sum(acc_l, axis=-1, keepdims=True), axis=0, keepdims=True)

    row = pl.BlockSpec((tm, d), lambda i: (i, 0))
    vec = pl.BlockSpec((1, d), lambda i: (0, 0))
    return pl.pallas_call(
        body, name=name, grid=(t // tm,), in_specs=[row, vec, row],
        out_specs=[row, vec, pl.BlockSpec((1, 128), lambda i: (0, 0))],
        out_shape=[jax.ShapeDtypeStruct((t, d), F32), jax.ShapeDtypeStruct((1, d), F32), jax.ShapeDtypeStruct((1, 128), F32)],
        compiler_params=_cparams("arbitrary"),
    )(h, gain, target)


FF_TILE = 1408


def _ffn_fwd(n, x, wg, wu, wd, name):
    t, d = x.shape
    tm, tn = _row_tile(t), FF_TILE
    nj = D_FF // tn

    def body(n_ref, x_ref, wg_ref, wu_ref, wd_ref, h_ref, g_ref, u_ref, acc_ref):
        j = pl.program_id(1)

        @pl.when(j == 0)
        def _():
            acc_ref[...] = jnp.zeros_like(acc_ref)

        nb = n_ref[...]
        g = jnp.dot(nb, wg_ref[...], preferred_element_type=F32)
        u = jnp.dot(nb, wu_ref[...], preferred_element_type=F32)
        g_ref[...] = g.astype(BF16)
        u_ref[...] = u.astype(BF16)
        a = g * _sigmoid(g) * u
        acc_ref[...] += jnp.dot(a.astype(BF16), wd_ref[...], preferred_element_type=F32)

        @pl.when(j == nj - 1)
        def _():
            h_ref[...] = x_ref[...] + 0.5 * acc_ref[...]

    row = pl.BlockSpec((tm, d), lambda i, j: (i, 0))
    act = pl.BlockSpec((tm, tn), lambda i, j: (i, j))
    return pl.pallas_call(
        body, name=name, grid=(t // tm, nj),
        in_specs=[row, row, pl.BlockSpec((d, tn), lambda i, j: (0, j)), pl.BlockSpec((d, tn), lambda i, j: (0, j)),
                  pl.BlockSpec((tn, d), lambda i, j: (j, 0))],
        out_specs=[row, act, act],
        out_shape=[jax.ShapeDtypeStruct((t, d), F32), jax.ShapeDtypeStruct((t, D_FF), BF16), jax.ShapeDtypeStruct((t, D_FF), BF16)],
        scratch_shapes=[pltpu.VMEM((tm, d), F32)], compiler_params=_cparams("parallel", "arbitrary"),
    )(n, x, wg, wu, wd)


def _ffn_bwd_act(dh, g, u, wg, wu, wd, name):
    t, d = dh.shape
    tm, tn = _row_tile(t), FF_TILE
    nj = D_FF // tn

    def body(dh_ref, g_ref, u_ref, wg_ref, wu_ref, wd_ref, dn_ref, dg_ref, du_ref, a_ref, acc_ref):
        j = pl.program_id(1)

        @pl.when(j == 0)
        def _():
            acc_ref[...] = jnp.zeros_like(acc_ref)

        dhb = (0.5 * dh_ref[...]).astype(BF16)
        da = lax.dot_general(dhb, wd_ref[...], _DIMS["nt"], preferred_element_type=F32)
        gv = g_ref[...].astype(F32)
        uv = u_ref[...].astype(F32)
        sg = _sigmoid(gv)
        s = gv * sg
        dg = (da * uv * (sg * (1.0 + gv * (1.0 - sg)))).astype(BF16)
        du = (da * s).astype(BF16)
        dg_ref[...] = dg
        du_ref[...] = du
        a_ref[...] = (s * uv).astype(BF16)
        acc_ref[...] += (lax.dot_general(dg, wg_ref[...], _DIMS["nt"], preferred_element_type=F32)
                         + lax.dot_general(du, wu_ref[...], _DIMS["nt"], preferred_element_type=F32))

        @pl.when(j == nj - 1)
        def _():
            dn_ref[...] = acc_ref[...]

    row = pl.BlockSpec((tm, d), lambda i, j: (i, 0))
    act = pl.BlockSpec((tm, tn), lambda i, j: (i, j))
    return pl.pallas_call(
        body, name=name, grid=(t // tm, nj),
        in_specs=[row, act, act, pl.BlockSpec((d, tn), lambda i, j: (0, j)), pl.BlockSpec((d, tn), lambda i, j: (0, j)),
                  pl.BlockSpec((tn, d), lambda i, j: (j, 0))],
        out_specs=[row, act, act, act],
        out_shape=[jax.ShapeDtypeStruct((t, d), F32)] + [jax.ShapeDtypeStruct((t, D_FF), BF16)] * 3,
        scratch_shapes=[pltpu.VMEM((tm, d), F32)], compiler_params=_cparams("parallel", "arbitrary"),
    )(dh, g, u, wg, wu, wd)


def _ffn_forward(x, gain, wg, wu, wd, tag):
    n = _rms_fwd(x, gain, f"{tag}_norm")
    h, g, u = _ffn_fwd(n, x, wg, wu, wd, f"{tag}_fwd")
    return h, (n, g, u)


def _ffn_backward(dh, x, gain, wg, wu, wd, saved, tag):
    n, g, u = saved
    dn, dg, du, a = _ffn_bwd_act(dh, g, u, wg, wu, wd, f"{tag}_bwd_act")
    kw = dict(mode="tn", out_dtype=BF16, tm=1024, tn=FF_TILE, tk=1024)
    d_wg = _matmul(n, dg, name=f"{tag}_dwg", **kw)
    d_wu = _matmul(n, du, name=f"{tag}_dwu", **kw)
    d_wd = _matmul(a, dh, mode="tn", out_dtype=BF16, tm=FF_TILE, tn=1024, tk=1024, scale=0.5, name=f"{tag}_dwd")
    dx, dgain = _rms_bwd(x, gain, dn, dh, f"{tag}_norm_bwd")
    return dx, dict(norm=dgain, w_gate=d_wg, w_up=d_wu, w_down=d_wd)


CONV_COLS = 512
HALO = 8
CONV_STRIP = 32


def _conv_fwd(p, w, b, name):
    t = p.shape[0]
    tm = _row_tile(t)
    c0 = P_XBC // CONV_COLS

    def body(x_ref, halo_ref, w_ref, b_ref, o_ref, s_ref):
        i = pl.program_id(1)
        s_ref[0:HALO, :] = jnp.where(i > 0, halo_ref[...].astype(F32), 0.0)
        s_ref[HALO:HALO + tm, :] = x_ref[...].astype(F32)
        wv = w_ref[...]
        bv = b_ref[...]
        for r0 in range(0, tm, CONV_STRIP):
            acc = bv + wv[0:1, :] * s_ref[HALO - 3 + r0:HALO - 3 + r0 + CONV_STRIP, :]
            for k in range(1, SSD_CONV):
                acc = acc + wv[k:k + 1, :] * s_ref[HALO - 3 + k + r0:HALO - 3 + k + r0 + CONV_STRIP, :]
            o_ref[r0:r0 + CONV_STRIP, :] = (acc * _sigmoid(acc)).astype(o_ref.dtype)

    return pl.pallas_call(
        body, name=name, grid=(XBC // CONV_COLS, t // tm),
        in_specs=[pl.BlockSpec((tm, CONV_COLS), lambda j, i: (i, c0 + j)),
                  pl.BlockSpec((HALO, CONV_COLS), lambda j, i: (jnp.maximum(i * (tm // HALO) - 1, 0), c0 + j)),
                  pl.BlockSpec((SSD_CONV, CONV_COLS), lambda j, i: (0, j)), pl.BlockSpec((1, CONV_COLS), lambda j, i: (0, j))],
        out_specs=pl.BlockSpec((tm, CONV_COLS), lambda j, i: (i, j)),
        out_shape=jax.ShapeDtypeStruct((t, XBC), BF16),
        scratch_shapes=[pltpu.VMEM((tm + HALO, CONV_COLS), F32)], compiler_params=_cparams("parallel", "parallel"),
    )(p, p, w, b)


def _conv_bwd_act(p, dy, w, b, col0, name):
    t, cols = dy.shape
    tm = _row_tile(t)
    c0 = (P_XBC + col0) // CONV_COLS
    w0 = col0 // CONV_COLS

    def body(x_ref, halo_ref, dy_ref, w_ref, b_ref, da_ref, dw_ref, db_ref, s_ref):
        i = pl.program_id(1)

        @pl.when(i == 0)
        def _():
            dw_ref[...] = jnp.zeros_like(dw_ref)
            db_ref[...] = jnp.zeros_like(db_ref)

        s_ref[0:HALO, :] = jnp.where(i > 0, halo_ref[...].astype(F32), 0.0)
        s_ref[HALO:HALO + tm, :] = x_ref[...].astype(F32)
        wv = w_ref[...]
        bv = b_ref[...]
        fold = lambda v: jnp.sum(v.reshape(CONV_STRIP // 8, 8, CONV_COLS), axis=0)
        sums = [jnp.zeros((8, CONV_COLS), F32) for _ in range(SSD_CONV + 1)]
        for r0 in range(0, tm, CONV_STRIP):
            taps = [s_ref[HALO - 3 + k + r0:HALO - 3 + k + r0 + CONV_STRIP, :] for k in range(SSD_CONV)]
            acc = bv + wv[0:1, :] * taps[0]
            for k in range(1, SSD_CONV):
                acc = acc + wv[k:k + 1, :] * taps[k]
            sg = _sigmoid(acc)
            dacc = dy_ref[r0:r0 + CONV_STRIP, :].astype(F32) * (sg * (1.0 + acc * (1.0 - sg)))
            da_ref[r0:r0 + CONV_STRIP, :] = dacc.astype(BF16)
            for k in range(SSD_CONV):
                sums[k] = sums[k] + fold(dacc * taps[k])
            sums[SSD_CONV] = sums[SSD_CONV] + fold(dacc)
        for k in range(SSD_CONV):
            dw_ref[k:k + 1, :] += jnp.sum(sums[k], axis=0, keepdims=True)
        db_ref[...] += jnp.sum(sums[SSD_CONV], axis=0, keepdims=True)

    return pl.pallas_call(
        body, name=name, grid=(cols // CONV_COLS, t // tm),
        in_specs=[pl.BlockSpec((tm, CONV_COLS), lambda j, i: (i, c0 + j)),
                  pl.BlockSpec((HALO, CONV_COLS), lambda j, i: (jnp.maximum(i * (tm // HALO) - 1, 0), c0 + j)),
                  pl.BlockSpec((tm, CONV_COLS), lambda j, i: (i, j)),
                  pl.BlockSpec((SSD_CONV, CONV_COLS), lambda j, i: (0, w0 + j)), pl.BlockSpec((1, CONV_COLS), lambda j, i: (0, w0 + j))],
        out_specs=[pl.BlockSpec((tm, CONV_COLS), lambda j, i: (i, j)), pl.BlockSpec((SSD_CONV, CONV_COLS), lambda j, i: (0, j)),
                   pl.BlockSpec((1, CONV_COLS), lambda j, i: (0, j))],
        out_shape=[jax.ShapeDtypeStruct((t, cols), BF16), jax.ShapeDtypeStruct((SSD_CONV, cols), F32), jax.ShapeDtypeStruct((1, cols), F32)],
        scratch_shapes=[pltpu.VMEM((tm + HALO, CONV_COLS), F32)], compiler_params=_cparams("parallel", "arbitrary"),
    )(p, p, dy, w, b)


def _conv_bwd_dx(dacc, w, col0, name):
    t, cols = dacc.shape
    tm = _row_tile(t)
    nt = t // tm
    w0 = col0 // CONV_COLS

    def body(d_ref, halo_ref, w_ref, o_ref, s_ref):
        i = pl.program_id(1)
        s_ref[0:tm, :] = d_ref[...].astype(F32)
        s_ref[tm:tm + HALO, :] = jnp.where(i < nt - 1, halo_ref[...].astype(F32), 0.0)
        wv = w_ref[...]
        for r0 in range(0, tm, CONV_STRIP):
            acc = wv[3:4, :] * s_ref[r0:r0 + CONV_STRIP, :]
            for k in range(SSD_CONV - 1):
                acc = acc + wv[k:k + 1, :] * s_ref[3 - k + r0:3 - k + r0 + CONV_STRIP, :]
            o_ref[r0:r0 + CONV_STRIP, :] = acc.astype(o_ref.dtype)

    return pl.pallas_call(
        body, name=name, grid=(cols // CONV_COLS, nt),
        in_specs=[pl.BlockSpec((tm, CONV_COLS), lambda j, i: (i, j)),
                  pl.BlockSpec((HALO, CONV_COLS), lambda j, i: (jnp.minimum((i + 1) * (tm // HALO), t // HALO - 1), j)),
                  pl.BlockSpec((SSD_CONV, CONV_COLS), lambda j, i: (0, w0 + j))],
        out_specs=pl.BlockSpec((tm, CONV_COLS), lambda j, i: (i, j)),
        out_shape=jax.ShapeDtypeStruct((t, cols), BF16),
        scratch_shapes=[pltpu.VMEM((tm + HALO, CONV_COLS), F32)], compiler_params=_cparams("parallel", "parallel"),
    )(dacc, dacc, w)


GROUP_COLS = SSD_INNER // SSD_GROUPS
PAIRS = GROUP_COLS // 128
HEADS_PER_GROUP = SSD_HEADS // SSD_GROUPS


def _dt_fwd(dt_raw, bias, name):
    t, n = dt_raw.shape
    tm = _row_tile(t)

    def body(x_ref, b_ref, o_ref):
        v = x_ref[...] + b_ref[...]
        o_ref[...] = jnp.maximum(v, 0.0) + jnp.log1p(jnp.exp(-jnp.abs(v)))

    row = pl.BlockSpec((tm, n), lambda i: (i, 0))
    vec = pl.BlockSpec((1, n), lambda i: (0, 0))
    return pl.pallas_call(body, name=name, grid=(t // tm,), in_specs=[row, vec], out_specs=row,
                          out_shape=jax.ShapeDtypeStruct((t, n), F32), compiler_params=_cparams("parallel"))(dt_raw, bias)


def _dt_bwd(ddt, dt_raw, bias, name):
    t, n = dt_raw.shape
    tm = _row_tile(t)

    def body(d_ref, x_ref, b_ref, o_ref, db_ref):
        @pl.when(pl.program_id(0) == 0)
        def _():
            db_ref[...] = jnp.zeros_like(db_ref)

        dr = d_ref[...] * _sigmoid(x_ref[...] + b_ref[...])
        o_ref[...] = dr.astype(o_ref.dtype)
        db_ref[...] += jnp.sum(dr, axis=0, keepdims=True)

    row = pl.BlockSpec((tm, n), lambda i: (i, 0))
    vec = pl.BlockSpec((1, n), lambda i: (0, 0))
    return pl.pallas_call(body, name=name, grid=(t // tm,), in_specs=[row, row, vec], out_specs=[row, vec],
                          out_shape=[jax.ShapeDtypeStruct((t, n), BF16), jax.ShapeDtypeStruct((1, n), F32)],
                          compiler_params=_cparams("arbitrary"))(ddt, dt_raw, bias)


def _ssd_common(dtw_ref, dtt_ref, aw_ref, at_ref):
    l = CHUNK
    dt = dtw_ref[0]
    dtt = dtt_ref[0]
    a = -jnp.exp(aw_ref[0])
    at = -jnp.exp(at_ref[0])
    rowi = lax.broadcasted_iota(jnp.int32, (l, l), 0)
    coli = lax.broadcasted_iota(jnp.int32, (l, l), 1)
    tri = rowi >= coli
    lower = tri.astype(F32)
    upper = (rowi <= coli).astype(F32)
    acs = jnp.dot(lower, dt * a, precision=HI, preferred_element_type=F32)
    acst = jnp.dot(dtt * at, upper, precision=HI, preferred_element_type=F32)
    return dt, a, acs, acst, tri, upper


def _pair_bc(w, lo, p):
    return jnp.where(lo, w[:, 2 * p:2 * p + 1], w[:, 2 * p + 1:2 * p + 2])


def _ssd_specs(t):
    nc = t // CHUNK
    return nc, dict(
        xs=lambda cm: pl.BlockSpec((CHUNK, GROUP_COLS), lambda g, c: (cm(c), g)),
        bm=lambda cm: pl.BlockSpec((CHUNK, SSD_STATE), lambda g, c: (cm(c), SSD_INNER // SSD_STATE + g)),
        cmat=lambda cm: pl.BlockSpec((CHUNK, SSD_STATE), lambda g, c: (cm(c), SSD_INNER // SSD_STATE + SSD_GROUPS + g)),
        dtw=lambda cm: pl.BlockSpec((1, CHUNK, 128), lambda g, c: (g, cm(c), 0)),
        dtt=lambda cm: pl.BlockSpec((1, HEADS_PER_GROUP, CHUNK), lambda g, c: (g, 0, cm(c))),
        wide=lambda cm: pl.BlockSpec((1, 1, 128), lambda g, c: (g, 0, 0)),
        tall=lambda cm: pl.BlockSpec((1, HEADS_PER_GROUP, 1), lambda g, c: (g, 0, 0)),
        grp=lambda cm: pl.BlockSpec((CHUNK, GROUP_COLS), lambda g, c: (cm(c), g)),
        vec=lambda cm: pl.BlockSpec((1, GROUP_COLS), lambda g, c: (0, g)),
        state=lambda cm: pl.BlockSpec((1, 1, PAIRS, SSD_STATE, 128), lambda g, c: (g, cm(c), 0, 0, 0)),
    )


def _ssd_fwd(xc, p, hv, norm_g, name):
    t = xc.shape[0]
    nc, sp = _ssd_specs(t)
    ident = lambda c: c

    def body(xs_ref, b_ref, c_ref, dtw_ref, dtt_ref, aw_ref, at_ref, dk_ref, z_ref, ng_ref,
             y_ref, ys_ref, h_ref, st_ref):
        @pl.when(pl.program_id(1) == 0)
        def _():
            st_ref[...] = jnp.zeros_like(st_ref)

        dt, a, acs, acst, tri, _ = _ssd_common(dtw_ref, dtt_ref, aw_ref, at_ref)
        ecs = jnp.exp(acs)
        alast = acs[CHUNK - 1:CHUNK, :]
        bmat, cmat = b_ref[...], c_ref[...]
        cb = lax.dot_general(cmat, bmat, _DIMS["nt"], preferred_element_type=F32)
        lo = lax.broadcasted_iota(jnp.int32, (1, 128), 1) < 64
        dskip = dk_ref[0]
        for pi in range(PAIRS):
            x = xs_ref[:, pi * 128:(pi + 1) * 128].astype(F32)
            xdt = x * _pair_bc(dt, lo, pi)
            ydiag = jnp.zeros((CHUNK, 128), F32)
            for r, mask in ((2 * pi, lo), (2 * pi + 1, jnp.logical_not(lo))):
                lam = jnp.exp(jnp.where(tri, acs[:, r:r + 1] - acst[r:r + 1, :], -1e30))
                m = (cb * lam).astype(BF16)
                ydiag = ydiag + jnp.dot(m, jnp.where(mask, xdt, 0.0).astype(BF16), preferred_element_type=F32)
            ht = st_ref[pi]
            h_ref[0, 0, pi] = ht
            yoff = jnp.dot(cmat, ht.astype(BF16), preferred_element_type=F32) * _pair_bc(ecs, lo, pi)
            y_ref[:, pi * 128:(pi + 1) * 128] = (ydiag + yoff + _pair_bc(dskip, lo, pi) * x).astype(y_ref.dtype)
            alp = _pair_bc(alast, lo, pi)
            e = jnp.exp(alp - _pair_bc(acs, lo, pi))
            st = lax.dot_general(bmat, (xdt * e).astype(BF16), _DIMS["tn"], preferred_element_type=F32)
            st_ref[pi] = ht * jnp.exp(alp) + st
        zf = z_ref[...].astype(F32)
        yg = y_ref[...].astype(F32) * (zf * _sigmoid(zf))
        rstd = lax.rsqrt(jnp.mean(yg * yg, axis=-1, keepdims=True) + EPS)
        ys_ref[...] = (yg * rstd * ng_ref[...]).astype(ys_ref.dtype)

    ins = ["xs", "bm", "cmat", "dtw", "dtt", "wide", "tall", "wide", "grp", "vec"]
    return pl.pallas_call(
        body, name=name, grid=(SSD_GROUPS, nc),
        in_specs=[sp[k](ident) for k in ins],
        out_specs=[sp["grp"](ident), sp["grp"](ident), sp["state"](ident)],
        out_shape=[jax.ShapeDtypeStruct((t, SSD_INNER), BF16), jax.ShapeDtypeStruct((t, SSD_INNER), BF16),
                   jax.ShapeDtypeStruct((SSD_GROUPS, nc, PAIRS, SSD_STATE, 128), F32)],
        scratch_shapes=[pltpu.VMEM((PAIRS, SSD_STATE, 128), F32)], compiler_params=_cparams("parallel", "arbitrary"),
    )(xc, xc, xc, hv["dtw"], hv["dtt"], hv["alog_w"], hv["alog_t"], hv["dskip_w"], p, norm_g)


def _ssd_bwd(xc, p, hv, norm_g, y, dys, states, name):
    t = xc.shape[0]
    nc, sp = _ssd_specs(t)
    rev = lambda c: nc - 1 - c

    def body(xs_ref, b_ref, c_ref, dtw_ref, dtt_ref, aw_ref, at_ref, dk_ref, z_ref, ng_ref,
             y_ref, dys_ref, h_ref,
             dxs_ref, db_ref, dc_ref, dz_ref, ddt_ref, hsum_ref, dng_ref, dst_ref):
        @pl.when(pl.program_id(1) == 0)
        def _():
            dst_ref[...] = jnp.zeros_like(dst_ref)
            hsum_ref[...] = jnp.zeros_like(hsum_ref)
            dng_ref[...] = jnp.zeros_like(dng_ref)

        dt, a, acs, acst, tri, upper = _ssd_common(dtw_ref, dtt_ref, aw_ref, at_ref)
        ecs = jnp.exp(acs)
        alast = acs[CHUNK - 1:CHUNK, :]
        bmat, cmat = b_ref[...], c_ref[...]
        cb = lax.dot_general(cmat, bmat, _DIMS["nt"], preferred_element_type=F32)
        lane = lax.broadcasted_iota(jnp.int32, (1, 128), 1)
        lo = lane < 64
        dskip = dk_ref[0]

        zf = z_ref[...].astype(F32)
        sg = _sigmoid(zf)
        sz = zf * sg
        yv = y_ref[...].astype(F32)
        yg = yv * sz
        rstd = lax.rsqrt(jnp.mean(yg * yg, axis=-1, keepdims=True) + EPS)
        yhat = yg * rstd
        dysv = dys_ref[...].astype(F32)
        dng_ref[...] += jnp.sum(dysv * yhat, axis=0, keepdims=True)
        dyh = dysv * ng_ref[...]
        dyg = rstd * (dyh - yhat * jnp.mean(dyh * yhat, axis=-1, keepdims=True))
        dz_ref[...] = (dyg * yv * (sg * (1.0 + zf * (1.0 - sg)))).astype(dz_ref.dtype)
        dy_all = dyg * sz

        sel_r = lax.broadcasted_iota(jnp.int32, (128, 128), 0)
        sel_c = lax.broadcasted_iota(jnp.int32, (128, 128), 1)
        dal = jnp.zeros((CHUNK, 128), F32)
        ddtm = jnp.zeros((CHUNK, 128), F32)
        dalast = jnp.zeros((8, 128), F32)
        ddsk = jnp.zeros((8, 128), F32)
        dcb = jnp.zeros((CHUNK, CHUNK), F32)
        qcol = jnp.zeros((8, CHUNK), F32)
        sub8 = lax.broadcasted_iota(jnp.int32, (8, CHUNK), 0)
        dc_acc = jnp.zeros((CHUNK, SSD_STATE), F32)
        db_acc = jnp.zeros((CHUNK, SSD_STATE), F32)
        for pi in range(PAIRS):
            sel = (sel_c == 2 * pi + (sel_r >= 64).astype(jnp.int32)).astype(BF16)

            def hsum(v, sel=sel):
                return jnp.dot(v.astype(BF16), sel, preferred_element_type=F32)

            dyp = dy_all[:, pi * 128:(pi + 1) * 128]
            x = xs_ref[:, pi * 128:(pi + 1) * 128].astype(F32)
            dtp = _pair_bc(dt, lo, pi)
            xdt = x * dtp
            dxdt = jnp.zeros((CHUNK, 128), F32)
            for r, mask in ((2 * pi, lo), (2 * pi + 1, jnp.logical_not(lo))):
                lam = jnp.exp(jnp.where(tri, acs[:, r:r + 1] - acst[r:r + 1, :], -1e30))
                m32 = cb * lam
                m = m32.astype(BF16)
                dyr = jnp.where(mask, dyp, 0.0).astype(BF16)
                xr = jnp.where(mask, xdt, 0.0).astype(BF16)
                dm = lax.dot_general(dyr, xr, _DIMS["nt"], preferred_element_type=F32)
                dcb = dcb + dm * lam
                q = dm * m32
                dal = dal + jnp.sum(q, axis=1, keepdims=True) * (lane == r).astype(F32)
                qcol = qcol + jnp.where(sub8 == r, jnp.sum(q, axis=0, keepdims=True), 0.0)
                dxdt = dxdt + lax.dot_general(m, dyr, _DIMS["tn"], preferred_element_type=F32)
            ht = h_ref[0, 0, pi]
            htb = ht.astype(BF16)
            ecp = _pair_bc(ecs, lo, pi)
            yoff = jnp.dot(cmat, htb, preferred_element_type=F32) * ecp
            dg = (dyp * ecp).astype(BF16)
            dc_acc = dc_acc + lax.dot_general(dg, htb, _DIMS["nt"], preferred_element_type=F32)
            dht = lax.dot_general(cmat, dg, _DIMS["tn"], preferred_element_type=F32)
            dal = dal + hsum(dyp * yoff)
            dhn = dst_ref[pi]
            dhnb = dhn.astype(BF16)
            alp = _pair_bc(alast, lo, pi)
            e = jnp.exp(alp - _pair_bc(acs, lo, pi))
            xe = xdt * e
            db_acc = db_acc + lax.dot_general(xe.astype(BF16), dhnb, _DIMS["nt"], preferred_element_type=F32)
            dxe = jnp.dot(bmat, dhnb, preferred_element_type=F32)
            dxdt = dxdt + dxe * e
            tt = hsum(dxe * xe)
            dal = dal - tt
            dec = jnp.exp(alp)
            dalast = dalast + jnp.sum(tt, axis=0, keepdims=True) + hsum(
                jnp.broadcast_to(jnp.sum(dhn * ht, axis=0, keepdims=True) * dec, (8, 128)))
            dst_ref[pi] = dht + dhn * dec
            dxs_ref[:, pi * 128:(pi + 1) * 128] = (_pair_bc(dskip, lo, pi) * dyp + dxdt * dtp).astype(dxs_ref.dtype)
            ddtm = ddtm + hsum(dxdt * x)
            ddsk = ddsk + hsum(jnp.broadcast_to(jnp.sum(dyp * x, axis=0, keepdims=True), (8, 128)))
        rowi = lax.broadcasted_iota(jnp.int32, (CHUNK, 128), 0)
        qcol_w = lax.dot_general(jnp.concatenate([qcol, jnp.zeros((CHUNK - 8, CHUNK), F32)], axis=0), (sel_r == sel_c).astype(F32),
                                 _DIMS["tn"], precision=HI, preferred_element_type=F32)
        dal = dal - qcol_w + jnp.where(rowi == CHUNK - 1, dalast[0:1, :], 0.0)
        dda = jnp.dot(upper, dal, precision=HI, preferred_element_type=F32)
        ddt_ref[0] = ddtm + dda * a
        hsum_ref[0, 1:2, :] += jnp.sum(dda * dt, axis=0, keepdims=True) * a
        hsum_ref[0, 2:3, :] += ddsk[0:1, :]
        dcbb = dcb.astype(BF16)
        dc_ref[...] = (jnp.dot(dcbb, bmat, preferred_element_type=F32) + dc_acc).astype(dc_ref.dtype)
        db_ref[...] = (lax.dot_general(dcbb, cmat, _DIMS["tn"], preferred_element_type=F32) + db_acc).astype(db_ref.dtype)

    ins = ["xs", "bm", "cmat", "dtw", "dtt", "wide", "tall", "wide", "grp", "vec", "grp", "grp", "state"]
    col = lambda: pl.BlockSpec((CHUNK, SSD_STATE), lambda g, c: (rev(c), g))
    return pl.pallas_call(
        body, name=name, grid=(SSD_GROUPS, nc),
        in_specs=[sp[k](rev) for k in ins],
        out_specs=[sp["grp"](rev), col(), col(), sp["grp"](rev), sp["dtw"](rev),
                   pl.BlockSpec((1, 8, 128), lambda g, c: (g, 0, 0)), sp["vec"](rev)],
        out_shape=[jax.ShapeDtypeStruct((t, SSD_INNER), BF16), jax.ShapeDtypeStruct((t, SSD_GROUPS * SSD_STATE), BF16),
                   jax.ShapeDtypeStruct((t, SSD_GROUPS * SSD_STATE), BF16), jax.ShapeDtypeStruct((t, SSD_INNER), BF16),
                   jax.ShapeDtypeStruct((SSD_GROUPS, t, 128), F32), jax.ShapeDtypeStruct((SSD_GROUPS, 8, 128), F32),
                   jax.ShapeDtypeStruct((1, SSD_INNER), F32)],
        scratch_shapes=[pltpu.VMEM((PAIRS, SSD_STATE, 128), F32)], compiler_params=_cparams("parallel", "arbitrary"),
    )(xc, xc, xc, hv["dtw"], hv["dtt"], hv["alog_w"], hv["alog_t"], hv["dskip_w"], p, norm_g, y, dys, states)


def _wide(v):
    return jnp.pad(v.reshape(SSD_GROUPS, 1, HEADS_PER_GROUP), ((0, 0), (0, 0), (0, 128 - HEADS_PER_GROUP)))


def _head_inputs(dt, a_log, d_skip):
    t = dt.shape[0]
    g = dt[:, :SSD_HEADS].reshape(t, SSD_GROUPS, HEADS_PER_GROUP)
    return dict(
        dtw=jnp.pad(jnp.transpose(g, (1, 0, 2)), ((0, 0), (0, 0), (0, 128 - HEADS_PER_GROUP))),
        dtt=jnp.transpose(g, (1, 2, 0)),
        alog_w=_wide(a_log), alog_t=a_log.reshape(SSD_GROUPS, HEADS_PER_GROUP, 1),
        dskip_w=_wide(d_skip),
    )


def _gelu(x):
    return 0.5 * x * (1.0 + lax.erf(x * (1.0 / math.sqrt(2.0))))


def _gelu_grad(x):
    return 0.5 * (1.0 + lax.erf(x * (1.0 / math.sqrt(2.0)))) + x * jnp.exp(-0.5 * x * x) * (1.0 / math.sqrt(2.0 * math.pi))


def _tril_mask():
    r = lax.broadcasted_iota(jnp.int32, (CHUNK, CHUNK), 0)
    c = lax.broadcasted_iota(jnp.int32, (CHUNK, CHUNK), 1)
    return r >= c


def _gmlp_fwd(p, v_gain, w_s, b_col, name):
    t = p.shape[0]
    tm = _row_tile(t)
    u0 = P_UV // GMLP_W

    def body(u_ref, v_ref, gn_ref, ws_ref, bs_ref, o_ref):
        v = _gelu(v_ref[...].astype(F32))
        v = (v * lax.rsqrt(jnp.mean(v * v, axis=-1, keepdims=True) + EPS) * gn_ref[...]).astype(BF16)
        tril = _tril_mask()
        wm = [jnp.where(tril, ws_ref[g], 0.0).astype(BF16) for g in range(GMLP_GROUPS)]
        for k in range(tm // CHUNK):
            rows = slice(k * CHUNK, (k + 1) * CHUNK)
            for g in range(GMLP_GROUPS):
                cols = slice(g * 128, (g + 1) * 128)
                mixed = jnp.dot(wm[g], v[rows, cols], preferred_element_type=F32) + bs_ref[g]
                o_ref[rows, cols] = (_gelu(u_ref[rows, cols].astype(F32)) * mixed).astype(o_ref.dtype)

    return pl.pallas_call(
        body, name=name, grid=(t // tm,),
        in_specs=[pl.BlockSpec((tm, GMLP_W), lambda i: (i, u0)), pl.BlockSpec((tm, GMLP_W), lambda i: (i, u0 + 1)),
                  pl.BlockSpec((1, GMLP_W), lambda i: (0, 0)), pl.BlockSpec((GMLP_GROUPS, CHUNK, CHUNK), lambda i: (0, 0, 0)),
                  pl.BlockSpec((GMLP_GROUPS, CHUNK, 1), lambda i: (0, 0, 0))],
        out_specs=pl.BlockSpec((tm, GMLP_W), lambda i: (i, 0)),
        out_shape=jax.ShapeDtypeStruct((t, GMLP_W), BF16), compiler_params=_cparams("parallel"),
    )(p, p, v_gain, w_s, b_col)


def _gmlp_bwd(p, dy, v_gain, w_s, b_col, name):
    t = p.shape[0]
    tm = _row_tile(t)
    u0 = P_UV // GMLP_W

    def body(u_ref, v_ref, dy_ref, gn_ref, ws_ref, bs_ref, duv_ref, dws_ref, dbs_ref, dgn_ref, dvn_ref):
        @pl.when(pl.program_id(0) == 0)
        def _():
            dws_ref[...] = jnp.zeros_like(dws_ref)
            dbs_ref[...] = jnp.zeros_like(dbs_ref)
            dgn_ref[...] = jnp.zeros_like(dgn_ref)

        vraw = v_ref[...].astype(F32)
        va = _gelu(vraw)
        rstd = lax.rsqrt(jnp.mean(va * va, axis=-1, keepdims=True) + EPS)
        vhat = va * rstd
        gain = gn_ref[...]
        vn = (vhat * gain).astype(BF16)
        tril = _tril_mask()
        wm = [jnp.where(tril, ws_ref[g], 0.0).astype(BF16) for g in range(GMLP_GROUPS)]
        for k in range(tm // CHUNK):
            rows = slice(k * CHUNK, (k + 1) * CHUNK)
            for g in range(GMLP_GROUPS):
                cols = slice(g * 128, (g + 1) * 128)
                uraw = u_ref[rows, cols].astype(F32)
                vb = vn[rows, cols]
                mixed = jnp.dot(wm[g], vb, preferred_element_type=F32) + bs_ref[g]
                dyb = dy_ref[rows, cols].astype(F32)
                duv_ref[rows, cols] = (dyb * mixed * _gelu_grad(uraw)).astype(duv_ref.dtype)
                dmix = dyb * _gelu(uraw)
                dmb = dmix.astype(BF16)
                dws_ref[g] += jnp.where(tril, lax.dot_general(dmb, vb, _DIMS["nt"], preferred_element_type=F32), 0.0)
                dbs_ref[g] += jnp.sum(dmix, axis=1, keepdims=True)
                dvn_ref[rows, cols] = lax.dot_general(wm[g], dmb, _DIMS["tn"], preferred_element_type=F32)
        dvn = dvn_ref[...]
        dgn_ref[...] += jnp.sum(dvn * vhat, axis=0, keepdims=True)
        dvh = dvn * gain
        dva = rstd * (dvh - vhat * jnp.mean(dvh * vhat, axis=-1, keepdims=True))
        duv_ref[:, GMLP_W:2 * GMLP_W] = (dva * _gelu_grad(vraw)).astype(duv_ref.dtype)

    return pl.pallas_call(
        body, name=name, grid=(t // tm,),
        in_specs=[pl.BlockSpec((tm, GMLP_W), lambda i: (i, u0)), pl.BlockSpec((tm, GMLP_W), lambda i: (i, u0 + 1)),
                  pl.BlockSpec((tm, GMLP_W), lambda i: (i, 0)),
                  pl.BlockSpec((1, GMLP_W), lambda i: (0, 0)), pl.BlockSpec((GMLP_GROUPS, CHUNK, CHUNK), lambda i: (0, 0, 0)),
                  pl.BlockSpec((GMLP_GROUPS, CHUNK, 1), lambda i: (0, 0, 0))],
        out_specs=[pl.BlockSpec((tm, 2 * GMLP_W), lambda i: (i, 0)), pl.BlockSpec((GMLP_GROUPS, CHUNK, CHUNK), lambda i: (0, 0, 0)),
                   pl.BlockSpec((GMLP_GROUPS, CHUNK, 1), lambda i: (0, 0, 0)), pl.BlockSpec((1, GMLP_W), lambda i: (0, 0))],
        out_shape=[jax.ShapeDtypeStruct((t, 2 * GMLP_W), BF16), jax.ShapeDtypeStruct((GMLP_GROUPS, CHUNK, CHUNK), F32),
                   jax.ShapeDtypeStruct((GMLP_GROUPS, CHUNK, 1), F32), jax.ShapeDtypeStruct((1, GMLP_W), F32)],
        scratch_shapes=[pltpu.VMEM((tm, GMLP_W), F32)], compiler_params=_cparams("arbitrary"),
    )(p, p, dy, v_gain, w_s, b_col)


def _head_masks():
    lane = lax.broadcasted_iota(jnp.int32, (1, MEM_W), 1)
    return [(lane >= h * 64) & (lane < (h + 1) * 64) for h in range(MEM_HEADS)]


def _mem_fwd(p, kv, name):
    t = p.shape[0]
    tm = _row_tile(t)
    q0 = P_Q // MEM_W

    def body(q_ref, kv_ref, o_ref):
        q = q_ref[...]
        k = kv_ref[:, 0:MEM_W].astype(BF16)
        v = kv_ref[:, MEM_W:2 * MEM_W].astype(BF16)
        out = jnp.zeros((tm, MEM_W), F32)
        for mask in _head_masks():
            s = lax.dot_general(jnp.where(mask, q, 0), k, _DIMS["nt"], preferred_element_type=F32) * 0.125
            e = jnp.exp(s - jnp.max(s, axis=-1, keepdims=True))
            pr = (e / jnp.sum(e, axis=-1, keepdims=True)).astype(BF16)
            out = out + jnp.where(mask, jnp.dot(pr, v, preferred_element_type=F32), 0.0)
        o_ref[...] = out.astype(o_ref.dtype)

    return pl.pallas_call(
        body, name=name, grid=(t // tm,),
        in_specs=[pl.BlockSpec((tm, MEM_W), lambda i: (i, q0)), pl.BlockSpec((MEM_LEN, 2 * MEM_W), lambda i: (0, 0))],
        out_specs=pl.BlockSpec((tm, MEM_W), lambda i: (i, 0)),
        out_shape=jax.ShapeDtypeStruct((t, MEM_W), BF16), compiler_params=_cparams("parallel"),
    )(p, kv)


def _mem_bwd(p, kv, dy, name):
    t = p.shape[0]
    tm = _row_tile(t)
    q0 = P_Q // MEM_W

    def body(q_ref, kv_ref, dy_ref, dq_ref, dkv_ref):
        @pl.when(pl.program_id(0) == 0)
        def _():
            dkv_ref[...] = jnp.zeros_like(dkv_ref)

        q = q_ref[...]
        dy = dy_ref[...]
        k = kv_ref[:, 0:MEM_W].astype(BF16)
        v = kv_ref[:, MEM_W:2 * MEM_W].astype(BF16)
        dq = jnp.zeros((tm, MEM_W), F32)
        dk = jnp.zeros((MEM_LEN, MEM_W), F32)
        dv = jnp.zeros((MEM_LEN, MEM_W), F32)
        for mask in _head_masks():
            qh = jnp.where(mask, q, 0)
            dyh = jnp.where(mask, dy, 0)
            s = lax.dot_general(qh, k, _DIMS["nt"], preferred_element_type=F32) * 0.125
            e = jnp.exp(s - jnp.max(s, axis=-1, keepdims=True))
            pr = e / jnp.sum(e, axis=-1, keepdims=True)
            prb = pr.astype(BF16)
            dp = lax.dot_general(dyh, v, _DIMS["nt"], preferred_element_type=F32)
            ds = (pr * (dp - jnp.sum(dp * pr, axis=-1, keepdims=True)) * 0.125).astype(BF16)
            dq = dq + jnp.where(mask, jnp.dot(ds, k, preferred_element_type=F32), 0.0)
            dk = dk + lax.dot_general(ds, qh, _DIMS["tn"], preferred_element_type=F32)
            dv = dv + lax.dot_general(prb, dyh, _DIMS["tn"], preferred_element_type=F32)
        dq_ref[...] = dq.astype(dq_ref.dtype)
        dkv_ref[:, 0:MEM_W] += dk
        dkv_ref[:, MEM_W:2 * MEM_W] += dv

    return pl.pallas_call(
        body, name=name, grid=(t // tm,),
        in_specs=[pl.BlockSpec((tm, MEM_W), lambda i: (i, q0)), pl.BlockSpec((MEM_LEN, 2 * MEM_W), lambda i: (0, 0)),
                  pl.BlockSpec((tm, MEM_W), lambda i: (i, 0))],
        out_specs=[pl.BlockSpec((tm, MEM_W), lambda i: (i, 0)), pl.BlockSpec((MEM_LEN, 2 * MEM_W), lambda i: (0, 0))],
        out_shape=[jax.ShapeDtypeStruct((t, MEM_W), BF16), jax.ShapeDtypeStruct((MEM_LEN, 2 * MEM_W), F32)],
        compiler_params=_cparams("arbitrary"),
    )(p, kv, dy)


def _merge_fwd(p, b_ssd, b_gmlp, b_mem, name):
    t = p.shape[0]
    tm = _row_tile(t)
    g0 = P_GL // D_MODEL

    def body(g1, g2, g3, b1, b2, b3, o_ref):
        def strip(rows, carry):
            acc = _sigmoid(g1[rows, :].astype(F32)) * b1[rows, :].astype(F32)
            acc = acc + _sigmoid(g2[rows, :].astype(F32)) * b2[rows, :].astype(F32)
            acc = acc + _sigmoid(g3[rows, :].astype(F32)) * b3[rows, :].astype(F32)
            o_ref[rows, :] = acc.astype(o_ref.dtype)
            return carry

        _strips(tm, strip, 0)

    row = pl.BlockSpec((tm, D_MODEL), lambda i: (i, 0))
    return pl.pallas_call(
        body, name=name, grid=(t // tm,),
        in_specs=[pl.BlockSpec((tm, D_MODEL), lambda i, k=k: (i, g0 + k)) for k in range(3)] + [row] * 3,
        out_specs=row, out_shape=jax.ShapeDtypeStruct((t, D_MODEL), BF16), compiler_params=_cparams("parallel"),
    )(p, p, p, b_ssd, b_gmlp, b_mem)


def _merge_bwd(p, dm, b_ssd, b_gmlp, b_mem, name):
    t = p.shape[0]
    tm = _row_tile(t)
    g0 = P_GL // D_MODEL

    def body(g1, g2, g3, dm_ref, b1, b2, b3, d1, d2, d3, dgl_ref):
        def strip(rows, carry):
            dmv = dm_ref[rows, :].astype(F32)
            for k, (g_ref, b_ref, d_ref) in enumerate(((g1, b1, d1), (g2, b2, d2), (g3, b3, d3))):
                sg = _sigmoid(g_ref[rows, :].astype(F32))
                d_ref[rows, :] = (dmv * sg).astype(d_ref.dtype)
                dgl_ref[rows, k * D_MODEL:(k + 1) * D_MODEL] = (dmv * b_ref[rows, :].astype(F32) * sg * (1.0 - sg)).astype(dgl_ref.dtype)
            return carry

        _strips(tm, strip, 0)

    row = pl.BlockSpec((tm, D_MODEL), lambda i: (i, 0))
    return pl.pallas_call(
        body, name=name, grid=(t // tm,),
        in_specs=[pl.BlockSpec((tm, D_MODEL), lambda i, k=k: (i, g0 + k)) for k in range(3)] + [row] * 4,
        out_specs=[row, row, row, pl.BlockSpec((tm, 3 * D_MODEL), lambda i: (i, 0))],
        out_shape=[jax.ShapeDtypeStruct((t, D_MODEL), BF16)] * 3 + [jax.ShapeDtypeStruct((t, 3 * D_MODEL), BF16)],
        compiler_params=_cparams("parallel"),
    )(p, p, p, dm, b_ssd, b_gmlp, b_mem)


def _local_step(x, mem, target, w):
    t = x.shape[0]
    mm = functools.partial(_matmul, tk=1024)

    h1, ffn1_saved = _ffn_forward(x, w["ffn1_norm"], w["ffn1_w_gate"], w["ffn1_w_up"], w["ffn1_w_down"], "ffn1")
    n2 = _rms_fwd(h1, w["mix_norm"], "mix_norm")
    p = mm(n2, w["w_in_p"], mode="nn", out_dtype=BF16, tm=512, tn=1536, name="in_proj")
    dt_raw = mm(n2, w["w_dt"], mode="nn", out_dtype=F32, tm=512, tn=128, name="dt_proj")
    dt_bias = jnp.pad(w["ssd_dt_bias"], (0, 128 - SSD_HEADS)).reshape(1, 128)
    hv = _head_inputs(_dt_fwd(dt_raw, dt_bias, "dt_fwd"), w["ssd_a_log"], w["ssd_d"])
    xc = _conv_fwd(p, w["ssd_conv_w"], w["ssd_conv_b"], "conv_fwd")
    y_ssd_raw, y_ssd, states = _ssd_fwd(xc, p, hv, w["ssd_norm"], "ssd_fwd")
    b_col = w["gmlp_b_s"].reshape(GMLP_GROUPS, CHUNK, 1)
    y_gmlp = _gmlp_fwd(p, w["gmlp_v_norm"], w["gmlp_w_s"], b_col, "gmlp_fwd")
    mem_n = _rms_fwd(mem, w["mem_norm"], "mem_norm")
    kv = mm(mem_n, w["w_mem_kv"], mode="nn", out_dtype=F32, tm=256, tn=512, name="mem_kv")
    y_mem = _mem_fwd(p, kv, "mem_fwd")
    b_ssd = mm(y_ssd, w["w_branch_ssd"], mode="nn", out_dtype=BF16, tm=512, tn=1024, name="branch_ssd")
    b_gmlp = mm(y_gmlp, w["w_branch_gmlp"], mode="nn", out_dtype=BF16, tm=512, tn=1024, name="branch_gmlp")
    b_mem = mm(y_mem, w["w_branch_mem"], mode="nn", out_dtype=BF16, tm=512, tn=1024, name="branch_mem")
    merged = _merge_fwd(p, b_ssd, b_gmlp, b_mem, "merge_fwd")
    h2 = mm(merged, w["w_out"], mode="nn", out_dtype=F32, tm=512, tn=1024, addend=h1, name="out_proj")
    h3, ffn2_saved = _ffn_forward(h2, w["ffn2_norm"], w["ffn2_w_gate"], w["ffn2_w_up"], w["ffn2_w_down"], "ffn2")
    dh3, d_final, loss = _loss_head(h3, w["final_norm"], target, "loss_head")

    g = {"final_norm": d_final}
    dh2, gf = _ffn_backward(dh3, h2, w["ffn2_norm"], w["ffn2_w_gate"], w["ffn2_w_up"], w["ffn2_w_down"], ffn2_saved, "ffn2")
    g.update({f"ffn2_{k}": v for k, v in gf.items()})
    dmerged = mm(dh2, w["w_out"], mode="nt", out_dtype=BF16, tm=512, tn=1024, name="out_proj_dx")
    g["w_out"] = mm(merged, dh2, mode="tn", out_dtype=BF16, tm=1024, tn=1024, name="out_proj_dw")
    db_ssd, db_gmlp, db_mem, dgl = _merge_bwd(p, dmerged, b_ssd, b_gmlp, b_mem, "merge_bwd")
    dy_ssd = mm(db_ssd, w["w_branch_ssd"], mode="nt", out_dtype=BF16, tm=512, tn=2048, name="branch_ssd_dx")
    dy_gmlp = mm(db_gmlp, w["w_branch_gmlp"], mode="nt", out_dtype=BF16, tm=512, tn=1024, name="branch_gmlp_dx")
    dy_mem = mm(db_mem, w["w_branch_mem"], mode="nt", out_dtype=BF16, tm=512, tn=256, name="branch_mem_dx")
    g["w_branch_ssd"] = mm(y_ssd, db_ssd, mode="tn", out_dtype=BF16, tm=1024, tn=1024, name="branch_ssd_dw")
    g["w_branch_gmlp"] = mm(y_gmlp, db_gmlp, mode="tn", out_dtype=BF16, tm=1024, tn=1024, name="branch_gmlp_dw")
    g["w_branch_mem"] = mm(y_mem, db_mem, mode="tn", out_dtype=BF16, tm=256, tn=1024, name="branch_mem_dw")
    dq, dkv = _mem_bwd(p, kv, dy_mem, "mem_bwd")
    g["w_mem_kv"] = mm(mem_n, dkv, mode="tn", out_dtype=BF16, tm=1024, tn=512, tk=256, name="mem_kv_dw")
    dmem_n = mm(dkv, w["w_mem_kv"], mode="nt", out_dtype=F32, tm=256, tn=1024, tk=512, name="mem_kv_dx")
    _, g["mem_norm"] = _rms_bwd(mem, w["mem_norm"], dmem_n, None, "mem_norm_bwd")
    duv, d_ws, d_bs, g["gmlp_v_norm"] = _gmlp_bwd(p, dy_gmlp, w["gmlp_v_norm"], w["gmlp_w_s"], b_col, "gmlp_bwd")
    g["gmlp_w_s"] = d_ws
    g["gmlp_b_s"] = d_bs.reshape(GMLP_GROUPS, CHUNK)
    dxs, d_bm, d_cm, dz, ddt_w, hsums, g["ssd_norm"] = _ssd_bwd(xc, p, hv, w["ssd_norm"], y_ssd_raw, dy_ssd, states, "ssd_bwd")
    heads = hsums[:, :, :HEADS_PER_GROUP]
    g["ssd_a_log"] = heads[:, 1, :].reshape(1, SSD_HEADS)
    g["ssd_d"] = heads[:, 2, :].reshape(1, SSD_HEADS)
    ddt = jnp.transpose(ddt_w[:, :, :HEADS_PER_GROUP], (1, 0, 2)).reshape(t, SSD_HEADS)
    ddt, d_bias = _dt_bwd(jnp.pad(ddt, ((0, 0), (0, 128 - SSD_HEADS))), dt_raw, dt_bias, "dt_bwd")
    g["ssd_dt_bias"] = d_bias[:, :SSD_HEADS]
    parts, dws, dbs = [], [], []
    for dyc, col0, tag in ((dxs, 0, "x"), (d_bm, SSD_INNER, "b"), (d_cm, SSD_INNER + SSD_GROUPS * SSD_STATE, "c")):
        dacc, dw_c, db_c = _conv_bwd_act(p, dyc, w["ssd_conv_w"], w["ssd_conv_b"], col0, f"conv_bwd_act_{tag}")
        parts.append(_conv_bwd_dx(dacc, w["ssd_conv_w"], col0, f"conv_bwd_dx_{tag}"))
        dws.append(dw_c)
        dbs.append(db_c)
    g["ssd_conv_w"] = jnp.concatenate(dws, axis=1)
    g["ssd_conv_b"] = jnp.concatenate(dbs, axis=1)
    dp = jnp.concatenate([dz, duv] + parts + [dgl, dq, jnp.zeros((t, P_W - P_USED), BF16)], axis=1)
    dn2 = mm(dp, w["w_in_p"], mode="nt", out_dtype=F32, tm=512, tn=1024, tk=1536, name="in_proj_dx")
    dn2 = _matmul(ddt, w["w_dt"], mode="nt", out_dtype=F32, tm=512, tn=1024, tk=128, addend=dn2, name="dt_proj_dx")
    d_win_p = mm(n2, dp, mode="tn", out_dtype=BF16, tm=1024, tn=1536, name="in_proj_dw")
    d_wdt = mm(n2, ddt, mode="tn", out_dtype=BF16, tm=1024, tn=128, name="dt_proj_dw")
    sl = lambda a, o, n: a[:, o:o + n]
    g["w_in"] = jnp.concatenate([sl(d_win_p, P_Z, 2048), sl(d_win_p, P_XBC, XBC), d_wdt[:, :SSD_HEADS], sl(d_win_p, P_UV, 2048),
                                 sl(d_win_p, P_Q, MEM_W), sl(d_win_p, P_GL, 3 * D_MODEL)], axis=1)
    dh1, g["mix_norm"] = _rms_bwd(h1, w["mix_norm"], dn2, dh2, "mix_norm_bwd")
    dx, gf = _ffn_backward(dh1, x, w["ffn1_norm"], w["ffn1_w_gate"], w["ffn1_w_up"], w["ffn1_w_down"], ffn1_saved, "ffn1")
    g.update({f"ffn1_{k}": v for k, v in gf.items()})
    return loss, dx, g


def _split_w_in(w_in):
    sl = lambda o, n: w_in[:, o:o + n]
    w_p = jnp.concatenate([sl(IN_Z, 2048), sl(IN_UV, 2048), sl(IN_XBC, XBC), sl(IN_GL, 3 * D_MODEL), sl(IN_Q, MEM_W),
                           jnp.zeros((D_MODEL, P_W - P_USED), w_in.dtype)], axis=1)
    w_dt = jnp.pad(sl(IN_DT, SSD_HEADS), ((0, 0), (0, 128 - SSD_HEADS)))
    return w_p, w_dt


def _pick_tile(rows, cap=512):
    best = None
    for tile in range(8, min(rows, cap) + 1, 8):
        if rows % tile == 0:
            best = tile
    return best if best is not None else rows


def _adamw(w, g, m, v, name):
    rows, lanes = w.shape
    tile = _pick_tile(rows, cap=max(8, (512 * 1024 // lanes) // 8 * 8))
    c1 = 1.0 / (1.0 - ADAM_B1 ** ADAM_STEP)
    c2 = 1.0 / (1.0 - ADAM_B2 ** ADAM_STEP)

    def body(w_ref, g_ref, m_ref, v_ref, d_ref, nm_ref, nv_ref):
        gv = g_ref[...]
        nm = ADAM_B1 * m_ref[...] + (1.0 - ADAM_B1) * gv
        nv = ADAM_B2 * v_ref[...] + (1.0 - ADAM_B2) * (gv * gv)
        nm_ref[...] = nm
        nv_ref[...] = nv
        d_ref[...] = -ADAM_LR * ((nm * c1) / (jnp.sqrt(nv * c2) + ADAM_EPS) + ADAM_WD * w_ref[...])

    blk = pl.BlockSpec((tile, lanes), lambda i: (i, 0))
    return pl.pallas_call(
        body, name=name, grid=(rows // tile,), in_specs=[blk] * 4, out_specs=[blk] * 3,
        out_shape=[jax.ShapeDtypeStruct((rows, lanes), F32)] * 3, compiler_params=_cparams("parallel"),
    )(w, g, m, v)


HBM = pl.BlockSpec(memory_space=pltpu.HBM)


def _place():
    x, y, c = lax.axis_index("x"), lax.axis_index("y"), lax.axis_index("c")
    chips = [(1 - x, y), (x, 1 - y), (1 - x, 1 - y)]
    return x, y, c, chips


def _gather_weights(slots):
    _, _, rh, lanes = slots.shape

    def body(in_ref, out_ref, send_sems, recv_sems):
        del in_ref
        x, y, c, chips = _place()
        me, sibling = (x, y, c), (x, y, 1 - c)

        def copy(k, src, dst, to):
            return pltpu.make_async_remote_copy(src_ref=src, dst_ref=dst, send_sem=send_sems.at[k], recv_sem=recv_sems.at[k],
                                                device_id=to, device_id_type=MESH)

        own = out_ref.at[2 * x + y, c]
        first = [copy(j, own, own, (*chip, c)) for j, chip in enumerate(chips)]
        for cp in first:
            cp.start()
        passed = []
        for j, (cx, cy) in enumerate(chips):
            landed = out_ref.at[2 * cx + cy, c]
            copy(j, landed, landed, me).wait_recv()
            fwd = copy(3 + j, landed, landed, sibling)
            fwd.start()
            passed.append(fwd)
        for j, (cx, cy) in enumerate(chips):
            other = out_ref.at[2 * cx + cy, 1 - c]
            copy(3 + j, other, other, me).wait_recv()
        for cp in first + passed:
            cp.wait_send()

    return pl.pallas_call(
        body, name="gather_weights", out_shape=jax.ShapeDtypeStruct(slots.shape, slots.dtype),
        in_specs=[HBM], out_specs=HBM, input_output_aliases={0: 0},
        scratch_shapes=[pltpu.SemaphoreType.DMA((6,)), pltpu.SemaphoreType.DMA((6,))],
    )(slots)


def _rs_swap(gp):
    _, ns, rh, lanes = gp.shape

    def body(in_ref, out_ref, send_sem, recv_sem):
        x, y, c, _ = _place()
        cp = pltpu.make_async_remote_copy(src_ref=in_ref.at[1 - c], dst_ref=out_ref, send_sem=send_sem, recv_sem=recv_sem,
                                          device_id=(x, y, 1 - c), device_id_type=MESH)
        cp.start()
        cp.wait_send()
        cp.wait_recv()

    return pl.pallas_call(
        body, name="rs_swap", out_shape=jax.ShapeDtypeStruct((ns, rh, lanes), gp.dtype), in_specs=[HBM], out_specs=HBM,
        scratch_shapes=[pltpu.SemaphoreType.DMA, pltpu.SemaphoreType.DMA],
    )(gp)


def _rs_tile(rh):
    return _pick_tile(rh, cap=512)


def _rs_add(gp, recv, c):
    _, ns, rh, lanes = gp.shape
    tile = _rs_tile(rh)

    def body(c_ref, a_ref, b_ref, o_ref):
        o_ref[...] = (a_ref[...].astype(F32) + b_ref[...].astype(F32)).astype(o_ref.dtype)

    return pl.pallas_call(
        body, name="rs_add", out_shape=jax.ShapeDtypeStruct((ns, rh, lanes), gp.dtype),
        grid_spec=pltpu.PrefetchScalarGridSpec(
            num_scalar_prefetch=1, grid=(ns, rh // tile),
            in_specs=[pl.BlockSpec((None, None, tile, lanes), lambda s, i, c_ref: (c_ref[0], s, i, 0)),
                      pl.BlockSpec((None, tile, lanes), lambda s, i, c_ref: (s, i, 0))],
            out_specs=pl.BlockSpec((None, tile, lanes), lambda s, i, c_ref: (s, i, 0))),
        compiler_params=_cparams("parallel", "parallel"),
    )(c, gp, recv)


def _rs_scatter(pa):
    ns, rh, lanes = pa.shape

    def body(in_ref, out_ref, send_sems, recv_sems):
        x, y, c, chips = _place()
        cps = [pltpu.make_async_remote_copy(src_ref=in_ref.at[2 * cx + cy], dst_ref=out_ref.at[j], send_sem=send_sems.at[j],
                                            recv_sem=recv_sems.at[j], device_id=(cx, cy, c), device_id_type=MESH)
               for j, (cx, cy) in enumerate(chips)]
        for cp in cps:
            cp.start()
        for cp in cps:
            cp.wait_recv()
        for cp in cps:
            cp.wait_send()

    return pl.pallas_call(
        body, name="rs_scatter", out_shape=jax.ShapeDtypeStruct((ns - 1, rh, lanes), pa.dtype), in_specs=[HBM], out_specs=HBM,
        scratch_shapes=[pltpu.SemaphoreType.DMA((3,)), pltpu.SemaphoreType.DMA((3,))],
    )(pa)


def _rs_sum(pa, recv, place):
    ns, rh, lanes = pa.shape
    tile = _rs_tile(rh)

    def body(place_ref, a_ref, r_ref, o_ref):
        acc = a_ref[...].astype(F32)
        for j in range(ns - 1):
            acc = acc + r_ref[j].astype(F32)
        o_ref[...] = acc

    return pl.pallas_call(
        body, name="rs_sum", out_shape=jax.ShapeDtypeStruct((2, rh, lanes), F32),
        grid_spec=pltpu.PrefetchScalarGridSpec(
            num_scalar_prefetch=1, grid=(rh // tile,),
            in_specs=[pl.BlockSpec((None, tile, lanes), lambda i, place_ref: (place_ref[0], i, 0)),
                      pl.BlockSpec((ns - 1, tile, lanes), lambda i, place_ref: (0, i, 0))],
            out_specs=pl.BlockSpec((None, tile, lanes), lambda i, place_ref: (place_ref[1], i, 0))),
        compiler_params=_cparams("parallel"),
    )(place, pa, recv)


def _rs_share(halves):
    def body(in_ref, out_ref, send_sem, recv_sem):
        del in_ref
        x, y, c, _ = _place()
        cp = pltpu.make_async_remote_copy(src_ref=out_ref.at[c], dst_ref=out_ref.at[c], send_sem=send_sem, recv_sem=recv_sem,
                                          device_id=(x, y, 1 - c), device_id_type=MESH)
        cp.start()
        other = out_ref.at[1 - c]
        pltpu.make_async_remote_copy(src_ref=other, dst_ref=other, send_sem=send_sem, recv_sem=recv_sem,
                                     device_id=(x, y, 1 - c), device_id_type=MESH).wait_recv()
        cp.wait_send()

    return pl.pallas_call(
        body, name="rs_share", out_shape=jax.ShapeDtypeStruct(halves.shape, halves.dtype), in_specs=[HBM], out_specs=HBM,
        input_output_aliases={0: 0}, scratch_shapes=[pltpu.SemaphoreType.DMA, pltpu.SemaphoreType.DMA],
    )(halves)


N_DEV = 8
SMALL_ROWS = 160


def _allreduce_small(v):
    m_per, n = v.shape

    def body(x_ref, out_ref, all_ref, send_sems, recv_sems, local_sem):
        x, y, c, chips = _place()
        me, sibling = (x, y, c), (x, y, 1 - c)

        def rows(px, py, pc):
            return all_ref.at[pl.ds((4 * px + 2 * py + pc) * m_per, m_per), :]

        def copy(k, block, to, src=None):
            return pltpu.make_async_remote_copy(src_ref=rows(*block) if src is None else src, dst_ref=rows(*block),
                                                send_sem=send_sems.at[k], recv_sem=recv_sems.at[k], device_id=to, device_id_type=MESH)

        mine = pltpu.make_async_copy(x_ref, rows(*me), local_sem)
        mine.start()
        first = [copy(0, me, sibling, src=x_ref)]
        first += [copy(1 + j, me, (*chip, c), src=x_ref) for j, chip in enumerate(chips)]
        for cp in first:
            cp.start()
        passed = [copy(4 + j, (*chip, c), sibling) for j, chip in enumerate(chips)]
        for j, chip in enumerate(chips):
            copy(1 + j, (*chip, c), me).wait_recv()
            passed[j].start()
        copy(0, sibling, me).wait_recv()
        for j, chip in enumerate(chips):
            copy(4 + j, (*chip, 1 - c), me).wait_recv()
        for cp in first + passed:
            cp.wait_send()
        mine.wait()
        step = 32
        for r in range(0, m_per, step):
            acc = all_ref[r:r + step, :]
            for d in range(1, N_DEV):
                acc = acc + all_ref[d * m_per + r:d * m_per + r + step, :]
            out_ref[r:r + step, :] = acc

    vm = pl.BlockSpec(memory_space=pltpu.VMEM)
    return pl.pallas_call(
        body, name="allreduce_small", out_shape=jax.ShapeDtypeStruct((m_per, n), v.dtype), in_specs=[vm], out_specs=vm,
        scratch_shapes=[pltpu.VMEM((N_DEV * m_per, n), v.dtype), pltpu.SemaphoreType.DMA((7,)), pltpu.SemaphoreType.DMA((7,)),
                        pltpu.SemaphoreType.DMA],
        compiler_params=pltpu.CompilerParams(vmem_limit_bytes=V7X_VMEM_LIMIT),
    )(v)


BIG = [("ffn1_w_gate", (D_MODEL, D_FF), 1), ("ffn1_w_up", (D_MODEL, D_FF), 1), ("ffn1_w_down", (D_FF, D_MODEL), 0),
       ("ffn2_w_gate", (D_MODEL, D_FF), 1), ("ffn2_w_up", (D_MODEL, D_FF), 1), ("ffn2_w_down", (D_FF, D_MODEL), 0),
       ("w_in", (D_MODEL, IN_WIDTH), 1), ("w_mem_kv", (D_MODEL, 2 * MEM_W), 0), ("w_branch_ssd", (SSD_INNER, D_MODEL), 0),
       ("w_branch_gmlp", (GMLP_W, D_MODEL), 0), ("w_branch_mem", (MEM_W, D_MODEL), 1), ("w_out", (D_MODEL, D_MODEL), 0)]
BIG_ROWS = [shape[0] * shape[1] // N_SHARD // LANES for _, shape, _ in BIG]
CONV_W_ROWS = 8
PACK_ROWS = 8096
assert sum(BIG_ROWS) + CONV_W_ROWS <= PACK_ROWS and (PACK_ROWS // 2) % 16 == 0

SMALL = [("ffn1_norm", 1), ("mix_norm", 1), ("mem_norm", 1), ("ssd_conv_b", 3), ("heads", 1), ("ssd_norm", 2),
         ("gmlp_v_norm", 1), ("gmlp_w_s", 128), ("gmlp_b_s", 1), ("ffn2_norm", 1), ("final_norm", 1), ("ssd_conv_w", 12)]
assert sum(n for _, n in SMALL) <= SMALL_ROWS
HEAD_VECS = ("ssd_dt_bias", "ssd_a_log", "ssd_d")


def _pack_small(vals, loss=None):
    parts = []
    for name, nrows in SMALL:
        if name == "heads":
            row = jnp.concatenate([vals[k].reshape(-1) for k in HEAD_VECS]
                                  + [jnp.zeros((1,), F32) if loss is None else loss.reshape(1)])
            parts.append(jnp.pad(row, (0, LANES - row.shape[0])).reshape(1, LANES))
        elif name in vals:
            parts.append(vals[name].reshape(nrows, LANES))
        else:
            parts.append(jnp.zeros((nrows, LANES), F32))
    buf = jnp.concatenate(parts, axis=0)
    return jnp.pad(buf, ((0, SMALL_ROWS - buf.shape[0]), (0, 0)))


def _unpack_small(buf):
    out, r = {}, 0
    for name, nrows in SMALL:
        blk = buf[r:r + nrows]
        r += nrows
        if name == "heads":
            for i, k in enumerate(HEAD_VECS):
                out[k] = blk[0, i * SSD_HEADS:(i + 1) * SSD_HEADS]
            out["loss"] = blk[0, 3 * SSD_HEADS]
        else:
            out[name] = blk
    return out


def _shard_rows(full, axis):
    a, b = full.shape
    if axis == 1:
        full = jnp.transpose(full.reshape(a, N_SHARD, b // N_SHARD), (1, 0, 2))
    return full.reshape(N_SHARD, a * b // N_SHARD // LANES, LANES)


def _unshard_rows(rows, shape, axis):
    a, b = shape
    if axis == 1:
        return jnp.transpose(rows.reshape(N_SHARD, a, b // N_SHARD), (1, 0, 2)).reshape(a, b)
    return rows.reshape(a, b)


def kernel(x, mem, ffn1_norm, ffn1_w_gate, ffn1_w_up, ffn1_w_down, mix_norm, mem_norm, w_in, ssd_conv_w, ssd_conv_b, ssd_dt_bias, ssd_a_log, ssd_d, ssd_norm, gmlp_v_norm, gmlp_w_s, gmlp_b_s, w_mem_kv, w_branch_ssd, w_branch_gmlp, w_branch_mem, w_out, ffn2_norm, ffn2_w_gate, ffn2_w_up, ffn2_w_down, final_norm, loss_target, m_ffn1_norm, m_ffn1_w_gate, m_ffn1_w_up, m_ffn1_w_down, m_mix_norm, m_mem_norm, m_w_in, m_ssd_conv_w, m_ssd_conv_b, m_ssd_dt_bias, m_ssd_a_log, m_ssd_d, m_ssd_norm, m_gmlp_v_norm, m_gmlp_w_s, m_gmlp_b_s, m_w_mem_kv, m_w_branch_ssd, m_w_branch_gmlp, m_w_branch_mem, m_w_out, m_ffn2_norm, m_ffn2_w_gate, m_ffn2_w_up, m_ffn2_w_down, m_final_norm, v_ffn1_norm, v_ffn1_w_gate, v_ffn1_w_up, v_ffn1_w_down, v_mix_norm, v_mem_norm, v_w_in, v_ssd_conv_w, v_ssd_conv_b, v_ssd_dt_bias, v_ssd_a_log, v_ssd_d, v_ssd_norm, v_gmlp_v_norm, v_gmlp_w_s, v_gmlp_b_s, v_w_mem_kv, v_w_branch_ssd, v_w_branch_gmlp, v_w_branch_mem, v_w_out, v_ffn2_norm, v_ffn2_w_gate, v_ffn2_w_up, v_ffn2_w_down, v_final_norm):
    given = dict(x=x, mem=mem, ffn1_norm=ffn1_norm, ffn1_w_gate=ffn1_w_gate, ffn1_w_up=ffn1_w_up, ffn1_w_down=ffn1_w_down, mix_norm=mix_norm, mem_norm=mem_norm, w_in=w_in, ssd_conv_w=ssd_conv_w, ssd_conv_b=ssd_conv_b, ssd_dt_bias=ssd_dt_bias, ssd_a_log=ssd_a_log, ssd_d=ssd_d, ssd_norm=ssd_norm, gmlp_v_norm=gmlp_v_norm, gmlp_w_s=gmlp_w_s, gmlp_b_s=gmlp_b_s, w_mem_kv=w_mem_kv, w_branch_ssd=w_branch_ssd, w_branch_gmlp=w_branch_gmlp, w_branch_mem=w_branch_mem, w_out=w_out, ffn2_norm=ffn2_norm, ffn2_w_gate=ffn2_w_gate, ffn2_w_up=ffn2_w_up, ffn2_w_down=ffn2_w_down, final_norm=final_norm, loss_target=loss_target, m_ffn1_norm=m_ffn1_norm, m_ffn1_w_gate=m_ffn1_w_gate, m_ffn1_w_up=m_ffn1_w_up, m_ffn1_w_down=m_ffn1_w_down, m_mix_norm=m_mix_norm, m_mem_norm=m_mem_norm, m_w_in=m_w_in, m_ssd_conv_w=m_ssd_conv_w, m_ssd_conv_b=m_ssd_conv_b, m_ssd_dt_bias=m_ssd_dt_bias, m_ssd_a_log=m_ssd_a_log, m_ssd_d=m_ssd_d, m_ssd_norm=m_ssd_norm, m_gmlp_v_norm=m_gmlp_v_norm, m_gmlp_w_s=m_gmlp_w_s, m_gmlp_b_s=m_gmlp_b_s, m_w_mem_kv=m_w_mem_kv, m_w_branch_ssd=m_w_branch_ssd, m_w_branch_gmlp=m_w_branch_gmlp, m_w_branch_mem=m_w_branch_mem, m_w_out=m_w_out, m_ffn2_norm=m_ffn2_norm, m_ffn2_w_gate=m_ffn2_w_gate, m_ffn2_w_up=m_ffn2_w_up, m_ffn2_w_down=m_ffn2_w_down, m_final_norm=m_final_norm, v_ffn1_norm=v_ffn1_norm, v_ffn1_w_gate=v_ffn1_w_gate, v_ffn1_w_up=v_ffn1_w_up, v_ffn1_w_down=v_ffn1_w_down, v_mix_norm=v_mix_norm, v_mem_norm=v_mem_norm, v_w_in=v_w_in, v_ssd_conv_w=v_ssd_conv_w, v_ssd_conv_b=v_ssd_conv_b, v_ssd_dt_bias=v_ssd_dt_bias, v_ssd_a_log=v_ssd_a_log, v_ssd_d=v_ssd_d, v_ssd_norm=v_ssd_norm, v_gmlp_v_norm=v_gmlp_v_norm, v_gmlp_w_s=v_gmlp_w_s, v_gmlp_b_s=v_gmlp_b_s, v_w_mem_kv=v_w_mem_kv, v_w_branch_ssd=v_w_branch_ssd, v_w_branch_gmlp=v_w_branch_gmlp, v_w_branch_mem=v_w_branch_mem, v_w_out=v_w_out, v_ffn2_norm=v_ffn2_norm, v_ffn2_w_gate=v_ffn2_w_gate, v_ffn2_w_up=v_ffn2_w_up, v_ffn2_w_down=v_ffn2_w_down, v_final_norm=v_final_norm)
    weights = [n for n in given if n not in ("x", "mem", "loss_target") and not n.startswith(("m_", "v_"))]
    xi, yi, ci = lax.axis_index("x"), lax.axis_index("y"), lax.axis_index("c")
    chip = (2 * xi + yi).astype(jnp.int32)
    rh = PACK_ROWS // 2

    local_rows = [given[name].astype(BF16).reshape(rows, LANES) for (name, _, _), rows in zip(BIG, BIG_ROWS)]
    conv_pairs = lax.bitcast_convert_type(given["ssd_conv_w"], BF16).reshape(-1)
    local_rows.append(jnp.pad(conv_pairs, (0, CONV_W_ROWS * LANES - conv_pairs.shape[0])).reshape(CONV_W_ROWS, LANES))
    packed = jnp.concatenate(local_rows, axis=0)
    packed = jnp.pad(packed, ((0, PACK_ROWS - packed.shape[0]), (0, 0))).reshape(1, 2, rh, LANES)
    gathered = _gather_weights(jnp.broadcast_to(packed, (N_SHARD, 2, rh, LANES))).reshape(N_SHARD, PACK_ROWS, LANES)
    w, r = {}, 0
    for (name, shape, axis), rows in zip(BIG, BIG_ROWS):
        w[name] = _unshard_rows(gathered[:, r:r + rows], shape, axis)
        r += rows
    conv_cols = XBC // N_SHARD
    conv_w = lax.bitcast_convert_type(gathered[:, r:r + CONV_W_ROWS].reshape(N_SHARD, -1)[:, :SSD_CONV * conv_cols * 2]
                                      .reshape(N_SHARD, SSD_CONV, conv_cols, 2), F32)
    w["ssd_conv_w"] = jnp.transpose(conv_w, (1, 0, 2)).reshape(SSD_CONV, XBC)
    w["w_in_p"], w["w_dt"] = _split_w_in(w.pop("w_in"))
    for name in ("ffn1_norm", "mix_norm", "mem_norm", "ssd_conv_b", "ssd_norm", "gmlp_v_norm", "ffn2_norm", "final_norm"):
        w[name] = given[name].reshape(1, -1)
    for name in HEAD_VECS:
        w[name] = given[name].reshape(-1)
    w["gmlp_w_s"] = given["gmlp_w_s"][0]
    w["gmlp_b_s"] = given["gmlp_b_s"][0]

    loss_part, grad_x, g = _local_step(x[0], mem[0], loss_target[0], w)

    gp = jnp.concatenate([_shard_rows(g[name], axis).astype(BF16) for name, _, axis in BIG], axis=1)
    gp = jnp.pad(gp, ((0, 0), (0, PACK_ROWS - gp.shape[1]), (0, 0))).reshape(N_SHARD, 2, rh, LANES)
    gp = jnp.transpose(gp, (1, 0, 2, 3))
    recv_a = _rs_swap(gp)
    pa = _rs_add(gp, recv_a, ci.astype(jnp.int32).reshape(1))
    recv_b = _rs_scatter(pa)
    gsum = _rs_sum(pa, recv_b, jnp.stack([chip, ci.astype(jnp.int32)]))
    gred = _rs_share(gsum).reshape(PACK_ROWS, LANES)

    small_vals = {k: g[k] for k, _ in SMALL if k != "heads"}
    small_vals.update({k: g[k] for k in HEAD_VECS})
    red = _unpack_small(_allreduce_small(_pack_small(small_vals, loss=loss_part[0, 0])))

    grads, deltas, new_m, new_v = {}, {}, {}, {}
    r = 0
    for (name, _, _), rows in zip(BIG, BIG_ROWS):
        gl = gred[r:r + rows].reshape(given[name].shape[1:])
        r += rows
        d, nm, nv = _adamw(given[name][0], gl, given["m_" + name][0], given["v_" + name][0], f"adamw_{name}")
        grads[name], deltas[name], new_m[name], new_v[name] = (a[None] for a in (gl, d, nm, nv))
    conv_g = lax.dynamic_slice_in_dim(red["ssd_conv_w"].reshape(SSD_CONV, XBC), chip * conv_cols, conv_cols, axis=1)
    d, nm, nv = _adamw(given["ssd_conv_w"][0], conv_g, given["m_ssd_conv_w"][0], given["v_ssd_conv_w"][0], "adamw_conv_w")
    grads["ssd_conv_w"], deltas["ssd_conv_w"], new_m["ssd_conv_w"], new_v["ssd_conv_w"] = (a[None] for a in (conv_g, d, nm, nv))
    small_names = [k for k, _ in SMALL if k not in ("heads", "ssd_conv_w")] + list(HEAD_VECS)
    pack_of = lambda prefix: _pack_small({k: given[prefix + k] for k in small_names})
    g_small = _pack_small({k: red[k] for k in small_names})
    d, nm, nv = (_unpack_small(a) for a in _adamw(pack_of(""), g_small, pack_of("m_"), pack_of("v_"), "adamw_small"))
    for k in small_names:
        shape = given[k].shape
        grads[k], deltas[k], new_m[k], new_v[k] = (a[k].reshape(shape) for a in (red, d, nm, nv))

    return (red["loss"], grad_x[None], *[grads[n] for n in weights], *[deltas[n] for n in weights],
            *[new_m[n] for n in weights], *[new_v[n] for n in weights])
```

```python
import functools
import math

import jax
import jax.numpy as jnp
from jax import lax
from jax.experimental import pallas as pl
from jax.experimental.pallas import tpu as pltpu

F32, BF16 = jnp.float32, jnp.bfloat16
HI = lax.Precision.HIGHEST
MESH = pl.DeviceIdType.MESH

D_MODEL = 1024
D_FF = 2816
MEM_LEN = 256
SSD_INNER = 2048
SSD_HEADS = 32
SSD_GROUPS = 4
SSD_STATE = 128
SSD_CONV = 4
CHUNK = 128
XBC = SSD_INNER + 2 * SSD_GROUPS * SSD_STATE
GMLP_W = 1024
GMLP_GROUPS = 8
MEM_W = 256
MEM_HEADS = 4
EPS = 1e-6
IN_WIDTH = 10528
IN_Z, IN_XBC, IN_DT, IN_UV, IN_Q, IN_GL = 0, 2048, 5120, 5152, 7200, 7456
P_Z, P_UV, P_XBC, P_GL, P_Q, P_W = 0, 2048, 4096, 7168, 10240, 10752
P_USED = 10496

ADAM_LR, ADAM_B1, ADAM_B2, ADAM_EPS, ADAM_WD, ADAM_STEP = 0.001, 0.9, 0.999, 1e-08, 0.01, 10

V7X_VMEM_LIMIT = 56 * 1024 * 1024
N_SHARD = 4
LANES = 1024


def _cparams(*sem):
    return pltpu.CompilerParams(dimension_semantics=sem, vmem_limit_bytes=V7X_VMEM_LIMIT)


def _sigmoid(x):
    return 1.0 / (1.0 + jnp.exp(-x))


def _row_tile(t):
    return min(512, t)


_DIMS = {"nn": (((1,), (0,)), ((), ())), "nt": (((1,), (1,)), ((), ())), "tn": (((0,), (0,)), ((), ()))}


def _matmul(a, b, *, mode, out_dtype, tm, tn, tk, name, scale=1.0, addend=None):
    if mode == "tn":
        k_dim, m_dim = a.shape
    else:
        m_dim, k_dim = a.shape
    n_dim = b.shape[0] if mode == "nt" else b.shape[1]
    tm, tn, tk = min(tm, m_dim), min(tn, n_dim), min(tk, k_dim)
    assert m_dim % tm == 0 and n_dim % tn == 0 and k_dim % tk == 0, (name, a.shape, b.shape, tm, tn, tk)
    ni, nj, nk = m_dim // tm, n_dim // tn, k_dim // tk
    a_spec = pl.BlockSpec((tk, tm), lambda j, i, k: (k, i)) if mode == "tn" else pl.BlockSpec((tm, tk), lambda j, i, k: (i, k))
    b_spec = pl.BlockSpec((tn, tk), lambda j, i, k: (j, k)) if mode == "nt" else pl.BlockSpec((tk, tn), lambda j, i, k: (k, j))
    o_spec = pl.BlockSpec((tm, tn), lambda j, i, k: (i, j))
    dims = _DIMS[mode]
    has_add = addend is not None

    def body(*refs):
        if has_add:
            a_ref, b_ref, r_ref, o_ref, acc_ref = refs
        else:
            a_ref, b_ref, o_ref, acc_ref = refs
        k = pl.program_id(2)

        @pl.when(k == 0)
        def _():
            acc_ref[...] = jnp.zeros_like(acc_ref)

        acc_ref[...] += lax.dot_general(a_ref[...].astype(BF16), b_ref[...].astype(BF16), dims, preferred_element_type=F32)

        @pl.when(k == nk - 1)
        def _():
            r = acc_ref[...] * scale
            if has_add:
                r = r + r_ref[...].astype(F32)
            o_ref[...] = r.astype(o_ref.dtype)

    in_specs = [a_spec, b_spec] + ([o_spec] if has_add else [])
    args = (a, b) + ((addend,) if has_add else ())
    return pl.pallas_call(
        body, name=name, grid=(nj, ni, nk), in_specs=in_specs, out_specs=o_spec,
        out_shape=jax.ShapeDtypeStruct((m_dim, n_dim), out_dtype),
        scratch_shapes=[pltpu.VMEM((tm, tn), F32)],
        compiler_params=_cparams("parallel", "parallel", "arbitrary"),
    )(*args)


ROW_STRIP = 16


def _strips(tm, fn, init=None, rb=ROW_STRIP):
    def step(i, carry):
        return fn(pl.ds(pl.multiple_of(i * rb, rb), rb), carry)
    return lax.fori_loop(0, tm // rb, step, init, unroll=2)


def _rms_fwd(x, gain, name):
    t, d = x.shape
    tm = _row_tile(t)

    def body(x_ref, g_ref, o_ref):
        xv = x_ref[...]
        r = lax.rsqrt(jnp.mean(xv * xv, axis=-1, keepdims=True) + EPS)
        o_ref[...] = (xv * r * g_ref[...]).astype(o_ref.dtype)

    return pl.pallas_call(
        body, name=name, grid=(t // tm,),
        in_specs=[pl.BlockSpec((tm, d), lambda i: (i, 0)), pl.BlockSpec((1, d), lambda i: (0, 0))],
        out_specs=pl.BlockSpec((tm, d), lambda i: (i, 0)),
        out_shape=jax.ShapeDtypeStruct((t, d), BF16), compiler_params=_cparams("parallel"),
    )(x, gain)


def _rms_bwd(x, gain, dn, dres, name):
    t, d = x.shape
    tm = _row_tile(t)
    has_res = dres is not None

    def body(*refs):
        if has_res:
            x_ref, g_ref, dn_ref, r_ref, dx_ref, dg_ref = refs
        else:
            x_ref, g_ref, dn_ref, dx_ref, dg_ref = refs

        @pl.when(pl.program_id(0) == 0)
        def _():
            dg_ref[...] = jnp.zeros_like(dg_ref)

        xv = x_ref[...]
        r = lax.rsqrt(jnp.mean(xv * xv, axis=-1, keepdims=True) + EPS)
        xh = xv * r
        dnv = dn_ref[...].astype(F32)
        dg_ref[...] += jnp.sum(dnv * xh, axis=0, keepdims=True)
        dxh = dnv * g_ref[...]
        dx = r * (dxh - xh * jnp.mean(dxh * xh, axis=-1, keepdims=True))
        if has_res:
            dx = dx + r_ref[...]
        dx_ref[...] = dx

    row = pl.BlockSpec((tm, d), lambda i: (i, 0))
    vec = pl.BlockSpec((1, d), lambda i: (0, 0))
    in_specs = [row, vec, row] + ([row] if has_res else [])
    args = (x, gain, dn) + ((dres,) if has_res else ())
    return pl.pallas_call(
        body, name=name, grid=(t // tm,), in_specs=in_specs, out_specs=[row, vec],
        out_shape=[jax.ShapeDtypeStruct((t, d), F32), jax.ShapeDtypeStruct((1, d), F32)],
        compiler_params=_cparams("arbitrary"),
    )(*args)


def _loss_head(h, gain, target, name):
    t, d = h.shape
    tm = _row_tile(t)

    def body(h_ref, g_ref, t_ref, dh_ref, dg_ref, l_ref):
        @pl.when(pl.program_id(0) == 0)
        def _():
            dg_ref[...] = jnp.zeros_like(dg_ref)
            l_ref[...] = jnp.zeros_like(l_ref)

        xv = h_ref[...]
        g = g_ref[...]
        r = lax.rsqrt(jnp.mean(xv * xv, axis=-1, keepdims=True) + EPS)
        xh = xv * r
        err = xh * g - t_ref[...]
        l_ref[...] += 0.5 * jnp.sum(jnp.mean(err * err, axis=-1, keepdims=True), axis=0, keepdims=True)
        dy = err * (1.0 / d)
        dg_ref[...] += jnp.sum(dy * xh, axis=0, keepdims=True)
        dxh = dy * g
        dh_ref[...] = r * (dxh - xh * jnp.mean(dxh * xh, axis=-1, keepdims=True))

    row = pl.BlockSpec((tm, d), lambda i: (i, 0))
    vec = pl.BlockSpec((1, d), lambda i: (0, 0))
    return pl.pallas_call(
        body, name=name, grid=(t // tm,), in_specs=[row, vec, row],
        out_specs=[row, vec, pl.BlockSpec((1, 128), lambda i: (0, 0))],
        out_shape=[jax.ShapeDtypeStruct((t, d), F32), jax.ShapeDtypeStruct((1, d), F32), jax.ShapeDtypeStruct((1, 128), F32)],
        compiler_params=_cparams("arbitrary"),
    )(h, gain, target)


FF_TILE = 1408


def _ffn_fwd(n, x, wg, wu, wd, name):
    t, d = x.shape
    tm, tn = _row_tile(t), FF_TILE
    nj = D_FF // tn

    def body(n_ref, x_ref, wg_ref, wu_ref, wd_ref, h_ref, g_ref, u_ref, acc_ref):
        j = pl.program_id(1)

        @pl.when(j == 0)
        def _():
            acc_ref[...] = jnp.zeros_like(acc_ref)

        nb = n_ref[...]
        g = jnp.dot(nb, wg_ref[...], preferred_element_type=F32)
        u = jnp.dot(nb, wu_ref[...], preferred_element_type=F32)
        g_ref[...] = g.astype(BF16)
        u_ref[...] = u.astype(BF16)
        a = g * _sigmoid(g) * u
        acc_ref[...] += jnp.dot(a.astype(BF16), wd_ref[...], preferred_element_type=F32)

        @pl.when(j == nj - 1)
        def _():
            h_ref[...] = x_ref[...] + 0.5 * acc_ref[...]

    row = pl.BlockSpec((tm, d), lambda i, j: (i, 0))
    act = pl.BlockSpec((tm, tn), lambda i, j: (i, j))
    return pl.pallas_call(
        body, name=name, grid=(t // tm, nj),
        in_specs=[row, row, pl.BlockSpec((d, tn), lambda i, j: (0, j)), pl.BlockSpec((d, tn), lambda i, j: (0, j)),
                  pl.BlockSpec((tn, d), lambda i, j: (j, 0))],
        out_specs=[row, act, act],
        out_shape=[jax.ShapeDtypeStruct((t, d), F32), jax.ShapeDtypeStruct((t, D_FF), BF16), jax.ShapeDtypeStruct((t, D_FF), BF16)],
        scratch_shapes=[pltpu.VMEM((tm, d), F32)], compiler_params=_cparams("parallel", "arbitrary"),
    )(n, x, wg, wu, wd)


def _ffn_bwd_act(dh, g, u, wg, wu, wd, name):
    t, d = dh.shape
    tm, tn = _row_tile(t), FF_TILE
    nj = D_FF // tn

    def body(dh_ref, g_ref, u_ref, wg_ref, wu_ref, wd_ref, dn_ref, dg_ref, du_ref, a_ref, acc_ref):
        j = pl.program_id(1)

        @pl.when(j == 0)
        def _():
            acc_ref[...] = jnp.zeros_like(acc_ref)

        dhb = (0.5 * dh_ref[...]).astype(BF16)
        da = lax.dot_general(dhb, wd_ref[...], _DIMS["nt"], preferred_element_type=F32)
        gv = g_ref[...].astype(F32)
        uv = u_ref[...].astype(F32)
        sg = _sigmoid(gv)
        s = gv * sg
        dg = (da * uv * (sg * (1.0 + gv * (1.0 - sg)))).astype(BF16)
        du = (da * s).astype(BF16)
        dg_ref[...] = dg
        du_ref[...] = du
        a_ref[...] = (s * uv).astype(BF16)
        acc_ref[...] += (lax.dot_general(dg, wg_ref[...], _DIMS["nt"], preferred_element_type=F32)
                         + lax.dot_general(du, wu_ref[...], _DIMS["nt"], preferred_element_type=F32))

        @pl.when(j == nj - 1)
        def _():
            dn_ref[...] = acc_ref[...]

    row = pl.BlockSpec((tm, d), lambda i, j: (i, 0))
    act = pl.BlockSpec((tm, tn), lambda i, j: (i, j))
    return pl.pallas_call(
        body, name=name, grid=(t // tm, nj),
        in_specs=[row, act, act, pl.BlockSpec((d, tn), lambda i, j: (0, j)), pl.BlockSpec((d, tn), lambda i, j: (0, j)),
                  pl.BlockSpec((tn, d), lambda i, j: (j, 0))],
        out_specs=[row, act, act, act],
        out_shape=[jax.ShapeDtypeStruct((t, d), F32)] + [jax.ShapeDtypeStruct((t, D_FF), BF16)] * 3,
        scratch_shapes=[pltpu.VMEM((tm, d), F32)], compiler_params=_cparams("parallel", "arbitrary"),
    )(dh, g, u, wg, wu, wd)


def _ffn_forward(x, gain, wg, wu, wd, tag):
    n = _rms_fwd(x, gain, f"{tag}_norm")
    h, g, u = _ffn_fwd(n, x, wg, wu, wd, f"{tag}_fwd")
    return h, (n, g, u)


def _ffn_backward(dh, x, gain, wg, wu, wd, saved, tag):
    n, g, u = saved
    dn, dg, du, a = _ffn_bwd_act(dh, g, u, wg, wu, wd, f"{tag}_bwd_act")
    kw = dict(mode="tn", out_dtype=BF16, tm=1024, tn=FF_TILE, tk=1024)
    d_wg = _matmul(n, dg, name=f"{tag}_dwg", **kw)
    d_wu = _matmul(n, du, name=f"{tag}_dwu", **kw)
    d_wd = _matmul(a, dh, mode="tn", out_dtype=BF16, tm=FF_TILE, tn=1024, tk=1024, scale=0.5, name=f"{tag}_dwd")
    dx, dgain = _rms_bwd(x, gain, dn, dh, f"{tag}_norm_bwd")
    return dx, dict(norm=dgain, w_gate=d_wg, w_up=d_wu, w_down=d_wd)


CONV_COLS = 512
HALO = 8
CONV_STRIP = 32


def _conv_fwd(p, w, b, name):
    t = p.shape[0]
    tm = _row_tile(t)
    c0 = P_XBC // CONV_COLS

    def body(x_ref, halo_ref, w_ref, b_ref, o_ref, s_ref):
        i = pl.program_id(1)
        s_ref[0:HALO, :] = jnp.where(i > 0, halo_ref[...].astype(F32), 0.0)
        s_ref[HALO:HALO + tm, :] = x_ref[...].astype(F32)
        wv = w_ref[...]
        bv = b_ref[...]
        for r0 in range(0, tm, CONV_STRIP):
            acc = bv + wv[0:1, :] * s_ref[HALO - 3 + r0:HALO - 3 + r0 + CONV_STRIP, :]
            for k in range(1, SSD_CONV):
                acc = acc + wv[k:k + 1, :] * s_ref[HALO - 3 + k + r0:HALO - 3 + k + r0 + CONV_STRIP, :]
            o_ref[r0:r0 + CONV_STRIP, :] = (acc * _sigmoid(acc)).astype(o_ref.dtype)

    return pl.pallas_call(
        body, name=name, grid=(XBC // CONV_COLS, t // tm),
        in_specs=[pl.BlockSpec((tm, CONV_COLS), lambda j, i: (i, c0 + j)),
                  pl.BlockSpec((HALO, CONV_COLS), lambda j, i: (jnp.maximum(i * (tm // HALO) - 1, 0), c0 + j)),
                  pl.BlockSpec((SSD_CONV, CONV_COLS), lambda j, i: (0, j)), pl.BlockSpec((1, CONV_COLS), lambda j, i: (0, j))],
        out_specs=pl.BlockSpec((tm, CONV_COLS), lambda j, i: (i, j)),
        out_shape=jax.ShapeDtypeStruct((t, XBC), BF16),
        scratch_shapes=[pltpu.VMEM((tm + HALO, CONV_COLS), F32)], compiler_params=_cparams("parallel", "parallel"),
    )(p, p, w, b)


def _conv_bwd_act(p, dy, w, b, col0, name):
    t, cols = dy.shape
    tm = _row_tile(t)
    c0 = (P_XBC + col0) // CONV_COLS
    w0 = col0 // CONV_COLS

    def body(x_ref, halo_ref, dy_ref, w_ref, b_ref, da_ref, dw_ref, db_ref, s_ref):
        i = pl.program_id(1)

        @pl.when(i == 0)
        def _():
            dw_ref[...] = jnp.zeros_like(dw_ref)
            db_ref[...] = jnp.zeros_like(db_ref)

        s_ref[0:HALO, :] = jnp.where(i > 0, halo_ref[...].astype(F32), 0.0)
        s_ref[HALO:HALO + tm, :] = x_ref[...].astype(F32)
        wv = w_ref[...]
        bv = b_ref[...]
        fold = lambda v: jnp.sum(v.reshape(CONV_STRIP // 8, 8, CONV_COLS), axis=0)
        sums = [jnp.zeros((8, CONV_COLS), F32) for _ in range(SSD_CONV + 1)]
        for r0 in range(0, tm, CONV_STRIP):
            taps = [s_ref[HALO - 3 + k + r0:HALO - 3 + k + r0 + CONV_STRIP, :] for k in range(SSD_CONV)]
            acc = bv + wv[0:1, :] * taps[0]
            for k in range(1, SSD_CONV):
                acc = acc + wv[k:k + 1, :] * taps[k]
            sg = _sigmoid(acc)
            dacc = dy_ref[r0:r0 + CONV_STRIP, :].astype(F32) * (sg * (1.0 + acc * (1.0 - sg)))
            da_ref[r0:r0 + CONV_STRIP, :] = dacc.astype(BF16)
            for k in range(SSD_CONV):
                sums[k] = sums[k] + fold(dacc * taps[k])
            sums[SSD_CONV] = sums[SSD_CONV] + fold(dacc)
        for k in range(SSD_CONV):
            dw_ref[k:k + 1, :] += jnp.sum(sums[k], axis=0, keepdims=True)
        db_ref[...] += jnp.sum(sums[SSD_CONV], axis=0, keepdims=True)

    return pl.pallas_call(
        body, name=name, grid=(cols // CONV_COLS, t // tm),
        in_specs=[pl.BlockSpec((tm, CONV_COLS), lambda j, i: (i, c0 + j)),
                  pl.BlockSpec((HALO, CONV_COLS), lambda j, i: (jnp.maximum(i * (tm // HALO) - 1, 0), c0 + j)),
                  pl.BlockSpec((tm, CONV_COLS), lambda j, i: (i, j)),
                  pl.BlockSpec((SSD_CONV, CONV_COLS), lambda j, i: (0, w0 + j)), pl.BlockSpec((1, CONV_COLS), lambda j, i: (0, w0 + j))],
        out_specs=[pl.BlockSpec((tm, CONV_COLS), lambda j, i: (i, j)), pl.BlockSpec((SSD_CONV, CONV_COLS), lambda j, i: (0, j)),
                   pl.BlockSpec((1, CONV_COLS), lambda j, i: (0, j))],
        out_shape=[jax.ShapeDtypeStruct((t, cols), BF16), jax.ShapeDtypeStruct((SSD_CONV, cols), F32), jax.ShapeDtypeStruct((1, cols), F32)],
        scratch_shapes=[pltpu.VMEM((tm + HALO, CONV_COLS), F32)], compiler_params=_cparams("parallel", "arbitrary"),
    )(p, p, dy, w, b)


def _conv_bwd_dx(dacc, w, col0, name):
    t, cols = dacc.shape
    tm = _row_tile(t)
    nt = t // tm
    w0 = col0 // CONV_COLS

    def body(d_ref, halo_ref, w_ref, o_ref, s_ref):
        i = pl.program_id(1)
        s_ref[0:tm, :] = d_ref[...].astype(F32)
        s_ref[tm:tm + HALO, :] = jnp.where(i < nt - 1, halo_ref[...].astype(F32), 0.0)
        wv = w_ref[...]
        for r0 in range(0, tm, CONV_STRIP):
            acc = wv[3:4, :] * s_ref[r0:r0 + CONV_STRIP, :]
            for k in range(SSD_CONV - 1):
                acc = acc + wv[k:k + 1, :] * s_ref[3 - k + r0:3 - k + r0 + CONV_STRIP, :]
            o_ref[r0:r0 + CONV_STRIP, :] = acc.astype(o_ref.dtype)

    return pl.pallas_call(
        body, name=name, grid=(cols // CONV_COLS, nt),
        in_specs=[pl.BlockSpec((tm, CONV_COLS), lambda j, i: (i, j)),
                  pl.BlockSpec((HALO, CONV_COLS), lambda j, i: (jnp.minimum((i + 1) * (tm // HALO), t // HALO - 1), j)),
                  pl.BlockSpec((SSD_CONV, CONV_COLS), lambda j, i: (0, w0 + j))],
        out_specs=pl.BlockSpec((tm, CONV_COLS), lambda j, i: (i, j)),
        out_shape=jax.ShapeDtypeStruct((t, cols), BF16),
        scratch_shapes=[pltpu.VMEM((tm + HALO, CONV_COLS), F32)], compiler_params=_cparams("parallel", "parallel"),
    )(dacc, dacc, w)


GROUP_COLS = SSD_INNER // SSD_GROUPS
PAIRS = GROUP_COLS // 128
HEADS_PER_GROUP = SSD_HEADS // SSD_GROUPS


def _dt_fwd(dt_raw, bias, name):
    t, n = dt_raw.shape
    tm = _row_tile(t)

    def body(x_ref, b_ref, o_ref):
        v = x_ref[...] + b_ref[...]
        o_ref[...] = jnp.maximum(v, 0.0) + jnp.log1p(jnp.exp(-jnp.abs(v)))

    row = pl.BlockSpec((tm, n), lambda i: (i, 0))
    vec = pl.BlockSpec((1, n), lambda i: (0, 0))
    return pl.pallas_call(body, name=name, grid=(t // tm,), in_specs=[row, vec], out_specs=row,
                          out_shape=jax.ShapeDtypeStruct((t, n), F32), compiler_params=_cparams("parallel"))(dt_raw, bias)


def _dt_bwd(ddt, dt_raw, bias, name):
    t, n = dt_raw.shape
    tm = _row_tile(t)

    def body(d_ref, x_ref, b_ref, o_ref, db_ref):
        @pl.when(pl.program_id(0) == 0)
        def _():
            db_ref[...] = jnp.zeros_like(db_ref)

        dr = d_ref[...] * _sigmoid(x_ref[...] + b_ref[...])
        o_ref[...] = dr.astype(o_ref.dtype)
        db_ref[...] += jnp.sum(dr, axis=0, keepdims=True)

    row = pl.BlockSpec((tm, n), lambda i: (i, 0))
    vec = pl.BlockSpec((1, n), lambda i: (0, 0))
    return pl.pallas_call(body, name=name, grid=(t // tm,), in_specs=[row, row, vec], out_specs=[row, vec],
                          out_shape=[jax.ShapeDtypeStruct((t, n), BF16), jax.ShapeDtypeStruct((1, n), F32)],
                          compiler_params=_cparams("arbitrary"))(ddt, dt_raw, bias)


def _ssd_common(dtw_ref, dtt_ref, aw_ref, at_ref):
    l = CHUNK
    dt = dtw_ref[0]
    dtt = dtt_ref[0]
    a = -jnp.exp(aw_ref[0])
    at = -jnp.exp(at_ref[0])
    rowi = lax.broadcasted_iota(jnp.int32, (l, l), 0)
    coli = lax.broadcasted_iota(jnp.int32, (l, l), 1)
    tri = rowi >= coli
    lower = tri.astype(F32)
    upper = (rowi <= coli).astype(F32)
    acs = jnp.dot(lower, dt * a, precision=HI, preferred_element_type=F32)
    acst = jnp.dot(dtt * at, upper, precision=HI, preferred_element_type=F32)
    return dt, a, acs, acst, tri, upper


def _pair_bc(w, lo, p):
    return jnp.where(lo, w[:, 2 * p:2 * p + 1], w[:, 2 * p + 1:2 * p + 2])


def _ssd_specs(t):
    nc = t // CHUNK
    return nc, dict(
        xs=lambda cm: pl.BlockSpec((CHUNK, GROUP_COLS), lambda g, c: (cm(c), g)),
        bm=lambda cm: pl.BlockSpec((CHUNK, SSD_STATE), lambda g, c: (cm(c), SSD_INNER // SSD_STATE + g)),
        cmat=lambda cm: pl.BlockSpec((CHUNK, SSD_STATE), lambda g, c: (cm(c), SSD_INNER // SSD_STATE + SSD_GROUPS + g)),
        dtw=lambda cm: pl.BlockSpec((1, CHUNK, 128), lambda g, c: (g, cm(c), 0)),
        dtt=lambda cm: pl.BlockSpec((1, HEADS_PER_GROUP, CHUNK), lambda g, c: (g, 0, cm(c))),
        wide=lambda cm: pl.BlockSpec((1, 1, 128), lambda g, c: (g, 0, 0)),
        tall=lambda cm: pl.BlockSpec((1, HEADS_PER_GROUP, 1), lambda g, c: (g, 0, 0)),
        grp=lambda cm: pl.BlockSpec((CHUNK, GROUP_COLS), lambda g, c: (cm(c), g)),
        vec=lambda cm: pl.BlockSpec((1, GROUP_COLS), lambda g, c: (0, g)),
        state=lambda cm: pl.BlockSpec((1, 1, PAIRS, SSD_STATE, 128), lambda g, c: (g, cm(c), 0, 0, 0)),
    )


def _ssd_fwd(xc, p, hv, norm_g, name):
    t = xc.shape[0]
    nc, sp = _ssd_specs(t)
    ident = lambda c: c

    def body(xs_ref, b_ref, c_ref, dtw_ref, dtt_ref, aw_ref, at_ref, dk_ref, z_ref, ng_ref,
             y_ref, ys_ref, h_ref, st_ref):
        @pl.when(pl.program_id(1) == 0)
        def _():
            st_ref[...] = jnp.zeros_like(st_ref)

        dt, a, acs, acst, tri, _ = _ssd_common(dtw_ref, dtt_ref, aw_ref, at_ref)
        ecs = jnp.exp(acs)
        alast = acs[CHUNK - 1:CHUNK, :]
        bmat, cmat = b_ref[...], c_ref[...]
        cb = lax.dot_general(cmat, bmat, _DIMS["nt"], preferred_element_type=F32)
        lo = lax.broadcasted_iota(jnp.int32, (1, 128), 1) < 64
        dskip = dk_ref[0]
        for pi in range(PAIRS):
            x = xs_ref[:, pi * 128:(pi + 1) * 128].astype(F32)
            xdt = x * _pair_bc(dt, lo, pi)
            ydiag = jnp.zeros((CHUNK, 128), F32)
            for r, mask in ((2 * pi, lo), (2 * pi + 1, jnp.logical_not(lo))):
                lam = jnp.exp(jnp.where(tri, acs[:, r:r + 1] - acst[r:r + 1, :], -1e30))
                m = (cb * lam).astype(BF16)
                ydiag = ydiag + jnp.dot(m, jnp.where(mask, xdt, 0.0).astype(BF16), preferred_element_type=F32)
            ht = st_ref[pi]
            h_ref[0, 0, pi] = ht
            yoff = jnp.dot(cmat, ht.astype(BF16), preferred_element_type=F32) * _pair_bc(ecs, lo, pi)
            y_ref[:, pi * 128:(pi + 1) * 128] = (ydiag + yoff + _pair_bc(dskip, lo, pi) * x).astype(y_ref.dtype)
            alp = _pair_bc(alast, lo, pi)
            e = jnp.exp(alp - _pair_bc(acs, lo, pi))
            st = lax.dot_general(bmat, (xdt * e).astype(BF16), _DIMS["tn"], preferred_element_type=F32)
            st_ref[pi] = ht * jnp.exp(alp) + st
        zf = z_ref[...].astype(F32)
        yg = y_ref[...].astype(F32) * (zf * _sigmoid(zf))
        rstd = lax.rsqrt(jnp.mean(yg * yg, axis=-1, keepdims=True) + EPS)
        ys_ref[...] = (yg * rstd * ng_ref[...]).astype(ys_ref.dtype)

    ins = ["xs", "bm", "cmat", "dtw", "dtt", "wide", "tall", "wide", "grp", "vec"]
    return pl.pallas_call(
        body, name=name, grid=(SSD_GROUPS, nc),
        in_specs=[sp[k](ident) for k in ins],
        out_specs=[sp["grp"](ident), sp["grp"](ident), sp["state"](ident)],
        out_shape=[jax.ShapeDtypeStruct((t, SSD_INNER), BF16), jax.ShapeDtypeStruct((t, SSD_INNER), BF16),
                   jax.ShapeDtypeStruct((SSD_GROUPS, nc, PAIRS, SSD_STATE, 128), F32)],
        scratch_shapes=[pltpu.VMEM((PAIRS, SSD_STATE, 128), F32)], compiler_params=_cparams("parallel", "arbitrary"),
    )(xc, xc, xc, hv["dtw"], hv["dtt"], hv["alog_w"], hv["alog_t"], hv["dskip_w"], p, norm_g)


def _ssd_bwd(xc, p, hv, norm_g, y, dys, states, name):
    t = xc.shape[0]
    nc, sp = _ssd_specs(t)
    rev = lambda c: nc - 1 - c

    def body(xs_ref, b_ref, c_ref, dtw_ref, dtt_ref, aw_ref, at_ref, dk_ref, z_ref, ng_ref,
             y_ref, dys_ref, h_ref,
             dxs_ref, db_ref, dc_ref, dz_ref, ddt_ref, hsum_ref, dng_ref, dst_ref):
        @pl.when(pl.program_id(1) == 0)
        def _():
            dst_ref[...] = jnp.zeros_like(dst_ref)
            hsum_ref[...] = jnp.zeros_like(hsum_ref)
            dng_ref[...] = jnp.zeros_like(dng_ref)

        dt, a, acs, acst, tri, upper = _ssd_common(dtw_ref, dtt_ref, aw_ref, at_ref)
        ecs = jnp.exp(acs)
        alast = acs[CHUNK - 1:CHUNK, :]
        bmat, cmat = b_ref[...], c_ref[...]
        cb = lax.dot_general(cmat, bmat, _DIMS["nt"], preferred_element_type=F32)
        lane = lax.broadcasted_iota(jnp.int32, (1, 128), 1)
        lo = lane < 64
        dskip = dk_ref[0]

        zf = z_ref[...].astype(F32)
        sg = _sigmoid(zf)
        sz = zf * sg
        yv = y_ref[...].astype(F32)
        yg = yv * sz
        rstd = lax.rsqrt(jnp.mean(yg * yg, axis=-1, keepdims=True) + EPS)
        yhat = yg * rstd
        dysv = dys_ref[...].astype(F32)
        dng_ref[...] += jnp.sum(dysv * yhat, axis=0, keepdims=True)
        dyh = dysv * ng_ref[...]
        dyg = rstd * (dyh - yhat * jnp.mean(dyh * yhat, axis=-1, keepdims=True))
        dz_ref[...] = (dyg * yv * (sg * (1.0 + zf * (1.0 - sg)))).astype(dz_ref.dtype)
        dy_all = dyg * sz

        sel_r = lax.broadcasted_iota(jnp.int32, (128, 128), 0)
        sel_c = lax.broadcasted_iota(jnp.int32, (128, 128), 1)
        dal = jnp.zeros((CHUNK, 128), F32)
        ddtm = jnp.zeros((CHUNK, 128), F32)
        dalast = jnp.zeros((8, 128), F32)
        ddsk = jnp.zeros((8, 128), F32)
        dcb = jnp.zeros((CHUNK, CHUNK), F32)
        qcol = jnp.zeros((8, CHUNK), F32)
        sub8 = lax.broadcasted_iota(jnp.int32, (8, CHUNK), 0)
        dc_acc = jnp.zeros((CHUNK, SSD_STATE), F32)
        db_acc = jnp.zeros((CHUNK, SSD_STATE), F32)
        for pi in range(PAIRS):
            sel = (sel_c == 2 * pi + (sel_r >= 64).astype(jnp.int32)).astype(BF16)

            def hsum(v, sel=sel):
                return jnp.dot(v.astype(BF16), sel, preferred_element_type=F32)

            dyp = dy_all[:, pi * 128:(pi + 1) * 128]
            x = xs_ref[:, pi * 128:(pi + 1) * 128].astype(F32)
            dtp = _pair_bc(dt, lo, pi)
            xdt = x * dtp
            dxdt = jnp.zeros((CHUNK, 128), F32)
            for r, mask in ((2 * pi, lo), (2 * pi + 1, jnp.logical_not(lo))):
                lam = jnp.exp(jnp.where(tri, acs[:, r:r + 1] - acst[r:r + 1, :], -1e30))
                m32 = cb * lam
                m = m32.astype(BF16)
                dyr = jnp.where(mask, dyp, 0.0).astype(BF16)
                xr = jnp.where(mask, xdt, 0.0).astype(BF16)
                dm = lax.dot_general(dyr, xr, _DIMS["nt"], preferred_element_type=F32)
                dcb = dcb + dm * lam
                q = dm * m32
                dal = dal + jnp.sum(q, axis=1, keepdims=True) * (lane == r).astype(F32)
                qcol = qcol + jnp.where(sub8 == r, jnp.sum(q, axis=0, keepdims=True), 0.0)
                dxdt = dxdt + lax.dot_general(m, dyr, _DIMS["tn"], preferred_element_type=F32)
            ht = h_ref[0, 0, pi]
            htb = ht.astype(BF16)
            ecp = _pair_bc(ecs, lo, pi)
            yoff = jnp.dot(cmat, htb, preferred_element_type=F32) * ecp
            dg = (dyp * ecp).astype(BF16)
            dc_acc = dc_acc + lax.dot_general(dg, htb, _DIMS["nt"], preferred_element_type=F32)
            dht = lax.dot_general(cmat, dg, _DIMS["tn"], preferred_element_type=F32)
            dal = dal + hsum(dyp * yoff)
            dhn = dst_ref[pi]
            dhnb = dhn.astype(BF16)
            alp = _pair_bc(alast, lo, pi)
            e = jnp.exp(alp - _pair_bc(acs, lo, pi))
            xe = xdt * e
            db_acc = db_acc + lax.dot_general(xe.astype(BF16), dhnb, _DIMS["nt"], preferred_element_type=F32)
            dxe = jnp.dot(bmat, dhnb, preferred_element_type=F32)
            dxdt = dxdt + dxe * e
            tt = hsum(dxe * xe)
            dal = dal - tt
            dec = jnp.exp(alp)
            dalast = dalast + jnp.sum(tt, axis=0, keepdims=True) + hsum(
                jnp.broadcast_to(jnp.sum(dhn * ht, axis=0, keepdims=True) * dec, (8, 128)))
            dst_ref[pi] = dht + dhn * dec
            dxs_ref[:, pi * 128:(pi + 1) * 128] = (_pair_bc(dskip, lo, pi) * dyp + dxdt * dtp).astype(dxs_ref.dtype)
            ddtm = ddtm + hsum(dxdt * x)
            ddsk = ddsk + hsum(jnp.broadcast_to(jnp.sum(dyp * x, axis=0, keepdims=True), (8, 128)))
        rowi = lax.broadcasted_iota(jnp.int32, (CHUNK, 128), 0)
        qcol_w = lax.dot_general(jnp.concatenate([qcol, jnp.zeros((CHUNK - 8, CHUNK), F32)], axis=0), (sel_r == sel_c).astype(F32),
                                 _DIMS["tn"], precision=HI, preferred_element_type=F32)
        dal = dal - qcol_w + jnp.where(rowi == CHUNK - 1, dalast[0:1, :], 0.0)
        dda = jnp.dot(upper, dal, precision=HI, preferred_element_type=F32)
        ddt_ref[0] = ddtm + dda * a
        hsum_ref[0, 1:2, :] += jnp.sum(dda * dt, axis=0, keepdims=True) * a
        hsum_ref[0, 2:3, :] += ddsk[0:1, :]
        dcbb = dcb.astype(BF16)
        dc_ref[...] = (jnp.dot(dcbb, bmat, preferred_element_type=F32) + dc_acc).astype(dc_ref.dtype)
        db_ref[...] = (lax.dot_general(dcbb, cmat, _DIMS["tn"], preferred_element_type=F32) + db_acc).astype(db_ref.dtype)

    ins = ["xs", "bm", "cmat", "dtw", "dtt", "wide", "tall", "wide", "grp", "vec", "grp", "grp", "state"]
    col = lambda: pl.BlockSpec((CHUNK, SSD_STATE), lambda g, c: (rev(c), g))
    return pl.pallas_call(
        body, name=name, grid=(SSD_GROUPS, nc),
        in_specs=[sp[k](rev) for k in ins],
        out_specs=[sp["grp"](rev), col(), col(), sp["grp"](rev), sp["dtw"](rev),
                   pl.BlockSpec((1, 8, 128), lambda g, c: (g, 0, 0)), sp["vec"](rev)],
        out_shape=[jax.ShapeDtypeStruct((t, SSD_INNER), BF16), jax.ShapeDtypeStruct((t, SSD_GROUPS * SSD_STATE), BF16),
                   jax.ShapeDtypeStruct((t, SSD_GROUPS * SSD_STATE), BF16), jax.ShapeDtypeStruct((t, SSD_INNER), BF16),
                   jax.ShapeDtypeStruct((SSD_GROUPS, t, 128), F32), jax.ShapeDtypeStruct((SSD_GROUPS, 8, 128), F32),
                   jax.ShapeDtypeStruct((1, SSD_INNER), F32)],
        scratch_shapes=[pltpu.VMEM((PAIRS, SSD_STATE, 128), F32)], compiler_params=_cparams("parallel", "arbitrary"),
    )(xc, xc, xc, hv["dtw"], hv["dtt"], hv["alog_w"], hv["alog_t"], hv["dskip_w"], p, norm_g, y, dys, states)


def _wide(v):
    return jnp.pad(v.reshape(SSD_GROUPS, 1, HEADS_PER_GROUP), ((0, 0), (0, 0), (0, 128 - HEADS_PER_GROUP)))


def _head_inputs(dt, a_log, d_skip):
    t = dt.shape[0]
    g = dt[:, :SSD_HEADS].reshape(t, SSD_GROUPS, HEADS_PER_GROUP)
    return dict(
        dtw=jnp.pad(jnp.transpose(g, (1, 0, 2)), ((0, 0), (0, 0), (0, 128 - HEADS_PER_GROUP))),
        dtt=jnp.transpose(g, (1, 2, 0)),
        alog_w=_wide(a_log), alog_t=a_log.reshape(SSD_GROUPS, HEADS_PER_GROUP, 1),
        dskip_w=_wide(d_skip),
    )


def _gelu(x):
    return 0.5 * x * (1.0 + lax.erf(x * (1.0 / math.sqrt(2.0))))


def _gelu_grad(x):
    return 0.5 * (1.0 + lax.erf(x * (1.0 / math.sqrt(2.0)))) + x * jnp.exp(-0.5 * x * x) * (1.0 / math.sqrt(2.0 * math.pi))


def _tril_mask():
    r = lax.broadcasted_iota(jnp.int32, (CHUNK, CHUNK), 0)
    c = lax.broadcasted_iota(jnp.int32, (CHUNK, CHUNK), 1)
    return r >= c


def _gmlp_fwd(p, v_gain, w_s, b_col, name):
    t = p.shape[0]
    tm = _row_tile(t)
    u0 = P_UV // GMLP_W

    def body(u_ref, v_ref, gn_ref, ws_ref, bs_ref, o_ref):
        v = _gelu(v_ref[...].astype(F32))
        v = (v * lax.rsqrt(jnp.mean(v * v, axis=-1, keepdims=True) + EPS) * gn_ref[...]).astype(BF16)
        tril = _tril_mask()
        wm = [jnp.where(tril, ws_ref[g], 0.0).astype(BF16) for g in range(GMLP_GROUPS)]
        for k in range(tm // CHUNK):
            rows = slice(k * CHUNK, (k + 1) * CHUNK)
            for g in range(GMLP_GROUPS):
                cols = slice(g * 128, (g + 1) * 128)
                mixed = jnp.dot(wm[g], v[rows, cols], preferred_element_type=F32) + bs_ref[g]
                o_ref[rows, cols] = (_gelu(u_ref[rows, cols].astype(F32)) * mixed).astype(o_ref.dtype)

    return pl.pallas_call(
        body, name=name, grid=(t // tm,),
        in_specs=[pl.BlockSpec((tm, GMLP_W), lambda i: (i, u0)), pl.BlockSpec((tm, GMLP_W), lambda i: (i, u0 + 1)),
                  pl.BlockSpec((1, GMLP_W), lambda i: (0, 0)), pl.BlockSpec((GMLP_GROUPS, CHUNK, CHUNK), lambda i: (0, 0, 0)),
                  pl.BlockSpec((GMLP_GROUPS, CHUNK, 1), lambda i: (0, 0, 0))],
        out_specs=pl.BlockSpec((tm, GMLP_W), lambda i: (i, 0)),
        out_shape=jax.ShapeDtypeStruct((t, GMLP_W), BF16), compiler_params=_cparams("parallel"),
    )(p, p, v_gain, w_s, b_col)


def _gmlp_bwd(p, dy, v_gain, w_s, b_col, name):
    t = p.shape[0]
    tm = _row_tile(t)
    u0 = P_UV // GMLP_W

    def body(u_ref, v_ref, dy_ref, gn_ref, ws_ref, bs_ref, duv_ref, dws_ref, dbs_ref, dgn_ref, dvn_ref):
        @pl.when(pl.program_id(0) == 0)
        def _():
            dws_ref[...] = jnp.zeros_like(dws_ref)
            dbs_ref[...] = jnp.zeros_like(dbs_ref)
            dgn_ref[...] = jnp.zeros_like(dgn_ref)

        vraw = v_ref[...].astype(F32)
        va = _gelu(vraw)
        rstd = lax.rsqrt(jnp.mean(va * va, axis=-1, keepdims=True) + EPS)
        vhat = va * rstd
        gain = gn_ref[...]
        vn = (vhat * gain).astype(BF16)
        tril = _tril_mask()
        wm = [jnp.where(tril, ws_ref[g], 0.0).astype(BF16) for g in range(GMLP_GROUPS)]
        for k in range(tm // CHUNK):
            rows = slice(k * CHUNK, (k + 1) * CHUNK)
            for g in range(GMLP_GROUPS):
                cols = slice(g * 128, (g + 1) * 128)
                uraw = u_ref[rows, cols].astype(F32)
                vb = vn[rows, cols]
                mixed = jnp.dot(wm[g], vb, preferred_element_type=F32) + bs_ref[g]
                dyb = dy_ref[rows, cols].astype(F32)
                duv_ref[rows, cols] = (dyb * mixed * _gelu_grad(uraw)).astype(duv_ref.dtype)
                dmix = dyb * _gelu(uraw)
                dmb = dmix.astype(BF16)
                dws_ref[g] += jnp.where(tril, lax.dot_general(dmb, vb, _DIMS["nt"], preferred_element_type=F32), 0.0)
                dbs_ref[g] += jnp.sum(dmix, axis=1, keepdims=True)
                dvn_ref[rows, cols] = lax.dot_general(wm[g], dmb, _DIMS["tn"], preferred_element_type=F32)
        dvn = dvn_ref[...]
        dgn_ref[...] += jnp.sum(dvn * vhat, axis=0, keepdims=True)
        dvh = dvn * gain
        dva = rstd * (dvh - vhat * jnp.mean(dvh * vhat, axis=-1, keepdims=True))
        duv_ref[:, GMLP_W:2 * GMLP_W] = (dva * _gelu_grad(vraw)).astype(duv_ref.dtype)

    return pl.pallas_call(
        body, name=name, grid=(t // tm,),
        in_specs=[pl.BlockSpec((tm, GMLP_W), lambda i: (i, u0)), pl.BlockSpec((tm, GMLP_W), lambda i: (i, u0 + 1)),
                  pl.BlockSpec((tm, GMLP_W), lambda i: (i, 0)),
                  pl.BlockSpec((1, GMLP_W), lambda i: (0, 0)), pl.BlockSpec((GMLP_GROUPS, CHUNK, CHUNK), lambda i: (0, 0, 0)),
                  pl.BlockSpec((GMLP_GROUPS, CHUNK, 1), lambda i: (0, 0, 0))],
        out_specs=[pl.BlockSpec((tm, 2 * GMLP_W), lambda i: (i, 0)), pl.BlockSpec((GMLP_GROUPS, CHUNK, CHUNK), lambda i: (0, 0, 0)),
                   pl.BlockSpec((GMLP_GROUPS, CHUNK, 1), lambda i: (0, 0, 0)), pl.BlockSpec((1, GMLP_W), lambda i: (0, 0))],
        out_shape=[jax.ShapeDtypeStruct((t, 2 * GMLP_W), BF16), jax.ShapeDtypeStruct((GMLP_GROUPS, CHUNK, CHUNK), F32),
                   jax.ShapeDtypeStruct((GMLP_GROUPS, CHUNK, 1), F32), jax.ShapeDtypeStruct((1, GMLP_W), F32)],
        scratch_shapes=[pltpu.VMEM((tm, GMLP_W), F32)], compiler_params=_cparams("arbitrary"),
    )(p, p, dy, v_gain, w_s, b_col)


def _head_masks():
    lane = lax.broadcasted_iota(jnp.int32, (1, MEM_W), 1)
    return [(lane >= h * 64) & (lane < (h + 1) * 64) for h in range(MEM_HEADS)]


def _mem_fwd(p, kv, name):
    t = p.shape[0]
    tm = _row_tile(t)
    q0 = P_Q // MEM_W

    def body(q_ref, kv_ref, o_ref):
        q = q_ref[...]
        k = kv_ref[:, 0:MEM_W].astype(BF16)
        v = kv_ref[:, MEM_W:2 * MEM_W].astype(BF16)
        out = jnp.zeros((tm, MEM_W), F32)
        for mask in _head_masks():
            s = lax.dot_general(jnp.where(mask, q, 0), k, _DIMS["nt"], preferred_element_type=F32) * 0.125
            e = jnp.exp(s - jnp.max(s, axis=-1, keepdims=True))
            pr = (e / jnp.sum(e, axis=-1, keepdims=True)).astype(BF16)
            out = out + jnp.where(mask, jnp.dot(pr, v, preferred_element_type=F32), 0.0)
        o_ref[...] = out.astype(o_ref.dtype)

    return pl.pallas_call(
        body, name=name, grid=(t // tm,),
        in_specs=[pl.BlockSpec((tm, MEM_W), lambda i: (i, q0)), pl.BlockSpec((MEM_LEN, 2 * MEM_W), lambda i: (0, 0))],
        out_specs=pl.BlockSpec((tm, MEM_W), lambda i: (i, 0)),
        out_shape=jax.ShapeDtypeStruct((t, MEM_W), BF16), compiler_params=_cparams("parallel"),
    )(p, kv)


def _mem_bwd(p, kv, dy, name):
    t = p.shape[0]
    tm = _row_tile(t)
    q0 = P_Q // MEM_W

    def body(q_ref, kv_ref, dy_ref, dq_ref, dkv_ref):
        @pl.when(pl.program_id(0) == 0)
        def _():
            dkv_ref[...] = jnp.zeros_like(dkv_ref)

        q = q_ref[...]
        dy = dy_ref[...]
        k = kv_ref[:, 0:MEM_W].astype(BF16)
        v = kv_ref[:, MEM_W:2 * MEM_W].astype(BF16)
        dq = jnp.zeros((tm, MEM_W), F32)
        dk = jnp.zeros((MEM_LEN, MEM_W), F32)
        dv = jnp.zeros((MEM_LEN, MEM_W), F32)
        for mask in _head_masks():
            qh = jnp.where(mask, q, 0)
            dyh = jnp.where(mask, dy, 0)
            s = lax.dot_general(qh, k, _DIMS["nt"], preferred_element_type=F32) * 0.125
            e = jnp.exp(s - jnp.max(s, axis=-1, keepdims=True))
            pr = e / jnp.sum(e, axis=-1, keepdims=True)
            prb = pr.astype(BF16)
            dp = lax.dot_general(dyh, v, _DIMS["nt"], preferred_element_type=F32)
            ds = (pr * (dp - jnp.sum(dp * pr, axis=-1, keepdims=True)) * 0.125).astype(BF16)
            dq = dq + jnp.where(mask, jnp.dot(ds, k, preferred_element_type=F32), 0.0)
            dk = dk + lax.dot_general(ds, qh, _DIMS["tn"], preferred_element_type=F32)
            dv = dv + lax.dot_general(prb, dyh, _DIMS["tn"], preferred_element_type=F32)
        dq_ref[...] = dq.astype(dq_ref.dtype)
        dkv_ref[:, 0:MEM_W] += dk
        dkv_ref[:, MEM_W:2 * MEM_W] += dv

    return pl.pallas_call(
        body, name=name, grid=(t // tm,),
        in_specs=[pl.BlockSpec((tm, MEM_W), lambda i: (i, q0)), pl.BlockSpec((MEM_LEN, 2 * MEM_W), lambda i: (0, 0)),
                  pl.BlockSpec((tm, MEM_W), lambda i: (i, 0))],
        out_specs=[pl.BlockSpec((tm, MEM_W), lambda i: (i, 0)), pl.BlockSpec((MEM_LEN, 2 * MEM_W), lambda i: (0, 0))],
        out_shape=[jax.ShapeDtypeStruct((t, MEM_W), BF16), jax.ShapeDtypeStruct((MEM_LEN, 2 * MEM_W), F32)],
        compiler_params=_cparams("arbitrary"),
    )(p, kv, dy)


def _merge_fwd(p, b_ssd, b_gmlp, b_mem, name):
    t = p.shape[0]
    tm = _row_tile(t)
    g0 = P_GL // D_MODEL

    def body(g1, g2, g3, b1, b2, b3, o_ref):
        def strip(rows, carry):
            acc = _sigmoid(g1[rows, :].astype(F32)) * b1[rows, :].astype(F32)
            acc = acc + _sigmoid(g2[rows, :].astype(F32)) * b2[rows, :].astype(F32)
            acc = acc + _sigmoid(g3[rows, :].astype(F32)) * b3[rows, :].astype(F32)
            o_ref[rows, :] = acc.astype(o_ref.dtype)
            return carry

        _strips(tm, strip, 0)

    row = pl.BlockSpec((tm, D_MODEL), lambda i: (i, 0))
    return pl.pallas_call(
        body, name=name, grid=(t // tm,),
        in_specs=[pl.BlockSpec((tm, D_MODEL), lambda i, k=k: (i, g0 + k)) for k in range(3)] + [row] * 3,
        out_specs=row, out_shape=jax.ShapeDtypeStruct((t, D_MODEL), BF16), compiler_params=_cparams("parallel"),
    )(p, p, p, b_ssd, b_gmlp, b_mem)


def _merge_bwd(p, dm, b_ssd, b_gmlp, b_mem, name):
    t = p.shape[0]
    tm = _row_tile(t)
    g0 = P_GL // D_MODEL

    def body(g1, g2, g3, dm_ref, b1, b2, b3, d1, d2, d3, dgl_ref):
        def strip(rows, carry):
            dmv = dm_ref[rows, :].astype(F32)
            for k, (g_ref, b_ref, d_ref) in enumerate(((g1, b1, d1), (g2, b2, d2), (g3, b3, d3))):
                sg = _sigmoid(g_ref[rows, :].astype(F32))
                d_ref[rows, :] = (dmv * sg).astype(d_ref.dtype)
                dgl_ref[rows, k * D_MODEL:(k + 1) * D_MODEL] = (dmv * b_ref[rows, :].astype(F32) * sg * (1.0 - sg)).astype(dgl_ref.dtype)
            return carry

        _strips(tm, strip, 0)

    row = pl.BlockSpec((tm, D_MODEL), lambda i: (i, 0))
    return pl.pallas_call(
        body, name=name, grid=(t // tm,),
        in_specs=[pl.BlockSpec((tm, D_MODEL), lambda i, k=k: (i, g0 + k)) for k in range(3)] + [row] * 4,
        out_specs=[row, row, row, pl.BlockSpec((tm, 3 * D_MODEL), lambda i: (i, 0))],
        out_shape=[jax.ShapeDtypeStruct((t, D_MODEL), BF16)] * 3 + [jax.ShapeDtypeStruct((t, 3 * D_MODEL), BF16)],
        compiler_params=_cparams("parallel"),
    )(p, p, p, dm, b_ssd, b_gmlp, b_mem)


def _local_step(x, mem, target, w, rest_weights, push):
    t = x.shape[0]
    mm = functools.partial(_matmul, tk=1024)

    h1, ffn1_saved = _ffn_forward(x, w["ffn1_norm"], w["ffn1_w_gate"], w["ffn1_w_up"], w["ffn1_w_down"], "ffn1")
    w = {**w, **rest_weights(h1)}
    n2 = _rms_fwd(h1, w["mix_norm"], "mix_norm")
    p = mm(n2, w["w_in_p"], mode="nn", out_dtype=BF16, tm=512, tn=1536, name="in_proj")
    dt_raw = mm(n2, w["w_dt"], mode="nn", out_dtype=F32, tm=512, tn=128, name="dt_proj")
    dt_bias = jnp.pad(w["ssd_dt_bias"], (0, 128 - SSD_HEADS)).reshape(1, 128)
    hv = _head_inputs(_dt_fwd(dt_raw, dt_bias, "dt_fwd"), w["ssd_a_log"], w["ssd_d"])
    xc = _conv_fwd(p, w["ssd_conv_w"], w["ssd_conv_b"], "conv_fwd")
    y_ssd_raw, y_ssd, states = _ssd_fwd(xc, p, hv, w["ssd_norm"], "ssd_fwd")
    b_col = w["gmlp_b_s"].reshape(GMLP_GROUPS, CHUNK, 1)
    y_gmlp = _gmlp_fwd(p, w["gmlp_v_norm"], w["gmlp_w_s"], b_col, "gmlp_fwd")
    mem_n = _rms_fwd(mem, w["mem_norm"], "mem_norm")
    kv = mm(mem_n, w["w_mem_kv"], mode="nn", out_dtype=F32, tm=256, tn=512, name="mem_kv")
    y_mem = _mem_fwd(p, kv, "mem_fwd")
    b_ssd = mm(y_ssd, w["w_branch_ssd"], mode="nn", out_dtype=BF16, tm=512, tn=1024, name="branch_ssd")
    b_gmlp = mm(y_gmlp, w["w_branch_gmlp"], mode="nn", out_dtype=BF16, tm=512, tn=1024, name="branch_gmlp")
    b_mem = mm(y_mem, w["w_branch_mem"], mode="nn", out_dtype=BF16, tm=512, tn=1024, name="branch_mem")
    merged = _merge_fwd(p, b_ssd, b_gmlp, b_mem, "merge_fwd")
    h2 = mm(merged, w["w_out"], mode="nn", out_dtype=F32, tm=512, tn=1024, addend=h1, name="out_proj")
    h3, ffn2_saved = _ffn_forward(h2, w["ffn2_norm"], w["ffn2_w_gate"], w["ffn2_w_up"], w["ffn2_w_down"], "ffn2")
    dh3, d_final, loss = _loss_head(h3, w["final_norm"], target, "loss_head")

    g = {"final_norm": d_final}
    big = {}
    dh2, gf = _ffn_backward(dh3, h2, w["ffn2_norm"], w["ffn2_w_gate"], w["ffn2_w_up"], w["ffn2_w_down"], ffn2_saved, "ffn2")
    g["ffn2_norm"] = gf.pop("norm")
    push(0, {f"ffn2_{k}": v for k, v in gf.items()})
    dmerged = mm(dh2, w["w_out"], mode="nt", out_dtype=BF16, tm=512, tn=1024, name="out_proj_dx")
    big["w_out"] = mm(merged, dh2, mode="tn", out_dtype=BF16, tm=1024, tn=1024, name="out_proj_dw")
    db_ssd, db_gmlp, db_mem, dgl = _merge_bwd(p, dmerged, b_ssd, b_gmlp, b_mem, "merge_bwd")
    dy_ssd = mm(db_ssd, w["w_branch_ssd"], mode="nt", out_dtype=BF16, tm=512, tn=2048, name="branch_ssd_dx")
    dy_gmlp = mm(db_gmlp, w["w_branch_gmlp"], mode="nt", out_dtype=BF16, tm=512, tn=1024, name="branch_gmlp_dx")
    dy_mem = mm(db_mem, w["w_branch_mem"], mode="nt", out_dtype=BF16, tm=512, tn=256, name="branch_mem_dx")
    big["w_branch_ssd"] = mm(y_ssd, db_ssd, mode="tn", out_dtype=BF16, tm=1024, tn=1024, name="branch_ssd_dw")
    big["w_branch_gmlp"] = mm(y_gmlp, db_gmlp, mode="tn", out_dtype=BF16, tm=1024, tn=1024, name="branch_gmlp_dw")
    big["w_branch_mem"] = mm(y_mem, db_mem, mode="tn", out_dtype=BF16, tm=256, tn=1024, name="branch_mem_dw")
    dq, dkv = _mem_bwd(p, kv, dy_mem, "mem_bwd")
    big["w_mem_kv"] = mm(mem_n, dkv, mode="tn", out_dtype=BF16, tm=1024, tn=512, tk=256, name="mem_kv_dw")
    dmem_n = mm(dkv, w["w_mem_kv"], mode="nt", out_dtype=F32, tm=256, tn=1024, tk=512, name="mem_kv_dx")
    _, g["mem_norm"] = _rms_bwd(mem, w["mem_norm"], dmem_n, None, "mem_norm_bwd")
    duv, d_ws, d_bs, g["gmlp_v_norm"] = _gmlp_bwd(p, dy_gmlp, w["gmlp_v_norm"], w["gmlp_w_s"], b_col, "gmlp_bwd")
    g["gmlp_w_s"] = d_ws
    g["gmlp_b_s"] = d_bs.reshape(GMLP_GROUPS, CHUNK)
    dxs, d_bm, d_cm, dz, ddt_w, hsums, g["ssd_norm"] = _ssd_bwd(xc, p, hv, w["ssd_norm"], y_ssd_raw, dy_ssd, states, "ssd_bwd")
    heads = hsums[:, :, :HEADS_PER_GROUP]
    g["ssd_a_log"] = heads[:, 1, :].reshape(1, SSD_HEADS)
    g["ssd_d"] = heads[:, 2, :].reshape(1, SSD_HEADS)
    ddt = jnp.transpose(ddt_w[:, :, :HEADS_PER_GROUP], (1, 0, 2)).reshape(t, SSD_HEADS)
    ddt, d_bias = _dt_bwd(jnp.pad(ddt, ((0, 0), (0, 128 - SSD_HEADS))), dt_raw, dt_bias, "dt_bwd")
    g["ssd_dt_bias"] = d_bias[:, :SSD_HEADS]
    parts, dws, dbs = [], [], []
    for dyc, col0, tag in ((dxs, 0, "x"), (d_bm, SSD_INNER, "b"), (d_cm, SSD_INNER + SSD_GROUPS * SSD_STATE, "c")):
        dacc, dw_c, db_c = _conv_bwd_act(p, dyc, w["ssd_conv_w"], w["ssd_conv_b"], col0, f"conv_bwd_act_{tag}")
        parts.append(_conv_bwd_dx(dacc, w["ssd_conv_w"], col0, f"conv_bwd_dx_{tag}"))
        dws.append(dw_c)
        dbs.append(db_c)
    g["ssd_conv_w"] = jnp.concatenate(dws, axis=1)
    g["ssd_conv_b"] = jnp.concatenate(dbs, axis=1)
    dp = jnp.concatenate([dz, duv] + parts + [dgl, dq, jnp.zeros((t, P_W - P_USED), BF16)], axis=1)
    d_win_p = mm(n2, dp, mode="tn", out_dtype=BF16, tm=1024, tn=1536, name="in_proj_dw")
    d_wdt = mm(n2, ddt, mode="tn", out_dtype=BF16, tm=1024, tn=128, name="dt_proj_dw")
    sl = lambda a, o, n: a[:, o:o + n]
    big["w_in"] = jnp.concatenate([sl(d_win_p, P_Z, 2048), sl(d_win_p, P_XBC, XBC), d_wdt[:, :SSD_HEADS], sl(d_win_p, P_UV, 2048),
                                   sl(d_win_p, P_Q, MEM_W), sl(d_win_p, P_GL, 3 * D_MODEL)], axis=1)
    push(1, big)
    dn2 = mm(dp, w["w_in_p"], mode="nt", out_dtype=F32, tm=512, tn=1024, tk=1536, name="in_proj_dx")
    dn2 = _matmul(ddt, w["w_dt"], mode="nt", out_dtype=F32, tm=512, tn=1024, tk=128, addend=dn2, name="dt_proj_dx")
    dh1, g["mix_norm"] = _rms_bwd(h1, w["mix_norm"], dn2, dh2, "mix_norm_bwd")
    dx, gf = _ffn_backward(dh1, x, w["ffn1_norm"], w["ffn1_w_gate"], w["ffn1_w_up"], w["ffn1_w_down"], ffn1_saved, "ffn1")
    g["ffn1_norm"] = gf.pop("norm")
    push(2, {f"ffn1_{k}": v for k, v in gf.items()})
    return loss, dx, g


def _split_w_in(w_in):
    sl = lambda o, n: w_in[:, o:o + n]
    w_p = jnp.concatenate([sl(IN_Z, 2048), sl(IN_UV, 2048), sl(IN_XBC, XBC), sl(IN_GL, 3 * D_MODEL), sl(IN_Q, MEM_W),
                           jnp.zeros((D_MODEL, P_W - P_USED), w_in.dtype)], axis=1)
    w_dt = jnp.pad(sl(IN_DT, SSD_HEADS), ((0, 0), (0, 128 - SSD_HEADS)))
    return w_p, w_dt


def _pick_tile(rows, cap=512):
    best = None
    for tile in range(8, min(rows, cap) + 1, 8):
        if rows % tile == 0:
            best = tile
    return best if best is not None else rows


def _adamw(w, g, m, v, name):
    rows, lanes = w.shape
    tile = _pick_tile(rows, cap=max(8, (512 * 1024 // lanes) // 8 * 8))
    c1 = 1.0 / (1.0 - ADAM_B1 ** ADAM_STEP)
    c2 = 1.0 / (1.0 - ADAM_B2 ** ADAM_STEP)

    def body(w_ref, g_ref, m_ref, v_ref, d_ref, nm_ref, nv_ref):
        gv = g_ref[...]
        nm = ADAM_B1 * m_ref[...] + (1.0 - ADAM_B1) * gv
        nv = ADAM_B2 * v_ref[...] + (1.0 - ADAM_B2) * (gv * gv)
        nm_ref[...] = nm
        nv_ref[...] = nv
        d_ref[...] = -ADAM_LR * ((nm * c1) / (jnp.sqrt(nv * c2) + ADAM_EPS) + ADAM_WD * w_ref[...])

    blk = pl.BlockSpec((tile, lanes), lambda i: (i, 0))
    return pl.pallas_call(
        body, name=name, grid=(rows // tile,), in_specs=[blk] * 4, out_specs=[blk] * 3,
        out_shape=[jax.ShapeDtypeStruct((rows, lanes), F32)] * 3, compiler_params=_cparams("parallel"),
    )(w, g, m, v)


HBM = pl.BlockSpec(memory_space=pltpu.HBM)


def _place():
    x, y, c = lax.axis_index("x"), lax.axis_index("y"), lax.axis_index("c")
    chips = [(1 - x, y), (x, 1 - y), (1 - x, 1 - y)]
    return x, y, c, chips


def _gather_weights(slots):
    _, _, rh, lanes = slots.shape

    def body(in_ref, out_ref, send_sems, recv_sems):
        del in_ref
        x, y, c, chips = _place()
        me, sibling = (x, y, c), (x, y, 1 - c)

        def copy(k, src, dst, to):
            return pltpu.make_async_remote_copy(src_ref=src, dst_ref=dst, send_sem=send_sems.at[k], recv_sem=recv_sems.at[k],
                                                device_id=to, device_id_type=MESH)

        own = out_ref.at[2 * x + y, c]
        first = [copy(j, own, own, (*chip, c)) for j, chip in enumerate(chips)]
        for cp in first:
            cp.start()
        passed = []
        for j, (cx, cy) in enumerate(chips):
            landed = out_ref.at[2 * cx + cy, c]
            copy(j, landed, landed, me).wait_recv()
            fwd = copy(3 + j, landed, landed, sibling)
            fwd.start()
            passed.append(fwd)
        for j, (cx, cy) in enumerate(chips):
            other = out_ref.at[2 * cx + cy, 1 - c]
            copy(3 + j, other, other, me).wait_recv()
        for cp in first + passed:
            cp.wait_send()

    return pl.pallas_call(
        body, name="gather_weights", out_shape=jax.ShapeDtypeStruct(slots.shape, slots.dtype),
        in_specs=[HBM], out_specs=HBM, input_output_aliases={0: 0},
        scratch_shapes=[pltpu.SemaphoreType.DMA((6,)), pltpu.SemaphoreType.DMA((6,))],
    )(slots)


SEM = pl.BlockSpec(memory_space=pltpu.SEMAPHORE)
EFFECT = pltpu.SideEffectType.DATAFLOW_SIDE_EFFECTING
N_PEER = 3


def _sem_outs():
    return tuple(pltpu.SemaphoreType.DMA(()) for _ in range(2 * N_PEER))


def _gather_start(slots):
    def body(in_ref, *refs):
        del in_ref
        sems, thru, token = refs[:2 * N_PEER], refs[2 * N_PEER], refs[2 * N_PEER + 1]
        x, y, c, chips = _place()
        own = thru.at[2 * x + y, c]
        for j, chip in enumerate(chips):
            pltpu.make_async_remote_copy(src_ref=own, dst_ref=own, send_sem=sems[j], recv_sem=sems[N_PEER + j],
                                         device_id=(*chip, c), device_id_type=MESH).start()
        token[...] = jnp.zeros_like(token)

    out = pl.pallas_call(
        body, name="gather_rest_start",
        out_shape=_sem_outs() + (pltpu.HBM(slots.shape, slots.dtype), jax.ShapeDtypeStruct((8, 128), F32)),
        in_specs=(HBM,), out_specs=(SEM,) * (2 * N_PEER) + (HBM, pl.BlockSpec(memory_space=pltpu.VMEM)),
        input_output_aliases={0: 2 * N_PEER}, compiler_params=pltpu.CompilerParams(has_side_effects=EFFECT),
    )(pltpu.with_memory_space_constraint(slots, pltpu.HBM))
    return out[:2 * N_PEER], out[2 * N_PEER], out[2 * N_PEER + 1]


def _gather_wait(sems, thru, after):
    def body(in_ref, *refs):
        del in_ref
        sems, out_ref = refs[:2 * N_PEER], refs[2 * N_PEER + 1]
        x, y, c, chips = _place()
        own = out_ref.at[2 * x + y, c]
        for j, (cx, cy) in enumerate(chips):
            cp = pltpu.make_async_remote_copy(src_ref=own, dst_ref=out_ref.at[2 * cx + cy, c], send_sem=sems[j],
                                              recv_sem=sems[N_PEER + j], device_id=(cx, cy, c), device_id_type=MESH)
            cp.wait_send()
            cp.wait_recv()

    return pl.pallas_call(
        body, name="gather_rest_wait", out_shape=pltpu.HBM(thru.shape, thru.dtype),
        in_specs=(HBM,) + (SEM,) * (2 * N_PEER) + (pl.BlockSpec(memory_space=pl.ANY),), out_specs=HBM,
        input_output_aliases={0: 0}, compiler_params=pltpu.CompilerParams(has_side_effects=EFFECT),
    )(thru, *sems, after)


def _gather_forward(slots):
    def body(in_ref, out_ref, send_sems, recv_sems):
        del in_ref
        x, y, c, chips = _place()
        cps = []
        for j, (cx, cy) in enumerate(chips):
            landed = out_ref.at[2 * cx + cy, c]
            cps.append(pltpu.make_async_remote_copy(src_ref=landed, dst_ref=landed, send_sem=send_sems.at[j], recv_sem=recv_sems.at[j],
                                                    device_id=(x, y, 1 - c), device_id_type=MESH))
        for cp in cps:
            cp.start()
        for j, (cx, cy) in enumerate(chips):
            other = out_ref.at[2 * cx + cy, 1 - c]
            pltpu.make_async_remote_copy(src_ref=other, dst_ref=other, send_sem=send_sems.at[j], recv_sem=recv_sems.at[j],
                                         device_id=(x, y, 1 - c), device_id_type=MESH).wait_recv()
        for cp in cps:
            cp.wait_send()

    return pl.pallas_call(
        body, name="gather_rest_forward", out_shape=jax.ShapeDtypeStruct(slots.shape, slots.dtype),
        in_specs=[HBM], out_specs=HBM, input_output_aliases={0: 0},
        scratch_shapes=[pltpu.SemaphoreType.DMA((N_PEER,)), pltpu.SemaphoreType.DMA((N_PEER,))],
    )(slots)


def _scatter_start(pa, tag):
    ns, rh, lanes = pa.shape
    land = pltpu.with_memory_space_constraint(lax.empty((N_PEER, rh, lanes), pa.dtype), pltpu.HBM)

    def body(pa_ref, land_ref, *refs):
        x, y, c, chips = _place()
        for j, (cx, cy) in enumerate(chips):
            pltpu.make_async_remote_copy(src_ref=pa_ref.at[2 * cx + cy], dst_ref=land_ref.at[j], send_sem=refs[j],
                                         recv_sem=refs[N_PEER + j], device_id=(cx, cy, c), device_id_type=MESH).start()

    out = pl.pallas_call(
        body, name=f"scatter_start_{tag}",
        out_shape=_sem_outs() + (pltpu.HBM(pa.shape, pa.dtype), pltpu.HBM(land.shape, land.dtype)),
        in_specs=(HBM, HBM), out_specs=(SEM,) * (2 * N_PEER) + (HBM, HBM),
        input_output_aliases={0: 2 * N_PEER, 1: 2 * N_PEER + 1}, compiler_params=pltpu.CompilerParams(has_side_effects=EFFECT),
    )(pltpu.with_memory_space_constraint(pa, pltpu.HBM), land)
    return out[:2 * N_PEER], out[2 * N_PEER], out[2 * N_PEER + 1]


def _scatter_wait(sems, pa_thru, land_thru, after, tag):
    def body(pa_ref, land_ref, *refs):
        sems = refs[:2 * N_PEER]
        x, y, c, chips = _place()
        for j, (cx, cy) in enumerate(chips):
            cp = pltpu.make_async_remote_copy(src_ref=pa_ref.at[2 * cx + cy], dst_ref=land_ref.at[j], send_sem=sems[j],
                                              recv_sem=sems[N_PEER + j], device_id=(cx, cy, c), device_id_type=MESH)
            cp.wait_send()
            cp.wait_recv()

    return pl.pallas_call(
        body, name=f"scatter_wait_{tag}",
        out_shape=(pltpu.HBM(pa_thru.shape, pa_thru.dtype), pltpu.HBM(land_thru.shape, land_thru.dtype)),
        in_specs=(HBM, HBM) + (SEM,) * (2 * N_PEER) + (pl.BlockSpec(memory_space=pl.ANY),), out_specs=(HBM, HBM),
        input_output_aliases={0: 0, 1: 1}, compiler_params=pltpu.CompilerParams(has_side_effects=EFFECT),
    )(pa_thru, land_thru, *sems, after)


def _rs_swap(gp, tag):
    _, ns, rh, lanes = gp.shape

    def body(in_ref, out_ref, send_sem, recv_sem):
        x, y, c, _ = _place()
        cp = pltpu.make_async_remote_copy(src_ref=in_ref.at[1 - c], dst_ref=out_ref, send_sem=send_sem, recv_sem=recv_sem,
                                          device_id=(x, y, 1 - c), device_id_type=MESH)
        cp.start()
        cp.wait_send()
        cp.wait_recv()

    return pl.pallas_call(
        body, name=f"rs_swap_{tag}", out_shape=jax.ShapeDtypeStruct((ns, rh, lanes), gp.dtype), in_specs=[HBM], out_specs=HBM,
        scratch_shapes=[pltpu.SemaphoreType.DMA, pltpu.SemaphoreType.DMA],
    )(gp)


def _rs_tile(rh):
    return _pick_tile(rh, cap=512)


def _rs_add(gp, recv, c, tag):
    _, ns, rh, lanes = gp.shape
    tile = _rs_tile(rh)

    def body(c_ref, a_ref, b_ref, o_ref):
        o_ref[...] = (a_ref[...].astype(F32) + b_ref[...].astype(F32)).astype(o_ref.dtype)

    return pl.pallas_call(
        body, name=f"rs_add_{tag}", out_shape=jax.ShapeDtypeStruct((ns, rh, lanes), gp.dtype),
        grid_spec=pltpu.PrefetchScalarGridSpec(
            num_scalar_prefetch=1, grid=(ns, rh // tile),
            in_specs=[pl.BlockSpec((None, None, tile, lanes), lambda s, i, c_ref: (c_ref[0], s, i, 0)),
                      pl.BlockSpec((None, tile, lanes), lambda s, i, c_ref: (s, i, 0))],
            out_specs=pl.BlockSpec((None, tile, lanes), lambda s, i, c_ref: (s, i, 0))),
        compiler_params=_cparams("parallel", "parallel"),
    )(c, gp, recv)


def _rs_sum(pa, recv, place, tag):
    ns, rh, lanes = pa.shape
    tile = _rs_tile(rh)

    def body(place_ref, a_ref, r_ref, o_ref):
        acc = a_ref[...].astype(F32)
        for j in range(ns - 1):
            acc = acc + r_ref[j].astype(F32)
        o_ref[...] = acc

    return pl.pallas_call(
        body, name=f"rs_sum_{tag}", out_shape=jax.ShapeDtypeStruct((2, rh, lanes), F32),
        grid_spec=pltpu.PrefetchScalarGridSpec(
            num_scalar_prefetch=1, grid=(rh // tile,),
            in_specs=[pl.BlockSpec((None, tile, lanes), lambda i, place_ref: (place_ref[0], i, 0)),
                      pl.BlockSpec((ns - 1, tile, lanes), lambda i, place_ref: (0, i, 0))],
            out_specs=pl.BlockSpec((None, tile, lanes), lambda i, place_ref: (place_ref[1], i, 0))),
        compiler_params=_cparams("parallel"),
    )(place, pa, recv)


def _rs_share(halves, tag):
    def body(in_ref, out_ref, send_sem, recv_sem):
        del in_ref
        x, y, c, _ = _place()
        cp = pltpu.make_async_remote_copy(src_ref=out_ref.at[c], dst_ref=out_ref.at[c], send_sem=send_sem, recv_sem=recv_sem,
                                          device_id=(x, y, 1 - c), device_id_type=MESH)
        cp.start()
        other = out_ref.at[1 - c]
        pltpu.make_async_remote_copy(src_ref=other, dst_ref=other, send_sem=send_sem, recv_sem=recv_sem,
                                     device_id=(x, y, 1 - c), device_id_type=MESH).wait_recv()
        cp.wait_send()

    return pl.pallas_call(
        body, name=f"rs_share_{tag}", out_shape=jax.ShapeDtypeStruct(halves.shape, halves.dtype), in_specs=[HBM], out_specs=HBM,
        input_output_aliases={0: 0}, scratch_shapes=[pltpu.SemaphoreType.DMA, pltpu.SemaphoreType.DMA],
    )(halves)


N_DEV = 8
SMALL_ROWS = 160


def _allreduce_small(v):
    m_per, n = v.shape

    def body(x_ref, out_ref, all_ref, send_sems, recv_sems, local_sem):
        x, y, c, chips = _place()
        me, sibling = (x, y, c), (x, y, 1 - c)

        def rows(px, py, pc):
            return all_ref.at[pl.ds((4 * px + 2 * py + pc) * m_per, m_per), :]

        def copy(k, block, to, src=None):
            return pltpu.make_async_remote_copy(src_ref=rows(*block) if src is None else src, dst_ref=rows(*block),
                                                send_sem=send_sems.at[k], recv_sem=recv_sems.at[k], device_id=to, device_id_type=MESH)

        mine = pltpu.make_async_copy(x_ref, rows(*me), local_sem)
        mine.start()
        first = [copy(0, me, sibling, src=x_ref)]
        first += [copy(1 + j, me, (*chip, c), src=x_ref) for j, chip in enumerate(chips)]
        for cp in first:
            cp.start()
        passed = [copy(4 + j, (*chip, c), sibling) for j, chip in enumerate(chips)]
        for j, chip in enumerate(chips):
            copy(1 + j, (*chip, c), me).wait_recv()
            passed[j].start()
        copy(0, sibling, me).wait_recv()
        for j, chip in enumerate(chips):
            copy(4 + j, (*chip, 1 - c), me).wait_recv()
        for cp in first + passed:
            cp.wait_send()
        mine.wait()
        step = 32
        for r in range(0, m_per, step):
            acc = all_ref[r:r + step, :]
            for d in range(1, N_DEV):
                acc = acc + all_ref[d * m_per + r:d * m_per + r + step, :]
            out_ref[r:r + step, :] = acc

    vm = pl.BlockSpec(memory_space=pltpu.VMEM)
    return pl.pallas_call(
        body, name="allreduce_small", out_shape=jax.ShapeDtypeStruct((m_per, n), v.dtype), in_specs=[vm], out_specs=vm,
        scratch_shapes=[pltpu.VMEM((N_DEV * m_per, n), v.dtype), pltpu.SemaphoreType.DMA((7,)), pltpu.SemaphoreType.DMA((7,)),
                        pltpu.SemaphoreType.DMA],
        compiler_params=pltpu.CompilerParams(vmem_limit_bytes=V7X_VMEM_LIMIT),
    )(v)


BIG = {"ffn1_w_gate": ((D_MODEL, D_FF), 1), "ffn1_w_up": ((D_MODEL, D_FF), 1), "ffn1_w_down": ((D_FF, D_MODEL), 0),
       "ffn2_w_gate": ((D_MODEL, D_FF), 1), "ffn2_w_up": ((D_MODEL, D_FF), 1), "ffn2_w_down": ((D_FF, D_MODEL), 0),
       "w_in": ((D_MODEL, IN_WIDTH), 1), "w_mem_kv": ((D_MODEL, 2 * MEM_W), 0), "w_branch_ssd": ((SSD_INNER, D_MODEL), 0),
       "w_branch_gmlp": ((GMLP_W, D_MODEL), 0), "w_branch_mem": ((MEM_W, D_MODEL), 1), "w_out": ((D_MODEL, D_MODEL), 0)}
FFN1 = ("ffn1_w_gate", "ffn1_w_up", "ffn1_w_down")
FFN2 = ("ffn2_w_gate", "ffn2_w_up", "ffn2_w_down")
MIXER = ("w_out", "w_branch_ssd", "w_branch_gmlp", "w_branch_mem", "w_mem_kv", "w_in")
GATHER_GROUPS = (FFN1, FFN2 + MIXER)
REDUCE_GROUPS = (FFN2, MIXER, FFN1)
CONV_W_ROWS = 8


def _shard_rows_of(name):
    (a, b), _ = BIG[name]
    return a * b // N_SHARD // LANES


def _group_rows(names, extra=0):
    return -(-(sum(_shard_rows_of(n) for n in names) + extra) // 32) * 32

SMALL = [("ffn1_norm", 1), ("mix_norm", 1), ("mem_norm", 1), ("ssd_conv_b", 3), ("heads", 1), ("ssd_norm", 2),
         ("gmlp_v_norm", 1), ("gmlp_w_s", 128), ("gmlp_b_s", 1), ("ffn2_norm", 1), ("final_norm", 1), ("ssd_conv_w", 12)]
assert sum(n for _, n in SMALL) <= SMALL_ROWS
HEAD_VECS = ("ssd_dt_bias", "ssd_a_log", "ssd_d")


def _pack_small(vals, loss=None):
    parts = []
    for name, nrows in SMALL:
        if name == "heads":
            row = jnp.concatenate([vals[k].reshape(-1) for k in HEAD_VECS]
                                  + [jnp.zeros((1,), F32) if loss is None else loss.reshape(1)])
            parts.append(jnp.pad(row, (0, LANES - row.shape[0])).reshape(1, LANES))
        elif name in vals:
            parts.append(vals[name].reshape(nrows, LANES))
        else:
            parts.append(jnp.zeros((nrows, LANES), F32))
    buf = jnp.concatenate(parts, axis=0)
    return jnp.pad(buf, ((0, SMALL_ROWS - buf.shape[0]), (0, 0)))


def _unpack_small(buf):
    out, r = {}, 0
    for name, nrows in SMALL:
        blk = buf[r:r + nrows]
        r += nrows
        if name == "heads":
            for i, k in enumerate(HEAD_VECS):
                out[k] = blk[0, i * SSD_HEADS:(i + 1) * SSD_HEADS]
            out["loss"] = blk[0, 3 * SSD_HEADS]
        else:
            out[name] = blk
    return out


def _shard_rows(full, axis):
    a, b = full.shape
    if axis == 1:
        full = jnp.transpose(full.reshape(a, N_SHARD, b // N_SHARD), (1, 0, 2))
    return full.reshape(N_SHARD, a * b // N_SHARD // LANES, LANES)


def _unshard_rows(rows, shape, axis):
    a, b = shape
    if axis == 1:
        return jnp.transpose(rows.reshape(N_SHARD, a, b // N_SHARD), (1, 0, 2)).reshape(a, b)
    return rows.reshape(a, b)


def _pack_weights(given, names, conv=False):
    parts = [given[n].astype(BF16).reshape(_shard_rows_of(n), LANES) for n in names]
    if conv:
        pairs = lax.bitcast_convert_type(given["ssd_conv_w"], BF16).reshape(-1)
        parts.append(jnp.pad(pairs, (0, CONV_W_ROWS * LANES - pairs.shape[0])).reshape(CONV_W_ROWS, LANES))
    total = _group_rows(names, CONV_W_ROWS if conv else 0)
    packed = jnp.concatenate(parts, axis=0)
    packed = jnp.pad(packed, ((0, total - packed.shape[0]), (0, 0))).reshape(1, 2, total // 2, LANES)
    return jnp.broadcast_to(packed, (N_SHARD, 2, total // 2, LANES))


def _unpack_weights(slots, names, conv=False):
    rows = slots.reshape(N_SHARD, -1, LANES)
    out, r = {}, 0
    for name in names:
        n = _shard_rows_of(name)
        out[name] = _unshard_rows(rows[:, r:r + n], *BIG[name])
        r += n
    if conv:
        cols = XBC // N_SHARD
        pairs = rows[:, r:r + CONV_W_ROWS].reshape(N_SHARD, -1)[:, :SSD_CONV * cols * 2].reshape(N_SHARD, SSD_CONV, cols, 2)
        out["ssd_conv_w"] = jnp.transpose(lax.bitcast_convert_type(pairs, F32), (1, 0, 2)).reshape(SSD_CONV, XBC)
    return out


def _pack_grads(grads, names):
    total = _group_rows(names)
    gp = jnp.concatenate([_shard_rows(grads[n], BIG[n][1]).astype(BF16) for n in names], axis=1)
    gp = jnp.pad(gp, ((0, 0), (0, total - gp.shape[1]), (0, 0))).reshape(N_SHARD, 2, total // 2, LANES)
    return jnp.transpose(gp, (1, 0, 2, 3))


def kernel(x, mem, ffn1_norm, ffn1_w_gate, ffn1_w_up, ffn1_w_down, mix_norm, mem_norm, w_in, ssd_conv_w, ssd_conv_b, ssd_dt_bias, ssd_a_log, ssd_d, ssd_norm, gmlp_v_norm, gmlp_w_s, gmlp_b_s, w_mem_kv, w_branch_ssd, w_branch_gmlp, w_branch_mem, w_out, ffn2_norm, ffn2_w_gate, ffn2_w_up, ffn2_w_down, final_norm, loss_target, m_ffn1_norm, m_ffn1_w_gate, m_ffn1_w_up, m_ffn1_w_down, m_mix_norm, m_mem_norm, m_w_in, m_ssd_conv_w, m_ssd_conv_b, m_ssd_dt_bias, m_ssd_a_log, m_ssd_d, m_ssd_norm, m_gmlp_v_norm, m_gmlp_w_s, m_gmlp_b_s, m_w_mem_kv, m_w_branch_ssd, m_w_branch_gmlp, m_w_branch_mem, m_w_out, m_ffn2_norm, m_ffn2_w_gate, m_ffn2_w_up, m_ffn2_w_down, m_final_norm, v_ffn1_norm, v_ffn1_w_gate, v_ffn1_w_up, v_ffn1_w_down, v_mix_norm, v_mem_norm, v_w_in, v_ssd_conv_w, v_ssd_conv_b, v_ssd_dt_bias, v_ssd_a_log, v_ssd_d, v_ssd_norm, v_gmlp_v_norm, v_gmlp_w_s, v_gmlp_b_s, v_w_mem_kv, v_w_branch_ssd, v_w_branch_gmlp, v_w_branch_mem, v_w_out, v_ffn2_norm, v_ffn2_w_gate, v_ffn2_w_up, v_ffn2_w_down, v_final_norm):
    given = dict(x=x, mem=mem, ffn1_norm=ffn1_norm, ffn1_w_gate=ffn1_w_gate, ffn1_w_up=ffn1_w_up, ffn1_w_down=ffn1_w_down, mix_norm=mix_norm, mem_norm=mem_norm, w_in=w_in, ssd_conv_w=ssd_conv_w, ssd_conv_b=ssd_conv_b, ssd_dt_bias=ssd_dt_bias, ssd_a_log=ssd_a_log, ssd_d=ssd_d, ssd_norm=ssd_norm, gmlp_v_norm=gmlp_v_norm, gmlp_w_s=gmlp_w_s, gmlp_b_s=gmlp_b_s, w_mem_kv=w_mem_kv, w_branch_ssd=w_branch_ssd, w_branch_gmlp=w_branch_gmlp, w_branch_mem=w_branch_mem, w_out=w_out, ffn2_norm=ffn2_norm, ffn2_w_gate=ffn2_w_gate, ffn2_w_up=ffn2_w_up, ffn2_w_down=ffn2_w_down, final_norm=final_norm, loss_target=loss_target, m_ffn1_norm=m_ffn1_norm, m_ffn1_w_gate=m_ffn1_w_gate, m_ffn1_w_up=m_ffn1_w_up, m_ffn1_w_down=m_ffn1_w_down, m_mix_norm=m_mix_norm, m_mem_norm=m_mem_norm, m_w_in=m_w_in, m_ssd_conv_w=m_ssd_conv_w, m_ssd_conv_b=m_ssd_conv_b, m_ssd_dt_bias=m_ssd_dt_bias, m_ssd_a_log=m_ssd_a_log, m_ssd_d=m_ssd_d, m_ssd_norm=m_ssd_norm, m_gmlp_v_norm=m_gmlp_v_norm, m_gmlp_w_s=m_gmlp_w_s, m_gmlp_b_s=m_gmlp_b_s, m_w_mem_kv=m_w_mem_kv, m_w_branch_ssd=m_w_branch_ssd, m_w_branch_gmlp=m_w_branch_gmlp, m_w_branch_mem=m_w_branch_mem, m_w_out=m_w_out, m_ffn2_norm=m_ffn2_norm, m_ffn2_w_gate=m_ffn2_w_gate, m_ffn2_w_up=m_ffn2_w_up, m_ffn2_w_down=m_ffn2_w_down, m_final_norm=m_final_norm, v_ffn1_norm=v_ffn1_norm, v_ffn1_w_gate=v_ffn1_w_gate, v_ffn1_w_up=v_ffn1_w_up, v_ffn1_w_down=v_ffn1_w_down, v_mix_norm=v_mix_norm, v_mem_norm=v_mem_norm, v_w_in=v_w_in, v_ssd_conv_w=v_ssd_conv_w, v_ssd_conv_b=v_ssd_conv_b, v_ssd_dt_bias=v_ssd_dt_bias, v_ssd_a_log=v_ssd_a_log, v_ssd_d=v_ssd_d, v_ssd_norm=v_ssd_norm, v_gmlp_v_norm=v_gmlp_v_norm, v_gmlp_w_s=v_gmlp_w_s, v_gmlp_b_s=v_gmlp_b_s, v_w_mem_kv=v_w_mem_kv, v_w_branch_ssd=v_w_branch_ssd, v_w_branch_gmlp=v_w_branch_gmlp, v_w_branch_mem=v_w_branch_mem, v_w_out=v_w_out, v_ffn2_norm=v_ffn2_norm, v_ffn2_w_gate=v_ffn2_w_gate, v_ffn2_w_up=v_ffn2_w_up, v_ffn2_w_down=v_ffn2_w_down, v_final_norm=v_final_norm)
    weights = [n for n in given if n not in ("x", "mem", "loss_target") and not n.startswith(("m_", "v_"))]
    xi, yi, ci = lax.axis_index("x"), lax.axis_index("y"), lax.axis_index("c")
    chip = (2 * xi + yi).astype(jnp.int32)
    core = ci.astype(jnp.int32)
    conv_cols = XBC // N_SHARD

    first = _gather_weights(_pack_weights(given, FFN1))
    first, rest_slots = lax.optimization_barrier((first, _pack_weights(given, GATHER_GROUPS[1], conv=True)))
    sems, thru, token = _gather_start(rest_slots)
    w = _unpack_weights(first, FFN1)
    for name in ("ffn1_norm", "mix_norm", "mem_norm", "ssd_conv_b", "ssd_norm", "gmlp_v_norm", "ffn2_norm", "final_norm"):
        w[name] = given[name].reshape(1, -1)
    w["ffn1_norm"] = w["ffn1_norm"] + token[0:1, 0:1]
    for name in HEAD_VECS:
        w[name] = given[name].reshape(-1)
    w["gmlp_w_s"] = given["gmlp_w_s"][0]
    w["gmlp_b_s"] = given["gmlp_b_s"][0]

    def rest_weights(after):
        rest = _unpack_weights(_gather_forward(_gather_wait(sems, thru, after)), GATHER_GROUPS[1], conv=True)
        rest["w_in_p"], rest["w_dt"] = _split_w_in(rest.pop("w_in"))
        return rest

    pending = {}

    def push(k, group_grads):
        gp = _pack_grads(group_grads, REDUCE_GROUPS[k])
        pa = _rs_add(gp, _rs_swap(gp, k), core.reshape(1), k)
        pending[k] = _scatter_start(pa, k)

    def reduced(k, after):
        pa, land = _scatter_wait(*pending[k], after, k)
        gsum = _rs_share(_rs_sum(pa, land, jnp.stack([chip, core]), k), k)
        rows = gsum.reshape(-1, LANES)
        out, r = {}, 0
        for name in REDUCE_GROUPS[k]:
            n = _shard_rows_of(name)
            out[name] = rows[r:r + n].reshape(given[name].shape[1:])
            r += n
        return out

    loss_part, grad_x, g = _local_step(x[0], mem[0], loss_target[0], w, rest_weights, push)

    grads, deltas, new_m, new_v = {}, {}, {}, {}

    def update(k, after):
        for name, gl in reduced(k, after).items():
            d, nm, nv = _adamw(given[name][0], gl, given["m_" + name][0], given["v_" + name][0], f"adamw_{name}")
            grads[name], deltas[name], new_m[name], new_v[name] = (a[None] for a in (gl, d, nm, nv))

    update(0, grad_x)
    update(1, deltas[REDUCE_GROUPS[0][-1]])

    small_vals = {k: g[k] for k, _ in SMALL if k != "heads"}
    small_vals.update({k: g[k] for k in HEAD_VECS})
    red = _unpack_small(_allreduce_small(_pack_small(small_vals, loss=loss_part[0, 0])))
    update(2, deltas[REDUCE_GROUPS[1][-1]])
    conv_g = lax.dynamic_slice_in_dim(red["ssd_conv_w"].reshape(SSD_CONV, XBC), chip * conv_cols, conv_cols, axis=1)
    d, nm, nv = _adamw(given["ssd_conv_w"][0], conv_g, given["m_ssd_conv_w"][0], given["v_ssd_conv_w"][0], "adamw_conv_w")
    grads["ssd_conv_w"], deltas["ssd_conv_w"], new_m["ssd_conv_w"], new_v["ssd_conv_w"] = (a[None] for a in (conv_g, d, nm, nv))
    small_names = [k for k, _ in SMALL if k not in ("heads", "ssd_conv_w")] + list(HEAD_VECS)
    pack_of = lambda prefix: _pack_small({k: given[prefix + k] for k in small_names})
    g_small = _pack_small({k: red[k] for k in small_names})
    d, nm, nv = (_unpack_small(a) for a in _adamw(pack_of(""), g_small, pack_of("m_"), pack_of("v_"), "adamw_small"))
    for k in small_names:
        shape = given[k].shape
        grads[k], deltas[k], new_m[k], new_v[k] = (a[k].reshape(shape) for a in (red, d, nm, nv))

    return (red["loss"], grad_x[None], *[grads[n] for n in weights], *[deltas[n] for n in weights],
            *[new_m[n] for n in weights], *[new_v[n] for n in weights])
```

```python
import functools
import math

import jax
import jax.numpy as jnp
from jax import lax
from jax.experimental import pallas as pl
from jax.experimental.pallas import tpu as pltpu

F32, BF16 = jnp.float32, jnp.bfloat16
HI = lax.Precision.HIGHEST
MESH = pl.DeviceIdType.MESH

D_MODEL = 1024
D_FF = 2816
MEM_LEN = 256
SSD_INNER = 2048
SSD_HEADS = 32
SSD_GROUPS = 4
SSD_STATE = 128
SSD_CONV = 4
CHUNK = 128
XBC = SSD_INNER + 2 * SSD_GROUPS * SSD_STATE
GMLP_W = 1024
GMLP_GROUPS = 8
MEM_W = 256
MEM_HEADS = 4
EPS = 1e-6
IN_WIDTH = 10528
IN_Z, IN_XBC, IN_DT, IN_UV, IN_Q, IN_GL = 0, 2048, 5120, 5152, 7200, 7456
P_Z, P_UV, P_XBC, P_GL, P_Q, P_W = 0, 2048, 4096, 7168, 10240, 10752
P_USED = 10496

ADAM_LR, ADAM_B1, ADAM_B2, ADAM_EPS, ADAM_WD, ADAM_STEP = 0.001, 0.9, 0.999, 1e-08, 0.01, 10

V7X_VMEM_LIMIT = 56 * 1024 * 1024
N_SHARD = 4
LANES = 1024


def _cparams(*sem):
    return pltpu.CompilerParams(dimension_semantics=sem, vmem_limit_bytes=V7X_VMEM_LIMIT)


def _sigmoid(x):
    return 1.0 / (1.0 + jnp.exp(-x))


def _row_tile(t):
    return min(512, t)


_DIMS = {"nn": (((1,), (0,)), ((), ())), "nt": (((1,), (1,)), ((), ())), "tn": (((0,), (0,)), ((), ()))}


def _matmul(a, b, *, mode, out_dtype, tm, tn, tk, name, scale=1.0, addend=None):
    if mode == "tn":
        k_dim, m_dim = a.shape
    else:
        m_dim, k_dim = a.shape
    n_dim = b.shape[0] if mode == "nt" else b.shape[1]
    tm, tn, tk = min(tm, m_dim), min(tn, n_dim), min(tk, k_dim)
    assert m_dim % tm == 0 and n_dim % tn == 0 and k_dim % tk == 0, (name, a.shape, b.shape, tm, tn, tk)
    ni, nj, nk = m_dim // tm, n_dim // tn, k_dim // tk
    a_spec = pl.BlockSpec((tk, tm), lambda j, i, k: (k, i)) if mode == "tn" else pl.BlockSpec((tm, tk), lambda j, i, k: (i, k))
    b_spec = pl.BlockSpec((tn, tk), lambda j, i, k: (j, k)) if mode == "nt" else pl.BlockSpec((tk, tn), lambda j, i, k: (k, j))
    o_spec = pl.BlockSpec((tm, tn), lambda j, i, k: (i, j))
    dims = _DIMS[mode]
    has_add = addend is not None

    def body(*refs):
        if has_add:
            a_ref, b_ref, r_ref, o_ref, acc_ref = refs
        else:
            a_ref, b_ref, o_ref, acc_ref = refs
        k = pl.program_id(2)

        @pl.when(k == 0)
        def _():
            acc_ref[...] = jnp.zeros_like(acc_ref)

        acc_ref[...] += lax.dot_general(a_ref[...].astype(BF16), b_ref[...].astype(BF16), dims, preferred_element_type=F32)

        @pl.when(k == nk - 1)
        def _():
            r = acc_ref[...] * scale
            if has_add:
                r = r + r_ref[...].astype(F32)
            o_ref[...] = r.astype(o_ref.dtype)

    in_specs = [a_spec, b_spec] + ([o_spec] if has_add else [])
    args = (a, b) + ((addend,) if has_add else ())
    return pl.pallas_call(
        body, name=name, grid=(nj, ni, nk), in_specs=in_specs, out_specs=o_spec,
        out_shape=jax.ShapeDtypeStruct((m_dim, n_dim), out_dtype),
        scratch_shapes=[pltpu.VMEM((tm, tn), F32)],
        compiler_params=_cparams("parallel", "parallel", "arbitrary"),
    )(*args)


ROW_STRIP = 16


def _strips(tm, fn, init=None, rb=ROW_STRIP):
    def step(i, carry):
        return fn(pl.ds(pl.multiple_of(i * rb, rb), rb), carry)
    return lax.fori_loop(0, tm // rb, step, init, unroll=2)


def _rms_fwd(x, gain, name):
    t, d = x.shape
    tm = _row_tile(t)

    def body(x_ref, g_ref, o_ref):
        xv = x_ref[...]
        r = lax.rsqrt(jnp.mean(xv * xv, axis=-1, keepdims=True) + EPS)
        o_ref[...] = (xv * r * g_ref[...]).astype(o_ref.dtype)

    return pl.pallas_call(
        body, name=name, grid=(t // tm,),
        in_specs=[pl.BlockSpec((tm, d), lambda i: (i, 0)), pl.BlockSpec((1, d), lambda i: (0, 0))],
        out_specs=pl.BlockSpec((tm, d), lambda i: (i, 0)),
        out_shape=jax.ShapeDtypeStruct((t, d), BF16), compiler_params=_cparams("parallel"),
    )(x, gain)


def _rms_bwd(x, gain, dn, dres, name):
    t, d = x.shape
    tm = _row_tile(t)
    has_res = dres is not None

    def body(*refs):
        if has_res:
            x_ref, g_ref, dn_ref, r_ref, dx_ref, dg_ref = refs
        else:
            x_ref, g_ref, dn_ref, dx_ref, dg_ref = refs

        @pl.when(pl.program_id(0) == 0)
        def _():
            dg_ref[...] = jnp.zeros_like(dg_ref)

        xv = x_ref[...]
        r = lax.rsqrt(jnp.mean(xv * xv, axis=-1, keepdims=True) + EPS)
        xh = xv * r
        dnv = dn_ref[...].astype(F32)
        dg_ref[...] += jnp.sum(dnv * xh, axis=0, keepdims=True)
        dxh = dnv * g_ref[...]
        dx = r * (dxh - xh * jnp.mean(dxh * xh, axis=-1, keepdims=True))
        if has_res:
            dx = dx + r_ref[...]
        dx_ref[...] = dx

    row = pl.BlockSpec((tm, d), lambda i: (i, 0))
    vec = pl.BlockSpec((1, d), lambda i: (0, 0))
    in_specs = [row, vec, row] + ([row] if has_res else [])
    args = (x, gain, dn) + ((dres,) if has_res else ())
    return pl.pallas_call(
        body, name=name, grid=(t // tm,), in_specs=in_specs, out_specs=[row, vec],
        out_shape=[jax.ShapeDtypeStruct((t, d), F32), jax.ShapeDtypeStruct((1, d), F32)],
        compiler_params=_cparams("arbitrary"),
    )(*args)


def _loss_head(h, gain, target, name):
    t, d = h.shape
    tm = _row_tile(t)

    def body(h_ref, g_ref, t_ref, dh_ref, dg_ref, l_ref):
        @pl.when(pl.program_id(0) == 0)
        def _():
            dg_ref[...] = jnp.zeros_like(dg_ref)
            l_ref[...] = jnp.zeros_like(l_ref)

        xv = h_ref[...]
        g = g_ref[...]
        r = lax.rsqrt(jnp.mean(xv * xv, axis=-1, keepdims=True) + EPS)
        xh = xv * r
        err = xh * g - t_ref[...]
        l_ref[...] += 0.5 * jnp.sum(jnp.mean(err * err, axis=-1, keepdims=True), axis=0, keepdims=True)
        dy = err * (1.0 / d)
        dg_ref[...] += jnp.sum(dy * xh, axis=0, keepdims=True)
        dxh = dy * g
        dh_ref[...] = r * (dxh - xh * jnp.mean(dxh * xh, axis=-1, keepdims=True))

    row = pl.BlockSpec((tm, d), lambda i: (i, 0))
    vec = pl.BlockSpec((1, d), lambda i: (0, 0))
    return pl.pallas_call(
        body, name=name, grid=(t // tm,), in_specs=[row, vec, row],
        out_specs=[row, vec, pl.BlockSpec((1, 128), lambda i: (0, 0))],
        out_shape=[jax.ShapeDtypeStruct((t, d), F32), jax.ShapeDtypeStruct((1, d), F32), jax.ShapeDtypeStruct((1, 128), F32)],
        compiler_params=_cparams("arbitrary"),
    )(h, gain, target)


FF_TILE = 1408


def _ffn_fwd(n, x, wg, wu, wd, name):
    t, d = x.shape
    tm, tn = _row_tile(t), FF_TILE
    nj = D_FF // tn

    def body(n_ref, x_ref, wg_ref, wu_ref, wd_ref, h_ref, g_ref, u_ref, acc_ref):
        j = pl.program_id(1)

        @pl.when(j == 0)
        def _():
            acc_ref[...] = jnp.zeros_like(acc_ref)

        nb = n_ref[...]
        g = jnp.dot(nb, wg_ref[...], preferred_element_type=F32)
        u = jnp.dot(nb, wu_ref[...], preferred_element_type=F32)
        g_ref[...] = g.astype(BF16)
        u_ref[...] = u.astype(BF16)
        a = g * _sigmoid(g) * u
        acc_ref[...] += jnp.dot(a.astype(BF16), wd_ref[...], preferred_element_type=F32)

        @pl.when(j == nj - 1)
        def _():
            h_ref[...] = x_ref[...] + 0.5 * acc_ref[...]

    row = pl.BlockSpec((tm, d), lambda i, j: (i, 0))
    act = pl.BlockSpec((tm, tn), lambda i, j: (i, j))
    return pl.pallas_call(
        body, name=name, grid=(t // tm, nj),
        in_specs=[row, row, pl.BlockSpec((d, tn), lambda i, j: (0, j)), pl.BlockSpec((d, tn), lambda i, j: (0, j)),
                  pl.BlockSpec((tn, d), lambda i, j: (j, 0))],
        out_specs=[row, act, act],
        out_shape=[jax.ShapeDtypeStruct((t, d), F32), jax.ShapeDtypeStruct((t, D_FF), BF16), jax.ShapeDtypeStruct((t, D_FF), BF16)],
        scratch_shapes=[pltpu.VMEM((tm, d), F32)], compiler_params=_cparams("parallel", "arbitrary"),
    )(n, x, wg, wu, wd)


def _ffn_bwd_act(dh, g, u, wg, wu, wd, name):
    t, d = dh.shape
    tm, tn = _row_tile(t), FF_TILE
    nj = D_FF // tn

    def body(dh_ref, g_ref, u_ref, wg_ref, wu_ref, wd_ref, dn_ref, dg_ref, du_ref, a_ref, acc_ref):
        j = pl.program_id(1)

        @pl.when(j == 0)
        def _():
            acc_ref[...] = jnp.zeros_like(acc_ref)

        dhb = (0.5 * dh_ref[...]).astype(BF16)
        da = lax.dot_general(dhb, wd_ref[...], _DIMS["nt"], preferred_element_type=F32)
        gv = g_ref[...].astype(F32)
        uv = u_ref[...].astype(F32)
        sg = _sigmoid(gv)
        s = gv * sg
        dg = (da * uv * (sg * (1.0 + gv * (1.0 - sg)))).astype(BF16)
        du = (da * s).astype(BF16)
        dg_ref[...] = dg
        du_ref[...] = du
        a_ref[...] = (s * uv).astype(BF16)
        acc_ref[...] += (lax.dot_general(dg, wg_ref[...], _DIMS["nt"], preferred_element_type=F32)
                         + lax.dot_general(du, wu_ref[...], _DIMS["nt"], preferred_element_type=F32))

        @pl.when(j == nj - 1)
        def _():
            dn_ref[...] = acc_ref[...]

    row = pl.BlockSpec((tm, d), lambda i, j: (i, 0))
    act = pl.BlockSpec((tm, tn), lambda i, j: (i, j))
    return pl.pallas_call(
        body, name=name, grid=(t // tm, nj),
        in_specs=[row, act, act, pl.BlockSpec((d, tn), lambda i, j: (0, j)), pl.BlockSpec((d, tn), lambda i, j: (0, j)),
                  pl.BlockSpec((tn, d), lambda i, j: (j, 0))],
        out_specs=[row, act, act, act],
        out_shape=[jax.ShapeDtypeStruct((t, d), F32)] + [jax.ShapeDtypeStruct((t, D_FF), BF16)] * 3,
        scratch_shapes=[pltpu.VMEM((tm, d), F32)], compiler_params=_cparams("parallel", "arbitrary"),
    )(dh, g, u, wg, wu, wd)


def _ffn_forward(x, gain, wg, wu, wd, tag):
    n = _rms_fwd(x, gain, f"{tag}_norm")
    h, g, u = _ffn_fwd(n, x, wg, wu, wd, f"{tag}_fwd")
    return h, (n, g, u)


def _ffn_backward(dh, x, gain, wg, wu, wd, saved, tag, push):
    n, g, u = saved
    dn, dg, du, a = _ffn_bwd_act(dh, g, u, wg, wu, wd, f"{tag}_bwd_act")
    kw = dict(mode="tn", out_dtype=BF16, tm=1024, tn=FF_TILE, tk=1024)
    d_wg = _matmul(n, dg, name=f"{tag}_dwg", **kw)
    d_wu = _matmul(n, du, name=f"{tag}_dwu", **kw)
    d_wd = _matmul(a, dh, mode="tn", out_dtype=BF16, tm=FF_TILE, tn=1024, tk=1024, scale=0.5, name=f"{tag}_dwd")
    token = push({f"{tag}_w_gate": d_wg, f"{tag}_w_up": d_wu, f"{tag}_w_down": d_wd})
    return _rms_bwd(x, gain + token, dn, dh, f"{tag}_norm_bwd")


CONV_COLS = 512
HALO = 8
CONV_STRIP = 32


def _conv_fwd(p, w, b, name):
    t = p.shape[0]
    tm = _row_tile(t)
    c0 = P_XBC // CONV_COLS

    def body(x_ref, halo_ref, w_ref, b_ref, o_ref, s_ref):
        i = pl.program_id(1)
        s_ref[0:HALO, :] = jnp.where(i > 0, halo_ref[...].astype(F32), 0.0)
        s_ref[HALO:HALO + tm, :] = x_ref[...].astype(F32)
        wv = w_ref[...]
        bv = b_ref[...]
        for r0 in range(0, tm, CONV_STRIP):
            acc = bv + wv[0:1, :] * s_ref[HALO - 3 + r0:HALO - 3 + r0 + CONV_STRIP, :]
            for k in range(1, SSD_CONV):
                acc = acc + wv[k:k + 1, :] * s_ref[HALO - 3 + k + r0:HALO - 3 + k + r0 + CONV_STRIP, :]
            o_ref[r0:r0 + CONV_STRIP, :] = (acc * _sigmoid(acc)).astype(o_ref.dtype)

    return pl.pallas_call(
        body, name=name, grid=(XBC // CONV_COLS, t // tm),
        in_specs=[pl.BlockSpec((tm, CONV_COLS), lambda j, i: (i, c0 + j)),
                  pl.BlockSpec((HALO, CONV_COLS), lambda j, i: (jnp.maximum(i * (tm // HALO) - 1, 0), c0 + j)),
                  pl.BlockSpec((SSD_CONV, CONV_COLS), lambda j, i: (0, j)), pl.BlockSpec((1, CONV_COLS), lambda j, i: (0, j))],
        out_specs=pl.BlockSpec((tm, CONV_COLS), lambda j, i: (i, j)),
        out_shape=jax.ShapeDtypeStruct((t, XBC), BF16),
        scratch_shapes=[pltpu.VMEM((tm + HALO, CONV_COLS), F32)], compiler_params=_cparams("parallel", "parallel"),
    )(p, p, w, b)


def _conv_bwd_act(p, dy, w, b, col0, name):
    t, cols = dy.shape
    tm = _row_tile(t)
    c0 = (P_XBC + col0) // CONV_COLS
    w0 = col0 // CONV_COLS

    def body(x_ref, halo_ref, dy_ref, w_ref, b_ref, da_ref, dw_ref, db_ref, s_ref):
        i = pl.program_id(1)

        @pl.when(i == 0)
        def _():
            dw_ref[...] = jnp.zeros_like(dw_ref)
            db_ref[...] = jnp.zeros_like(db_ref)

        s_ref[0:HALO, :] = jnp.where(i > 0, halo_ref[...].astype(F32), 0.0)
        s_ref[HALO:HALO + tm, :] = x_ref[...].astype(F32)
        wv = w_ref[...]
        bv = b_ref[...]
        fold = lambda v: jnp.sum(v.reshape(CONV_STRIP // 8, 8, CONV_COLS), axis=0)
        sums = [jnp.zeros((8, CONV_COLS), F32) for _ in range(SSD_CONV + 1)]
        for r0 in range(0, tm, CONV_STRIP):
            taps = [s_ref[HALO - 3 + k + r0:HALO - 3 + k + r0 + CONV_STRIP, :] for k in range(SSD_CONV)]
            acc = bv + wv[0:1, :] * taps[0]
            for k in range(1, SSD_CONV):
                acc = acc + wv[k:k + 1, :] * taps[k]
            sg = _sigmoid(acc)
            dacc = dy_ref[r0:r0 + CONV_STRIP, :].astype(F32) * (sg * (1.0 + acc * (1.0 - sg)))
            da_ref[r0:r0 + CONV_STRIP, :] = dacc.astype(BF16)
            for k in range(SSD_CONV):
                sums[k] = sums[k] + fold(dacc * taps[k])
            sums[SSD_CONV] = sums[SSD_CONV] + fold(dacc)
        for k in range(SSD_CONV):
            dw_ref[k:k + 1, :] += jnp.sum(sums[k], axis=0, keepdims=True)
        db_ref[...] += jnp.sum(sums[SSD_CONV], axis=0, keepdims=True)

    return pl.pallas_call(
        body, name=name, grid=(cols // CONV_COLS, t // tm),
        in_specs=[pl.BlockSpec((tm, CONV_COLS), lambda j, i: (i, c0 + j)),
                  pl.BlockSpec((HALO, CONV_COLS), lambda j, i: (jnp.maximum(i * (tm // HALO) - 1, 0), c0 + j)),
                  pl.BlockSpec((tm, CONV_COLS), lambda j, i: (i, j)),
                  pl.BlockSpec((SSD_CONV, CONV_COLS), lambda j, i: (0, w0 + j)), pl.BlockSpec((1, CONV_COLS), lambda j, i: (0, w0 + j))],
        out_specs=[pl.BlockSpec((tm, CONV_COLS), lambda j, i: (i, j)), pl.BlockSpec((SSD_CONV, CONV_COLS), lambda j, i: (0, j)),
                   pl.BlockSpec((1, CONV_COLS), lambda j, i: (0, j))],
        out_shape=[jax.ShapeDtypeStruct((t, cols), BF16), jax.ShapeDtypeStruct((SSD_CONV, cols), F32), jax.ShapeDtypeStruct((1, cols), F32)],
        scratch_shapes=[pltpu.VMEM((tm + HALO, CONV_COLS), F32)], compiler_params=_cparams("parallel", "arbitrary"),
    )(p, p, dy, w, b)


def _conv_bwd_dx(dacc, w, col0, name):
    t, cols = dacc.shape
    tm = _row_tile(t)
    nt = t // tm
    w0 = col0 // CONV_COLS

    def body(d_ref, halo_ref, w_ref, o_ref, s_ref):
        i = pl.program_id(1)
        s_ref[0:tm, :] = d_ref[...].astype(F32)
        s_ref[tm:tm + HALO, :] = jnp.where(i < nt - 1, halo_ref[...].astype(F32), 0.0)
        wv = w_ref[...]
        for r0 in range(0, tm, CONV_STRIP):
            acc = wv[3:4, :] * s_ref[r0:r0 + CONV_STRIP, :]
            for k in range(SSD_CONV - 1):
                acc = acc + wv[k:k + 1, :] * s_ref[3 - k + r0:3 - k + r0 + CONV_STRIP, :]
            o_ref[r0:r0 + CONV_STRIP, :] = acc.astype(o_ref.dtype)

    return pl.pallas_call(
        body, name=name, grid=(cols // CONV_COLS, nt),
        in_specs=[pl.BlockSpec((tm, CONV_COLS), lambda j, i: (i, j)),
                  pl.BlockSpec((HALO, CONV_COLS), lambda j, i: (jnp.minimum((i + 1) * (tm // HALO), t // HALO - 1), j)),
                  pl.BlockSpec((SSD_CONV, CONV_COLS), lambda j, i: (0, w0 + j))],
        out_specs=pl.BlockSpec((tm, CONV_COLS), lambda j, i: (i, j)),
        out_shape=jax.ShapeDtypeStruct((t, cols), BF16),
        scratch_shapes=[pltpu.VMEM((tm + HALO, CONV_COLS), F32)], compiler_params=_cparams("parallel", "parallel"),
    )(dacc, dacc, w)


GROUP_COLS = SSD_INNER // SSD_GROUPS
PAIRS = GROUP_COLS // 128
HEADS_PER_GROUP = SSD_HEADS // SSD_GROUPS


def _dt_fwd(dt_raw, bias, name):
    t, n = dt_raw.shape
    tm = _row_tile(t)

    def body(x_ref, b_ref, o_ref):
        v = x_ref[...] + b_ref[...]
        o_ref[...] = jnp.maximum(v, 0.0) + jnp.log1p(jnp.exp(-jnp.abs(v)))

    row = pl.BlockSpec((tm, n), lambda i: (i, 0))
    vec = pl.BlockSpec((1, n), lambda i: (0, 0))
    return pl.pallas_call(body, name=name, grid=(t // tm,), in_specs=[row, vec], out_specs=row,
                          out_shape=jax.ShapeDtypeStruct((t, n), F32), compiler_params=_cparams("parallel"))(dt_raw, bias)


def _dt_bwd(ddt, dt_raw, bias, name):
    t, n = dt_raw.shape
    tm = _row_tile(t)

    def body(d_ref, x_ref, b_ref, o_ref, db_ref):
        @pl.when(pl.program_id(0) == 0)
        def _():
            db_ref[...] = jnp.zeros_like(db_ref)

        dr = d_ref[...] * _sigmoid(x_ref[...] + b_ref[...])
        o_ref[...] = dr.astype(o_ref.dtype)
        db_ref[...] += jnp.sum(dr, axis=0, keepdims=True)

    row = pl.BlockSpec((tm, n), lambda i: (i, 0))
    vec = pl.BlockSpec((1, n), lambda i: (0, 0))
    return pl.pallas_call(body, name=name, grid=(t // tm,), in_specs=[row, row, vec], out_specs=[row, vec],
                          out_shape=[jax.ShapeDtypeStruct((t, n), BF16), jax.ShapeDtypeStruct((1, n), F32)],
                          compiler_params=_cparams("arbitrary"))(ddt, dt_raw, bias)


def _ssd_common(dtw_ref, dtt_ref, aw_ref, at_ref):
    l = CHUNK
    dt = dtw_ref[0]
    dtt = dtt_ref[0]
    a = -jnp.exp(aw_ref[0])
    at = -jnp.exp(at_ref[0])
    rowi = lax.broadcasted_iota(jnp.int32, (l, l), 0)
    coli = lax.broadcasted_iota(jnp.int32, (l, l), 1)
    tri = rowi >= coli
    lower = tri.astype(F32)
    upper = (rowi <= coli).astype(F32)
    acs = jnp.dot(lower, dt * a, precision=HI, preferred_element_type=F32)
    acst = jnp.dot(dtt * at, upper, precision=HI, preferred_element_type=F32)
    return dt, a, acs, acst, tri, upper


def _pair_bc(w, lo, p):
    return jnp.where(lo, w[:, 2 * p:2 * p + 1], w[:, 2 * p + 1:2 * p + 2])


def _ssd_specs(t):
    nc = t // CHUNK
    return nc, dict(
        xs=lambda cm: pl.BlockSpec((CHUNK, GROUP_COLS), lambda g, c: (cm(c), g)),
        bm=lambda cm: pl.BlockSpec((CHUNK, SSD_STATE), lambda g, c: (cm(c), SSD_INNER // SSD_STATE + g)),
        cmat=lambda cm: pl.BlockSpec((CHUNK, SSD_STATE), lambda g, c: (cm(c), SSD_INNER // SSD_STATE + SSD_GROUPS + g)),
        dtw=lambda cm: pl.BlockSpec((1, CHUNK, 128), lambda g, c: (g, cm(c), 0)),
        dtt=lambda cm: pl.BlockSpec((1, HEADS_PER_GROUP, CHUNK), lambda g, c: (g, 0, cm(c))),
        wide=lambda cm: pl.BlockSpec((1, 1, 128), lambda g, c: (g, 0, 0)),
        tall=lambda cm: pl.BlockSpec((1, HEADS_PER_GROUP, 1), lambda g, c: (g, 0, 0)),
        grp=lambda cm: pl.BlockSpec((CHUNK, GROUP_COLS), lambda g, c: (cm(c), g)),
        vec=lambda cm: pl.BlockSpec((1, GROUP_COLS), lambda g, c: (0, g)),
        state=lambda cm: pl.BlockSpec((1, 1, PAIRS, SSD_STATE, 128), lambda g, c: (g, cm(c), 0, 0, 0)),
    )


def _ssd_fwd(xc, p, hv, norm_g, name):
    t = xc.shape[0]
    nc, sp = _ssd_specs(t)
    ident = lambda c: c

    def body(xs_ref, b_ref, c_ref, dtw_ref, dtt_ref, aw_ref, at_ref, dk_ref, z_ref, ng_ref,
             y_ref, ys_ref, h_ref, st_ref):
        @pl.when(pl.program_id(1) == 0)
        def _():
            st_ref[...] = jnp.zeros_like(st_ref)

        dt, a, acs, acst, tri, _ = _ssd_common(dtw_ref, dtt_ref, aw_ref, at_ref)
        ecs = jnp.exp(acs)
        alast = acs[CHUNK - 1:CHUNK, :]
        bmat, cmat = b_ref[...], c_ref[...]
        cb = lax.dot_general(cmat, bmat, _DIMS["nt"], preferred_element_type=F32)
        lo = lax.broadcasted_iota(jnp.int32, (1, 128), 1) < 64
        dskip = dk_ref[0]
        for pi in range(PAIRS):
            x = xs_ref[:, pi * 128:(pi + 1) * 128].astype(F32)
            xdt = x * _pair_bc(dt, lo, pi)
            ydiag = jnp.zeros((CHUNK, 128), F32)
            for r, mask in ((2 * pi, lo), (2 * pi + 1, jnp.logical_not(lo))):
                lam = jnp.exp(jnp.where(tri, acs[:, r:r + 1] - acst[r:r + 1, :], -1e30))
                m = (cb * lam).astype(BF16)
                ydiag = ydiag + jnp.dot(m, jnp.where(mask, xdt, 0.0).astype(BF16), preferred_element_type=F32)
            ht = st_ref[pi]
            h_ref[0, 0, pi] = ht
            yoff = jnp.dot(cmat, ht.astype(BF16), preferred_element_type=F32) * _pair_bc(ecs, lo, pi)
            y_ref[:, pi * 128:(pi + 1) * 128] = (ydiag + yoff + _pair_bc(dskip, lo, pi) * x).astype(y_ref.dtype)
            alp = _pair_bc(alast, lo, pi)
            e = jnp.exp(alp - _pair_bc(acs, lo, pi))
            st = lax.dot_general(bmat, (xdt * e).astype(BF16), _DIMS["tn"], preferred_element_type=F32)
            st_ref[pi] = ht * jnp.exp(alp) + st
        zf = z_ref[...].astype(F32)
        yg = y_ref[...].astype(F32) * (zf * _sigmoid(zf))
        rstd = lax.rsqrt(jnp.mean(yg * yg, axis=-1, keepdims=True) + EPS)
        ys_ref[...] = (yg * rstd * ng_ref[...]).astype(ys_ref.dtype)

    ins = ["xs", "bm", "cmat", "dtw", "dtt", "wide", "tall", "wide", "grp", "vec"]
    return pl.pallas_call(
        body, name=name, grid=(SSD_GROUPS, nc),
        in_specs=[sp[k](ident) for k in ins],
        out_specs=[sp["grp"](ident), sp["grp"](ident), sp["state"](ident)],
        out_shape=[jax.ShapeDtypeStruct((t, SSD_INNER), BF16), jax.ShapeDtypeStruct((t, SSD_INNER), BF16),
                   jax.ShapeDtypeStruct((SSD_GROUPS, nc, PAIRS, SSD_STATE, 128), F32)],
        scratch_shapes=[pltpu.VMEM((PAIRS, SSD_STATE, 128), F32)], compiler_params=_cparams("parallel", "arbitrary"),
    )(xc, xc, xc, hv["dtw"], hv["dtt"], hv["alog_w"], hv["alog_t"], hv["dskip_w"], p, norm_g)


def _ssd_bwd(xc, p, hv, norm_g, y, dys, states, name):
    t = xc.shape[0]
    nc, sp = _ssd_specs(t)
    rev = lambda c: nc - 1 - c

    def body(xs_ref, b_ref, c_ref, dtw_ref, dtt_ref, aw_ref, at_ref, dk_ref, z_ref, ng_ref,
             y_ref, dys_ref, h_ref,
             dxs_ref, db_ref, dc_ref, dz_ref, ddt_ref, hsum_ref, dng_ref, dst_ref):
        @pl.when(pl.program_id(1) == 0)
        def _():
            dst_ref[...] = jnp.zeros_like(dst_ref)
            hsum_ref[...] = jnp.zeros_like(hsum_ref)
            dng_ref[...] = jnp.zeros_like(dng_ref)

        dt, a, acs, acst, tri, upper = _ssd_common(dtw_ref, dtt_ref, aw_ref, at_ref)
        ecs = jnp.exp(acs)
        alast = acs[CHUNK - 1:CHUNK, :]
        bmat, cmat = b_ref[...], c_ref[...]
        cb = lax.dot_general(cmat, bmat, _DIMS["nt"], preferred_element_type=F32)
        lane = lax.broadcasted_iota(jnp.int32, (1, 128), 1)
        lo = lane < 64
        dskip = dk_ref[0]

        zf = z_ref[...].astype(F32)
        sg = _sigmoid(zf)
        sz = zf * sg
        yv = y_ref[...].astype(F32)
        yg = yv * sz
        rstd = lax.rsqrt(jnp.mean(yg * yg, axis=-1, keepdims=True) + EPS)
        yhat = yg * rstd
        dysv = dys_ref[...].astype(F32)
        dng_ref[...] += jnp.sum(dysv * yhat, axis=0, keepdims=True)
        dyh = dysv * ng_ref[...]
        dyg = rstd * (dyh - yhat * jnp.mean(dyh * yhat, axis=-1, keepdims=True))
        dz_ref[...] = (dyg * yv * (sg * (1.0 + zf * (1.0 - sg)))).astype(dz_ref.dtype)
        dy_all = dyg * sz

        sel_r = lax.broadcasted_iota(jnp.int32, (128, 128), 0)
        sel_c = lax.broadcasted_iota(jnp.int32, (128, 128), 1)
        dal = jnp.zeros((CHUNK, 128), F32)
        ddtm = jnp.zeros((CHUNK, 128), F32)
        dalast = jnp.zeros((8, 128), F32)
        ddsk = jnp.zeros((8, 128), F32)
        dcb = jnp.zeros((CHUNK, CHUNK), F32)
        qcol = jnp.zeros((8, CHUNK), F32)
        sub8 = lax.broadcasted_iota(jnp.int32, (8, CHUNK), 0)
        dc_acc = jnp.zeros((CHUNK, SSD_STATE), F32)
        db_acc = jnp.zeros((CHUNK, SSD_STATE), F32)
        for pi in range(PAIRS):
            sel = (sel_c == 2 * pi + (sel_r >= 64).astype(jnp.int32)).astype(BF16)

            def hsum(v, sel=sel):
                return jnp.dot(v.astype(BF16), sel, preferred_element_type=F32)

            dyp = dy_all[:, pi * 128:(pi + 1) * 128]
            x = xs_ref[:, pi * 128:(pi + 1) * 128].astype(F32)
            dtp = _pair_bc(dt, lo, pi)
            xdt = x * dtp
            dxdt = jnp.zeros((CHUNK, 128), F32)
            for r, mask in ((2 * pi, lo), (2 * pi + 1, jnp.logical_not(lo))):
                lam = jnp.exp(jnp.where(tri, acs[:, r:r + 1] - acst[r:r + 1, :], -1e30))
                m32 = cb * lam
                m = m32.astype(BF16)
                dyr = jnp.where(mask, dyp, 0.0).astype(BF16)
                xr = jnp.where(mask, xdt, 0.0).astype(BF16)
                dm = lax.dot_general(dyr, xr, _DIMS["nt"], preferred_element_type=F32)
                dcb = dcb + dm * lam
                q = dm * m32
                dal = dal + jnp.sum(q, axis=1, keepdims=True) * (lane == r).astype(F32)
                qcol = qcol + jnp.where(sub8 == r, jnp.sum(q, axis=0, keepdims=True), 0.0)
                dxdt = dxdt + lax.dot_general(m, dyr, _DIMS["tn"], preferred_element_type=F32)
            ht = h_ref[0, 0, pi]
            htb = ht.astype(BF16)
            ecp = _pair_bc(ecs, lo, pi)
            yoff = jnp.dot(cmat, htb, preferred_element_type=F32) * ecp
            dg = (dyp * ecp).astype(BF16)
            dc_acc = dc_acc + lax.dot_general(dg, htb, _DIMS["nt"], preferred_element_type=F32)
            dht = lax.dot_general(cmat, dg, _DIMS["tn"], preferred_element_type=F32)
            dal = dal + hsum(dyp * yoff)
            dhn = dst_ref[pi]
            dhnb = dhn.astype(BF16)
            alp = _pair_bc(alast, lo, pi)
            e = jnp.exp(alp - _pair_bc(acs, lo, pi))
            xe = xdt * e
            db_acc = db_acc + lax.dot_general(xe.astype(BF16), dhnb, _DIMS["nt"], preferred_element_type=F32)
            dxe = jnp.dot(bmat, dhnb, preferred_element_type=F32)
            dxdt = dxdt + dxe * e
            tt = hsum(dxe * xe)
            dal = dal - tt
            dec = jnp.exp(alp)
            dalast = dalast + jnp.sum(tt, axis=0, keepdims=True) + hsum(
                jnp.broadcast_to(jnp.sum(dhn * ht, axis=0, keepdims=True) * dec, (8, 128)))
            dst_ref[pi] = dht + dhn * dec
            dxs_ref[:, pi * 128:(pi + 1) * 128] = (_pair_bc(dskip, lo, pi) * dyp + dxdt * dtp).astype(dxs_ref.dtype)
            ddtm = ddtm + hsum(dxdt * x)
            ddsk = ddsk + hsum(jnp.broadcast_to(jnp.sum(dyp * x, axis=0, keepdims=True), (8, 128)))
        rowi = lax.broadcasted_iota(jnp.int32, (CHUNK, 128), 0)
        qcol_w = lax.dot_general(jnp.concatenate([qcol, jnp.zeros((CHUNK - 8, CHUNK), F32)], axis=0), (sel_r == sel_c).astype(F32),
                                 _DIMS["tn"], precision=HI, preferred_element_type=F32)
        dal = dal - qcol_w + jnp.where(rowi == CHUNK - 1, dalast[0:1, :], 0.0)
        dda = jnp.dot(upper, dal, precision=HI, preferred_element_type=F32)
        ddt_ref[0] = ddtm + dda * a
        hsum_ref[0, 1:2, :] += jnp.sum(dda * dt, axis=0, keepdims=True) * a
        hsum_ref[0, 2:3, :] += ddsk[0:1, :]
        dcbb = dcb.astype(BF16)
        dc_ref[...] = (jnp.dot(dcbb, bmat, preferred_element_type=F32) + dc_acc).astype(dc_ref.dtype)
        db_ref[...] = (lax.dot_general(dcbb, cmat, _DIMS["tn"], preferred_element_type=F32) + db_acc).astype(db_ref.dtype)

    ins = ["xs", "bm", "cmat", "dtw", "dtt", "wide", "tall", "wide", "grp", "vec", "grp", "grp", "state"]
    col = lambda: pl.BlockSpec((CHUNK, SSD_STATE), lambda g, c: (rev(c), g))
    return pl.pallas_call(
        body, name=name, grid=(SSD_GROUPS, nc),
        in_specs=[sp[k](rev) for k in ins],
        out_specs=[sp["grp"](rev), col(), col(), sp["grp"](rev), sp["dtw"](rev),
                   pl.BlockSpec((1, 8, 128), lambda g, c: (g, 0, 0)), sp["vec"](rev)],
        out_shape=[jax.ShapeDtypeStruct((t, SSD_INNER), BF16), jax.ShapeDtypeStruct((t, SSD_GROUPS * SSD_STATE), BF16),
                   jax.ShapeDtypeStruct((t, SSD_GROUPS * SSD_STATE), BF16), jax.ShapeDtypeStruct((t, SSD_INNER), BF16),
                   jax.ShapeDtypeStruct((SSD_GROUPS, t, 128), F32), jax.ShapeDtypeStruct((SSD_GROUPS, 8, 128), F32),
                   jax.ShapeDtypeStruct((1, SSD_INNER), F32)],
        scratch_shapes=[pltpu.VMEM((PAIRS, SSD_STATE, 128), F32)], compiler_params=_cparams("parallel", "arbitrary"),
    )(xc, xc, xc, hv["dtw"], hv["dtt"], hv["alog_w"], hv["alog_t"], hv["dskip_w"], p, norm_g, y, dys, states)


def _wide(v):
    return jnp.pad(v.reshape(SSD_GROUPS, 1, HEADS_PER_GROUP), ((0, 0), (0, 0), (0, 128 - HEADS_PER_GROUP)))


def _head_inputs(dt, a_log, d_skip):
    t = dt.shape[0]
    g = dt[:, :SSD_HEADS].reshape(t, SSD_GROUPS, HEADS_PER_GROUP)
    return dict(
        dtw=jnp.pad(jnp.transpose(g, (1, 0, 2)), ((0, 0), (0, 0), (0, 128 - HEADS_PER_GROUP))),
        dtt=jnp.transpose(g, (1, 2, 0)),
        alog_w=_wide(a_log), alog_t=a_log.reshape(SSD_GROUPS, HEADS_PER_GROUP, 1),
        dskip_w=_wide(d_skip),
    )


def _gelu(x):
    return 0.5 * x * (1.0 + lax.erf(x * (1.0 / math.sqrt(2.0))))


def _gelu_grad(x):
    return 0.5 * (1.0 + lax.erf(x * (1.0 / math.sqrt(2.0)))) + x * jnp.exp(-0.5 * x * x) * (1.0 / math.sqrt(2.0 * math.pi))


def _tril_mask():
    r = lax.broadcasted_iota(jnp.int32, (CHUNK, CHUNK), 0)
    c = lax.broadcasted_iota(jnp.int32, (CHUNK, CHUNK), 1)
    return r >= c


def _gmlp_fwd(p, v_gain, w_s, b_col, name):
    t = p.shape[0]
    tm = _row_tile(t)
    u0 = P_UV // GMLP_W

    def body(u_ref, v_ref, gn_ref, ws_ref, bs_ref, o_ref):
        v = _gelu(v_ref[...].astype(F32))
        v = (v * lax.rsqrt(jnp.mean(v * v, axis=-1, keepdims=True) + EPS) * gn_ref[...]).astype(BF16)
        tril = _tril_mask()
        wm = [jnp.where(tril, ws_ref[g], 0.0).astype(BF16) for g in range(GMLP_GROUPS)]
        for k in range(tm // CHUNK):
            rows = slice(k * CHUNK, (k + 1) * CHUNK)
            for g in range(GMLP_GROUPS):
                cols = slice(g * 128, (g + 1) * 128)
                mixed = jnp.dot(wm[g], v[rows, cols], preferred_element_type=F32) + bs_ref[g]
                o_ref[rows, cols] = (_gelu(u_ref[rows, cols].astype(F32)) * mixed).astype(o_ref.dtype)

    return pl.pallas_call(
        body, name=name, grid=(t // tm,),
        in_specs=[pl.BlockSpec((tm, GMLP_W), lambda i: (i, u0)), pl.BlockSpec((tm, GMLP_W), lambda i: (i, u0 + 1)),
                  pl.BlockSpec((1, GMLP_W), lambda i: (0, 0)), pl.BlockSpec((GMLP_GROUPS, CHUNK, CHUNK), lambda i: (0, 0, 0)),
                  pl.BlockSpec((GMLP_GROUPS, CHUNK, 1), lambda i: (0, 0, 0))],
        out_specs=pl.BlockSpec((tm, GMLP_W), lambda i: (i, 0)),
        out_shape=jax.ShapeDtypeStruct((t, GMLP_W), BF16), compiler_params=_cparams("parallel"),
    )(p, p, v_gain, w_s, b_col)


def _gmlp_bwd(p, dy, v_gain, w_s, b_col, name):
    t = p.shape[0]
    tm = _row_tile(t)
    u0 = P_UV // GMLP_W

    def body(u_ref, v_ref, dy_ref, gn_ref, ws_ref, bs_ref, duv_ref, dws_ref, dbs_ref, dgn_ref, dvn_ref):
        @pl.when(pl.program_id(0) == 0)
        def _():
            dws_ref[...] = jnp.zeros_like(dws_ref)
            dbs_ref[...] = jnp.zeros_like(dbs_ref)
            dgn_ref[...] = jnp.zeros_like(dgn_ref)

        vraw = v_ref[...].astype(F32)
        va = _gelu(vraw)
        rstd = lax.rsqrt(jnp.mean(va * va, axis=-1, keepdims=True) + EPS)
        vhat = va * rstd
        gain = gn_ref[...]
        vn = (vhat * gain).astype(BF16)
        tril = _tril_mask()
        wm = [jnp.where(tril, ws_ref[g], 0.0).astype(BF16) for g in range(GMLP_GROUPS)]
        for k in range(tm // CHUNK):
            rows = slice(k * CHUNK, (k + 1) * CHUNK)
            for g in range(GMLP_GROUPS):
                cols = slice(g * 128, (g + 1) * 128)
                uraw = u_ref[rows, cols].astype(F32)
                vb = vn[rows, cols]
                mixed = jnp.dot(wm[g], vb, preferred_element_type=F32) + bs_ref[g]
                dyb = dy_ref[rows, cols].astype(F32)
                duv_ref[rows, cols] = (dyb * mixed * _gelu_grad(uraw)).astype(duv_ref.dtype)
                dmix = dyb * _gelu(uraw)
                dmb = dmix.astype(BF16)
                dws_ref[g] += jnp.where(tril, lax.dot_general(dmb, vb, _DIMS["nt"], preferred_element_type=F32), 0.0)
                dbs_ref[g] += jnp.sum(dmix, axis=1, keepdims=True)
                dvn_ref[rows, cols] = lax.dot_general(wm[g], dmb, _DIMS["tn"], preferred_element_type=F32)
        dvn = dvn_ref[...]
        dgn_ref[...] += jnp.sum(dvn * vhat, axis=0, keepdims=True)
        dvh = dvn * gain
        dva = rstd * (dvh - vhat * jnp.mean(dvh * vhat, axis=-1, keepdims=True))
        duv_ref[:, GMLP_W:2 * GMLP_W] = (dva * _gelu_grad(vraw)).astype(duv_ref.dtype)

    return pl.pallas_call(
        body, name=name, grid=(t // tm,),
        in_specs=[pl.BlockSpec((tm, GMLP_W), lambda i: (i, u0)), pl.BlockSpec((tm, GMLP_W), lambda i: (i, u0 + 1)),
                  pl.BlockSpec((tm, GMLP_W), lambda i: (i, 0)),
                  pl.BlockSpec((1, GMLP_W), lambda i: (0, 0)), pl.BlockSpec((GMLP_GROUPS, CHUNK, CHUNK), lambda i: (0, 0, 0)),
                  pl.BlockSpec((GMLP_GROUPS, CHUNK, 1), lambda i: (0, 0, 0))],
        out_specs=[pl.BlockSpec((tm, 2 * GMLP_W), lambda i: (i, 0)), pl.BlockSpec((GMLP_GROUPS, CHUNK, CHUNK), lambda i: (0, 0, 0)),
                   pl.BlockSpec((GMLP_GROUPS, CHUNK, 1), lambda i: (0, 0, 0)), pl.BlockSpec((1, GMLP_W), lambda i: (0, 0))],
        out_shape=[jax.ShapeDtypeStruct((t, 2 * GMLP_W), BF16), jax.ShapeDtypeStruct((GMLP_GROUPS, CHUNK, CHUNK), F32),
                   jax.ShapeDtypeStruct((GMLP_GROUPS, CHUNK, 1), F32), jax.ShapeDtypeStruct((1, GMLP_W), F32)],
        scratch_shapes=[pltpu.VMEM((tm, GMLP_W), F32)], compiler_params=_cparams("arbitrary"),
    )(p, p, dy, v_gain, w_s, b_col)


def _head_masks():
    lane = lax.broadcasted_iota(jnp.int32, (1, MEM_W), 1)
    return [(lane >= h * 64) & (lane < (h + 1) * 64) for h in range(MEM_HEADS)]


def _mem_fwd(p, kv, name):
    t = p.shape[0]
    tm = _row_tile(t)
    q0 = P_Q // MEM_W

    def body(q_ref, kv_ref, o_ref):
        q = q_ref[...]
        k = kv_ref[:, 0:MEM_W].astype(BF16)
        v = kv_ref[:, MEM_W:2 * MEM_W].astype(BF16)
        out = jnp.zeros((tm, MEM_W), F32)
        for mask in _head_masks():
            s = lax.dot_general(jnp.where(mask, q, 0), k, _DIMS["nt"], preferred_element_type=F32) * 0.125
            e = jnp.exp(s - jnp.max(s, axis=-1, keepdims=True))
            pr = (e / jnp.sum(e, axis=-1, keepdims=True)).astype(BF16)
            out = out + jnp.where(mask, jnp.dot(pr, v, preferred_element_type=F32), 0.0)
        o_ref[...] = out.astype(o_ref.dtype)

    return pl.pallas_call(
        body, name=name, grid=(t // tm,),
        in_specs=[pl.BlockSpec((tm, MEM_W), lambda i: (i, q0)), pl.BlockSpec((MEM_LEN, 2 * MEM_W), lambda i: (0, 0))],
        out_specs=pl.BlockSpec((tm, MEM_W), lambda i: (i, 0)),
        out_shape=jax.ShapeDtypeStruct((t, MEM_W), BF16), compiler_params=_cparams("parallel"),
    )(p, kv)


def _mem_bwd(p, kv, dy, name):
    t = p.shape[0]
    tm = _row_tile(t)
    q0 = P_Q // MEM_W

    def body(q_ref, kv_ref, dy_ref, dq_ref, dkv_ref):
        @pl.when(pl.program_id(0) == 0)
        def _():
            dkv_ref[...] = jnp.zeros_like(dkv_ref)

        q = q_ref[...]
        dy = dy_ref[...]
        k = kv_ref[:, 0:MEM_W].astype(BF16)
        v = kv_ref[:, MEM_W:2 * MEM_W].astype(BF16)
        dq = jnp.zeros((tm, MEM_W), F32)
        dk = jnp.zeros((MEM_LEN, MEM_W), F32)
        dv = jnp.zeros((MEM_LEN, MEM_W), F32)
        for mask in _head_masks():
            qh = jnp.where(mask, q, 0)
            dyh = jnp.where(mask, dy, 0)
            s = lax.dot_general(qh, k, _DIMS["nt"], preferred_element_type=F32) * 0.125
            e = jnp.exp(s - jnp.max(s, axis=-1, keepdims=True))
            pr = e / jnp.sum(e, axis=-1, keepdims=True)
            prb = pr.astype(BF16)
            dp = lax.dot_general(dyh, v, _DIMS["nt"], preferred_element_type=F32)
            ds = (pr * (dp - jnp.sum(dp * pr, axis=-1, keepdims=True)) * 0.125).astype(BF16)
            dq = dq + jnp.where(mask, jnp.dot(ds, k, preferred_element_type=F32), 0.0)
            dk = dk + lax.dot_general(ds, qh, _DIMS["tn"], preferred_element_type=F32)
            dv = dv + lax.dot_general(prb, dyh, _DIMS["tn"], preferred_element_type=F32)
        dq_ref[...] = dq.astype(dq_ref.dtype)
        dkv_ref[:, 0:MEM_W] += dk
        dkv_ref[:, MEM_W:2 * MEM_W] += dv

    return pl.pallas_call(
        body, name=name, grid=(t // tm,),
        in_specs=[pl.BlockSpec((tm, MEM_W), lambda i: (i, q0)), pl.BlockSpec((MEM_LEN, 2 * MEM_W), lambda i: (0, 0)),
                  pl.BlockSpec((tm, MEM_W), lambda i: (i, 0))],
        out_specs=[pl.BlockSpec((tm, MEM_W), lambda i: (i, 0)), pl.BlockSpec((MEM_LEN, 2 * MEM_W), lambda i: (0, 0))],
        out_shape=[jax.ShapeDtypeStruct((t, MEM_W), BF16), jax.ShapeDtypeStruct((MEM_LEN, 2 * MEM_W), F32)],
        compiler_params=_cparams("arbitrary"),
    )(p, kv, dy)


def _merge_fwd(p, b_ssd, b_gmlp, b_mem, name):
    t = p.shape[0]
    tm = _row_tile(t)
    g0 = P_GL // D_MODEL

    def body(g1, g2, g3, b1, b2, b3, o_ref):
        def strip(rows, carry):
            acc = _sigmoid(g1[rows, :].astype(F32)) * b1[rows, :].astype(F32)
            acc = acc + _sigmoid(g2[rows, :].astype(F32)) * b2[rows, :].astype(F32)
            acc = acc + _sigmoid(g3[rows, :].astype(F32)) * b3[rows, :].astype(F32)
            o_ref[rows, :] = acc.astype(o_ref.dtype)
            return carry

        _strips(tm, strip, 0)

    row = pl.BlockSpec((tm, D_MODEL), lambda i: (i, 0))
    return pl.pallas_call(
        body, name=name, grid=(t // tm,),
        in_specs=[pl.BlockSpec((tm, D_MODEL), lambda i, k=k: (i, g0 + k)) for k in range(3)] + [row] * 3,
        out_specs=row, out_shape=jax.ShapeDtypeStruct((t, D_MODEL), BF16), compiler_params=_cparams("parallel"),
    )(p, p, p, b_ssd, b_gmlp, b_mem)


def _merge_bwd(p, dm, b_ssd, b_gmlp, b_mem, name):
    t = p.shape[0]
    tm = _row_tile(t)
    g0 = P_GL // D_MODEL

    def body(g1, g2, g3, dm_ref, b1, b2, b3, d1, d2, d3, dgl_ref):
        def strip(rows, carry):
            dmv = dm_ref[rows, :].astype(F32)
            for k, (g_ref, b_ref, d_ref) in enumerate(((g1, b1, d1), (g2, b2, d2), (g3, b3, d3))):
                sg = _sigmoid(g_ref[rows, :].astype(F32))
                d_ref[rows, :] = (dmv * sg).astype(d_ref.dtype)
                dgl_ref[rows, k * D_MODEL:(k + 1) * D_MODEL] = (dmv * b_ref[rows, :].astype(F32) * sg * (1.0 - sg)).astype(dgl_ref.dtype)
            return carry

        _strips(tm, strip, 0)

    row = pl.BlockSpec((tm, D_MODEL), lambda i: (i, 0))
    return pl.pallas_call(
        body, name=name, grid=(t // tm,),
        in_specs=[pl.BlockSpec((tm, D_MODEL), lambda i, k=k: (i, g0 + k)) for k in range(3)] + [row] * 4,
        out_specs=[row, row, row, pl.BlockSpec((tm, 3 * D_MODEL), lambda i: (i, 0))],
        out_shape=[jax.ShapeDtypeStruct((t, D_MODEL), BF16)] * 3 + [jax.ShapeDtypeStruct((t, 3 * D_MODEL), BF16)],
        compiler_params=_cparams("parallel"),
    )(p, p, p, dm, b_ssd, b_gmlp, b_mem)


def _local_step(x, mem, target, w, rest_weights, push):
    t = x.shape[0]
    mm = functools.partial(_matmul, tk=1024)

    h1, ffn1_saved = _ffn_forward(x, w["ffn1_norm"], w["ffn1_w_gate"], w["ffn1_w_up"], w["ffn1_w_down"], "ffn1")
    w = {**w, **rest_weights(h1)}
    n2 = _rms_fwd(h1, w["mix_norm"], "mix_norm")
    p = mm(n2, w["w_in_p"], mode="nn", out_dtype=BF16, tm=512, tn=1536, name="in_proj")
    dt_raw = mm(n2, w["w_dt"], mode="nn", out_dtype=F32, tm=512, tn=128, name="dt_proj")
    dt_bias = jnp.pad(w["ssd_dt_bias"], (0, 128 - SSD_HEADS)).reshape(1, 128)
    hv = _head_inputs(_dt_fwd(dt_raw, dt_bias, "dt_fwd"), w["ssd_a_log"], w["ssd_d"])
    xc = _conv_fwd(p, w["ssd_conv_w"], w["ssd_conv_b"], "conv_fwd")
    y_ssd_raw, y_ssd, states = _ssd_fwd(xc, p, hv, w["ssd_norm"], "ssd_fwd")
    b_col = w["gmlp_b_s"].reshape(GMLP_GROUPS, CHUNK, 1)
    y_gmlp = _gmlp_fwd(p, w["gmlp_v_norm"], w["gmlp_w_s"], b_col, "gmlp_fwd")
    mem_n = _rms_fwd(mem, w["mem_norm"], "mem_norm")
    kv = mm(mem_n, w["w_mem_kv"], mode="nn", out_dtype=F32, tm=256, tn=512, name="mem_kv")
    y_mem = _mem_fwd(p, kv, "mem_fwd")
    b_ssd = mm(y_ssd, w["w_branch_ssd"], mode="nn", out_dtype=BF16, tm=512, tn=1024, name="branch_ssd")
    b_gmlp = mm(y_gmlp, w["w_branch_gmlp"], mode="nn", out_dtype=BF16, tm=512, tn=1024, name="branch_gmlp")
    b_mem = mm(y_mem, w["w_branch_mem"], mode="nn", out_dtype=BF16, tm=512, tn=1024, name="branch_mem")
    merged = _merge_fwd(p, b_ssd, b_gmlp, b_mem, "merge_fwd")
    h2 = mm(merged, w["w_out"], mode="nn", out_dtype=F32, tm=512, tn=1024, addend=h1, name="out_proj")
    h3, ffn2_saved = _ffn_forward(h2, w["ffn2_norm"], w["ffn2_w_gate"], w["ffn2_w_up"], w["ffn2_w_down"], "ffn2")
    dh3, d_final, loss = _loss_head(h3, w["final_norm"], target, "loss_head")

    g = {"final_norm": d_final}
    big = {}
    dh2, g["ffn2_norm"] = _ffn_backward(dh3, h2, w["ffn2_norm"], w["ffn2_w_gate"], w["ffn2_w_up"], w["ffn2_w_down"], ffn2_saved,
                                        "ffn2", functools.partial(push, 0))
    dmerged = mm(dh2, w["w_out"], mode="nt", out_dtype=BF16, tm=512, tn=1024, name="out_proj_dx")
    big["w_out"] = mm(merged, dh2, mode="tn", out_dtype=BF16, tm=1024, tn=1024, name="out_proj_dw")
    db_ssd, db_gmlp, db_mem, dgl = _merge_bwd(p, dmerged, b_ssd, b_gmlp, b_mem, "merge_bwd")
    dy_ssd = mm(db_ssd, w["w_branch_ssd"], mode="nt", out_dtype=BF16, tm=512, tn=2048, name="branch_ssd_dx")
    dy_gmlp = mm(db_gmlp, w["w_branch_gmlp"], mode="nt", out_dtype=BF16, tm=512, tn=1024, name="branch_gmlp_dx")
    dy_mem = mm(db_mem, w["w_branch_mem"], mode="nt", out_dtype=BF16, tm=512, tn=256, name="branch_mem_dx")
    big["w_branch_ssd"] = mm(y_ssd, db_ssd, mode="tn", out_dtype=BF16, tm=1024, tn=1024, name="branch_ssd_dw")
    big["w_branch_gmlp"] = mm(y_gmlp, db_gmlp, mode="tn", out_dtype=BF16, tm=1024, tn=1024, name="branch_gmlp_dw")
    big["w_branch_mem"] = mm(y_mem, db_mem, mode="tn", out_dtype=BF16, tm=256, tn=1024, name="branch_mem_dw")
    dq, dkv = _mem_bwd(p, kv, dy_mem, "mem_bwd")
    big["w_mem_kv"] = mm(mem_n, dkv, mode="tn", out_dtype=BF16, tm=1024, tn=512, tk=256, name="mem_kv_dw")
    dmem_n = mm(dkv, w["w_mem_kv"], mode="nt", out_dtype=F32, tm=256, tn=1024, tk=512, name="mem_kv_dx")
    _, g["mem_norm"] = _rms_bwd(mem, w["mem_norm"], dmem_n, None, "mem_norm_bwd")
    duv, d_ws, d_bs, g["gmlp_v_norm"] = _gmlp_bwd(p, dy_gmlp, w["gmlp_v_norm"], w["gmlp_w_s"], b_col, "gmlp_bwd")
    g["gmlp_w_s"] = d_ws
    g["gmlp_b_s"] = d_bs.reshape(GMLP_GROUPS, CHUNK)
    dxs, d_bm, d_cm, dz, ddt_w, hsums, g["ssd_norm"] = _ssd_bwd(xc, p, hv, w["ssd_norm"], y_ssd_raw, dy_ssd, states, "ssd_bwd")
    heads = hsums[:, :, :HEADS_PER_GROUP]
    g["ssd_a_log"] = heads[:, 1, :].reshape(1, SSD_HEADS)
    g["ssd_d"] = heads[:, 2, :].reshape(1, SSD_HEADS)
    ddt = jnp.transpose(ddt_w[:, :, :HEADS_PER_GROUP], (1, 0, 2)).reshape(t, SSD_HEADS)
    ddt, d_bias = _dt_bwd(jnp.pad(ddt, ((0, 0), (0, 128 - SSD_HEADS))), dt_raw, dt_bias, "dt_bwd")
    g["ssd_dt_bias"] = d_bias[:, :SSD_HEADS]
    parts, dws, dbs = [], [], []
    for dyc, col0, tag in ((dxs, 0, "x"), (d_bm, SSD_INNER, "b"), (d_cm, SSD_INNER + SSD_GROUPS * SSD_STATE, "c")):
        dacc, dw_c, db_c = _conv_bwd_act(p, dyc, w["ssd_conv_w"], w["ssd_conv_b"], col0, f"conv_bwd_act_{tag}")
        parts.append(_conv_bwd_dx(dacc, w["ssd_conv_w"], col0, f"conv_bwd_dx_{tag}"))
        dws.append(dw_c)
        dbs.append(db_c)
    g["ssd_conv_w"] = jnp.concatenate(dws, axis=1)
    g["ssd_conv_b"] = jnp.concatenate(dbs, axis=1)
    dp = jnp.concatenate([dz, duv] + parts + [dgl, dq, jnp.zeros((t, P_W - P_USED), BF16)], axis=1)
    d_win_p = mm(n2, dp, mode="tn", out_dtype=BF16, tm=1024, tn=1536, name="in_proj_dw")
    d_wdt = mm(n2, ddt, mode="tn", out_dtype=BF16, tm=1024, tn=128, name="dt_proj_dw")
    sl = lambda a, o, n: a[:, o:o + n]
    big["w_in"] = jnp.concatenate([sl(d_win_p, P_Z, 2048), sl(d_win_p, P_XBC, XBC), d_wdt[:, :SSD_HEADS], sl(d_win_p, P_UV, 2048),
                                   sl(d_win_p, P_Q, MEM_W), sl(d_win_p, P_GL, 3 * D_MODEL)], axis=1)
    token = push(1, big)
    dn2 = mm(dp, w["w_in_p"], mode="nt", out_dtype=F32, tm=512, tn=1024, tk=1536, name="in_proj_dx")
    dn2 = _matmul(ddt, w["w_dt"], mode="nt", out_dtype=F32, tm=512, tn=1024, tk=128, addend=dn2, name="dt_proj_dx")
    dh1, g["mix_norm"] = _rms_bwd(h1, w["mix_norm"] + token, dn2, dh2, "mix_norm_bwd")
    dx, g["ffn1_norm"] = _ffn_backward(dh1, x, w["ffn1_norm"], w["ffn1_w_gate"], w["ffn1_w_up"], w["ffn1_w_down"], ffn1_saved,
                                       "ffn1", functools.partial(push, 2))
    return loss, dx, g


def _split_w_in(w_in):
    sl = lambda o, n: w_in[:, o:o + n]
    w_p = jnp.concatenate([sl(IN_Z, 2048), sl(IN_UV, 2048), sl(IN_XBC, XBC), sl(IN_GL, 3 * D_MODEL), sl(IN_Q, MEM_W),
                           jnp.zeros((D_MODEL, P_W - P_USED), w_in.dtype)], axis=1)
    w_dt = jnp.pad(sl(IN_DT, SSD_HEADS), ((0, 0), (0, 128 - SSD_HEADS)))
    return w_p, w_dt


def _pick_tile(rows, cap=512):
    best = None
    for tile in range(8, min(rows, cap) + 1, 8):
        if rows % tile == 0:
            best = tile
    return best if best is not None else rows


def _adamw(w, g, m, v, name):
    rows, lanes = w.shape
    tile = _pick_tile(rows, cap=max(8, (512 * 1024 // lanes) // 8 * 8))
    c1 = 1.0 / (1.0 - ADAM_B1 ** ADAM_STEP)
    c2 = 1.0 / (1.0 - ADAM_B2 ** ADAM_STEP)

    def body(w_ref, g_ref, m_ref, v_ref, d_ref, nm_ref, nv_ref):
        gv = g_ref[...]
        nm = ADAM_B1 * m_ref[...] + (1.0 - ADAM_B1) * gv
        nv = ADAM_B2 * v_ref[...] + (1.0 - ADAM_B2) * (gv * gv)
        nm_ref[...] = nm
        nv_ref[...] = nv
        d_ref[...] = -ADAM_LR * ((nm * c1) / (jnp.sqrt(nv * c2) + ADAM_EPS) + ADAM_WD * w_ref[...])

    blk = pl.BlockSpec((tile, lanes), lambda i: (i, 0))
    return pl.pallas_call(
        body, name=name, grid=(rows // tile,), in_specs=[blk] * 4, out_specs=[blk] * 3,
        out_shape=[jax.ShapeDtypeStruct((rows, lanes), F32)] * 3, compiler_params=_cparams("parallel"),
    )(w, g, m, v)


HBM = pl.BlockSpec(memory_space=pltpu.HBM)


def _place():
    x, y, c = lax.axis_index("x"), lax.axis_index("y"), lax.axis_index("c")
    chips = [(1 - x, y), (x, 1 - y), (1 - x, 1 - y)]
    return x, y, c, chips


def _gather_weights(slots):
    _, _, rh, lanes = slots.shape

    def body(in_ref, out_ref, send_sems, recv_sems):
        del in_ref
        x, y, c, chips = _place()
        me, sibling = (x, y, c), (x, y, 1 - c)

        def copy(k, src, dst, to):
            return pltpu.make_async_remote_copy(src_ref=src, dst_ref=dst, send_sem=send_sems.at[k], recv_sem=recv_sems.at[k],
                                                device_id=to, device_id_type=MESH)

        own = out_ref.at[2 * x + y, c]
        first = [copy(j, own, own, (*chip, c)) for j, chip in enumerate(chips)]
        for cp in first:
            cp.start()
        passed = []
        for j, (cx, cy) in enumerate(chips):
            landed = out_ref.at[2 * cx + cy, c]
            copy(j, landed, landed, me).wait_recv()
            fwd = copy(3 + j, landed, landed, sibling)
            fwd.start()
            passed.append(fwd)
        for j, (cx, cy) in enumerate(chips):
            other = out_ref.at[2 * cx + cy, 1 - c]
            copy(3 + j, other, other, me).wait_recv()
        for cp in first + passed:
            cp.wait_send()

    return pl.pallas_call(
        body, name="gather_weights", out_shape=jax.ShapeDtypeStruct(slots.shape, slots.dtype),
        in_specs=[HBM], out_specs=HBM, input_output_aliases={0: 0},
        scratch_shapes=[pltpu.SemaphoreType.DMA((6,)), pltpu.SemaphoreType.DMA((6,))],
    )(slots)


SEM = pl.BlockSpec(memory_space=pltpu.SEMAPHORE)
EFFECT = pltpu.SideEffectType.DATAFLOW_SIDE_EFFECTING
N_PEER = 3


def _sem_outs():
    return tuple(pltpu.SemaphoreType.DMA(()) for _ in range(2 * N_PEER))


def _gather_start(slots):
    def body(in_ref, *refs):
        del in_ref
        sems, thru, token = refs[:2 * N_PEER], refs[2 * N_PEER], refs[2 * N_PEER + 1]
        x, y, c, chips = _place()
        own = thru.at[2 * x + y, c]
        for j, chip in enumerate(chips):
            pltpu.make_async_remote_copy(src_ref=own, dst_ref=own, send_sem=sems[j], recv_sem=sems[N_PEER + j],
                                         device_id=(*chip, c), device_id_type=MESH).start()
        token[...] = jnp.zeros_like(token)

    out = pl.pallas_call(
        body, name="gather_rest_start",
        out_shape=_sem_outs() + (pltpu.HBM(slots.shape, slots.dtype), jax.ShapeDtypeStruct((8, 128), F32)),
        in_specs=(HBM,), out_specs=(SEM,) * (2 * N_PEER) + (HBM, pl.BlockSpec(memory_space=pltpu.VMEM)),
        input_output_aliases={0: 2 * N_PEER}, compiler_params=pltpu.CompilerParams(has_side_effects=EFFECT),
    )(pltpu.with_memory_space_constraint(slots, pltpu.HBM))
    return out[:2 * N_PEER], out[2 * N_PEER], out[2 * N_PEER + 1]


def _gather_wait(sems, thru, after):
    def body(in_ref, *refs):
        del in_ref
        sems, out_ref = refs[:2 * N_PEER], refs[2 * N_PEER + 1]
        x, y, c, chips = _place()
        own = out_ref.at[2 * x + y, c]
        for j, (cx, cy) in enumerate(chips):
            cp = pltpu.make_async_remote_copy(src_ref=own, dst_ref=out_ref.at[2 * cx + cy, c], send_sem=sems[j],
                                              recv_sem=sems[N_PEER + j], device_id=(cx, cy, c), device_id_type=MESH)
            cp.wait_send()
            cp.wait_recv()

    return pl.pallas_call(
        body, name="gather_rest_wait", out_shape=pltpu.HBM(thru.shape, thru.dtype),
        in_specs=(HBM,) + (SEM,) * (2 * N_PEER) + (pl.BlockSpec(memory_space=pl.ANY),), out_specs=HBM,
        input_output_aliases={0: 0}, compiler_params=pltpu.CompilerParams(has_side_effects=EFFECT),
    )(thru, *sems, after)


def _gather_forward(slots):
    def body(in_ref, out_ref, send_sems, recv_sems):
        del in_ref
        x, y, c, chips = _place()
        cps = []
        for j, (cx, cy) in enumerate(chips):
            landed = out_ref.at[2 * cx + cy, c]
            cps.append(pltpu.make_async_remote_copy(src_ref=landed, dst_ref=landed, send_sem=send_sems.at[j], recv_sem=recv_sems.at[j],
                                                    device_id=(x, y, 1 - c), device_id_type=MESH))
        for cp in cps:
            cp.start()
        for j, (cx, cy) in enumerate(chips):
            other = out_ref.at[2 * cx + cy, 1 - c]
            pltpu.make_async_remote_copy(src_ref=other, dst_ref=other, send_sem=send_sems.at[j], recv_sem=recv_sems.at[j],
                                         device_id=(x, y, 1 - c), device_id_type=MESH).wait_recv()
        for cp in cps:
            cp.wait_send()

    return pl.pallas_call(
        body, name="gather_rest_forward", out_shape=jax.ShapeDtypeStruct(slots.shape, slots.dtype),
        in_specs=[HBM], out_specs=HBM, input_output_aliases={0: 0},
        scratch_shapes=[pltpu.SemaphoreType.DMA((N_PEER,)), pltpu.SemaphoreType.DMA((N_PEER,))],
    )(slots)


def _scatter_start(pa, tag):
    ns, rh, lanes = pa.shape
    land = pltpu.with_memory_space_constraint(lax.empty((N_PEER, rh, lanes), pa.dtype), pltpu.HBM)

    def body(pa_ref, land_ref, *refs):
        x, y, c, chips = _place()
        for j, (cx, cy) in enumerate(chips):
            pltpu.make_async_remote_copy(src_ref=pa_ref.at[2 * cx + cy], dst_ref=land_ref.at[j], send_sem=refs[j],
                                         recv_sem=refs[N_PEER + j], device_id=(cx, cy, c), device_id_type=MESH).start()
        refs[-1][...] = jnp.zeros_like(refs[-1])

    out = pl.pallas_call(
        body, name=f"scatter_start_{tag}",
        out_shape=_sem_outs() + (pltpu.HBM(pa.shape, pa.dtype), pltpu.HBM(land.shape, land.dtype), jax.ShapeDtypeStruct((8, 128), F32)),
        in_specs=(HBM, HBM), out_specs=(SEM,) * (2 * N_PEER) + (HBM, HBM, pl.BlockSpec(memory_space=pltpu.VMEM)),
        input_output_aliases={0: 2 * N_PEER, 1: 2 * N_PEER + 1}, compiler_params=pltpu.CompilerParams(has_side_effects=EFFECT),
    )(pltpu.with_memory_space_constraint(pa, pltpu.HBM), land)
    return (out[:2 * N_PEER], out[2 * N_PEER], out[2 * N_PEER + 1]), out[2 * N_PEER + 2]


def _scatter_wait(sems, pa_thru, land_thru, after, tag):
    def body(pa_ref, land_ref, *refs):
        sems = refs[:2 * N_PEER]
        x, y, c, chips = _place()
        for j, (cx, cy) in enumerate(chips):
            cp = pltpu.make_async_remote_copy(src_ref=pa_ref.at[2 * cx + cy], dst_ref=land_ref.at[j], send_sem=sems[j],
                                              recv_sem=sems[N_PEER + j], device_id=(cx, cy, c), device_id_type=MESH)
            cp.wait_send()
            cp.wait_recv()

    return pl.pallas_call(
        body, name=f"scatter_wait_{tag}",
        out_shape=(pltpu.HBM(pa_thru.shape, pa_thru.dtype), pltpu.HBM(land_thru.shape, land_thru.dtype)),
        in_specs=(HBM, HBM) + (SEM,) * (2 * N_PEER) + (pl.BlockSpec(memory_space=pl.ANY),), out_specs=(HBM, HBM),
        input_output_aliases={0: 0, 1: 1}, compiler_params=pltpu.CompilerParams(has_side_effects=EFFECT),
    )(pa_thru, land_thru, *sems, after)


def _rs_swap(gp, tag):
    _, ns, rh, lanes = gp.shape

    def body(in_ref, out_ref, send_sem, recv_sem):
        x, y, c, _ = _place()
        cp = pltpu.make_async_remote_copy(src_ref=in_ref.at[1 - c], dst_ref=out_ref, send_sem=send_sem, recv_sem=recv_sem,
                                          device_id=(x, y, 1 - c), device_id_type=MESH)
        cp.start()
        cp.wait_send()
        cp.wait_recv()

    return pl.pallas_call(
        body, name=f"rs_swap_{tag}", out_shape=jax.ShapeDtypeStruct((ns, rh, lanes), gp.dtype), in_specs=[HBM], out_specs=HBM,
        scratch_shapes=[pltpu.SemaphoreType.DMA, pltpu.SemaphoreType.DMA],
    )(gp)


def _rs_tile(rh):
    return _pick_tile(rh, cap=512)


def _rs_add(gp, recv, c, tag):
    _, ns, rh, lanes = gp.shape
    tile = _rs_tile(rh)

    def body(c_ref, a_ref, b_ref, o_ref):
        o_ref[...] = (a_ref[...].astype(F32) + b_ref[...].astype(F32)).astype(o_ref.dtype)

    return pl.pallas_call(
        body, name=f"rs_add_{tag}", out_shape=jax.ShapeDtypeStruct((ns, rh, lanes), gp.dtype),
        grid_spec=pltpu.PrefetchScalarGridSpec(
            num_scalar_prefetch=1, grid=(ns, rh // tile),
            in_specs=[pl.BlockSpec((None, None, tile, lanes), lambda s, i, c_ref: (c_ref[0], s, i, 0)),
                      pl.BlockSpec((None, tile, lanes), lambda s, i, c_ref: (s, i, 0))],
            out_specs=pl.BlockSpec((None, tile, lanes), lambda s, i, c_ref: (s, i, 0))),
        compiler_params=_cparams("parallel", "parallel"),
    )(c, gp, recv)


def _rs_sum(pa, recv, place, tag):
    ns, rh, lanes = pa.shape
    tile = _rs_tile(rh)

    def body(place_ref, a_ref, r_ref, o_ref):
        acc = a_ref[...].astype(F32)
        for j in range(ns - 1):
            acc = acc + r_ref[j].astype(F32)
        o_ref[...] = acc

    return pl.pallas_call(
        body, name=f"rs_sum_{tag}", out_shape=jax.ShapeDtypeStruct((2, rh, lanes), F32),
        grid_spec=pltpu.PrefetchScalarGridSpec(
            num_scalar_prefetch=1, grid=(rh // tile,),
            in_specs=[pl.BlockSpec((None, tile, lanes), lambda i, place_ref: (place_ref[0], i, 0)),
                      pl.BlockSpec((ns - 1, tile, lanes), lambda i, place_ref: (0, i, 0))],
            out_specs=pl.BlockSpec((None, tile, lanes), lambda i, place_ref: (place_ref[1], i, 0))),
        compiler_params=_cparams("parallel"),
    )(place, pa, recv)


def _rs_share(halves, tag):
    def body(in_ref, out_ref, send_sem, recv_sem):
        del in_ref
        x, y, c, _ = _place()
        cp = pltpu.make_async_remote_copy(src_ref=out_ref.at[c], dst_ref=out_ref.at[c], send_sem=send_sem, recv_sem=recv_sem,
                                          device_id=(x, y, 1 - c), device_id_type=MESH)
        cp.start()
        other = out_ref.at[1 - c]
        pltpu.make_async_remote_copy(src_ref=other, dst_ref=other, send_sem=send_sem, recv_sem=recv_sem,
                                     device_id=(x, y, 1 - c), device_id_type=MESH).wait_recv()
        cp.wait_send()

    return pl.pallas_call(
        body, name=f"rs_share_{tag}", out_shape=jax.ShapeDtypeStruct(halves.shape, halves.dtype), in_specs=[HBM], out_specs=HBM,
        input_output_aliases={0: 0}, scratch_shapes=[pltpu.SemaphoreType.DMA, pltpu.SemaphoreType.DMA],
    )(halves)


N_DEV = 8
SMALL_ROWS = 160


def _allreduce_small(v):
    m_per, n = v.shape

    def body(x_ref, out_ref, all_ref, send_sems, recv_sems, local_sem):
        x, y, c, chips = _place()
        me, sibling = (x, y, c), (x, y, 1 - c)

        def rows(px, py, pc):
            return all_ref.at[pl.ds((4 * px + 2 * py + pc) * m_per, m_per), :]

        def copy(k, block, to, src=None):
            return pltpu.make_async_remote_copy(src_ref=rows(*block) if src is None else src, dst_ref=rows(*block),
                                                send_sem=send_sems.at[k], recv_sem=recv_sems.at[k], device_id=to, device_id_type=MESH)

        mine = pltpu.make_async_copy(x_ref, rows(*me), local_sem)
        mine.start()
        first = [copy(0, me, sibling, src=x_ref)]
        first += [copy(1 + j, me, (*chip, c), src=x_ref) for j, chip in enumerate(chips)]
        for cp in first:
            cp.start()
        passed = [copy(4 + j, (*chip, c), sibling) for j, chip in enumerate(chips)]
        for j, chip in enumerate(chips):
            copy(1 + j, (*chip, c), me).wait_recv()
            passed[j].start()
        copy(0, sibling, me).wait_recv()
        for j, chip in enumerate(chips):
            copy(4 + j, (*chip, 1 - c), me).wait_recv()
        for cp in first + passed:
            cp.wait_send()
        mine.wait()
        step = 32
        for r in range(0, m_per, step):
            acc = all_ref[r:r + step, :]
            for d in range(1, N_DEV):
                acc = acc + all_ref[d * m_per + r:d * m_per + r + step, :]
            out_ref[r:r + step, :] = acc

    vm = pl.BlockSpec(memory_space=pltpu.VMEM)
    return pl.pallas_call(
        body, name="allreduce_small", out_shape=jax.ShapeDtypeStruct((m_per, n), v.dtype), in_specs=[vm], out_specs=vm,
        scratch_shapes=[pltpu.VMEM((N_DEV * m_per, n), v.dtype), pltpu.SemaphoreType.DMA((7,)), pltpu.SemaphoreType.DMA((7,)),
                        pltpu.SemaphoreType.DMA],
        compiler_params=pltpu.CompilerParams(vmem_limit_bytes=V7X_VMEM_LIMIT),
    )(v)


BIG = {"ffn1_w_gate": ((D_MODEL, D_FF), 1), "ffn1_w_up": ((D_MODEL, D_FF), 1), "ffn1_w_down": ((D_FF, D_MODEL), 0),
       "ffn2_w_gate": ((D_MODEL, D_FF), 1), "ffn2_w_up": ((D_MODEL, D_FF), 1), "ffn2_w_down": ((D_FF, D_MODEL), 0),
       "w_in": ((D_MODEL, IN_WIDTH), 1), "w_mem_kv": ((D_MODEL, 2 * MEM_W), 0), "w_branch_ssd": ((SSD_INNER, D_MODEL), 0),
       "w_branch_gmlp": ((GMLP_W, D_MODEL), 0), "w_branch_mem": ((MEM_W, D_MODEL), 1), "w_out": ((D_MODEL, D_MODEL), 0)}
FFN1 = ("ffn1_w_gate", "ffn1_w_up", "ffn1_w_down")
FFN2 = ("ffn2_w_gate", "ffn2_w_up", "ffn2_w_down")
MIXER = ("w_out", "w_branch_ssd", "w_branch_gmlp", "w_branch_mem", "w_mem_kv", "w_in")
GATHER_GROUPS = (FFN1, FFN2 + MIXER)
REDUCE_GROUPS = (FFN2, MIXER, FFN1)
CONV_W_ROWS = 8


def _shard_rows_of(name):
    (a, b), _ = BIG[name]
    return a * b // N_SHARD // LANES


def _group_rows(names, extra=0):
    return -(-(sum(_shard_rows_of(n) for n in names) + extra) // 32) * 32

SMALL = [("ffn1_norm", 1), ("mix_norm", 1), ("mem_norm", 1), ("ssd_conv_b", 3), ("heads", 1), ("ssd_norm", 2),
         ("gmlp_v_norm", 1), ("gmlp_w_s", 128), ("gmlp_b_s", 1), ("ffn2_norm", 1), ("final_norm", 1), ("ssd_conv_w", 12)]
assert sum(n for _, n in SMALL) <= SMALL_ROWS
HEAD_VECS = ("ssd_dt_bias", "ssd_a_log", "ssd_d")


def _pack_small(vals, loss=None):
    parts = []
    for name, nrows in SMALL:
        if name == "heads":
            row = jnp.concatenate([vals[k].reshape(-1) for k in HEAD_VECS]
                                  + [jnp.zeros((1,), F32) if loss is None else loss.reshape(1)])
            parts.append(jnp.pad(row, (0, LANES - row.shape[0])).reshape(1, LANES))
        elif name in vals:
            parts.append(vals[name].reshape(nrows, LANES))
        else:
            parts.append(jnp.zeros((nrows, LANES), F32))
    buf = jnp.concatenate(parts, axis=0)
    return jnp.pad(buf, ((0, SMALL_ROWS - buf.shape[0]), (0, 0)))


def _unpack_small(buf):
    out, r = {}, 0
    for name, nrows in SMALL:
        blk = buf[r:r + nrows]
        r += nrows
        if name == "heads":
            for i, k in enumerate(HEAD_VECS):
                out[k] = blk[0, i * SSD_HEADS:(i + 1) * SSD_HEADS]
            out["loss"] = blk[0, 3 * SSD_HEADS]
        else:
            out[name] = blk
    return out


def _shard_rows(full, axis):
    a, b = full.shape
    if axis == 1:
        full = jnp.transpose(full.reshape(a, N_SHARD, b // N_SHARD), (1, 0, 2))
    return full.reshape(N_SHARD, a * b // N_SHARD // LANES, LANES)


def _unshard_rows(rows, shape, axis):
    a, b = shape
    if axis == 1:
        return jnp.transpose(rows.reshape(N_SHARD, a, b // N_SHARD), (1, 0, 2)).reshape(a, b)
    return rows.reshape(a, b)


def _pack_weights(given, names, conv=False):
    parts = [given[n].astype(BF16).reshape(_shard_rows_of(n), LANES) for n in names]
    if conv:
        pairs = lax.bitcast_convert_type(given["ssd_conv_w"], BF16).reshape(-1)
        parts.append(jnp.pad(pairs, (0, CONV_W_ROWS * LANES - pairs.shape[0])).reshape(CONV_W_ROWS, LANES))
    total = _group_rows(names, CONV_W_ROWS if conv else 0)
    packed = jnp.concatenate(parts, axis=0)
    packed = jnp.pad(packed, ((0, total - packed.shape[0]), (0, 0))).reshape(1, 2, total // 2, LANES)
    return jnp.broadcast_to(packed, (N_SHARD, 2, total // 2, LANES))


def _unpack_weights(slots, names, conv=False):
    rows = slots.reshape(N_SHARD, -1, LANES)
    out, r = {}, 0
    for name in names:
        n = _shard_rows_of(name)
        out[name] = _unshard_rows(rows[:, r:r + n], *BIG[name])
        r += n
    if conv:
        cols = XBC // N_SHARD
        pairs = rows[:, r:r + CONV_W_ROWS].reshape(N_SHARD, -1)[:, :SSD_CONV * cols * 2].reshape(N_SHARD, SSD_CONV, cols, 2)
        out["ssd_conv_w"] = jnp.transpose(lax.bitcast_convert_type(pairs, F32), (1, 0, 2)).reshape(SSD_CONV, XBC)
    return out


def _pack_grads(grads, names):
    total = _group_rows(names)
    gp = jnp.concatenate([_shard_rows(grads[n], BIG[n][1]).astype(BF16) for n in names], axis=1)
    gp = jnp.pad(gp, ((0, 0), (0, total - gp.shape[1]), (0, 0))).reshape(N_SHARD, 2, total // 2, LANES)
    return jnp.transpose(gp, (1, 0, 2, 3))


def kernel(x, mem, ffn1_norm, ffn1_w_gate, ffn1_w_up, ffn1_w_down, mix_norm, mem_norm, w_in, ssd_conv_w, ssd_conv_b, ssd_dt_bias, ssd_a_log, ssd_d, ssd_norm, gmlp_v_norm, gmlp_w_s, gmlp_b_s, w_mem_kv, w_branch_ssd, w_branch_gmlp, w_branch_mem, w_out, ffn2_norm, ffn2_w_gate, ffn2_w_up, ffn2_w_down, final_norm, loss_target, m_ffn1_norm, m_ffn1_w_gate, m_ffn1_w_up, m_ffn1_w_down, m_mix_norm, m_mem_norm, m_w_in, m_ssd_conv_w, m_ssd_conv_b, m_ssd_dt_bias, m_ssd_a_log, m_ssd_d, m_ssd_norm, m_gmlp_v_norm, m_gmlp_w_s, m_gmlp_b_s, m_w_mem_kv, m_w_branch_ssd, m_w_branch_gmlp, m_w_branch_mem, m_w_out, m_ffn2_norm, m_ffn2_w_gate, m_ffn2_w_up, m_ffn2_w_down, m_final_norm, v_ffn1_norm, v_ffn1_w_gate, v_ffn1_w_up, v_ffn1_w_down, v_mix_norm, v_mem_norm, v_w_in, v_ssd_conv_w, v_ssd_conv_b, v_ssd_dt_bias, v_ssd_a_log, v_ssd_d, v_ssd_norm, v_gmlp_v_norm, v_gmlp_w_s, v_gmlp_b_s, v_w_mem_kv, v_w_branch_ssd, v_w_branch_gmlp, v_w_branch_mem, v_w_out, v_ffn2_norm, v_ffn2_w_gate, v_ffn2_w_up, v_ffn2_w_down, v_final_norm):
    given = dict(x=x, mem=mem, ffn1_norm=ffn1_norm, ffn1_w_gate=ffn1_w_gate, ffn1_w_up=ffn1_w_up, ffn1_w_down=ffn1_w_down, mix_norm=mix_norm, mem_norm=mem_norm, w_in=w_in, ssd_conv_w=ssd_conv_w, ssd_conv_b=ssd_conv_b, ssd_dt_bias=ssd_dt_bias, ssd_a_log=ssd_a_log, ssd_d=ssd_d, ssd_norm=ssd_norm, gmlp_v_norm=gmlp_v_norm, gmlp_w_s=gmlp_w_s, gmlp_b_s=gmlp_b_s, w_mem_kv=w_mem_kv, w_branch_ssd=w_branch_ssd, w_branch_gmlp=w_branch_gmlp, w_branch_mem=w_branch_mem, w_out=w_out, ffn2_norm=ffn2_norm, ffn2_w_gate=ffn2_w_gate, ffn2_w_up=ffn2_w_up, ffn2_w_down=ffn2_w_down, final_norm=final_norm, loss_target=loss_target, m_ffn1_norm=m_ffn1_norm, m_ffn1_w_gate=m_ffn1_w_gate, m_ffn1_w_up=m_ffn1_w_up, m_ffn1_w_down=m_ffn1_w_down, m_mix_norm=m_mix_norm, m_mem_norm=m_mem_norm, m_w_in=m_w_in, m_ssd_conv_w=m_ssd_conv_w, m_ssd_conv_b=m_ssd_conv_b, m_ssd_dt_bias=m_ssd_dt_bias, m_ssd_a_log=m_ssd_a_log, m_ssd_d=m_ssd_d, m_ssd_norm=m_ssd_norm, m_gmlp_v_norm=m_gmlp_v_norm, m_gmlp_w_s=m_gmlp_w_s, m_gmlp_b_s=m_gmlp_b_s, m_w_mem_kv=m_w_mem_kv, m_w_branch_ssd=m_w_branch_ssd, m_w_branch_gmlp=m_w_branch_gmlp, m_w_branch_mem=m_w_branch_mem, m_w_out=m_w_out, m_ffn2_norm=m_ffn2_norm, m_ffn2_w_gate=m_ffn2_w_gate, m_ffn2_w_up=m_ffn2_w_up, m_ffn2_w_down=m_ffn2_w_down, m_final_norm=m_final_norm, v_ffn1_norm=v_ffn1_norm, v_ffn1_w_gate=v_ffn1_w_gate, v_ffn1_w_up=v_ffn1_w_up, v_ffn1_w_down=v_ffn1_w_down, v_mix_norm=v_mix_norm, v_mem_norm=v_mem_norm, v_w_in=v_w_in, v_ssd_conv_w=v_ssd_conv_w, v_ssd_conv_b=v_ssd_conv_b, v_ssd_dt_bias=v_ssd_dt_bias, v_ssd_a_log=v_ssd_a_log, v_ssd_d=v_ssd_d, v_ssd_norm=v_ssd_norm, v_gmlp_v_norm=v_gmlp_v_norm, v_gmlp_w_s=v_gmlp_w_s, v_gmlp_b_s=v_gmlp_b_s, v_w_mem_kv=v_w_mem_kv, v_w_branch_ssd=v_w_branch_ssd, v_w_branch_gmlp=v_w_branch_gmlp, v_w_branch_mem=v_w_branch_mem, v_w_out=v_w_out, v_ffn2_norm=v_ffn2_norm, v_ffn2_w_gate=v_ffn2_w_gate, v_ffn2_w_up=v_ffn2_w_up, v_ffn2_w_down=v_ffn2_w_down, v_final_norm=v_final_norm)
    weights = [n for n in given if n not in ("x", "mem", "loss_target") and not n.startswith(("m_", "v_"))]
    xi, yi, ci = lax.axis_index("x"), lax.axis_index("y"), lax.axis_index("c")
    chip = (2 * xi + yi).astype(jnp.int32)
    core = ci.astype(jnp.int32)
    conv_cols = XBC // N_SHARD

    first = _gather_weights(_pack_weights(given, FFN1))
    first, rest_slots = lax.optimization_barrier((first, _pack_weights(given, GATHER_GROUPS[1], conv=True)))
    sems, thru, token = _gather_start(rest_slots)
    w = _unpack_weights(first, FFN1)
    for name in ("ffn1_norm", "mix_norm", "mem_norm", "ssd_conv_b", "ssd_norm", "gmlp_v_norm", "ffn2_norm", "final_norm"):
        w[name] = given[name].reshape(1, -1)
    w["ffn1_norm"] = w["ffn1_norm"] + token[0:1, 0:1]
    for name in HEAD_VECS:
        w[name] = given[name].reshape(-1)
    w["gmlp_w_s"] = given["gmlp_w_s"][0]
    w["gmlp_b_s"] = given["gmlp_b_s"][0]

    def rest_weights(after):
        rest = _unpack_weights(_gather_forward(_gather_wait(sems, thru, after)), GATHER_GROUPS[1], conv=True)
        rest["w_in_p"], rest["w_dt"] = _split_w_in(rest.pop("w_in"))
        return rest

    pending = {}

    def push(k, group_grads):
        gp = _pack_grads(group_grads, REDUCE_GROUPS[k])
        pa = _rs_add(gp, _rs_swap(gp, k), core.reshape(1), k)
        pending[k], token = _scatter_start(pa, k)
        return token[0:1, 0:1]

    def reduced(k, after):
        pa, land = _scatter_wait(*pending[k], after, k)
        gsum = _rs_share(_rs_sum(pa, land, jnp.stack([chip, core]), k), k)
        rows = gsum.reshape(-1, LANES)
        out, r = {}, 0
        for name in REDUCE_GROUPS[k]:
            n = _shard_rows_of(name)
            out[name] = rows[r:r + n].reshape(given[name].shape[1:])
            r += n
        return out

    loss_part, grad_x, g = _local_step(x[0], mem[0], loss_target[0], w, rest_weights, push)

    grads, deltas, new_m, new_v = {}, {}, {}, {}

    def update(k, after):
        for name, gl in reduced(k, after).items():
            d, nm, nv = _adamw(given[name][0], gl, given["m_" + name][0], given["v_" + name][0], f"adamw_{name}")
            grads[name], deltas[name], new_m[name], new_v[name] = (a[None] for a in (gl, d, nm, nv))

    update(0, grad_x)
    update(1, deltas[REDUCE_GROUPS[0][-1]])

    small_vals = {k: g[k] for k, _ in SMALL if k != "heads"}
    small_vals.update({k: g[k] for k in HEAD_VECS})
    red = _unpack_small(_allreduce_small(_pack_small(small_vals, loss=loss_part[0, 0])))
    update(2, deltas[REDUCE_GROUPS[1][-1]])
    conv_g = lax.dynamic_slice_in_dim(red["ssd_conv_w"].reshape(SSD_CONV, XBC), chip * conv_cols, conv_cols, axis=1)
    d, nm, nv = _adamw(given["ssd_conv_w"][0], conv_g, given["m_ssd_conv_w"][0], given["v_ssd_conv_w"][0], "adamw_conv_w")
    grads["ssd_conv_w"], deltas["ssd_conv_w"], new_m["ssd_conv_w"], new_v["ssd_conv_w"] = (a[None] for a in (conv_g, d, nm, nv))
    small_names = [k for k, _ in SMALL if k not in ("heads", "ssd_conv_w")] + list(HEAD_VECS)
    pack_of = lambda prefix: _pack_small({k: given[prefix + k] for k in small_names})
    g_small = _pack_small({k: red[k] for k in small_names})
    d, nm, nv = (_unpack_small(a) for a in _adamw(pack_of(""), g_small, pack_of("m_"), pack_of("v_"), "adamw_small"))
    for k in small_names:
        shape = given[k].shape
        grads[k], deltas[k], new_m[k], new_v[k] = (a[k].reshape(shape) for a in (red, d, nm, nv))

    return (red["loss"], grad_x[None], *[grads[n] for n in weights], *[deltas[n] for n in weights],
            *[new_m[n] for n in weights], *[new_v[n] for n in weights])
```

```python
import functools
import math

import jax
import jax.numpy as jnp
from jax import lax
from jax.experimental import pallas as pl
from jax.experimental.pallas import tpu as pltpu

F32, BF16 = jnp.float32, jnp.bfloat16
HI = lax.Precision.HIGHEST
MESH = pl.DeviceIdType.MESH

D_MODEL = 1024
D_FF = 2816
MEM_LEN = 256
SSD_INNER = 2048
SSD_HEADS = 32
SSD_GROUPS = 4
SSD_STATE = 128
SSD_CONV = 4
CHUNK = 128
XBC = SSD_INNER + 2 * SSD_GROUPS * SSD_STATE
GMLP_W = 1024
GMLP_GROUPS = 8
MEM_W = 256
MEM_HEADS = 4
EPS = 1e-6
IN_WIDTH = 10528
IN_Z, IN_XBC, IN_DT, IN_UV, IN_Q, IN_GL = 0, 2048, 5120, 5152, 7200, 7456
P_Z, P_UV, P_XBC, P_GL, P_Q, P_W = 0, 2048, 4096, 7168, 10240, 10752
P_USED = 10496

ADAM_LR, ADAM_B1, ADAM_B2, ADAM_EPS, ADAM_WD, ADAM_STEP = 0.001, 0.9, 0.999, 1e-08, 0.01, 10

V7X_VMEM_LIMIT = 56 * 1024 * 1024
N_SHARD = 4
LANES = 1024


def _cparams(*sem):
    return pltpu.CompilerParams(dimension_semantics=sem, vmem_limit_bytes=V7X_VMEM_LIMIT)


def _sigmoid(x):
    return 1.0 / (1.0 + jnp.exp(-x))


def _row_tile(t):
    return min(512, t)


_DIMS = {"nn": (((1,), (0,)), ((), ())), "nt": (((1,), (1,)), ((), ())), "tn": (((0,), (0,)), ((), ()))}


def _matmul(a, b, *, mode, out_dtype, tm, tn, tk, name, scale=1.0, addend=None):
    if mode == "tn":
        k_dim, m_dim = a.shape
    else:
        m_dim, k_dim = a.shape
    n_dim = b.shape[0] if mode == "nt" else b.shape[1]
    tm, tn, tk = min(tm, m_dim), min(tn, n_dim), min(tk, k_dim)
    assert m_dim % tm == 0 and n_dim % tn == 0 and k_dim % tk == 0, (name, a.shape, b.shape, tm, tn, tk)
    ni, nj, nk = m_dim // tm, n_dim // tn, k_dim // tk
    a_spec = pl.BlockSpec((tk, tm), lambda j, i, k: (k, i)) if mode == "tn" else pl.BlockSpec((tm, tk), lambda j, i, k: (i, k))
    b_spec = pl.BlockSpec((tn, tk), lambda j, i, k: (j, k)) if mode == "nt" else pl.BlockSpec((tk, tn), lambda j, i, k: (k, j))
    o_spec = pl.BlockSpec((tm, tn), lambda j, i, k: (i, j))
    dims = _DIMS[mode]
    has_add = addend is not None

    def body(*refs):
        if has_add:
            a_ref, b_ref, r_ref, o_ref, acc_ref = refs
        else:
            a_ref, b_ref, o_ref, acc_ref = refs
        k = pl.program_id(2)

        @pl.when(k == 0)
        def _():
            acc_ref[...] = jnp.zeros_like(acc_ref)

        acc_ref[...] += lax.dot_general(a_ref[...].astype(BF16), b_ref[...].astype(BF16), dims, preferred_element_type=F32)

        @pl.when(k == nk - 1)
        def _():
            r = acc_ref[...] * scale
            if has_add:
                r = r + r_ref[...].astype(F32)
            o_ref[...] = r.astype(o_ref.dtype)

    in_specs = [a_spec, b_spec] + ([o_spec] if has_add else [])
    args = (a, b) + ((addend,) if has_add else ())
    return pl.pallas_call(
        body, name=name, grid=(nj, ni, nk), in_specs=in_specs, out_specs=o_spec,
        out_shape=jax.ShapeDtypeStruct((m_dim, n_dim), out_dtype),
        scratch_shapes=[pltpu.VMEM((tm, tn), F32)],
        compiler_params=_cparams("parallel", "parallel", "arbitrary"),
    )(*args)


ROW_STRIP = 16


def _strips(tm, fn, init=None, rb=ROW_STRIP):
    def step(i, carry):
        return fn(pl.ds(pl.multiple_of(i * rb, rb), rb), carry)
    return lax.fori_loop(0, tm // rb, step, init, unroll=2)


def _rms_fwd(x, gain, name):
    t, d = x.shape
    tm = _row_tile(t)

    def body(x_ref, g_ref, o_ref):
        xv = x_ref[...]
        r = lax.rsqrt(jnp.mean(xv * xv, axis=-1, keepdims=True) + EPS)
        o_ref[...] = (xv * r * g_ref[...]).astype(o_ref.dtype)

    return pl.pallas_call(
        body, name=name, grid=(t // tm,),
        in_specs=[pl.BlockSpec((tm, d), lambda i: (i, 0)), pl.BlockSpec((1, d), lambda i: (0, 0))],
        out_specs=pl.BlockSpec((tm, d), lambda i: (i, 0)),
        out_shape=jax.ShapeDtypeStruct((t, d), BF16), compiler_params=_cparams("parallel"),
    )(x, gain)


def _rms_bwd(x, gain, dn, dres, name):
    t, d = x.shape
    tm = _row_tile(t)
    has_res = dres is not None

    def body(*refs):
        if has_res:
            x_ref, g_ref, dn_ref, r_ref, dx_ref, dg_ref = refs
        else:
            x_ref, g_ref, dn_ref, dx_ref, dg_ref = refs

        @pl.when(pl.program_id(0) == 0)
        def _():
            dg_ref[...] = jnp.zeros_like(dg_ref)

        xv = x_ref[...]
        r = lax.rsqrt(jnp.mean(xv * xv, axis=-1, keepdims=True) + EPS)
        xh = xv * r
        dnv = dn_ref[...].astype(F32)
        dg_ref[...] += jnp.sum(dnv * xh, axis=0, keepdims=True)
        dxh = dnv * g_ref[...]
        dx = r * (dxh - xh * jnp.mean(dxh * xh, axis=-1, keepdims=True))
        if has_res:
            dx = dx + r_ref[...]
        dx_ref[...] = dx

    row = pl.BlockSpec((tm, d), lambda i: (i, 0))
    vec = pl.BlockSpec((1, d), lambda i: (0, 0))
    in_specs = [row, vec, row] + ([row] if has_res else [])
    args = (x, gain, dn) + ((dres,) if has_res else ())
    return pl.pallas_call(
        body, name=name, grid=(t // tm,), in_specs=in_specs, out_specs=[row, vec],
        out_shape=[jax.ShapeDtypeStruct((t, d), F32), jax.ShapeDtypeStruct((1, d), F32)],
        compiler_params=_cparams("arbitrary"),
    )(*args)


def _loss_head(h, gain, target, name):
    t, d = h.shape
    tm = _row_tile(t)

    def body(h_ref, g_ref, t_ref, dh_ref, dg_ref, l_ref):
        @pl.when(pl.program_id(0) == 0)
        def _():
            dg_ref[...] = jnp.zeros_like(dg_ref)
            l_ref[...] = jnp.zeros_like(l_ref)

        xv = h_ref[...]
        g = g_ref[...]
        r = lax.rsqrt(jnp.mean(xv * xv, axis=-1, keepdims=True) + EPS)
        xh = xv * r
        err = xh * g - t_ref[...]
        l_ref[...] += 0.5 * jnp.sum(jnp.mean(err * err, axis=-1, keepdims=True), axis=0, keepdims=True)
        dy = err * (1.0 / d)
        dg_ref[...] += jnp.sum(dy * xh, axis=0, keepdims=True)
        dxh = dy * g
        dh_ref[...] = r * (dxh - xh * jnp.mean(dxh * xh, axis=-1, keepdims=True))

    row = pl.BlockSpec((tm, d), lambda i: (i, 0))
    vec = pl.BlockSpec((1, d), lambda i: (0, 0))
    return pl.pallas_call(
        body, name=name, grid=(t // tm,), in_specs=[row, vec, row],
        out_specs=[row, vec, pl.BlockSpec((1, 128), lambda i: (0, 0))],
        out_shape=[jax.ShapeDtypeStruct((t, d), F32), jax.ShapeDtypeStruct((1, d), F32), jax.ShapeDtypeStruct((1, 128), F32)],
        compiler_params=_cparams("arbitrary"),
    )(h, gain, target)


FF_TILE = 1408


def _ffn_fwd(n, x, wg, wu, wd, name):
    t, d = x.shape
    tm, tn = _row_tile(t), FF_TILE
    nj = D_FF // tn

    def body(n_ref, x_ref, wg_ref, wu_ref, wd_ref, h_ref, g_ref, u_ref, acc_ref):
        j = pl.program_id(1)

        @pl.when(j == 0)
        def _():
            acc_ref[...] = jnp.zeros_like(acc_ref)

        nb = n_ref[...]
        g = lax.dot_general(nb, wg_ref[...], _DIMS["nt"], preferred_element_type=F32)
        u = lax.dot_general(nb, wu_ref[...], _DIMS["nt"], preferred_element_type=F32)
        g_ref[...] = g.astype(BF16)
        u_ref[...] = u.astype(BF16)
        a = g * _sigmoid(g) * u
        acc_ref[...] += jnp.dot(a.astype(BF16), wd_ref[...], preferred_element_type=F32)

        @pl.when(j == nj - 1)
        def _():
            h_ref[...] = x_ref[...] + 0.5 * acc_ref[...]

    row = pl.BlockSpec((tm, d), lambda i, j: (i, 0))
    act = pl.BlockSpec((tm, tn), lambda i, j: (i, j))
    return pl.pallas_call(
        body, name=name, grid=(t // tm, nj),
        in_specs=[row, row] + [pl.BlockSpec((tn, d), lambda i, j: (j, 0))] * 3,
        out_specs=[row, act, act],
        out_shape=[jax.ShapeDtypeStruct((t, d), F32), jax.ShapeDtypeStruct((t, D_FF), BF16), jax.ShapeDtypeStruct((t, D_FF), BF16)],
        scratch_shapes=[pltpu.VMEM((tm, d), F32)], compiler_params=_cparams("parallel", "arbitrary"),
    )(n, x, wg, wu, wd)


def _ffn_bwd_act(dh, g, u, wg, wu, wd, name):
    t, d = dh.shape
    tm, tn = _row_tile(t), FF_TILE
    nj = D_FF // tn

    def body(dh_ref, g_ref, u_ref, wg_ref, wu_ref, wd_ref, dn_ref, dg_ref, du_ref, a_ref, acc_ref):
        j = pl.program_id(1)

        @pl.when(j == 0)
        def _():
            acc_ref[...] = jnp.zeros_like(acc_ref)

        dhb = (0.5 * dh_ref[...]).astype(BF16)
        da = lax.dot_general(dhb, wd_ref[...], _DIMS["nt"], preferred_element_type=F32)
        gv = g_ref[...].astype(F32)
        uv = u_ref[...].astype(F32)
        sg = _sigmoid(gv)
        s = gv * sg
        dg = (da * uv * (sg * (1.0 + gv * (1.0 - sg)))).astype(BF16)
        du = (da * s).astype(BF16)
        dg_ref[...] = dg
        du_ref[...] = du
        a_ref[...] = (s * uv).astype(BF16)
        acc_ref[...] += (jnp.dot(dg, wg_ref[...], preferred_element_type=F32)
                         + jnp.dot(du, wu_ref[...], preferred_element_type=F32))

        @pl.when(j == nj - 1)
        def _():
            dn_ref[...] = acc_ref[...]

    row = pl.BlockSpec((tm, d), lambda i, j: (i, 0))
    act = pl.BlockSpec((tm, tn), lambda i, j: (i, j))
    return pl.pallas_call(
        body, name=name, grid=(t // tm, nj),
        in_specs=[row, act, act] + [pl.BlockSpec((tn, d), lambda i, j: (j, 0))] * 3,
        out_specs=[row, act, act, act],
        out_shape=[jax.ShapeDtypeStruct((t, d), F32)] + [jax.ShapeDtypeStruct((t, D_FF), BF16)] * 3,
        scratch_shapes=[pltpu.VMEM((tm, d), F32)], compiler_params=_cparams("parallel", "arbitrary"),
    )(dh, g, u, wg, wu, wd)


def _ffn_forward(x, gain, wg, wu, wd, tag):
    n = _rms_fwd(x, gain, f"{tag}_norm")
    h, g, u = _ffn_fwd(n, x, wg, wu, wd, f"{tag}_fwd")
    return h, (n, g, u)


def _ffn_backward(dh, x, gain, wg, wu, wd, saved, tag, push):
    n, g, u = saved
    dn, dg, du, a = _ffn_bwd_act(dh, g, u, wg, wu, wd, f"{tag}_bwd_act")
    kw = dict(mode="tn", out_dtype=BF16, tm=FF_TILE, tn=1024, tk=1024)
    d_wg = _matmul(dg, n, name=f"{tag}_dwg", **kw)
    d_wu = _matmul(du, n, name=f"{tag}_dwu", **kw)
    d_wd = _matmul(a, dh, scale=0.5, name=f"{tag}_dwd", **kw)
    token = push({f"{tag}_w_gate": d_wg, f"{tag}_w_up": d_wu, f"{tag}_w_down": d_wd})
    return _rms_bwd(x, gain + token, dn, dh, f"{tag}_norm_bwd")


CONV_COLS = 512
HALO = 8
CONV_STRIP = 32


def _conv_fwd(p, w, b, name):
    t = p.shape[0]
    tm = _row_tile(t)
    c0 = P_XBC // CONV_COLS

    def body(x_ref, halo_ref, w_ref, b_ref, o_ref, s_ref):
        i = pl.program_id(1)
        s_ref[0:HALO, :] = jnp.where(i > 0, halo_ref[...].astype(F32), 0.0)
        s_ref[HALO:HALO + tm, :] = x_ref[...].astype(F32)
        wv = w_ref[...]
        bv = b_ref[...]
        for r0 in range(0, tm, CONV_STRIP):
            acc = bv + wv[0:1, :] * s_ref[HALO - 3 + r0:HALO - 3 + r0 + CONV_STRIP, :]
            for k in range(1, SSD_CONV):
                acc = acc + wv[k:k + 1, :] * s_ref[HALO - 3 + k + r0:HALO - 3 + k + r0 + CONV_STRIP, :]
            o_ref[r0:r0 + CONV_STRIP, :] = (acc * _sigmoid(acc)).astype(o_ref.dtype)

    return pl.pallas_call(
        body, name=name, grid=(XBC // CONV_COLS, t // tm),
        in_specs=[pl.BlockSpec((tm, CONV_COLS), lambda j, i: (i, c0 + j)),
                  pl.BlockSpec((HALO, CONV_COLS), lambda j, i: (jnp.maximum(i * (tm // HALO) - 1, 0), c0 + j)),
                  pl.BlockSpec((SSD_CONV, CONV_COLS), lambda j, i: (0, j)), pl.BlockSpec((1, CONV_COLS), lambda j, i: (0, j))],
        out_specs=pl.BlockSpec((tm, CONV_COLS), lambda j, i: (i, j)),
        out_shape=jax.ShapeDtypeStruct((t, XBC), BF16),
        scratch_shapes=[pltpu.VMEM((tm + HALO, CONV_COLS), F32)], compiler_params=_cparams("parallel", "parallel"),
    )(p, p, w, b)


def _conv_bwd_act(p, dy, w, b, col0, name):
    t, cols = dy.shape
    tm = _row_tile(t)
    c0 = (P_XBC + col0) // CONV_COLS
    w0 = col0 // CONV_COLS

    def body(x_ref, halo_ref, dy_ref, w_ref, b_ref, da_ref, dw_ref, db_ref, s_ref):
        i = pl.program_id(1)

        @pl.when(i == 0)
        def _():
            dw_ref[...] = jnp.zeros_like(dw_ref)
            db_ref[...] = jnp.zeros_like(db_ref)

        s_ref[0:HALO, :] = jnp.where(i > 0, halo_ref[...].astype(F32), 0.0)
        s_ref[HALO:HALO + tm, :] = x_ref[...].astype(F32)
        wv = w_ref[...]
        bv = b_ref[...]
        fold = lambda v: jnp.sum(v.reshape(CONV_STRIP // 8, 8, CONV_COLS), axis=0)
        sums = [jnp.zeros((8, CONV_COLS), F32) for _ in range(SSD_CONV + 1)]
        for r0 in range(0, tm, CONV_STRIP):
            taps = [s_ref[HALO - 3 + k + r0:HALO - 3 + k + r0 + CONV_STRIP, :] for k in range(SSD_CONV)]
            acc = bv + wv[0:1, :] * taps[0]
            for k in range(1, SSD_CONV):
                acc = acc + wv[k:k + 1, :] * taps[k]
            sg = _sigmoid(acc)
            dacc = dy_ref[r0:r0 + CONV_STRIP, :].astype(F32) * (sg * (1.0 + acc * (1.0 - sg)))
            da_ref[r0:r0 + CONV_STRIP, :] = dacc.astype(BF16)
            for k in range(SSD_CONV):
                sums[k] = sums[k] + fold(dacc * taps[k])
            sums[SSD_CONV] = sums[SSD_CONV] + fold(dacc)
        for k in range(SSD_CONV):
            dw_ref[k:k + 1, :] += jnp.sum(sums[k], axis=0, keepdims=True)
        db_ref[...] += jnp.sum(sums[SSD_CONV], axis=0, keepdims=True)

    return pl.pallas_call(
        body, name=name, grid=(cols // CONV_COLS, t // tm),
        in_specs=[pl.BlockSpec((tm, CONV_COLS), lambda j, i: (i, c0 + j)),
                  pl.BlockSpec((HALO, CONV_COLS), lambda j, i: (jnp.maximum(i * (tm // HALO) - 1, 0), c0 + j)),
                  pl.BlockSpec((tm, CONV_COLS), lambda j, i: (i, j)),
                  pl.BlockSpec((SSD_CONV, CONV_COLS), lambda j, i: (0, w0 + j)), pl.BlockSpec((1, CONV_COLS), lambda j, i: (0, w0 + j))],
        out_specs=[pl.BlockSpec((tm, CONV_COLS), lambda j, i: (i, j)), pl.BlockSpec((SSD_CONV, CONV_COLS), lambda j, i: (0, j)),
                   pl.BlockSpec((1, CONV_COLS), lambda j, i: (0, j))],
        out_shape=[jax.ShapeDtypeStruct((t, cols), BF16), jax.ShapeDtypeStruct((SSD_CONV, cols), F32), jax.ShapeDtypeStruct((1, cols), F32)],
        scratch_shapes=[pltpu.VMEM((tm + HALO, CONV_COLS), F32)], compiler_params=_cparams("parallel", "arbitrary"),
    )(p, p, dy, w, b)


def _conv_bwd_dx(dacc, w, col0, name):
    t, cols = dacc.shape
    tm = _row_tile(t)
    nt = t // tm
    w0 = col0 // CONV_COLS

    def body(d_ref, halo_ref, w_ref, o_ref, s_ref):
        i = pl.program_id(1)
        s_ref[0:tm, :] = d_ref[...].astype(F32)
        s_ref[tm:tm + HALO, :] = jnp.where(i < nt - 1, halo_ref[...].astype(F32), 0.0)
        wv = w_ref[...]
        for r0 in range(0, tm, CONV_STRIP):
            acc = wv[3:4, :] * s_ref[r0:r0 + CONV_STRIP, :]
            for k in range(SSD_CONV - 1):
                acc = acc + wv[k:k + 1, :] * s_ref[3 - k + r0:3 - k + r0 + CONV_STRIP, :]
            o_ref[r0:r0 + CONV_STRIP, :] = acc.astype(o_ref.dtype)

    return pl.pallas_call(
        body, name=name, grid=(cols // CONV_COLS, nt),
        in_specs=[pl.BlockSpec((tm, CONV_COLS), lambda j, i: (i, j)),
                  pl.BlockSpec((HALO, CONV_COLS), lambda j, i: (jnp.minimum((i + 1) * (tm // HALO), t // HALO - 1), j)),
                  pl.BlockSpec((SSD_CONV, CONV_COLS), lambda j, i: (0, w0 + j))],
        out_specs=pl.BlockSpec((tm, CONV_COLS), lambda j, i: (i, j)),
        out_shape=jax.ShapeDtypeStruct((t, cols), BF16),
        scratch_shapes=[pltpu.VMEM((tm + HALO, CONV_COLS), F32)], compiler_params=_cparams("parallel", "parallel"),
    )(dacc, dacc, w)


GROUP_COLS = SSD_INNER // SSD_GROUPS
PAIRS = GROUP_COLS // 128
HEADS_PER_GROUP = SSD_HEADS // SSD_GROUPS


def _dt_fwd(dt_raw, bias, name):
    t, n = dt_raw.shape
    tm = _row_tile(t)

    def body(x_ref, b_ref, o_ref):
        v = x_ref[...] + b_ref[...]
        o_ref[...] = jnp.maximum(v, 0.0) + jnp.log1p(jnp.exp(-jnp.abs(v)))

    row = pl.BlockSpec((tm, n), lambda i: (i, 0))
    vec = pl.BlockSpec((1, n), lambda i: (0, 0))
    return pl.pallas_call(body, name=name, grid=(t // tm,), in_specs=[row, vec], out_specs=row,
                          out_shape=jax.ShapeDtypeStruct((t, n), F32), compiler_params=_cparams("parallel"))(dt_raw, bias)


def _dt_bwd(ddt, dt_raw, bias, name):
    t, n = dt_raw.shape
    tm = _row_tile(t)

    def body(d_ref, x_ref, b_ref, o_ref, db_ref):
        @pl.when(pl.program_id(0) == 0)
        def _():
            db_ref[...] = jnp.zeros_like(db_ref)

        dr = d_ref[...] * _sigmoid(x_ref[...] + b_ref[...])
        o_ref[...] = dr.astype(o_ref.dtype)
        db_ref[...] += jnp.sum(dr, axis=0, keepdims=True)

    row = pl.BlockSpec((tm, n), lambda i: (i, 0))
    vec = pl.BlockSpec((1, n), lambda i: (0, 0))
    return pl.pallas_call(body, name=name, grid=(t // tm,), in_specs=[row, row, vec], out_specs=[row, vec],
                          out_shape=[jax.ShapeDtypeStruct((t, n), BF16), jax.ShapeDtypeStruct((1, n), F32)],
                          compiler_params=_cparams("arbitrary"))(ddt, dt_raw, bias)


def _ssd_common(dtw_ref, dtt_ref, aw_ref, at_ref):
    l = CHUNK
    dt = dtw_ref[0]
    dtt = dtt_ref[0]
    a = -jnp.exp(aw_ref[0])
    at = -jnp.exp(at_ref[0])
    rowi = lax.broadcasted_iota(jnp.int32, (l, l), 0)
    coli = lax.broadcasted_iota(jnp.int32, (l, l), 1)
    tri = rowi >= coli
    lower = tri.astype(F32)
    upper = (rowi <= coli).astype(F32)
    acs = jnp.dot(lower, dt * a, precision=HI, preferred_element_type=F32)
    acst = jnp.dot(dtt * at, upper, precision=HI, preferred_element_type=F32)
    return dt, a, acs, acst, tri, upper


def _pair_bc(w, lo, p):
    return jnp.where(lo, w[:, 2 * p:2 * p + 1], w[:, 2 * p + 1:2 * p + 2])


def _ssd_specs(t):
    nc = t // CHUNK
    return nc, dict(
        xs=lambda cm: pl.BlockSpec((CHUNK, GROUP_COLS), lambda g, c: (cm(c), g)),
        bm=lambda cm: pl.BlockSpec((CHUNK, SSD_STATE), lambda g, c: (cm(c), SSD_INNER // SSD_STATE + g)),
        cmat=lambda cm: pl.BlockSpec((CHUNK, SSD_STATE), lambda g, c: (cm(c), SSD_INNER // SSD_STATE + SSD_GROUPS + g)),
        dtw=lambda cm: pl.BlockSpec((1, CHUNK, 128), lambda g, c: (g, cm(c), 0)),
        dtt=lambda cm: pl.BlockSpec((1, HEADS_PER_GROUP, CHUNK), lambda g, c: (g, 0, cm(c))),
        wide=lambda cm: pl.BlockSpec((1, 1, 128), lambda g, c: (g, 0, 0)),
        tall=lambda cm: pl.BlockSpec((1, HEADS_PER_GROUP, 1), lambda g, c: (g, 0, 0)),
        grp=lambda cm: pl.BlockSpec((CHUNK, GROUP_COLS), lambda g, c: (cm(c), g)),
        vec=lambda cm: pl.BlockSpec((1, GROUP_COLS), lambda g, c: (0, g)),
        state=lambda cm: pl.BlockSpec((1, 1, PAIRS, SSD_STATE, 128), lambda g, c: (g, cm(c), 0, 0, 0)),
    )


def _ssd_fwd(xc, p, hv, norm_g, name):
    t = xc.shape[0]
    nc, sp = _ssd_specs(t)
    ident = lambda c: c

    def body(xs_ref, b_ref, c_ref, dtw_ref, dtt_ref, aw_ref, at_ref, dk_ref, z_ref, ng_ref,
             y_ref, ys_ref, h_ref, st_ref):
        @pl.when(pl.program_id(1) == 0)
        def _():
            st_ref[...] = jnp.zeros_like(st_ref)

        dt, a, acs, acst, tri, _ = _ssd_common(dtw_ref, dtt_ref, aw_ref, at_ref)
        ecs = jnp.exp(acs)
        alast = acs[CHUNK - 1:CHUNK, :]
        bmat, cmat = b_ref[...], c_ref[...]
        cb = lax.dot_general(cmat, bmat, _DIMS["nt"], preferred_element_type=F32)
        lo = lax.broadcasted_iota(jnp.int32, (1, 128), 1) < 64
        dskip = dk_ref[0]
        for pi in range(PAIRS):
            x = xs_ref[:, pi * 128:(pi + 1) * 128].astype(F32)
            xdt = x * _pair_bc(dt, lo, pi)
            ydiag = jnp.zeros((CHUNK, 128), F32)
            for r, mask in ((2 * pi, lo), (2 * pi + 1, jnp.logical_not(lo))):
                lam = jnp.exp(jnp.where(tri, acs[:, r:r + 1] - acst[r:r + 1, :], -1e30))
                m = (cb * lam).astype(BF16)
                ydiag = ydiag + jnp.dot(m, jnp.where(mask, xdt, 0.0).astype(BF16), preferred_element_type=F32)
            ht = st_ref[pi]
            h_ref[0, 0, pi] = ht
            yoff = jnp.dot(cmat, ht.astype(BF16), preferred_element_type=F32) * _pair_bc(ecs, lo, pi)
            y_ref[:, pi * 128:(pi + 1) * 128] = (ydiag + yoff + _pair_bc(dskip, lo, pi) * x).astype(y_ref.dtype)
            alp = _pair_bc(alast, lo, pi)
            e = jnp.exp(alp - _pair_bc(acs, lo, pi))
            st = lax.dot_general(bmat, (xdt * e).astype(BF16), _DIMS["tn"], preferred_element_type=F32)
            st_ref[pi] = ht * jnp.exp(alp) + st
        zf = z_ref[...].astype(F32)
        yg = y_ref[...].astype(F32) * (zf * _sigmoid(zf))
        rstd = lax.rsqrt(jnp.mean(yg * yg, axis=-1, keepdims=True) + EPS)
        ys_ref[...] = (yg * rstd * ng_ref[...]).astype(ys_ref.dtype)

    ins = ["xs", "bm", "cmat", "dtw", "dtt", "wide", "tall", "wide", "grp", "vec"]
    return pl.pallas_call(
        body, name=name, grid=(SSD_GROUPS, nc),
        in_specs=[sp[k](ident) for k in ins],
        out_specs=[sp["grp"](ident), sp["grp"](ident), sp["state"](ident)],
        out_shape=[jax.ShapeDtypeStruct((t, SSD_INNER), BF16), jax.ShapeDtypeStruct((t, SSD_INNER), BF16),
                   jax.ShapeDtypeStruct((SSD_GROUPS, nc, PAIRS, SSD_STATE, 128), F32)],
        scratch_shapes=[pltpu.VMEM((PAIRS, SSD_STATE, 128), F32)], compiler_params=_cparams("parallel", "arbitrary"),
    )(xc, xc, xc, hv["dtw"], hv["dtt"], hv["alog_w"], hv["alog_t"], hv["dskip_w"], p, norm_g)


def _ssd_bwd(xc, p, hv, norm_g, y, dys, states, name):
    t = xc.shape[0]
    nc, sp = _ssd_specs(t)
    rev = lambda c: nc - 1 - c

    def body(xs_ref, b_ref, c_ref, dtw_ref, dtt_ref, aw_ref, at_ref, dk_ref, z_ref, ng_ref,
             y_ref, dys_ref, h_ref,
             dxs_ref, db_ref, dc_ref, dz_ref, ddt_ref, hsum_ref, dng_ref, dst_ref):
        @pl.when(pl.program_id(1) == 0)
        def _():
            dst_ref[...] = jnp.zeros_like(dst_ref)
            hsum_ref[...] = jnp.zeros_like(hsum_ref)
            dng_ref[...] = jnp.zeros_like(dng_ref)

        dt, a, acs, acst, tri, upper = _ssd_common(dtw_ref, dtt_ref, aw_ref, at_ref)
        ecs = jnp.exp(acs)
        alast = acs[CHUNK - 1:CHUNK, :]
        bmat, cmat = b_ref[...], c_ref[...]
        cb = lax.dot_general(cmat, bmat, _DIMS["nt"], preferred_element_type=F32)
        lane = lax.broadcasted_iota(jnp.int32, (1, 128), 1)
        lo = lane < 64
        dskip = dk_ref[0]

        zf = z_ref[...].astype(F32)
        sg = _sigmoid(zf)
        sz = zf * sg
        yv = y_ref[...].astype(F32)
        yg = yv * sz
        rstd = lax.rsqrt(jnp.mean(yg * yg, axis=-1, keepdims=True) + EPS)
        yhat = yg * rstd
        dysv = dys_ref[...].astype(F32)
        dng_ref[...] += jnp.sum(dysv * yhat, axis=0, keepdims=True)
        dyh = dysv * ng_ref[...]
        dyg = rstd * (dyh - yhat * jnp.mean(dyh * yhat, axis=-1, keepdims=True))
        dz_ref[...] = (dyg * yv * (sg * (1.0 + zf * (1.0 - sg)))).astype(dz_ref.dtype)
        dy_all = dyg * sz

        sel_r = lax.broadcasted_iota(jnp.int32, (128, 128), 0)
        sel_c = lax.broadcasted_iota(jnp.int32, (128, 128), 1)
        dal = jnp.zeros((CHUNK, 128), F32)
        ddtm = jnp.zeros((CHUNK, 128), F32)
        dalast = jnp.zeros((8, 128), F32)
        ddsk = jnp.zeros((8, 128), F32)
        dcb = jnp.zeros((CHUNK, CHUNK), F32)
        qcol = jnp.zeros((8, CHUNK), F32)
        sub8 = lax.broadcasted_iota(jnp.int32, (8, CHUNK), 0)
        dc_acc = jnp.zeros((CHUNK, SSD_STATE), F32)
        db_acc = jnp.zeros((CHUNK, SSD_STATE), F32)
        for pi in range(PAIRS):
            sel = (sel_c == 2 * pi + (sel_r >= 64).astype(jnp.int32)).astype(BF16)

            def hsum(v, sel=sel):
                return jnp.dot(v.astype(BF16), sel, preferred_element_type=F32)

            dyp = dy_all[:, pi * 128:(pi + 1) * 128]
            x = xs_ref[:, pi * 128:(pi + 1) * 128].astype(F32)
            dtp = _pair_bc(dt, lo, pi)
            xdt = x * dtp
            dxdt = jnp.zeros((CHUNK, 128), F32)
            for r, mask in ((2 * pi, lo), (2 * pi + 1, jnp.logical_not(lo))):
                lam = jnp.exp(jnp.where(tri, acs[:, r:r + 1] - acst[r:r + 1, :], -1e30))
                m32 = cb * lam
                m = m32.astype(BF16)
                dyr = jnp.where(mask, dyp, 0.0).astype(BF16)
                xr = jnp.where(mask, xdt, 0.0).astype(BF16)
                dm = lax.dot_general(dyr, xr, _DIMS["nt"], preferred_element_type=F32)
                dcb = dcb + dm * lam
                q = dm * m32
                dal = dal + jnp.sum(q, axis=1, keepdims=True) * (lane == r).astype(F32)
                qcol = qcol + jnp.where(sub8 == r, jnp.sum(q, axis=0, keepdims=True), 0.0)
                dxdt = dxdt + lax.dot_general(m, dyr, _DIMS["tn"], preferred_element_type=F32)
            ht = h_ref[0, 0, pi]
            htb = ht.astype(BF16)
            ecp = _pair_bc(ecs, lo, pi)
            yoff = jnp.dot(cmat, htb, preferred_element_type=F32) * ecp
            dg = (dyp * ecp).astype(BF16)
            dc_acc = dc_acc + lax.dot_general(dg, htb, _DIMS["nt"], preferred_element_type=F32)
            dht = lax.dot_general(cmat, dg, _DIMS["tn"], preferred_element_type=F32)
            dal = dal + hsum(dyp * yoff)
            dhn = dst_ref[pi]
            dhnb = dhn.astype(BF16)
            alp = _pair_bc(alast, lo, pi)
            e = jnp.exp(alp - _pair_bc(acs, lo, pi))
            xe = xdt * e
            db_acc = db_acc + lax.dot_general(xe.astype(BF16), dhnb, _DIMS["nt"], preferred_element_type=F32)
            dxe = jnp.dot(bmat, dhnb, preferred_element_type=F32)
            dxdt = dxdt + dxe * e
            tt = hsum(dxe * xe)
            dal = dal - tt
            dec = jnp.exp(alp)
            dalast = dalast + jnp.sum(tt, axis=0, keepdims=True) + hsum(
                jnp.broadcast_to(jnp.sum(dhn * ht, axis=0, keepdims=True) * dec, (8, 128)))
            dst_ref[pi] = dht + dhn * dec
            dxs_ref[:, pi * 128:(pi + 1) * 128] = (_pair_bc(dskip, lo, pi) * dyp + dxdt * dtp).astype(dxs_ref.dtype)
            ddtm = ddtm + hsum(dxdt * x)
            ddsk = ddsk + hsum(jnp.broadcast_to(jnp.sum(dyp * x, axis=0, keepdims=True), (8, 128)))
        rowi = lax.broadcasted_iota(jnp.int32, (CHUNK, 128), 0)
        qcol_w = lax.dot_general(jnp.concatenate([qcol, jnp.zeros((CHUNK - 8, CHUNK), F32)], axis=0), (sel_r == sel_c).astype(F32),
                                 _DIMS["tn"], precision=HI, preferred_element_type=F32)
        dal = dal - qcol_w + jnp.where(rowi == CHUNK - 1, dalast[0:1, :], 0.0)
        dda = jnp.dot(upper, dal, precision=HI, preferred_element_type=F32)
        ddt_ref[0] = ddtm + dda * a
        hsum_ref[0, 1:2, :] += jnp.sum(dda * dt, axis=0, keepdims=True) * a
        hsum_ref[0, 2:3, :] += ddsk[0:1, :]
        dcbb = dcb.astype(BF16)
        dc_ref[...] = (jnp.dot(dcbb, bmat, preferred_element_type=F32) + dc_acc).astype(dc_ref.dtype)
        db_ref[...] = (lax.dot_general(dcbb, cmat, _DIMS["tn"], preferred_element_type=F32) + db_acc).astype(db_ref.dtype)

    ins = ["xs", "bm", "cmat", "dtw", "dtt", "wide", "tall", "wide", "grp", "vec", "grp", "grp", "state"]
    col = lambda: pl.BlockSpec((CHUNK, SSD_STATE), lambda g, c: (rev(c), g))
    return pl.pallas_call(
        body, name=name, grid=(SSD_GROUPS, nc),
        in_specs=[sp[k](rev) for k in ins],
        out_specs=[sp["grp"](rev), col(), col(), sp["grp"](rev), sp["dtw"](rev),
                   pl.BlockSpec((1, 8, 128), lambda g, c: (g, 0, 0)), sp["vec"](rev)],
        out_shape=[jax.ShapeDtypeStruct((t, SSD_INNER), BF16), jax.ShapeDtypeStruct((t, SSD_GROUPS * SSD_STATE), BF16),
                   jax.ShapeDtypeStruct((t, SSD_GROUPS * SSD_STATE), BF16), jax.ShapeDtypeStruct((t, SSD_INNER), BF16),
                   jax.ShapeDtypeStruct((SSD_GROUPS, t, 128), F32), jax.ShapeDtypeStruct((SSD_GROUPS, 8, 128), F32),
                   jax.ShapeDtypeStruct((1, SSD_INNER), F32)],
        scratch_shapes=[pltpu.VMEM((PAIRS, SSD_STATE, 128), F32)], compiler_params=_cparams("parallel", "arbitrary"),
    )(xc, xc, xc, hv["dtw"], hv["dtt"], hv["alog_w"], hv["alog_t"], hv["dskip_w"], p, norm_g, y, dys, states)


def _wide(v):
    return jnp.pad(v.reshape(SSD_GROUPS, 1, HEADS_PER_GROUP), ((0, 0), (0, 0), (0, 128 - HEADS_PER_GROUP)))


def _head_inputs(dt, a_log, d_skip):
    t = dt.shape[0]
    g = dt[:, :SSD_HEADS].reshape(t, SSD_GROUPS, HEADS_PER_GROUP)
    return dict(
        dtw=jnp.pad(jnp.transpose(g, (1, 0, 2)), ((0, 0), (0, 0), (0, 128 - HEADS_PER_GROUP))),
        dtt=jnp.transpose(g, (1, 2, 0)),
        alog_w=_wide(a_log), alog_t=a_log.reshape(SSD_GROUPS, HEADS_PER_GROUP, 1),
        dskip_w=_wide(d_skip),
    )


def _gelu(x):
    return 0.5 * x * (1.0 + lax.erf(x * (1.0 / math.sqrt(2.0))))


def _gelu_grad(x):
    return 0.5 * (1.0 + lax.erf(x * (1.0 / math.sqrt(2.0)))) + x * jnp.exp(-0.5 * x * x) * (1.0 / math.sqrt(2.0 * math.pi))


def _tril_mask():
    r = lax.broadcasted_iota(jnp.int32, (CHUNK, CHUNK), 0)
    c = lax.broadcasted_iota(jnp.int32, (CHUNK, CHUNK), 1)
    return r >= c


def _gmlp_fwd(p, v_gain, w_s, b_col, name):
    t = p.shape[0]
    tm = _row_tile(t)
    u0 = P_UV // GMLP_W

    def body(u_ref, v_ref, gn_ref, ws_ref, bs_ref, o_ref):
        v = _gelu(v_ref[...].astype(F32))
        v = (v * lax.rsqrt(jnp.mean(v * v, axis=-1, keepdims=True) + EPS) * gn_ref[...]).astype(BF16)
        tril = _tril_mask()
        wm = [jnp.where(tril, ws_ref[g], 0.0).astype(BF16) for g in range(GMLP_GROUPS)]
        for k in range(tm // CHUNK):
            rows = slice(k * CHUNK, (k + 1) * CHUNK)
            for g in range(GMLP_GROUPS):
                cols = slice(g * 128, (g + 1) * 128)
                mixed = jnp.dot(wm[g], v[rows, cols], preferred_element_type=F32) + bs_ref[g]
                o_ref[rows, cols] = (_gelu(u_ref[rows, cols].astype(F32)) * mixed).astype(o_ref.dtype)

    return pl.pallas_call(
        body, name=name, grid=(t // tm,),
        in_specs=[pl.BlockSpec((tm, GMLP_W), lambda i: (i, u0)), pl.BlockSpec((tm, GMLP_W), lambda i: (i, u0 + 1)),
                  pl.BlockSpec((1, GMLP_W), lambda i: (0, 0)), pl.BlockSpec((GMLP_GROUPS, CHUNK, CHUNK), lambda i: (0, 0, 0)),
                  pl.BlockSpec((GMLP_GROUPS, CHUNK, 1), lambda i: (0, 0, 0))],
        out_specs=pl.BlockSpec((tm, GMLP_W), lambda i: (i, 0)),
        out_shape=jax.ShapeDtypeStruct((t, GMLP_W), BF16), compiler_params=_cparams("parallel"),
    )(p, p, v_gain, w_s, b_col)


def _gmlp_bwd(p, dy, v_gain, w_s, b_col, name):
    t = p.shape[0]
    tm = _row_tile(t)
    u0 = P_UV // GMLP_W

    def body(u_ref, v_ref, dy_ref, gn_ref, ws_ref, bs_ref, duv_ref, dws_ref, dbs_ref, dgn_ref, dvn_ref):
        @pl.when(pl.program_id(0) == 0)
        def _():
            dws_ref[...] = jnp.zeros_like(dws_ref)
            dbs_ref[...] = jnp.zeros_like(dbs_ref)
            dgn_ref[...] = jnp.zeros_like(dgn_ref)

        vraw = v_ref[...].astype(F32)
        va = _gelu(vraw)
        rstd = lax.rsqrt(jnp.mean(va * va, axis=-1, keepdims=True) + EPS)
        vhat = va * rstd
        gain = gn_ref[...]
        vn = (vhat * gain).astype(BF16)
        tril = _tril_mask()
        wm = [jnp.where(tril, ws_ref[g], 0.0).astype(BF16) for g in range(GMLP_GROUPS)]
        for k in range(tm // CHUNK):
            rows = slice(k * CHUNK, (k + 1) * CHUNK)
            for g in range(GMLP_GROUPS):
                cols = slice(g * 128, (g + 1) * 128)
                uraw = u_ref[rows, cols].astype(F32)
                vb = vn[rows, cols]
                mixed = jnp.dot(wm[g], vb, preferred_element_type=F32) + bs_ref[g]
                dyb = dy_ref[rows, cols].astype(F32)
                duv_ref[rows, cols] = (dyb * mixed * _gelu_grad(uraw)).astype(duv_ref.dtype)
                dmix = dyb * _gelu(uraw)
                dmb = dmix.astype(BF16)
                dws_ref[g] += jnp.where(tril, lax.dot_general(dmb, vb, _DIMS["nt"], preferred_element_type=F32), 0.0)
                dbs_ref[g] += jnp.sum(dmix, axis=1, keepdims=True)
                dvn_ref[rows, cols] = lax.dot_general(wm[g], dmb, _DIMS["tn"], preferred_element_type=F32)
        dvn = dvn_ref[...]
        dgn_ref[...] += jnp.sum(dvn * vhat, axis=0, keepdims=True)
        dvh = dvn * gain
        dva = rstd * (dvh - vhat * jnp.mean(dvh * vhat, axis=-1, keepdims=True))
        duv_ref[:, GMLP_W:2 * GMLP_W] = (dva * _gelu_grad(vraw)).astype(duv_ref.dtype)

    return pl.pallas_call(
        body, name=name, grid=(t // tm,),
        in_specs=[pl.BlockSpec((tm, GMLP_W), lambda i: (i, u0)), pl.BlockSpec((tm, GMLP_W), lambda i: (i, u0 + 1)),
                  pl.BlockSpec((tm, GMLP_W), lambda i: (i, 0)),
                  pl.BlockSpec((1, GMLP_W), lambda i: (0, 0)), pl.BlockSpec((GMLP_GROUPS, CHUNK, CHUNK), lambda i: (0, 0, 0)),
                  pl.BlockSpec((GMLP_GROUPS, CHUNK, 1), lambda i: (0, 0, 0))],
        out_specs=[pl.BlockSpec((tm, 2 * GMLP_W), lambda i: (i, 0)), pl.BlockSpec((GMLP_GROUPS, CHUNK, CHUNK), lambda i: (0, 0, 0)),
                   pl.BlockSpec((GMLP_GROUPS, CHUNK, 1), lambda i: (0, 0, 0)), pl.BlockSpec((1, GMLP_W), lambda i: (0, 0))],
        out_shape=[jax.ShapeDtypeStruct((t, 2 * GMLP_W), BF16), jax.ShapeDtypeStruct((GMLP_GROUPS, CHUNK, CHUNK), F32),
                   jax.ShapeDtypeStruct((GMLP_GROUPS, CHUNK, 1), F32), jax.ShapeDtypeStruct((1, GMLP_W), F32)],
        scratch_shapes=[pltpu.VMEM((tm, GMLP_W), F32)], compiler_params=_cparams("arbitrary"),
    )(p, p, dy, v_gain, w_s, b_col)


def _head_masks():
    lane = lax.broadcasted_iota(jnp.int32, (1, MEM_W), 1)
    return [(lane >= h * 64) & (lane < (h + 1) * 64) for h in range(MEM_HEADS)]


def _mem_fwd(p, kv, name):
    t = p.shape[0]
    tm = _row_tile(t)
    q0 = P_Q // MEM_W

    def body(q_ref, kv_ref, o_ref):
        q = q_ref[...]
        k = kv_ref[:, 0:MEM_W].astype(BF16)
        v = kv_ref[:, MEM_W:2 * MEM_W].astype(BF16)
        out = jnp.zeros((tm, MEM_W), F32)
        for mask in _head_masks():
            s = lax.dot_general(jnp.where(mask, q, 0), k, _DIMS["nt"], preferred_element_type=F32) * 0.125
            e = jnp.exp(s - jnp.max(s, axis=-1, keepdims=True))
            pr = (e / jnp.sum(e, axis=-1, keepdims=True)).astype(BF16)
            out = out + jnp.where(mask, jnp.dot(pr, v, preferred_element_type=F32), 0.0)
        o_ref[...] = out.astype(o_ref.dtype)

    return pl.pallas_call(
        body, name=name, grid=(t // tm,),
        in_specs=[pl.BlockSpec((tm, MEM_W), lambda i: (i, q0)), pl.BlockSpec((MEM_LEN, 2 * MEM_W), lambda i: (0, 0))],
        out_specs=pl.BlockSpec((tm, MEM_W), lambda i: (i, 0)),
        out_shape=jax.ShapeDtypeStruct((t, MEM_W), BF16), compiler_params=_cparams("parallel"),
    )(p, kv)


def _mem_bwd(p, kv, dy, name):
    t = p.shape[0]
    tm = _row_tile(t)
    q0 = P_Q // MEM_W

    def body(q_ref, kv_ref, dy_ref, dq_ref, dkv_ref):
        @pl.when(pl.program_id(0) == 0)
        def _():
            dkv_ref[...] = jnp.zeros_like(dkv_ref)

        q = q_ref[...]
        dy = dy_ref[...]
        k = kv_ref[:, 0:MEM_W].astype(BF16)
        v = kv_ref[:, MEM_W:2 * MEM_W].astype(BF16)
        dq = jnp.zeros((tm, MEM_W), F32)
        dk = jnp.zeros((MEM_LEN, MEM_W), F32)
        dv = jnp.zeros((MEM_LEN, MEM_W), F32)
        for mask in _head_masks():
            qh = jnp.where(mask, q, 0)
            dyh = jnp.where(mask, dy, 0)
            s = lax.dot_general(qh, k, _DIMS["nt"], preferred_element_type=F32) * 0.125
            e = jnp.exp(s - jnp.max(s, axis=-1, keepdims=True))
            pr = e / jnp.sum(e, axis=-1, keepdims=True)
            prb = pr.astype(BF16)
            dp = lax.dot_general(dyh, v, _DIMS["nt"], preferred_element_type=F32)
            ds = (pr * (dp - jnp.sum(dp * pr, axis=-1, keepdims=True)) * 0.125).astype(BF16)
            dq = dq + jnp.where(mask, jnp.dot(ds, k, preferred_element_type=F32), 0.0)
            dk = dk + lax.dot_general(ds, qh, _DIMS["tn"], preferred_element_type=F32)
            dv = dv + lax.dot_general(prb, dyh, _DIMS["tn"], preferred_element_type=F32)
        dq_ref[...] = dq.astype(dq_ref.dtype)
        dkv_ref[:, 0:MEM_W] += dk
        dkv_ref[:, MEM_W:2 * MEM_W] += dv

    return pl.pallas_call(
        body, name=name, grid=(t // tm,),
        in_specs=[pl.BlockSpec((tm, MEM_W), lambda i: (i, q0)), pl.BlockSpec((MEM_LEN, 2 * MEM_W), lambda i: (0, 0)),
                  pl.BlockSpec((tm, MEM_W), lambda i: (i, 0))],
        out_specs=[pl.BlockSpec((tm, MEM_W), lambda i: (i, 0)), pl.BlockSpec((MEM_LEN, 2 * MEM_W), lambda i: (0, 0))],
        out_shape=[jax.ShapeDtypeStruct((t, MEM_W), BF16), jax.ShapeDtypeStruct((MEM_LEN, 2 * MEM_W), F32)],
        compiler_params=_cparams("arbitrary"),
    )(p, kv, dy)


def _merge_fwd(p, b_ssd, b_gmlp, b_mem, name):
    t = p.shape[0]
    tm = _row_tile(t)
    g0 = P_GL // D_MODEL

    def body(g1, g2, g3, b1, b2, b3, o_ref):
        def strip(rows, carry):
            acc = _sigmoid(g1[rows, :].astype(F32)) * b1[rows, :].astype(F32)
            acc = acc + _sigmoid(g2[rows, :].astype(F32)) * b2[rows, :].astype(F32)
            acc = acc + _sigmoid(g3[rows, :].astype(F32)) * b3[rows, :].astype(F32)
            o_ref[rows, :] = acc.astype(o_ref.dtype)
            return carry

        _strips(tm, strip, 0)

    row = pl.BlockSpec((tm, D_MODEL), lambda i: (i, 0))
    return pl.pallas_call(
        body, name=name, grid=(t // tm,),
        in_specs=[pl.BlockSpec((tm, D_MODEL), lambda i, k=k: (i, g0 + k)) for k in range(3)] + [row] * 3,
        out_specs=row, out_shape=jax.ShapeDtypeStruct((t, D_MODEL), BF16), compiler_params=_cparams("parallel"),
    )(p, p, p, b_ssd, b_gmlp, b_mem)


def _merge_bwd(p, dm, b_ssd, b_gmlp, b_mem, name):
    t = p.shape[0]
    tm = _row_tile(t)
    g0 = P_GL // D_MODEL

    def body(g1, g2, g3, dm_ref, b1, b2, b3, d1, d2, d3, dgl_ref):
        def strip(rows, carry):
            dmv = dm_ref[rows, :].astype(F32)
            for k, (g_ref, b_ref, d_ref) in enumerate(((g1, b1, d1), (g2, b2, d2), (g3, b3, d3))):
                sg = _sigmoid(g_ref[rows, :].astype(F32))
                d_ref[rows, :] = (dmv * sg).astype(d_ref.dtype)
                dgl_ref[rows, k * D_MODEL:(k + 1) * D_MODEL] = (dmv * b_ref[rows, :].astype(F32) * sg * (1.0 - sg)).astype(dgl_ref.dtype)
            return carry

        _strips(tm, strip, 0)

    row = pl.BlockSpec((tm, D_MODEL), lambda i: (i, 0))
    return pl.pallas_call(
        body, name=name, grid=(t // tm,),
        in_specs=[pl.BlockSpec((tm, D_MODEL), lambda i, k=k: (i, g0 + k)) for k in range(3)] + [row] * 4,
        out_specs=[row, row, row, pl.BlockSpec((tm, 3 * D_MODEL), lambda i: (i, 0))],
        out_shape=[jax.ShapeDtypeStruct((t, D_MODEL), BF16)] * 3 + [jax.ShapeDtypeStruct((t, 3 * D_MODEL), BF16)],
        compiler_params=_cparams("parallel"),
    )(p, p, p, dm, b_ssd, b_gmlp, b_mem)


def _local_step(x, mem, target, w, rest_weights, push):
    t = x.shape[0]
    mm = functools.partial(_matmul, tk=1024)

    h1, ffn1_saved = _ffn_forward(x, w["ffn1_norm"], w["ffn1_w_gate"], w["ffn1_w_up"], w["ffn1_w_down"], "ffn1")
    w = {**w, **rest_weights(h1)}
    n2 = _rms_fwd(h1, w["mix_norm"], "mix_norm")
    p = mm(n2, w["w_in_p"], mode="nt", out_dtype=BF16, tm=512, tn=1536, name="in_proj")
    dt_raw = mm(n2, w["w_dt"], mode="nt", out_dtype=F32, tm=512, tn=128, name="dt_proj")
    dt_bias = jnp.pad(w["ssd_dt_bias"], (0, 128 - SSD_HEADS)).reshape(1, 128)
    hv = _head_inputs(_dt_fwd(dt_raw, dt_bias, "dt_fwd"), w["ssd_a_log"], w["ssd_d"])
    xc = _conv_fwd(p, w["ssd_conv_w"], w["ssd_conv_b"], "conv_fwd")
    y_ssd_raw, y_ssd, states = _ssd_fwd(xc, p, hv, w["ssd_norm"], "ssd_fwd")
    b_col = w["gmlp_b_s"].reshape(GMLP_GROUPS, CHUNK, 1)
    y_gmlp = _gmlp_fwd(p, w["gmlp_v_norm"], w["gmlp_w_s"], b_col, "gmlp_fwd")
    mem_n = _rms_fwd(mem, w["mem_norm"], "mem_norm")
    kv = mm(mem_n, w["w_mem_kv"], mode="nn", out_dtype=F32, tm=256, tn=512, name="mem_kv")
    y_mem = _mem_fwd(p, kv, "mem_fwd")
    b_ssd = mm(y_ssd, w["w_branch_ssd"], mode="nn", out_dtype=BF16, tm=512, tn=1024, name="branch_ssd")
    b_gmlp = mm(y_gmlp, w["w_branch_gmlp"], mode="nn", out_dtype=BF16, tm=512, tn=1024, name="branch_gmlp")
    b_mem = mm(y_mem, w["w_branch_mem"], mode="nt", out_dtype=BF16, tm=512, tn=1024, tk=MEM_W, name="branch_mem")
    merged = _merge_fwd(p, b_ssd, b_gmlp, b_mem, "merge_fwd")
    h2 = mm(merged, w["w_out"], mode="nn", out_dtype=F32, tm=512, tn=1024, addend=h1, name="out_proj")
    h3, ffn2_saved = _ffn_forward(h2, w["ffn2_norm"], w["ffn2_w_gate"], w["ffn2_w_up"], w["ffn2_w_down"], "ffn2")
    dh3, d_final, loss = _loss_head(h3, w["final_norm"], target, "loss_head")

    g = {"final_norm": d_final}
    big = {}
    dh2, g["ffn2_norm"] = _ffn_backward(dh3, h2, w["ffn2_norm"], w["ffn2_w_gate"], w["ffn2_w_up"], w["ffn2_w_down"], ffn2_saved,
                                        "ffn2", functools.partial(push, 0))
    dmerged = mm(dh2, w["w_out"], mode="nt", out_dtype=BF16, tm=512, tn=1024, name="out_proj_dx")
    big["w_out"] = mm(merged, dh2, mode="tn", out_dtype=BF16, tm=1024, tn=1024, name="out_proj_dw")
    db_ssd, db_gmlp, db_mem, dgl = _merge_bwd(p, dmerged, b_ssd, b_gmlp, b_mem, "merge_bwd")
    dy_ssd = mm(db_ssd, w["w_branch_ssd"], mode="nt", out_dtype=BF16, tm=512, tn=2048, name="branch_ssd_dx")
    dy_gmlp = mm(db_gmlp, w["w_branch_gmlp"], mode="nt", out_dtype=BF16, tm=512, tn=1024, name="branch_gmlp_dx")
    dy_mem = mm(db_mem, w["w_branch_mem"], mode="nn", out_dtype=BF16, tm=512, tn=256, name="branch_mem_dx")
    big["w_branch_ssd"] = mm(y_ssd, db_ssd, mode="tn", out_dtype=BF16, tm=1024, tn=1024, name="branch_ssd_dw")
    big["w_branch_gmlp"] = mm(y_gmlp, db_gmlp, mode="tn", out_dtype=BF16, tm=1024, tn=1024, name="branch_gmlp_dw")
    big["w_branch_mem"] = mm(db_mem, y_mem, mode="tn", out_dtype=BF16, tm=1024, tn=256, name="branch_mem_dw")
    dq, dkv = _mem_bwd(p, kv, dy_mem, "mem_bwd")
    big["w_mem_kv"] = mm(mem_n, dkv, mode="tn", out_dtype=BF16, tm=1024, tn=512, tk=256, name="mem_kv_dw")
    dmem_n = mm(dkv, w["w_mem_kv"], mode="nt", out_dtype=F32, tm=256, tn=1024, tk=512, name="mem_kv_dx")
    _, g["mem_norm"] = _rms_bwd(mem, w["mem_norm"], dmem_n, None, "mem_norm_bwd")
    duv, d_ws, d_bs, g["gmlp_v_norm"] = _gmlp_bwd(p, dy_gmlp, w["gmlp_v_norm"], w["gmlp_w_s"], b_col, "gmlp_bwd")
    g["gmlp_w_s"] = d_ws
    g["gmlp_b_s"] = d_bs.reshape(GMLP_GROUPS, CHUNK)
    dxs, d_bm, d_cm, dz, ddt_w, hsums, g["ssd_norm"] = _ssd_bwd(xc, p, hv, w["ssd_norm"], y_ssd_raw, dy_ssd, states, "ssd_bwd")
    heads = hsums[:, :, :HEADS_PER_GROUP]
    g["ssd_a_log"] = heads[:, 1, :].reshape(1, SSD_HEADS)
    g["ssd_d"] = heads[:, 2, :].reshape(1, SSD_HEADS)
    ddt = jnp.transpose(ddt_w[:, :, :HEADS_PER_GROUP], (1, 0, 2)).reshape(t, SSD_HEADS)
    ddt, d_bias = _dt_bwd(jnp.pad(ddt, ((0, 0), (0, 128 - SSD_HEADS))), dt_raw, dt_bias, "dt_bwd")
    g["ssd_dt_bias"] = d_bias[:, :SSD_HEADS]
    parts, dws, dbs = [], [], []
    for dyc, col0, tag in ((dxs, 0, "x"), (d_bm, SSD_INNER, "b"), (d_cm, SSD_INNER + SSD_GROUPS * SSD_STATE, "c")):
        dacc, dw_c, db_c = _conv_bwd_act(p, dyc, w["ssd_conv_w"], w["ssd_conv_b"], col0, f"conv_bwd_act_{tag}")
        parts.append(_conv_bwd_dx(dacc, w["ssd_conv_w"], col0, f"conv_bwd_dx_{tag}"))
        dws.append(dw_c)
        dbs.append(db_c)
    g["ssd_conv_w"] = jnp.concatenate(dws, axis=1)
    g["ssd_conv_b"] = jnp.concatenate(dbs, axis=1)
    dp = jnp.concatenate([dz, duv] + parts + [dgl, dq, jnp.zeros((t, P_W - P_USED), BF16)], axis=1)
    d_win_p = mm(dp, n2, mode="tn", out_dtype=BF16, tm=1536, tn=1024, name="in_proj_dw")
    d_wdt = mm(ddt, n2, mode="tn", out_dtype=BF16, tm=128, tn=1024, name="dt_proj_dw")
    sl = lambda a, o, n: a[o:o + n]
    big["w_in"] = jnp.concatenate([sl(d_win_p, P_Z, 2048), sl(d_win_p, P_XBC, XBC), d_wdt[:SSD_HEADS], sl(d_win_p, P_UV, 2048),
                                   sl(d_win_p, P_Q, MEM_W), sl(d_win_p, P_GL, 3 * D_MODEL)], axis=0)
    token = push(1, big)
    dn2 = mm(dp, w["w_in_p"], mode="nn", out_dtype=F32, tm=512, tn=1024, tk=1536, name="in_proj_dx")
    dn2 = _matmul(ddt, w["w_dt"], mode="nn", out_dtype=F32, tm=512, tn=1024, tk=128, addend=dn2, name="dt_proj_dx")
    dh1, g["mix_norm"] = _rms_bwd(h1, w["mix_norm"] + token, dn2, dh2, "mix_norm_bwd")
    dx, g["ffn1_norm"] = _ffn_backward(dh1, x, w["ffn1_norm"], w["ffn1_w_gate"], w["ffn1_w_up"], w["ffn1_w_down"], ffn1_saved,
                                       "ffn1", functools.partial(push, 2))
    return loss, dx, g


def _split_w_in(w_in_t):
    sl = lambda o, n: w_in_t[o:o + n]
    w_p = jnp.concatenate([sl(IN_Z, 2048), sl(IN_UV, 2048), sl(IN_XBC, XBC), sl(IN_GL, 3 * D_MODEL), sl(IN_Q, MEM_W),
                           jnp.zeros((P_W - P_USED, D_MODEL), w_in_t.dtype)], axis=0)
    w_dt = jnp.pad(sl(IN_DT, SSD_HEADS), ((0, 128 - SSD_HEADS), (0, 0)))
    return w_p, w_dt


def _pick_tile(rows, cap=512):
    best = None
    for tile in range(8, min(rows, cap) + 1, 8):
        if rows % tile == 0:
            best = tile
    return best if best is not None else rows


def _adamw(w, g, m, v, name):
    rows, lanes = w.shape
    tile = _pick_tile(rows, cap=max(8, (512 * 1024 // lanes) // 8 * 8))
    c1 = 1.0 / (1.0 - ADAM_B1 ** ADAM_STEP)
    c2 = 1.0 / (1.0 - ADAM_B2 ** ADAM_STEP)

    def body(w_ref, g_ref, m_ref, v_ref, d_ref, nm_ref, nv_ref):
        gv = g_ref[...]
        nm = ADAM_B1 * m_ref[...] + (1.0 - ADAM_B1) * gv
        nv = ADAM_B2 * v_ref[...] + (1.0 - ADAM_B2) * (gv * gv)
        nm_ref[...] = nm
        nv_ref[...] = nv
        d_ref[...] = -ADAM_LR * ((nm * c1) / (jnp.sqrt(nv * c2) + ADAM_EPS) + ADAM_WD * w_ref[...])

    blk = pl.BlockSpec((tile, lanes), lambda i: (i, 0))
    return pl.pallas_call(
        body, name=name, grid=(rows // tile,), in_specs=[blk] * 4, out_specs=[blk] * 3,
        out_shape=[jax.ShapeDtypeStruct((rows, lanes), F32)] * 3, compiler_params=_cparams("parallel"),
    )(w, g, m, v)


HBM = pl.BlockSpec(memory_space=pltpu.HBM)


def _place():
    x, y, c = lax.axis_index("x"), lax.axis_index("y"), lax.axis_index("c")
    chips = [(1 - x, y), (x, 1 - y), (1 - x, 1 - y)]
    return x, y, c, chips


def _gather_weights(slots):
    _, _, rh, lanes = slots.shape

    def body(in_ref, out_ref, send_sems, recv_sems):
        del in_ref
        x, y, c, chips = _place()
        me, sibling = (x, y, c), (x, y, 1 - c)

        def copy(k, src, dst, to):
            return pltpu.make_async_remote_copy(src_ref=src, dst_ref=dst, send_sem=send_sems.at[k], recv_sem=recv_sems.at[k],
                                                device_id=to, device_id_type=MESH)

        own = out_ref.at[2 * x + y, c]
        first = [copy(j, own, own, (*chip, c)) for j, chip in enumerate(chips)]
        for cp in first:
            cp.start()
        passed = []
        for j, (cx, cy) in enumerate(chips):
            landed = out_ref.at[2 * cx + cy, c]
            copy(j, landed, landed, me).wait_recv()
            fwd = copy(3 + j, landed, landed, sibling)
            fwd.start()
            passed.append(fwd)
        for j, (cx, cy) in enumerate(chips):
            other = out_ref.at[2 * cx + cy, 1 - c]
            copy(3 + j, other, other, me).wait_recv()
        for cp in first + passed:
            cp.wait_send()

    return pl.pallas_call(
        body, name="gather_weights", out_shape=jax.ShapeDtypeStruct(slots.shape, slots.dtype),
        in_specs=[HBM], out_specs=HBM, input_output_aliases={0: 0},
        scratch_shapes=[pltpu.SemaphoreType.DMA((6,)), pltpu.SemaphoreType.DMA((6,))],
    )(slots)


SEM = pl.BlockSpec(memory_space=pltpu.SEMAPHORE)
EFFECT = pltpu.SideEffectType.DATAFLOW_SIDE_EFFECTING
N_PEER = 3


def _sem_outs():
    return tuple(pltpu.SemaphoreType.DMA(()) for _ in range(2 * N_PEER))


def _gather_start(slots):
    def body(in_ref, *refs):
        del in_ref
        sems, thru, token = refs[:2 * N_PEER], refs[2 * N_PEER], refs[2 * N_PEER + 1]
        x, y, c, chips = _place()
        own = thru.at[2 * x + y, c]
        for j, chip in enumerate(chips):
            pltpu.make_async_remote_copy(src_ref=own, dst_ref=own, send_sem=sems[j], recv_sem=sems[N_PEER + j],
                                         device_id=(*chip, c), device_id_type=MESH).start()
        token[...] = jnp.zeros_like(token)

    out = pl.pallas_call(
        body, name="gather_rest_start",
        out_shape=_sem_outs() + (pltpu.HBM(slots.shape, slots.dtype), jax.ShapeDtypeStruct((8, 128), F32)),
        in_specs=(HBM,), out_specs=(SEM,) * (2 * N_PEER) + (HBM, pl.BlockSpec(memory_space=pltpu.VMEM)),
        input_output_aliases={0: 2 * N_PEER}, compiler_params=pltpu.CompilerParams(has_side_effects=EFFECT),
    )(pltpu.with_memory_space_constraint(slots, pltpu.HBM))
    return out[:2 * N_PEER], out[2 * N_PEER], out[2 * N_PEER + 1]


def _gather_wait(sems, thru, after):
    def body(in_ref, *refs):
        del in_ref
        sems, out_ref = refs[:2 * N_PEER], refs[2 * N_PEER + 1]
        x, y, c, chips = _place()
        own = out_ref.at[2 * x + y, c]
        for j, (cx, cy) in enumerate(chips):
            cp = pltpu.make_async_remote_copy(src_ref=own, dst_ref=out_ref.at[2 * cx + cy, c], send_sem=sems[j],
                                              recv_sem=sems[N_PEER + j], device_id=(cx, cy, c), device_id_type=MESH)
            cp.wait_send()
            cp.wait_recv()

    return pl.pallas_call(
        body, name="gather_rest_wait", out_shape=pltpu.HBM(thru.shape, thru.dtype),
        in_specs=(HBM,) + (SEM,) * (2 * N_PEER) + (pl.BlockSpec(memory_space=pl.ANY),), out_specs=HBM,
        input_output_aliases={0: 0}, compiler_params=pltpu.CompilerParams(has_side_effects=EFFECT),
    )(thru, *sems, after)


def _gather_forward(slots):
    def body(in_ref, out_ref, send_sems, recv_sems):
        del in_ref
        x, y, c, chips = _place()
        cps = []
        for j, (cx, cy) in enumerate(chips):
            landed = out_ref.at[2 * cx + cy, c]
            cps.append(pltpu.make_async_remote_copy(src_ref=landed, dst_ref=landed, send_sem=send_sems.at[j], recv_sem=recv_sems.at[j],
                                                    device_id=(x, y, 1 - c), device_id_type=MESH))
        for cp in cps:
            cp.start()
        for j, (cx, cy) in enumerate(chips):
            other = out_ref.at[2 * cx + cy, 1 - c]
            pltpu.make_async_remote_copy(src_ref=other, dst_ref=other, send_sem=send_sems.at[j], recv_sem=recv_sems.at[j],
                                         device_id=(x, y, 1 - c), device_id_type=MESH).wait_recv()
        for cp in cps:
            cp.wait_send()

    return pl.pallas_call(
        body, name="gather_rest_forward", out_shape=jax.ShapeDtypeStruct(slots.shape, slots.dtype),
        in_specs=[HBM], out_specs=HBM, input_output_aliases={0: 0},
        scratch_shapes=[pltpu.SemaphoreType.DMA((N_PEER,)), pltpu.SemaphoreType.DMA((N_PEER,))],
    )(slots)


def _scatter_start(pa, tag):
    ns, rh, lanes = pa.shape
    land = pltpu.with_memory_space_constraint(lax.empty((N_PEER, rh, lanes), pa.dtype), pltpu.HBM)

    def body(pa_ref, land_ref, *refs):
        x, y, c, chips = _place()
        for j, (cx, cy) in enumerate(chips):
            pltpu.make_async_remote_copy(src_ref=pa_ref.at[2 * cx + cy], dst_ref=land_ref.at[j], send_sem=refs[j],
                                         recv_sem=refs[N_PEER + j], device_id=(cx, cy, c), device_id_type=MESH).start()
        refs[-1][...] = jnp.zeros_like(refs[-1])

    out = pl.pallas_call(
        body, name=f"scatter_start_{tag}",
        out_shape=_sem_outs() + (pltpu.HBM(pa.shape, pa.dtype), pltpu.HBM(land.shape, land.dtype), jax.ShapeDtypeStruct((8, 128), F32)),
        in_specs=(HBM, HBM), out_specs=(SEM,) * (2 * N_PEER) + (HBM, HBM, pl.BlockSpec(memory_space=pltpu.VMEM)),
        input_output_aliases={0: 2 * N_PEER, 1: 2 * N_PEER + 1}, compiler_params=pltpu.CompilerParams(has_side_effects=EFFECT),
    )(pltpu.with_memory_space_constraint(pa, pltpu.HBM), land)
    return (out[:2 * N_PEER], out[2 * N_PEER], out[2 * N_PEER + 1]), out[2 * N_PEER + 2]


def _scatter_wait(sems, pa_thru, land_thru, after, tag):
    def body(pa_ref, land_ref, *refs):
        sems = refs[:2 * N_PEER]
        x, y, c, chips = _place()
        for j, (cx, cy) in enumerate(chips):
            cp = pltpu.make_async_remote_copy(src_ref=pa_ref.at[2 * cx + cy], dst_ref=land_ref.at[j], send_sem=sems[j],
                                              recv_sem=sems[N_PEER + j], device_id=(cx, cy, c), device_id_type=MESH)
            cp.wait_send()
            cp.wait_recv()

    return pl.pallas_call(
        body, name=f"scatter_wait_{tag}",
        out_shape=(pltpu.HBM(pa_thru.shape, pa_thru.dtype), pltpu.HBM(land_thru.shape, land_thru.dtype)),
        in_specs=(HBM, HBM) + (SEM,) * (2 * N_PEER) + (pl.BlockSpec(memory_space=pl.ANY),), out_specs=(HBM, HBM),
        input_output_aliases={0: 0, 1: 1}, compiler_params=pltpu.CompilerParams(has_side_effects=EFFECT),
    )(pa_thru, land_thru, *sems, after)


def _rs_swap(gp, tag):
    _, ns, rh, lanes = gp.shape

    def body(in_ref, out_ref, send_sem, recv_sem):
        x, y, c, _ = _place()
        cp = pltpu.make_async_remote_copy(src_ref=in_ref.at[1 - c], dst_ref=out_ref, send_sem=send_sem, recv_sem=recv_sem,
                                          device_id=(x, y, 1 - c), device_id_type=MESH)
        cp.start()
        cp.wait_send()
        cp.wait_recv()

    return pl.pallas_call(
        body, name=f"rs_swap_{tag}", out_shape=jax.ShapeDtypeStruct((ns, rh, lanes), gp.dtype), in_specs=[HBM], out_specs=HBM,
        scratch_shapes=[pltpu.SemaphoreType.DMA, pltpu.SemaphoreType.DMA],
    )(gp)


def _rs_tile(rh):
    return _pick_tile(rh, cap=512)


def _rs_add(gp, recv, c, tag):
    _, ns, rh, lanes = gp.shape
    tile = _rs_tile(rh)

    def body(c_ref, a_ref, b_ref, o_ref):
        o_ref[...] = (a_ref[...].astype(F32) + b_ref[...].astype(F32)).astype(o_ref.dtype)

    return pl.pallas_call(
        body, name=f"rs_add_{tag}", out_shape=jax.ShapeDtypeStruct((ns, rh, lanes), gp.dtype),
        grid_spec=pltpu.PrefetchScalarGridSpec(
            num_scalar_prefetch=1, grid=(ns, rh // tile),
            in_specs=[pl.BlockSpec((None, None, tile, lanes), lambda s, i, c_ref: (c_ref[0], s, i, 0)),
                      pl.BlockSpec((None, tile, lanes), lambda s, i, c_ref: (s, i, 0))],
            out_specs=pl.BlockSpec((None, tile, lanes), lambda s, i, c_ref: (s, i, 0))),
        compiler_params=_cparams("parallel", "parallel"),
    )(c, gp, recv)


def _rs_sum(pa, recv, place, tag):
    ns, rh, lanes = pa.shape
    tile = _rs_tile(rh)

    def body(place_ref, a_ref, r_ref, o_ref):
        acc = a_ref[...].astype(F32)
        for j in range(ns - 1):
            acc = acc + r_ref[j].astype(F32)
        o_ref[...] = acc

    return pl.pallas_call(
        body, name=f"rs_sum_{tag}", out_shape=jax.ShapeDtypeStruct((2, rh, lanes), F32),
        grid_spec=pltpu.PrefetchScalarGridSpec(
            num_scalar_prefetch=1, grid=(rh // tile,),
            in_specs=[pl.BlockSpec((None, tile, lanes), lambda i, place_ref: (place_ref[0], i, 0)),
                      pl.BlockSpec((ns - 1, tile, lanes), lambda i, place_ref: (0, i, 0))],
            out_specs=pl.BlockSpec((None, tile, lanes), lambda i, place_ref: (place_ref[1], i, 0))),
        compiler_params=_cparams("parallel"),
    )(place, pa, recv)


def _rs_share(halves, tag):
    def body(in_ref, out_ref, send_sem, recv_sem):
        del in_ref
        x, y, c, _ = _place()
        cp = pltpu.make_async_remote_copy(src_ref=out_ref.at[c], dst_ref=out_ref.at[c], send_sem=send_sem, recv_sem=recv_sem,
                                          device_id=(x, y, 1 - c), device_id_type=MESH)
        cp.start()
        other = out_ref.at[1 - c]
        pltpu.make_async_remote_copy(src_ref=other, dst_ref=other, send_sem=send_sem, recv_sem=recv_sem,
                                     device_id=(x, y, 1 - c), device_id_type=MESH).wait_recv()
        cp.wait_send()

    return pl.pallas_call(
        body, name=f"rs_share_{tag}", out_shape=jax.ShapeDtypeStruct(halves.shape, halves.dtype), in_specs=[HBM], out_specs=HBM,
        input_output_aliases={0: 0}, scratch_shapes=[pltpu.SemaphoreType.DMA, pltpu.SemaphoreType.DMA],
    )(halves)


N_DEV = 8
SMALL_ROWS = 160


def _allreduce_small(v):
    m_per, n = v.shape

    def body(x_ref, out_ref, all_ref, send_sems, recv_sems, local_sem):
        x, y, c, chips = _place()
        me, sibling = (x, y, c), (x, y, 1 - c)

        def rows(px, py, pc):
            return all_ref.at[pl.ds((4 * px + 2 * py + pc) * m_per, m_per), :]

        def copy(k, block, to, src=None):
            return pltpu.make_async_remote_copy(src_ref=rows(*block) if src is None else src, dst_ref=rows(*block),
                                                send_sem=send_sems.at[k], recv_sem=recv_sems.at[k], device_id=to, device_id_type=MESH)

        mine = pltpu.make_async_copy(x_ref, rows(*me), local_sem)
        mine.start()
        first = [copy(0, me, sibling, src=x_ref)]
        first += [copy(1 + j, me, (*chip, c), src=x_ref) for j, chip in enumerate(chips)]
        for cp in first:
            cp.start()
        passed = [copy(4 + j, (*chip, c), sibling) for j, chip in enumerate(chips)]
        for j, chip in enumerate(chips):
            copy(1 + j, (*chip, c), me).wait_recv()
            passed[j].start()
        copy(0, sibling, me).wait_recv()
        for j, chip in enumerate(chips):
            copy(4 + j, (*chip, 1 - c), me).wait_recv()
        for cp in first + passed:
            cp.wait_send()
        mine.wait()
        step = 32
        for r in range(0, m_per, step):
            acc = all_ref[r:r + step, :]
            for d in range(1, N_DEV):
                acc = acc + all_ref[d * m_per + r:d * m_per + r + step, :]
            out_ref[r:r + step, :] = acc

    vm = pl.BlockSpec(memory_space=pltpu.VMEM)
    return pl.pallas_call(
        body, name="allreduce_small", out_shape=jax.ShapeDtypeStruct((m_per, n), v.dtype), in_specs=[vm], out_specs=vm,
        scratch_shapes=[pltpu.VMEM((N_DEV * m_per, n), v.dtype), pltpu.SemaphoreType.DMA((7,)), pltpu.SemaphoreType.DMA((7,)),
                        pltpu.SemaphoreType.DMA],
        compiler_params=pltpu.CompilerParams(vmem_limit_bytes=V7X_VMEM_LIMIT),
    )(v)


BIG = {"ffn1_w_gate": ((D_MODEL, D_FF), 1), "ffn1_w_up": ((D_MODEL, D_FF), 1), "ffn1_w_down": ((D_FF, D_MODEL), 0),
       "ffn2_w_gate": ((D_MODEL, D_FF), 1), "ffn2_w_up": ((D_MODEL, D_FF), 1), "ffn2_w_down": ((D_FF, D_MODEL), 0),
       "w_in": ((D_MODEL, IN_WIDTH), 1), "w_mem_kv": ((D_MODEL, 2 * MEM_W), 0), "w_branch_ssd": ((SSD_INNER, D_MODEL), 0),
       "w_branch_gmlp": ((GMLP_W, D_MODEL), 0), "w_branch_mem": ((MEM_W, D_MODEL), 1), "w_out": ((D_MODEL, D_MODEL), 0)}
FFN1 = ("ffn1_w_gate", "ffn1_w_up", "ffn1_w_down")
FFN2 = ("ffn2_w_gate", "ffn2_w_up", "ffn2_w_down")
MIXER = ("w_out", "w_branch_ssd", "w_branch_gmlp", "w_branch_mem", "w_mem_kv", "w_in")
GATHER_GROUPS = (FFN1, FFN2 + MIXER)
REDUCE_GROUPS = (FFN2, MIXER, FFN1)
CONV_W_ROWS = 8


def _shard_rows_of(name):
    (a, b), _ = BIG[name]
    return a * b // N_SHARD // LANES


def _group_rows(names, extra=0):
    return -(-(sum(_shard_rows_of(n) for n in names) + extra) // 32) * 32

SMALL = [("ffn1_norm", 1), ("mix_norm", 1), ("mem_norm", 1), ("ssd_conv_b", 3), ("heads", 1), ("ssd_norm", 2),
         ("gmlp_v_norm", 1), ("gmlp_w_s", 128), ("gmlp_b_s", 1), ("ffn2_norm", 1), ("final_norm", 1), ("ssd_conv_w", 12)]
assert sum(n for _, n in SMALL) <= SMALL_ROWS
HEAD_VECS = ("ssd_dt_bias", "ssd_a_log", "ssd_d")


def _pack_small(vals, loss=None):
    parts = []
    for name, nrows in SMALL:
        if name == "heads":
            row = jnp.concatenate([vals[k].reshape(-1) for k in HEAD_VECS]
                                  + [jnp.zeros((1,), F32) if loss is None else loss.reshape(1)])
            parts.append(jnp.pad(row, (0, LANES - row.shape[0])).reshape(1, LANES))
        elif name in vals:
            parts.append(vals[name].reshape(nrows, LANES))
        else:
            parts.append(jnp.zeros((nrows, LANES), F32))
    buf = jnp.concatenate(parts, axis=0)
    return jnp.pad(buf, ((0, SMALL_ROWS - buf.shape[0]), (0, 0)))


def _unpack_small(buf):
    out, r = {}, 0
    for name, nrows in SMALL:
        blk = buf[r:r + nrows]
        r += nrows
        if name == "heads":
            for i, k in enumerate(HEAD_VECS):
                out[k] = blk[0, i * SSD_HEADS:(i + 1) * SSD_HEADS]
            out["loss"] = blk[0, 3 * SSD_HEADS]
        else:
            out[name] = blk
    return out


def _wire_shape(name):
    (a, b), axis = BIG[name]
    return (b, a) if axis == 1 else (a, b)


def _pack_weights(given, names, conv=False):
    parts = [(given[n][0].T if BIG[n][1] == 1 else given[n][0]).astype(BF16).reshape(_shard_rows_of(n), LANES) for n in names]
    if conv:
        pairs = lax.bitcast_convert_type(given["ssd_conv_w"], BF16).reshape(-1)
        parts.append(jnp.pad(pairs, (0, CONV_W_ROWS * LANES - pairs.shape[0])).reshape(CONV_W_ROWS, LANES))
    total = _group_rows(names, CONV_W_ROWS if conv else 0)
    packed = jnp.concatenate(parts, axis=0)
    packed = jnp.pad(packed, ((0, total - packed.shape[0]), (0, 0))).reshape(1, 2, total // 2, LANES)
    return jnp.broadcast_to(packed, (N_SHARD, 2, total // 2, LANES))


def _unpack_weights(slots, names, conv=False):
    rows = slots.reshape(N_SHARD, -1, LANES)
    out, r = {}, 0
    for name in names:
        n = _shard_rows_of(name)
        out[name] = rows[:, r:r + n].reshape(_wire_shape(name))
        r += n
    if conv:
        cols = XBC // N_SHARD
        pairs = rows[:, r:r + CONV_W_ROWS].reshape(N_SHARD, -1)[:, :SSD_CONV * cols * 2].reshape(N_SHARD, SSD_CONV, cols, 2)
        out["ssd_conv_w"] = jnp.transpose(lax.bitcast_convert_type(pairs, F32), (1, 0, 2)).reshape(SSD_CONV, XBC)
    return out


def _pack_grads(grads, names):
    total = _group_rows(names)
    gp = jnp.concatenate([grads[n].astype(BF16).reshape(N_SHARD, _shard_rows_of(n), LANES) for n in names], axis=1)
    gp = jnp.pad(gp, ((0, 0), (0, total - gp.shape[1]), (0, 0))).reshape(N_SHARD, 2, total // 2, LANES)
    return jnp.transpose(gp, (1, 0, 2, 3))


def kernel(x, mem, ffn1_norm, ffn1_w_gate, ffn1_w_up, ffn1_w_down, mix_norm, mem_norm, w_in, ssd_conv_w, ssd_conv_b, ssd_dt_bias, ssd_a_log, ssd_d, ssd_norm, gmlp_v_norm, gmlp_w_s, gmlp_b_s, w_mem_kv, w_branch_ssd, w_branch_gmlp, w_branch_mem, w_out, ffn2_norm, ffn2_w_gate, ffn2_w_up, ffn2_w_down, final_norm, loss_target, m_ffn1_norm, m_ffn1_w_gate, m_ffn1_w_up, m_ffn1_w_down, m_mix_norm, m_mem_norm, m_w_in, m_ssd_conv_w, m_ssd_conv_b, m_ssd_dt_bias, m_ssd_a_log, m_ssd_d, m_ssd_norm, m_gmlp_v_norm, m_gmlp_w_s, m_gmlp_b_s, m_w_mem_kv, m_w_branch_ssd, m_w_branch_gmlp, m_w_branch_mem, m_w_out, m_ffn2_norm, m_ffn2_w_gate, m_ffn2_w_up, m_ffn2_w_down, m_final_norm, v_ffn1_norm, v_ffn1_w_gate, v_ffn1_w_up, v_ffn1_w_down, v_mix_norm, v_mem_norm, v_w_in, v_ssd_conv_w, v_ssd_conv_b, v_ssd_dt_bias, v_ssd_a_log, v_ssd_d, v_ssd_norm, v_gmlp_v_norm, v_gmlp_w_s, v_gmlp_b_s, v_w_mem_kv, v_w_branch_ssd, v_w_branch_gmlp, v_w_branch_mem, v_w_out, v_ffn2_norm, v_ffn2_w_gate, v_ffn2_w_up, v_ffn2_w_down, v_final_norm):
    given = dict(x=x, mem=mem, ffn1_norm=ffn1_norm, ffn1_w_gate=ffn1_w_gate, ffn1_w_up=ffn1_w_up, ffn1_w_down=ffn1_w_down, mix_norm=mix_norm, mem_norm=mem_norm, w_in=w_in, ssd_conv_w=ssd_conv_w, ssd_conv_b=ssd_conv_b, ssd_dt_bias=ssd_dt_bias, ssd_a_log=ssd_a_log, ssd_d=ssd_d, ssd_norm=ssd_norm, gmlp_v_norm=gmlp_v_norm, gmlp_w_s=gmlp_w_s, gmlp_b_s=gmlp_b_s, w_mem_kv=w_mem_kv, w_branch_ssd=w_branch_ssd, w_branch_gmlp=w_branch_gmlp, w_branch_mem=w_branch_mem, w_out=w_out, ffn2_norm=ffn2_norm, ffn2_w_gate=ffn2_w_gate, ffn2_w_up=ffn2_w_up, ffn2_w_down=ffn2_w_down, final_norm=final_norm, loss_target=loss_target, m_ffn1_norm=m_ffn1_norm, m_ffn1_w_gate=m_ffn1_w_gate, m_ffn1_w_up=m_ffn1_w_up, m_ffn1_w_down=m_ffn1_w_down, m_mix_norm=m_mix_norm, m_mem_norm=m_mem_norm, m_w_in=m_w_in, m_ssd_conv_w=m_ssd_conv_w, m_ssd_conv_b=m_ssd_conv_b, m_ssd_dt_bias=m_ssd_dt_bias, m_ssd_a_log=m_ssd_a_log, m_ssd_d=m_ssd_d, m_ssd_norm=m_ssd_norm, m_gmlp_v_norm=m_gmlp_v_norm, m_gmlp_w_s=m_gmlp_w_s, m_gmlp_b_s=m_gmlp_b_s, m_w_mem_kv=m_w_mem_kv, m_w_branch_ssd=m_w_branch_ssd, m_w_branch_gmlp=m_w_branch_gmlp, m_w_branch_mem=m_w_branch_mem, m_w_out=m_w_out, m_ffn2_norm=m_ffn2_norm, m_ffn2_w_gate=m_ffn2_w_gate, m_ffn2_w_up=m_ffn2_w_up, m_ffn2_w_down=m_ffn2_w_down, m_final_norm=m_final_norm, v_ffn1_norm=v_ffn1_norm, v_ffn1_w_gate=v_ffn1_w_gate, v_ffn1_w_up=v_ffn1_w_up, v_ffn1_w_down=v_ffn1_w_down, v_mix_norm=v_mix_norm, v_mem_norm=v_mem_norm, v_w_in=v_w_in, v_ssd_conv_w=v_ssd_conv_w, v_ssd_conv_b=v_ssd_conv_b, v_ssd_dt_bias=v_ssd_dt_bias, v_ssd_a_log=v_ssd_a_log, v_ssd_d=v_ssd_d, v_ssd_norm=v_ssd_norm, v_gmlp_v_norm=v_gmlp_v_norm, v_gmlp_w_s=v_gmlp_w_s, v_gmlp_b_s=v_gmlp_b_s, v_w_mem_kv=v_w_mem_kv, v_w_branch_ssd=v_w_branch_ssd, v_w_branch_gmlp=v_w_branch_gmlp, v_w_branch_mem=v_w_branch_mem, v_w_out=v_w_out, v_ffn2_norm=v_ffn2_norm, v_ffn2_w_gate=v_ffn2_w_gate, v_ffn2_w_up=v_ffn2_w_up, v_ffn2_w_down=v_ffn2_w_down, v_final_norm=v_final_norm)
    weights = [n for n in given if n not in ("x", "mem", "loss_target") and not n.startswith(("m_", "v_"))]
    xi, yi, ci = lax.axis_index("x"), lax.axis_index("y"), lax.axis_index("c")
    chip = (2 * xi + yi).astype(jnp.int32)
    core = ci.astype(jnp.int32)
    conv_cols = XBC // N_SHARD

    first = _gather_weights(_pack_weights(given, FFN1))
    first, rest_slots = lax.optimization_barrier((first, _pack_weights(given, GATHER_GROUPS[1], conv=True)))
    sems, thru, token = _gather_start(rest_slots)
    w = _unpack_weights(first, FFN1)
    for name in ("ffn1_norm", "mix_norm", "mem_norm", "ssd_conv_b", "ssd_norm", "gmlp_v_norm", "ffn2_norm", "final_norm"):
        w[name] = given[name].reshape(1, -1)
    w["ffn1_norm"] = w["ffn1_norm"] + token[0:1, 0:1]
    for name in HEAD_VECS:
        w[name] = given[name].reshape(-1)
    w["gmlp_w_s"] = given["gmlp_w_s"][0]
    w["gmlp_b_s"] = given["gmlp_b_s"][0]

    def rest_weights(after):
        rest = _unpack_weights(_gather_forward(_gather_wait(sems, thru, after)), GATHER_GROUPS[1], conv=True)
        rest["w_in_p"], rest["w_dt"] = _split_w_in(rest.pop("w_in"))
        return rest

    pending = {}

    def push(k, group_grads):
        gp = _pack_grads(group_grads, REDUCE_GROUPS[k])
        pa = _rs_add(gp, _rs_swap(gp, k), core.reshape(1), k)
        pending[k], token = _scatter_start(pa, k)
        return token[0:1, 0:1]

    def reduced(k, after):
        pa, land = _scatter_wait(*pending[k], after, k)
        gsum = _rs_share(_rs_sum(pa, land, jnp.stack([chip, core]), k), k)
        rows = gsum.reshape(-1, LANES)
        out, r = {}, 0
        for name in REDUCE_GROUPS[k]:
            n = _shard_rows_of(name)
            a, b = given[name].shape[1:]
            out[name] = rows[r:r + n].reshape(b, a).T if BIG[name][1] == 1 else rows[r:r + n].reshape(a, b)
            r += n
        return out

    loss_part, grad_x, g = _local_step(x[0], mem[0], loss_target[0], w, rest_weights, push)

    grads, deltas, new_m, new_v = {}, {}, {}, {}

    def update(k, after):
        for name, gl in reduced(k, after).items():
            d, nm, nv = _adamw(given[name][0], gl, given["m_" + name][0], given["v_" + name][0], f"adamw_{name}")
            grads[name], deltas[name], new_m[name], new_v[name] = (a[None] for a in (gl, d, nm, nv))

    update(0, grad_x)
    update(1, deltas[REDUCE_GROUPS[0][-1]])

    small_vals = {k: g[k] for k, _ in SMALL if k != "heads"}
    small_vals.update({k: g[k] for k in HEAD_VECS})
    red = _unpack_small(_allreduce_small(_pack_small(small_vals, loss=loss_part[0, 0])))
    update(2, deltas[REDUCE_GROUPS[1][-1]])
    conv_g = lax.dynamic_slice_in_dim(red["ssd_conv_w"].reshape(SSD_CONV, XBC), chip * conv_cols, conv_cols, axis=1)
    d, nm, nv = _adamw(given["ssd_conv_w"][0], conv_g, given["m_ssd_conv_w"][0], given["v_ssd_conv_w"][0], "adamw_conv_w")
    grads["ssd_conv_w"], deltas["ssd_conv_w"], new_m["ssd_conv_w"], new_v["ssd_conv_w"] = (a[None] for a in (conv_g, d, nm, nv))
    small_names = [k for k, _ in SMALL if k not in ("heads", "ssd_conv_w")] + list(HEAD_VECS)
    pack_of = lambda prefix: _pack_small({k: given[prefix + k] for k in small_names})
    g_small = _pack_small({k: red[k] for k in small_names})
    d, nm, nv = (_unpack_small(a) for a in _adamw(pack_of(""), g_small, pack_of("m_"), pack_of("v_"), "adamw_small"))
    for k in small_names:
        shape = given[k].shape
        grads[k], deltas[k], new_m[k], new_v[k] = (a[k].reshape(shape) for a in (red, d, nm, nv))

    return (red["loss"], grad_x[None], *[grads[n] for n in weights], *[deltas[n] for n in weights],
            *[new_m[n] for n in weights], *[new_v[n] for n in weights])
```

```python
import functools
import math

import jax
import jax.numpy as jnp
from jax import lax
from jax.experimental import pallas as pl
from jax.experimental.pallas import tpu as pltpu

F32, BF16 = jnp.float32, jnp.bfloat16
HI = lax.Precision.HIGHEST
MESH = pl.DeviceIdType.MESH

D_MODEL = 1024
D_FF = 2816
MEM_LEN = 256
SSD_INNER = 2048
SSD_HEADS = 32
SSD_GROUPS = 4
SSD_STATE = 128
SSD_CONV = 4
CHUNK = 128
XBC = SSD_INNER + 2 * SSD_GROUPS * SSD_STATE
GMLP_W = 1024
GMLP_GROUPS = 8
MEM_W = 256
MEM_HEADS = 4
EPS = 1e-6
IN_WIDTH = 10528
IN_Z, IN_XBC, IN_DT, IN_UV, IN_Q, IN_GL = 0, 2048, 5120, 5152, 7200, 7456
P_Z, P_UV, P_XBC, P_GL, P_Q, P_W = 0, 2048, 4096, 7168, 10240, 10752
P_USED = 10496

ADAM_LR, ADAM_B1, ADAM_B2, ADAM_EPS, ADAM_WD, ADAM_STEP = 0.001, 0.9, 0.999, 1e-08, 0.01, 10

V7X_VMEM_LIMIT = 56 * 1024 * 1024
N_SHARD = 4
LANES = 1024


def _cparams(*sem):
    return pltpu.CompilerParams(dimension_semantics=sem, vmem_limit_bytes=V7X_VMEM_LIMIT)


def _sigmoid(x):
    return 0.5 * jnp.tanh(0.5 * x) + 0.5


def _row_tile(t):
    return min(512, t)


_DIMS = {"nn": (((1,), (0,)), ((), ())), "nt": (((1,), (1,)), ((), ())), "tn": (((0,), (0,)), ((), ()))}


def _matmul(a, b, *, mode, out_dtype, tm, tn, tk, name, scale=1.0, addend=None):
    if mode == "tn":
        k_dim, m_dim = a.shape
    else:
        m_dim, k_dim = a.shape
    n_dim = b.shape[0] if mode == "nt" else b.shape[1]
    tm, tn, tk = min(tm, m_dim), min(tn, n_dim), min(tk, k_dim)
    assert m_dim % tm == 0 and n_dim % tn == 0 and k_dim % tk == 0, (name, a.shape, b.shape, tm, tn, tk)
    ni, nj, nk = m_dim // tm, n_dim // tn, k_dim // tk
    a_spec = pl.BlockSpec((tk, tm), lambda j, i, k: (k, i)) if mode == "tn" else pl.BlockSpec((tm, tk), lambda j, i, k: (i, k))
    b_spec = pl.BlockSpec((tn, tk), lambda j, i, k: (j, k)) if mode == "nt" else pl.BlockSpec((tk, tn), lambda j, i, k: (k, j))
    o_spec = pl.BlockSpec((tm, tn), lambda j, i, k: (i, j))
    dims = _DIMS[mode]
    has_add = addend is not None

    def body(*refs):
        if has_add:
            a_ref, b_ref, r_ref, o_ref, acc_ref = refs
        else:
            a_ref, b_ref, o_ref, acc_ref = refs
        k = pl.program_id(2)

        @pl.when(k == 0)
        def _():
            acc_ref[...] = jnp.zeros_like(acc_ref)

        acc_ref[...] += lax.dot_general(a_ref[...].astype(BF16), b_ref[...].astype(BF16), dims, preferred_element_type=F32)

        @pl.when(k == nk - 1)
        def _():
            r = acc_ref[...] * scale
            if has_add:
                r = r + r_ref[...].astype(F32)
            o_ref[...] = r.astype(o_ref.dtype)

    in_specs = [a_spec, b_spec] + ([o_spec] if has_add else [])
    args = (a, b) + ((addend,) if has_add else ())
    return pl.pallas_call(
        body, name=name, grid=(nj, ni, nk), in_specs=in_specs, out_specs=o_spec,
        out_shape=jax.ShapeDtypeStruct((m_dim, n_dim), out_dtype),
        scratch_shapes=[pltpu.VMEM((tm, tn), F32)],
        compiler_params=_cparams("parallel", "parallel", "arbitrary"),
    )(*args)


ROW_STRIP = 16


def _strips(tm, fn, init=None, rb=ROW_STRIP):
    def step(i, carry):
        return fn(pl.ds(pl.multiple_of(i * rb, rb), rb), carry)
    return lax.fori_loop(0, tm // rb, step, init, unroll=2)


def _rms_fwd(x, gain, name):
    t, d = x.shape
    tm = _row_tile(t)

    def body(x_ref, g_ref, o_ref):
        xv = x_ref[...]
        r = lax.rsqrt(jnp.mean(xv * xv, axis=-1, keepdims=True) + EPS)
        o_ref[...] = (xv * r * g_ref[...]).astype(o_ref.dtype)

    return pl.pallas_call(
        body, name=name, grid=(t // tm,),
        in_specs=[pl.BlockSpec((tm, d), lambda i: (i, 0)), pl.BlockSpec((1, d), lambda i: (0, 0))],
        out_specs=pl.BlockSpec((tm, d), lambda i: (i, 0)),
        out_shape=jax.ShapeDtypeStruct((t, d), BF16), compiler_params=_cparams("parallel"),
    )(x, gain)


def _rms_bwd(x, gain, dn, dres, name):
    t, d = x.shape
    tm = _row_tile(t)
    has_res = dres is not None

    def body(*refs):
        if has_res:
            x_ref, g_ref, dn_ref, r_ref, dx_ref, dg_ref = refs
        else:
            x_ref, g_ref, dn_ref, dx_ref, dg_ref = refs

        @pl.when(pl.program_id(0) == 0)
        def _():
            dg_ref[...] = jnp.zeros_like(dg_ref)

        xv = x_ref[...]
        r = lax.rsqrt(jnp.mean(xv * xv, axis=-1, keepdims=True) + EPS)
        xh = xv * r
        dnv = dn_ref[...].astype(F32)
        dg_ref[...] += jnp.sum(dnv * xh, axis=0, keepdims=True)
        dxh = dnv * g_ref[...]
        dx = r * (dxh - xh * jnp.mean(dxh * xh, axis=-1, keepdims=True))
        if has_res:
            dx = dx + r_ref[...]
        dx_ref[...] = dx

    row = pl.BlockSpec((tm, d), lambda i: (i, 0))
    vec = pl.BlockSpec((1, d), lambda i: (0, 0))
    in_specs = [row, vec, row] + ([row] if has_res else [])
    args = (x, gain, dn) + ((dres,) if has_res else ())
    return pl.pallas_call(
        body, name=name, grid=(t // tm,), in_specs=in_specs, out_specs=[row, vec],
        out_shape=[jax.ShapeDtypeStruct((t, d), F32), jax.ShapeDtypeStruct((1, d), F32)],
        compiler_params=_cparams("arbitrary"),
    )(*args)


def _loss_head(h, gain, target, name):
    t, d = h.shape
    tm = _row_tile(t)

    def body(h_ref, g_ref, t_ref, dh_ref, dg_ref, l_ref):
        @pl.when(pl.program_id(0) == 0)
        def _():
            dg_ref[...] = jnp.zeros_like(dg_ref)
            l_ref[...] = jnp.zeros_like(l_ref)

        xv = h_ref[...]
        g = g_ref[...]
        r = lax.rsqrt(jnp.mean(xv * xv, axis=-1, keepdims=True) + EPS)
        xh = xv * r
        err = xh * g - t_ref[...]
        l_ref[...] += 0.5 * jnp.sum(jnp.mean(err * err, axis=-1, keepdims=True), axis=0, keepdims=True)
        dy = err * (1.0 / d)
        dg_ref[...] += jnp.sum(dy * xh, axis=0, keepdims=True)
        dxh = dy * g
        dh_ref[...] = r * (dxh - xh * jnp.mean(dxh * xh, axis=-1, keepdims=True))

    row = pl.BlockSpec((tm, d), lambda i: (i, 0))
    vec = pl.BlockSpec((1, d), lambda i: (0, 0))
    return pl.pallas_call(
        body, name=name, grid=(t // tm,), in_specs=[row, vec, row],
        out_specs=[row, vec, pl.BlockSpec((1, 128), lambda i: (0, 0))],
        out_shape=[jax.ShapeDtypeStruct((t, d), F32), jax.ShapeDtypeStruct((1, d), F32), jax.ShapeDtypeStruct((1, 128), F32)],
        compiler_params=_cparams("arbitrary"),
    )(h, gain, target)


FF_TILE = 1408


def _ffn_fwd(n, x, wg, wu, wd, name):
    t, d = x.shape
    tm, tn = _row_tile(t), FF_TILE
    nj = D_FF // tn

    def body(n_ref, x_ref, wg_ref, wu_ref, wd_ref, h_ref, g_ref, u_ref, acc_ref):
        j = pl.program_id(1)

        @pl.when(j == 0)
        def _():
            acc_ref[...] = jnp.zeros_like(acc_ref)

        nb = n_ref[...]
        g = lax.dot_general(nb, wg_ref[...], _DIMS["nt"], preferred_element_type=F32)
        u = lax.dot_general(nb, wu_ref[...], _DIMS["nt"], preferred_element_type=F32)
        g_ref[...] = g.astype(BF16)
        u_ref[...] = u.astype(BF16)
        a = g * _sigmoid(g) * u
        acc_ref[...] += jnp.dot(a.astype(BF16), wd_ref[...], preferred_element_type=F32)

        @pl.when(j == nj - 1)
        def _():
            h_ref[...] = x_ref[...] + 0.5 * acc_ref[...]

    row = pl.BlockSpec((tm, d), lambda i, j: (i, 0))
    act = pl.BlockSpec((tm, tn), lambda i, j: (i, j))
    return pl.pallas_call(
        body, name=name, grid=(t // tm, nj),
        in_specs=[row, row] + [pl.BlockSpec((tn, d), lambda i, j: (j, 0))] * 3,
        out_specs=[row, act, act],
        out_shape=[jax.ShapeDtypeStruct((t, d), F32), jax.ShapeDtypeStruct((t, D_FF), BF16), jax.ShapeDtypeStruct((t, D_FF), BF16)],
        scratch_shapes=[pltpu.VMEM((tm, d), F32)], compiler_params=_cparams("parallel", "arbitrary"),
    )(n, x, wg, wu, wd)


def _ffn_bwd_act(dh, g, u, wg, wu, wd, name):
    t, d = dh.shape
    tm, tn = _row_tile(t), FF_TILE
    nj = D_FF // tn

    def body(dh_ref, g_ref, u_ref, wg_ref, wu_ref, wd_ref, dn_ref, dg_ref, du_ref, a_ref, acc_ref):
        j = pl.program_id(1)

        @pl.when(j == 0)
        def _():
            acc_ref[...] = jnp.zeros_like(acc_ref)

        dhb = (0.5 * dh_ref[...]).astype(BF16)
        da = lax.dot_general(dhb, wd_ref[...], _DIMS["nt"], preferred_element_type=F32)
        gv = g_ref[...].astype(F32)
        uv = u_ref[...].astype(F32)
        sg = _sigmoid(gv)
        s = gv * sg
        dg = (da * uv * (sg * (1.0 + gv * (1.0 - sg)))).astype(BF16)
        du = (da * s).astype(BF16)
        dg_ref[...] = dg
        du_ref[...] = du
        a_ref[...] = (s * uv).astype(BF16)
        acc_ref[...] += (jnp.dot(dg, wg_ref[...], preferred_element_type=F32)
                         + jnp.dot(du, wu_ref[...], preferred_element_type=F32))

        @pl.when(j == nj - 1)
        def _():
            dn_ref[...] = acc_ref[...]

    row = pl.BlockSpec((tm, d), lambda i, j: (i, 0))
    act = pl.BlockSpec((tm, tn), lambda i, j: (i, j))
    return pl.pallas_call(
        body, name=name, grid=(t // tm, nj),
        in_specs=[row, act, act] + [pl.BlockSpec((tn, d), lambda i, j: (j, 0))] * 3,
        out_specs=[row, act, act, act],
        out_shape=[jax.ShapeDtypeStruct((t, d), F32)] + [jax.ShapeDtypeStruct((t, D_FF), BF16)] * 3,
        scratch_shapes=[pltpu.VMEM((tm, d), F32)], compiler_params=_cparams("parallel", "arbitrary"),
    )(dh, g, u, wg, wu, wd)


def _ffn_forward(x, gain, wg, wu, wd, tag):
    n = _rms_fwd(x, gain, f"{tag}_norm")
    h, g, u = _ffn_fwd(n, x, wg, wu, wd, f"{tag}_fwd")
    return h, (n, g, u)


def _ffn_backward(dh, x, gain, wg, wu, wd, saved, tag, push):
    n, g, u = saved
    dn, dg, du, a = _ffn_bwd_act(dh, g, u, wg, wu, wd, f"{tag}_bwd_act")
    kw = dict(mode="tn", out_dtype=BF16, tm=FF_TILE, tn=1024, tk=2048)
    d_wg = _matmul(dg, n, name=f"{tag}_dwg", **kw)
    d_wu = _matmul(du, n, name=f"{tag}_dwu", **kw)
    d_wd = _matmul(a, dh, scale=0.5, name=f"{tag}_dwd", **kw)
    token = push({f"{tag}_w_gate": d_wg, f"{tag}_w_up": d_wu, f"{tag}_w_down": d_wd})
    return _rms_bwd(x, gain + token, dn, dh, f"{tag}_norm_bwd")


CONV_COLS = 512
HALO = 8
CONV_STRIP = 32


def _conv_fwd(p, w, b, name):
    t = p.shape[0]
    tm = _row_tile(t)
    c0 = P_XBC // CONV_COLS

    def body(x_ref, halo_ref, w_ref, b_ref, o_ref, s_ref):
        i = pl.program_id(1)
        s_ref[0:HALO, :] = jnp.where(i > 0, halo_ref[...].astype(F32), 0.0)
        s_ref[HALO:HALO + tm, :] = x_ref[...].astype(F32)
        wv = w_ref[...]
        bv = b_ref[...]
        for r0 in range(0, tm, CONV_STRIP):
            acc = bv + wv[0:1, :] * s_ref[HALO - 3 + r0:HALO - 3 + r0 + CONV_STRIP, :]
            for k in range(1, SSD_CONV):
                acc = acc + wv[k:k + 1, :] * s_ref[HALO - 3 + k + r0:HALO - 3 + k + r0 + CONV_STRIP, :]
            o_ref[r0:r0 + CONV_STRIP, :] = (acc * _sigmoid(acc)).astype(o_ref.dtype)

    return pl.pallas_call(
        body, name=name, grid=(XBC // CONV_COLS, t // tm),
        in_specs=[pl.BlockSpec((tm, CONV_COLS), lambda j, i: (i, c0 + j)),
                  pl.BlockSpec((HALO, CONV_COLS), lambda j, i: (jnp.maximum(i * (tm // HALO) - 1, 0), c0 + j)),
                  pl.BlockSpec((SSD_CONV, CONV_COLS), lambda j, i: (0, j)), pl.BlockSpec((1, CONV_COLS), lambda j, i: (0, j))],
        out_specs=pl.BlockSpec((tm, CONV_COLS), lambda j, i: (i, j)),
        out_shape=jax.ShapeDtypeStruct((t, XBC), BF16),
        scratch_shapes=[pltpu.VMEM((tm + HALO, CONV_COLS), F32)], compiler_params=_cparams("parallel", "parallel"),
    )(p, p, w, b)


def _conv_bwd_act(p, dy, w, b, col0, name):
    t, cols = dy.shape
    tm = _row_tile(t)
    c0 = (P_XBC + col0) // CONV_COLS
    w0 = col0 // CONV_COLS

    def body(x_ref, halo_ref, dy_ref, w_ref, b_ref, da_ref, dw_ref, db_ref, s_ref):
        i = pl.program_id(1)

        @pl.when(i == 0)
        def _():
            dw_ref[...] = jnp.zeros_like(dw_ref)
            db_ref[...] = jnp.zeros_like(db_ref)

        s_ref[0:HALO, :] = jnp.where(i > 0, halo_ref[...].astype(F32), 0.0)
        s_ref[HALO:HALO + tm, :] = x_ref[...].astype(F32)
        wv = w_ref[...]
        bv = b_ref[...]
        fold = lambda v: jnp.sum(v.reshape(CONV_STRIP // 8, 8, CONV_COLS), axis=0)
        sums = [jnp.zeros((8, CONV_COLS), F32) for _ in range(SSD_CONV + 1)]
        for r0 in range(0, tm, CONV_STRIP):
            taps = [s_ref[HALO - 3 + k + r0:HALO - 3 + k + r0 + CONV_STRIP, :] for k in range(SSD_CONV)]
            acc = bv + wv[0:1, :] * taps[0]
            for k in range(1, SSD_CONV):
                acc = acc + wv[k:k + 1, :] * taps[k]
            sg = _sigmoid(acc)
            dacc = dy_ref[r0:r0 + CONV_STRIP, :].astype(F32) * (sg * (1.0 + acc * (1.0 - sg)))
            da_ref[r0:r0 + CONV_STRIP, :] = dacc.astype(BF16)
            for k in range(SSD_CONV):
                sums[k] = sums[k] + fold(dacc * taps[k])
            sums[SSD_CONV] = sums[SSD_CONV] + fold(dacc)
        for k in range(SSD_CONV):
            dw_ref[k:k + 1, :] += jnp.sum(sums[k], axis=0, keepdims=True)
        db_ref[...] += jnp.sum(sums[SSD_CONV], axis=0, keepdims=True)

    return pl.pallas_call(
        body, name=name, grid=(cols // CONV_COLS, t // tm),
        in_specs=[pl.BlockSpec((tm, CONV_COLS), lambda j, i: (i, c0 + j)),
                  pl.BlockSpec((HALO, CONV_COLS), lambda j, i: (jnp.maximum(i * (tm // HALO) - 1, 0), c0 + j)),
                  pl.BlockSpec((tm, CONV_COLS), lambda j, i: (i, j)),
                  pl.BlockSpec((SSD_CONV, CONV_COLS), lambda j, i: (0, w0 + j)), pl.BlockSpec((1, CONV_COLS), lambda j, i: (0, w0 + j))],
        out_specs=[pl.BlockSpec((tm, CONV_COLS), lambda j, i: (i, j)), pl.BlockSpec((SSD_CONV, CONV_COLS), lambda j, i: (0, j)),
                   pl.BlockSpec((1, CONV_COLS), lambda j, i: (0, j))],
        out_shape=[jax.ShapeDtypeStruct((t, cols), BF16), jax.ShapeDtypeStruct((SSD_CONV, cols), F32), jax.ShapeDtypeStruct((1, cols), F32)],
        scratch_shapes=[pltpu.VMEM((tm + HALO, CONV_COLS), F32)], compiler_params=_cparams("parallel", "arbitrary"),
    )(p, p, dy, w, b)


def _conv_bwd_dx(dacc, w, col0, name):
    t, cols = dacc.shape
    tm = _row_tile(t)
    nt = t // tm
    w0 = col0 // CONV_COLS

    def body(d_ref, halo_ref, w_ref, o_ref, s_ref):
        i = pl.program_id(1)
        s_ref[0:tm, :] = d_ref[...].astype(F32)
        s_ref[tm:tm + HALO, :] = jnp.where(i < nt - 1, halo_ref[...].astype(F32), 0.0)
        wv = w_ref[...]
        for r0 in range(0, tm, CONV_STRIP):
            acc = wv[3:4, :] * s_ref[r0:r0 + CONV_STRIP, :]
            for k in range(SSD_CONV - 1):
                acc = acc + wv[k:k + 1, :] * s_ref[3 - k + r0:3 - k + r0 + CONV_STRIP, :]
            o_ref[r0:r0 + CONV_STRIP, :] = acc.astype(o_ref.dtype)

    return pl.pallas_call(
        body, name=name, grid=(cols // CONV_COLS, nt),
        in_specs=[pl.BlockSpec((tm, CONV_COLS), lambda j, i: (i, j)),
                  pl.BlockSpec((HALO, CONV_COLS), lambda j, i: (jnp.minimum((i + 1) * (tm // HALO), t // HALO - 1), j)),
                  pl.BlockSpec((SSD_CONV, CONV_COLS), lambda j, i: (0, w0 + j))],
        out_specs=pl.BlockSpec((tm, CONV_COLS), lambda j, i: (i, j)),
        out_shape=jax.ShapeDtypeStruct((t, cols), BF16),
        scratch_shapes=[pltpu.VMEM((tm + HALO, CONV_COLS), F32)], compiler_params=_cparams("parallel", "parallel"),
    )(dacc, dacc, w)


GROUP_COLS = SSD_INNER // SSD_GROUPS
PAIRS = GROUP_COLS // 128
HEADS_PER_GROUP = SSD_HEADS // SSD_GROUPS


def _dt_fwd(dt_raw, bias, name):
    t, n = dt_raw.shape
    tm = _row_tile(t)

    def body(x_ref, b_ref, o_ref):
        v = x_ref[...] + b_ref[...]
        o_ref[...] = jnp.maximum(v, 0.0) + jnp.log1p(jnp.exp(-jnp.abs(v)))

    row = pl.BlockSpec((tm, n), lambda i: (i, 0))
    vec = pl.BlockSpec((1, n), lambda i: (0, 0))
    return pl.pallas_call(body, name=name, grid=(t // tm,), in_specs=[row, vec], out_specs=row,
                          out_shape=jax.ShapeDtypeStruct((t, n), F32), compiler_params=_cparams("parallel"))(dt_raw, bias)


def _dt_bwd(ddt, dt_raw, bias, name):
    t, n = dt_raw.shape
    tm = _row_tile(t)

    def body(d_ref, x_ref, b_ref, o_ref, db_ref):
        @pl.when(pl.program_id(0) == 0)
        def _():
            db_ref[...] = jnp.zeros_like(db_ref)

        dr = d_ref[...] * _sigmoid(x_ref[...] + b_ref[...])
        o_ref[...] = dr.astype(o_ref.dtype)
        db_ref[...] += jnp.sum(dr, axis=0, keepdims=True)

    row = pl.BlockSpec((tm, n), lambda i: (i, 0))
    vec = pl.BlockSpec((1, n), lambda i: (0, 0))
    return pl.pallas_call(body, name=name, grid=(t // tm,), in_specs=[row, row, vec], out_specs=[row, vec],
                          out_shape=[jax.ShapeDtypeStruct((t, n), BF16), jax.ShapeDtypeStruct((1, n), F32)],
                          compiler_params=_cparams("arbitrary"))(ddt, dt_raw, bias)


SSD_STEP = 2


def _ssd_common(dt, dtt, a_log_w, a_log_t):
    l = CHUNK
    a = -jnp.exp(a_log_w)
    at = -jnp.exp(a_log_t)
    rowi = lax.broadcasted_iota(jnp.int32, (l, l), 0)
    coli = lax.broadcasted_iota(jnp.int32, (l, l), 1)
    tri = rowi >= coli
    lower = tri.astype(F32)
    upper = (rowi <= coli).astype(F32)
    acs = jnp.dot(lower, dt * a, precision=HI, preferred_element_type=F32)
    acst = jnp.dot(dtt * at, upper, precision=HI, preferred_element_type=F32)
    return a, acs, acst, tri, upper


def _pair_bc(w, lo, p):
    return jnp.where(lo, w[:, 2 * p:2 * p + 1], w[:, 2 * p + 1:2 * p + 2])


def _ssd_specs(t):
    rows = SSD_STEP * CHUNK
    assert t % rows == 0
    return t // rows, dict(
        xs=lambda cm: pl.BlockSpec((rows, GROUP_COLS), lambda g, c: (cm(c), g)),
        bm=lambda cm: pl.BlockSpec((rows, SSD_STATE), lambda g, c: (cm(c), SSD_INNER // SSD_STATE + g)),
        cmat=lambda cm: pl.BlockSpec((rows, SSD_STATE), lambda g, c: (cm(c), SSD_INNER // SSD_STATE + SSD_GROUPS + g)),
        dtw=lambda cm: pl.BlockSpec((1, rows, 128), lambda g, c: (g, cm(c), 0)),
        dtt=lambda cm: pl.BlockSpec((1, HEADS_PER_GROUP, rows), lambda g, c: (g, 0, cm(c))),
        wide=lambda cm: pl.BlockSpec((1, 1, 128), lambda g, c: (g, 0, 0)),
        tall=lambda cm: pl.BlockSpec((1, HEADS_PER_GROUP, 1), lambda g, c: (g, 0, 0)),
        grp=lambda cm: pl.BlockSpec((rows, GROUP_COLS), lambda g, c: (cm(c), g)),
        vec=lambda cm: pl.BlockSpec((1, GROUP_COLS), lambda g, c: (0, g)),
        state=lambda cm: pl.BlockSpec((1, SSD_STEP, PAIRS, SSD_STATE, 128), lambda g, c: (g, cm(c), 0, 0, 0)),
    )


def _ssd_fwd(xc, p, hv, norm_g, name):
    t = xc.shape[0]
    nc, sp = _ssd_specs(t)
    ident = lambda c: c

    def body(xs_ref, b_ref, c_ref, dtw_ref, dtt_ref, aw_ref, at_ref, dk_ref, z_ref, ng_ref,
             y_ref, ys_ref, h_ref, st_ref):
        @pl.when(pl.program_id(1) == 0)
        def _():
            st_ref[...] = jnp.zeros_like(st_ref)

        lo = lax.broadcasted_iota(jnp.int32, (1, 128), 1) < 64
        dskip = dk_ref[0]
        for s in range(SSD_STEP):
            rows = slice(s * CHUNK, (s + 1) * CHUNK)
            dt = dtw_ref[0, rows, :]
            a, acs, acst, tri, _ = _ssd_common(dt, dtt_ref[0, :, rows], aw_ref[0], at_ref[0])
            ecs = jnp.exp(acs)
            alast = acs[CHUNK - 1:CHUNK, :]
            bmat, cmat = b_ref[rows, :], c_ref[rows, :]
            cb = lax.dot_general(cmat, bmat, _DIMS["nt"], preferred_element_type=F32)
            for pi in range(PAIRS):
                cols = slice(pi * 128, (pi + 1) * 128)
                x = xs_ref[rows, cols].astype(F32)
                xdt = x * _pair_bc(dt, lo, pi)
                ydiag = jnp.zeros((CHUNK, 128), F32)
                for r, mask in ((2 * pi, lo), (2 * pi + 1, jnp.logical_not(lo))):
                    lam = jnp.exp(jnp.where(tri, acs[:, r:r + 1] - acst[r:r + 1, :], -1e30))
                    m = (cb * lam).astype(BF16)
                    ydiag = ydiag + jnp.dot(m, jnp.where(mask, xdt, 0.0).astype(BF16), preferred_element_type=F32)
                ht = st_ref[pi]
                h_ref[0, s, pi] = ht
                yoff = jnp.dot(cmat, ht.astype(BF16), preferred_element_type=F32) * _pair_bc(ecs, lo, pi)
                y_ref[rows, cols] = (ydiag + yoff + _pair_bc(dskip, lo, pi) * x).astype(y_ref.dtype)
                alp = _pair_bc(alast, lo, pi)
                e = jnp.exp(alp - _pair_bc(acs, lo, pi))
                st = lax.dot_general(bmat, (xdt * e).astype(BF16), _DIMS["tn"], preferred_element_type=F32)
                st_ref[pi] = ht * jnp.exp(alp) + st
            zf = z_ref[rows, :].astype(F32)
            yg = y_ref[rows, :].astype(F32) * (zf * _sigmoid(zf))
            rstd = lax.rsqrt(jnp.mean(yg * yg, axis=-1, keepdims=True) + EPS)
            ys_ref[rows, :] = (yg * rstd * ng_ref[...]).astype(ys_ref.dtype)

    ins = ["xs", "bm", "cmat", "dtw", "dtt", "wide", "tall", "wide", "grp", "vec"]
    return pl.pallas_call(
        body, name=name, grid=(SSD_GROUPS, nc),
        in_specs=[sp[k](ident) for k in ins],
        out_specs=[sp["grp"](ident), sp["grp"](ident), sp["state"](ident)],
        out_shape=[jax.ShapeDtypeStruct((t, SSD_INNER), BF16), jax.ShapeDtypeStruct((t, SSD_INNER), BF16),
                   jax.ShapeDtypeStruct((SSD_GROUPS, t // CHUNK, PAIRS, SSD_STATE, 128), F32)],
        scratch_shapes=[pltpu.VMEM((PAIRS, SSD_STATE, 128), F32)], compiler_params=_cparams("parallel", "arbitrary"),
    )(xc, xc, xc, hv["dtw"], hv["dtt"], hv["alog_w"], hv["alog_t"], hv["dskip_w"], p, norm_g)


def _ssd_bwd(xc, p, hv, norm_g, y, dys, states, name):
    t = xc.shape[0]
    nc, sp = _ssd_specs(t)
    rev = lambda c: nc - 1 - c

    def body(xs_ref, b_ref, c_ref, dtw_ref, dtt_ref, aw_ref, at_ref, dk_ref, z_ref, ng_ref,
             y_ref, dys_ref, h_ref,
             dxs_ref, db_ref, dc_ref, dz_ref, ddt_ref, hsum_ref, dng_ref, dst_ref):
        @pl.when(pl.program_id(1) == 0)
        def _():
            dst_ref[...] = jnp.zeros_like(dst_ref)
            hsum_ref[...] = jnp.zeros_like(hsum_ref)
            dng_ref[...] = jnp.zeros_like(dng_ref)

        lane = lax.broadcasted_iota(jnp.int32, (1, 128), 1)
        lo = lane < 64
        dskip = dk_ref[0]
        sel_r = lax.broadcasted_iota(jnp.int32, (128, 128), 0)
        sel_c = lax.broadcasted_iota(jnp.int32, (128, 128), 1)
        refs = (xs_ref, b_ref, c_ref, dtw_ref, dtt_ref, aw_ref, at_ref, dk_ref, z_ref, ng_ref, y_ref, dys_ref, h_ref,
                dxs_ref, db_ref, dc_ref, dz_ref, ddt_ref, hsum_ref, dng_ref, dst_ref)
        for s in reversed(range(SSD_STEP)):
            chunk_bwd(refs, slice(s * CHUNK, (s + 1) * CHUNK), s, lane, lo, dskip, sel_r, sel_c)

    def chunk_bwd(refs, rows, s, lane, lo, dskip, sel_r, sel_c):
        (xs_ref, b_ref, c_ref, dtw_ref, dtt_ref, aw_ref, at_ref, dk_ref, z_ref, ng_ref, y_ref, dys_ref, h_ref,
         dxs_ref, db_ref, dc_ref, dz_ref, ddt_ref, hsum_ref, dng_ref, dst_ref) = refs
        dt = dtw_ref[0, rows, :]
        a, acs, acst, tri, upper = _ssd_common(dt, dtt_ref[0, :, rows], aw_ref[0], at_ref[0])
        ecs = jnp.exp(acs)
        alast = acs[CHUNK - 1:CHUNK, :]
        bmat, cmat = b_ref[rows, :], c_ref[rows, :]
        cb = lax.dot_general(cmat, bmat, _DIMS["nt"], preferred_element_type=F32)

        zf = z_ref[rows, :].astype(F32)
        sg = _sigmoid(zf)
        sz = zf * sg
        yv = y_ref[rows, :].astype(F32)
        yg = yv * sz
        rstd = lax.rsqrt(jnp.mean(yg * yg, axis=-1, keepdims=True) + EPS)
        yhat = yg * rstd
        dysv = dys_ref[rows, :].astype(F32)
        dng_ref[...] += jnp.sum(dysv * yhat, axis=0, keepdims=True)
        dyh = dysv * ng_ref[...]
        dyg = rstd * (dyh - yhat * jnp.mean(dyh * yhat, axis=-1, keepdims=True))
        dz_ref[rows, :] = (dyg * yv * (sg * (1.0 + zf * (1.0 - sg)))).astype(dz_ref.dtype)
        dy_all = dyg * sz

        dal = jnp.zeros((CHUNK, 128), F32)
        ddtm = jnp.zeros((CHUNK, 128), F32)
        dalast = jnp.zeros((8, 128), F32)
        ddsk = jnp.zeros((8, 128), F32)
        dcb = jnp.zeros((CHUNK, CHUNK), F32)
        qcol = jnp.zeros((8, CHUNK), F32)
        sub8 = lax.broadcasted_iota(jnp.int32, (8, CHUNK), 0)
        dc_acc = jnp.zeros((CHUNK, SSD_STATE), F32)
        db_acc = jnp.zeros((CHUNK, SSD_STATE), F32)
        for pi in range(PAIRS):
            sel = (sel_c == 2 * pi + (sel_r >= 64).astype(jnp.int32)).astype(BF16)

            def hsum(v, sel=sel):
                return jnp.dot(v.astype(BF16), sel, preferred_element_type=F32)

            dyp = dy_all[:, pi * 128:(pi + 1) * 128]
            x = xs_ref[rows, pi * 128:(pi + 1) * 128].astype(F32)
            dtp = _pair_bc(dt, lo, pi)
            xdt = x * dtp
            dxdt = jnp.zeros((CHUNK, 128), F32)
            for r, mask in ((2 * pi, lo), (2 * pi + 1, jnp.logical_not(lo))):
                lam = jnp.exp(jnp.where(tri, acs[:, r:r + 1] - acst[r:r + 1, :], -1e30))
                m32 = cb * lam
                m = m32.astype(BF16)
                dyr = jnp.where(mask, dyp, 0.0).astype(BF16)
                xr = jnp.where(mask, xdt, 0.0).astype(BF16)
                dm = lax.dot_general(dyr, xr, _DIMS["nt"], preferred_element_type=F32)
                dcb = dcb + dm * lam
                q = dm * m32
                dal = dal + jnp.sum(q, axis=1, keepdims=True) * (lane == r).astype(F32)
                qcol = qcol + jnp.where(sub8 == r, jnp.sum(q, axis=0, keepdims=True), 0.0)
                dxdt = dxdt + lax.dot_general(m, dyr, _DIMS["tn"], preferred_element_type=F32)
            ht = h_ref[0, s, pi]
            htb = ht.astype(BF16)
            ecp = _pair_bc(ecs, lo, pi)
            yoff = jnp.dot(cmat, htb, preferred_element_type=F32) * ecp
            dg = (dyp * ecp).astype(BF16)
            dc_acc = dc_acc + lax.dot_general(dg, htb, _DIMS["nt"], preferred_element_type=F32)
            dht = lax.dot_general(cmat, dg, _DIMS["tn"], preferred_element_type=F32)
            dal = dal + hsum(dyp * yoff)
            dhn = dst_ref[pi]
            dhnb = dhn.astype(BF16)
            alp = _pair_bc(alast, lo, pi)
            e = jnp.exp(alp - _pair_bc(acs, lo, pi))
            xe = xdt * e
            db_acc = db_acc + lax.dot_general(xe.astype(BF16), dhnb, _DIMS["nt"], preferred_element_type=F32)
            dxe = jnp.dot(bmat, dhnb, preferred_element_type=F32)
            dxdt = dxdt + dxe * e
            tt = hsum(dxe * xe)
            dal = dal - tt
            dec = jnp.exp(alp)
            dalast = dalast + jnp.sum(tt, axis=0, keepdims=True) + hsum(
                jnp.broadcast_to(jnp.sum(dhn * ht, axis=0, keepdims=True) * dec, (8, 128)))
            dst_ref[pi] = dht + dhn * dec
            dxs_ref[rows, pi * 128:(pi + 1) * 128] = (_pair_bc(dskip, lo, pi) * dyp + dxdt * dtp).astype(dxs_ref.dtype)
            ddtm = ddtm + hsum(dxdt * x)
            ddsk = ddsk + hsum(jnp.broadcast_to(jnp.sum(dyp * x, axis=0, keepdims=True), (8, 128)))
        rowi = lax.broadcasted_iota(jnp.int32, (CHUNK, 128), 0)
        qcol_w = lax.dot_general(jnp.concatenate([qcol, jnp.zeros((CHUNK - 8, CHUNK), F32)], axis=0), (sel_r == sel_c).astype(F32),
                                 _DIMS["tn"], precision=HI, preferred_element_type=F32)
        dal = dal - qcol_w + jnp.where(rowi == CHUNK - 1, dalast[0:1, :], 0.0)
        dda = jnp.dot(upper, dal, precision=HI, preferred_element_type=F32)
        ddt_ref[0, rows, :] = ddtm + dda * a
        hsum_ref[0, 1:2, :] += jnp.sum(dda * dt, axis=0, keepdims=True) * a
        hsum_ref[0, 2:3, :] += ddsk[0:1, :]
        dcbb = dcb.astype(BF16)
        dc_ref[rows, :] = (jnp.dot(dcbb, bmat, preferred_element_type=F32) + dc_acc).astype(dc_ref.dtype)
        db_ref[rows, :] = (lax.dot_general(dcbb, cmat, _DIMS["tn"], preferred_element_type=F32) + db_acc).astype(db_ref.dtype)

    ins = ["xs", "bm", "cmat", "dtw", "dtt", "wide", "tall", "wide", "grp", "vec", "grp", "grp", "state"]
    col = lambda: pl.BlockSpec((SSD_STEP * CHUNK, SSD_STATE), lambda g, c: (rev(c), g))
    return pl.pallas_call(
        body, name=name, grid=(SSD_GROUPS, nc),
        in_specs=[sp[k](rev) for k in ins],
        out_specs=[sp["grp"](rev), col(), col(), sp["grp"](rev), sp["dtw"](rev),
                   pl.BlockSpec((1, 8, 128), lambda g, c: (g, 0, 0)), sp["vec"](rev)],
        out_shape=[jax.ShapeDtypeStruct((t, SSD_INNER), BF16), jax.ShapeDtypeStruct((t, SSD_GROUPS * SSD_STATE), BF16),
                   jax.ShapeDtypeStruct((t, SSD_GROUPS * SSD_STATE), BF16), jax.ShapeDtypeStruct((t, SSD_INNER), BF16),
                   jax.ShapeDtypeStruct((SSD_GROUPS, t, 128), F32), jax.ShapeDtypeStruct((SSD_GROUPS, 8, 128), F32),
                   jax.ShapeDtypeStruct((1, SSD_INNER), F32)],
        scratch_shapes=[pltpu.VMEM((PAIRS, SSD_STATE, 128), F32)], compiler_params=_cparams("parallel", "arbitrary"),
    )(xc, xc, xc, hv["dtw"], hv["dtt"], hv["alog_w"], hv["alog_t"], hv["dskip_w"], p, norm_g, y, dys, states)


def _wide(v):
    return jnp.pad(v.reshape(SSD_GROUPS, 1, HEADS_PER_GROUP), ((0, 0), (0, 0), (0, 128 - HEADS_PER_GROUP)))


def _head_inputs(dt, a_log, d_skip):
    t = dt.shape[0]
    g = dt[:, :SSD_HEADS].reshape(t, SSD_GROUPS, HEADS_PER_GROUP)
    return dict(
        dtw=jnp.pad(jnp.transpose(g, (1, 0, 2)), ((0, 0), (0, 0), (0, 128 - HEADS_PER_GROUP))),
        dtt=jnp.transpose(g, (1, 2, 0)),
        alog_w=_wide(a_log), alog_t=a_log.reshape(SSD_GROUPS, HEADS_PER_GROUP, 1),
        dskip_w=_wide(d_skip),
    )


def _gelu(x):
    return 0.5 * x * (1.0 + lax.erf(x * (1.0 / math.sqrt(2.0))))


def _gelu_grad(x):
    return 0.5 * (1.0 + lax.erf(x * (1.0 / math.sqrt(2.0)))) + x * jnp.exp(-0.5 * x * x) * (1.0 / math.sqrt(2.0 * math.pi))


def _tril_mask():
    r = lax.broadcasted_iota(jnp.int32, (CHUNK, CHUNK), 0)
    c = lax.broadcasted_iota(jnp.int32, (CHUNK, CHUNK), 1)
    return r >= c


def _gmlp_fwd(p, v_gain, w_s, b_col, name):
    t = p.shape[0]
    tm = _row_tile(t)
    u0 = P_UV // GMLP_W

    def body(u_ref, v_ref, gn_ref, ws_ref, bs_ref, o_ref):
        v = _gelu(v_ref[...].astype(F32))
        v = (v * lax.rsqrt(jnp.mean(v * v, axis=-1, keepdims=True) + EPS) * gn_ref[...]).astype(BF16)
        tril = _tril_mask()
        wm = [jnp.where(tril, ws_ref[g], 0.0).astype(BF16) for g in range(GMLP_GROUPS)]
        for k in range(tm // CHUNK):
            rows = slice(k * CHUNK, (k + 1) * CHUNK)
            for g in range(GMLP_GROUPS):
                cols = slice(g * 128, (g + 1) * 128)
                mixed = jnp.dot(wm[g], v[rows, cols], preferred_element_type=F32) + bs_ref[g]
                o_ref[rows, cols] = (_gelu(u_ref[rows, cols].astype(F32)) * mixed).astype(o_ref.dtype)

    return pl.pallas_call(
        body, name=name, grid=(t // tm,),
        in_specs=[pl.BlockSpec((tm, GMLP_W), lambda i: (i, u0)), pl.BlockSpec((tm, GMLP_W), lambda i: (i, u0 + 1)),
                  pl.BlockSpec((1, GMLP_W), lambda i: (0, 0)), pl.BlockSpec((GMLP_GROUPS, CHUNK, CHUNK), lambda i: (0, 0, 0)),
                  pl.BlockSpec((GMLP_GROUPS, CHUNK, 1), lambda i: (0, 0, 0))],
        out_specs=pl.BlockSpec((tm, GMLP_W), lambda i: (i, 0)),
        out_shape=jax.ShapeDtypeStruct((t, GMLP_W), BF16), compiler_params=_cparams("parallel"),
    )(p, p, v_gain, w_s, b_col)


def _gmlp_bwd(p, dy, v_gain, w_s, b_col, name):
    t = p.shape[0]
    tm = _row_tile(t)
    u0 = P_UV // GMLP_W

    def body(u_ref, v_ref, dy_ref, gn_ref, ws_ref, bs_ref, duv_ref, dws_ref, dbs_ref, dgn_ref, dvn_ref):
        @pl.when(pl.program_id(0) == 0)
        def _():
            dws_ref[...] = jnp.zeros_like(dws_ref)
            dbs_ref[...] = jnp.zeros_like(dbs_ref)
            dgn_ref[...] = jnp.zeros_like(dgn_ref)

        vraw = v_ref[...].astype(F32)
        va = _gelu(vraw)
        rstd = lax.rsqrt(jnp.mean(va * va, axis=-1, keepdims=True) + EPS)
        vhat = va * rstd
        gain = gn_ref[...]
        vn = (vhat * gain).astype(BF16)
        tril = _tril_mask()
        wm = [jnp.where(tril, ws_ref[g], 0.0).astype(BF16) for g in range(GMLP_GROUPS)]
        for k in range(tm // CHUNK):
            rows = slice(k * CHUNK, (k + 1) * CHUNK)
            for g in range(GMLP_GROUPS):
                cols = slice(g * 128, (g + 1) * 128)
                uraw = u_ref[rows, cols].astype(F32)
                vb = vn[rows, cols]
                mixed = jnp.dot(wm[g], vb, preferred_element_type=F32) + bs_ref[g]
                dyb = dy_ref[rows, cols].astype(F32)
                duv_ref[rows, cols] = (dyb * mixed * _gelu_grad(uraw)).astype(duv_ref.dtype)
                dmix = dyb * _gelu(uraw)
                dmb = dmix.astype(BF16)
                dws_ref[g] += jnp.where(tril, lax.dot_general(dmb, vb, _DIMS["nt"], preferred_element_type=F32), 0.0)
                dbs_ref[g] += jnp.sum(dmix, axis=1, keepdims=True)
                dvn_ref[rows, cols] = lax.dot_general(wm[g], dmb, _DIMS["tn"], preferred_element_type=F32)
        dvn = dvn_ref[...]
        dgn_ref[...] += jnp.sum(dvn * vhat, axis=0, keepdims=True)
        dvh = dvn * gain
        dva = rstd * (dvh - vhat * jnp.mean(dvh * vhat, axis=-1, keepdims=True))
        duv_ref[:, GMLP_W:2 * GMLP_W] = (dva * _gelu_grad(vraw)).astype(duv_ref.dtype)

    return pl.pallas_call(
        body, name=name, grid=(t // tm,),
        in_specs=[pl.BlockSpec((tm, GMLP_W), lambda i: (i, u0)), pl.BlockSpec((tm, GMLP_W), lambda i: (i, u0 + 1)),
                  pl.BlockSpec((tm, GMLP_W), lambda i: (i, 0)),
                  pl.BlockSpec((1, GMLP_W), lambda i: (0, 0)), pl.BlockSpec((GMLP_GROUPS, CHUNK, CHUNK), lambda i: (0, 0, 0)),
                  pl.BlockSpec((GMLP_GROUPS, CHUNK, 1), lambda i: (0, 0, 0))],
        out_specs=[pl.BlockSpec((tm, 2 * GMLP_W), lambda i: (i, 0)), pl.BlockSpec((GMLP_GROUPS, CHUNK, CHUNK), lambda i: (0, 0, 0)),
                   pl.BlockSpec((GMLP_GROUPS, CHUNK, 1), lambda i: (0, 0, 0)), pl.BlockSpec((1, GMLP_W), lambda i: (0, 0))],
        out_shape=[jax.ShapeDtypeStruct((t, 2 * GMLP_W), BF16), jax.ShapeDtypeStruct((GMLP_GROUPS, CHUNK, CHUNK), F32),
                   jax.ShapeDtypeStruct((GMLP_GROUPS, CHUNK, 1), F32), jax.ShapeDtypeStruct((1, GMLP_W), F32)],
        scratch_shapes=[pltpu.VMEM((tm, GMLP_W), F32)], compiler_params=_cparams("arbitrary"),
    )(p, p, dy, v_gain, w_s, b_col)


def _head_masks():
    lane = lax.broadcasted_iota(jnp.int32, (1, MEM_W), 1)
    return [(lane >= h * 64) & (lane < (h + 1) * 64) for h in range(MEM_HEADS)]


def _mem_fwd(p, kv, name):
    t = p.shape[0]
    tm = _row_tile(t)
    q0 = P_Q // MEM_W

    def body(q_ref, kv_ref, o_ref):
        q = q_ref[...]
        k = kv_ref[:, 0:MEM_W].astype(BF16)
        v = kv_ref[:, MEM_W:2 * MEM_W].astype(BF16)
        out = jnp.zeros((tm, MEM_W), F32)
        for mask in _head_masks():
            s = lax.dot_general(jnp.where(mask, q, 0), k, _DIMS["nt"], preferred_element_type=F32) * 0.125
            e = jnp.exp(s - jnp.max(s, axis=-1, keepdims=True))
            pr = (e * (1.0 / jnp.sum(e, axis=-1, keepdims=True))).astype(BF16)
            out = out + jnp.where(mask, jnp.dot(pr, v, preferred_element_type=F32), 0.0)
        o_ref[...] = out.astype(o_ref.dtype)

    return pl.pallas_call(
        body, name=name, grid=(t // tm,),
        in_specs=[pl.BlockSpec((tm, MEM_W), lambda i: (i, q0)), pl.BlockSpec((MEM_LEN, 2 * MEM_W), lambda i: (0, 0))],
        out_specs=pl.BlockSpec((tm, MEM_W), lambda i: (i, 0)),
        out_shape=jax.ShapeDtypeStruct((t, MEM_W), BF16), compiler_params=_cparams("parallel"),
    )(p, kv)


def _mem_bwd(p, kv, dy, name):
    t = p.shape[0]
    tm = _row_tile(t)
    q0 = P_Q // MEM_W

    def body(q_ref, kv_ref, dy_ref, dq_ref, dkv_ref):
        @pl.when(pl.program_id(0) == 0)
        def _():
            dkv_ref[...] = jnp.zeros_like(dkv_ref)

        q = q_ref[...]
        dy = dy_ref[...]
        k = kv_ref[:, 0:MEM_W].astype(BF16)
        v = kv_ref[:, MEM_W:2 * MEM_W].astype(BF16)
        dq = jnp.zeros((tm, MEM_W), F32)
        dk = jnp.zeros((MEM_LEN, MEM_W), F32)
        dv = jnp.zeros((MEM_LEN, MEM_W), F32)
        for mask in _head_masks():
            qh = jnp.where(mask, q, 0)
            dyh = jnp.where(mask, dy, 0)
            s = lax.dot_general(qh, k, _DIMS["nt"], preferred_element_type=F32) * 0.125
            e = jnp.exp(s - jnp.max(s, axis=-1, keepdims=True))
            pr = e * (1.0 / jnp.sum(e, axis=-1, keepdims=True))
            prb = pr.astype(BF16)
            dp = lax.dot_general(dyh, v, _DIMS["nt"], preferred_element_type=F32)
            ds = (pr * (dp - jnp.sum(dp * pr, axis=-1, keepdims=True)) * 0.125).astype(BF16)
            dq = dq + jnp.where(mask, jnp.dot(ds, k, preferred_element_type=F32), 0.0)
            dk = dk + lax.dot_general(ds, qh, _DIMS["tn"], preferred_element_type=F32)
            dv = dv + lax.dot_general(prb, dyh, _DIMS["tn"], preferred_element_type=F32)
        dq_ref[...] = dq.astype(dq_ref.dtype)
        dkv_ref[:, 0:MEM_W] += dk
        dkv_ref[:, MEM_W:2 * MEM_W] += dv

    return pl.pallas_call(
        body, name=name, grid=(t // tm,),
        in_specs=[pl.BlockSpec((tm, MEM_W), lambda i: (i, q0)), pl.BlockSpec((MEM_LEN, 2 * MEM_W), lambda i: (0, 0)),
                  pl.BlockSpec((tm, MEM_W), lambda i: (i, 0))],
        out_specs=[pl.BlockSpec((tm, MEM_W), lambda i: (i, 0)), pl.BlockSpec((MEM_LEN, 2 * MEM_W), lambda i: (0, 0))],
        out_shape=[jax.ShapeDtypeStruct((t, MEM_W), BF16), jax.ShapeDtypeStruct((MEM_LEN, 2 * MEM_W), F32)],
        compiler_params=_cparams("arbitrary"),
    )(p, kv, dy)


def _merge_fwd(p, b_ssd, b_gmlp, b_mem, name):
    t = p.shape[0]
    tm = _row_tile(t)
    g0 = P_GL // D_MODEL

    def body(g1, g2, g3, b1, b2, b3, o_ref):
        def strip(rows, carry):
            acc = _sigmoid(g1[rows, :].astype(F32)) * b1[rows, :].astype(F32)
            acc = acc + _sigmoid(g2[rows, :].astype(F32)) * b2[rows, :].astype(F32)
            acc = acc + _sigmoid(g3[rows, :].astype(F32)) * b3[rows, :].astype(F32)
            o_ref[rows, :] = acc.astype(o_ref.dtype)
            return carry

        _strips(tm, strip, 0)

    row = pl.BlockSpec((tm, D_MODEL), lambda i: (i, 0))
    return pl.pallas_call(
        body, name=name, grid=(t // tm,),
        in_specs=[pl.BlockSpec((tm, D_MODEL), lambda i, k=k: (i, g0 + k)) for k in range(3)] + [row] * 3,
        out_specs=row, out_shape=jax.ShapeDtypeStruct((t, D_MODEL), BF16), compiler_params=_cparams("parallel"),
    )(p, p, p, b_ssd, b_gmlp, b_mem)


def _merge_bwd(p, dm, b_ssd, b_gmlp, b_mem, name):
    t = p.shape[0]
    tm = _row_tile(t)
    g0 = P_GL // D_MODEL

    def body(g1, g2, g3, dm_ref, b1, b2, b3, d1, d2, d3, dgl_ref):
        def strip(rows, carry):
            dmv = dm_ref[rows, :].astype(F32)
            for k, (g_ref, b_ref, d_ref) in enumerate(((g1, b1, d1), (g2, b2, d2), (g3, b3, d3))):
                sg = _sigmoid(g_ref[rows, :].astype(F32))
                d_ref[rows, :] = (dmv * sg).astype(d_ref.dtype)
                dgl_ref[rows, k * D_MODEL:(k + 1) * D_MODEL] = (dmv * b_ref[rows, :].astype(F32) * sg * (1.0 - sg)).astype(dgl_ref.dtype)
            return carry

        _strips(tm, strip, 0)

    row = pl.BlockSpec((tm, D_MODEL), lambda i: (i, 0))
    return pl.pallas_call(
        body, name=name, grid=(t // tm,),
        in_specs=[pl.BlockSpec((tm, D_MODEL), lambda i, k=k: (i, g0 + k)) for k in range(3)] + [row] * 4,
        out_specs=[row, row, row, pl.BlockSpec((tm, 3 * D_MODEL), lambda i: (i, 0))],
        out_shape=[jax.ShapeDtypeStruct((t, D_MODEL), BF16)] * 3 + [jax.ShapeDtypeStruct((t, 3 * D_MODEL), BF16)],
        compiler_params=_cparams("parallel"),
    )(p, p, p, dm, b_ssd, b_gmlp, b_mem)


def _local_step(x, mem, target, w, rest_weights, push):
    t = x.shape[0]
    mm = functools.partial(_matmul, tk=1024)

    h1, ffn1_saved = _ffn_forward(x, w["ffn1_norm"], w["ffn1_w_gate"], w["ffn1_w_up"], w["ffn1_w_down"], "ffn1")
    w = {**w, **rest_weights(h1)}
    n2 = _rms_fwd(h1, w["mix_norm"], "mix_norm")
    p = mm(n2, w["w_in_p"], mode="nt", out_dtype=BF16, tm=512, tn=1536, name="in_proj")
    dt_raw = mm(n2, w["w_dt"], mode="nt", out_dtype=F32, tm=512, tn=128, name="dt_proj")
    dt_bias = jnp.pad(w["ssd_dt_bias"], (0, 128 - SSD_HEADS)).reshape(1, 128)
    hv = _head_inputs(_dt_fwd(dt_raw, dt_bias, "dt_fwd"), w["ssd_a_log"], w["ssd_d"])
    xc = _conv_fwd(p, w["ssd_conv_w"], w["ssd_conv_b"], "conv_fwd")
    y_ssd_raw, y_ssd, states = _ssd_fwd(xc, p, hv, w["ssd_norm"], "ssd_fwd")
    b_col = w["gmlp_b_s"].reshape(GMLP_GROUPS, CHUNK, 1)
    y_gmlp = _gmlp_fwd(p, w["gmlp_v_norm"], w["gmlp_w_s"], b_col, "gmlp_fwd")
    mem_n = _rms_fwd(mem, w["mem_norm"], "mem_norm")
    kv = mm(mem_n, w["w_mem_kv"], mode="nn", out_dtype=F32, tm=256, tn=512, name="mem_kv")
    y_mem = _mem_fwd(p, kv, "mem_fwd")
    b_ssd = mm(y_ssd, w["w_branch_ssd"], mode="nn", out_dtype=BF16, tm=512, tn=1024, name="branch_ssd")
    b_gmlp = mm(y_gmlp, w["w_branch_gmlp"], mode="nn", out_dtype=BF16, tm=512, tn=1024, name="branch_gmlp")
    b_mem = mm(y_mem, w["w_branch_mem"], mode="nt", out_dtype=BF16, tm=512, tn=1024, tk=MEM_W, name="branch_mem")
    merged = _merge_fwd(p, b_ssd, b_gmlp, b_mem, "merge_fwd")
    h2 = mm(merged, w["w_out"], mode="nn", out_dtype=F32, tm=512, tn=1024, addend=h1, name="out_proj")
    h3, ffn2_saved = _ffn_forward(h2, w["ffn2_norm"], w["ffn2_w_gate"], w["ffn2_w_up"], w["ffn2_w_down"], "ffn2")
    dh3, d_final, loss = _loss_head(h3, w["final_norm"], target, "loss_head")

    g = {"final_norm": d_final}
    big = {}
    dh2, g["ffn2_norm"] = _ffn_backward(dh3, h2, w["ffn2_norm"], w["ffn2_w_gate"], w["ffn2_w_up"], w["ffn2_w_down"], ffn2_saved,
                                        "ffn2", functools.partial(push, 0))
    dmerged = mm(dh2, w["w_out"], mode="nt", out_dtype=BF16, tm=512, tn=1024, name="out_proj_dx")
    big["w_out"] = mm(merged, dh2, mode="tn", out_dtype=BF16, tm=1024, tn=1024, name="out_proj_dw")
    db_ssd, db_gmlp, db_mem, dgl = _merge_bwd(p, dmerged, b_ssd, b_gmlp, b_mem, "merge_bwd")
    dy_ssd = mm(db_ssd, w["w_branch_ssd"], mode="nt", out_dtype=BF16, tm=512, tn=2048, name="branch_ssd_dx")
    dy_gmlp = mm(db_gmlp, w["w_branch_gmlp"], mode="nt", out_dtype=BF16, tm=512, tn=1024, name="branch_gmlp_dx")
    dy_mem = mm(db_mem, w["w_branch_mem"], mode="nn", out_dtype=BF16, tm=512, tn=256, name="branch_mem_dx")
    big["w_branch_ssd"] = mm(y_ssd, db_ssd, mode="tn", out_dtype=BF16, tm=1024, tn=1024, name="branch_ssd_dw")
    big["w_branch_gmlp"] = mm(y_gmlp, db_gmlp, mode="tn", out_dtype=BF16, tm=1024, tn=1024, name="branch_gmlp_dw")
    big["w_branch_mem"] = mm(db_mem, y_mem, mode="tn", out_dtype=BF16, tm=1024, tn=256, name="branch_mem_dw")
    dq, dkv = _mem_bwd(p, kv, dy_mem, "mem_bwd")
    big["w_mem_kv"] = mm(mem_n, dkv, mode="tn", out_dtype=BF16, tm=1024, tn=512, tk=256, name="mem_kv_dw")
    dmem_n = mm(dkv, w["w_mem_kv"], mode="nt", out_dtype=F32, tm=256, tn=1024, tk=512, name="mem_kv_dx")
    _, g["mem_norm"] = _rms_bwd(mem, w["mem_norm"], dmem_n, None, "mem_norm_bwd")
    duv, d_ws, d_bs, g["gmlp_v_norm"] = _gmlp_bwd(p, dy_gmlp, w["gmlp_v_norm"], w["gmlp_w_s"], b_col, "gmlp_bwd")
    g["gmlp_w_s"] = d_ws
    g["gmlp_b_s"] = d_bs.reshape(GMLP_GROUPS, CHUNK)
    dxs, d_bm, d_cm, dz, ddt_w, hsums, g["ssd_norm"] = _ssd_bwd(xc, p, hv, w["ssd_norm"], y_ssd_raw, dy_ssd, states, "ssd_bwd")
    heads = hsums[:, :, :HEADS_PER_GROUP]
    g["ssd_a_log"] = heads[:, 1, :].reshape(1, SSD_HEADS)
    g["ssd_d"] = heads[:, 2, :].reshape(1, SSD_HEADS)
    ddt = jnp.transpose(ddt_w[:, :, :HEADS_PER_GROUP], (1, 0, 2)).reshape(t, SSD_HEADS)
    ddt, d_bias = _dt_bwd(jnp.pad(ddt, ((0, 0), (0, 128 - SSD_HEADS))), dt_raw, dt_bias, "dt_bwd")
    g["ssd_dt_bias"] = d_bias[:, :SSD_HEADS]
    parts, dws, dbs = [], [], []
    for dyc, col0, tag in ((dxs, 0, "x"), (d_bm, SSD_INNER, "b"), (d_cm, SSD_INNER + SSD_GROUPS * SSD_STATE, "c")):
        dacc, dw_c, db_c = _conv_bwd_act(p, dyc, w["ssd_conv_w"], w["ssd_conv_b"], col0, f"conv_bwd_act_{tag}")
        parts.append(_conv_bwd_dx(dacc, w["ssd_conv_w"], col0, f"conv_bwd_dx_{tag}"))
        dws.append(dw_c)
        dbs.append(db_c)
    g["ssd_conv_w"] = jnp.concatenate(dws, axis=1)
    g["ssd_conv_b"] = jnp.concatenate(dbs, axis=1)
    dp = jnp.concatenate([dz, duv] + parts + [dgl, dq, jnp.zeros((t, P_W - P_USED), BF16)], axis=1)
    d_win_p = _matmul(dp, n2, mode="tn", out_dtype=BF16, tm=1536, tn=1024, tk=2048, name="in_proj_dw")
    d_wdt = mm(ddt, n2, mode="tn", out_dtype=BF16, tm=128, tn=1024, name="dt_proj_dw")
    sl = lambda a, o, n: a[o:o + n]
    big["w_in"] = jnp.concatenate([sl(d_win_p, P_Z, 2048), sl(d_win_p, P_XBC, XBC), d_wdt[:SSD_HEADS], sl(d_win_p, P_UV, 2048),
                                   sl(d_win_p, P_Q, MEM_W), sl(d_win_p, P_GL, 3 * D_MODEL)], axis=0)
    token = push(1, big)
    dn2 = mm(dp, w["w_in_p"], mode="nn", out_dtype=F32, tm=512, tn=1024, tk=3584, name="in_proj_dx")
    dn2 = _matmul(ddt, w["w_dt"], mode="nn", out_dtype=F32, tm=512, tn=1024, tk=128, addend=dn2, name="dt_proj_dx")
    dh1, g["mix_norm"] = _rms_bwd(h1, w["mix_norm"] + token, dn2, dh2, "mix_norm_bwd")
    dx, g["ffn1_norm"] = _ffn_backward(dh1, x, w["ffn1_norm"], w["ffn1_w_gate"], w["ffn1_w_up"], w["ffn1_w_down"], ffn1_saved,
                                       "ffn1", functools.partial(push, 2))
    return loss, dx, g


def _split_w_in(w_in_t):
    sl = lambda o, n: w_in_t[o:o + n]
    w_p = jnp.concatenate([sl(IN_Z, 2048), sl(IN_UV, 2048), sl(IN_XBC, XBC), sl(IN_GL, 3 * D_MODEL), sl(IN_Q, MEM_W),
                           jnp.zeros((P_W - P_USED, D_MODEL), w_in_t.dtype)], axis=0)
    w_dt = jnp.pad(sl(IN_DT, SSD_HEADS), ((0, 128 - SSD_HEADS), (0, 0)))
    return w_p, w_dt


def _pick_tile(rows, cap=512):
    best = None
    for tile in range(8, min(rows, cap) + 1, 8):
        if rows % tile == 0:
            best = tile
    return best if best is not None else rows


def _adamw(w, g, m, v, name):
    rows, lanes = w.shape
    tile = _pick_tile(rows, cap=max(8, (512 * 1024 // lanes) // 8 * 8))
    c1 = 1.0 / (1.0 - ADAM_B1 ** ADAM_STEP)
    c2 = 1.0 / (1.0 - ADAM_B2 ** ADAM_STEP)

    def body(w_ref, g_ref, m_ref, v_ref, d_ref, nm_ref, nv_ref):
        gv = g_ref[...]
        nm = ADAM_B1 * m_ref[...] + (1.0 - ADAM_B1) * gv
        nv = ADAM_B2 * v_ref[...] + (1.0 - ADAM_B2) * (gv * gv)
        nm_ref[...] = nm
        nv_ref[...] = nv
        d_ref[...] = -ADAM_LR * ((nm * c1) / (jnp.sqrt(nv * c2) + ADAM_EPS) + ADAM_WD * w_ref[...])

    blk = pl.BlockSpec((tile, lanes), lambda i: (i, 0))
    return pl.pallas_call(
        body, name=name, grid=(rows // tile,), in_specs=[blk] * 4, out_specs=[blk] * 3,
        out_shape=[jax.ShapeDtypeStruct((rows, lanes), F32)] * 3, compiler_params=_cparams("parallel"),
    )(w, g, m, v)


HBM = pl.BlockSpec(memory_space=pltpu.HBM)


def _place():
    x, y, c = lax.axis_index("x"), lax.axis_index("y"), lax.axis_index("c")
    chips = [(1 - x, y), (x, 1 - y), (1 - x, 1 - y)]
    return x, y, c, chips


def _gather_weights(slots):
    _, _, rh, lanes = slots.shape

    def body(in_ref, out_ref, send_sems, recv_sems):
        del in_ref
        x, y, c, chips = _place()
        me, sibling = (x, y, c), (x, y, 1 - c)

        def copy(k, src, dst, to):
            return pltpu.make_async_remote_copy(src_ref=src, dst_ref=dst, send_sem=send_sems.at[k], recv_sem=recv_sems.at[k],
                                                device_id=to, device_id_type=MESH)

        own = out_ref.at[2 * x + y, c]
        first = [copy(j, own, own, (*chip, c)) for j, chip in enumerate(chips)]
        for cp in first:
            cp.start()
        passed = []
        for j, (cx, cy) in enumerate(chips):
            landed = out_ref.at[2 * cx + cy, c]
            copy(j, landed, landed, me).wait_recv()
            fwd = copy(3 + j, landed, landed, sibling)
            fwd.start()
            passed.append(fwd)
        for j, (cx, cy) in enumerate(chips):
            other = out_ref.at[2 * cx + cy, 1 - c]
            copy(3 + j, other, other, me).wait_recv()
        for cp in first + passed:
            cp.wait_send()

    return pl.pallas_call(
        body, name="gather_weights", out_shape=jax.ShapeDtypeStruct(slots.shape, slots.dtype),
        in_specs=[HBM], out_specs=HBM, input_output_aliases={0: 0},
        scratch_shapes=[pltpu.SemaphoreType.DMA((6,)), pltpu.SemaphoreType.DMA((6,))],
    )(slots)


SEM = pl.BlockSpec(memory_space=pltpu.SEMAPHORE)
EFFECT = pltpu.SideEffectType.DATAFLOW_SIDE_EFFECTING
N_PEER = 3


def _sem_outs():
    return tuple(pltpu.SemaphoreType.DMA(()) for _ in range(2 * N_PEER))


def _gather_start(slots):
    def body(in_ref, *refs):
        del in_ref
        sems, thru, token = refs[:2 * N_PEER], refs[2 * N_PEER], refs[2 * N_PEER + 1]
        x, y, c, chips = _place()
        own = thru.at[2 * x + y, c]
        for j, chip in enumerate(chips):
            pltpu.make_async_remote_copy(src_ref=own, dst_ref=own, send_sem=sems[j], recv_sem=sems[N_PEER + j],
                                         device_id=(*chip, c), device_id_type=MESH).start()
        token[...] = jnp.zeros_like(token)

    out = pl.pallas_call(
        body, name="gather_rest_start",
        out_shape=_sem_outs() + (pltpu.HBM(slots.shape, slots.dtype), jax.ShapeDtypeStruct((8, 128), F32)),
        in_specs=(HBM,), out_specs=(SEM,) * (2 * N_PEER) + (HBM, pl.BlockSpec(memory_space=pltpu.VMEM)),
        input_output_aliases={0: 2 * N_PEER}, compiler_params=pltpu.CompilerParams(has_side_effects=EFFECT),
    )(pltpu.with_memory_space_constraint(slots, pltpu.HBM))
    return out[:2 * N_PEER], out[2 * N_PEER], out[2 * N_PEER + 1]


def _gather_wait(sems, thru, after):
    def body(in_ref, *refs):
        del in_ref
        sems, out_ref = refs[:2 * N_PEER], refs[2 * N_PEER + 1]
        x, y, c, chips = _place()
        own = out_ref.at[2 * x + y, c]
        for j, (cx, cy) in enumerate(chips):
            cp = pltpu.make_async_remote_copy(src_ref=own, dst_ref=out_ref.at[2 * cx + cy, c], send_sem=sems[j],
                                              recv_sem=sems[N_PEER + j], device_id=(cx, cy, c), device_id_type=MESH)
            cp.wait_send()
            cp.wait_recv()

    return pl.pallas_call(
        body, name="gather_rest_wait", out_shape=pltpu.HBM(thru.shape, thru.dtype),
        in_specs=(HBM,) + (SEM,) * (2 * N_PEER) + (pl.BlockSpec(memory_space=pl.ANY),), out_specs=HBM,
        input_output_aliases={0: 0}, compiler_params=pltpu.CompilerParams(has_side_effects=EFFECT),
    )(thru, *sems, after)


def _gather_forward(slots):
    def body(in_ref, out_ref, send_sems, recv_sems):
        del in_ref
        x, y, c, chips = _place()
        cps = []
        for j, (cx, cy) in enumerate(chips):
            landed = out_ref.at[2 * cx + cy, c]
            cps.append(pltpu.make_async_remote_copy(src_ref=landed, dst_ref=landed, send_sem=send_sems.at[j], recv_sem=recv_sems.at[j],
                                                    device_id=(x, y, 1 - c), device_id_type=MESH))
        for cp in cps:
            cp.start()
        for j, (cx, cy) in enumerate(chips):
            other = out_ref.at[2 * cx + cy, 1 - c]
            pltpu.make_async_remote_copy(src_ref=other, dst_ref=other, send_sem=send_sems.at[j], recv_sem=recv_sems.at[j],
                                         device_id=(x, y, 1 - c), device_id_type=MESH).wait_recv()
        for cp in cps:
            cp.wait_send()

    return pl.pallas_call(
        body, name="gather_rest_forward", out_shape=jax.ShapeDtypeStruct(slots.shape, slots.dtype),
        in_specs=[HBM], out_specs=HBM, input_output_aliases={0: 0},
        scratch_shapes=[pltpu.SemaphoreType.DMA((N_PEER,)), pltpu.SemaphoreType.DMA((N_PEER,))],
    )(slots)


def _scatter_start(pa, tag):
    ns, rh, lanes = pa.shape
    land = pltpu.with_memory_space_constraint(lax.empty((N_PEER, rh, lanes), pa.dtype), pltpu.HBM)

    def body(pa_ref, land_ref, *refs):
        x, y, c, chips = _place()
        for j, (cx, cy) in enumerate(chips):
            pltpu.make_async_remote_copy(src_ref=pa_ref.at[2 * cx + cy], dst_ref=land_ref.at[j], send_sem=refs[j],
                                         recv_sem=refs[N_PEER + j], device_id=(cx, cy, c), device_id_type=MESH).start()
        refs[-1][...] = jnp.zeros_like(refs[-1])

    out = pl.pallas_call(
        body, name=f"scatter_start_{tag}",
        out_shape=_sem_outs() + (pltpu.HBM(pa.shape, pa.dtype), pltpu.HBM(land.shape, land.dtype), jax.ShapeDtypeStruct((8, 128), F32)),
        in_specs=(HBM, HBM), out_specs=(SEM,) * (2 * N_PEER) + (HBM, HBM, pl.BlockSpec(memory_space=pltpu.VMEM)),
        input_output_aliases={0: 2 * N_PEER, 1: 2 * N_PEER + 1}, compiler_params=pltpu.CompilerParams(has_side_effects=EFFECT),
    )(pltpu.with_memory_space_constraint(pa, pltpu.HBM), land)
    return (out[:2 * N_PEER], out[2 * N_PEER], out[2 * N_PEER + 1]), out[2 * N_PEER + 2]


def _scatter_wait(sems, pa_thru, land_thru, after, tag):
    def body(pa_ref, land_ref, *refs):
        sems = refs[:2 * N_PEER]
        x, y, c, chips = _place()
        for j, (cx, cy) in enumerate(chips):
            cp = pltpu.make_async_remote_copy(src_ref=pa_ref.at[2 * cx + cy], dst_ref=land_ref.at[j], send_sem=sems[j],
                                              recv_sem=sems[N_PEER + j], device_id=(cx, cy, c), device_id_type=MESH)
            cp.wait_send()
            cp.wait_recv()

    return pl.pallas_call(
        body, name=f"scatter_wait_{tag}",
        out_shape=(pltpu.HBM(pa_thru.shape, pa_thru.dtype), pltpu.HBM(land_thru.shape, land_thru.dtype)),
        in_specs=(HBM, HBM) + (SEM,) * (2 * N_PEER) + (pl.BlockSpec(memory_space=pl.ANY),), out_specs=(HBM, HBM),
        input_output_aliases={0: 0, 1: 1}, compiler_params=pltpu.CompilerParams(has_side_effects=EFFECT),
    )(pa_thru, land_thru, *sems, after)


def _rs_swap(gp, tag):
    _, ns, rh, lanes = gp.shape

    def body(in_ref, out_ref, send_sem, recv_sem):
        x, y, c, _ = _place()
        cp = pltpu.make_async_remote_copy(src_ref=in_ref.at[1 - c], dst_ref=out_ref, send_sem=send_sem, recv_sem=recv_sem,
                                          device_id=(x, y, 1 - c), device_id_type=MESH)
        cp.start()
        cp.wait_send()
        cp.wait_recv()

    return pl.pallas_call(
        body, name=f"rs_swap_{tag}", out_shape=jax.ShapeDtypeStruct((ns, rh, lanes), gp.dtype), in_specs=[HBM], out_specs=HBM,
        scratch_shapes=[pltpu.SemaphoreType.DMA, pltpu.SemaphoreType.DMA],
    )(gp)


def _rs_tile(rh):
    return _pick_tile(rh, cap=512)


def _rs_add(gp, recv, c, tag):
    _, ns, rh, lanes = gp.shape
    tile = _rs_tile(rh)

    def body(c_ref, a_ref, b_ref, o_ref):
        o_ref[...] = (a_ref[...].astype(F32) + b_ref[...].astype(F32)).astype(o_ref.dtype)

    return pl.pallas_call(
        body, name=f"rs_add_{tag}", out_shape=jax.ShapeDtypeStruct((ns, rh, lanes), gp.dtype),
        grid_spec=pltpu.PrefetchScalarGridSpec(
            num_scalar_prefetch=1, grid=(ns, rh // tile),
            in_specs=[pl.BlockSpec((None, None, tile, lanes), lambda s, i, c_ref: (c_ref[0], s, i, 0)),
                      pl.BlockSpec((None, tile, lanes), lambda s, i, c_ref: (s, i, 0))],
            out_specs=pl.BlockSpec((None, tile, lanes), lambda s, i, c_ref: (s, i, 0))),
        compiler_params=_cparams("parallel", "parallel"),
    )(c, gp, recv)


def _rs_sum(pa, recv, place, tag):
    ns, rh, lanes = pa.shape
    tile = _rs_tile(rh)

    def body(place_ref, a_ref, r_ref, o_ref):
        acc = a_ref[...].astype(F32)
        for j in range(ns - 1):
            acc = acc + r_ref[j].astype(F32)
        o_ref[...] = acc

    return pl.pallas_call(
        body, name=f"rs_sum_{tag}", out_shape=jax.ShapeDtypeStruct((2, rh, lanes), F32),
        grid_spec=pltpu.PrefetchScalarGridSpec(
            num_scalar_prefetch=1, grid=(rh // tile,),
            in_specs=[pl.BlockSpec((None, tile, lanes), lambda i, place_ref: (place_ref[0], i, 0)),
                      pl.BlockSpec((ns - 1, tile, lanes), lambda i, place_ref: (0, i, 0))],
            out_specs=pl.BlockSpec((None, tile, lanes), lambda i, place_ref: (place_ref[1], i, 0))),
        compiler_params=_cparams("parallel"),
    )(place, pa, recv)


def _rs_share(halves, tag):
    def body(in_ref, out_ref, send_sem, recv_sem):
        del in_ref
        x, y, c, _ = _place()
        cp = pltpu.make_async_remote_copy(src_ref=out_ref.at[c], dst_ref=out_ref.at[c], send_sem=send_sem, recv_sem=recv_sem,
                                          device_id=(x, y, 1 - c), device_id_type=MESH)
        cp.start()
        other = out_ref.at[1 - c]
        pltpu.make_async_remote_copy(src_ref=other, dst_ref=other, send_sem=send_sem, recv_sem=recv_sem,
                                     device_id=(x, y, 1 - c), device_id_type=MESH).wait_recv()
        cp.wait_send()

    return pl.pallas_call(
        body, name=f"rs_share_{tag}", out_shape=jax.ShapeDtypeStruct(halves.shape, halves.dtype), in_specs=[HBM], out_specs=HBM,
        input_output_aliases={0: 0}, scratch_shapes=[pltpu.SemaphoreType.DMA, pltpu.SemaphoreType.DMA],
    )(halves)


N_DEV = 8
SMALL_ROWS = 160


def _allreduce_small(v):
    m_per, n = v.shape

    def body(x_ref, out_ref, all_ref, send_sems, recv_sems, local_sem):
        x, y, c, chips = _place()
        me, sibling = (x, y, c), (x, y, 1 - c)

        def rows(px, py, pc):
            return all_ref.at[pl.ds((4 * px + 2 * py + pc) * m_per, m_per), :]

        def copy(k, block, to, src=None):
            return pltpu.make_async_remote_copy(src_ref=rows(*block) if src is None else src, dst_ref=rows(*block),
                                                send_sem=send_sems.at[k], recv_sem=recv_sems.at[k], device_id=to, device_id_type=MESH)

        mine = pltpu.make_async_copy(x_ref, rows(*me), local_sem)
        mine.start()
        first = [copy(0, me, sibling, src=x_ref)]
        first += [copy(1 + j, me, (*chip, c), src=x_ref) for j, chip in enumerate(chips)]
        for cp in first:
            cp.start()
        passed = [copy(4 + j, (*chip, c), sibling) for j, chip in enumerate(chips)]
        for j, chip in enumerate(chips):
            copy(1 + j, (*chip, c), me).wait_recv()
            passed[j].start()
        copy(0, sibling, me).wait_recv()
        for j, chip in enumerate(chips):
            copy(4 + j, (*chip, 1 - c), me).wait_recv()
        for cp in first + passed:
            cp.wait_send()
        mine.wait()
        step = 32
        for r in range(0, m_per, step):
            acc = all_ref[r:r + step, :]
            for d in range(1, N_DEV):
                acc = acc + all_ref[d * m_per + r:d * m_per + r + step, :]
            out_ref[r:r + step, :] = acc

    vm = pl.BlockSpec(memory_space=pltpu.VMEM)
    return pl.pallas_call(
        body, name="allreduce_small", out_shape=jax.ShapeDtypeStruct((m_per, n), v.dtype), in_specs=[vm], out_specs=vm,
        scratch_shapes=[pltpu.VMEM((N_DEV * m_per, n), v.dtype), pltpu.SemaphoreType.DMA((7,)), pltpu.SemaphoreType.DMA((7,)),
                        pltpu.SemaphoreType.DMA],
        compiler_params=pltpu.CompilerParams(vmem_limit_bytes=V7X_VMEM_LIMIT),
    )(v)


BIG = {"ffn1_w_gate": ((D_MODEL, D_FF), 1), "ffn1_w_up": ((D_MODEL, D_FF), 1), "ffn1_w_down": ((D_FF, D_MODEL), 0),
       "ffn2_w_gate": ((D_MODEL, D_FF), 1), "ffn2_w_up": ((D_MODEL, D_FF), 1), "ffn2_w_down": ((D_FF, D_MODEL), 0),
       "w_in": ((D_MODEL, IN_WIDTH), 1), "w_mem_kv": ((D_MODEL, 2 * MEM_W), 0), "w_branch_ssd": ((SSD_INNER, D_MODEL), 0),
       "w_branch_gmlp": ((GMLP_W, D_MODEL), 0), "w_branch_mem": ((MEM_W, D_MODEL), 1), "w_out": ((D_MODEL, D_MODEL), 0)}
FFN1 = ("ffn1_w_gate", "ffn1_w_up", "ffn1_w_down")
FFN2 = ("ffn2_w_gate", "ffn2_w_up", "ffn2_w_down")
MIXER = ("w_out", "w_branch_ssd", "w_branch_gmlp", "w_branch_mem", "w_mem_kv", "w_in")
GATHER_GROUPS = (FFN1, FFN2 + MIXER)
REDUCE_GROUPS = (FFN2, MIXER, FFN1)
CONV_W_ROWS = 8


def _shard_rows_of(name):
    (a, b), _ = BIG[name]
    return a * b // N_SHARD // LANES


def _group_rows(names, extra=0):
    return -(-(sum(_shard_rows_of(n) for n in names) + extra) // 32) * 32

SMALL = [("ffn1_norm", 1), ("mix_norm", 1), ("mem_norm", 1), ("ssd_conv_b", 3), ("heads", 1), ("ssd_norm", 2),
         ("gmlp_v_norm", 1), ("gmlp_w_s", 128), ("gmlp_b_s", 1), ("ffn2_norm", 1), ("final_norm", 1), ("ssd_conv_w", 12)]
assert sum(n for _, n in SMALL) <= SMALL_ROWS
HEAD_VECS = ("ssd_dt_bias", "ssd_a_log", "ssd_d")


def _pack_small(vals, loss=None):
    parts = []
    for name, nrows in SMALL:
        if name == "heads":
            row = jnp.concatenate([vals[k].reshape(-1) for k in HEAD_VECS]
                                  + [jnp.zeros((1,), F32) if loss is None else loss.reshape(1)])
            parts.append(jnp.pad(row, (0, LANES - row.shape[0])).reshape(1, LANES))
        elif name in vals:
            parts.append(vals[name].reshape(nrows, LANES))
        else:
            parts.append(jnp.zeros((nrows, LANES), F32))
    buf = jnp.concatenate(parts, axis=0)
    return jnp.pad(buf, ((0, SMALL_ROWS - buf.shape[0]), (0, 0)))


def _unpack_small(buf):
    out, r = {}, 0
    for name, nrows in SMALL:
        blk = buf[r:r + nrows]
        r += nrows
        if name == "heads":
            for i, k in enumerate(HEAD_VECS):
                out[k] = blk[0, i * SSD_HEADS:(i + 1) * SSD_HEADS]
            out["loss"] = blk[0, 3 * SSD_HEADS]
        else:
            out[name] = blk
    return out


def _wire_shape(name):
    (a, b), axis = BIG[name]
    return (b, a) if axis == 1 else (a, b)


def _pack_weights(given, names, conv=False):
    parts = [(given[n][0].T if BIG[n][1] == 1 else given[n][0]).astype(BF16).reshape(_shard_rows_of(n), LANES) for n in names]
    if conv:
        pairs = lax.bitcast_convert_type(given["ssd_conv_w"], BF16).reshape(-1)
        parts.append(jnp.pad(pairs, (0, CONV_W_ROWS * LANES - pairs.shape[0])).reshape(CONV_W_ROWS, LANES))
    total = _group_rows(names, CONV_W_ROWS if conv else 0)
    packed = jnp.concatenate(parts, axis=0)
    packed = jnp.pad(packed, ((0, total - packed.shape[0]), (0, 0))).reshape(1, 2, total // 2, LANES)
    return jnp.broadcast_to(packed, (N_SHARD, 2, total // 2, LANES))


def _unpack_weights(slots, names, conv=False):
    rows = slots.reshape(N_SHARD, -1, LANES)
    out, r = {}, 0
    for name in names:
        n = _shard_rows_of(name)
        out[name] = rows[:, r:r + n].reshape(_wire_shape(name))
        r += n
    if conv:
        cols = XBC // N_SHARD
        pairs = rows[:, r:r + CONV_W_ROWS].reshape(N_SHARD, -1)[:, :SSD_CONV * cols * 2].reshape(N_SHARD, SSD_CONV, cols, 2)
        out["ssd_conv_w"] = jnp.transpose(lax.bitcast_convert_type(pairs, F32), (1, 0, 2)).reshape(SSD_CONV, XBC)
    return out


def _pack_grads(grads, names):
    total = _group_rows(names)
    gp = jnp.concatenate([grads[n].astype(BF16).reshape(N_SHARD, _shard_rows_of(n), LANES) for n in names], axis=1)
    gp = jnp.pad(gp, ((0, 0), (0, total - gp.shape[1]), (0, 0))).reshape(N_SHARD, 2, total // 2, LANES)
    return jnp.transpose(gp, (1, 0, 2, 3))


def kernel(x, mem, ffn1_norm, ffn1_w_gate, ffn1_w_up, ffn1_w_down, mix_norm, mem_norm, w_in, ssd_conv_w, ssd_conv_b, ssd_dt_bias, ssd_a_log, ssd_d, ssd_norm, gmlp_v_norm, gmlp_w_s, gmlp_b_s, w_mem_kv, w_branch_ssd, w_branch_gmlp, w_branch_mem, w_out, ffn2_norm, ffn2_w_gate, ffn2_w_up, ffn2_w_down, final_norm, loss_target, m_ffn1_norm, m_ffn1_w_gate, m_ffn1_w_up, m_ffn1_w_down, m_mix_norm, m_mem_norm, m_w_in, m_ssd_conv_w, m_ssd_conv_b, m_ssd_dt_bias, m_ssd_a_log, m_ssd_d, m_ssd_norm, m_gmlp_v_norm, m_gmlp_w_s, m_gmlp_b_s, m_w_mem_kv, m_w_branch_ssd, m_w_branch_gmlp, m_w_branch_mem, m_w_out, m_ffn2_norm, m_ffn2_w_gate, m_ffn2_w_up, m_ffn2_w_down, m_final_norm, v_ffn1_norm, v_ffn1_w_gate, v_ffn1_w_up, v_ffn1_w_down, v_mix_norm, v_mem_norm, v_w_in, v_ssd_conv_w, v_ssd_conv_b, v_ssd_dt_bias, v_ssd_a_log, v_ssd_d, v_ssd_norm, v_gmlp_v_norm, v_gmlp_w_s, v_gmlp_b_s, v_w_mem_kv, v_w_branch_ssd, v_w_branch_gmlp, v_w_branch_mem, v_w_out, v_ffn2_norm, v_ffn2_w_gate, v_ffn2_w_up, v_ffn2_w_down, v_final_norm):
    given = dict(x=x, mem=mem, ffn1_norm=ffn1_norm, ffn1_w_gate=ffn1_w_gate, ffn1_w_up=ffn1_w_up, ffn1_w_down=ffn1_w_down, mix_norm=mix_norm, mem_norm=mem_norm, w_in=w_in, ssd_conv_w=ssd_conv_w, ssd_conv_b=ssd_conv_b, ssd_dt_bias=ssd_dt_bias, ssd_a_log=ssd_a_log, ssd_d=ssd_d, ssd_norm=ssd_norm, gmlp_v_norm=gmlp_v_norm, gmlp_w_s=gmlp_w_s, gmlp_b_s=gmlp_b_s, w_mem_kv=w_mem_kv, w_branch_ssd=w_branch_ssd, w_branch_gmlp=w_branch_gmlp, w_branch_mem=w_branch_mem, w_out=w_out, ffn2_norm=ffn2_norm, ffn2_w_gate=ffn2_w_gate, ffn2_w_up=ffn2_w_up, ffn2_w_down=ffn2_w_down, final_norm=final_norm, loss_target=loss_target, m_ffn1_norm=m_ffn1_norm, m_ffn1_w_gate=m_ffn1_w_gate, m_ffn1_w_up=m_ffn1_w_up, m_ffn1_w_down=m_ffn1_w_down, m_mix_norm=m_mix_norm, m_mem_norm=m_mem_norm, m_w_in=m_w_in, m_ssd_conv_w=m_ssd_conv_w, m_ssd_conv_b=m_ssd_conv_b, m_ssd_dt_bias=m_ssd_dt_bias, m_ssd_a_log=m_ssd_a_log, m_ssd_d=m_ssd_d, m_ssd_norm=m_ssd_norm, m_gmlp_v_norm=m_gmlp_v_norm, m_gmlp_w_s=m_gmlp_w_s, m_gmlp_b_s=m_gmlp_b_s, m_w_mem_kv=m_w_mem_kv, m_w_branch_ssd=m_w_branch_ssd, m_w_branch_gmlp=m_w_branch_gmlp, m_w_branch_mem=m_w_branch_mem, m_w_out=m_w_out, m_ffn2_norm=m_ffn2_norm, m_ffn2_w_gate=m_ffn2_w_gate, m_ffn2_w_up=m_ffn2_w_up, m_ffn2_w_down=m_ffn2_w_down, m_final_norm=m_final_norm, v_ffn1_norm=v_ffn1_norm, v_ffn1_w_gate=v_ffn1_w_gate, v_ffn1_w_up=v_ffn1_w_up, v_ffn1_w_down=v_ffn1_w_down, v_mix_norm=v_mix_norm, v_mem_norm=v_mem_norm, v_w_in=v_w_in, v_ssd_conv_w=v_ssd_conv_w, v_ssd_conv_b=v_ssd_conv_b, v_ssd_dt_bias=v_ssd_dt_bias, v_ssd_a_log=v_ssd_a_log, v_ssd_d=v_ssd_d, v_ssd_norm=v_ssd_norm, v_gmlp_v_norm=v_gmlp_v_norm, v_gmlp_w_s=v_gmlp_w_s, v_gmlp_b_s=v_gmlp_b_s, v_w_mem_kv=v_w_mem_kv, v_w_branch_ssd=v_w_branch_ssd, v_w_branch_gmlp=v_w_branch_gmlp, v_w_branch_mem=v_w_branch_mem, v_w_out=v_w_out, v_ffn2_norm=v_ffn2_norm, v_ffn2_w_gate=v_ffn2_w_gate, v_ffn2_w_up=v_ffn2_w_up, v_ffn2_w_down=v_ffn2_w_down, v_final_norm=v_final_norm)
    weights = [n for n in given if n not in ("x", "mem", "loss_target") and not n.startswith(("m_", "v_"))]
    xi, yi, ci = lax.axis_index("x"), lax.axis_index("y"), lax.axis_index("c")
    chip = (2 * xi + yi).astype(jnp.int32)
    core = ci.astype(jnp.int32)
    conv_cols = XBC // N_SHARD

    first = _gather_weights(_pack_weights(given, FFN1))
    first, rest_slots = lax.optimization_barrier((first, _pack_weights(given, GATHER_GROUPS[1], conv=True)))
    sems, thru, token = _gather_start(rest_slots)
    w = _unpack_weights(first, FFN1)
    for name in ("ffn1_norm", "mix_norm", "mem_norm", "ssd_conv_b", "ssd_norm", "gmlp_v_norm", "ffn2_norm", "final_norm"):
        w[name] = given[name].reshape(1, -1)
    w["ffn1_norm"] = w["ffn1_norm"] + token[0:1, 0:1]
    for name in HEAD_VECS:
        w[name] = given[name].reshape(-1)
    w["gmlp_w_s"] = given["gmlp_w_s"][0]
    w["gmlp_b_s"] = given["gmlp_b_s"][0]

    def rest_weights(after):
        rest = _unpack_weights(_gather_forward(_gather_wait(sems, thru, after)), GATHER_GROUPS[1], conv=True)
        rest["w_in_p"], rest["w_dt"] = _split_w_in(rest.pop("w_in"))
        return rest

    pending = {}

    def push(k, group_grads):
        gp = _pack_grads(group_grads, REDUCE_GROUPS[k])
        pa = _rs_add(gp, _rs_swap(gp, k), core.reshape(1), k)
        pending[k], token = _scatter_start(pa, k)
        return token[0:1, 0:1]

    def reduced(k, after):
        pa, land = _scatter_wait(*pending[k], after, k)
        gsum = _rs_share(_rs_sum(pa, land, jnp.stack([chip, core]), k), k)
        rows = gsum.reshape(-1, LANES)
        out, r = {}, 0
        for name in REDUCE_GROUPS[k]:
            n = _shard_rows_of(name)
            a, b = given[name].shape[1:]
            out[name] = rows[r:r + n].reshape(b, a).T if BIG[name][1] == 1 else rows[r:r + n].reshape(a, b)
            r += n
        return out

    loss_part, grad_x, g = _local_step(x[0], mem[0], loss_target[0], w, rest_weights, push)

    grads, deltas, new_m, new_v = {}, {}, {}, {}

    def update(k, after):
        for name, gl in reduced(k, after).items():
            d, nm, nv = _adamw(given[name][0], gl, given["m_" + name][0], given["v_" + name][0], f"adamw_{name}")
            grads[name], deltas[name], new_m[name], new_v[name] = (a[None] for a in (gl, d, nm, nv))

    update(0, grad_x)
    update(1, deltas[REDUCE_GROUPS[0][-1]])

    small_vals = {k: g[k] for k, _ in SMALL if k != "heads"}
    small_vals.update({k: g[k] for k in HEAD_VECS})
    red = _unpack_small(_allreduce_small(_pack_small(small_vals, loss=loss_part[0, 0])))
    update(2, deltas[REDUCE_GROUPS[1][-1]])
    conv_g = lax.dynamic_slice_in_dim(red["ssd_conv_w"].reshape(SSD_CONV, XBC), chip * conv_cols, conv_cols, axis=1)
    d, nm, nv = _adamw(given["ssd_conv_w"][0], conv_g, given["m_ssd_conv_w"][0], given["v_ssd_conv_w"][0], "adamw_conv_w")
    grads["ssd_conv_w"], deltas["ssd_conv_w"], new_m["ssd_conv_w"], new_v["ssd_conv_w"] = (a[None] for a in (conv_g, d, nm, nv))
    small_names = [k for k, _ in SMALL if k not in ("heads", "ssd_conv_w")] + list(HEAD_VECS)
    pack_of = lambda prefix: _pack_small({k: given[prefix + k] for k in small_names})
    g_small = _pack_small({k: red[k] for k in small_names})
    d, nm, nv = (_unpack_small(a) for a in _adamw(pack_of(""), g_small, pack_of("m_"), pack_of("v_"), "adamw_small"))
    for k in small_names:
        shape = given[k].shape
        grads[k], deltas[k], new_m[k], new_v[k] = (a[k].reshape(shape) for a in (red, d, nm, nv))

    return (red["loss"], grad_x[None], *[grads[n] for n in weights], *[deltas[n] for n in weights],
            *[new_m[n] for n in weights], *[new_v[n] for n in weights])
```

```python
import functools
import math

import jax
import jax.numpy as jnp
from jax import lax
from jax.experimental import pallas as pl
from jax.experimental.pallas import tpu as pltpu

F32, BF16 = jnp.float32, jnp.bfloat16
HI = lax.Precision.HIGHEST
MESH = pl.DeviceIdType.MESH

D_MODEL = 1024
D_FF = 2816
MEM_LEN = 256
SSD_INNER = 2048
SSD_HEADS = 32
SSD_GROUPS = 4
SSD_STATE = 128
SSD_CONV = 4
CHUNK = 128
XBC = SSD_INNER + 2 * SSD_GROUPS * SSD_STATE
GMLP_W = 1024
GMLP_GROUPS = 8
MEM_W = 256
MEM_HEADS = 4
EPS = 1e-6
IN_WIDTH = 10528
IN_Z, IN_XBC, IN_DT, IN_UV, IN_Q, IN_GL = 0, 2048, 5120, 5152, 7200, 7456
P_Z, P_UV, P_XBC, P_GL, P_Q, P_W = 0, 2048, 4096, 7168, 10240, 10752
P_USED = 10496

ADAM_LR, ADAM_B1, ADAM_B2, ADAM_EPS, ADAM_WD, ADAM_STEP = 0.001, 0.9, 0.999, 1e-08, 0.01, 10

V7X_VMEM_LIMIT = 56 * 1024 * 1024
N_SHARD = 4
LANES = 1024


def _cparams(*sem):
    return pltpu.CompilerParams(dimension_semantics=sem, vmem_limit_bytes=V7X_VMEM_LIMIT)


def _sigmoid(x):
    return 0.5 * jnp.tanh(0.5 * x) + 0.5


def _row_tile(t):
    return min(512, t)


_DIMS = {"nn": (((1,), (0,)), ((), ())), "nt": (((1,), (1,)), ((), ())), "tn": (((0,), (0,)), ((), ()))}


def _matmul(a, b, *, mode, out_dtype, tm, tn, tk, name, scale=1.0, addend=None):
    if mode == "tn":
        k_dim, m_dim = a.shape
    else:
        m_dim, k_dim = a.shape
    n_dim = b.shape[0] if mode == "nt" else b.shape[1]
    tm, tn, tk = min(tm, m_dim), min(tn, n_dim), min(tk, k_dim)
    assert m_dim % tm == 0 and n_dim % tn == 0 and k_dim % tk == 0, (name, a.shape, b.shape, tm, tn, tk)
    ni, nj, nk = m_dim // tm, n_dim // tn, k_dim // tk
    a_spec = pl.BlockSpec((tk, tm), lambda j, i, k: (k, i)) if mode == "tn" else pl.BlockSpec((tm, tk), lambda j, i, k: (i, k))
    b_spec = pl.BlockSpec((tn, tk), lambda j, i, k: (j, k)) if mode == "nt" else pl.BlockSpec((tk, tn), lambda j, i, k: (k, j))
    o_spec = pl.BlockSpec((tm, tn), lambda j, i, k: (i, j))
    dims = _DIMS[mode]
    has_add = addend is not None

    def body(*refs):
        a_ref, b_ref = refs[:2]
        r_ref = refs[2] if has_add else None
        o_ref = refs[2 + has_add]

        def finish(acc):
            r = acc * scale if scale != 1.0 else acc
            if has_add:
                r = r + r_ref[...].astype(F32)
            o_ref[...] = r.astype(o_ref.dtype)

        prod = lax.dot_general(a_ref[...].astype(BF16), b_ref[...].astype(BF16), dims, preferred_element_type=F32)
        if nk == 1:
            finish(prod)
            return
        acc_ref = refs[-1]
        k = pl.program_id(2)

        @pl.when(k == 0)
        def _():
            acc_ref[...] = prod

        @pl.when(k > 0)
        def _():
            acc_ref[...] += prod

        @pl.when(k == nk - 1)
        def _():
            finish(acc_ref[...])

    in_specs = [a_spec, b_spec] + ([o_spec] if has_add else [])
    args = (a, b) + ((addend,) if has_add else ())
    return pl.pallas_call(
        body, name=name, grid=(nj, ni, nk), in_specs=in_specs, out_specs=o_spec,
        out_shape=jax.ShapeDtypeStruct((m_dim, n_dim), out_dtype),
        scratch_shapes=[] if nk == 1 else [pltpu.VMEM((tm, tn), F32)],
        compiler_params=_cparams("parallel", "parallel", "arbitrary"),
    )(*args)


ROW_STRIP = 16


def _strips(tm, fn, init=None, rb=ROW_STRIP):
    def step(i, carry):
        return fn(pl.ds(pl.multiple_of(i * rb, rb), rb), carry)
    return lax.fori_loop(0, tm // rb, step, init, unroll=2)


def _rms_fwd(x, gain, name):
    t, d = x.shape
    tm = _row_tile(t)

    def body(x_ref, g_ref, o_ref):
        xv = x_ref[...]
        r = lax.rsqrt(jnp.mean(xv * xv, axis=-1, keepdims=True) + EPS)
        o_ref[...] = (xv * r * g_ref[...]).astype(o_ref.dtype)

    return pl.pallas_call(
        body, name=name, grid=(t // tm,),
        in_specs=[pl.BlockSpec((tm, d), lambda i: (i, 0)), pl.BlockSpec((1, d), lambda i: (0, 0))],
        out_specs=pl.BlockSpec((tm, d), lambda i: (i, 0)),
        out_shape=jax.ShapeDtypeStruct((t, d), BF16), compiler_params=_cparams("parallel"),
    )(x, gain)


def _rms_bwd(x, gain, dn, dres, name):
    t, d = x.shape
    tm = _row_tile(t)
    has_res = dres is not None

    def body(*refs):
        if has_res:
            x_ref, g_ref, dn_ref, r_ref, dx_ref, dg_ref = refs
        else:
            x_ref, g_ref, dn_ref, dx_ref, dg_ref = refs

        @pl.when(pl.program_id(0) == 0)
        def _():
            dg_ref[...] = jnp.zeros_like(dg_ref)

        xv = x_ref[...]
        r = lax.rsqrt(jnp.mean(xv * xv, axis=-1, keepdims=True) + EPS)
        xh = xv * r
        dnv = dn_ref[...].astype(F32)
        dg_ref[...] += jnp.sum(dnv * xh, axis=0, keepdims=True)
        dxh = dnv * g_ref[...]
        dx = r * (dxh - xh * jnp.mean(dxh * xh, axis=-1, keepdims=True))
        if has_res:
            dx = dx + r_ref[...]
        dx_ref[...] = dx

    row = pl.BlockSpec((tm, d), lambda i: (i, 0))
    vec = pl.BlockSpec((1, d), lambda i: (0, 0))
    in_specs = [row, vec, row] + ([row] if has_res else [])
    args = (x, gain, dn) + ((dres,) if has_res else ())
    return pl.pallas_call(
        body, name=name, grid=(t // tm,), in_specs=in_specs, out_specs=[row, vec],
        out_shape=[jax.ShapeDtypeStruct((t, d), F32), jax.ShapeDtypeStruct((1, d), F32)],
        compiler_params=_cparams("arbitrary"),
    )(*args)


def _loss_head(h, gain, target, name):
    t, d = h.shape
    tm = _row_tile(t)

    def body(h_ref, g_ref, t_ref, dh_ref, dg_ref, l_ref):
        @pl.when(pl.program_id(0) == 0)
        def _():
            dg_ref[...] = jnp.zeros_like(dg_ref)
            l_ref[...] = jnp.zeros_like(l_ref)

        xv = h_ref[...]
        g = g_ref[...]
        r = lax.rsqrt(jnp.mean(xv * xv, axis=-1, keepdims=True) + EPS)
        xh = xv * r
        err = xh * g - t_ref[...]
        l_ref[...] += 0.5 * jnp.sum(jnp.mean(err * err, axis=-1, keepdims=True), axis=0, keepdims=True)
        dy = err * (1.0 / d)
        dg_ref[...] += jnp.sum(dy * xh, axis=0, keepdims=True)
        dxh = dy * g
        dh_ref[...] = r * (dxh - xh * jnp.mean(dxh * xh, axis=-1, keepdims=True))

    row = pl.BlockSpec((tm, d), lambda i: (i, 0))
    vec = pl.BlockSpec((1, d), lambda i: (0, 0))
    return pl.pallas_call(
        body, name=name, grid=(t // tm,), in_specs=[row, vec, row],
        out_specs=[row, vec, pl.BlockSpec((1, 128), lambda i: (0, 0))],
        out_shape=[jax.ShapeDtypeStruct((t, d), F32), jax.ShapeDtypeStruct((1, d), F32), jax.ShapeDtypeStruct((1, 128), F32)],
        compiler_params=_cparams("arbitrary"),
    )(h, gain, target)


FF_TILE = 1408


def _ffn_fwd(n, x, wg, wu, wd, name):
    t, d = x.shape
    tm, tn = _row_tile(t), FF_TILE
    nj = D_FF // tn

    def body(n_ref, x_ref, wg_ref, wu_ref, wd_ref, h_ref, g_ref, u_ref, acc_ref):
        j = pl.program_id(1)

        @pl.when(j == 0)
        def _():
            acc_ref[...] = jnp.zeros_like(acc_ref)

        nb = n_ref[...]
        g = lax.dot_general(nb, wg_ref[...], _DIMS["nt"], preferred_element_type=F32)
        u = lax.dot_general(nb, wu_ref[...], _DIMS["nt"], preferred_element_type=F32)
        g_ref[...] = g.astype(BF16)
        u_ref[...] = u.astype(BF16)
        a = g * _sigmoid(g) * u
        acc_ref[...] += jnp.dot(a.astype(BF16), wd_ref[...], preferred_element_type=F32)

        @pl.when(j == nj - 1)
        def _():
            h_ref[...] = x_ref[...] + 0.5 * acc_ref[...]

    row = pl.BlockSpec((tm, d), lambda i, j: (i, 0))
    act = pl.BlockSpec((tm, tn), lambda i, j: (i, j))
    return pl.pallas_call(
        body, name=name, grid=(t // tm, nj),
        in_specs=[row, row] + [pl.BlockSpec((tn, d), lambda i, j: (j, 0))] * 3,
        out_specs=[row, act, act],
        out_shape=[jax.ShapeDtypeStruct((t, d), F32), jax.ShapeDtypeStruct((t, D_FF), BF16), jax.ShapeDtypeStruct((t, D_FF), BF16)],
        scratch_shapes=[pltpu.VMEM((tm, d), F32)], compiler_params=_cparams("parallel", "arbitrary"),
    )(n, x, wg, wu, wd)


def _ffn_bwd_act(dh, g, u, wg, wu, wd, name):
    t, d = dh.shape
    tm, tn = _row_tile(t), FF_TILE
    nj = D_FF // tn

    def body(dh_ref, g_ref, u_ref, wg_ref, wu_ref, wd_ref, dn_ref, dg_ref, du_ref, a_ref, acc_ref):
        j = pl.program_id(1)

        @pl.when(j == 0)
        def _():
            acc_ref[...] = jnp.zeros_like(acc_ref)

        dhb = (0.5 * dh_ref[...]).astype(BF16)
        da = lax.dot_general(dhb, wd_ref[...], _DIMS["nt"], preferred_element_type=F32)
        gv = g_ref[...].astype(F32)
        uv = u_ref[...].astype(F32)
        sg = _sigmoid(gv)
        s = gv * sg
        dg = (da * uv * (sg * (1.0 + gv * (1.0 - sg)))).astype(BF16)
        du = (da * s).astype(BF16)
        dg_ref[...] = dg
        du_ref[...] = du
        a_ref[...] = (s * uv).astype(BF16)
        acc_ref[...] += (jnp.dot(dg, wg_ref[...], preferred_element_type=F32)
                         + jnp.dot(du, wu_ref[...], preferred_element_type=F32))

        @pl.when(j == nj - 1)
        def _():
            dn_ref[...] = acc_ref[...]

    row = pl.BlockSpec((tm, d), lambda i, j: (i, 0))
    act = pl.BlockSpec((tm, tn), lambda i, j: (i, j))
    return pl.pallas_call(
        body, name=name, grid=(t // tm, nj),
        in_specs=[row, act, act] + [pl.BlockSpec((tn, d), lambda i, j: (j, 0))] * 3,
        out_specs=[row, act, act, act],
        out_shape=[jax.ShapeDtypeStruct((t, d), F32)] + [jax.ShapeDtypeStruct((t, D_FF), BF16)] * 3,
        scratch_shapes=[pltpu.VMEM((tm, d), F32)], compiler_params=_cparams("parallel", "arbitrary"),
    )(dh, g, u, wg, wu, wd)


def _ffn_forward(x, gain, wg, wu, wd, tag):
    n = _rms_fwd(x, gain, f"{tag}_norm")
    h, g, u = _ffn_fwd(n, x, wg, wu, wd, f"{tag}_fwd")
    return h, (n, g, u)


def _ffn_backward(dh, x, gain, wg, wu, wd, saved, tag, push):
    n, g, u = saved
    dn, dg, du, a = _ffn_bwd_act(dh, g, u, wg, wu, wd, f"{tag}_bwd_act")
    kw = dict(mode="tn", out_dtype=BF16, tm=FF_TILE, tn=1024, tk=2048)
    d_wg = _matmul(dg, n, name=f"{tag}_dwg", **kw)
    d_wu = _matmul(du, n, name=f"{tag}_dwu", **kw)
    d_wd = _matmul(a, dh, scale=0.5, name=f"{tag}_dwd", **kw)
    token = push({f"{tag}_w_gate": d_wg, f"{tag}_w_up": d_wu, f"{tag}_w_down": d_wd})
    return _rms_bwd(x, gain + token, dn, dh, f"{tag}_norm_bwd")


CONV_COLS = 512
HALO = 8
CONV_STRIP = 32


def _conv_fwd(p, w, b, name):
    t = p.shape[0]
    tm = _row_tile(t)
    c0 = P_XBC // CONV_COLS

    def body(x_ref, halo_ref, w_ref, b_ref, o_ref, s_ref):
        i = pl.program_id(1)
        s_ref[0:HALO, :] = jnp.where(i > 0, halo_ref[...].astype(F32), 0.0)
        s_ref[HALO:HALO + tm, :] = x_ref[...].astype(F32)
        wv = w_ref[...]
        bv = b_ref[...]
        for r0 in range(0, tm, CONV_STRIP):
            acc = bv + wv[0:1, :] * s_ref[HALO - 3 + r0:HALO - 3 + r0 + CONV_STRIP, :]
            for k in range(1, SSD_CONV):
                acc = acc + wv[k:k + 1, :] * s_ref[HALO - 3 + k + r0:HALO - 3 + k + r0 + CONV_STRIP, :]
            o_ref[r0:r0 + CONV_STRIP, :] = (acc * _sigmoid(acc)).astype(o_ref.dtype)

    return pl.pallas_call(
        body, name=name, grid=(XBC // CONV_COLS, t // tm),
        in_specs=[pl.BlockSpec((tm, CONV_COLS), lambda j, i: (i, c0 + j)),
                  pl.BlockSpec((HALO, CONV_COLS), lambda j, i: (jnp.maximum(i * (tm // HALO) - 1, 0), c0 + j)),
                  pl.BlockSpec((SSD_CONV, CONV_COLS), lambda j, i: (0, j)), pl.BlockSpec((1, CONV_COLS), lambda j, i: (0, j))],
        out_specs=pl.BlockSpec((tm, CONV_COLS), lambda j, i: (i, j)),
        out_shape=jax.ShapeDtypeStruct((t, XBC), BF16),
        scratch_shapes=[pltpu.VMEM((tm + HALO, CONV_COLS), F32)], compiler_params=_cparams("parallel", "parallel"),
    )(p, p, w, b)


def _conv_bwd_act(p, dy, w, b, col0, name):
    t, cols = dy.shape
    tm = _row_tile(t)
    c0 = (P_XBC + col0) // CONV_COLS
    w0 = col0 // CONV_COLS

    def body(x_ref, halo_ref, dy_ref, w_ref, b_ref, da_ref, dw_ref, db_ref, s_ref):
        i = pl.program_id(1)

        @pl.when(i == 0)
        def _():
            dw_ref[...] = jnp.zeros_like(dw_ref)
            db_ref[...] = jnp.zeros_like(db_ref)

        s_ref[0:HALO, :] = jnp.where(i > 0, halo_ref[...].astype(F32), 0.0)
        s_ref[HALO:HALO + tm, :] = x_ref[...].astype(F32)
        wv = w_ref[...]
        bv = b_ref[...]
        fold = lambda v: jnp.sum(v.reshape(CONV_STRIP // 8, 8, CONV_COLS), axis=0)
        sums = [jnp.zeros((8, CONV_COLS), F32) for _ in range(SSD_CONV + 1)]
        for r0 in range(0, tm, CONV_STRIP):
            taps = [s_ref[HALO - 3 + k + r0:HALO - 3 + k + r0 + CONV_STRIP, :] for k in range(SSD_CONV)]
            acc = bv + wv[0:1, :] * taps[0]
            for k in range(1, SSD_CONV):
                acc = acc + wv[k:k + 1, :] * taps[k]
            sg = _sigmoid(acc)
            dacc = dy_ref[r0:r0 + CONV_STRIP, :].astype(F32) * (sg * (1.0 + acc * (1.0 - sg)))
            da_ref[r0:r0 + CONV_STRIP, :] = dacc.astype(BF16)
            for k in range(SSD_CONV):
                sums[k] = sums[k] + fold(dacc * taps[k])
            sums[SSD_CONV] = sums[SSD_CONV] + fold(dacc)
        for k in range(SSD_CONV):
            dw_ref[k:k + 1, :] += jnp.sum(sums[k], axis=0, keepdims=True)
        db_ref[...] += jnp.sum(sums[SSD_CONV], axis=0, keepdims=True)

    return pl.pallas_call(
        body, name=name, grid=(cols // CONV_COLS, t // tm),
        in_specs=[pl.BlockSpec((tm, CONV_COLS), lambda j, i: (i, c0 + j)),
                  pl.BlockSpec((HALO, CONV_COLS), lambda j, i: (jnp.maximum(i * (tm // HALO) - 1, 0), c0 + j)),
                  pl.BlockSpec((tm, CONV_COLS), lambda j, i: (i, j)),
                  pl.BlockSpec((SSD_CONV, CONV_COLS), lambda j, i: (0, w0 + j)), pl.BlockSpec((1, CONV_COLS), lambda j, i: (0, w0 + j))],
        out_specs=[pl.BlockSpec((tm, CONV_COLS), lambda j, i: (i, j)), pl.BlockSpec((SSD_CONV, CONV_COLS), lambda j, i: (0, j)),
                   pl.BlockSpec((1, CONV_COLS), lambda j, i: (0, j))],
        out_shape=[jax.ShapeDtypeStruct((t, cols), BF16), jax.ShapeDtypeStruct((SSD_CONV, cols), F32), jax.ShapeDtypeStruct((1, cols), F32)],
        scratch_shapes=[pltpu.VMEM((tm + HALO, CONV_COLS), F32)], compiler_params=_cparams("parallel", "arbitrary"),
    )(p, p, dy, w, b)


def _conv_bwd_dx(dacc, w, col0, name):
    t, cols = dacc.shape
    tm = _row_tile(t)
    nt = t // tm
    w0 = col0 // CONV_COLS

    def body(d_ref, halo_ref, w_ref, o_ref, s_ref):
        i = pl.program_id(1)
        s_ref[0:tm, :] = d_ref[...].astype(F32)
        s_ref[tm:tm + HALO, :] = jnp.where(i < nt - 1, halo_ref[...].astype(F32), 0.0)
        wv = w_ref[...]
        for r0 in range(0, tm, CONV_STRIP):
            acc = wv[3:4, :] * s_ref[r0:r0 + CONV_STRIP, :]
            for k in range(SSD_CONV - 1):
                acc = acc + wv[k:k + 1, :] * s_ref[3 - k + r0:3 - k + r0 + CONV_STRIP, :]
            o_ref[r0:r0 + CONV_STRIP, :] = acc.astype(o_ref.dtype)

    return pl.pallas_call(
        body, name=name, grid=(cols // CONV_COLS, nt),
        in_specs=[pl.BlockSpec((tm, CONV_COLS), lambda j, i: (i, j)),
                  pl.BlockSpec((HALO, CONV_COLS), lambda j, i: (jnp.minimum((i + 1) * (tm // HALO), t // HALO - 1), j)),
                  pl.BlockSpec((SSD_CONV, CONV_COLS), lambda j, i: (0, w0 + j))],
        out_specs=pl.BlockSpec((tm, CONV_COLS), lambda j, i: (i, j)),
        out_shape=jax.ShapeDtypeStruct((t, cols), BF16),
        scratch_shapes=[pltpu.VMEM((tm + HALO, CONV_COLS), F32)], compiler_params=_cparams("parallel", "parallel"),
    )(dacc, dacc, w)


GROUP_COLS = SSD_INNER // SSD_GROUPS
PAIRS = GROUP_COLS // 128
HEADS_PER_GROUP = SSD_HEADS // SSD_GROUPS


def _dt_fwd(dt_raw, bias, name):
    t, n = dt_raw.shape
    tm = _row_tile(t)

    def body(x_ref, b_ref, o_ref):
        v = x_ref[...] + b_ref[...]
        o_ref[...] = jnp.maximum(v, 0.0) + jnp.log1p(jnp.exp(-jnp.abs(v)))

    row = pl.BlockSpec((tm, n), lambda i: (i, 0))
    vec = pl.BlockSpec((1, n), lambda i: (0, 0))
    return pl.pallas_call(body, name=name, grid=(t // tm,), in_specs=[row, vec], out_specs=row,
                          out_shape=jax.ShapeDtypeStruct((t, n), F32), compiler_params=_cparams("parallel"))(dt_raw, bias)


def _dt_bwd(ddt, dt_raw, bias, name):
    t, n = dt_raw.shape
    tm = _row_tile(t)

    def body(d_ref, x_ref, b_ref, o_ref, db_ref):
        @pl.when(pl.program_id(0) == 0)
        def _():
            db_ref[...] = jnp.zeros_like(db_ref)

        dr = d_ref[...] * _sigmoid(x_ref[...] + b_ref[...])
        o_ref[...] = dr.astype(o_ref.dtype)
        db_ref[...] += jnp.sum(dr, axis=0, keepdims=True)

    row = pl.BlockSpec((tm, n), lambda i: (i, 0))
    vec = pl.BlockSpec((1, n), lambda i: (0, 0))
    return pl.pallas_call(body, name=name, grid=(t // tm,), in_specs=[row, row, vec], out_specs=[row, vec],
                          out_shape=[jax.ShapeDtypeStruct((t, n), BF16), jax.ShapeDtypeStruct((1, n), F32)],
                          compiler_params=_cparams("arbitrary"))(ddt, dt_raw, bias)


SSD_STEP = 2


def _ssd_common(dt, dtt, a_log_w, a_log_t):
    l = CHUNK
    a = -jnp.exp(a_log_w)
    at = -jnp.exp(a_log_t)
    rowi = lax.broadcasted_iota(jnp.int32, (l, l), 0)
    coli = lax.broadcasted_iota(jnp.int32, (l, l), 1)
    tri = rowi >= coli
    lower = tri.astype(F32)
    upper = (rowi <= coli).astype(F32)
    acs = jnp.dot(lower, dt * a, precision=HI, preferred_element_type=F32)
    acst = jnp.dot(dtt * at, upper, precision=HI, preferred_element_type=F32)
    return a, acs, acst, tri, upper


def _pair_bc(w, lo, p):
    return jnp.where(lo, w[:, 2 * p:2 * p + 1], w[:, 2 * p + 1:2 * p + 2])


def _ssd_specs(t):
    rows = SSD_STEP * CHUNK
    assert t % rows == 0
    return t // rows, dict(
        xs=lambda cm: pl.BlockSpec((rows, GROUP_COLS), lambda g, c: (cm(c), g)),
        bm=lambda cm: pl.BlockSpec((rows, SSD_STATE), lambda g, c: (cm(c), SSD_INNER // SSD_STATE + g)),
        cmat=lambda cm: pl.BlockSpec((rows, SSD_STATE), lambda g, c: (cm(c), SSD_INNER // SSD_STATE + SSD_GROUPS + g)),
        dtw=lambda cm: pl.BlockSpec((1, rows, 128), lambda g, c: (g, cm(c), 0)),
        dtt=lambda cm: pl.BlockSpec((1, HEADS_PER_GROUP, rows), lambda g, c: (g, 0, cm(c))),
        wide=lambda cm: pl.BlockSpec((1, 1, 128), lambda g, c: (g, 0, 0)),
        tall=lambda cm: pl.BlockSpec((1, HEADS_PER_GROUP, 1), lambda g, c: (g, 0, 0)),
        grp=lambda cm: pl.BlockSpec((rows, GROUP_COLS), lambda g, c: (cm(c), g)),
        vec=lambda cm: pl.BlockSpec((1, GROUP_COLS), lambda g, c: (0, g)),
        state=lambda cm: pl.BlockSpec((1, SSD_STEP, PAIRS, SSD_STATE, 128), lambda g, c: (g, cm(c), 0, 0, 0)),
    )


def _ssd_fwd(xc, p, hv, norm_g, name):
    t = xc.shape[0]
    nc, sp = _ssd_specs(t)
    ident = lambda c: c

    def body(xs_ref, b_ref, c_ref, dtw_ref, dtt_ref, aw_ref, at_ref, dk_ref, z_ref, ng_ref,
             y_ref, ys_ref, h_ref, st_ref):
        @pl.when(pl.program_id(1) == 0)
        def _():
            st_ref[...] = jnp.zeros_like(st_ref)

        lo = lax.broadcasted_iota(jnp.int32, (1, 128), 1) < 64
        dskip = dk_ref[0]
        for s in range(SSD_STEP):
            rows = slice(s * CHUNK, (s + 1) * CHUNK)
            dt = dtw_ref[0, rows, :]
            a, acs, acst, tri, _ = _ssd_common(dt, dtt_ref[0, :, rows], aw_ref[0], at_ref[0])
            ecs = jnp.exp(acs)
            alast = acs[CHUNK - 1:CHUNK, :]
            bmat, cmat = b_ref[rows, :], c_ref[rows, :]
            cb = lax.dot_general(cmat, bmat, _DIMS["nt"], preferred_element_type=F32)
            for pi in range(PAIRS):
                cols = slice(pi * 128, (pi + 1) * 128)
                x = xs_ref[rows, cols].astype(F32)
                xdt = x * _pair_bc(dt, lo, pi)
                ydiag = jnp.zeros((CHUNK, 128), F32)
                for r, mask in ((2 * pi, lo), (2 * pi + 1, jnp.logical_not(lo))):
                    lam = jnp.exp(jnp.where(tri, acs[:, r:r + 1] - acst[r:r + 1, :], -1e30))
                    m = (cb * lam).astype(BF16)
                    ydiag = ydiag + jnp.dot(m, jnp.where(mask, xdt, 0.0).astype(BF16), preferred_element_type=F32)
                ht = st_ref[pi]
                h_ref[0, s, pi] = ht
                yoff = jnp.dot(cmat, ht.astype(BF16), preferred_element_type=F32) * _pair_bc(ecs, lo, pi)
                y_ref[rows, cols] = (ydiag + yoff + _pair_bc(dskip, lo, pi) * x).astype(y_ref.dtype)
                alp = _pair_bc(alast, lo, pi)
                e = jnp.exp(alp - _pair_bc(acs, lo, pi))
                st = lax.dot_general(bmat, (xdt * e).astype(BF16), _DIMS["tn"], preferred_element_type=F32)
                st_ref[pi] = ht * jnp.exp(alp) + st
            zf = z_ref[rows, :].astype(F32)
            yg = y_ref[rows, :].astype(F32) * (zf * _sigmoid(zf))
            rstd = lax.rsqrt(jnp.mean(yg * yg, axis=-1, keepdims=True) + EPS)
            ys_ref[rows, :] = (yg * rstd * ng_ref[...]).astype(ys_ref.dtype)

    ins = ["xs", "bm", "cmat", "dtw", "dtt", "wide", "tall", "wide", "grp", "vec"]
    return pl.pallas_call(
        body, name=name, grid=(SSD_GROUPS, nc),
        in_specs=[sp[k](ident) for k in ins],
        out_specs=[sp["grp"](ident), sp["grp"](ident), sp["state"](ident)],
        out_shape=[jax.ShapeDtypeStruct((t, SSD_INNER), BF16), jax.ShapeDtypeStruct((t, SSD_INNER), BF16),
                   jax.ShapeDtypeStruct((SSD_GROUPS, t // CHUNK, PAIRS, SSD_STATE, 128), F32)],
        scratch_shapes=[pltpu.VMEM((PAIRS, SSD_STATE, 128), F32)], compiler_params=_cparams("parallel", "arbitrary"),
    )(xc, xc, xc, hv["dtw"], hv["dtt"], hv["alog_w"], hv["alog_t"], hv["dskip_w"], p, norm_g)


def _ssd_bwd(xc, p, hv, norm_g, y, dys, states, name):
    t = xc.shape[0]
    nc, sp = _ssd_specs(t)
    rev = lambda c: nc - 1 - c

    def body(xs_ref, b_ref, c_ref, dtw_ref, dtt_ref, aw_ref, at_ref, dk_ref, z_ref, ng_ref,
             y_ref, dys_ref, h_ref,
             dxs_ref, db_ref, dc_ref, dz_ref, ddt_ref, hsum_ref, dng_ref, dst_ref):
        @pl.when(pl.program_id(1) == 0)
        def _():
            dst_ref[...] = jnp.zeros_like(dst_ref)
            hsum_ref[...] = jnp.zeros_like(hsum_ref)
            dng_ref[...] = jnp.zeros_like(dng_ref)

        lane = lax.broadcasted_iota(jnp.int32, (1, 128), 1)
        lo = lane < 64
        dskip = dk_ref[0]
        sel_r = lax.broadcasted_iota(jnp.int32, (128, 128), 0)
        sel_c = lax.broadcasted_iota(jnp.int32, (128, 128), 1)
        refs = (xs_ref, b_ref, c_ref, dtw_ref, dtt_ref, aw_ref, at_ref, dk_ref, z_ref, ng_ref, y_ref, dys_ref, h_ref,
                dxs_ref, db_ref, dc_ref, dz_ref, ddt_ref, hsum_ref, dng_ref, dst_ref)
        for s in reversed(range(SSD_STEP)):
            chunk_bwd(refs, slice(s * CHUNK, (s + 1) * CHUNK), s, lane, lo, dskip, sel_r, sel_c)

    def chunk_bwd(refs, rows, s, lane, lo, dskip, sel_r, sel_c):
        (xs_ref, b_ref, c_ref, dtw_ref, dtt_ref, aw_ref, at_ref, dk_ref, z_ref, ng_ref, y_ref, dys_ref, h_ref,
         dxs_ref, db_ref, dc_ref, dz_ref, ddt_ref, hsum_ref, dng_ref, dst_ref) = refs
        dt = dtw_ref[0, rows, :]
        a, acs, acst, tri, upper = _ssd_common(dt, dtt_ref[0, :, rows], aw_ref[0], at_ref[0])
        ecs = jnp.exp(acs)
        alast = acs[CHUNK - 1:CHUNK, :]
        bmat, cmat = b_ref[rows, :], c_ref[rows, :]
        cb = lax.dot_general(cmat, bmat, _DIMS["nt"], preferred_element_type=F32)

        zf = z_ref[rows, :].astype(F32)
        sg = _sigmoid(zf)
        sz = zf * sg
        yv = y_ref[rows, :].astype(F32)
        yg = yv * sz
        rstd = lax.rsqrt(jnp.mean(yg * yg, axis=-1, keepdims=True) + EPS)
        yhat = yg * rstd
        dysv = dys_ref[rows, :].astype(F32)
        dng_ref[...] += jnp.sum(dysv * yhat, axis=0, keepdims=True)
        dyh = dysv * ng_ref[...]
        dyg = rstd * (dyh - yhat * jnp.mean(dyh * yhat, axis=-1, keepdims=True))
        dz_ref[rows, :] = (dyg * yv * (sg * (1.0 + zf * (1.0 - sg)))).astype(dz_ref.dtype)
        dy_all = dyg * sz

        dal = jnp.zeros((CHUNK, 128), F32)
        ddtm = jnp.zeros((CHUNK, 128), F32)
        dalast = jnp.zeros((8, 128), F32)
        ddsk = jnp.zeros((8, 128), F32)
        dcb = jnp.zeros((CHUNK, CHUNK), F32)
        qcol = jnp.zeros((8, CHUNK), F32)
        sub8 = lax.broadcasted_iota(jnp.int32, (8, CHUNK), 0)
        dc_acc = jnp.zeros((CHUNK, SSD_STATE), F32)
        db_acc = jnp.zeros((CHUNK, SSD_STATE), F32)
        for pi in range(PAIRS):
            sel = (sel_c == 2 * pi + (sel_r >= 64).astype(jnp.int32)).astype(BF16)

            def hsum(v, sel=sel):
                return jnp.dot(v.astype(BF16), sel, preferred_element_type=F32)

            dyp = dy_all[:, pi * 128:(pi + 1) * 128]
            x = xs_ref[rows, pi * 128:(pi + 1) * 128].astype(F32)
            dtp = _pair_bc(dt, lo, pi)
            xdt = x * dtp
            dxdt = jnp.zeros((CHUNK, 128), F32)
            for r, mask in ((2 * pi, lo), (2 * pi + 1, jnp.logical_not(lo))):
                lam = jnp.exp(jnp.where(tri, acs[:, r:r + 1] - acst[r:r + 1, :], -1e30))
                m32 = cb * lam
                m = m32.astype(BF16)
                dyr = jnp.where(mask, dyp, 0.0).astype(BF16)
                xr = jnp.where(mask, xdt, 0.0).astype(BF16)
                dm = lax.dot_general(dyr, xr, _DIMS["nt"], preferred_element_type=F32)
                dcb = dcb + dm * lam
                q = dm * m32
                dal = dal + jnp.sum(q, axis=1, keepdims=True) * (lane == r).astype(F32)
                qcol = qcol + jnp.where(sub8 == r, jnp.sum(q, axis=0, keepdims=True), 0.0)
                dxdt = dxdt + lax.dot_general(m, dyr, _DIMS["tn"], preferred_element_type=F32)
            ht = h_ref[0, s, pi]
            htb = ht.astype(BF16)
            ecp = _pair_bc(ecs, lo, pi)
            yoff = jnp.dot(cmat, htb, preferred_element_type=F32) * ecp
            dg = (dyp * ecp).astype(BF16)
            dc_acc = dc_acc + lax.dot_general(dg, htb, _DIMS["nt"], preferred_element_type=F32)
            dht = lax.dot_general(cmat, dg, _DIMS["tn"], preferred_element_type=F32)
            dal = dal + hsum(dyp * yoff)
            dhn = dst_ref[pi]
            dhnb = dhn.astype(BF16)
            alp = _pair_bc(alast, lo, pi)
            e = jnp.exp(alp - _pair_bc(acs, lo, pi))
            xe = xdt * e
            db_acc = db_acc + lax.dot_general(xe.astype(BF16), dhnb, _DIMS["nt"], preferred_element_type=F32)
            dxe = jnp.dot(bmat, dhnb, preferred_element_type=F32)
            dxdt = dxdt + dxe * e
            tt = hsum(dxe * xe)
            dal = dal - tt
            dec = jnp.exp(alp)
            dalast = dalast + jnp.sum(tt, axis=0, keepdims=True) + hsum(
                jnp.broadcast_to(jnp.sum(dhn * ht, axis=0, keepdims=True) * dec, (8, 128)))
            dst_ref[pi] = dht + dhn * dec
            dxs_ref[rows, pi * 128:(pi + 1) * 128] = (_pair_bc(dskip, lo, pi) * dyp + dxdt * dtp).astype(dxs_ref.dtype)
            ddtm = ddtm + hsum(dxdt * x)
            ddsk = ddsk + hsum(jnp.broadcast_to(jnp.sum(dyp * x, axis=0, keepdims=True), (8, 128)))
        rowi = lax.broadcasted_iota(jnp.int32, (CHUNK, 128), 0)
        qcol_w = lax.dot_general(jnp.concatenate([qcol, jnp.zeros((CHUNK - 8, CHUNK), F32)], axis=0), (sel_r == sel_c).astype(F32),
                                 _DIMS["tn"], precision=HI, preferred_element_type=F32)
        dal = dal - qcol_w + jnp.where(rowi == CHUNK - 1, dalast[0:1, :], 0.0)
        dda = jnp.dot(upper, dal, precision=HI, preferred_element_type=F32)
        ddt_ref[0, rows, :] = ddtm + dda * a
        hsum_ref[0, 1:2, :] += jnp.sum(dda * dt, axis=0, keepdims=True) * a
        hsum_ref[0, 2:3, :] += ddsk[0:1, :]
        dcbb = dcb.astype(BF16)
        dc_ref[rows, :] = (jnp.dot(dcbb, bmat, preferred_element_type=F32) + dc_acc).astype(dc_ref.dtype)
        db_ref[rows, :] = (lax.dot_general(dcbb, cmat, _DIMS["tn"], preferred_element_type=F32) + db_acc).astype(db_ref.dtype)

    ins = ["xs", "bm", "cmat", "dtw", "dtt", "wide", "tall", "wide", "grp", "vec", "grp", "grp", "state"]
    col = lambda: pl.BlockSpec((SSD_STEP * CHUNK, SSD_STATE), lambda g, c: (rev(c), g))
    return pl.pallas_call(
        body, name=name, grid=(SSD_GROUPS, nc),
        in_specs=[sp[k](rev) for k in ins],
        out_specs=[sp["grp"](rev), col(), col(), sp["grp"](rev), sp["dtw"](rev),
                   pl.BlockSpec((1, 8, 128), lambda g, c: (g, 0, 0)), sp["vec"](rev)],
        out_shape=[jax.ShapeDtypeStruct((t, SSD_INNER), BF16), jax.ShapeDtypeStruct((t, SSD_GROUPS * SSD_STATE), BF16),
                   jax.ShapeDtypeStruct((t, SSD_GROUPS * SSD_STATE), BF16), jax.ShapeDtypeStruct((t, SSD_INNER), BF16),
                   jax.ShapeDtypeStruct((SSD_GROUPS, t, 128), F32), jax.ShapeDtypeStruct((SSD_GROUPS, 8, 128), F32),
                   jax.ShapeDtypeStruct((1, SSD_INNER), F32)],
        scratch_shapes=[pltpu.VMEM((PAIRS, SSD_STATE, 128), F32)], compiler_params=_cparams("parallel", "arbitrary"),
    )(xc, xc, xc, hv["dtw"], hv["dtt"], hv["alog_w"], hv["alog_t"], hv["dskip_w"], p, norm_g, y, dys, states)


def _wide(v):
    return jnp.pad(v.reshape(SSD_GROUPS, 1, HEADS_PER_GROUP), ((0, 0), (0, 0), (0, 128 - HEADS_PER_GROUP)))


def _head_inputs(dt, a_log, d_skip):
    t = dt.shape[0]
    g = dt[:, :SSD_HEADS].reshape(t, SSD_GROUPS, HEADS_PER_GROUP)
    return dict(
        dtw=jnp.pad(jnp.transpose(g, (1, 0, 2)), ((0, 0), (0, 0), (0, 128 - HEADS_PER_GROUP))),
        dtt=jnp.transpose(g, (1, 2, 0)),
        alog_w=_wide(a_log), alog_t=a_log.reshape(SSD_GROUPS, HEADS_PER_GROUP, 1),
        dskip_w=_wide(d_skip),
    )


def _gelu(x):
    return 0.5 * x * (1.0 + lax.erf(x * (1.0 / math.sqrt(2.0))))


def _gelu_grad(x):
    return 0.5 * (1.0 + lax.erf(x * (1.0 / math.sqrt(2.0)))) + x * jnp.exp(-0.5 * x * x) * (1.0 / math.sqrt(2.0 * math.pi))


def _tril_mask():
    r = lax.broadcasted_iota(jnp.int32, (CHUNK, CHUNK), 0)
    c = lax.broadcasted_iota(jnp.int32, (CHUNK, CHUNK), 1)
    return r >= c


def _gmlp_fwd(p, v_gain, w_s, b_col, name):
    t = p.shape[0]
    tm = _row_tile(t)
    u0 = P_UV // GMLP_W

    def body(u_ref, v_ref, gn_ref, ws_ref, bs_ref, o_ref):
        v = _gelu(v_ref[...].astype(F32))
        v = (v * lax.rsqrt(jnp.mean(v * v, axis=-1, keepdims=True) + EPS) * gn_ref[...]).astype(BF16)
        tril = _tril_mask()
        wm = [jnp.where(tril, ws_ref[g], 0.0).astype(BF16) for g in range(GMLP_GROUPS)]
        for k in range(tm // CHUNK):
            rows = slice(k * CHUNK, (k + 1) * CHUNK)
            for g in range(GMLP_GROUPS):
                cols = slice(g * 128, (g + 1) * 128)
                mixed = jnp.dot(wm[g], v[rows, cols], preferred_element_type=F32) + bs_ref[g]
                o_ref[rows, cols] = (_gelu(u_ref[rows, cols].astype(F32)) * mixed).astype(o_ref.dtype)

    return pl.pallas_call(
        body, name=name, grid=(t // tm,),
        in_specs=[pl.BlockSpec((tm, GMLP_W), lambda i: (i, u0)), pl.BlockSpec((tm, GMLP_W), lambda i: (i, u0 + 1)),
                  pl.BlockSpec((1, GMLP_W), lambda i: (0, 0)), pl.BlockSpec((GMLP_GROUPS, CHUNK, CHUNK), lambda i: (0, 0, 0)),
                  pl.BlockSpec((GMLP_GROUPS, CHUNK, 1), lambda i: (0, 0, 0))],
        out_specs=pl.BlockSpec((tm, GMLP_W), lambda i: (i, 0)),
        out_shape=jax.ShapeDtypeStruct((t, GMLP_W), BF16), compiler_params=_cparams("parallel"),
    )(p, p, v_gain, w_s, b_col)


def _gmlp_bwd(p, dy, v_gain, w_s, b_col, name):
    t = p.shape[0]
    tm = _row_tile(t)
    u0 = P_UV // GMLP_W

    def body(u_ref, v_ref, dy_ref, gn_ref, ws_ref, bs_ref, duv_ref, dws_ref, dbs_ref, dgn_ref, dvn_ref):
        @pl.when(pl.program_id(0) == 0)
        def _():
            dws_ref[...] = jnp.zeros_like(dws_ref)
            dbs_ref[...] = jnp.zeros_like(dbs_ref)
            dgn_ref[...] = jnp.zeros_like(dgn_ref)

        vraw = v_ref[...].astype(F32)
        va = _gelu(vraw)
        rstd = lax.rsqrt(jnp.mean(va * va, axis=-1, keepdims=True) + EPS)
        vhat = va * rstd
        gain = gn_ref[...]
        vn = (vhat * gain).astype(BF16)
        tril = _tril_mask()
        wm = [jnp.where(tril, ws_ref[g], 0.0).astype(BF16) for g in range(GMLP_GROUPS)]
        for k in range(tm // CHUNK):
            rows = slice(k * CHUNK, (k + 1) * CHUNK)
            for g in range(GMLP_GROUPS):
                cols = slice(g * 128, (g + 1) * 128)
                uraw = u_ref[rows, cols].astype(F32)
                vb = vn[rows, cols]
                mixed = jnp.dot(wm[g], vb, preferred_element_type=F32) + bs_ref[g]
                dyb = dy_ref[rows, cols].astype(F32)
                duv_ref[rows, cols] = (dyb * mixed * _gelu_grad(uraw)).astype(duv_ref.dtype)
                dmix = dyb * _gelu(uraw)
                dmb = dmix.astype(BF16)
                dws_ref[g] += jnp.where(tril, lax.dot_general(dmb, vb, _DIMS["nt"], preferred_element_type=F32), 0.0)
                dbs_ref[g] += jnp.sum(dmix, axis=1, keepdims=True)
                dvn_ref[rows, cols] = lax.dot_general(wm[g], dmb, _DIMS["tn"], preferred_element_type=F32)
        dvn = dvn_ref[...]
        dgn_ref[...] += jnp.sum(dvn * vhat, axis=0, keepdims=True)
        dvh = dvn * gain
        dva = rstd * (dvh - vhat * jnp.mean(dvh * vhat, axis=-1, keepdims=True))
        duv_ref[:, GMLP_W:2 * GMLP_W] = (dva * _gelu_grad(vraw)).astype(duv_ref.dtype)

    return pl.pallas_call(
        body, name=name, grid=(t // tm,),
        in_specs=[pl.BlockSpec((tm, GMLP_W), lambda i: (i, u0)), pl.BlockSpec((tm, GMLP_W), lambda i: (i, u0 + 1)),
                  pl.BlockSpec((tm, GMLP_W), lambda i: (i, 0)),
                  pl.BlockSpec((1, GMLP_W), lambda i: (0, 0)), pl.BlockSpec((GMLP_GROUPS, CHUNK, CHUNK), lambda i: (0, 0, 0)),
                  pl.BlockSpec((GMLP_GROUPS, CHUNK, 1), lambda i: (0, 0, 0))],
        out_specs=[pl.BlockSpec((tm, 2 * GMLP_W), lambda i: (i, 0)), pl.BlockSpec((GMLP_GROUPS, CHUNK, CHUNK), lambda i: (0, 0, 0)),
                   pl.BlockSpec((GMLP_GROUPS, CHUNK, 1), lambda i: (0, 0, 0)), pl.BlockSpec((1, GMLP_W), lambda i: (0, 0))],
        out_shape=[jax.ShapeDtypeStruct((t, 2 * GMLP_W), BF16), jax.ShapeDtypeStruct((GMLP_GROUPS, CHUNK, CHUNK), F32),
                   jax.ShapeDtypeStruct((GMLP_GROUPS, CHUNK, 1), F32), jax.ShapeDtypeStruct((1, GMLP_W), F32)],
        scratch_shapes=[pltpu.VMEM((tm, GMLP_W), F32)], compiler_params=_cparams("arbitrary"),
    )(p, p, dy, v_gain, w_s, b_col)


def _head_masks():
    lane = lax.broadcasted_iota(jnp.int32, (1, MEM_W), 1)
    return [(lane >= h * 64) & (lane < (h + 1) * 64) for h in range(MEM_HEADS)]


def _mem_fwd(p, kv, name):
    t = p.shape[0]
    tm = _row_tile(t)
    q0 = P_Q // MEM_W

    def body(q_ref, kv_ref, o_ref):
        q = q_ref[...]
        k = kv_ref[:, 0:MEM_W].astype(BF16)
        v = kv_ref[:, MEM_W:2 * MEM_W].astype(BF16)
        out = jnp.zeros((tm, MEM_W), F32)
        for mask in _head_masks():
            s = lax.dot_general(jnp.where(mask, q, 0), k, _DIMS["nt"], preferred_element_type=F32) * 0.125
            e = jnp.exp(s - jnp.max(s, axis=-1, keepdims=True))
            pr = (e * (1.0 / jnp.sum(e, axis=-1, keepdims=True))).astype(BF16)
            out = out + jnp.where(mask, jnp.dot(pr, v, preferred_element_type=F32), 0.0)
        o_ref[...] = out.astype(o_ref.dtype)

    return pl.pallas_call(
        body, name=name, grid=(t // tm,),
        in_specs=[pl.BlockSpec((tm, MEM_W), lambda i: (i, q0)), pl.BlockSpec((MEM_LEN, 2 * MEM_W), lambda i: (0, 0))],
        out_specs=pl.BlockSpec((tm, MEM_W), lambda i: (i, 0)),
        out_shape=jax.ShapeDtypeStruct((t, MEM_W), BF16), compiler_params=_cparams("parallel"),
    )(p, kv)


def _mem_bwd(p, kv, dy, name):
    t = p.shape[0]
    tm = _row_tile(t)
    q0 = P_Q // MEM_W

    def body(q_ref, kv_ref, dy_ref, dq_ref, dkv_ref):
        @pl.when(pl.program_id(0) == 0)
        def _():
            dkv_ref[...] = jnp.zeros_like(dkv_ref)

        q = q_ref[...]
        dy = dy_ref[...]
        k = kv_ref[:, 0:MEM_W].astype(BF16)
        v = kv_ref[:, MEM_W:2 * MEM_W].astype(BF16)
        dq = jnp.zeros((tm, MEM_W), F32)
        dk = jnp.zeros((MEM_LEN, MEM_W), F32)
        dv = jnp.zeros((MEM_LEN, MEM_W), F32)
        for mask in _head_masks():
            qh = jnp.where(mask, q, 0)
            dyh = jnp.where(mask, dy, 0)
            s = lax.dot_general(qh, k, _DIMS["nt"], preferred_element_type=F32) * 0.125
            e = jnp.exp(s - jnp.max(s, axis=-1, keepdims=True))
            pr = e * (1.0 / jnp.sum(e, axis=-1, keepdims=True))
            prb = pr.astype(BF16)
            dp = lax.dot_general(dyh, v, _DIMS["nt"], preferred_element_type=F32)
            ds = (pr * (dp - jnp.sum(dp * pr, axis=-1, keepdims=True)) * 0.125).astype(BF16)
            dq = dq + jnp.where(mask, jnp.dot(ds, k, preferred_element_type=F32), 0.0)
            dk = dk + lax.dot_general(ds, qh, _DIMS["tn"], preferred_element_type=F32)
            dv = dv + lax.dot_general(prb, dyh, _DIMS["tn"], preferred_element_type=F32)
        dq_ref[...] = dq.astype(dq_ref.dtype)
        dkv_ref[:, 0:MEM_W] += dk
        dkv_ref[:, MEM_W:2 * MEM_W] += dv

    return pl.pallas_call(
        body, name=name, grid=(t // tm,),
        in_specs=[pl.BlockSpec((tm, MEM_W), lambda i: (i, q0)), pl.BlockSpec((MEM_LEN, 2 * MEM_W), lambda i: (0, 0)),
                  pl.BlockSpec((tm, MEM_W), lambda i: (i, 0))],
        out_specs=[pl.BlockSpec((tm, MEM_W), lambda i: (i, 0)), pl.BlockSpec((MEM_LEN, 2 * MEM_W), lambda i: (0, 0))],
        out_shape=[jax.ShapeDtypeStruct((t, MEM_W), BF16), jax.ShapeDtypeStruct((MEM_LEN, 2 * MEM_W), F32)],
        compiler_params=_cparams("arbitrary"),
    )(p, kv, dy)


def _merge_fwd(p, b_ssd, b_gmlp, b_mem, name):
    t = p.shape[0]
    tm = _row_tile(t)
    g0 = P_GL // D_MODEL

    def body(g1, g2, g3, b1, b2, b3, o_ref):
        def strip(rows, carry):
            acc = _sigmoid(g1[rows, :].astype(F32)) * b1[rows, :].astype(F32)
            acc = acc + _sigmoid(g2[rows, :].astype(F32)) * b2[rows, :].astype(F32)
            acc = acc + _sigmoid(g3[rows, :].astype(F32)) * b3[rows, :].astype(F32)
            o_ref[rows, :] = acc.astype(o_ref.dtype)
            return carry

        _strips(tm, strip, 0)

    row = pl.BlockSpec((tm, D_MODEL), lambda i: (i, 0))
    return pl.pallas_call(
        body, name=name, grid=(t // tm,),
        in_specs=[pl.BlockSpec((tm, D_MODEL), lambda i, k=k: (i, g0 + k)) for k in range(3)] + [row] * 3,
        out_specs=row, out_shape=jax.ShapeDtypeStruct((t, D_MODEL), BF16), compiler_params=_cparams("parallel"),
    )(p, p, p, b_ssd, b_gmlp, b_mem)


def _merge_bwd(p, dm, b_ssd, b_gmlp, b_mem, name):
    t = p.shape[0]
    tm = _row_tile(t)
    g0 = P_GL // D_MODEL

    def body(g1, g2, g3, dm_ref, b1, b2, b3, d1, d2, d3, dgl_ref):
        def strip(rows, carry):
            dmv = dm_ref[rows, :].astype(F32)
            for k, (g_ref, b_ref, d_ref) in enumerate(((g1, b1, d1), (g2, b2, d2), (g3, b3, d3))):
                sg = _sigmoid(g_ref[rows, :].astype(F32))
                d_ref[rows, :] = (dmv * sg).astype(d_ref.dtype)
                dgl_ref[rows, k * D_MODEL:(k + 1) * D_MODEL] = (dmv * b_ref[rows, :].astype(F32) * sg * (1.0 - sg)).astype(dgl_ref.dtype)
            return carry

        _strips(tm, strip, 0)

    row = pl.BlockSpec((tm, D_MODEL), lambda i: (i, 0))
    return pl.pallas_call(
        body, name=name, grid=(t // tm,),
        in_specs=[pl.BlockSpec((tm, D_MODEL), lambda i, k=k: (i, g0 + k)) for k in range(3)] + [row] * 4,
        out_specs=[row, row, row, pl.BlockSpec((tm, 3 * D_MODEL), lambda i: (i, 0))],
        out_shape=[jax.ShapeDtypeStruct((t, D_MODEL), BF16)] * 3 + [jax.ShapeDtypeStruct((t, 3 * D_MODEL), BF16)],
        compiler_params=_cparams("parallel"),
    )(p, p, p, dm, b_ssd, b_gmlp, b_mem)


def _local_step(x, mem, target, w, rest_weights, push):
    t = x.shape[0]
    mm = functools.partial(_matmul, tk=1024)

    h1, ffn1_saved = _ffn_forward(x, w["ffn1_norm"], w["ffn1_w_gate"], w["ffn1_w_up"], w["ffn1_w_down"], "ffn1")
    w = {**w, **rest_weights(h1)}
    n2 = _rms_fwd(h1, w["mix_norm"], "mix_norm")
    p = mm(n2, w["w_in_p"], mode="nt", out_dtype=BF16, tm=1024, tn=1536, name="in_proj")
    dt_raw = mm(n2, w["w_dt"], mode="nt", out_dtype=F32, tm=512, tn=128, name="dt_proj")
    dt_bias = jnp.pad(w["ssd_dt_bias"], (0, 128 - SSD_HEADS)).reshape(1, 128)
    hv = _head_inputs(_dt_fwd(dt_raw, dt_bias, "dt_fwd"), w["ssd_a_log"], w["ssd_d"])
    xc = _conv_fwd(p, w["ssd_conv_w"], w["ssd_conv_b"], "conv_fwd")
    y_ssd_raw, y_ssd, states = _ssd_fwd(xc, p, hv, w["ssd_norm"], "ssd_fwd")
    b_col = w["gmlp_b_s"].reshape(GMLP_GROUPS, CHUNK, 1)
    y_gmlp = _gmlp_fwd(p, w["gmlp_v_norm"], w["gmlp_w_s"], b_col, "gmlp_fwd")
    mem_n = _rms_fwd(mem, w["mem_norm"], "mem_norm")
    kv = mm(mem_n, w["w_mem_kv"], mode="nn", out_dtype=F32, tm=256, tn=512, name="mem_kv")
    y_mem = _mem_fwd(p, kv, "mem_fwd")
    b_ssd = mm(y_ssd, w["w_branch_ssd"], mode="nn", out_dtype=BF16, tm=512, tn=1024, name="branch_ssd")
    b_gmlp = mm(y_gmlp, w["w_branch_gmlp"], mode="nn", out_dtype=BF16, tm=512, tn=1024, name="branch_gmlp")
    b_mem = mm(y_mem, w["w_branch_mem"], mode="nt", out_dtype=BF16, tm=512, tn=1024, tk=MEM_W, name="branch_mem")
    merged = _merge_fwd(p, b_ssd, b_gmlp, b_mem, "merge_fwd")
    h2 = mm(merged, w["w_out"], mode="nn", out_dtype=F32, tm=512, tn=1024, addend=h1, name="out_proj")
    h3, ffn2_saved = _ffn_forward(h2, w["ffn2_norm"], w["ffn2_w_gate"], w["ffn2_w_up"], w["ffn2_w_down"], "ffn2")
    dh3, d_final, loss = _loss_head(h3, w["final_norm"], target, "loss_head")

    g = {"final_norm": d_final}
    big = {}
    dh2, g["ffn2_norm"] = _ffn_backward(dh3, h2, w["ffn2_norm"], w["ffn2_w_gate"], w["ffn2_w_up"], w["ffn2_w_down"], ffn2_saved,
                                        "ffn2", functools.partial(push, 0))
    dmerged = mm(dh2, w["w_out"], mode="nt", out_dtype=BF16, tm=512, tn=1024, name="out_proj_dx")
    big["w_out"] = mm(merged, dh2, mode="tn", out_dtype=BF16, tm=1024, tn=1024, name="out_proj_dw")
    db_ssd, db_gmlp, db_mem, dgl = _merge_bwd(p, dmerged, b_ssd, b_gmlp, b_mem, "merge_bwd")
    dy_ssd = mm(db_ssd, w["w_branch_ssd"], mode="nt", out_dtype=BF16, tm=512, tn=2048, name="branch_ssd_dx")
    dy_gmlp = mm(db_gmlp, w["w_branch_gmlp"], mode="nt", out_dtype=BF16, tm=512, tn=1024, name="branch_gmlp_dx")
    dy_mem = mm(db_mem, w["w_branch_mem"], mode="nn", out_dtype=BF16, tm=512, tn=256, name="branch_mem_dx")
    big["w_branch_ssd"] = mm(y_ssd, db_ssd, mode="tn", out_dtype=BF16, tm=1024, tn=1024, name="branch_ssd_dw")
    big["w_branch_gmlp"] = mm(y_gmlp, db_gmlp, mode="tn", out_dtype=BF16, tm=1024, tn=1024, name="branch_gmlp_dw")
    big["w_branch_mem"] = mm(db_mem, y_mem, mode="tn", out_dtype=BF16, tm=1024, tn=256, name="branch_mem_dw")
    dq, dkv = _mem_bwd(p, kv, dy_mem, "mem_bwd")
    big["w_mem_kv"] = mm(mem_n, dkv, mode="tn", out_dtype=BF16, tm=1024, tn=512, tk=256, name="mem_kv_dw")
    dmem_n = mm(dkv, w["w_mem_kv"], mode="nt", out_dtype=F32, tm=256, tn=1024, tk=512, name="mem_kv_dx")
    _, g["mem_norm"] = _rms_bwd(mem, w["mem_norm"], dmem_n, None, "mem_norm_bwd")
    duv, d_ws, d_bs, g["gmlp_v_norm"] = _gmlp_bwd(p, dy_gmlp, w["gmlp_v_norm"], w["gmlp_w_s"], b_col, "gmlp_bwd")
    g["gmlp_w_s"] = d_ws
    g["gmlp_b_s"] = d_bs.reshape(GMLP_GROUPS, CHUNK)
    dxs, d_bm, d_cm, dz, ddt_w, hsums, g["ssd_norm"] = _ssd_bwd(xc, p, hv, w["ssd_norm"], y_ssd_raw, dy_ssd, states, "ssd_bwd")
    heads = hsums[:, :, :HEADS_PER_GROUP]
    g["ssd_a_log"] = heads[:, 1, :].reshape(1, SSD_HEADS)
    g["ssd_d"] = heads[:, 2, :].reshape(1, SSD_HEADS)
    ddt = jnp.transpose(ddt_w[:, :, :HEADS_PER_GROUP], (1, 0, 2)).reshape(t, SSD_HEADS)
    ddt, d_bias = _dt_bwd(jnp.pad(ddt, ((0, 0), (0, 128 - SSD_HEADS))), dt_raw, dt_bias, "dt_bwd")
    g["ssd_dt_bias"] = d_bias[:, :SSD_HEADS]
    parts, dws, dbs = [], [], []
    for dyc, col0, tag in ((dxs, 0, "x"), (d_bm, SSD_INNER, "b"), (d_cm, SSD_INNER + SSD_GROUPS * SSD_STATE, "c")):
        dacc, dw_c, db_c = _conv_bwd_act(p, dyc, w["ssd_conv_w"], w["ssd_conv_b"], col0, f"conv_bwd_act_{tag}")
        parts.append(_conv_bwd_dx(dacc, w["ssd_conv_w"], col0, f"conv_bwd_dx_{tag}"))
        dws.append(dw_c)
        dbs.append(db_c)
    g["ssd_conv_w"] = jnp.concatenate(dws, axis=1)
    g["ssd_conv_b"] = jnp.concatenate(dbs, axis=1)
    dp = jnp.concatenate([dz, duv] + parts + [dgl, dq, jnp.zeros((t, P_W - P_USED), BF16)], axis=1)
    d_win_p = _matmul(dp, n2, mode="tn", out_dtype=BF16, tm=1536, tn=1024, tk=2048, name="in_proj_dw")
    d_wdt = mm(ddt, n2, mode="tn", out_dtype=BF16, tm=128, tn=1024, name="dt_proj_dw")
    sl = lambda a, o, n: a[o:o + n]
    big["w_in"] = jnp.concatenate([sl(d_win_p, P_Z, 2048), sl(d_win_p, P_XBC, XBC), d_wdt[:SSD_HEADS], sl(d_win_p, P_UV, 2048),
                                   sl(d_win_p, P_Q, MEM_W), sl(d_win_p, P_GL, 3 * D_MODEL)], axis=0)
    token = push(1, big)
    dn2 = mm(dp, w["w_in_p"], mode="nn", out_dtype=F32, tm=1024, tn=1024, tk=3584, name="in_proj_dx")
    dn2 = _matmul(ddt, w["w_dt"], mode="nn", out_dtype=F32, tm=512, tn=1024, tk=128, addend=dn2, name="dt_proj_dx")
    dh1, g["mix_norm"] = _rms_bwd(h1, w["mix_norm"] + token, dn2, dh2, "mix_norm_bwd")
    dx, g["ffn1_norm"] = _ffn_backward(dh1, x, w["ffn1_norm"], w["ffn1_w_gate"], w["ffn1_w_up"], w["ffn1_w_down"], ffn1_saved,
                                       "ffn1", functools.partial(push, 2))
    return loss, dx, g


def _split_w_in(w_in_t):
    sl = lambda o, n: w_in_t[o:o + n]
    w_p = jnp.concatenate([sl(IN_Z, 2048), sl(IN_UV, 2048), sl(IN_XBC, XBC), sl(IN_GL, 3 * D_MODEL), sl(IN_Q, MEM_W),
                           jnp.zeros((P_W - P_USED, D_MODEL), w_in_t.dtype)], axis=0)
    w_dt = jnp.pad(sl(IN_DT, SSD_HEADS), ((0, 128 - SSD_HEADS), (0, 0)))
    return w_p, w_dt


def _pick_tile(rows, cap=512):
    best = None
    for tile in range(8, min(rows, cap) + 1, 8):
        if rows % tile == 0:
            best = tile
    return best if best is not None else rows


def _adamw(w, g, m, v, name):
    rows, lanes = w.shape
    tile = _pick_tile(rows, cap=max(8, (512 * 1024 // lanes) // 8 * 8))
    c1 = 1.0 / (1.0 - ADAM_B1 ** ADAM_STEP)
    c2 = 1.0 / (1.0 - ADAM_B2 ** ADAM_STEP)

    def body(w_ref, g_ref, m_ref, v_ref, d_ref, nm_ref, nv_ref):
        gv = g_ref[...]
        nm = ADAM_B1 * m_ref[...] + (1.0 - ADAM_B1) * gv
        nv = ADAM_B2 * v_ref[...] + (1.0 - ADAM_B2) * (gv * gv)
        nm_ref[...] = nm
        nv_ref[...] = nv
        d_ref[...] = -ADAM_LR * ((nm * c1) / (jnp.sqrt(nv * c2) + ADAM_EPS) + ADAM_WD * w_ref[...])

    blk = pl.BlockSpec((tile, lanes), lambda i: (i, 0))
    return pl.pallas_call(
        body, name=name, grid=(rows // tile,), in_specs=[blk] * 4, out_specs=[blk] * 3,
        out_shape=[jax.ShapeDtypeStruct((rows, lanes), F32)] * 3, compiler_params=_cparams("parallel"),
    )(w, g, m, v)


HBM = pl.BlockSpec(memory_space=pltpu.HBM)


def _place():
    x, y, c = lax.axis_index("x"), lax.axis_index("y"), lax.axis_index("c")
    chips = [(1 - x, y), (x, 1 - y), (1 - x, 1 - y)]
    return x, y, c, chips


def _gather_weights(slots):
    _, _, rh, lanes = slots.shape

    def body(in_ref, out_ref, send_sems, recv_sems):
        del in_ref
        x, y, c, chips = _place()
        me, sibling = (x, y, c), (x, y, 1 - c)

        def copy(k, src, dst, to):
            return pltpu.make_async_remote_copy(src_ref=src, dst_ref=dst, send_sem=send_sems.at[k], recv_sem=recv_sems.at[k],
                                                device_id=to, device_id_type=MESH)

        own = out_ref.at[2 * x + y, c]
        first = [copy(j, own, own, (*chip, c)) for j, chip in enumerate(chips)]
        for cp in first:
            cp.start()
        passed = []
        for j, (cx, cy) in enumerate(chips):
            landed = out_ref.at[2 * cx + cy, c]
            copy(j, landed, landed, me).wait_recv()
            fwd = copy(3 + j, landed, landed, sibling)
            fwd.start()
            passed.append(fwd)
        for j, (cx, cy) in enumerate(chips):
            other = out_ref.at[2 * cx + cy, 1 - c]
            copy(3 + j, other, other, me).wait_recv()
        for cp in first + passed:
            cp.wait_send()

    return pl.pallas_call(
        body, name="gather_weights", out_shape=jax.ShapeDtypeStruct(slots.shape, slots.dtype),
        in_specs=[HBM], out_specs=HBM, input_output_aliases={0: 0},
        scratch_shapes=[pltpu.SemaphoreType.DMA((6,)), pltpu.SemaphoreType.DMA((6,))],
    )(slots)


SEM = pl.BlockSpec(memory_space=pltpu.SEMAPHORE)
EFFECT = pltpu.SideEffectType.DATAFLOW_SIDE_EFFECTING
N_PEER = 3


def _sem_outs():
    return tuple(pltpu.SemaphoreType.DMA(()) for _ in range(2 * N_PEER))


def _gather_start(slots):
    def body(in_ref, *refs):
        del in_ref
        sems, thru, token = refs[:2 * N_PEER], refs[2 * N_PEER], refs[2 * N_PEER + 1]
        x, y, c, chips = _place()
        own = thru.at[2 * x + y, c]
        for j, chip in enumerate(chips):
            pltpu.make_async_remote_copy(src_ref=own, dst_ref=own, send_sem=sems[j], recv_sem=sems[N_PEER + j],
                                         device_id=(*chip, c), device_id_type=MESH).start()
        token[...] = jnp.zeros_like(token)

    out = pl.pallas_call(
        body, name="gather_rest_start",
        out_shape=_sem_outs() + (pltpu.HBM(slots.shape, slots.dtype), jax.ShapeDtypeStruct((8, 128), F32)),
        in_specs=(HBM,), out_specs=(SEM,) * (2 * N_PEER) + (HBM, pl.BlockSpec(memory_space=pltpu.VMEM)),
        input_output_aliases={0: 2 * N_PEER}, compiler_params=pltpu.CompilerParams(has_side_effects=EFFECT),
    )(pltpu.with_memory_space_constraint(slots, pltpu.HBM))
    return out[:2 * N_PEER], out[2 * N_PEER], out[2 * N_PEER + 1]


def _gather_wait(sems, thru, after):
    def body(in_ref, *refs):
        del in_ref
        sems, out_ref = refs[:2 * N_PEER], refs[2 * N_PEER + 1]
        x, y, c, chips = _place()
        own = out_ref.at[2 * x + y, c]
        for j, (cx, cy) in enumerate(chips):
            cp = pltpu.make_async_remote_copy(src_ref=own, dst_ref=out_ref.at[2 * cx + cy, c], send_sem=sems[j],
                                              recv_sem=sems[N_PEER + j], device_id=(cx, cy, c), device_id_type=MESH)
            cp.wait_send()
            cp.wait_recv()

    return pl.pallas_call(
        body, name="gather_rest_wait", out_shape=pltpu.HBM(thru.shape, thru.dtype),
        in_specs=(HBM,) + (SEM,) * (2 * N_PEER) + (pl.BlockSpec(memory_space=pl.ANY),), out_specs=HBM,
        input_output_aliases={0: 0}, compiler_params=pltpu.CompilerParams(has_side_effects=EFFECT),
    )(thru, *sems, after)


def _gather_forward(slots):
    def body(in_ref, out_ref, send_sems, recv_sems):
        del in_ref
        x, y, c, chips = _place()
        cps = []
        for j, (cx, cy) in enumerate(chips):
            landed = out_ref.at[2 * cx + cy, c]
            cps.append(pltpu.make_async_remote_copy(src_ref=landed, dst_ref=landed, send_sem=send_sems.at[j], recv_sem=recv_sems.at[j],
                                                    device_id=(x, y, 1 - c), device_id_type=MESH))
        for cp in cps:
            cp.start()
        for j, (cx, cy) in enumerate(chips):
            other = out_ref.at[2 * cx + cy, 1 - c]
            pltpu.make_async_remote_copy(src_ref=other, dst_ref=other, send_sem=send_sems.at[j], recv_sem=recv_sems.at[j],
                                         device_id=(x, y, 1 - c), device_id_type=MESH).wait_recv()
        for cp in cps:
            cp.wait_send()

    return pl.pallas_call(
        body, name="gather_rest_forward", out_shape=jax.ShapeDtypeStruct(slots.shape, slots.dtype),
        in_specs=[HBM], out_specs=HBM, input_output_aliases={0: 0},
        scratch_shapes=[pltpu.SemaphoreType.DMA((N_PEER,)), pltpu.SemaphoreType.DMA((N_PEER,))],
    )(slots)


def _scatter_start(pa, tag):
    ns, rh, lanes = pa.shape
    land = pltpu.with_memory_space_constraint(lax.empty((N_PEER, rh, lanes), pa.dtype), pltpu.HBM)

    def body(pa_ref, land_ref, *refs):
        x, y, c, chips = _place()
        for j, (cx, cy) in enumerate(chips):
            pltpu.make_async_remote_copy(src_ref=pa_ref.at[2 * cx + cy], dst_ref=land_ref.at[j], send_sem=refs[j],
                                         recv_sem=refs[N_PEER + j], device_id=(cx, cy, c), device_id_type=MESH).start()
        refs[-1][...] = jnp.zeros_like(refs[-1])

    out = pl.pallas_call(
        body, name=f"scatter_start_{tag}",
        out_shape=_sem_outs() + (pltpu.HBM(pa.shape, pa.dtype), pltpu.HBM(land.shape, land.dtype), jax.ShapeDtypeStruct((8, 128), F32)),
        in_specs=(HBM, HBM), out_specs=(SEM,) * (2 * N_PEER) + (HBM, HBM, pl.BlockSpec(memory_space=pltpu.VMEM)),
        input_output_aliases={0: 2 * N_PEER, 1: 2 * N_PEER + 1}, compiler_params=pltpu.CompilerParams(has_side_effects=EFFECT),
    )(pltpu.with_memory_space_constraint(pa, pltpu.HBM), land)
    return (out[:2 * N_PEER], out[2 * N_PEER], out[2 * N_PEER + 1]), out[2 * N_PEER + 2]


def _scatter_wait(sems, pa_thru, land_thru, after, tag):
    def body(pa_ref, land_ref, *refs):
        sems = refs[:2 * N_PEER]
        x, y, c, chips = _place()
        for j, (cx, cy) in enumerate(chips):
            cp = pltpu.make_async_remote_copy(src_ref=pa_ref.at[2 * cx + cy], dst_ref=land_ref.at[j], send_sem=sems[j],
                                              recv_sem=sems[N_PEER + j], device_id=(cx, cy, c), device_id_type=MESH)
            cp.wait_send()
            cp.wait_recv()

    return pl.pallas_call(
        body, name=f"scatter_wait_{tag}",
        out_shape=(pltpu.HBM(pa_thru.shape, pa_thru.dtype), pltpu.HBM(land_thru.shape, land_thru.dtype)),
        in_specs=(HBM, HBM) + (SEM,) * (2 * N_PEER) + (pl.BlockSpec(memory_space=pl.ANY),), out_specs=(HBM, HBM),
        input_output_aliases={0: 0, 1: 1}, compiler_params=pltpu.CompilerParams(has_side_effects=EFFECT),
    )(pa_thru, land_thru, *sems, after)


def _rs_swap(gp, tag):
    _, ns, rh, lanes = gp.shape

    def body(in_ref, out_ref, send_sem, recv_sem):
        x, y, c, _ = _place()
        cp = pltpu.make_async_remote_copy(src_ref=in_ref.at[1 - c], dst_ref=out_ref, send_sem=send_sem, recv_sem=recv_sem,
                                          device_id=(x, y, 1 - c), device_id_type=MESH)
        cp.start()
        cp.wait_send()
        cp.wait_recv()

    return pl.pallas_call(
        body, name=f"rs_swap_{tag}", out_shape=jax.ShapeDtypeStruct((ns, rh, lanes), gp.dtype), in_specs=[HBM], out_specs=HBM,
        scratch_shapes=[pltpu.SemaphoreType.DMA, pltpu.SemaphoreType.DMA],
    )(gp)


def _rs_tile(rh):
    return _pick_tile(rh, cap=512)


def _rs_add(gp, recv, c, tag):
    _, ns, rh, lanes = gp.shape
    tile = _rs_tile(rh)

    def body(c_ref, a_ref, b_ref, o_ref):
        o_ref[...] = (a_ref[...].astype(F32) + b_ref[...].astype(F32)).astype(o_ref.dtype)

    return pl.pallas_call(
        body, name=f"rs_add_{tag}", out_shape=jax.ShapeDtypeStruct((ns, rh, lanes), gp.dtype),
        grid_spec=pltpu.PrefetchScalarGridSpec(
            num_scalar_prefetch=1, grid=(ns, rh // tile),
            in_specs=[pl.BlockSpec((None, None, tile, lanes), lambda s, i, c_ref: (c_ref[0], s, i, 0)),
                      pl.BlockSpec((None, tile, lanes), lambda s, i, c_ref: (s, i, 0))],
            out_specs=pl.BlockSpec((None, tile, lanes), lambda s, i, c_ref: (s, i, 0))),
        compiler_params=_cparams("parallel", "parallel"),
    )(c, gp, recv)


def _rs_sum(pa, recv, place, tag):
    ns, rh, lanes = pa.shape
    tile = _rs_tile(rh)

    def body(place_ref, a_ref, r_ref, o_ref):
        acc = a_ref[...].astype(F32)
        for j in range(ns - 1):
            acc = acc + r_ref[j].astype(F32)
        o_ref[...] = acc

    return pl.pallas_call(
        body, name=f"rs_sum_{tag}", out_shape=jax.ShapeDtypeStruct((2, rh, lanes), F32),
        grid_spec=pltpu.PrefetchScalarGridSpec(
            num_scalar_prefetch=1, grid=(rh // tile,),
            in_specs=[pl.BlockSpec((None, tile, lanes), lambda i, place_ref: (place_ref[0], i, 0)),
                      pl.BlockSpec((ns - 1, tile, lanes), lambda i, place_ref: (0, i, 0))],
            out_specs=pl.BlockSpec((None, tile, lanes), lambda i, place_ref: (place_ref[1], i, 0))),
        compiler_params=_cparams("parallel"),
    )(place, pa, recv)


def _rs_share(halves, tag):
    def body(in_ref, out_ref, send_sem, recv_sem):
        del in_ref
        x, y, c, _ = _place()
        cp = pltpu.make_async_remote_copy(src_ref=out_ref.at[c], dst_ref=out_ref.at[c], send_sem=send_sem, recv_sem=recv_sem,
                                          device_id=(x, y, 1 - c), device_id_type=MESH)
        cp.start()
        other = out_ref.at[1 - c]
        pltpu.make_async_remote_copy(src_ref=other, dst_ref=other, send_sem=send_sem, recv_sem=recv_sem,
                                     device_id=(x, y, 1 - c), device_id_type=MESH).wait_recv()
        cp.wait_send()

    return pl.pallas_call(
        body, name=f"rs_share_{tag}", out_shape=jax.ShapeDtypeStruct(halves.shape, halves.dtype), in_specs=[HBM], out_specs=HBM,
        input_output_aliases={0: 0}, scratch_shapes=[pltpu.SemaphoreType.DMA, pltpu.SemaphoreType.DMA],
    )(halves)


N_DEV = 8
SMALL_ROWS = 160


def _allreduce_small(v):
    m_per, n = v.shape

    def body(x_ref, out_ref, all_ref, send_sems, recv_sems, local_sem):
        x, y, c, chips = _place()
        me, sibling = (x, y, c), (x, y, 1 - c)

        def rows(px, py, pc):
            return all_ref.at[pl.ds((4 * px + 2 * py + pc) * m_per, m_per), :]

        def copy(k, block, to, src=None):
            return pltpu.make_async_remote_copy(src_ref=rows(*block) if src is None else src, dst_ref=rows(*block),
                                                send_sem=send_sems.at[k], recv_sem=recv_sems.at[k], device_id=to, device_id_type=MESH)

        mine = pltpu.make_async_copy(x_ref, rows(*me), local_sem)
        mine.start()
        first = [copy(0, me, sibling, src=x_ref)]
        first += [copy(1 + j, me, (*chip, c), src=x_ref) for j, chip in enumerate(chips)]
        for cp in first:
            cp.start()
        passed = [copy(4 + j, (*chip, c), sibling) for j, chip in enumerate(chips)]
        for j, chip in enumerate(chips):
            copy(1 + j, (*chip, c), me).wait_recv()
            passed[j].start()
        copy(0, sibling, me).wait_recv()
        for j, chip in enumerate(chips):
            copy(4 + j, (*chip, 1 - c), me).wait_recv()
        for cp in first + passed:
            cp.wait_send()
        mine.wait()
        step = 32
        for r in range(0, m_per, step):
            acc = all_ref[r:r + step, :]
            for d in range(1, N_DEV):
                acc = acc + all_ref[d * m_per + r:d * m_per + r + step, :]
            out_ref[r:r + step, :] = acc

    vm = pl.BlockSpec(memory_space=pltpu.VMEM)
    return pl.pallas_call(
        body, name="allreduce_small", out_shape=jax.ShapeDtypeStruct((m_per, n), v.dtype), in_specs=[vm], out_specs=vm,
        scratch_shapes=[pltpu.VMEM((N_DEV * m_per, n), v.dtype), pltpu.SemaphoreType.DMA((7,)), pltpu.SemaphoreType.DMA((7,)),
                        pltpu.SemaphoreType.DMA],
        compiler_params=pltpu.CompilerParams(vmem_limit_bytes=V7X_VMEM_LIMIT),
    )(v)


BIG = {"ffn1_w_gate": ((D_MODEL, D_FF), 1), "ffn1_w_up": ((D_MODEL, D_FF), 1), "ffn1_w_down": ((D_FF, D_MODEL), 0),
       "ffn2_w_gate": ((D_MODEL, D_FF), 1), "ffn2_w_up": ((D_MODEL, D_FF), 1), "ffn2_w_down": ((D_FF, D_MODEL), 0),
       "w_in": ((D_MODEL, IN_WIDTH), 1), "w_mem_kv": ((D_MODEL, 2 * MEM_W), 0), "w_branch_ssd": ((SSD_INNER, D_MODEL), 0),
       "w_branch_gmlp": ((GMLP_W, D_MODEL), 0), "w_branch_mem": ((MEM_W, D_MODEL), 1), "w_out": ((D_MODEL, D_MODEL), 0)}
FFN1 = ("ffn1_w_gate", "ffn1_w_up", "ffn1_w_down")
FFN2 = ("ffn2_w_gate", "ffn2_w_up", "ffn2_w_down")
MIXER = ("w_out", "w_branch_ssd", "w_branch_gmlp", "w_branch_mem", "w_mem_kv", "w_in")
GATHER_GROUPS = (FFN1, FFN2 + MIXER)
REDUCE_GROUPS = (FFN2, MIXER, FFN1)
CONV_W_ROWS = 8


def _shard_rows_of(name):
    (a, b), _ = BIG[name]
    return a * b // N_SHARD // LANES


def _group_rows(names, extra=0):
    return -(-(sum(_shard_rows_of(n) for n in names) + extra) // 32) * 32

SMALL = [("ffn1_norm", 1), ("mix_norm", 1), ("mem_norm", 1), ("ssd_conv_b", 3), ("heads", 1), ("ssd_norm", 2),
         ("gmlp_v_norm", 1), ("gmlp_w_s", 128), ("gmlp_b_s", 1), ("ffn2_norm", 1), ("final_norm", 1), ("ssd_conv_w", 12)]
assert sum(n for _, n in SMALL) <= SMALL_ROWS
HEAD_VECS = ("ssd_dt_bias", "ssd_a_log", "ssd_d")


def _pack_small(vals, loss=None):
    parts = []
    for name, nrows in SMALL:
        if name == "heads":
            row = jnp.concatenate([vals[k].reshape(-1) for k in HEAD_VECS]
                                  + [jnp.zeros((1,), F32) if loss is None else loss.reshape(1)])
            parts.append(jnp.pad(row, (0, LANES - row.shape[0])).reshape(1, LANES))
        elif name in vals:
            parts.append(vals[name].reshape(nrows, LANES))
        else:
            parts.append(jnp.zeros((nrows, LANES), F32))
    buf = jnp.concatenate(parts, axis=0)
    return jnp.pad(buf, ((0, SMALL_ROWS - buf.shape[0]), (0, 0)))


def _unpack_small(buf):
    out, r = {}, 0
    for name, nrows in SMALL:
        blk = buf[r:r + nrows]
        r += nrows
        if name == "heads":
            for i, k in enumerate(HEAD_VECS):
                out[k] = blk[0, i * SSD_HEADS:(i + 1) * SSD_HEADS]
            out["loss"] = blk[0, 3 * SSD_HEADS]
        else:
            out[name] = blk
    return out


def _wire_shape(name):
    (a, b), axis = BIG[name]
    return (b, a) if axis == 1 else (a, b)


def _pack_weights(given, names, conv=False):
    parts = [(given[n][0].T if BIG[n][1] == 1 else given[n][0]).astype(BF16).reshape(_shard_rows_of(n), LANES) for n in names]
    if conv:
        pairs = lax.bitcast_convert_type(given["ssd_conv_w"], BF16).reshape(-1)
        parts.append(jnp.pad(pairs, (0, CONV_W_ROWS * LANES - pairs.shape[0])).reshape(CONV_W_ROWS, LANES))
    total = _group_rows(names, CONV_W_ROWS if conv else 0)
    packed = jnp.concatenate(parts, axis=0)
    packed = jnp.pad(packed, ((0, total - packed.shape[0]), (0, 0))).reshape(1, 2, total // 2, LANES)
    return jnp.broadcast_to(packed, (N_SHARD, 2, total // 2, LANES))


def _unpack_weights(slots, names, conv=False):
    rows = slots.reshape(N_SHARD, -1, LANES)
    out, r = {}, 0
    for name in names:
        n = _shard_rows_of(name)
        out[name] = rows[:, r:r + n].reshape(_wire_shape(name))
        r += n
    if conv:
        cols = XBC // N_SHARD
        pairs = rows[:, r:r + CONV_W_ROWS].reshape(N_SHARD, -1)[:, :SSD_CONV * cols * 2].reshape(N_SHARD, SSD_CONV, cols, 2)
        out["ssd_conv_w"] = jnp.transpose(lax.bitcast_convert_type(pairs, F32), (1, 0, 2)).reshape(SSD_CONV, XBC)
    return out


def _pack_grads(grads, names):
    total = _group_rows(names)
    gp = jnp.concatenate([grads[n].astype(BF16).reshape(N_SHARD, _shard_rows_of(n), LANES) for n in names], axis=1)
    gp = jnp.pad(gp, ((0, 0), (0, total - gp.shape[1]), (0, 0))).reshape(N_SHARD, 2, total // 2, LANES)
    return jnp.transpose(gp, (1, 0, 2, 3))


def kernel(x, mem, ffn1_norm, ffn1_w_gate, ffn1_w_up, ffn1_w_down, mix_norm, mem_norm, w_in, ssd_conv_w, ssd_conv_b, ssd_dt_bias, ssd_a_log, ssd_d, ssd_norm, gmlp_v_norm, gmlp_w_s, gmlp_b_s, w_mem_kv, w_branch_ssd, w_branch_gmlp, w_branch_mem, w_out, ffn2_norm, ffn2_w_gate, ffn2_w_up, ffn2_w_down, final_norm, loss_target, m_ffn1_norm, m_ffn1_w_gate, m_ffn1_w_up, m_ffn1_w_down, m_mix_norm, m_mem_norm, m_w_in, m_ssd_conv_w, m_ssd_conv_b, m_ssd_dt_bias, m_ssd_a_log, m_ssd_d, m_ssd_norm, m_gmlp_v_norm, m_gmlp_w_s, m_gmlp_b_s, m_w_mem_kv, m_w_branch_ssd, m_w_branch_gmlp, m_w_branch_mem, m_w_out, m_ffn2_norm, m_ffn2_w_gate, m_ffn2_w_up, m_ffn2_w_down, m_final_norm, v_ffn1_norm, v_ffn1_w_gate, v_ffn1_w_up, v_ffn1_w_down, v_mix_norm, v_mem_norm, v_w_in, v_ssd_conv_w, v_ssd_conv_b, v_ssd_dt_bias, v_ssd_a_log, v_ssd_d, v_ssd_norm, v_gmlp_v_norm, v_gmlp_w_s, v_gmlp_b_s, v_w_mem_kv, v_w_branch_ssd, v_w_branch_gmlp, v_w_branch_mem, v_w_out, v_ffn2_norm, v_ffn2_w_gate, v_ffn2_w_up, v_ffn2_w_down, v_final_norm):
    given = dict(x=x, mem=mem, ffn1_norm=ffn1_norm, ffn1_w_gate=ffn1_w_gate, ffn1_w_up=ffn1_w_up, ffn1_w_down=ffn1_w_down, mix_norm=mix_norm, mem_norm=mem_norm, w_in=w_in, ssd_conv_w=ssd_conv_w, ssd_conv_b=ssd_conv_b, ssd_dt_bias=ssd_dt_bias, ssd_a_log=ssd_a_log, ssd_d=ssd_d, ssd_norm=ssd_norm, gmlp_v_norm=gmlp_v_norm, gmlp_w_s=gmlp_w_s, gmlp_b_s=gmlp_b_s, w_mem_kv=w_mem_kv, w_branch_ssd=w_branch_ssd, w_branch_gmlp=w_branch_gmlp, w_branch_mem=w_branch_mem, w_out=w_out, ffn2_norm=ffn2_norm, ffn2_w_gate=ffn2_w_gate, ffn2_w_up=ffn2_w_up, ffn2_w_down=ffn2_w_down, final_norm=final_norm, loss_target=loss_target, m_ffn1_norm=m_ffn1_norm, m_ffn1_w_gate=m_ffn1_w_gate, m_ffn1_w_up=m_ffn1_w_up, m_ffn1_w_down=m_ffn1_w_down, m_mix_norm=m_mix_norm, m_mem_norm=m_mem_norm, m_w_in=m_w_in, m_ssd_conv_w=m_ssd_conv_w, m_ssd_conv_b=m_ssd_conv_b, m_ssd_dt_bias=m_ssd_dt_bias, m_ssd_a_log=m_ssd_a_log, m_ssd_d=m_ssd_d, m_ssd_norm=m_ssd_norm, m_gmlp_v_norm=m_gmlp_v_norm, m_gmlp_w_s=m_gmlp_w_s, m_gmlp_b_s=m_gmlp_b_s, m_w_mem_kv=m_w_mem_kv, m_w_branch_ssd=m_w_branch_ssd, m_w_branch_gmlp=m_w_branch_gmlp, m_w_branch_mem=m_w_branch_mem, m_w_out=m_w_out, m_ffn2_norm=m_ffn2_norm, m_ffn2_w_gate=m_ffn2_w_gate, m_ffn2_w_up=m_ffn2_w_up, m_ffn2_w_down=m_ffn2_w_down, m_final_norm=m_final_norm, v_ffn1_norm=v_ffn1_norm, v_ffn1_w_gate=v_ffn1_w_gate, v_ffn1_w_up=v_ffn1_w_up, v_ffn1_w_down=v_ffn1_w_down, v_mix_norm=v_mix_norm, v_mem_norm=v_mem_norm, v_w_in=v_w_in, v_ssd_conv_w=v_ssd_conv_w, v_ssd_conv_b=v_ssd_conv_b, v_ssd_dt_bias=v_ssd_dt_bias, v_ssd_a_log=v_ssd_a_log, v_ssd_d=v_ssd_d, v_ssd_norm=v_ssd_norm, v_gmlp_v_norm=v_gmlp_v_norm, v_gmlp_w_s=v_gmlp_w_s, v_gmlp_b_s=v_gmlp_b_s, v_w_mem_kv=v_w_mem_kv, v_w_branch_ssd=v_w_branch_ssd, v_w_branch_gmlp=v_w_branch_gmlp, v_w_branch_mem=v_w_branch_mem, v_w_out=v_w_out, v_ffn2_norm=v_ffn2_norm, v_ffn2_w_gate=v_ffn2_w_gate, v_ffn2_w_up=v_ffn2_w_up, v_ffn2_w_down=v_ffn2_w_down, v_final_norm=v_final_norm)
    weights = [n for n in given if n not in ("x", "mem", "loss_target") and not n.startswith(("m_", "v_"))]
    xi, yi, ci = lax.axis_index("x"), lax.axis_index("y"), lax.axis_index("c")
    chip = (2 * xi + yi).astype(jnp.int32)
    core = ci.astype(jnp.int32)
    conv_cols = XBC // N_SHARD

    first = _gather_weights(_pack_weights(given, FFN1))
    first, rest_slots = lax.optimization_barrier((first, _pack_weights(given, GATHER_GROUPS[1], conv=True)))
    sems, thru, token = _gather_start(rest_slots)
    w = _unpack_weights(first, FFN1)
    for name in ("ffn1_norm", "mix_norm", "mem_norm", "ssd_conv_b", "ssd_norm", "gmlp_v_norm", "ffn2_norm", "final_norm"):
        w[name] = given[name].reshape(1, -1)
    w["ffn1_norm"] = w["ffn1_norm"] + token[0:1, 0:1]
    for name in HEAD_VECS:
        w[name] = given[name].reshape(-1)
    w["gmlp_w_s"] = given["gmlp_w_s"][0]
    w["gmlp_b_s"] = given["gmlp_b_s"][0]

    def rest_weights(after):
        rest = _unpack_weights(_gather_forward(_gather_wait(sems, thru, after)), GATHER_GROUPS[1], conv=True)
        rest["w_in_p"], rest["w_dt"] = _split_w_in(rest.pop("w_in"))
        return rest

    pending = {}

    def push(k, group_grads):
        gp = _pack_grads(group_grads, REDUCE_GROUPS[k])
        pa = _rs_add(gp, _rs_swap(gp, k), core.reshape(1), k)
        pending[k], token = _scatter_start(pa, k)
        return token[0:1, 0:1]

    def reduced(k, after):
        pa, land = _scatter_wait(*pending[k], after, k)
        gsum = _rs_share(_rs_sum(pa, land, jnp.stack([chip, core]), k), k)
        rows = gsum.reshape(-1, LANES)
        out, r = {}, 0
        for name in REDUCE_GROUPS[k]:
            n = _shard_rows_of(name)
            a, b = given[name].shape[1:]
            out[name] = rows[r:r + n].reshape(b, a).T if BIG[name][1] == 1 else rows[r:r + n].reshape(a, b)
            r += n
        return out

    loss_part, grad_x, g = _local_step(x[0], mem[0], loss_target[0], w, rest_weights, push)

    grads, deltas, new_m, new_v = {}, {}, {}, {}

    def update(k, after):
        for name, gl in reduced(k, after).items():
            d, nm, nv = _adamw(given[name][0], gl, given["m_" + name][0], given["v_" + name][0], f"adamw_{name}")
            grads[name], deltas[name], new_m[name], new_v[name] = (a[None] for a in (gl, d, nm, nv))

    update(0, grad_x)
    update(1, deltas[REDUCE_GROUPS[0][-1]])

    small_vals = {k: g[k] for k, _ in SMALL if k != "heads"}
    small_vals.update({k: g[k] for k in HEAD_VECS})
    red = _unpack_small(_allreduce_small(_pack_small(small_vals, loss=loss_part[0, 0])))
    update(2, deltas[REDUCE_GROUPS[1][-1]])
    conv_g = lax.dynamic_slice_in_dim(red["ssd_conv_w"].reshape(SSD_CONV, XBC), chip * conv_cols, conv_cols, axis=1)
    d, nm, nv = _adamw(given["ssd_conv_w"][0], conv_g, given["m_ssd_conv_w"][0], given["v_ssd_conv_w"][0], "adamw_conv_w")
    grads["ssd_conv_w"], deltas["ssd_conv_w"], new_m["ssd_conv_w"], new_v["ssd_conv_w"] = (a[None] for a in (conv_g, d, nm, nv))
    for k in [k for k, _ in SMALL if k not in ("heads", "ssd_conv_w")] + list(HEAD_VECS):
        shape = given[k].shape
        as2d = lambda a: a.reshape(-1, shape[-1])
        d, nm, nv = _adamw(as2d(given[k]), as2d(red[k]), as2d(given["m_" + k]), as2d(given["v_" + k]), f"adamw_{k}")
        grads[k], deltas[k], new_m[k], new_v[k] = (a.reshape(shape) for a in (red[k], d, nm, nv))

    return (red["loss"], grad_x[None], *[grads[n] for n in weights], *[deltas[n] for n in weights],
            *[new_m[n] for n in weights], *[new_v[n] for n in weights])
```

```python
import functools
import math

import jax
import jax.numpy as jnp
from jax import lax
from jax.experimental import pallas as pl
from jax.experimental.pallas import tpu as pltpu

F32, BF16 = jnp.float32, jnp.bfloat16
HI = lax.Precision.HIGHEST
MESH = pl.DeviceIdType.MESH

D_MODEL = 1024
D_FF = 2816
MEM_LEN = 256
SSD_INNER = 2048
SSD_HEADS = 32
SSD_GROUPS = 4
SSD_STATE = 128
SSD_CONV = 4
CHUNK = 128
XBC = SSD_INNER + 2 * SSD_GROUPS * SSD_STATE
GMLP_W = 1024
GMLP_GROUPS = 8
MEM_W = 256
MEM_HEADS = 4
EPS = 1e-6
IN_WIDTH = 10528
IN_Z, IN_XBC, IN_DT, IN_UV, IN_Q, IN_GL = 0, 2048, 5120, 5152, 7200, 7456
P_Z, P_UV, P_XBC, P_GL, P_Q, P_W = 0, 2048, 4096, 7168, 10240, 10752
P_USED = 10496

ADAM_LR, ADAM_B1, ADAM_B2, ADAM_EPS, ADAM_WD, ADAM_STEP = 0.001, 0.9, 0.999, 1e-08, 0.01, 10

V7X_VMEM_LIMIT = 56 * 1024 * 1024
N_SHARD = 4
LANES = 1024


def _cparams(*sem):
    return pltpu.CompilerParams(dimension_semantics=sem, vmem_limit_bytes=V7X_VMEM_LIMIT)


def _sigmoid(x):
    return 0.5 * jnp.tanh(0.5 * x) + 0.5


def _row_tile(t):
    return min(512, t)


_DIMS = {"nn": (((1,), (0,)), ((), ())), "nt": (((1,), (1,)), ((), ())), "tn": (((0,), (0,)), ((), ()))}


def _matmul(a, b, *, mode, out_dtype, tm, tn, tk, name, scale=1.0, addend=None):
    if mode == "tn":
        k_dim, m_dim = a.shape
    else:
        m_dim, k_dim = a.shape
    n_dim = b.shape[0] if mode == "nt" else b.shape[1]
    tm, tn, tk = min(tm, m_dim), min(tn, n_dim), min(tk, k_dim)
    assert m_dim % tm == 0 and n_dim % tn == 0 and k_dim % tk == 0, (name, a.shape, b.shape, tm, tn, tk)
    ni, nj, nk = m_dim // tm, n_dim // tn, k_dim // tk
    a_spec = pl.BlockSpec((tk, tm), lambda j, i, k: (k, i)) if mode == "tn" else pl.BlockSpec((tm, tk), lambda j, i, k: (i, k))
    b_spec = pl.BlockSpec((tn, tk), lambda j, i, k: (j, k)) if mode == "nt" else pl.BlockSpec((tk, tn), lambda j, i, k: (k, j))
    o_spec = pl.BlockSpec((tm, tn), lambda j, i, k: (i, j))
    dims = _DIMS[mode]
    has_add = addend is not None

    def body(*refs):
        a_ref, b_ref = refs[:2]
        r_ref = refs[2] if has_add else None
        o_ref = refs[2 + has_add]

        def finish(acc):
            r = acc * scale if scale != 1.0 else acc
            if has_add:
                r = r + r_ref[...].astype(F32)
            o_ref[...] = r.astype(o_ref.dtype)

        prod = lax.dot_general(a_ref[...].astype(BF16), b_ref[...].astype(BF16), dims, preferred_element_type=F32)
        if nk == 1:
            finish(prod)
            return
        acc_ref = refs[-1]
        k = pl.program_id(2)

        @pl.when(k == 0)
        def _():
            acc_ref[...] = prod

        @pl.when(k > 0)
        def _():
            acc_ref[...] += prod

        @pl.when(k == nk - 1)
        def _():
            finish(acc_ref[...])

    in_specs = [a_spec, b_spec] + ([o_spec] if has_add else [])
    args = (a, b) + ((addend,) if has_add else ())
    return pl.pallas_call(
        body, name=name, grid=(nj, ni, nk), in_specs=in_specs, out_specs=o_spec,
        out_shape=jax.ShapeDtypeStruct((m_dim, n_dim), out_dtype),
        scratch_shapes=[] if nk == 1 else [pltpu.VMEM((tm, tn), F32)],
        compiler_params=_cparams("parallel", "parallel", "arbitrary"),
    )(*args)


ROW_STRIP = 16


def _strips(tm, fn, init=None, rb=ROW_STRIP):
    def step(i, carry):
        return fn(pl.ds(pl.multiple_of(i * rb, rb), rb), carry)
    return lax.fori_loop(0, tm // rb, step, init, unroll=2)


def _rms_fwd(x, gain, name):
    t, d = x.shape
    tm = _row_tile(t)

    def body(x_ref, g_ref, o_ref):
        xv = x_ref[...]
        r = lax.rsqrt(jnp.mean(xv * xv, axis=-1, keepdims=True) + EPS)
        o_ref[...] = (xv * r * g_ref[...]).astype(o_ref.dtype)

    return pl.pallas_call(
        body, name=name, grid=(t // tm,),
        in_specs=[pl.BlockSpec((tm, d), lambda i: (i, 0)), pl.BlockSpec((1, d), lambda i: (0, 0))],
        out_specs=pl.BlockSpec((tm, d), lambda i: (i, 0)),
        out_shape=jax.ShapeDtypeStruct((t, d), BF16), compiler_params=_cparams("parallel"),
    )(x, gain)


def _rms_bwd(x, gain, dn, dres, name):
    t, d = x.shape
    tm = _row_tile(t)
    has_res = dres is not None

    def body(*refs):
        if has_res:
            x_ref, g_ref, dn_ref, r_ref, dx_ref, dg_ref = refs
        else:
            x_ref, g_ref, dn_ref, dx_ref, dg_ref = refs

        @pl.when(pl.program_id(0) == 0)
        def _():
            dg_ref[...] = jnp.zeros_like(dg_ref)

        xv = x_ref[...]
        r = lax.rsqrt(jnp.mean(xv * xv, axis=-1, keepdims=True) + EPS)
        xh = xv * r
        dnv = dn_ref[...].astype(F32)
        dg_ref[...] += jnp.sum(dnv * xh, axis=0, keepdims=True)
        dxh = dnv * g_ref[...]
        dx = r * (dxh - xh * jnp.mean(dxh * xh, axis=-1, keepdims=True))
        if has_res:
            dx = dx + r_ref[...]
        dx_ref[...] = dx

    row = pl.BlockSpec((tm, d), lambda i: (i, 0))
    vec = pl.BlockSpec((1, d), lambda i: (0, 0))
    in_specs = [row, vec, row] + ([row] if has_res else [])
    args = (x, gain, dn) + ((dres,) if has_res else ())
    return pl.pallas_call(
        body, name=name, grid=(t // tm,), in_specs=in_specs, out_specs=[row, vec],
        out_shape=[jax.ShapeDtypeStruct((t, d), F32), jax.ShapeDtypeStruct((1, d), F32)],
        compiler_params=_cparams("arbitrary"),
    )(*args)


def _loss_head(h, gain, target, name):
    t, d = h.shape
    tm = _row_tile(t)

    def body(h_ref, g_ref, t_ref, dh_ref, dg_ref, l_ref):
        @pl.when(pl.program_id(0) == 0)
        def _():
            dg_ref[...] = jnp.zeros_like(dg_ref)
            l_ref[...] = jnp.zeros_like(l_ref)

        xv = h_ref[...]
        g = g_ref[...]
        r = lax.rsqrt(jnp.mean(xv * xv, axis=-1, keepdims=True) + EPS)
        xh = xv * r
        err = xh * g - t_ref[...]
        l_ref[...] += 0.5 * jnp.sum(jnp.mean(err * err, axis=-1, keepdims=True), axis=0, keepdims=True)
        dy = err * (1.0 / d)
        dg_ref[...] += jnp.sum(dy * xh, axis=0, keepdims=True)
        dxh = dy * g
        dh_ref[...] = r * (dxh - xh * jnp.mean(dxh * xh, axis=-1, keepdims=True))

    row = pl.BlockSpec((tm, d), lambda i: (i, 0))
    vec = pl.BlockSpec((1, d), lambda i: (0, 0))
    return pl.pallas_call(
        body, name=name, grid=(t // tm,), in_specs=[row, vec, row],
        out_specs=[row, vec, pl.BlockSpec((1, 128), lambda i: (0, 0))],
        out_shape=[jax.ShapeDtypeStruct((t, d), F32), jax.ShapeDtypeStruct((1, d), F32), jax.ShapeDtypeStruct((1, 128), F32)],
        compiler_params=_cparams("arbitrary"),
    )(h, gain, target)


FF_TILE = 1408


def _ffn_fwd(n, x, wg, wu, wd, name):
    t, d = x.shape
    tm, tn = _row_tile(t), FF_TILE
    nj = D_FF // tn

    def body(n_ref, x_ref, wg_ref, wu_ref, wd_ref, h_ref, g_ref, u_ref, acc_ref):
        j = pl.program_id(1)

        @pl.when(j == 0)
        def _():
            acc_ref[...] = jnp.zeros_like(acc_ref)

        nb = n_ref[...]
        g = lax.dot_general(nb, wg_ref[...], _DIMS["nt"], preferred_element_type=F32)
        u = lax.dot_general(nb, wu_ref[...], _DIMS["nt"], preferred_element_type=F32)
        g_ref[...] = g.astype(BF16)
        u_ref[...] = u.astype(BF16)
        a = g * _sigmoid(g) * u
        acc_ref[...] += jnp.dot(a.astype(BF16), wd_ref[...], preferred_element_type=F32)

        @pl.when(j == nj - 1)
        def _():
            h_ref[...] = x_ref[...] + 0.5 * acc_ref[...]

    row = pl.BlockSpec((tm, d), lambda i, j: (i, 0))
    act = pl.BlockSpec((tm, tn), lambda i, j: (i, j))
    return pl.pallas_call(
        body, name=name, grid=(t // tm, nj),
        in_specs=[row, row] + [pl.BlockSpec((tn, d), lambda i, j: (j, 0))] * 3,
        out_specs=[row, act, act],
        out_shape=[jax.ShapeDtypeStruct((t, d), F32), jax.ShapeDtypeStruct((t, D_FF), BF16), jax.ShapeDtypeStruct((t, D_FF), BF16)],
        scratch_shapes=[pltpu.VMEM((tm, d), F32)], compiler_params=_cparams("parallel", "arbitrary"),
    )(n, x, wg, wu, wd)


def _ffn_bwd_act(dh, g, u, wg, wu, wd, name):
    t, d = dh.shape
    tm, tn = _row_tile(t), FF_TILE
    nj = D_FF // tn

    def body(dh_ref, g_ref, u_ref, wg_ref, wu_ref, wd_ref, dn_ref, dg_ref, du_ref, a_ref, acc_ref):
        j = pl.program_id(1)

        @pl.when(j == 0)
        def _():
            acc_ref[...] = jnp.zeros_like(acc_ref)

        dhb = (0.5 * dh_ref[...]).astype(BF16)
        da = lax.dot_general(dhb, wd_ref[...], _DIMS["nt"], preferred_element_type=F32)
        gv = g_ref[...].astype(F32)
        uv = u_ref[...].astype(F32)
        sg = _sigmoid(gv)
        s = gv * sg
        dg = (da * uv * (sg * (1.0 + gv * (1.0 - sg)))).astype(BF16)
        du = (da * s).astype(BF16)
        dg_ref[...] = dg
        du_ref[...] = du
        a_ref[...] = (s * uv).astype(BF16)
        acc_ref[...] += (jnp.dot(dg, wg_ref[...], preferred_element_type=F32)
                         + jnp.dot(du, wu_ref[...], preferred_element_type=F32))

        @pl.when(j == nj - 1)
        def _():
            dn_ref[...] = acc_ref[...]

    row = pl.BlockSpec((tm, d), lambda i, j: (i, 0))
    act = pl.BlockSpec((tm, tn), lambda i, j: (i, j))
    return pl.pallas_call(
        body, name=name, grid=(t // tm, nj),
        in_specs=[row, act, act] + [pl.BlockSpec((tn, d), lambda i, j: (j, 0))] * 3,
        out_specs=[row, act, act, act],
        out_shape=[jax.ShapeDtypeStruct((t, d), F32)] + [jax.ShapeDtypeStruct((t, D_FF), BF16)] * 3,
        scratch_shapes=[pltpu.VMEM((tm, d), F32)], compiler_params=_cparams("parallel", "arbitrary"),
    )(dh, g, u, wg, wu, wd)


def _ffn_forward(x, gain, wg, wu, wd, tag):
    n = _rms_fwd(x, gain, f"{tag}_norm")
    h, g, u = _ffn_fwd(n, x, wg, wu, wd, f"{tag}_fwd")
    return h, (n, g, u)


def _ffn_backward(dh, x, gain, wg, wu, wd, saved, tag, push):
    n, g, u = saved
    dn, dg, du, a = _ffn_bwd_act(dh, g, u, wg, wu, wd, f"{tag}_bwd_act")
    kw = dict(mode="tn", out_dtype=BF16, tm=FF_TILE, tn=1024, tk=2048)
    d_wg = _matmul(dg, n, name=f"{tag}_dwg", **kw)
    d_wu = _matmul(du, n, name=f"{tag}_dwu", **kw)
    d_wd = _matmul(a, dh, scale=0.5, name=f"{tag}_dwd", **kw)
    token = push({f"{tag}_w_gate": d_wg, f"{tag}_w_up": d_wu, f"{tag}_w_down": d_wd})
    return _rms_bwd(x, gain + token, dn, dh, f"{tag}_norm_bwd")


CONV_COLS = 512
HALO = 8
CONV_STRIP = 32
CONV_ROWS = 1024


def _conv_fwd(p, w, b, name):
    t = p.shape[0]
    tm = min(CONV_ROWS, t)
    c0 = P_XBC // CONV_COLS

    def body(x_ref, halo_ref, w_ref, b_ref, o_ref, s_ref):
        i = pl.program_id(1)
        s_ref[0:HALO, :] = jnp.where(i > 0, halo_ref[...].astype(F32), 0.0)
        s_ref[HALO:HALO + tm, :] = x_ref[...].astype(F32)
        wv = w_ref[...]
        bv = b_ref[...]
        for r0 in range(0, tm, CONV_STRIP):
            acc = bv + wv[0:1, :] * s_ref[HALO - 3 + r0:HALO - 3 + r0 + CONV_STRIP, :]
            for k in range(1, SSD_CONV):
                acc = acc + wv[k:k + 1, :] * s_ref[HALO - 3 + k + r0:HALO - 3 + k + r0 + CONV_STRIP, :]
            o_ref[r0:r0 + CONV_STRIP, :] = (acc * _sigmoid(acc)).astype(o_ref.dtype)

    return pl.pallas_call(
        body, name=name, grid=(XBC // CONV_COLS, t // tm),
        in_specs=[pl.BlockSpec((tm, CONV_COLS), lambda j, i: (i, c0 + j)),
                  pl.BlockSpec((HALO, CONV_COLS), lambda j, i: (jnp.maximum(i * (tm // HALO) - 1, 0), c0 + j)),
                  pl.BlockSpec((SSD_CONV, CONV_COLS), lambda j, i: (0, j)), pl.BlockSpec((1, CONV_COLS), lambda j, i: (0, j))],
        out_specs=pl.BlockSpec((tm, CONV_COLS), lambda j, i: (i, j)),
        out_shape=jax.ShapeDtypeStruct((t, XBC), BF16),
        scratch_shapes=[pltpu.VMEM((tm + HALO, CONV_COLS), F32)], compiler_params=_cparams("parallel", "parallel"),
    )(p, p, w, b)


def _conv_bwd_act(p, dy, w, b, col0, name):
    t, cols = dy.shape
    tm = min(CONV_ROWS, t)
    c0 = (P_XBC + col0) // CONV_COLS
    w0 = col0 // CONV_COLS

    def body(x_ref, halo_ref, dy_ref, w_ref, b_ref, da_ref, dw_ref, db_ref, s_ref):
        i = pl.program_id(1)

        @pl.when(i == 0)
        def _():
            dw_ref[...] = jnp.zeros_like(dw_ref)
            db_ref[...] = jnp.zeros_like(db_ref)

        s_ref[0:HALO, :] = jnp.where(i > 0, halo_ref[...].astype(F32), 0.0)
        s_ref[HALO:HALO + tm, :] = x_ref[...].astype(F32)
        wv = w_ref[...]
        bv = b_ref[...]
        fold = lambda v: jnp.sum(v.reshape(CONV_STRIP // 8, 8, CONV_COLS), axis=0)
        sums = [jnp.zeros((8, CONV_COLS), F32) for _ in range(SSD_CONV + 1)]
        for r0 in range(0, tm, CONV_STRIP):
            taps = [s_ref[HALO - 3 + k + r0:HALO - 3 + k + r0 + CONV_STRIP, :] for k in range(SSD_CONV)]
            acc = bv + wv[0:1, :] * taps[0]
            for k in range(1, SSD_CONV):
                acc = acc + wv[k:k + 1, :] * taps[k]
            sg = _sigmoid(acc)
            dacc = dy_ref[r0:r0 + CONV_STRIP, :].astype(F32) * (sg * (1.0 + acc * (1.0 - sg)))
            da_ref[r0:r0 + CONV_STRIP, :] = dacc.astype(BF16)
            for k in range(SSD_CONV):
                sums[k] = sums[k] + fold(dacc * taps[k])
            sums[SSD_CONV] = sums[SSD_CONV] + fold(dacc)
        for k in range(SSD_CONV):
            dw_ref[k:k + 1, :] += jnp.sum(sums[k], axis=0, keepdims=True)
        db_ref[...] += jnp.sum(sums[SSD_CONV], axis=0, keepdims=True)

    return pl.pallas_call(
        body, name=name, grid=(cols // CONV_COLS, t // tm),
        in_specs=[pl.BlockSpec((tm, CONV_COLS), lambda j, i: (i, c0 + j)),
                  pl.BlockSpec((HALO, CONV_COLS), lambda j, i: (jnp.maximum(i * (tm // HALO) - 1, 0), c0 + j)),
                  pl.BlockSpec((tm, CONV_COLS), lambda j, i: (i, j)),
                  pl.BlockSpec((SSD_CONV, CONV_COLS), lambda j, i: (0, w0 + j)), pl.BlockSpec((1, CONV_COLS), lambda j, i: (0, w0 + j))],
        out_specs=[pl.BlockSpec((tm, CONV_COLS), lambda j, i: (i, j)), pl.BlockSpec((SSD_CONV, CONV_COLS), lambda j, i: (0, j)),
                   pl.BlockSpec((1, CONV_COLS), lambda j, i: (0, j))],
        out_shape=[jax.ShapeDtypeStruct((t, cols), BF16), jax.ShapeDtypeStruct((SSD_CONV, cols), F32), jax.ShapeDtypeStruct((1, cols), F32)],
        scratch_shapes=[pltpu.VMEM((tm + HALO, CONV_COLS), F32)], compiler_params=_cparams("parallel", "arbitrary"),
    )(p, p, dy, w, b)


def _conv_bwd_dx(dacc, w, col0, name):
    t, cols = dacc.shape
    tm = min(CONV_ROWS, t)
    nt = t // tm
    w0 = col0 // CONV_COLS

    def body(d_ref, halo_ref, w_ref, o_ref, s_ref):
        i = pl.program_id(1)
        s_ref[0:tm, :] = d_ref[...].astype(F32)
        s_ref[tm:tm + HALO, :] = jnp.where(i < nt - 1, halo_ref[...].astype(F32), 0.0)
        wv = w_ref[...]
        for r0 in range(0, tm, CONV_STRIP):
            acc = wv[3:4, :] * s_ref[r0:r0 + CONV_STRIP, :]
            for k in range(SSD_CONV - 1):
                acc = acc + wv[k:k + 1, :] * s_ref[3 - k + r0:3 - k + r0 + CONV_STRIP, :]
            o_ref[r0:r0 + CONV_STRIP, :] = acc.astype(o_ref.dtype)

    return pl.pallas_call(
        body, name=name, grid=(cols // CONV_COLS, nt),
        in_specs=[pl.BlockSpec((tm, CONV_COLS), lambda j, i: (i, j)),
                  pl.BlockSpec((HALO, CONV_COLS), lambda j, i: (jnp.minimum((i + 1) * (tm // HALO), t // HALO - 1), j)),
                  pl.BlockSpec((SSD_CONV, CONV_COLS), lambda j, i: (0, w0 + j))],
        out_specs=pl.BlockSpec((tm, CONV_COLS), lambda j, i: (i, j)),
        out_shape=jax.ShapeDtypeStruct((t, cols), BF16),
        scratch_shapes=[pltpu.VMEM((tm + HALO, CONV_COLS), F32)], compiler_params=_cparams("parallel", "parallel"),
    )(dacc, dacc, w)


GROUP_COLS = SSD_INNER // SSD_GROUPS
PAIRS = GROUP_COLS // 128
HEADS_PER_GROUP = SSD_HEADS // SSD_GROUPS


def _dt_fwd(dt_raw, bias, name):
    t, n = dt_raw.shape
    tm = _row_tile(t)

    def body(x_ref, b_ref, o_ref):
        v = x_ref[...] + b_ref[...]
        o_ref[...] = jnp.maximum(v, 0.0) + jnp.log1p(jnp.exp(-jnp.abs(v)))

    row = pl.BlockSpec((tm, n), lambda i: (i, 0))
    vec = pl.BlockSpec((1, n), lambda i: (0, 0))
    return pl.pallas_call(body, name=name, grid=(t // tm,), in_specs=[row, vec], out_specs=row,
                          out_shape=jax.ShapeDtypeStruct((t, n), F32), compiler_params=_cparams("parallel"))(dt_raw, bias)


def _dt_bwd(ddt, dt_raw, bias, name):
    t, n = dt_raw.shape
    tm = _row_tile(t)

    def body(d_ref, x_ref, b_ref, o_ref, db_ref):
        @pl.when(pl.program_id(0) == 0)
        def _():
            db_ref[...] = jnp.zeros_like(db_ref)

        dr = d_ref[...] * _sigmoid(x_ref[...] + b_ref[...])
        o_ref[...] = dr.astype(o_ref.dtype)
        db_ref[...] += jnp.sum(dr, axis=0, keepdims=True)

    row = pl.BlockSpec((tm, n), lambda i: (i, 0))
    vec = pl.BlockSpec((1, n), lambda i: (0, 0))
    return pl.pallas_call(body, name=name, grid=(t // tm,), in_specs=[row, row, vec], out_specs=[row, vec],
                          out_shape=[jax.ShapeDtypeStruct((t, n), BF16), jax.ShapeDtypeStruct((1, n), F32)],
                          compiler_params=_cparams("arbitrary"))(ddt, dt_raw, bias)


SSD_STEP = 4


def _ssd_common(dt, dtt, a_log_w, a_log_t):
    l = CHUNK
    a = -jnp.exp(a_log_w)
    at = -jnp.exp(a_log_t)
    rowi = lax.broadcasted_iota(jnp.int32, (l, l), 0)
    coli = lax.broadcasted_iota(jnp.int32, (l, l), 1)
    tri = rowi >= coli
    lower = tri.astype(F32)
    upper = (rowi <= coli).astype(F32)
    acs = jnp.dot(lower, dt * a, precision=HI, preferred_element_type=F32)
    acst = jnp.dot(dtt * at, upper, precision=HI, preferred_element_type=F32)
    return a, acs, acst, tri, upper


def _pair_bc(w, lo, p):
    return jnp.where(lo, w[:, 2 * p:2 * p + 1], w[:, 2 * p + 1:2 * p + 2])


def _ssd_specs(t):
    rows = SSD_STEP * CHUNK
    assert t % rows == 0
    return t // rows, dict(
        xs=lambda cm: pl.BlockSpec((rows, GROUP_COLS), lambda g, c: (cm(c), g)),
        bm=lambda cm: pl.BlockSpec((rows, SSD_STATE), lambda g, c: (cm(c), SSD_INNER // SSD_STATE + g)),
        cmat=lambda cm: pl.BlockSpec((rows, SSD_STATE), lambda g, c: (cm(c), SSD_INNER // SSD_STATE + SSD_GROUPS + g)),
        dtw=lambda cm: pl.BlockSpec((1, rows, 128), lambda g, c: (g, cm(c), 0)),
        dtt=lambda cm: pl.BlockSpec((1, HEADS_PER_GROUP, rows), lambda g, c: (g, 0, cm(c))),
        wide=lambda cm: pl.BlockSpec((1, 1, 128), lambda g, c: (g, 0, 0)),
        tall=lambda cm: pl.BlockSpec((1, HEADS_PER_GROUP, 1), lambda g, c: (g, 0, 0)),
        grp=lambda cm: pl.BlockSpec((rows, GROUP_COLS), lambda g, c: (cm(c), g)),
        vec=lambda cm: pl.BlockSpec((1, GROUP_COLS), lambda g, c: (0, g)),
        state=lambda cm: pl.BlockSpec((1, SSD_STEP, PAIRS, SSD_STATE, 128), lambda g, c: (g, cm(c), 0, 0, 0)),
    )


def _ssd_fwd(xc, p, hv, norm_g, name):
    t = xc.shape[0]
    nc, sp = _ssd_specs(t)
    ident = lambda c: c

    def body(xs_ref, b_ref, c_ref, dtw_ref, dtt_ref, aw_ref, at_ref, dk_ref, z_ref, ng_ref,
             y_ref, ys_ref, h_ref, st_ref):
        @pl.when(pl.program_id(1) == 0)
        def _():
            st_ref[...] = jnp.zeros_like(st_ref)

        lo = lax.broadcasted_iota(jnp.int32, (1, 128), 1) < 64
        dskip = dk_ref[0]
        for s in range(SSD_STEP):
            rows = slice(s * CHUNK, (s + 1) * CHUNK)
            dt = dtw_ref[0, rows, :]
            a, acs, acst, tri, _ = _ssd_common(dt, dtt_ref[0, :, rows], aw_ref[0], at_ref[0])
            ecs = jnp.exp(acs)
            alast = acs[CHUNK - 1:CHUNK, :]
            bmat, cmat = b_ref[rows, :], c_ref[rows, :]
            cb = lax.dot_general(cmat, bmat, _DIMS["nt"], preferred_element_type=F32)
            for pi in range(PAIRS):
                cols = slice(pi * 128, (pi + 1) * 128)
                x = xs_ref[rows, cols].astype(F32)
                xdt = x * _pair_bc(dt, lo, pi)
                ydiag = jnp.zeros((CHUNK, 128), F32)
                for r, mask in ((2 * pi, lo), (2 * pi + 1, jnp.logical_not(lo))):
                    lam = jnp.exp(jnp.where(tri, acs[:, r:r + 1] - acst[r:r + 1, :], -1e30))
                    m = (cb * lam).astype(BF16)
                    ydiag = ydiag + jnp.dot(m, jnp.where(mask, xdt, 0.0).astype(BF16), preferred_element_type=F32)
                ht = st_ref[pi]
                h_ref[0, s, pi] = ht
                yoff = jnp.dot(cmat, ht.astype(BF16), preferred_element_type=F32) * _pair_bc(ecs, lo, pi)
                y_ref[rows, cols] = (ydiag + yoff + _pair_bc(dskip, lo, pi) * x).astype(y_ref.dtype)
                alp = _pair_bc(alast, lo, pi)
                e = jnp.exp(alp - _pair_bc(acs, lo, pi))
                st = lax.dot_general(bmat, (xdt * e).astype(BF16), _DIMS["tn"], preferred_element_type=F32)
                st_ref[pi] = ht * jnp.exp(alp) + st
            zf = z_ref[rows, :].astype(F32)
            yg = y_ref[rows, :].astype(F32) * (zf * _sigmoid(zf))
            rstd = lax.rsqrt(jnp.mean(yg * yg, axis=-1, keepdims=True) + EPS)
            ys_ref[rows, :] = (yg * rstd * ng_ref[...]).astype(ys_ref.dtype)

    ins = ["xs", "bm", "cmat", "dtw", "dtt", "wide", "tall", "wide", "grp", "vec"]
    return pl.pallas_call(
        body, name=name, grid=(SSD_GROUPS, nc),
        in_specs=[sp[k](ident) for k in ins],
        out_specs=[sp["grp"](ident), sp["grp"](ident), sp["state"](ident)],
        out_shape=[jax.ShapeDtypeStruct((t, SSD_INNER), BF16), jax.ShapeDtypeStruct((t, SSD_INNER), BF16),
                   jax.ShapeDtypeStruct((SSD_GROUPS, t // CHUNK, PAIRS, SSD_STATE, 128), F32)],
        scratch_shapes=[pltpu.VMEM((PAIRS, SSD_STATE, 128), F32)], compiler_params=_cparams("parallel", "arbitrary"),
    )(xc, xc, xc, hv["dtw"], hv["dtt"], hv["alog_w"], hv["alog_t"], hv["dskip_w"], p, norm_g)


def _ssd_bwd(xc, p, hv, norm_g, y, dys, states, name):
    t = xc.shape[0]
    nc, sp = _ssd_specs(t)
    rev = lambda c: nc - 1 - c

    def body(xs_ref, b_ref, c_ref, dtw_ref, dtt_ref, aw_ref, at_ref, dk_ref, z_ref, ng_ref,
             y_ref, dys_ref, h_ref,
             dxs_ref, db_ref, dc_ref, dz_ref, ddt_ref, hsum_ref, dng_ref, dst_ref):
        @pl.when(pl.program_id(1) == 0)
        def _():
            dst_ref[...] = jnp.zeros_like(dst_ref)
            hsum_ref[...] = jnp.zeros_like(hsum_ref)
            dng_ref[...] = jnp.zeros_like(dng_ref)

        lane = lax.broadcasted_iota(jnp.int32, (1, 128), 1)
        lo = lane < 64
        dskip = dk_ref[0]
        sel_r = lax.broadcasted_iota(jnp.int32, (128, 128), 0)
        sel_c = lax.broadcasted_iota(jnp.int32, (128, 128), 1)
        refs = (xs_ref, b_ref, c_ref, dtw_ref, dtt_ref, aw_ref, at_ref, dk_ref, z_ref, ng_ref, y_ref, dys_ref, h_ref,
                dxs_ref, db_ref, dc_ref, dz_ref, ddt_ref, hsum_ref, dng_ref, dst_ref)
        for s in reversed(range(SSD_STEP)):
            chunk_bwd(refs, slice(s * CHUNK, (s + 1) * CHUNK), s, lane, lo, dskip, sel_r, sel_c)

    def chunk_bwd(refs, rows, s, lane, lo, dskip, sel_r, sel_c):
        (xs_ref, b_ref, c_ref, dtw_ref, dtt_ref, aw_ref, at_ref, dk_ref, z_ref, ng_ref, y_ref, dys_ref, h_ref,
         dxs_ref, db_ref, dc_ref, dz_ref, ddt_ref, hsum_ref, dng_ref, dst_ref) = refs
        dt = dtw_ref[0, rows, :]
        a, acs, acst, tri, upper = _ssd_common(dt, dtt_ref[0, :, rows], aw_ref[0], at_ref[0])
        ecs = jnp.exp(acs)
        alast = acs[CHUNK - 1:CHUNK, :]
        bmat, cmat = b_ref[rows, :], c_ref[rows, :]
        cb = lax.dot_general(cmat, bmat, _DIMS["nt"], preferred_element_type=F32)

        zf = z_ref[rows, :].astype(F32)
        sg = _sigmoid(zf)
        sz = zf * sg
        yv = y_ref[rows, :].astype(F32)
        yg = yv * sz
        rstd = lax.rsqrt(jnp.mean(yg * yg, axis=-1, keepdims=True) + EPS)
        yhat = yg * rstd
        dysv = dys_ref[rows, :].astype(F32)
        dng_ref[...] += jnp.sum(dysv * yhat, axis=0, keepdims=True)
        dyh = dysv * ng_ref[...]
        dyg = rstd * (dyh - yhat * jnp.mean(dyh * yhat, axis=-1, keepdims=True))
        dz_ref[rows, :] = (dyg * yv * (sg * (1.0 + zf * (1.0 - sg)))).astype(dz_ref.dtype)
        dy_all = dyg * sz

        dal = jnp.zeros((CHUNK, 128), F32)
        ddtm = jnp.zeros((CHUNK, 128), F32)
        dalast = jnp.zeros((8, 128), F32)
        ddsk = jnp.zeros((8, 128), F32)
        dcb = jnp.zeros((CHUNK, CHUNK), F32)
        qcol = jnp.zeros((8, CHUNK), F32)
        sub8 = lax.broadcasted_iota(jnp.int32, (8, CHUNK), 0)
        dc_acc = jnp.zeros((CHUNK, SSD_STATE), F32)
        db_acc = jnp.zeros((CHUNK, SSD_STATE), F32)
        for pi in range(PAIRS):
            sel = (sel_c == 2 * pi + (sel_r >= 64).astype(jnp.int32)).astype(BF16)

            def hsum(v, sel=sel):
                return jnp.dot(v.astype(BF16), sel, preferred_element_type=F32)

            dyp = dy_all[:, pi * 128:(pi + 1) * 128]
            x = xs_ref[rows, pi * 128:(pi + 1) * 128].astype(F32)
            dtp = _pair_bc(dt, lo, pi)
            xdt = x * dtp
            dxdt = jnp.zeros((CHUNK, 128), F32)
            for r, mask in ((2 * pi, lo), (2 * pi + 1, jnp.logical_not(lo))):
                lam = jnp.exp(jnp.where(tri, acs[:, r:r + 1] - acst[r:r + 1, :], -1e30))
                m32 = cb * lam
                m = m32.astype(BF16)
                dyr = jnp.where(mask, dyp, 0.0).astype(BF16)
                xr = jnp.where(mask, xdt, 0.0).astype(BF16)
                dm = lax.dot_general(dyr, xr, _DIMS["nt"], preferred_element_type=F32)
                dcb = dcb + dm * lam
                q = dm * m32
                dal = dal + jnp.sum(q, axis=1, keepdims=True) * (lane == r).astype(F32)
                qcol = qcol + jnp.where(sub8 == r, jnp.sum(q, axis=0, keepdims=True), 0.0)
                dxdt = dxdt + lax.dot_general(m, dyr, _DIMS["tn"], preferred_element_type=F32)
            ht = h_ref[0, s, pi]
            htb = ht.astype(BF16)
            ecp = _pair_bc(ecs, lo, pi)
            yoff = jnp.dot(cmat, htb, preferred_element_type=F32) * ecp
            dg = (dyp * ecp).astype(BF16)
            dc_acc = dc_acc + lax.dot_general(dg, htb, _DIMS["nt"], preferred_element_type=F32)
            dht = lax.dot_general(cmat, dg, _DIMS["tn"], preferred_element_type=F32)
            dal = dal + hsum(dyp * yoff)
            dhn = dst_ref[pi]
            dhnb = dhn.astype(BF16)
            alp = _pair_bc(alast, lo, pi)
            e = jnp.exp(alp - _pair_bc(acs, lo, pi))
            xe = xdt * e
            db_acc = db_acc + lax.dot_general(xe.astype(BF16), dhnb, _DIMS["nt"], preferred_element_type=F32)
            dxe = jnp.dot(bmat, dhnb, preferred_element_type=F32)
            dxdt = dxdt + dxe * e
            tt = hsum(dxe * xe)
            dal = dal - tt
            dec = jnp.exp(alp)
            dalast = dalast + jnp.sum(tt, axis=0, keepdims=True) + hsum(
                jnp.broadcast_to(jnp.sum(dhn * ht, axis=0, keepdims=True) * dec, (8, 128)))
            dst_ref[pi] = dht + dhn * dec
            dxs_ref[rows, pi * 128:(pi + 1) * 128] = (_pair_bc(dskip, lo, pi) * dyp + dxdt * dtp).astype(dxs_ref.dtype)
            ddtm = ddtm + hsum(dxdt * x)
            ddsk = ddsk + hsum(jnp.broadcast_to(jnp.sum(dyp * x, axis=0, keepdims=True), (8, 128)))
        rowi = lax.broadcasted_iota(jnp.int32, (CHUNK, 128), 0)
        qcol_w = lax.dot_general(jnp.concatenate([qcol, jnp.zeros((CHUNK - 8, CHUNK), F32)], axis=0), (sel_r == sel_c).astype(F32),
                                 _DIMS["tn"], precision=HI, preferred_element_type=F32)
        dal = dal - qcol_w + jnp.where(rowi == CHUNK - 1, dalast[0:1, :], 0.0)
        dda = jnp.dot(upper, dal, precision=HI, preferred_element_type=F32)
        ddt_ref[0, rows, :] = ddtm + dda * a
        hsum_ref[0, 1:2, :] += jnp.sum(dda * dt, axis=0, keepdims=True) * a
        hsum_ref[0, 2:3, :] += ddsk[0:1, :]
        dcbb = dcb.astype(BF16)
        dc_ref[rows, :] = (jnp.dot(dcbb, bmat, preferred_element_type=F32) + dc_acc).astype(dc_ref.dtype)
        db_ref[rows, :] = (lax.dot_general(dcbb, cmat, _DIMS["tn"], preferred_element_type=F32) + db_acc).astype(db_ref.dtype)

    ins = ["xs", "bm", "cmat", "dtw", "dtt", "wide", "tall", "wide", "grp", "vec", "grp", "grp", "state"]
    col = lambda: pl.BlockSpec((SSD_STEP * CHUNK, SSD_STATE), lambda g, c: (rev(c), g))
    return pl.pallas_call(
        body, name=name, grid=(SSD_GROUPS, nc),
        in_specs=[sp[k](rev) for k in ins],
        out_specs=[sp["grp"](rev), col(), col(), sp["grp"](rev), sp["dtw"](rev),
                   pl.BlockSpec((1, 8, 128), lambda g, c: (g, 0, 0)), sp["vec"](rev)],
        out_shape=[jax.ShapeDtypeStruct((t, SSD_INNER), BF16), jax.ShapeDtypeStruct((t, SSD_GROUPS * SSD_STATE), BF16),
                   jax.ShapeDtypeStruct((t, SSD_GROUPS * SSD_STATE), BF16), jax.ShapeDtypeStruct((t, SSD_INNER), BF16),
                   jax.ShapeDtypeStruct((SSD_GROUPS, t, 128), F32), jax.ShapeDtypeStruct((SSD_GROUPS, 8, 128), F32),
                   jax.ShapeDtypeStruct((1, SSD_INNER), F32)],
        scratch_shapes=[pltpu.VMEM((PAIRS, SSD_STATE, 128), F32)], compiler_params=_cparams("parallel", "arbitrary"),
    )(xc, xc, xc, hv["dtw"], hv["dtt"], hv["alog_w"], hv["alog_t"], hv["dskip_w"], p, norm_g, y, dys, states)


def _wide(v):
    return jnp.pad(v.reshape(SSD_GROUPS, 1, HEADS_PER_GROUP), ((0, 0), (0, 0), (0, 128 - HEADS_PER_GROUP)))


def _head_inputs(dt, a_log, d_skip):
    t = dt.shape[0]
    g = dt[:, :SSD_HEADS].reshape(t, SSD_GROUPS, HEADS_PER_GROUP)
    return dict(
        dtw=jnp.pad(jnp.transpose(g, (1, 0, 2)), ((0, 0), (0, 0), (0, 128 - HEADS_PER_GROUP))),
        dtt=jnp.transpose(g, (1, 2, 0)),
        alog_w=_wide(a_log), alog_t=a_log.reshape(SSD_GROUPS, HEADS_PER_GROUP, 1),
        dskip_w=_wide(d_skip),
    )


def _gelu(x):
    return 0.5 * x * (1.0 + lax.erf(x * (1.0 / math.sqrt(2.0))))


def _gelu_grad(x):
    return 0.5 * (1.0 + lax.erf(x * (1.0 / math.sqrt(2.0)))) + x * jnp.exp(-0.5 * x * x) * (1.0 / math.sqrt(2.0 * math.pi))


def _tril_mask():
    r = lax.broadcasted_iota(jnp.int32, (CHUNK, CHUNK), 0)
    c = lax.broadcasted_iota(jnp.int32, (CHUNK, CHUNK), 1)
    return r >= c


def _gmlp_fwd(p, v_gain, w_s, b_col, name):
    t = p.shape[0]
    tm = _row_tile(t)
    u0 = P_UV // GMLP_W

    def body(u_ref, v_ref, gn_ref, ws_ref, bs_ref, o_ref):
        v = _gelu(v_ref[...].astype(F32))
        v = (v * lax.rsqrt(jnp.mean(v * v, axis=-1, keepdims=True) + EPS) * gn_ref[...]).astype(BF16)
        tril = _tril_mask()
        wm = [jnp.where(tril, ws_ref[g], 0.0).astype(BF16) for g in range(GMLP_GROUPS)]
        for k in range(tm // CHUNK):
            rows = slice(k * CHUNK, (k + 1) * CHUNK)
            for g in range(GMLP_GROUPS):
                cols = slice(g * 128, (g + 1) * 128)
                mixed = jnp.dot(wm[g], v[rows, cols], preferred_element_type=F32) + bs_ref[g]
                o_ref[rows, cols] = (_gelu(u_ref[rows, cols].astype(F32)) * mixed).astype(o_ref.dtype)

    return pl.pallas_call(
        body, name=name, grid=(t // tm,),
        in_specs=[pl.BlockSpec((tm, GMLP_W), lambda i: (i, u0)), pl.BlockSpec((tm, GMLP_W), lambda i: (i, u0 + 1)),
                  pl.BlockSpec((1, GMLP_W), lambda i: (0, 0)), pl.BlockSpec((GMLP_GROUPS, CHUNK, CHUNK), lambda i: (0, 0, 0)),
                  pl.BlockSpec((GMLP_GROUPS, CHUNK, 1), lambda i: (0, 0, 0))],
        out_specs=pl.BlockSpec((tm, GMLP_W), lambda i: (i, 0)),
        out_shape=jax.ShapeDtypeStruct((t, GMLP_W), BF16), compiler_params=_cparams("parallel"),
    )(p, p, v_gain, w_s, b_col)


def _gmlp_bwd(p, dy, v_gain, w_s, b_col, name):
    t = p.shape[0]
    tm = _row_tile(t)
    u0 = P_UV // GMLP_W

    def body(u_ref, v_ref, dy_ref, gn_ref, ws_ref, bs_ref, duv_ref, dws_ref, dbs_ref, dgn_ref, dvn_ref):
        @pl.when(pl.program_id(0) == 0)
        def _():
            dws_ref[...] = jnp.zeros_like(dws_ref)
            dbs_ref[...] = jnp.zeros_like(dbs_ref)
            dgn_ref[...] = jnp.zeros_like(dgn_ref)

        vraw = v_ref[...].astype(F32)
        va = _gelu(vraw)
        rstd = lax.rsqrt(jnp.mean(va * va, axis=-1, keepdims=True) + EPS)
        vhat = va * rstd
        gain = gn_ref[...]
        vn = (vhat * gain).astype(BF16)
        tril = _tril_mask()
        wm = [jnp.where(tril, ws_ref[g], 0.0).astype(BF16) for g in range(GMLP_GROUPS)]
        for k in range(tm // CHUNK):
            rows = slice(k * CHUNK, (k + 1) * CHUNK)
            for g in range(GMLP_GROUPS):
                cols = slice(g * 128, (g + 1) * 128)
                uraw = u_ref[rows, cols].astype(F32)
                vb = vn[rows, cols]
                mixed = jnp.dot(wm[g], vb, preferred_element_type=F32) + bs_ref[g]
                dyb = dy_ref[rows, cols].astype(F32)
                duv_ref[rows, cols] = (dyb * mixed * _gelu_grad(uraw)).astype(duv_ref.dtype)
                dmix = dyb * _gelu(uraw)
                dmb = dmix.astype(BF16)
                dws_ref[g] += jnp.where(tril, lax.dot_general(dmb, vb, _DIMS["nt"], preferred_element_type=F32), 0.0)
                dbs_ref[g] += jnp.sum(dmix, axis=1, keepdims=True)
                dvn_ref[rows, cols] = lax.dot_general(wm[g], dmb, _DIMS["tn"], preferred_element_type=F32)
        dvn = dvn_ref[...]
        dgn_ref[...] += jnp.sum(dvn * vhat, axis=0, keepdims=True)
        dvh = dvn * gain
        dva = rstd * (dvh - vhat * jnp.mean(dvh * vhat, axis=-1, keepdims=True))
        duv_ref[:, GMLP_W:2 * GMLP_W] = (dva * _gelu_grad(vraw)).astype(duv_ref.dtype)

    return pl.pallas_call(
        body, name=name, grid=(t // tm,),
        in_specs=[pl.BlockSpec((tm, GMLP_W), lambda i: (i, u0)), pl.BlockSpec((tm, GMLP_W), lambda i: (i, u0 + 1)),
                  pl.BlockSpec((tm, GMLP_W), lambda i: (i, 0)),
                  pl.BlockSpec((1, GMLP_W), lambda i: (0, 0)), pl.BlockSpec((GMLP_GROUPS, CHUNK, CHUNK), lambda i: (0, 0, 0)),
                  pl.BlockSpec((GMLP_GROUPS, CHUNK, 1), lambda i: (0, 0, 0))],
        out_specs=[pl.BlockSpec((tm, 2 * GMLP_W), lambda i: (i, 0)), pl.BlockSpec((GMLP_GROUPS, CHUNK, CHUNK), lambda i: (0, 0, 0)),
                   pl.BlockSpec((GMLP_GROUPS, CHUNK, 1), lambda i: (0, 0, 0)), pl.BlockSpec((1, GMLP_W), lambda i: (0, 0))],
        out_shape=[jax.ShapeDtypeStruct((t, 2 * GMLP_W), BF16), jax.ShapeDtypeStruct((GMLP_GROUPS, CHUNK, CHUNK), F32),
                   jax.ShapeDtypeStruct((GMLP_GROUPS, CHUNK, 1), F32), jax.ShapeDtypeStruct((1, GMLP_W), F32)],
        scratch_shapes=[pltpu.VMEM((tm, GMLP_W), F32)], compiler_params=_cparams("arbitrary"),
    )(p, p, dy, v_gain, w_s, b_col)


def _head_masks():
    lane = lax.broadcasted_iota(jnp.int32, (1, MEM_W), 1)
    return [(lane >= h * 64) & (lane < (h + 1) * 64) for h in range(MEM_HEADS)]


def _mem_fwd(p, kv, name):
    t = p.shape[0]
    tm = _row_tile(t)
    q0 = P_Q // MEM_W

    def body(q_ref, kv_ref, o_ref):
        q = q_ref[...]
        k = kv_ref[:, 0:MEM_W].astype(BF16)
        v = kv_ref[:, MEM_W:2 * MEM_W].astype(BF16)
        out = jnp.zeros((tm, MEM_W), F32)
        for mask in _head_masks():
            s = lax.dot_general(jnp.where(mask, q, 0), k, _DIMS["nt"], preferred_element_type=F32) * 0.125
            e = jnp.exp(s - jnp.max(s, axis=-1, keepdims=True))
            pr = (e * (1.0 / jnp.sum(e, axis=-1, keepdims=True))).astype(BF16)
            out = out + jnp.where(mask, jnp.dot(pr, v, preferred_element_type=F32), 0.0)
        o_ref[...] = out.astype(o_ref.dtype)

    return pl.pallas_call(
        body, name=name, grid=(t // tm,),
        in_specs=[pl.BlockSpec((tm, MEM_W), lambda i: (i, q0)), pl.BlockSpec((MEM_LEN, 2 * MEM_W), lambda i: (0, 0))],
        out_specs=pl.BlockSpec((tm, MEM_W), lambda i: (i, 0)),
        out_shape=jax.ShapeDtypeStruct((t, MEM_W), BF16), compiler_params=_cparams("parallel"),
    )(p, kv)


def _mem_bwd(p, kv, dy, name):
    t = p.shape[0]
    tm = _row_tile(t)
    q0 = P_Q // MEM_W

    def body(q_ref, kv_ref, dy_ref, dq_ref, dkv_ref):
        @pl.when(pl.program_id(0) == 0)
        def _():
            dkv_ref[...] = jnp.zeros_like(dkv_ref)

        q = q_ref[...]
        dy = dy_ref[...]
        k = kv_ref[:, 0:MEM_W].astype(BF16)
        v = kv_ref[:, MEM_W:2 * MEM_W].astype(BF16)
        dq = jnp.zeros((tm, MEM_W), F32)
        dk = jnp.zeros((MEM_LEN, MEM_W), F32)
        dv = jnp.zeros((MEM_LEN, MEM_W), F32)
        for mask in _head_masks():
            qh = jnp.where(mask, q, 0)
            dyh = jnp.where(mask, dy, 0)
            s = lax.dot_general(qh, k, _DIMS["nt"], preferred_element_type=F32) * 0.125
            e = jnp.exp(s - jnp.max(s, axis=-1, keepdims=True))
            pr = e * (1.0 / jnp.sum(e, axis=-1, keepdims=True))
            prb = pr.astype(BF16)
            dp = lax.dot_general(dyh, v, _DIMS["nt"], preferred_element_type=F32)
            ds = (pr * (dp - jnp.sum(dp * pr, axis=-1, keepdims=True)) * 0.125).astype(BF16)
            dq = dq + jnp.where(mask, jnp.dot(ds, k, preferred_element_type=F32), 0.0)
            dk = dk + lax.dot_general(ds, qh, _DIMS["tn"], preferred_element_type=F32)
            dv = dv + lax.dot_general(prb, dyh, _DIMS["tn"], preferred_element_type=F32)
        dq_ref[...] = dq.astype(dq_ref.dtype)
        dkv_ref[:, 0:MEM_W] += dk
        dkv_ref[:, MEM_W:2 * MEM_W] += dv

    return pl.pallas_call(
        body, name=name, grid=(t // tm,),
        in_specs=[pl.BlockSpec((tm, MEM_W), lambda i: (i, q0)), pl.BlockSpec((MEM_LEN, 2 * MEM_W), lambda i: (0, 0)),
                  pl.BlockSpec((tm, MEM_W), lambda i: (i, 0))],
        out_specs=[pl.BlockSpec((tm, MEM_W), lambda i: (i, 0)), pl.BlockSpec((MEM_LEN, 2 * MEM_W), lambda i: (0, 0))],
        out_shape=[jax.ShapeDtypeStruct((t, MEM_W), BF16), jax.ShapeDtypeStruct((MEM_LEN, 2 * MEM_W), F32)],
        compiler_params=_cparams("arbitrary"),
    )(p, kv, dy)


def _merge_fwd(p, b_ssd, b_gmlp, b_mem, name):
    t = p.shape[0]
    tm = _row_tile(t)
    g0 = P_GL // D_MODEL

    def body(g1, g2, g3, b1, b2, b3, o_ref):
        def strip(rows, carry):
            acc = _sigmoid(g1[rows, :].astype(F32)) * b1[rows, :].astype(F32)
            acc = acc + _sigmoid(g2[rows, :].astype(F32)) * b2[rows, :].astype(F32)
            acc = acc + _sigmoid(g3[rows, :].astype(F32)) * b3[rows, :].astype(F32)
            o_ref[rows, :] = acc.astype(o_ref.dtype)
            return carry

        _strips(tm, strip, 0)

    row = pl.BlockSpec((tm, D_MODEL), lambda i: (i, 0))
    return pl.pallas_call(
        body, name=name, grid=(t // tm,),
        in_specs=[pl.BlockSpec((tm, D_MODEL), lambda i, k=k: (i, g0 + k)) for k in range(3)] + [row] * 3,
        out_specs=row, out_shape=jax.ShapeDtypeStruct((t, D_MODEL), BF16), compiler_params=_cparams("parallel"),
    )(p, p, p, b_ssd, b_gmlp, b_mem)


def _merge_bwd(p, dm, b_ssd, b_gmlp, b_mem, name):
    t = p.shape[0]
    tm = _row_tile(t)
    g0 = P_GL // D_MODEL

    def body(g1, g2, g3, dm_ref, b1, b2, b3, d1, d2, d3, dgl_ref):
        def strip(rows, carry):
            dmv = dm_ref[rows, :].astype(F32)
            for k, (g_ref, b_ref, d_ref) in enumerate(((g1, b1, d1), (g2, b2, d2), (g3, b3, d3))):
                sg = _sigmoid(g_ref[rows, :].astype(F32))
                d_ref[rows, :] = (dmv * sg).astype(d_ref.dtype)
                dgl_ref[rows, k * D_MODEL:(k + 1) * D_MODEL] = (dmv * b_ref[rows, :].astype(F32) * sg * (1.0 - sg)).astype(dgl_ref.dtype)
            return carry

        _strips(tm, strip, 0)

    row = pl.BlockSpec((tm, D_MODEL), lambda i: (i, 0))
    return pl.pallas_call(
        body, name=name, grid=(t // tm,),
        in_specs=[pl.BlockSpec((tm, D_MODEL), lambda i, k=k: (i, g0 + k)) for k in range(3)] + [row] * 4,
        out_specs=[row, row, row, pl.BlockSpec((tm, 3 * D_MODEL), lambda i: (i, 0))],
        out_shape=[jax.ShapeDtypeStruct((t, D_MODEL), BF16)] * 3 + [jax.ShapeDtypeStruct((t, 3 * D_MODEL), BF16)],
        compiler_params=_cparams("parallel"),
    )(p, p, p, dm, b_ssd, b_gmlp, b_mem)


def _local_step(x, mem, target, w, rest_weights, push):
    t = x.shape[0]
    mm = functools.partial(_matmul, tk=1024)

    h1, ffn1_saved = _ffn_forward(x, w["ffn1_norm"], w["ffn1_w_gate"], w["ffn1_w_up"], w["ffn1_w_down"], "ffn1")
    w = {**w, **rest_weights(h1)}
    n2 = _rms_fwd(h1, w["mix_norm"], "mix_norm")
    p = mm(n2, w["w_in_p"], mode="nt", out_dtype=BF16, tm=1024, tn=1536, name="in_proj")
    dt_raw = mm(n2, w["w_dt"], mode="nt", out_dtype=F32, tm=512, tn=128, name="dt_proj")
    dt_bias = jnp.pad(w["ssd_dt_bias"], (0, 128 - SSD_HEADS)).reshape(1, 128)
    hv = _head_inputs(_dt_fwd(dt_raw, dt_bias, "dt_fwd"), w["ssd_a_log"], w["ssd_d"])
    xc = _conv_fwd(p, w["ssd_conv_w"], w["ssd_conv_b"], "conv_fwd")
    y_ssd_raw, y_ssd, states = _ssd_fwd(xc, p, hv, w["ssd_norm"], "ssd_fwd")
    b_col = w["gmlp_b_s"].reshape(GMLP_GROUPS, CHUNK, 1)
    y_gmlp = _gmlp_fwd(p, w["gmlp_v_norm"], w["gmlp_w_s"], b_col, "gmlp_fwd")
    mem_n = _rms_fwd(mem, w["mem_norm"], "mem_norm")
    kv = mm(mem_n, w["w_mem_kv"], mode="nn", out_dtype=F32, tm=256, tn=512, name="mem_kv")
    y_mem = _mem_fwd(p, kv, "mem_fwd")
    b_ssd = mm(y_ssd, w["w_branch_ssd"], mode="nn", out_dtype=BF16, tm=512, tn=1024, name="branch_ssd")
    b_gmlp = mm(y_gmlp, w["w_branch_gmlp"], mode="nn", out_dtype=BF16, tm=512, tn=1024, name="branch_gmlp")
    b_mem = mm(y_mem, w["w_branch_mem"], mode="nt", out_dtype=BF16, tm=512, tn=1024, tk=MEM_W, name="branch_mem")
    merged = _merge_fwd(p, b_ssd, b_gmlp, b_mem, "merge_fwd")
    h2 = mm(merged, w["w_out"], mode="nn", out_dtype=F32, tm=512, tn=1024, addend=h1, name="out_proj")
    h3, ffn2_saved = _ffn_forward(h2, w["ffn2_norm"], w["ffn2_w_gate"], w["ffn2_w_up"], w["ffn2_w_down"], "ffn2")
    dh3, d_final, loss = _loss_head(h3, w["final_norm"], target, "loss_head")

    g = {"final_norm": d_final}
    big = {}
    dh2, g["ffn2_norm"] = _ffn_backward(dh3, h2, w["ffn2_norm"], w["ffn2_w_gate"], w["ffn2_w_up"], w["ffn2_w_down"], ffn2_saved,
                                        "ffn2", functools.partial(push, 0))
    dmerged = mm(dh2, w["w_out"], mode="nt", out_dtype=BF16, tm=512, tn=1024, name="out_proj_dx")
    big["w_out"] = mm(merged, dh2, mode="tn", out_dtype=BF16, tm=1024, tn=1024, name="out_proj_dw")
    db_ssd, db_gmlp, db_mem, dgl = _merge_bwd(p, dmerged, b_ssd, b_gmlp, b_mem, "merge_bwd")
    dy_ssd = mm(db_ssd, w["w_branch_ssd"], mode="nt", out_dtype=BF16, tm=512, tn=2048, name="branch_ssd_dx")
    dy_gmlp = mm(db_gmlp, w["w_branch_gmlp"], mode="nt", out_dtype=BF16, tm=512, tn=1024, name="branch_gmlp_dx")
    dy_mem = mm(db_mem, w["w_branch_mem"], mode="nn", out_dtype=BF16, tm=512, tn=256, name="branch_mem_dx")
    big["w_branch_ssd"] = mm(y_ssd, db_ssd, mode="tn", out_dtype=BF16, tm=1024, tn=1024, name="branch_ssd_dw")
    big["w_branch_gmlp"] = mm(y_gmlp, db_gmlp, mode="tn", out_dtype=BF16, tm=1024, tn=1024, name="branch_gmlp_dw")
    big["w_branch_mem"] = mm(db_mem, y_mem, mode="tn", out_dtype=BF16, tm=1024, tn=256, name="branch_mem_dw")
    dq, dkv = _mem_bwd(p, kv, dy_mem, "mem_bwd")
    big["w_mem_kv"] = mm(mem_n, dkv, mode="tn", out_dtype=BF16, tm=1024, tn=512, tk=256, name="mem_kv_dw")
    dmem_n = mm(dkv, w["w_mem_kv"], mode="nt", out_dtype=F32, tm=256, tn=1024, tk=512, name="mem_kv_dx")
    _, g["mem_norm"] = _rms_bwd(mem, w["mem_norm"], dmem_n, None, "mem_norm_bwd")
    duv, d_ws, d_bs, g["gmlp_v_norm"] = _gmlp_bwd(p, dy_gmlp, w["gmlp_v_norm"], w["gmlp_w_s"], b_col, "gmlp_bwd")
    g["gmlp_w_s"] = d_ws
    g["gmlp_b_s"] = d_bs.reshape(GMLP_GROUPS, CHUNK)
    dxs, d_bm, d_cm, dz, ddt_w, hsums, g["ssd_norm"] = _ssd_bwd(xc, p, hv, w["ssd_norm"], y_ssd_raw, dy_ssd, states, "ssd_bwd")
    heads = hsums[:, :, :HEADS_PER_GROUP]
    g["ssd_a_log"] = heads[:, 1, :].reshape(1, SSD_HEADS)
    g["ssd_d"] = heads[:, 2, :].reshape(1, SSD_HEADS)
    ddt = jnp.transpose(ddt_w[:, :, :HEADS_PER_GROUP], (1, 0, 2)).reshape(t, SSD_HEADS)
    ddt, d_bias = _dt_bwd(jnp.pad(ddt, ((0, 0), (0, 128 - SSD_HEADS))), dt_raw, dt_bias, "dt_bwd")
    g["ssd_dt_bias"] = d_bias[:, :SSD_HEADS]
    parts, dws, dbs = [], [], []
    for dyc, col0, tag in ((dxs, 0, "x"), (d_bm, SSD_INNER, "b"), (d_cm, SSD_INNER + SSD_GROUPS * SSD_STATE, "c")):
        dacc, dw_c, db_c = _conv_bwd_act(p, dyc, w["ssd_conv_w"], w["ssd_conv_b"], col0, f"conv_bwd_act_{tag}")
        parts.append(_conv_bwd_dx(dacc, w["ssd_conv_w"], col0, f"conv_bwd_dx_{tag}"))
        dws.append(dw_c)
        dbs.append(db_c)
    g["ssd_conv_w"] = jnp.concatenate(dws, axis=1)
    g["ssd_conv_b"] = jnp.concatenate(dbs, axis=1)
    dp = jnp.concatenate([dz, duv] + parts + [dgl, dq, jnp.zeros((t, P_W - P_USED), BF16)], axis=1)
    d_win_p = _matmul(dp, n2, mode="tn", out_dtype=BF16, tm=1536, tn=1024, tk=2048, name="in_proj_dw")
    d_wdt = mm(ddt, n2, mode="tn", out_dtype=BF16, tm=128, tn=1024, name="dt_proj_dw")
    sl = lambda a, o, n: a[o:o + n]
    big["w_in"] = jnp.concatenate([sl(d_win_p, P_Z, 2048), sl(d_win_p, P_XBC, XBC), d_wdt[:SSD_HEADS], sl(d_win_p, P_UV, 2048),
                                   sl(d_win_p, P_Q, MEM_W), sl(d_win_p, P_GL, 3 * D_MODEL)], axis=0)
    token = push(1, big)
    dn2 = mm(dp, w["w_in_p"], mode="nn", out_dtype=F32, tm=1024, tn=1024, tk=3584, name="in_proj_dx")
    dn2 = _matmul(ddt, w["w_dt"], mode="nn", out_dtype=F32, tm=512, tn=1024, tk=128, addend=dn2, name="dt_proj_dx")
    dh1, g["mix_norm"] = _rms_bwd(h1, w["mix_norm"] + token, dn2, dh2, "mix_norm_bwd")
    dx, g["ffn1_norm"] = _ffn_backward(dh1, x, w["ffn1_norm"], w["ffn1_w_gate"], w["ffn1_w_up"], w["ffn1_w_down"], ffn1_saved,
                                       "ffn1", functools.partial(push, 2))
    return loss, dx, g


def _split_w_in(w_in_t):
    sl = lambda o, n: w_in_t[o:o + n]
    w_p = jnp.concatenate([sl(IN_Z, 2048), sl(IN_UV, 2048), sl(IN_XBC, XBC), sl(IN_GL, 3 * D_MODEL), sl(IN_Q, MEM_W),
                           jnp.zeros((P_W - P_USED, D_MODEL), w_in_t.dtype)], axis=0)
    w_dt = jnp.pad(sl(IN_DT, SSD_HEADS), ((0, 128 - SSD_HEADS), (0, 0)))
    return w_p, w_dt


def _pick_tile(rows, cap=512):
    best = None
    for tile in range(8, min(rows, cap) + 1, 8):
        if rows % tile == 0:
            best = tile
    return best if best is not None else rows


def _adamw(w, g, m, v, name):
    rows, lanes = w.shape
    tile = _pick_tile(rows, cap=max(8, (512 * 1024 // lanes) // 8 * 8))
    c1 = 1.0 / (1.0 - ADAM_B1 ** ADAM_STEP)
    c2 = 1.0 / (1.0 - ADAM_B2 ** ADAM_STEP)

    def body(w_ref, g_ref, m_ref, v_ref, d_ref, nm_ref, nv_ref):
        gv = g_ref[...]
        nm = ADAM_B1 * m_ref[...] + (1.0 - ADAM_B1) * gv
        nv = ADAM_B2 * v_ref[...] + (1.0 - ADAM_B2) * (gv * gv)
        nm_ref[...] = nm
        nv_ref[...] = nv
        d_ref[...] = -ADAM_LR * ((nm * c1) / (jnp.sqrt(nv * c2) + ADAM_EPS) + ADAM_WD * w_ref[...])

    blk = pl.BlockSpec((tile, lanes), lambda i: (i, 0))
    return pl.pallas_call(
        body, name=name, grid=(rows // tile,), in_specs=[blk] * 4, out_specs=[blk] * 3,
        out_shape=[jax.ShapeDtypeStruct((rows, lanes), F32)] * 3, compiler_params=_cparams("parallel"),
    )(w, g, m, v)


HBM = pl.BlockSpec(memory_space=pltpu.HBM)


def _place():
    x, y, c = lax.axis_index("x"), lax.axis_index("y"), lax.axis_index("c")
    chips = [(1 - x, y), (x, 1 - y), (1 - x, 1 - y)]
    return x, y, c, chips


def _gather_weights(slots):
    _, _, rh, lanes = slots.shape

    def body(in_ref, out_ref, send_sems, recv_sems):
        del in_ref
        x, y, c, chips = _place()
        me, sibling = (x, y, c), (x, y, 1 - c)

        def copy(k, src, dst, to):
            return pltpu.make_async_remote_copy(src_ref=src, dst_ref=dst, send_sem=send_sems.at[k], recv_sem=recv_sems.at[k],
                                                device_id=to, device_id_type=MESH)

        own = out_ref.at[2 * x + y, c]
        first = [copy(j, own, own, (*chip, c)) for j, chip in enumerate(chips)]
        for cp in first:
            cp.start()
        passed = []
        for j, (cx, cy) in enumerate(chips):
            landed = out_ref.at[2 * cx + cy, c]
            copy(j, landed, landed, me).wait_recv()
            fwd = copy(3 + j, landed, landed, sibling)
            fwd.start()
            passed.append(fwd)
        for j, (cx, cy) in enumerate(chips):
            other = out_ref.at[2 * cx + cy, 1 - c]
            copy(3 + j, other, other, me).wait_recv()
        for cp in first + passed:
            cp.wait_send()

    return pl.pallas_call(
        body, name="gather_weights", out_shape=jax.ShapeDtypeStruct(slots.shape, slots.dtype),
        in_specs=[HBM], out_specs=HBM, input_output_aliases={0: 0},
        scratch_shapes=[pltpu.SemaphoreType.DMA((6,)), pltpu.SemaphoreType.DMA((6,))],
    )(slots)


SEM = pl.BlockSpec(memory_space=pltpu.SEMAPHORE)
EFFECT = pltpu.SideEffectType.DATAFLOW_SIDE_EFFECTING
N_PEER = 3


def _sem_outs():
    return tuple(pltpu.SemaphoreType.DMA(()) for _ in range(2 * N_PEER))


def _gather_start(slots):
    def body(in_ref, *refs):
        del in_ref
        sems, thru, token = refs[:2 * N_PEER], refs[2 * N_PEER], refs[2 * N_PEER + 1]
        x, y, c, chips = _place()
        own = thru.at[2 * x + y, c]
        for j, chip in enumerate(chips):
            pltpu.make_async_remote_copy(src_ref=own, dst_ref=own, send_sem=sems[j], recv_sem=sems[N_PEER + j],
                                         device_id=(*chip, c), device_id_type=MESH).start()
        token[...] = jnp.zeros_like(token)

    out = pl.pallas_call(
        body, name="gather_rest_start",
        out_shape=_sem_outs() + (pltpu.HBM(slots.shape, slots.dtype), jax.ShapeDtypeStruct((8, 128), F32)),
        in_specs=(HBM,), out_specs=(SEM,) * (2 * N_PEER) + (HBM, pl.BlockSpec(memory_space=pltpu.VMEM)),
        input_output_aliases={0: 2 * N_PEER}, compiler_params=pltpu.CompilerParams(has_side_effects=EFFECT),
    )(pltpu.with_memory_space_constraint(slots, pltpu.HBM))
    return out[:2 * N_PEER], out[2 * N_PEER], out[2 * N_PEER + 1]


def _gather_wait(sems, thru, after):
    def body(in_ref, *refs):
        del in_ref
        sems, out_ref = refs[:2 * N_PEER], refs[2 * N_PEER + 1]
        x, y, c, chips = _place()
        own = out_ref.at[2 * x + y, c]
        for j, (cx, cy) in enumerate(chips):
            cp = pltpu.make_async_remote_copy(src_ref=own, dst_ref=out_ref.at[2 * cx + cy, c], send_sem=sems[j],
                                              recv_sem=sems[N_PEER + j], device_id=(cx, cy, c), device_id_type=MESH)
            cp.wait_send()
            cp.wait_recv()

    return pl.pallas_call(
        body, name="gather_rest_wait", out_shape=pltpu.HBM(thru.shape, thru.dtype),
        in_specs=(HBM,) + (SEM,) * (2 * N_PEER) + (pl.BlockSpec(memory_space=pl.ANY),), out_specs=HBM,
        input_output_aliases={0: 0}, compiler_params=pltpu.CompilerParams(has_side_effects=EFFECT),
    )(thru, *sems, after)


def _gather_forward(slots):
    def body(in_ref, out_ref, send_sems, recv_sems):
        del in_ref
        x, y, c, chips = _place()
        cps = []
        for j, (cx, cy) in enumerate(chips):
            landed = out_ref.at[2 * cx + cy, c]
            cps.append(pltpu.make_async_remote_copy(src_ref=landed, dst_ref=landed, send_sem=send_sems.at[j], recv_sem=recv_sems.at[j],
                                                    device_id=(x, y, 1 - c), device_id_type=MESH))
        for cp in cps:
            cp.start()
        for j, (cx, cy) in enumerate(chips):
            other = out_ref.at[2 * cx + cy, 1 - c]
            pltpu.make_async_remote_copy(src_ref=other, dst_ref=other, send_sem=send_sems.at[j], recv_sem=recv_sems.at[j],
                                         device_id=(x, y, 1 - c), device_id_type=MESH).wait_recv()
        for cp in cps:
            cp.wait_send()

    return pl.pallas_call(
        body, name="gather_rest_forward", out_shape=jax.ShapeDtypeStruct(slots.shape, slots.dtype),
        in_specs=[HBM], out_specs=HBM, input_output_aliases={0: 0},
        scratch_shapes=[pltpu.SemaphoreType.DMA((N_PEER,)), pltpu.SemaphoreType.DMA((N_PEER,))],
    )(slots)


def _scatter_start(pa, tag):
    ns, rh, lanes = pa.shape
    land = pltpu.with_memory_space_constraint(lax.empty((N_PEER, rh, lanes), pa.dtype), pltpu.HBM)

    def body(pa_ref, land_ref, *refs):
        x, y, c, chips = _place()
        for j, (cx, cy) in enumerate(chips):
            pltpu.make_async_remote_copy(src_ref=pa_ref.at[2 * cx + cy], dst_ref=land_ref.at[j], send_sem=refs[j],
                                         recv_sem=refs[N_PEER + j], device_id=(cx, cy, c), device_id_type=MESH).start()
        refs[-1][...] = jnp.zeros_like(refs[-1])

    out = pl.pallas_call(
        body, name=f"scatter_start_{tag}",
        out_shape=_sem_outs() + (pltpu.HBM(pa.shape, pa.dtype), pltpu.HBM(land.shape, land.dtype), jax.ShapeDtypeStruct((8, 128), F32)),
        in_specs=(HBM, HBM), out_specs=(SEM,) * (2 * N_PEER) + (HBM, HBM, pl.BlockSpec(memory_space=pltpu.VMEM)),
        input_output_aliases={0: 2 * N_PEER, 1: 2 * N_PEER + 1}, compiler_params=pltpu.CompilerParams(has_side_effects=EFFECT),
    )(pltpu.with_memory_space_constraint(pa, pltpu.HBM), land)
    return (out[:2 * N_PEER], out[2 * N_PEER], out[2 * N_PEER + 1]), out[2 * N_PEER + 2]


def _scatter_wait(sems, pa_thru, land_thru, after, tag):
    def body(pa_ref, land_ref, *refs):
        sems = refs[:2 * N_PEER]
        x, y, c, chips = _place()
        for j, (cx, cy) in enumerate(chips):
            cp = pltpu.make_async_remote_copy(src_ref=pa_ref.at[2 * cx + cy], dst_ref=land_ref.at[j], send_sem=sems[j],
                                              recv_sem=sems[N_PEER + j], device_id=(cx, cy, c), device_id_type=MESH)
            cp.wait_send()
            cp.wait_recv()

    return pl.pallas_call(
        body, name=f"scatter_wait_{tag}",
        out_shape=(pltpu.HBM(pa_thru.shape, pa_thru.dtype), pltpu.HBM(land_thru.shape, land_thru.dtype)),
        in_specs=(HBM, HBM) + (SEM,) * (2 * N_PEER) + (pl.BlockSpec(memory_space=pl.ANY),), out_specs=(HBM, HBM),
        input_output_aliases={0: 0, 1: 1}, compiler_params=pltpu.CompilerParams(has_side_effects=EFFECT),
    )(pa_thru, land_thru, *sems, after)


def _rs_swap(gp, tag):
    _, ns, rh, lanes = gp.shape

    def body(in_ref, out_ref, send_sem, recv_sem):
        x, y, c, _ = _place()
        cp = pltpu.make_async_remote_copy(src_ref=in_ref.at[1 - c], dst_ref=out_ref, send_sem=send_sem, recv_sem=recv_sem,
                                          device_id=(x, y, 1 - c), device_id_type=MESH)
        cp.start()
        cp.wait_send()
        cp.wait_recv()

    return pl.pallas_call(
        body, name=f"rs_swap_{tag}", out_shape=jax.ShapeDtypeStruct((ns, rh, lanes), gp.dtype), in_specs=[HBM], out_specs=HBM,
        scratch_shapes=[pltpu.SemaphoreType.DMA, pltpu.SemaphoreType.DMA],
    )(gp)


def _rs_tile(rh):
    return _pick_tile(rh, cap=512)


def _rs_add(gp, recv, c, tag):
    _, ns, rh, lanes = gp.shape
    tile = _rs_tile(rh)

    def body(c_ref, a_ref, b_ref, o_ref):
        o_ref[...] = (a_ref[...].astype(F32) + b_ref[...].astype(F32)).astype(o_ref.dtype)

    return pl.pallas_call(
        body, name=f"rs_add_{tag}", out_shape=jax.ShapeDtypeStruct((ns, rh, lanes), gp.dtype),
        grid_spec=pltpu.PrefetchScalarGridSpec(
            num_scalar_prefetch=1, grid=(ns, rh // tile),
            in_specs=[pl.BlockSpec((None, None, tile, lanes), lambda s, i, c_ref: (c_ref[0], s, i, 0)),
                      pl.BlockSpec((None, tile, lanes), lambda s, i, c_ref: (s, i, 0))],
            out_specs=pl.BlockSpec((None, tile, lanes), lambda s, i, c_ref: (s, i, 0))),
        compiler_params=_cparams("parallel", "parallel"),
    )(c, gp, recv)


def _rs_sum(pa, recv, place, tag):
    ns, rh, lanes = pa.shape
    tile = _rs_tile(rh)

    def body(place_ref, a_ref, r_ref, o_ref):
        acc = a_ref[...].astype(F32)
        for j in range(ns - 1):
            acc = acc + r_ref[j].astype(F32)
        o_ref[...] = acc

    return pl.pallas_call(
        body, name=f"rs_sum_{tag}", out_shape=jax.ShapeDtypeStruct((2, rh, lanes), F32),
        grid_spec=pltpu.PrefetchScalarGridSpec(
            num_scalar_prefetch=1, grid=(rh // tile,),
            in_specs=[pl.BlockSpec((None, tile, lanes), lambda i, place_ref: (place_ref[0], i, 0)),
                      pl.BlockSpec((ns - 1, tile, lanes), lambda i, place_ref: (0, i, 0))],
            out_specs=pl.BlockSpec((None, tile, lanes), lambda i, place_ref: (place_ref[1], i, 0))),
        compiler_params=_cparams("parallel"),
    )(place, pa, recv)


def _rs_share(halves, tag):
    def body(in_ref, out_ref, send_sem, recv_sem):
        del in_ref
        x, y, c, _ = _place()
        cp = pltpu.make_async_remote_copy(src_ref=out_ref.at[c], dst_ref=out_ref.at[c], send_sem=send_sem, recv_sem=recv_sem,
                                          device_id=(x, y, 1 - c), device_id_type=MESH)
        cp.start()
        other = out_ref.at[1 - c]
        pltpu.make_async_remote_copy(src_ref=other, dst_ref=other, send_sem=send_sem, recv_sem=recv_sem,
                                     device_id=(x, y, 1 - c), device_id_type=MESH).wait_recv()
        cp.wait_send()

    return pl.pallas_call(
        body, name=f"rs_share_{tag}", out_shape=jax.ShapeDtypeStruct(halves.shape, halves.dtype), in_specs=[HBM], out_specs=HBM,
        input_output_aliases={0: 0}, scratch_shapes=[pltpu.SemaphoreType.DMA, pltpu.SemaphoreType.DMA],
    )(halves)


N_DEV = 8
SMALL_ROWS = 160


def _allreduce_small(v):
    m_per, n = v.shape

    def body(x_ref, out_ref, all_ref, send_sems, recv_sems, local_sem):
        x, y, c, chips = _place()
        me, sibling = (x, y, c), (x, y, 1 - c)

        def rows(px, py, pc):
            return all_ref.at[pl.ds((4 * px + 2 * py + pc) * m_per, m_per), :]

        def copy(k, block, to, src=None):
            return pltpu.make_async_remote_copy(src_ref=rows(*block) if src is None else src, dst_ref=rows(*block),
                                                send_sem=send_sems.at[k], recv_sem=recv_sems.at[k], device_id=to, device_id_type=MESH)

        mine = pltpu.make_async_copy(x_ref, rows(*me), local_sem)
        mine.start()
        first = [copy(0, me, sibling, src=x_ref)]
        first += [copy(1 + j, me, (*chip, c), src=x_ref) for j, chip in enumerate(chips)]
        for cp in first:
            cp.start()
        passed = [copy(4 + j, (*chip, c), sibling) for j, chip in enumerate(chips)]
        for j, chip in enumerate(chips):
            copy(1 + j, (*chip, c), me).wait_recv()
            passed[j].start()
        copy(0, sibling, me).wait_recv()
        for j, chip in enumerate(chips):
            copy(4 + j, (*chip, 1 - c), me).wait_recv()
        for cp in first + passed:
            cp.wait_send()
        mine.wait()
        step = 32
        for r in range(0, m_per, step):
            acc = all_ref[r:r + step, :]
            for d in range(1, N_DEV):
                acc = acc + all_ref[d * m_per + r:d * m_per + r + step, :]
            out_ref[r:r + step, :] = acc

    vm = pl.BlockSpec(memory_space=pltpu.VMEM)
    return pl.pallas_call(
        body, name="allreduce_small", out_shape=jax.ShapeDtypeStruct((m_per, n), v.dtype), in_specs=[vm], out_specs=vm,
        scratch_shapes=[pltpu.VMEM((N_DEV * m_per, n), v.dtype), pltpu.SemaphoreType.DMA((7,)), pltpu.SemaphoreType.DMA((7,)),
                        pltpu.SemaphoreType.DMA],
        compiler_params=pltpu.CompilerParams(vmem_limit_bytes=V7X_VMEM_LIMIT),
    )(v)


BIG = {"ffn1_w_gate": ((D_MODEL, D_FF), 1), "ffn1_w_up": ((D_MODEL, D_FF), 1), "ffn1_w_down": ((D_FF, D_MODEL), 0),
       "ffn2_w_gate": ((D_MODEL, D_FF), 1), "ffn2_w_up": ((D_MODEL, D_FF), 1), "ffn2_w_down": ((D_FF, D_MODEL), 0),
       "w_in": ((D_MODEL, IN_WIDTH), 1), "w_mem_kv": ((D_MODEL, 2 * MEM_W), 0), "w_branch_ssd": ((SSD_INNER, D_MODEL), 0),
       "w_branch_gmlp": ((GMLP_W, D_MODEL), 0), "w_branch_mem": ((MEM_W, D_MODEL), 1), "w_out": ((D_MODEL, D_MODEL), 0)}
FFN1 = ("ffn1_w_gate", "ffn1_w_up", "ffn1_w_down")
FFN2 = ("ffn2_w_gate", "ffn2_w_up", "ffn2_w_down")
MIXER = ("w_out", "w_branch_ssd", "w_branch_gmlp", "w_branch_mem", "w_mem_kv", "w_in")
GATHER_GROUPS = (FFN1, FFN2 + MIXER)
REDUCE_GROUPS = (FFN2, MIXER, FFN1)
CONV_W_ROWS = 8


def _shard_rows_of(name):
    (a, b), _ = BIG[name]
    return a * b // N_SHARD // LANES


def _group_rows(names, extra=0):
    return -(-(sum(_shard_rows_of(n) for n in names) + extra) // 32) * 32

SMALL = [("ffn1_norm", 1), ("mix_norm", 1), ("mem_norm", 1), ("ssd_conv_b", 3), ("heads", 1), ("ssd_norm", 2),
         ("gmlp_v_norm", 1), ("gmlp_w_s", 128), ("gmlp_b_s", 1), ("ffn2_norm", 1), ("final_norm", 1), ("ssd_conv_w", 12)]
assert sum(n for _, n in SMALL) <= SMALL_ROWS
HEAD_VECS = ("ssd_dt_bias", "ssd_a_log", "ssd_d")


def _pack_small(vals, loss=None):
    parts = []
    for name, nrows in SMALL:
        if name == "heads":
            row = jnp.concatenate([vals[k].reshape(-1) for k in HEAD_VECS]
                                  + [jnp.zeros((1,), F32) if loss is None else loss.reshape(1)])
            parts.append(jnp.pad(row, (0, LANES - row.shape[0])).reshape(1, LANES))
        elif name in vals:
            parts.append(vals[name].reshape(nrows, LANES))
        else:
            parts.append(jnp.zeros((nrows, LANES), F32))
    buf = jnp.concatenate(parts, axis=0)
    return jnp.pad(buf, ((0, SMALL_ROWS - buf.shape[0]), (0, 0)))


def _unpack_small(buf):
    out, r = {}, 0
    for name, nrows in SMALL:
        blk = buf[r:r + nrows]
        r += nrows
        if name == "heads":
            for i, k in enumerate(HEAD_VECS):
                out[k] = blk[0, i * SSD_HEADS:(i + 1) * SSD_HEADS]
            out["loss"] = blk[0, 3 * SSD_HEADS]
        else:
            out[name] = blk
    return out


def _wire_shape(name):
    (a, b), axis = BIG[name]
    return (b, a) if axis == 1 else (a, b)


def _pack_weights(given, names, conv=False):
    parts = [(given[n][0].T if BIG[n][1] == 1 else given[n][0]).astype(BF16).reshape(_shard_rows_of(n), LANES) for n in names]
    if conv:
        pairs = lax.bitcast_convert_type(given["ssd_conv_w"], BF16).reshape(-1)
        parts.append(jnp.pad(pairs, (0, CONV_W_ROWS * LANES - pairs.shape[0])).reshape(CONV_W_ROWS, LANES))
    total = _group_rows(names, CONV_W_ROWS if conv else 0)
    packed = jnp.concatenate(parts, axis=0)
    packed = jnp.pad(packed, ((0, total - packed.shape[0]), (0, 0))).reshape(1, 2, total // 2, LANES)
    return jnp.broadcast_to(packed, (N_SHARD, 2, total // 2, LANES))


def _unpack_weights(slots, names, conv=False):
    rows = slots.reshape(N_SHARD, -1, LANES)
    out, r = {}, 0
    for name in names:
        n = _shard_rows_of(name)
        out[name] = rows[:, r:r + n].reshape(_wire_shape(name))
        r += n
    if conv:
        cols = XBC // N_SHARD
        pairs = rows[:, r:r + CONV_W_ROWS].reshape(N_SHARD, -1)[:, :SSD_CONV * cols * 2].reshape(N_SHARD, SSD_CONV, cols, 2)
        out["ssd_conv_w"] = jnp.transpose(lax.bitcast_convert_type(pairs, F32), (1, 0, 2)).reshape(SSD_CONV, XBC)
    return out


def _pack_grads(grads, names):
    total = _group_rows(names)
    gp = jnp.concatenate([grads[n].astype(BF16).reshape(N_SHARD, _shard_rows_of(n), LANES) for n in names], axis=1)
    gp = jnp.pad(gp, ((0, 0), (0, total - gp.shape[1]), (0, 0))).reshape(N_SHARD, 2, total // 2, LANES)
    return jnp.transpose(gp, (1, 0, 2, 3))


def kernel(x, mem, ffn1_norm, ffn1_w_gate, ffn1_w_up, ffn1_w_down, mix_norm, mem_norm, w_in, ssd_conv_w, ssd_conv_b, ssd_dt_bias, ssd_a_log, ssd_d, ssd_norm, gmlp_v_norm, gmlp_w_s, gmlp_b_s, w_mem_kv, w_branch_ssd, w_branch_gmlp, w_branch_mem, w_out, ffn2_norm, ffn2_w_gate, ffn2_w_up, ffn2_w_down, final_norm, loss_target, m_ffn1_norm, m_ffn1_w_gate, m_ffn1_w_up, m_ffn1_w_down, m_mix_norm, m_mem_norm, m_w_in, m_ssd_conv_w, m_ssd_conv_b, m_ssd_dt_bias, m_ssd_a_log, m_ssd_d, m_ssd_norm, m_gmlp_v_norm, m_gmlp_w_s, m_gmlp_b_s, m_w_mem_kv, m_w_branch_ssd, m_w_branch_gmlp, m_w_branch_mem, m_w_out, m_ffn2_norm, m_ffn2_w_gate, m_ffn2_w_up, m_ffn2_w_down, m_final_norm, v_ffn1_norm, v_ffn1_w_gate, v_ffn1_w_up, v_ffn1_w_down, v_mix_norm, v_mem_norm, v_w_in, v_ssd_conv_w, v_ssd_conv_b, v_ssd_dt_bias, v_ssd_a_log, v_ssd_d, v_ssd_norm, v_gmlp_v_norm, v_gmlp_w_s, v_gmlp_b_s, v_w_mem_kv, v_w_branch_ssd, v_w_branch_gmlp, v_w_branch_mem, v_w_out, v_ffn2_norm, v_ffn2_w_gate, v_ffn2_w_up, v_ffn2_w_down, v_final_norm):
    given = dict(x=x, mem=mem, ffn1_norm=ffn1_norm, ffn1_w_gate=ffn1_w_gate, ffn1_w_up=ffn1_w_up, ffn1_w_down=ffn1_w_down, mix_norm=mix_norm, mem_norm=mem_norm, w_in=w_in, ssd_conv_w=ssd_conv_w, ssd_conv_b=ssd_conv_b, ssd_dt_bias=ssd_dt_bias, ssd_a_log=ssd_a_log, ssd_d=ssd_d, ssd_norm=ssd_norm, gmlp_v_norm=gmlp_v_norm, gmlp_w_s=gmlp_w_s, gmlp_b_s=gmlp_b_s, w_mem_kv=w_mem_kv, w_branch_ssd=w_branch_ssd, w_branch_gmlp=w_branch_gmlp, w_branch_mem=w_branch_mem, w_out=w_out, ffn2_norm=ffn2_norm, ffn2_w_gate=ffn2_w_gate, ffn2_w_up=ffn2_w_up, ffn2_w_down=ffn2_w_down, final_norm=final_norm, loss_target=loss_target, m_ffn1_norm=m_ffn1_norm, m_ffn1_w_gate=m_ffn1_w_gate, m_ffn1_w_up=m_ffn1_w_up, m_ffn1_w_down=m_ffn1_w_down, m_mix_norm=m_mix_norm, m_mem_norm=m_mem_norm, m_w_in=m_w_in, m_ssd_conv_w=m_ssd_conv_w, m_ssd_conv_b=m_ssd_conv_b, m_ssd_dt_bias=m_ssd_dt_bias, m_ssd_a_log=m_ssd_a_log, m_ssd_d=m_ssd_d, m_ssd_norm=m_ssd_norm, m_gmlp_v_norm=m_gmlp_v_norm, m_gmlp_w_s=m_gmlp_w_s, m_gmlp_b_s=m_gmlp_b_s, m_w_mem_kv=m_w_mem_kv, m_w_branch_ssd=m_w_branch_ssd, m_w_branch_gmlp=m_w_branch_gmlp, m_w_branch_mem=m_w_branch_mem, m_w_out=m_w_out, m_ffn2_norm=m_ffn2_norm, m_ffn2_w_gate=m_ffn2_w_gate, m_ffn2_w_up=m_ffn2_w_up, m_ffn2_w_down=m_ffn2_w_down, m_final_norm=m_final_norm, v_ffn1_norm=v_ffn1_norm, v_ffn1_w_gate=v_ffn1_w_gate, v_ffn1_w_up=v_ffn1_w_up, v_ffn1_w_down=v_ffn1_w_down, v_mix_norm=v_mix_norm, v_mem_norm=v_mem_norm, v_w_in=v_w_in, v_ssd_conv_w=v_ssd_conv_w, v_ssd_conv_b=v_ssd_conv_b, v_ssd_dt_bias=v_ssd_dt_bias, v_ssd_a_log=v_ssd_a_log, v_ssd_d=v_ssd_d, v_ssd_norm=v_ssd_norm, v_gmlp_v_norm=v_gmlp_v_norm, v_gmlp_w_s=v_gmlp_w_s, v_gmlp_b_s=v_gmlp_b_s, v_w_mem_kv=v_w_mem_kv, v_w_branch_ssd=v_w_branch_ssd, v_w_branch_gmlp=v_w_branch_gmlp, v_w_branch_mem=v_w_branch_mem, v_w_out=v_w_out, v_ffn2_norm=v_ffn2_norm, v_ffn2_w_gate=v_ffn2_w_gate, v_ffn2_w_up=v_ffn2_w_up, v_ffn2_w_down=v_ffn2_w_down, v_final_norm=v_final_norm)
    weights = [n for n in given if n not in ("x", "mem", "loss_target") and not n.startswith(("m_", "v_"))]
    xi, yi, ci = lax.axis_index("x"), lax.axis_index("y"), lax.axis_index("c")
    chip = (2 * xi + yi).astype(jnp.int32)
    core = ci.astype(jnp.int32)
    conv_cols = XBC // N_SHARD

    first = _gather_weights(_pack_weights(given, FFN1))
    first, rest_slots = lax.optimization_barrier((first, _pack_weights(given, GATHER_GROUPS[1], conv=True)))
    sems, thru, token = _gather_start(rest_slots)
    w = _unpack_weights(first, FFN1)
    for name in ("ffn1_norm", "mix_norm", "mem_norm", "ssd_conv_b", "ssd_norm", "gmlp_v_norm", "ffn2_norm", "final_norm"):
        w[name] = given[name].reshape(1, -1)
    w["ffn1_norm"] = w["ffn1_norm"] + token[0:1, 0:1]
    for name in HEAD_VECS:
        w[name] = given[name].reshape(-1)
    w["gmlp_w_s"] = given["gmlp_w_s"][0]
    w["gmlp_b_s"] = given["gmlp_b_s"][0]

    def rest_weights(after):
        rest = _unpack_weights(_gather_forward(_gather_wait(sems, thru, after)), GATHER_GROUPS[1], conv=True)
        rest["w_in_p"], rest["w_dt"] = _split_w_in(rest.pop("w_in"))
        return rest

    pending = {}

    def push(k, group_grads):
        gp = _pack_grads(group_grads, REDUCE_GROUPS[k])
        pa = _rs_add(gp, _rs_swap(gp, k), core.reshape(1), k)
        pending[k], token = _scatter_start(pa, k)
        return token[0:1, 0:1]

    def reduced(k, after):
        pa, land = _scatter_wait(*pending[k], after, k)
        gsum = _rs_share(_rs_sum(pa, land, jnp.stack([chip, core]), k), k)
        rows = gsum.reshape(-1, LANES)
        out, r = {}, 0
        for name in REDUCE_GROUPS[k]:
            n = _shard_rows_of(name)
            a, b = given[name].shape[1:]
            out[name] = rows[r:r + n].reshape(b, a).T if BIG[name][1] == 1 else rows[r:r + n].reshape(a, b)
            r += n
        return out

    loss_part, grad_x, g = _local_step(x[0], mem[0], loss_target[0], w, rest_weights, push)

    grads, deltas, new_m, new_v = {}, {}, {}, {}

    def update(k, after):
        for name, gl in reduced(k, after).items():
            d, nm, nv = _adamw(given[name][0], gl, given["m_" + name][0], given["v_" + name][0], f"adamw_{name}")
            grads[name], deltas[name], new_m[name], new_v[name] = (a[None] for a in (gl, d, nm, nv))

    update(0, grad_x)
    update(1, deltas[REDUCE_GROUPS[0][-1]])

    small_vals = {k: g[k] for k, _ in SMALL if k != "heads"}
    small_vals.update({k: g[k] for k in HEAD_VECS})
    red = _unpack_small(_allreduce_small(_pack_small(small_vals, loss=loss_part[0, 0])))
    update(2, deltas[REDUCE_GROUPS[1][-1]])
    conv_g = lax.dynamic_slice_in_dim(red["ssd_conv_w"].reshape(SSD_CONV, XBC), chip * conv_cols, conv_cols, axis=1)
    d, nm, nv = _adamw(given["ssd_conv_w"][0], conv_g, given["m_ssd_conv_w"][0], given["v_ssd_conv_w"][0], "adamw_conv_w")
    grads["ssd_conv_w"], deltas["ssd_conv_w"], new_m["ssd_conv_w"], new_v["ssd_conv_w"] = (a[None] for a in (conv_g, d, nm, nv))
    for k in [k for k, _ in SMALL if k not in ("heads", "ssd_conv_w")] + list(HEAD_VECS):
        shape = given[k].shape
        as2d = lambda a: a.reshape(-1, shape[-1])
        d, nm, nv = _adamw(as2d(given[k]), as2d(red[k]), as2d(given["m_" + k]), as2d(given["v_" + k]), f"adamw_{k}")
        grads[k], deltas[k], new_m[k], new_v[k] = (a.reshape(shape) for a in (red[k], d, nm, nv))

    return (red["loss"], grad_x[None], *[grads[n] for n in weights], *[deltas[n] for n in weights],
            *[new_m[n] for n in weights], *[new_v[n] for n in weights])
```

```python
import functools
import math

import jax
import jax.numpy as jnp
from jax import lax
from jax.experimental import pallas as pl
from jax.experimental.pallas import tpu as pltpu

F32, BF16 = jnp.float32, jnp.bfloat16
HI = lax.Precision.HIGHEST
MESH = pl.DeviceIdType.MESH

D_MODEL = 1024
D_FF = 2816
MEM_LEN = 256
SSD_INNER = 2048
SSD_HEADS = 32
SSD_GROUPS = 4
SSD_STATE = 128
SSD_CONV = 4
CHUNK = 128
XBC = SSD_INNER + 2 * SSD_GROUPS * SSD_STATE
GMLP_W = 1024
GMLP_GROUPS = 8
MEM_W = 256
MEM_HEADS = 4
EPS = 1e-6
IN_WIDTH = 10528
IN_Z, IN_XBC, IN_DT, IN_UV, IN_Q, IN_GL = 0, 2048, 5120, 5152, 7200, 7456
P_GL, P_Z, P_XBC, P_UV, P_Q, P_W = 0, 3072, 5120, 8192, 10240, 10752
P_USED = 10496

ADAM_LR, ADAM_B1, ADAM_B2, ADAM_EPS, ADAM_WD, ADAM_STEP = 0.001, 0.9, 0.999, 1e-08, 0.01, 10

V7X_VMEM_LIMIT = 56 * 1024 * 1024
N_SHARD = 4
LANES = 1024


def _cparams(*sem):
    return pltpu.CompilerParams(dimension_semantics=sem, vmem_limit_bytes=V7X_VMEM_LIMIT)


ANY = pl.BlockSpec(memory_space=pl.ANY)


def _sigmoid(x):
    return 0.5 * jnp.tanh(0.5 * x) + 0.5


def _row_tile(t):
    return min(512, t)


_DIMS = {"nn": (((1,), (0,)), ((), ())), "nt": (((1,), (1,)), ((), ())), "tn": (((0,), (0,)), ((), ()))}


def _matmul(a, b, *, mode, out_dtype, tm, tn, tk, name, scale=1.0, addend=None):
    if mode == "tn":
        k_dim, m_dim = a.shape
    else:
        m_dim, k_dim = a.shape
    n_dim = b.shape[0] if mode == "nt" else b.shape[1]
    tm, tn, tk = min(tm, m_dim), min(tn, n_dim), min(tk, k_dim)
    assert m_dim % tm == 0 and n_dim % tn == 0 and k_dim % tk == 0, (name, a.shape, b.shape, tm, tn, tk)
    ni, nj, nk = m_dim // tm, n_dim // tn, k_dim // tk
    a_spec = pl.BlockSpec((tk, tm), lambda j, i, k: (k, i)) if mode == "tn" else pl.BlockSpec((tm, tk), lambda j, i, k: (i, k))
    b_spec = pl.BlockSpec((tn, tk), lambda j, i, k: (j, k)) if mode == "nt" else pl.BlockSpec((tk, tn), lambda j, i, k: (k, j))
    o_spec = pl.BlockSpec((tm, tn), lambda j, i, k: (i, j))
    dims = _DIMS[mode]
    has_add = addend is not None

    def body(*refs):
        a_ref, b_ref = refs[:2]
        r_ref = refs[2] if has_add else None
        o_ref = refs[2 + has_add]

        def finish(acc):
            r = acc * scale if scale != 1.0 else acc
            if has_add:
                r = r + r_ref[...].astype(F32)
            o_ref[...] = r.astype(o_ref.dtype)

        prod = lax.dot_general(a_ref[...].astype(BF16), b_ref[...].astype(BF16), dims, preferred_element_type=F32)
        if nk == 1:
            finish(prod)
            return
        acc_ref = refs[-1]
        k = pl.program_id(2)

        @pl.when(k == 0)
        def _():
            acc_ref[...] = prod

        @pl.when(k > 0)
        def _():
            acc_ref[...] += prod

        @pl.when(k == nk - 1)
        def _():
            finish(acc_ref[...])

    in_specs = [a_spec, b_spec] + ([o_spec] if has_add else [])
    args = (a, b) + ((addend,) if has_add else ())
    return pl.pallas_call(
        body, name=name, grid=(nj, ni, nk), in_specs=in_specs, out_specs=o_spec,
        out_shape=jax.ShapeDtypeStruct((m_dim, n_dim), out_dtype),
        scratch_shapes=[] if nk == 1 else [pltpu.VMEM((tm, tn), F32)],
        compiler_params=_cparams("parallel", "parallel", "arbitrary"),
    )(*args)


ROW_STRIP = 16


def _strips(tm, fn, init=None, rb=ROW_STRIP):
    def step(i, carry):
        return fn(pl.ds(pl.multiple_of(i * rb, rb), rb), carry)
    return lax.fori_loop(0, tm // rb, step, init, unroll=2)


def _rms_fwd(x, gain, name):
    t, d = x.shape
    tm = _row_tile(t)

    def body(x_ref, g_ref, o_ref):
        xv = x_ref[...]
        r = lax.rsqrt(jnp.mean(xv * xv, axis=-1, keepdims=True) + EPS)
        o_ref[...] = (xv * r * g_ref[...]).astype(o_ref.dtype)

    return pl.pallas_call(
        body, name=name, grid=(t // tm,),
        in_specs=[pl.BlockSpec((tm, d), lambda i: (i, 0)), pl.BlockSpec((1, d), lambda i: (0, 0))],
        out_specs=pl.BlockSpec((tm, d), lambda i: (i, 0)),
        out_shape=jax.ShapeDtypeStruct((t, d), BF16), compiler_params=_cparams("parallel"),
    )(x, gain)


def _rms_bwd(x, gain, dn, dres, name):
    t, d = x.shape
    tm = _row_tile(t)
    has_res = dres is not None

    def body(*refs):
        if has_res:
            x_ref, g_ref, dn_ref, r_ref, dx_ref, dg_ref = refs
        else:
            x_ref, g_ref, dn_ref, dx_ref, dg_ref = refs

        @pl.when(pl.program_id(0) == 0)
        def _():
            dg_ref[...] = jnp.zeros_like(dg_ref)

        xv = x_ref[...]
        r = lax.rsqrt(jnp.mean(xv * xv, axis=-1, keepdims=True) + EPS)
        xh = xv * r
        dnv = dn_ref[...].astype(F32)
        dg_ref[...] += jnp.sum(dnv * xh, axis=0, keepdims=True)
        dxh = dnv * g_ref[...]
        dx = r * (dxh - xh * jnp.mean(dxh * xh, axis=-1, keepdims=True))
        if has_res:
            dx = dx + r_ref[...]
        dx_ref[...] = dx

    row = pl.BlockSpec((tm, d), lambda i: (i, 0))
    vec = pl.BlockSpec((1, d), lambda i: (0, 0))
    in_specs = [row, vec, row] + ([row] if has_res else [])
    args = (x, gain, dn) + ((dres,) if has_res else ())
    return pl.pallas_call(
        body, name=name, grid=(t // tm,), in_specs=in_specs, out_specs=[row, vec],
        out_shape=[jax.ShapeDtypeStruct((t, d), F32), jax.ShapeDtypeStruct((1, d), F32)],
        compiler_params=_cparams("arbitrary"),
    )(*args)


def _loss_head(h, gain, target, name):
    t, d = h.shape
    tm = _row_tile(t)

    def body(h_ref, g_ref, t_ref, dh_ref, dg_ref, l_ref):
        @pl.when(pl.program_id(0) == 0)
        def _():
            dg_ref[...] = jnp.zeros_like(dg_ref)
            l_ref[...] = jnp.zeros_like(l_ref)

        xv = h_ref[...]
        g = g_ref[...]
        r = lax.rsqrt(jnp.mean(xv * xv, axis=-1, keepdims=True) + EPS)
        xh = xv * r
        err = xh * g - t_ref[...]
        l_ref[...] += 0.5 * jnp.sum(jnp.mean(err * err, axis=-1, keepdims=True), axis=0, keepdims=True)
        dy = err * (1.0 / d)
        dg_ref[...] += jnp.sum(dy * xh, axis=0, keepdims=True)
        dxh = dy * g
        dh_ref[...] = r * (dxh - xh * jnp.mean(dxh * xh, axis=-1, keepdims=True))

    row = pl.BlockSpec((tm, d), lambda i: (i, 0))
    vec = pl.BlockSpec((1, d), lambda i: (0, 0))
    return pl.pallas_call(
        body, name=name, grid=(t // tm,), in_specs=[row, vec, row],
        out_specs=[row, vec, pl.BlockSpec((1, 128), lambda i: (0, 0))],
        out_shape=[jax.ShapeDtypeStruct((t, d), F32), jax.ShapeDtypeStruct((1, d), F32), jax.ShapeDtypeStruct((1, 128), F32)],
        compiler_params=_cparams("arbitrary"),
    )(h, gain, target)


FF_TILE = 1408


def _ffn_fwd(n, x, wg, wu, wd, name):
    t, d = x.shape
    tm, tn = _row_tile(t), FF_TILE
    nj = D_FF // tn

    def body(n_ref, x_ref, wg_ref, wu_ref, wd_ref, h_ref, g_ref, u_ref, acc_ref):
        j = pl.program_id(1)

        @pl.when(j == 0)
        def _():
            acc_ref[...] = jnp.zeros_like(acc_ref)

        nb = n_ref[...]
        g = lax.dot_general(nb, wg_ref[...], _DIMS["nt"], preferred_element_type=F32)
        u = lax.dot_general(nb, wu_ref[...], _DIMS["nt"], preferred_element_type=F32)
        g_ref[...] = g.astype(BF16)
        u_ref[...] = u.astype(BF16)
        a = g * _sigmoid(g) * u
        acc_ref[...] += jnp.dot(a.astype(BF16), wd_ref[...], preferred_element_type=F32)

        @pl.when(j == nj - 1)
        def _():
            h_ref[...] = x_ref[...] + 0.5 * acc_ref[...]

    row = pl.BlockSpec((tm, d), lambda i, j: (i, 0))
    act = pl.BlockSpec((tm, tn), lambda i, j: (i, j))
    return pl.pallas_call(
        body, name=name, grid=(t // tm, nj),
        in_specs=[row, row] + [pl.BlockSpec((tn, d), lambda i, j: (j, 0))] * 3,
        out_specs=[row, act, act],
        out_shape=[jax.ShapeDtypeStruct((t, d), F32), jax.ShapeDtypeStruct((t, D_FF), BF16), jax.ShapeDtypeStruct((t, D_FF), BF16)],
        scratch_shapes=[pltpu.VMEM((tm, d), F32)], compiler_params=_cparams("parallel", "arbitrary"),
    )(n, x, wg, wu, wd)


def _ffn_bwd_act(dh, g, u, wg, wu, wd, name):
    t, d = dh.shape
    tm, tn = _row_tile(t), FF_TILE
    nj = D_FF // tn

    def body(dh_ref, g_ref, u_ref, wg_ref, wu_ref, wd_ref, dn_ref, dg_ref, du_ref, a_ref, acc_ref):
        j = pl.program_id(1)

        @pl.when(j == 0)
        def _():
            acc_ref[...] = jnp.zeros_like(acc_ref)

        dhb = (0.5 * dh_ref[...]).astype(BF16)
        da = lax.dot_general(dhb, wd_ref[...], _DIMS["nt"], preferred_element_type=F32)
        gv = g_ref[...].astype(F32)
        uv = u_ref[...].astype(F32)
        sg = _sigmoid(gv)
        s = gv * sg
        dg = (da * uv * (sg * (1.0 + gv * (1.0 - sg)))).astype(BF16)
        du = (da * s).astype(BF16)
        dg_ref[...] = dg
        du_ref[...] = du
        a_ref[...] = (s * uv).astype(BF16)
        acc_ref[...] += (jnp.dot(dg, wg_ref[...], preferred_element_type=F32)
                         + jnp.dot(du, wu_ref[...], preferred_element_type=F32))

        @pl.when(j == nj - 1)
        def _():
            dn_ref[...] = acc_ref[...]

    row = pl.BlockSpec((tm, d), lambda i, j: (i, 0))
    act = pl.BlockSpec((tm, tn), lambda i, j: (i, j))
    return pl.pallas_call(
        body, name=name, grid=(t // tm, nj),
        in_specs=[row, act, act] + [pl.BlockSpec((tn, d), lambda i, j: (j, 0))] * 3,
        out_specs=[row, act, act, act],
        out_shape=[jax.ShapeDtypeStruct((t, d), F32)] + [jax.ShapeDtypeStruct((t, D_FF), BF16)] * 3,
        scratch_shapes=[pltpu.VMEM((tm, d), F32)], compiler_params=_cparams("parallel", "arbitrary"),
    )(dh, g, u, wg, wu, wd)


def _ffn_forward(x, gain, wg, wu, wd, tag):
    n = _rms_fwd(x, gain, f"{tag}_norm")
    h, g, u = _ffn_fwd(n, x, wg, wu, wd, f"{tag}_fwd")
    return h, (n, g, u)


def _ffn_backward(dh, x, gain, wg, wu, wd, saved, tag, push):
    n, g, u = saved
    dn, dg, du, a = _ffn_bwd_act(dh, g, u, wg, wu, wd, f"{tag}_bwd_act")
    kw = dict(mode="tn", out_dtype=BF16, tm=FF_TILE, tn=1024, tk=2048)
    d_wg = _matmul(dg, n, name=f"{tag}_dwg", **kw)
    d_wu = _matmul(du, n, name=f"{tag}_dwu", **kw)
    d_wd = _matmul(a, dh, scale=0.5, name=f"{tag}_dwd", **kw)
    token = push({f"{tag}_w_gate": d_wg, f"{tag}_w_up": d_wu, f"{tag}_w_down": d_wd})
    return _rms_bwd(x, gain + token, dn, dh, f"{tag}_norm_bwd")


CONV_COLS = 512
HALO = 8
CONV_STRIP = 32
CONV_ROWS = 1024


def _conv_fwd(p, w, b, name):
    t = p.shape[0]
    tm = min(CONV_ROWS, t)
    c0 = P_XBC // CONV_COLS

    def body(x_ref, halo_ref, w_ref, b_ref, o_ref, s_ref):
        i = pl.program_id(1)
        s_ref[0:HALO, :] = jnp.where(i > 0, halo_ref[...].astype(F32), 0.0)
        s_ref[HALO:HALO + tm, :] = x_ref[...].astype(F32)
        wv = w_ref[...]
        bv = b_ref[...]
        for r0 in range(0, tm, CONV_STRIP):
            acc = bv + wv[0:1, :] * s_ref[HALO - 3 + r0:HALO - 3 + r0 + CONV_STRIP, :]
            for k in range(1, SSD_CONV):
                acc = acc + wv[k:k + 1, :] * s_ref[HALO - 3 + k + r0:HALO - 3 + k + r0 + CONV_STRIP, :]
            o_ref[r0:r0 + CONV_STRIP, :] = (acc * _sigmoid(acc)).astype(o_ref.dtype)

    return pl.pallas_call(
        body, name=name, grid=(XBC // CONV_COLS, t // tm),
        in_specs=[pl.BlockSpec((tm, CONV_COLS), lambda j, i: (i, c0 + j)),
                  pl.BlockSpec((HALO, CONV_COLS), lambda j, i: (jnp.maximum(i * (tm // HALO) - 1, 0), c0 + j)),
                  pl.BlockSpec((SSD_CONV, CONV_COLS), lambda j, i: (0, j)), pl.BlockSpec((1, CONV_COLS), lambda j, i: (0, j))],
        out_specs=pl.BlockSpec((tm, CONV_COLS), lambda j, i: (i, j)),
        out_shape=jax.ShapeDtypeStruct((t, XBC), BF16),
        scratch_shapes=[pltpu.VMEM((tm + HALO, CONV_COLS), F32)], compiler_params=_cparams("parallel", "parallel"),
    )(p, p, w, b)


def _conv_bwd_act(p, dy, w, b, col0, name):
    t, cols = dy.shape
    tm = min(CONV_ROWS, t)
    c0 = (P_XBC + col0) // CONV_COLS
    w0 = col0 // CONV_COLS

    def body(x_ref, halo_ref, dy_ref, w_ref, b_ref, da_ref, dw_ref, db_ref, s_ref):
        i = pl.program_id(1)

        @pl.when(i == 0)
        def _():
            dw_ref[...] = jnp.zeros_like(dw_ref)
            db_ref[...] = jnp.zeros_like(db_ref)

        s_ref[0:HALO, :] = jnp.where(i > 0, halo_ref[...].astype(F32), 0.0)
        s_ref[HALO:HALO + tm, :] = x_ref[...].astype(F32)
        wv = w_ref[...]
        bv = b_ref[...]
        fold = lambda v: jnp.sum(v.reshape(CONV_STRIP // 8, 8, CONV_COLS), axis=0)
        sums = [jnp.zeros((8, CONV_COLS), F32) for _ in range(SSD_CONV + 1)]
        for r0 in range(0, tm, CONV_STRIP):
            taps = [s_ref[HALO - 3 + k + r0:HALO - 3 + k + r0 + CONV_STRIP, :] for k in range(SSD_CONV)]
            acc = bv + wv[0:1, :] * taps[0]
            for k in range(1, SSD_CONV):
                acc = acc + wv[k:k + 1, :] * taps[k]
            sg = _sigmoid(acc)
            dacc = dy_ref[r0:r0 + CONV_STRIP, :].astype(F32) * (sg * (1.0 + acc * (1.0 - sg)))
            da_ref[r0:r0 + CONV_STRIP, :] = dacc.astype(BF16)
            for k in range(SSD_CONV):
                sums[k] = sums[k] + fold(dacc * taps[k])
            sums[SSD_CONV] = sums[SSD_CONV] + fold(dacc)
        for k in range(SSD_CONV):
            dw_ref[k:k + 1, :] += jnp.sum(sums[k], axis=0, keepdims=True)
        db_ref[...] += jnp.sum(sums[SSD_CONV], axis=0, keepdims=True)

    return pl.pallas_call(
        body, name=name, grid=(cols // CONV_COLS, t // tm),
        in_specs=[pl.BlockSpec((tm, CONV_COLS), lambda j, i: (i, c0 + j)),
                  pl.BlockSpec((HALO, CONV_COLS), lambda j, i: (jnp.maximum(i * (tm // HALO) - 1, 0), c0 + j)),
                  pl.BlockSpec((tm, CONV_COLS), lambda j, i: (i, j)),
                  pl.BlockSpec((SSD_CONV, CONV_COLS), lambda j, i: (0, w0 + j)), pl.BlockSpec((1, CONV_COLS), lambda j, i: (0, w0 + j))],
        out_specs=[pl.BlockSpec((tm, CONV_COLS), lambda j, i: (i, j)), pl.BlockSpec((SSD_CONV, CONV_COLS), lambda j, i: (0, j)),
                   pl.BlockSpec((1, CONV_COLS), lambda j, i: (0, j))],
        out_shape=[jax.ShapeDtypeStruct((t, cols), BF16), jax.ShapeDtypeStruct((SSD_CONV, cols), F32), jax.ShapeDtypeStruct((1, cols), F32)],
        scratch_shapes=[pltpu.VMEM((tm + HALO, CONV_COLS), F32)], compiler_params=_cparams("parallel", "arbitrary"),
    )(p, p, dy, w, b)


def _conv_bwd_dx(dacc, w, col0, dp, name):
    t, cols = dacc.shape
    tm = min(CONV_ROWS, t)
    nt = t // tm
    w0 = col0 // CONV_COLS
    c0 = (P_XBC + col0) // CONV_COLS

    def body(d_ref, halo_ref, w_ref, dp_ref, o_ref, s_ref):
        del dp_ref
        i = pl.program_id(1)
        s_ref[0:tm, :] = d_ref[...].astype(F32)
        s_ref[tm:tm + HALO, :] = jnp.where(i < nt - 1, halo_ref[...].astype(F32), 0.0)
        wv = w_ref[...]
        for r0 in range(0, tm, CONV_STRIP):
            acc = wv[3:4, :] * s_ref[r0:r0 + CONV_STRIP, :]
            for k in range(SSD_CONV - 1):
                acc = acc + wv[k:k + 1, :] * s_ref[3 - k + r0:3 - k + r0 + CONV_STRIP, :]
            o_ref[r0:r0 + CONV_STRIP, :] = acc.astype(o_ref.dtype)

    return pl.pallas_call(
        body, name=name, grid=(cols // CONV_COLS, nt),
        in_specs=[pl.BlockSpec((tm, CONV_COLS), lambda j, i: (i, j)),
                  pl.BlockSpec((HALO, CONV_COLS), lambda j, i: (jnp.minimum((i + 1) * (tm // HALO), t // HALO - 1), j)),
                  pl.BlockSpec((SSD_CONV, CONV_COLS), lambda j, i: (0, w0 + j)), ANY],
        out_specs=pl.BlockSpec((tm, CONV_COLS), lambda j, i: (i, c0 + j)),
        out_shape=jax.ShapeDtypeStruct(dp.shape, dp.dtype), input_output_aliases={3: 0},
        scratch_shapes=[pltpu.VMEM((tm + HALO, CONV_COLS), F32)], compiler_params=_cparams("parallel", "parallel"),
    )(dacc, dacc, w, dp)


GROUP_COLS = SSD_INNER // SSD_GROUPS
PAIRS = GROUP_COLS // 128
HEADS_PER_GROUP = SSD_HEADS // SSD_GROUPS


def _dt_fwd(dt_raw, bias, name):
    t, n = dt_raw.shape
    tm = _row_tile(t)

    def body(x_ref, b_ref, o_ref):
        v = x_ref[...] + b_ref[...]
        o_ref[...] = jnp.maximum(v, 0.0) + jnp.log1p(jnp.exp(-jnp.abs(v)))

    row = pl.BlockSpec((tm, n), lambda i: (i, 0))
    vec = pl.BlockSpec((1, n), lambda i: (0, 0))
    return pl.pallas_call(body, name=name, grid=(t // tm,), in_specs=[row, vec], out_specs=row,
                          out_shape=jax.ShapeDtypeStruct((t, n), F32), compiler_params=_cparams("parallel"))(dt_raw, bias)


def _dt_bwd(ddt, dt_raw, bias, name):
    t, n = dt_raw.shape
    tm = _row_tile(t)

    def body(d_ref, x_ref, b_ref, o_ref, db_ref):
        @pl.when(pl.program_id(0) == 0)
        def _():
            db_ref[...] = jnp.zeros_like(db_ref)

        dr = d_ref[...] * _sigmoid(x_ref[...] + b_ref[...])
        o_ref[...] = dr.astype(o_ref.dtype)
        db_ref[...] += jnp.sum(dr, axis=0, keepdims=True)

    row = pl.BlockSpec((tm, n), lambda i: (i, 0))
    vec = pl.BlockSpec((1, n), lambda i: (0, 0))
    return pl.pallas_call(body, name=name, grid=(t // tm,), in_specs=[row, row, vec], out_specs=[row, vec],
                          out_shape=[jax.ShapeDtypeStruct((t, n), BF16), jax.ShapeDtypeStruct((1, n), F32)],
                          compiler_params=_cparams("arbitrary"))(ddt, dt_raw, bias)


SSD_STEP = 4


def _ssd_common(dt, dtt, a_log_w, a_log_t):
    l = CHUNK
    a = -jnp.exp(a_log_w)
    at = -jnp.exp(a_log_t)
    rowi = lax.broadcasted_iota(jnp.int32, (l, l), 0)
    coli = lax.broadcasted_iota(jnp.int32, (l, l), 1)
    tri = rowi >= coli
    lower = tri.astype(F32)
    upper = (rowi <= coli).astype(F32)
    acs = jnp.dot(lower, dt * a, precision=HI, preferred_element_type=F32)
    acst = jnp.dot(dtt * at, upper, precision=HI, preferred_element_type=F32)
    return a, acs, acst, tri, upper


def _pair_bc(w, lo, p):
    return jnp.where(lo, w[:, 2 * p:2 * p + 1], w[:, 2 * p + 1:2 * p + 2])


def _ssd_specs(t):
    rows = SSD_STEP * CHUNK
    assert t % rows == 0
    return t // rows, dict(
        xs=lambda cm: pl.BlockSpec((rows, GROUP_COLS), lambda g, c: (cm(c), g)),
        bm=lambda cm: pl.BlockSpec((rows, SSD_STATE), lambda g, c: (cm(c), SSD_INNER // SSD_STATE + g)),
        cmat=lambda cm: pl.BlockSpec((rows, SSD_STATE), lambda g, c: (cm(c), SSD_INNER // SSD_STATE + SSD_GROUPS + g)),
        dtw=lambda cm: pl.BlockSpec((1, rows, 128), lambda g, c: (g, cm(c), 0)),
        dtt=lambda cm: pl.BlockSpec((1, HEADS_PER_GROUP, rows), lambda g, c: (g, 0, cm(c))),
        wide=lambda cm: pl.BlockSpec((1, 1, 128), lambda g, c: (g, 0, 0)),
        tall=lambda cm: pl.BlockSpec((1, HEADS_PER_GROUP, 1), lambda g, c: (g, 0, 0)),
        grp=lambda cm: pl.BlockSpec((rows, GROUP_COLS), lambda g, c: (cm(c), g)),
        zp=lambda cm: pl.BlockSpec((rows, GROUP_COLS), lambda g, c: (cm(c), P_Z // GROUP_COLS + g)),
        vec=lambda cm: pl.BlockSpec((1, GROUP_COLS), lambda g, c: (0, g)),
        state=lambda cm: pl.BlockSpec((1, SSD_STEP, PAIRS, SSD_STATE, 128), lambda g, c: (g, cm(c), 0, 0, 0)),
    )


def _ssd_fwd(xc, p, hv, norm_g, name):
    t = xc.shape[0]
    nc, sp = _ssd_specs(t)
    ident = lambda c: c

    def body(xs_ref, b_ref, c_ref, dtw_ref, dtt_ref, aw_ref, at_ref, dk_ref, z_ref, ng_ref,
             y_ref, ys_ref, h_ref, st_ref):
        @pl.when(pl.program_id(1) == 0)
        def _():
            st_ref[...] = jnp.zeros_like(st_ref)

        lo = lax.broadcasted_iota(jnp.int32, (1, 128), 1) < 64
        dskip = dk_ref[0]
        for s in range(SSD_STEP):
            rows = slice(s * CHUNK, (s + 1) * CHUNK)
            dt = dtw_ref[0, rows, :]
            a, acs, acst, tri, _ = _ssd_common(dt, dtt_ref[0, :, rows], aw_ref[0], at_ref[0])
            ecs = jnp.exp(acs)
            alast = acs[CHUNK - 1:CHUNK, :]
            bmat, cmat = b_ref[rows, :], c_ref[rows, :]
            cb = lax.dot_general(cmat, bmat, _DIMS["nt"], preferred_element_type=F32)
            for pi in range(PAIRS):
                cols = slice(pi * 128, (pi + 1) * 128)
                x = xs_ref[rows, cols].astype(F32)
                xdt = x * _pair_bc(dt, lo, pi)
                ydiag = jnp.zeros((CHUNK, 128), F32)
                for r, mask in ((2 * pi, lo), (2 * pi + 1, jnp.logical_not(lo))):
                    lam = jnp.exp(jnp.where(tri, acs[:, r:r + 1] - acst[r:r + 1, :], -1e30))
                    m = (cb * lam).astype(BF16)
                    ydiag = ydiag + jnp.dot(m, jnp.where(mask, xdt, 0.0).astype(BF16), preferred_element_type=F32)
                ht = st_ref[pi]
                h_ref[0, s, pi] = ht
                yoff = jnp.dot(cmat, ht.astype(BF16), preferred_element_type=F32) * _pair_bc(ecs, lo, pi)
                y_ref[rows, cols] = (ydiag + yoff + _pair_bc(dskip, lo, pi) * x).astype(y_ref.dtype)
                alp = _pair_bc(alast, lo, pi)
                e = jnp.exp(alp - _pair_bc(acs, lo, pi))
                st = lax.dot_general(bmat, (xdt * e).astype(BF16), _DIMS["tn"], preferred_element_type=F32)
                st_ref[pi] = ht * jnp.exp(alp) + st
            zf = z_ref[rows, :].astype(F32)
            yg = y_ref[rows, :].astype(F32) * (zf * _sigmoid(zf))
            rstd = lax.rsqrt(jnp.mean(yg * yg, axis=-1, keepdims=True) + EPS)
            ys_ref[rows, :] = (yg * rstd * ng_ref[...]).astype(ys_ref.dtype)

    ins = ["xs", "bm", "cmat", "dtw", "dtt", "wide", "tall", "wide", "zp", "vec"]
    return pl.pallas_call(
        body, name=name, grid=(SSD_GROUPS, nc),
        in_specs=[sp[k](ident) for k in ins],
        out_specs=[sp["grp"](ident), sp["grp"](ident), sp["state"](ident)],
        out_shape=[jax.ShapeDtypeStruct((t, SSD_INNER), BF16), jax.ShapeDtypeStruct((t, SSD_INNER), BF16),
                   jax.ShapeDtypeStruct((SSD_GROUPS, t // CHUNK, PAIRS, SSD_STATE, 128), F32)],
        scratch_shapes=[pltpu.VMEM((PAIRS, SSD_STATE, 128), F32)], compiler_params=_cparams("parallel", "arbitrary"),
    )(xc, xc, xc, hv["dtw"], hv["dtt"], hv["alog_w"], hv["alog_t"], hv["dskip_w"], p, norm_g)


def _ssd_bwd(xc, p, hv, norm_g, y, dys, states, dp, name):
    t = xc.shape[0]
    nc, sp = _ssd_specs(t)
    rev = lambda c: nc - 1 - c

    def body(xs_ref, b_ref, c_ref, dtw_ref, dtt_ref, aw_ref, at_ref, dk_ref, z_ref, ng_ref,
             y_ref, dys_ref, h_ref, dp_ref,
             dxs_ref, db_ref, dc_ref, dz_ref, ddt_ref, hsum_ref, dng_ref, dst_ref):
        del dp_ref
        @pl.when(pl.program_id(1) == 0)
        def _():
            dst_ref[...] = jnp.zeros_like(dst_ref)
            hsum_ref[...] = jnp.zeros_like(hsum_ref)
            dng_ref[...] = jnp.zeros_like(dng_ref)

        lane = lax.broadcasted_iota(jnp.int32, (1, 128), 1)
        lo = lane < 64
        dskip = dk_ref[0]
        sel_r = lax.broadcasted_iota(jnp.int32, (128, 128), 0)
        sel_c = lax.broadcasted_iota(jnp.int32, (128, 128), 1)
        refs = (xs_ref, b_ref, c_ref, dtw_ref, dtt_ref, aw_ref, at_ref, dk_ref, z_ref, ng_ref, y_ref, dys_ref, h_ref,
                dxs_ref, db_ref, dc_ref, dz_ref, ddt_ref, hsum_ref, dng_ref, dst_ref)
        for s in reversed(range(SSD_STEP)):
            chunk_bwd(refs, slice(s * CHUNK, (s + 1) * CHUNK), s, lane, lo, dskip, sel_r, sel_c)

    def chunk_bwd(refs, rows, s, lane, lo, dskip, sel_r, sel_c):
        (xs_ref, b_ref, c_ref, dtw_ref, dtt_ref, aw_ref, at_ref, dk_ref, z_ref, ng_ref, y_ref, dys_ref, h_ref,
         dxs_ref, db_ref, dc_ref, dz_ref, ddt_ref, hsum_ref, dng_ref, dst_ref) = refs
        dt = dtw_ref[0, rows, :]
        a, acs, acst, tri, upper = _ssd_common(dt, dtt_ref[0, :, rows], aw_ref[0], at_ref[0])
        ecs = jnp.exp(acs)
        alast = acs[CHUNK - 1:CHUNK, :]
        bmat, cmat = b_ref[rows, :], c_ref[rows, :]
        cb = lax.dot_general(cmat, bmat, _DIMS["nt"], preferred_element_type=F32)

        zf = z_ref[rows, :].astype(F32)
        sg = _sigmoid(zf)
        sz = zf * sg
        yv = y_ref[rows, :].astype(F32)
        yg = yv * sz
        rstd = lax.rsqrt(jnp.mean(yg * yg, axis=-1, keepdims=True) + EPS)
        yhat = yg * rstd
        dysv = dys_ref[rows, :].astype(F32)
        dng_ref[...] += jnp.sum(dysv * yhat, axis=0, keepdims=True)
        dyh = dysv * ng_ref[...]
        dyg = rstd * (dyh - yhat * jnp.mean(dyh * yhat, axis=-1, keepdims=True))
        dz_ref[rows, :] = (dyg * yv * (sg * (1.0 + zf * (1.0 - sg)))).astype(dz_ref.dtype)
        dy_all = dyg * sz

        dal = jnp.zeros((CHUNK, 128), F32)
        ddtm = jnp.zeros((CHUNK, 128), F32)
        dalast = jnp.zeros((8, 128), F32)
        ddsk = jnp.zeros((8, 128), F32)
        dcb = jnp.zeros((CHUNK, CHUNK), F32)
        qcol = jnp.zeros((8, CHUNK), F32)
        sub8 = lax.broadcasted_iota(jnp.int32, (8, CHUNK), 0)
        dc_acc = jnp.zeros((CHUNK, SSD_STATE), F32)
        db_acc = jnp.zeros((CHUNK, SSD_STATE), F32)
        for pi in range(PAIRS):
            sel = (sel_c == 2 * pi + (sel_r >= 64).astype(jnp.int32)).astype(BF16)

            def hsum(v, sel=sel):
                return jnp.dot(v.astype(BF16), sel, preferred_element_type=F32)

            dyp = dy_all[:, pi * 128:(pi + 1) * 128]
            x = xs_ref[rows, pi * 128:(pi + 1) * 128].astype(F32)
            dtp = _pair_bc(dt, lo, pi)
            xdt = x * dtp
            dxdt = jnp.zeros((CHUNK, 128), F32)
            for r, mask in ((2 * pi, lo), (2 * pi + 1, jnp.logical_not(lo))):
                lam = jnp.exp(jnp.where(tri, acs[:, r:r + 1] - acst[r:r + 1, :], -1e30))
                m32 = cb * lam
                m = m32.astype(BF16)
                dyr = jnp.where(mask, dyp, 0.0).astype(BF16)
                xr = jnp.where(mask, xdt, 0.0).astype(BF16)
                dm = lax.dot_general(dyr, xr, _DIMS["nt"], preferred_element_type=F32)
                dcb = dcb + dm * lam
                q = dm * m32
                dal = dal + jnp.sum(q, axis=1, keepdims=True) * (lane == r).astype(F32)
                qcol = qcol + jnp.where(sub8 == r, jnp.sum(q, axis=0, keepdims=True), 0.0)
                dxdt = dxdt + lax.dot_general(m, dyr, _DIMS["tn"], preferred_element_type=F32)
            ht = h_ref[0, s, pi]
            htb = ht.astype(BF16)
            ecp = _pair_bc(ecs, lo, pi)
            yoff = jnp.dot(cmat, htb, preferred_element_type=F32) * ecp
            dg = (dyp * ecp).astype(BF16)
            dc_acc = dc_acc + lax.dot_general(dg, htb, _DIMS["nt"], preferred_element_type=F32)
            dht = lax.dot_general(cmat, dg, _DIMS["tn"], preferred_element_type=F32)
            dal = dal + hsum(dyp * yoff)
            dhn = dst_ref[pi]
            dhnb = dhn.astype(BF16)
            alp = _pair_bc(alast, lo, pi)
            e = jnp.exp(alp - _pair_bc(acs, lo, pi))
            xe = xdt * e
            db_acc = db_acc + lax.dot_general(xe.astype(BF16), dhnb, _DIMS["nt"], preferred_element_type=F32)
            dxe = jnp.dot(bmat, dhnb, preferred_element_type=F32)
            dxdt = dxdt + dxe * e
            tt = hsum(dxe * xe)
            dal = dal - tt
            dec = jnp.exp(alp)
            dalast = dalast + jnp.sum(tt, axis=0, keepdims=True) + hsum(
                jnp.broadcast_to(jnp.sum(dhn * ht, axis=0, keepdims=True) * dec, (8, 128)))
            dst_ref[pi] = dht + dhn * dec
            dxs_ref[rows, pi * 128:(pi + 1) * 128] = (_pair_bc(dskip, lo, pi) * dyp + dxdt * dtp).astype(dxs_ref.dtype)
            ddtm = ddtm + hsum(dxdt * x)
            ddsk = ddsk + hsum(jnp.broadcast_to(jnp.sum(dyp * x, axis=0, keepdims=True), (8, 128)))
        rowi = lax.broadcasted_iota(jnp.int32, (CHUNK, 128), 0)
        qcol_w = lax.dot_general(jnp.concatenate([qcol, jnp.zeros((CHUNK - 8, CHUNK), F32)], axis=0), (sel_r == sel_c).astype(F32),
                                 _DIMS["tn"], precision=HI, preferred_element_type=F32)
        dal = dal - qcol_w + jnp.where(rowi == CHUNK - 1, dalast[0:1, :], 0.0)
        dda = jnp.dot(upper, dal, precision=HI, preferred_element_type=F32)
        ddt_ref[0, rows, :] = ddtm + dda * a
        hsum_ref[0, 1:2, :] += jnp.sum(dda * dt, axis=0, keepdims=True) * a
        hsum_ref[0, 2:3, :] += ddsk[0:1, :]
        dcbb = dcb.astype(BF16)
        dc_ref[rows, :] = (jnp.dot(dcbb, bmat, preferred_element_type=F32) + dc_acc).astype(dc_ref.dtype)
        db_ref[rows, :] = (lax.dot_general(dcbb, cmat, _DIMS["tn"], preferred_element_type=F32) + db_acc).astype(db_ref.dtype)

    ins = ["xs", "bm", "cmat", "dtw", "dtt", "wide", "tall", "wide", "zp", "vec", "grp", "grp", "state"]
    col = lambda: pl.BlockSpec((SSD_STEP * CHUNK, SSD_STATE), lambda g, c: (rev(c), g))
    return pl.pallas_call(
        body, name=name, grid=(SSD_GROUPS, nc),
        in_specs=[sp[k](rev) for k in ins] + [ANY],
        out_specs=[sp["grp"](rev), col(), col(), sp["zp"](rev), sp["dtw"](rev),
                   pl.BlockSpec((1, 8, 128), lambda g, c: (g, 0, 0)), sp["vec"](rev)],
        out_shape=[jax.ShapeDtypeStruct((t, SSD_INNER), BF16), jax.ShapeDtypeStruct((t, SSD_GROUPS * SSD_STATE), BF16),
                   jax.ShapeDtypeStruct((t, SSD_GROUPS * SSD_STATE), BF16), jax.ShapeDtypeStruct(dp.shape, dp.dtype),
                   jax.ShapeDtypeStruct((SSD_GROUPS, t, 128), F32), jax.ShapeDtypeStruct((SSD_GROUPS, 8, 128), F32),
                   jax.ShapeDtypeStruct((1, SSD_INNER), F32)],
        input_output_aliases={len(ins): 3},
        scratch_shapes=[pltpu.VMEM((PAIRS, SSD_STATE, 128), F32)], compiler_params=_cparams("parallel", "arbitrary"),
    )(xc, xc, xc, hv["dtw"], hv["dtt"], hv["alog_w"], hv["alog_t"], hv["dskip_w"], p, norm_g, y, dys, states, dp)


def _wide(v):
    return jnp.pad(v.reshape(SSD_GROUPS, 1, HEADS_PER_GROUP), ((0, 0), (0, 0), (0, 128 - HEADS_PER_GROUP)))


def _head_inputs(dt, a_log, d_skip):
    t = dt.shape[0]
    g = dt[:, :SSD_HEADS].reshape(t, SSD_GROUPS, HEADS_PER_GROUP)
    return dict(
        dtw=jnp.pad(jnp.transpose(g, (1, 0, 2)), ((0, 0), (0, 0), (0, 128 - HEADS_PER_GROUP))),
        dtt=jnp.transpose(g, (1, 2, 0)),
        alog_w=_wide(a_log), alog_t=a_log.reshape(SSD_GROUPS, HEADS_PER_GROUP, 1),
        dskip_w=_wide(d_skip),
    )


def _gelu(x):
    return 0.5 * x * (1.0 + lax.erf(x * (1.0 / math.sqrt(2.0))))


def _gelu_grad(x):
    return 0.5 * (1.0 + lax.erf(x * (1.0 / math.sqrt(2.0)))) + x * jnp.exp(-0.5 * x * x) * (1.0 / math.sqrt(2.0 * math.pi))


def _tril_mask():
    r = lax.broadcasted_iota(jnp.int32, (CHUNK, CHUNK), 0)
    c = lax.broadcasted_iota(jnp.int32, (CHUNK, CHUNK), 1)
    return r >= c


def _gmlp_fwd(p, v_gain, w_s, b_col, name):
    t = p.shape[0]
    tm = _row_tile(t)
    u0 = P_UV // GMLP_W

    def body(u_ref, v_ref, gn_ref, ws_ref, bs_ref, o_ref):
        v = _gelu(v_ref[...].astype(F32))
        v = (v * lax.rsqrt(jnp.mean(v * v, axis=-1, keepdims=True) + EPS) * gn_ref[...]).astype(BF16)
        tril = _tril_mask()
        wm = [jnp.where(tril, ws_ref[g], 0.0).astype(BF16) for g in range(GMLP_GROUPS)]
        for k in range(tm // CHUNK):
            rows = slice(k * CHUNK, (k + 1) * CHUNK)
            for g in range(GMLP_GROUPS):
                cols = slice(g * 128, (g + 1) * 128)
                mixed = jnp.dot(wm[g], v[rows, cols], preferred_element_type=F32) + bs_ref[g]
                o_ref[rows, cols] = (_gelu(u_ref[rows, cols].astype(F32)) * mixed).astype(o_ref.dtype)

    return pl.pallas_call(
        body, name=name, grid=(t // tm,),
        in_specs=[pl.BlockSpec((tm, GMLP_W), lambda i: (i, u0)), pl.BlockSpec((tm, GMLP_W), lambda i: (i, u0 + 1)),
                  pl.BlockSpec((1, GMLP_W), lambda i: (0, 0)), pl.BlockSpec((GMLP_GROUPS, CHUNK, CHUNK), lambda i: (0, 0, 0)),
                  pl.BlockSpec((GMLP_GROUPS, CHUNK, 1), lambda i: (0, 0, 0))],
        out_specs=pl.BlockSpec((tm, GMLP_W), lambda i: (i, 0)),
        out_shape=jax.ShapeDtypeStruct((t, GMLP_W), BF16), compiler_params=_cparams("parallel"),
    )(p, p, v_gain, w_s, b_col)


def _gmlp_bwd(p, dy, v_gain, w_s, b_col, dp, name):
    t = p.shape[0]
    tm = _row_tile(t)
    u0 = P_UV // GMLP_W

    def body(u_ref, v_ref, dy_ref, gn_ref, ws_ref, bs_ref, dp_ref, duv_ref, dws_ref, dbs_ref, dgn_ref, dvn_ref):
        del dp_ref

        @pl.when(pl.program_id(0) == 0)
        def _():
            dws_ref[...] = jnp.zeros_like(dws_ref)
            dbs_ref[...] = jnp.zeros_like(dbs_ref)
            dgn_ref[...] = jnp.zeros_like(dgn_ref)

        vraw = v_ref[...].astype(F32)
        va = _gelu(vraw)
        rstd = lax.rsqrt(jnp.mean(va * va, axis=-1, keepdims=True) + EPS)
        vhat = va * rstd
        gain = gn_ref[...]
        vn = (vhat * gain).astype(BF16)
        tril = _tril_mask()
        wm = [jnp.where(tril, ws_ref[g], 0.0).astype(BF16) for g in range(GMLP_GROUPS)]
        for k in range(tm // CHUNK):
            rows = slice(k * CHUNK, (k + 1) * CHUNK)
            for g in range(GMLP_GROUPS):
                cols = slice(g * 128, (g + 1) * 128)
                uraw = u_ref[rows, cols].astype(F32)
                vb = vn[rows, cols]
                mixed = jnp.dot(wm[g], vb, preferred_element_type=F32) + bs_ref[g]
                dyb = dy_ref[rows, cols].astype(F32)
                duv_ref[rows, cols] = (dyb * mixed * _gelu_grad(uraw)).astype(duv_ref.dtype)
                dmix = dyb * _gelu(uraw)
                dmb = dmix.astype(BF16)
                dws_ref[g] += jnp.where(tril, lax.dot_general(dmb, vb, _DIMS["nt"], preferred_element_type=F32), 0.0)
                dbs_ref[g] += jnp.sum(dmix, axis=1, keepdims=True)
                dvn_ref[rows, cols] = lax.dot_general(wm[g], dmb, _DIMS["tn"], preferred_element_type=F32)
        dvn = dvn_ref[...]
        dgn_ref[...] += jnp.sum(dvn * vhat, axis=0, keepdims=True)
        dvh = dvn * gain
        dva = rstd * (dvh - vhat * jnp.mean(dvh * vhat, axis=-1, keepdims=True))
        duv_ref[:, GMLP_W:2 * GMLP_W] = (dva * _gelu_grad(vraw)).astype(duv_ref.dtype)

    return pl.pallas_call(
        body, name=name, grid=(t // tm,),
        in_specs=[pl.BlockSpec((tm, GMLP_W), lambda i: (i, u0)), pl.BlockSpec((tm, GMLP_W), lambda i: (i, u0 + 1)),
                  pl.BlockSpec((tm, GMLP_W), lambda i: (i, 0)),
                  pl.BlockSpec((1, GMLP_W), lambda i: (0, 0)), pl.BlockSpec((GMLP_GROUPS, CHUNK, CHUNK), lambda i: (0, 0, 0)),
                  pl.BlockSpec((GMLP_GROUPS, CHUNK, 1), lambda i: (0, 0, 0)), ANY],
        out_specs=[pl.BlockSpec((tm, 2 * GMLP_W), lambda i: (i, P_UV // (2 * GMLP_W))),
                   pl.BlockSpec((GMLP_GROUPS, CHUNK, CHUNK), lambda i: (0, 0, 0)),
                   pl.BlockSpec((GMLP_GROUPS, CHUNK, 1), lambda i: (0, 0, 0)), pl.BlockSpec((1, GMLP_W), lambda i: (0, 0))],
        out_shape=[jax.ShapeDtypeStruct(dp.shape, dp.dtype), jax.ShapeDtypeStruct((GMLP_GROUPS, CHUNK, CHUNK), F32),
                   jax.ShapeDtypeStruct((GMLP_GROUPS, CHUNK, 1), F32), jax.ShapeDtypeStruct((1, GMLP_W), F32)],
        input_output_aliases={6: 0},
        scratch_shapes=[pltpu.VMEM((tm, GMLP_W), F32)], compiler_params=_cparams("arbitrary"),
    )(p, p, dy, v_gain, w_s, b_col, dp)


def _head_masks():
    lane = lax.broadcasted_iota(jnp.int32, (1, MEM_W), 1)
    return [(lane >= h * 64) & (lane < (h + 1) * 64) for h in range(MEM_HEADS)]


def _mem_fwd(p, kv, name):
    t = p.shape[0]
    tm = _row_tile(t)
    q0 = P_Q // MEM_W

    def body(q_ref, kv_ref, o_ref):
        q = q_ref[...]
        k = kv_ref[:, 0:MEM_W].astype(BF16)
        v = kv_ref[:, MEM_W:2 * MEM_W].astype(BF16)
        out = jnp.zeros((tm, MEM_W), F32)
        for mask in _head_masks():
            s = lax.dot_general(jnp.where(mask, q, 0), k, _DIMS["nt"], preferred_element_type=F32) * 0.125
            e = jnp.exp(s - jnp.max(s, axis=-1, keepdims=True))
            pr = (e * (1.0 / jnp.sum(e, axis=-1, keepdims=True))).astype(BF16)
            out = out + jnp.where(mask, jnp.dot(pr, v, preferred_element_type=F32), 0.0)
        o_ref[...] = out.astype(o_ref.dtype)

    return pl.pallas_call(
        body, name=name, grid=(t // tm,),
        in_specs=[pl.BlockSpec((tm, MEM_W), lambda i: (i, q0)), pl.BlockSpec((MEM_LEN, 2 * MEM_W), lambda i: (0, 0))],
        out_specs=pl.BlockSpec((tm, MEM_W), lambda i: (i, 0)),
        out_shape=jax.ShapeDtypeStruct((t, MEM_W), BF16), compiler_params=_cparams("parallel"),
    )(p, kv)


def _mem_bwd(p, kv, dy, dp, name):
    t = p.shape[0]
    tm = _row_tile(t)
    q0 = P_Q // MEM_W
    assert P_W - P_Q == 2 * MEM_W

    def body(q_ref, kv_ref, dy_ref, dp_ref, dq_ref, dkv_ref):
        del dp_ref

        @pl.when(pl.program_id(0) == 0)
        def _():
            dkv_ref[...] = jnp.zeros_like(dkv_ref)

        q = q_ref[...]
        dy = dy_ref[...]
        k = kv_ref[:, 0:MEM_W].astype(BF16)
        v = kv_ref[:, MEM_W:2 * MEM_W].astype(BF16)
        dq = jnp.zeros((tm, MEM_W), F32)
        dk = jnp.zeros((MEM_LEN, MEM_W), F32)
        dv = jnp.zeros((MEM_LEN, MEM_W), F32)
        for mask in _head_masks():
            qh = jnp.where(mask, q, 0)
            dyh = jnp.where(mask, dy, 0)
            s = lax.dot_general(qh, k, _DIMS["nt"], preferred_element_type=F32) * 0.125
            e = jnp.exp(s - jnp.max(s, axis=-1, keepdims=True))
            pr = e * (1.0 / jnp.sum(e, axis=-1, keepdims=True))
            prb = pr.astype(BF16)
            dp = lax.dot_general(dyh, v, _DIMS["nt"], preferred_element_type=F32)
            ds = (pr * (dp - jnp.sum(dp * pr, axis=-1, keepdims=True)) * 0.125).astype(BF16)
            dq = dq + jnp.where(mask, jnp.dot(ds, k, preferred_element_type=F32), 0.0)
            dk = dk + lax.dot_general(ds, qh, _DIMS["tn"], preferred_element_type=F32)
            dv = dv + lax.dot_general(prb, dyh, _DIMS["tn"], preferred_element_type=F32)
        dq_ref[:, 0:MEM_W] = dq.astype(dq_ref.dtype)
        dq_ref[:, MEM_W:2 * MEM_W] = jnp.zeros((tm, MEM_W), dq_ref.dtype)
        dkv_ref[:, 0:MEM_W] += dk
        dkv_ref[:, MEM_W:2 * MEM_W] += dv

    return pl.pallas_call(
        body, name=name, grid=(t // tm,),
        in_specs=[pl.BlockSpec((tm, MEM_W), lambda i: (i, q0)), pl.BlockSpec((MEM_LEN, 2 * MEM_W), lambda i: (0, 0)),
                  pl.BlockSpec((tm, MEM_W), lambda i: (i, 0)), ANY],
        out_specs=[pl.BlockSpec((tm, 2 * MEM_W), lambda i: (i, P_Q // (2 * MEM_W))),
                   pl.BlockSpec((MEM_LEN, 2 * MEM_W), lambda i: (0, 0))],
        out_shape=[jax.ShapeDtypeStruct(dp.shape, dp.dtype), jax.ShapeDtypeStruct((MEM_LEN, 2 * MEM_W), F32)],
        input_output_aliases={3: 0}, compiler_params=_cparams("arbitrary"),
    )(p, kv, dy, dp)


def _merge_fwd(p, b_ssd, b_gmlp, b_mem, name):
    t = p.shape[0]
    tm = _row_tile(t)
    g0 = P_GL // D_MODEL

    def body(g1, g2, g3, b1, b2, b3, o_ref):
        def strip(rows, carry):
            acc = _sigmoid(g1[rows, :].astype(F32)) * b1[rows, :].astype(F32)
            acc = acc + _sigmoid(g2[rows, :].astype(F32)) * b2[rows, :].astype(F32)
            acc = acc + _sigmoid(g3[rows, :].astype(F32)) * b3[rows, :].astype(F32)
            o_ref[rows, :] = acc.astype(o_ref.dtype)
            return carry

        _strips(tm, strip, 0)

    row = pl.BlockSpec((tm, D_MODEL), lambda i: (i, 0))
    return pl.pallas_call(
        body, name=name, grid=(t // tm,),
        in_specs=[pl.BlockSpec((tm, D_MODEL), lambda i, k=k: (i, g0 + k)) for k in range(3)] + [row] * 3,
        out_specs=row, out_shape=jax.ShapeDtypeStruct((t, D_MODEL), BF16), compiler_params=_cparams("parallel"),
    )(p, p, p, b_ssd, b_gmlp, b_mem)


def _merge_bwd(p, dm, b_ssd, b_gmlp, b_mem, dp, name):
    t = p.shape[0]
    tm = _row_tile(t)
    g0 = P_GL // D_MODEL

    def body(g1, g2, g3, dm_ref, b1, b2, b3, dp_ref, d1, d2, d3, dgl_ref):
        del dp_ref

        def strip(rows, carry):
            dmv = dm_ref[rows, :].astype(F32)
            for k, (g_ref, b_ref, d_ref) in enumerate(((g1, b1, d1), (g2, b2, d2), (g3, b3, d3))):
                sg = _sigmoid(g_ref[rows, :].astype(F32))
                d_ref[rows, :] = (dmv * sg).astype(d_ref.dtype)
                dgl_ref[rows, k * D_MODEL:(k + 1) * D_MODEL] = (dmv * b_ref[rows, :].astype(F32) * sg * (1.0 - sg)).astype(dgl_ref.dtype)
            return carry

        _strips(tm, strip, 0)

    row = pl.BlockSpec((tm, D_MODEL), lambda i: (i, 0))
    return pl.pallas_call(
        body, name=name, grid=(t // tm,),
        in_specs=[pl.BlockSpec((tm, D_MODEL), lambda i, k=k: (i, g0 + k)) for k in range(3)] + [row] * 4 + [ANY],
        out_specs=[row, row, row, pl.BlockSpec((tm, 3 * D_MODEL), lambda i: (i, P_GL // (3 * D_MODEL)))],
        out_shape=[jax.ShapeDtypeStruct((t, D_MODEL), BF16)] * 3 + [jax.ShapeDtypeStruct(dp.shape, dp.dtype)],
        input_output_aliases={7: 3}, compiler_params=_cparams("parallel"),
    )(p, p, p, dm, b_ssd, b_gmlp, b_mem, dp)


def _local_step(x, mem, target, w, rest_weights, push):
    t = x.shape[0]
    mm = functools.partial(_matmul, tk=1024)

    h1, ffn1_saved = _ffn_forward(x, w["ffn1_norm"], w["ffn1_w_gate"], w["ffn1_w_up"], w["ffn1_w_down"], "ffn1")
    w = {**w, **rest_weights(h1)}
    n2 = _rms_fwd(h1, w["mix_norm"], "mix_norm")
    p = mm(n2, w["w_in_p"], mode="nt", out_dtype=BF16, tm=1024, tn=1536, name="in_proj")
    dt_raw = mm(n2, w["w_dt"], mode="nt", out_dtype=F32, tm=2048, tn=128, name="dt_proj")
    dt_bias = jnp.pad(w["ssd_dt_bias"], (0, 128 - SSD_HEADS)).reshape(1, 128)
    hv = _head_inputs(_dt_fwd(dt_raw, dt_bias, "dt_fwd"), w["ssd_a_log"], w["ssd_d"])
    xc = _conv_fwd(p, w["ssd_conv_w"], w["ssd_conv_b"], "conv_fwd")
    y_ssd_raw, y_ssd, states = _ssd_fwd(xc, p, hv, w["ssd_norm"], "ssd_fwd")
    b_col = w["gmlp_b_s"].reshape(GMLP_GROUPS, CHUNK, 1)
    y_gmlp = _gmlp_fwd(p, w["gmlp_v_norm"], w["gmlp_w_s"], b_col, "gmlp_fwd")
    mem_n = _rms_fwd(mem, w["mem_norm"], "mem_norm")
    kv = mm(mem_n, w["w_mem_kv"], mode="nn", out_dtype=F32, tm=256, tn=512, name="mem_kv")
    y_mem = _mem_fwd(p, kv, "mem_fwd")
    b_ssd = mm(y_ssd, w["w_branch_ssd"], mode="nn", out_dtype=BF16, tm=1024, tn=1024, tk=2048, name="branch_ssd")
    b_gmlp = mm(y_gmlp, w["w_branch_gmlp"], mode="nn", out_dtype=BF16, tm=1024, tn=1024, name="branch_gmlp")
    b_mem = mm(y_mem, w["w_branch_mem"], mode="nt", out_dtype=BF16, tm=2048, tn=1024, tk=MEM_W, name="branch_mem")
    merged = _merge_fwd(p, b_ssd, b_gmlp, b_mem, "merge_fwd")
    h2 = mm(merged, w["w_out"], mode="nn", out_dtype=F32, tm=1024, tn=1024, addend=h1, name="out_proj")
    h3, ffn2_saved = _ffn_forward(h2, w["ffn2_norm"], w["ffn2_w_gate"], w["ffn2_w_up"], w["ffn2_w_down"], "ffn2")
    dh3, d_final, loss = _loss_head(h3, w["final_norm"], target, "loss_head")

    g = {"final_norm": d_final}
    big = {}
    dh2, g["ffn2_norm"] = _ffn_backward(dh3, h2, w["ffn2_norm"], w["ffn2_w_gate"], w["ffn2_w_up"], w["ffn2_w_down"], ffn2_saved,
                                        "ffn2", functools.partial(push, 0))
    dmerged = mm(dh2, w["w_out"], mode="nt", out_dtype=BF16, tm=1024, tn=1024, name="out_proj_dx")
    big["w_out"] = mm(merged, dh2, mode="tn", out_dtype=BF16, tm=1024, tn=1024, tk=2048, name="out_proj_dw")
    db_ssd, db_gmlp, db_mem, dp = _merge_bwd(p, dmerged, b_ssd, b_gmlp, b_mem, lax.empty((t, P_W), BF16), "merge_bwd")
    dy_ssd = mm(db_ssd, w["w_branch_ssd"], mode="nt", out_dtype=BF16, tm=1024, tn=2048, name="branch_ssd_dx")
    dy_gmlp = mm(db_gmlp, w["w_branch_gmlp"], mode="nt", out_dtype=BF16, tm=1024, tn=1024, name="branch_gmlp_dx")
    dy_mem = mm(db_mem, w["w_branch_mem"], mode="nn", out_dtype=BF16, tm=2048, tn=256, name="branch_mem_dx")
    big["w_branch_ssd"] = mm(y_ssd, db_ssd, mode="tn", out_dtype=BF16, tm=1024, tn=1024, tk=2048, name="branch_ssd_dw")
    big["w_branch_gmlp"] = mm(y_gmlp, db_gmlp, mode="tn", out_dtype=BF16, tm=1024, tn=1024, tk=2048, name="branch_gmlp_dw")
    big["w_branch_mem"] = mm(db_mem, y_mem, mode="tn", out_dtype=BF16, tm=1024, tn=256, tk=2048, name="branch_mem_dw")
    dp, dkv = _mem_bwd(p, kv, dy_mem, dp, "mem_bwd")
    big["w_mem_kv"] = mm(mem_n, dkv, mode="tn", out_dtype=BF16, tm=1024, tn=512, tk=256, name="mem_kv_dw")
    dmem_n = mm(dkv, w["w_mem_kv"], mode="nt", out_dtype=F32, tm=256, tn=1024, tk=512, name="mem_kv_dx")
    _, g["mem_norm"] = _rms_bwd(mem, w["mem_norm"], dmem_n, None, "mem_norm_bwd")
    dp, d_ws, d_bs, g["gmlp_v_norm"] = _gmlp_bwd(p, dy_gmlp, w["gmlp_v_norm"], w["gmlp_w_s"], b_col, dp, "gmlp_bwd")
    g["gmlp_w_s"] = d_ws
    g["gmlp_b_s"] = d_bs.reshape(GMLP_GROUPS, CHUNK)
    dxs, d_bm, d_cm, dp, ddt_w, hsums, g["ssd_norm"] = _ssd_bwd(xc, p, hv, w["ssd_norm"], y_ssd_raw, dy_ssd, states, dp, "ssd_bwd")
    heads = hsums[:, :, :HEADS_PER_GROUP]
    g["ssd_a_log"] = heads[:, 1, :].reshape(1, SSD_HEADS)
    g["ssd_d"] = heads[:, 2, :].reshape(1, SSD_HEADS)
    ddt = jnp.transpose(ddt_w[:, :, :HEADS_PER_GROUP], (1, 0, 2)).reshape(t, SSD_HEADS)
    ddt, d_bias = _dt_bwd(jnp.pad(ddt, ((0, 0), (0, 128 - SSD_HEADS))), dt_raw, dt_bias, "dt_bwd")
    g["ssd_dt_bias"] = d_bias[:, :SSD_HEADS]
    dws, dbs = [], []
    for dyc, col0, tag in ((dxs, 0, "x"), (d_bm, SSD_INNER, "b"), (d_cm, SSD_INNER + SSD_GROUPS * SSD_STATE, "c")):
        dacc, dw_c, db_c = _conv_bwd_act(p, dyc, w["ssd_conv_w"], w["ssd_conv_b"], col0, f"conv_bwd_act_{tag}")
        dp = _conv_bwd_dx(dacc, w["ssd_conv_w"], col0, dp, f"conv_bwd_dx_{tag}")
        dws.append(dw_c)
        dbs.append(db_c)
    g["ssd_conv_w"] = jnp.concatenate(dws, axis=1)
    g["ssd_conv_b"] = jnp.concatenate(dbs, axis=1)
    d_win_p = _matmul(dp, n2, mode="tn", out_dtype=BF16, tm=1536, tn=1024, tk=2048, name="in_proj_dw")
    d_wdt = mm(ddt, n2, mode="tn", out_dtype=BF16, tm=128, tn=1024, tk=2048, name="dt_proj_dw")
    sl = lambda a, o, n: a[o:o + n]
    big["w_in"] = jnp.concatenate([sl(d_win_p, P_Z, 2048), sl(d_win_p, P_XBC, XBC), d_wdt[:SSD_HEADS], sl(d_win_p, P_UV, 2048),
                                   sl(d_win_p, P_Q, MEM_W), sl(d_win_p, P_GL, 3 * D_MODEL)], axis=0)
    token = push(1, big)
    dn2 = mm(dp, w["w_in_p"], mode="nn", out_dtype=F32, tm=1024, tn=1024, tk=3584, name="in_proj_dx")
    dn2 = _matmul(ddt, w["w_dt"], mode="nn", out_dtype=F32, tm=1024, tn=1024, tk=128, addend=dn2, name="dt_proj_dx")
    dh1, g["mix_norm"] = _rms_bwd(h1, w["mix_norm"] + token, dn2, dh2, "mix_norm_bwd")
    dx, g["ffn1_norm"] = _ffn_backward(dh1, x, w["ffn1_norm"], w["ffn1_w_gate"], w["ffn1_w_up"], w["ffn1_w_down"], ffn1_saved,
                                       "ffn1", functools.partial(push, 2))
    return loss, dx, g


def _split_w_in(w_in_t):
    sl = lambda o, n: w_in_t[o:o + n]
    w_p = jnp.concatenate([sl(IN_GL, 3 * D_MODEL), sl(IN_Z, 2048), sl(IN_XBC, XBC), sl(IN_UV, 2048), sl(IN_Q, MEM_W),
                           jnp.zeros((P_W - P_USED, D_MODEL), w_in_t.dtype)], axis=0)
    w_dt = jnp.pad(sl(IN_DT, SSD_HEADS), ((0, 128 - SSD_HEADS), (0, 0)))
    return w_p, w_dt


def _pick_tile(rows, cap=512):
    best = None
    for tile in range(8, min(rows, cap) + 1, 8):
        if rows % tile == 0:
            best = tile
    return best if best is not None else rows


def _adamw(w, g, m, v, name):
    rows, lanes = w.shape
    tile = _pick_tile(rows, cap=max(8, (512 * 1024 // lanes) // 8 * 8))
    c1 = 1.0 / (1.0 - ADAM_B1 ** ADAM_STEP)
    c2 = 1.0 / (1.0 - ADAM_B2 ** ADAM_STEP)

    def body(w_ref, g_ref, m_ref, v_ref, d_ref, nm_ref, nv_ref):
        gv = g_ref[...]
        nm = ADAM_B1 * m_ref[...] + (1.0 - ADAM_B1) * gv
        nv = ADAM_B2 * v_ref[...] + (1.0 - ADAM_B2) * (gv * gv)
        nm_ref[...] = nm
        nv_ref[...] = nv
        d_ref[...] = -ADAM_LR * ((nm * c1) / (jnp.sqrt(nv * c2) + ADAM_EPS) + ADAM_WD * w_ref[...])

    blk = pl.BlockSpec((tile, lanes), lambda i: (i, 0))
    return pl.pallas_call(
        body, name=name, grid=(rows // tile,), in_specs=[blk] * 4, out_specs=[blk] * 3,
        out_shape=[jax.ShapeDtypeStruct((rows, lanes), F32)] * 3, compiler_params=_cparams("parallel"),
    )(w, g, m, v)


HBM = pl.BlockSpec(memory_space=pltpu.HBM)


def _place():
    x, y, c = lax.axis_index("x"), lax.axis_index("y"), lax.axis_index("c")
    chips = [(1 - x, y), (x, 1 - y), (1 - x, 1 - y)]
    return x, y, c, chips


def _gather_weights(slots):
    _, _, rh, lanes = slots.shape

    def body(in_ref, out_ref, send_sems, recv_sems):
        del in_ref
        x, y, c, chips = _place()
        me, sibling = (x, y, c), (x, y, 1 - c)

        def copy(k, src, dst, to):
            return pltpu.make_async_remote_copy(src_ref=src, dst_ref=dst, send_sem=send_sems.at[k], recv_sem=recv_sems.at[k],
                                                device_id=to, device_id_type=MESH)

        own = out_ref.at[2 * x + y, c]
        first = [copy(j, own, own, (*chip, c)) for j, chip in enumerate(chips)]
        for cp in first:
            cp.start()
        passed = []
        for j, (cx, cy) in enumerate(chips):
            landed = out_ref.at[2 * cx + cy, c]
            copy(j, landed, landed, me).wait_recv()
            fwd = copy(3 + j, landed, landed, sibling)
            fwd.start()
            passed.append(fwd)
        for j, (cx, cy) in enumerate(chips):
            other = out_ref.at[2 * cx + cy, 1 - c]
            copy(3 + j, other, other, me).wait_recv()
        for cp in first + passed:
            cp.wait_send()

    return pl.pallas_call(
        body, name="gather_weights", out_shape=jax.ShapeDtypeStruct(slots.shape, slots.dtype),
        in_specs=[HBM], out_specs=HBM, input_output_aliases={0: 0},
        scratch_shapes=[pltpu.SemaphoreType.DMA((6,)), pltpu.SemaphoreType.DMA((6,))],
    )(slots)


SEM = pl.BlockSpec(memory_space=pltpu.SEMAPHORE)
EFFECT = pltpu.SideEffectType.DATAFLOW_SIDE_EFFECTING
N_PEER = 3


def _sem_outs():
    return tuple(pltpu.SemaphoreType.DMA(()) for _ in range(2 * N_PEER))


def _gather_start(slots):
    def body(in_ref, *refs):
        del in_ref
        sems, thru, token = refs[:2 * N_PEER], refs[2 * N_PEER], refs[2 * N_PEER + 1]
        x, y, c, chips = _place()
        own = thru.at[2 * x + y, c]
        for j, chip in enumerate(chips):
            pltpu.make_async_remote_copy(src_ref=own, dst_ref=own, send_sem=sems[j], recv_sem=sems[N_PEER + j],
                                         device_id=(*chip, c), device_id_type=MESH).start()
        token[...] = jnp.zeros_like(token)

    out = pl.pallas_call(
        body, name="gather_rest_start",
        out_shape=_sem_outs() + (pltpu.HBM(slots.shape, slots.dtype), jax.ShapeDtypeStruct((8, 128), F32)),
        in_specs=(HBM,), out_specs=(SEM,) * (2 * N_PEER) + (HBM, pl.BlockSpec(memory_space=pltpu.VMEM)),
        input_output_aliases={0: 2 * N_PEER}, compiler_params=pltpu.CompilerParams(has_side_effects=EFFECT),
    )(pltpu.with_memory_space_constraint(slots, pltpu.HBM))
    return out[:2 * N_PEER], out[2 * N_PEER], out[2 * N_PEER + 1]


def _gather_wait(sems, thru, after):
    def body(in_ref, *refs):
        del in_ref
        sems, out_ref = refs[:2 * N_PEER], refs[2 * N_PEER + 1]
        x, y, c, chips = _place()
        own = out_ref.at[2 * x + y, c]
        for j, (cx, cy) in enumerate(chips):
            cp = pltpu.make_async_remote_copy(src_ref=own, dst_ref=out_ref.at[2 * cx + cy, c], send_sem=sems[j],
                                              recv_sem=sems[N_PEER + j], device_id=(cx, cy, c), device_id_type=MESH)
            cp.wait_send()
            cp.wait_recv()

    return pl.pallas_call(
        body, name="gather_rest_wait", out_shape=pltpu.HBM(thru.shape, thru.dtype),
        in_specs=(HBM,) + (SEM,) * (2 * N_PEER) + (pl.BlockSpec(memory_space=pl.ANY),), out_specs=HBM,
        input_output_aliases={0: 0}, compiler_params=pltpu.CompilerParams(has_side_effects=EFFECT),
    )(thru, *sems, after)


def _gather_forward(slots):
    def body(in_ref, out_ref, send_sems, recv_sems):
        del in_ref
        x, y, c, chips = _place()
        cps = []
        for j, (cx, cy) in enumerate(chips):
            landed = out_ref.at[2 * cx + cy, c]
            cps.append(pltpu.make_async_remote_copy(src_ref=landed, dst_ref=landed, send_sem=send_sems.at[j], recv_sem=recv_sems.at[j],
                                                    device_id=(x, y, 1 - c), device_id_type=MESH))
        for cp in cps:
            cp.start()
        for j, (cx, cy) in enumerate(chips):
            other = out_ref.at[2 * cx + cy, 1 - c]
            pltpu.make_async_remote_copy(src_ref=other, dst_ref=other, send_sem=send_sems.at[j], recv_sem=recv_sems.at[j],
                                         device_id=(x, y, 1 - c), device_id_type=MESH).wait_recv()
        for cp in cps:
            cp.wait_send()

    return pl.pallas_call(
        body, name="gather_rest_forward", out_shape=jax.ShapeDtypeStruct(slots.shape, slots.dtype),
        in_specs=[HBM], out_specs=HBM, input_output_aliases={0: 0},
        scratch_shapes=[pltpu.SemaphoreType.DMA((N_PEER,)), pltpu.SemaphoreType.DMA((N_PEER,))],
    )(slots)


def _scatter_start(pa, tag):
    ns, rh, lanes = pa.shape
    land = pltpu.with_memory_space_constraint(lax.empty((N_PEER, rh, lanes), pa.dtype), pltpu.HBM)

    def body(pa_ref, land_ref, *refs):
        x, y, c, chips = _place()
        for j, (cx, cy) in enumerate(chips):
            pltpu.make_async_remote_copy(src_ref=pa_ref.at[2 * cx + cy], dst_ref=land_ref.at[j], send_sem=refs[j],
                                         recv_sem=refs[N_PEER + j], device_id=(cx, cy, c), device_id_type=MESH).start()
        refs[-1][...] = jnp.zeros_like(refs[-1])

    out = pl.pallas_call(
        body, name=f"scatter_start_{tag}",
        out_shape=_sem_outs() + (pltpu.HBM(pa.shape, pa.dtype), pltpu.HBM(land.shape, land.dtype), jax.ShapeDtypeStruct((8, 128), F32)),
        in_specs=(HBM, HBM), out_specs=(SEM,) * (2 * N_PEER) + (HBM, HBM, pl.BlockSpec(memory_space=pltpu.VMEM)),
        input_output_aliases={0: 2 * N_PEER, 1: 2 * N_PEER + 1}, compiler_params=pltpu.CompilerParams(has_side_effects=EFFECT),
    )(pltpu.with_memory_space_constraint(pa, pltpu.HBM), land)
    return (out[:2 * N_PEER], out[2 * N_PEER], out[2 * N_PEER + 1]), out[2 * N_PEER + 2]


def _scatter_wait(sems, pa_thru, land_thru, after, tag):
    def body(pa_ref, land_ref, *refs):
        sems = refs[:2 * N_PEER]
        x, y, c, chips = _place()
        for j, (cx, cy) in enumerate(chips):
            cp = pltpu.make_async_remote_copy(src_ref=pa_ref.at[2 * cx + cy], dst_ref=land_ref.at[j], send_sem=sems[j],
                                              recv_sem=sems[N_PEER + j], device_id=(cx, cy, c), device_id_type=MESH)
            cp.wait_send()
            cp.wait_recv()

    return pl.pallas_call(
        body, name=f"scatter_wait_{tag}",
        out_shape=(pltpu.HBM(pa_thru.shape, pa_thru.dtype), pltpu.HBM(land_thru.shape, land_thru.dtype)),
        in_specs=(HBM, HBM) + (SEM,) * (2 * N_PEER) + (pl.BlockSpec(memory_space=pl.ANY),), out_specs=(HBM, HBM),
        input_output_aliases={0: 0, 1: 1}, compiler_params=pltpu.CompilerParams(has_side_effects=EFFECT),
    )(pa_thru, land_thru, *sems, after)


def _rs_swap(gp, tag):
    _, ns, rh, lanes = gp.shape

    def body(in_ref, out_ref, send_sem, recv_sem):
        x, y, c, _ = _place()
        cp = pltpu.make_async_remote_copy(src_ref=in_ref.at[1 - c], dst_ref=out_ref, send_sem=send_sem, recv_sem=recv_sem,
                                          device_id=(x, y, 1 - c), device_id_type=MESH)
        cp.start()
        cp.wait_send()
        cp.wait_recv()

    return pl.pallas_call(
        body, name=f"rs_swap_{tag}", out_shape=jax.ShapeDtypeStruct((ns, rh, lanes), gp.dtype), in_specs=[HBM], out_specs=HBM,
        scratch_shapes=[pltpu.SemaphoreType.DMA, pltpu.SemaphoreType.DMA],
    )(gp)


def _rs_tile(rh):
    return _pick_tile(rh, cap=512)


def _rs_add(gp, recv, c, tag):
    _, ns, rh, lanes = gp.shape
    tile = _rs_tile(rh)

    def body(c_ref, a_ref, b_ref, o_ref):
        o_ref[...] = (a_ref[...].astype(F32) + b_ref[...].astype(F32)).astype(o_ref.dtype)

    return pl.pallas_call(
        body, name=f"rs_add_{tag}", out_shape=jax.ShapeDtypeStruct((ns, rh, lanes), gp.dtype),
        grid_spec=pltpu.PrefetchScalarGridSpec(
            num_scalar_prefetch=1, grid=(ns, rh // tile),
            in_specs=[pl.BlockSpec((None, None, tile, lanes), lambda s, i, c_ref: (c_ref[0], s, i, 0)),
                      pl.BlockSpec((None, tile, lanes), lambda s, i, c_ref: (s, i, 0))],
            out_specs=pl.BlockSpec((None, tile, lanes), lambda s, i, c_ref: (s, i, 0))),
        compiler_params=_cparams("parallel", "parallel"),
    )(c, gp, recv)


def _rs_sum(pa, recv, place, tag):
    ns, rh, lanes = pa.shape
    tile = _rs_tile(rh)

    def body(place_ref, a_ref, r_ref, o_ref):
        acc = a_ref[...].astype(F32)
        for j in range(ns - 1):
            acc = acc + r_ref[j].astype(F32)
        o_ref[...] = acc

    return pl.pallas_call(
        body, name=f"rs_sum_{tag}", out_shape=jax.ShapeDtypeStruct((2, rh, lanes), F32),
        grid_spec=pltpu.PrefetchScalarGridSpec(
            num_scalar_prefetch=1, grid=(rh // tile,),
            in_specs=[pl.BlockSpec((None, tile, lanes), lambda i, place_ref: (place_ref[0], i, 0)),
                      pl.BlockSpec((ns - 1, tile, lanes), lambda i, place_ref: (0, i, 0))],
            out_specs=pl.BlockSpec((None, tile, lanes), lambda i, place_ref: (place_ref[1], i, 0))),
        compiler_params=_cparams("parallel"),
    )(place, pa, recv)


def _rs_share(halves, tag):
    def body(in_ref, out_ref, send_sem, recv_sem):
        del in_ref
        x, y, c, _ = _place()
        cp = pltpu.make_async_remote_copy(src_ref=out_ref.at[c], dst_ref=out_ref.at[c], send_sem=send_sem, recv_sem=recv_sem,
                                          device_id=(x, y, 1 - c), device_id_type=MESH)
        cp.start()
        other = out_ref.at[1 - c]
        pltpu.make_async_remote_copy(src_ref=other, dst_ref=other, send_sem=send_sem, recv_sem=recv_sem,
                                     device_id=(x, y, 1 - c), device_id_type=MESH).wait_recv()
        cp.wait_send()

    return pl.pallas_call(
        body, name=f"rs_share_{tag}", out_shape=jax.ShapeDtypeStruct(halves.shape, halves.dtype), in_specs=[HBM], out_specs=HBM,
        input_output_aliases={0: 0}, scratch_shapes=[pltpu.SemaphoreType.DMA, pltpu.SemaphoreType.DMA],
    )(halves)


N_DEV = 8
SMALL_ROWS = 160


def _allreduce_small(v):
    m_per, n = v.shape

    def body(x_ref, out_ref, all_ref, send_sems, recv_sems, local_sem):
        x, y, c, chips = _place()
        me, sibling = (x, y, c), (x, y, 1 - c)

        def rows(px, py, pc):
            return all_ref.at[pl.ds((4 * px + 2 * py + pc) * m_per, m_per), :]

        def copy(k, block, to, src=None):
            return pltpu.make_async_remote_copy(src_ref=rows(*block) if src is None else src, dst_ref=rows(*block),
                                                send_sem=send_sems.at[k], recv_sem=recv_sems.at[k], device_id=to, device_id_type=MESH)

        mine = pltpu.make_async_copy(x_ref, rows(*me), local_sem)
        mine.start()
        first = [copy(0, me, sibling, src=x_ref)]
        first += [copy(1 + j, me, (*chip, c), src=x_ref) for j, chip in enumerate(chips)]
        for cp in first:
            cp.start()
        passed = [copy(4 + j, (*chip, c), sibling) for j, chip in enumerate(chips)]
        for j, chip in enumerate(chips):
            copy(1 + j, (*chip, c), me).wait_recv()
            passed[j].start()
        copy(0, sibling, me).wait_recv()
        for j, chip in enumerate(chips):
            copy(4 + j, (*chip, 1 - c), me).wait_recv()
        for cp in first + passed:
            cp.wait_send()
        mine.wait()
        step = 32
        for r in range(0, m_per, step):
            acc = all_ref[r:r + step, :]
            for d in range(1, N_DEV):
                acc = acc + all_ref[d * m_per + r:d * m_per + r + step, :]
            out_ref[r:r + step, :] = acc

    vm = pl.BlockSpec(memory_space=pltpu.VMEM)
    return pl.pallas_call(
        body, name="allreduce_small", out_shape=jax.ShapeDtypeStruct((m_per, n), v.dtype), in_specs=[vm], out_specs=vm,
        scratch_shapes=[pltpu.VMEM((N_DEV * m_per, n), v.dtype), pltpu.SemaphoreType.DMA((7,)), pltpu.SemaphoreType.DMA((7,)),
                        pltpu.SemaphoreType.DMA],
        compiler_params=pltpu.CompilerParams(vmem_limit_bytes=V7X_VMEM_LIMIT),
    )(v)


BIG = {"ffn1_w_gate": ((D_MODEL, D_FF), 1), "ffn1_w_up": ((D_MODEL, D_FF), 1), "ffn1_w_down": ((D_FF, D_MODEL), 0),
       "ffn2_w_gate": ((D_MODEL, D_FF), 1), "ffn2_w_up": ((D_MODEL, D_FF), 1), "ffn2_w_down": ((D_FF, D_MODEL), 0),
       "w_in": ((D_MODEL, IN_WIDTH), 1), "w_mem_kv": ((D_MODEL, 2 * MEM_W), 0), "w_branch_ssd": ((SSD_INNER, D_MODEL), 0),
       "w_branch_gmlp": ((GMLP_W, D_MODEL), 0), "w_branch_mem": ((MEM_W, D_MODEL), 1), "w_out": ((D_MODEL, D_MODEL), 0)}
FFN1 = ("ffn1_w_gate", "ffn1_w_up", "ffn1_w_down")
FFN2 = ("ffn2_w_gate", "ffn2_w_up", "ffn2_w_down")
MIXER = ("w_out", "w_branch_ssd", "w_branch_gmlp", "w_branch_mem", "w_mem_kv", "w_in")
GATHER_GROUPS = (FFN1, FFN2 + MIXER)
REDUCE_GROUPS = (FFN2, MIXER, FFN1)
CONV_W_ROWS = 8


def _shard_rows_of(name):
    (a, b), _ = BIG[name]
    return a * b // N_SHARD // LANES


def _group_rows(names, extra=0):
    return -(-(sum(_shard_rows_of(n) for n in names) + extra) // 32) * 32

SMALL = [("ffn1_norm", 1), ("mix_norm", 1), ("mem_norm", 1), ("ssd_conv_b", 3), ("heads", 1), ("ssd_norm", 2),
         ("gmlp_v_norm", 1), ("gmlp_w_s", 128), ("gmlp_b_s", 1), ("ffn2_norm", 1), ("final_norm", 1), ("ssd_conv_w", 12)]
assert sum(n for _, n in SMALL) <= SMALL_ROWS
HEAD_VECS = ("ssd_dt_bias", "ssd_a_log", "ssd_d")


def _pack_small(vals, loss=None):
    parts = []
    for name, nrows in SMALL:
        if name == "heads":
            row = jnp.concatenate([vals[k].reshape(-1) for k in HEAD_VECS]
                                  + [jnp.zeros((1,), F32) if loss is None else loss.reshape(1)])
            parts.append(jnp.pad(row, (0, LANES - row.shape[0])).reshape(1, LANES))
        elif name in vals:
            parts.append(vals[name].reshape(nrows, LANES))
        else:
            parts.append(jnp.zeros((nrows, LANES), F32))
    buf = jnp.concatenate(parts, axis=0)
    return jnp.pad(buf, ((0, SMALL_ROWS - buf.shape[0]), (0, 0)))


def _unpack_small(buf):
    out, r = {}, 0
    for name, nrows in SMALL:
        blk = buf[r:r + nrows]
        r += nrows
        if name == "heads":
            for i, k in enumerate(HEAD_VECS):
                out[k] = blk[0, i * SSD_HEADS:(i + 1) * SSD_HEADS]
            out["loss"] = blk[0, 3 * SSD_HEADS]
        else:
            out[name] = blk
    return out


def _wire_shape(name):
    (a, b), axis = BIG[name]
    return (b, a) if axis == 1 else (a, b)


def _pack_weights(given, names, conv=False):
    parts = [(given[n][0].T if BIG[n][1] == 1 else given[n][0]).astype(BF16).reshape(_shard_rows_of(n), LANES) for n in names]
    if conv:
        pairs = lax.bitcast_convert_type(given["ssd_conv_w"], BF16).reshape(-1)
        parts.append(jnp.pad(pairs, (0, CONV_W_ROWS * LANES - pairs.shape[0])).reshape(CONV_W_ROWS, LANES))
    total = _group_rows(names, CONV_W_ROWS if conv else 0)
    packed = jnp.concatenate(parts, axis=0)
    packed = jnp.pad(packed, ((0, total - packed.shape[0]), (0, 0))).reshape(1, 2, total // 2, LANES)
    return jnp.broadcast_to(packed, (N_SHARD, 2, total // 2, LANES))


def _unpack_weights(slots, names, conv=False):
    rows = slots.reshape(N_SHARD, -1, LANES)
    out, r = {}, 0
    for name in names:
        n = _shard_rows_of(name)
        out[name] = rows[:, r:r + n].reshape(_wire_shape(name))
        r += n
    if conv:
        cols = XBC // N_SHARD
        pairs = rows[:, r:r + CONV_W_ROWS].reshape(N_SHARD, -1)[:, :SSD_CONV * cols * 2].reshape(N_SHARD, SSD_CONV, cols, 2)
        out["ssd_conv_w"] = jnp.transpose(lax.bitcast_convert_type(pairs, F32), (1, 0, 2)).reshape(SSD_CONV, XBC)
    return out


def _pack_grads(grads, names):
    total = _group_rows(names)
    gp = jnp.concatenate([grads[n].astype(BF16).reshape(N_SHARD, _shard_rows_of(n), LANES) for n in names], axis=1)
    gp = jnp.pad(gp, ((0, 0), (0, total - gp.shape[1]), (0, 0))).reshape(N_SHARD, 2, total // 2, LANES)
    return jnp.transpose(gp, (1, 0, 2, 3))


def kernel(x, mem, ffn1_norm, ffn1_w_gate, ffn1_w_up, ffn1_w_down, mix_norm, mem_norm, w_in, ssd_conv_w, ssd_conv_b, ssd_dt_bias, ssd_a_log, ssd_d, ssd_norm, gmlp_v_norm, gmlp_w_s, gmlp_b_s, w_mem_kv, w_branch_ssd, w_branch_gmlp, w_branch_mem, w_out, ffn2_norm, ffn2_w_gate, ffn2_w_up, ffn2_w_down, final_norm, loss_target, m_ffn1_norm, m_ffn1_w_gate, m_ffn1_w_up, m_ffn1_w_down, m_mix_norm, m_mem_norm, m_w_in, m_ssd_conv_w, m_ssd_conv_b, m_ssd_dt_bias, m_ssd_a_log, m_ssd_d, m_ssd_norm, m_gmlp_v_norm, m_gmlp_w_s, m_gmlp_b_s, m_w_mem_kv, m_w_branch_ssd, m_w_branch_gmlp, m_w_branch_mem, m_w_out, m_ffn2_norm, m_ffn2_w_gate, m_ffn2_w_up, m_ffn2_w_down, m_final_norm, v_ffn1_norm, v_ffn1_w_gate, v_ffn1_w_up, v_ffn1_w_down, v_mix_norm, v_mem_norm, v_w_in, v_ssd_conv_w, v_ssd_conv_b, v_ssd_dt_bias, v_ssd_a_log, v_ssd_d, v_ssd_norm, v_gmlp_v_norm, v_gmlp_w_s, v_gmlp_b_s, v_w_mem_kv, v_w_branch_ssd, v_w_branch_gmlp, v_w_branch_mem, v_w_out, v_ffn2_norm, v_ffn2_w_gate, v_ffn2_w_up, v_ffn2_w_down, v_final_norm):
    given = dict(x=x, mem=mem, ffn1_norm=ffn1_norm, ffn1_w_gate=ffn1_w_gate, ffn1_w_up=ffn1_w_up, ffn1_w_down=ffn1_w_down, mix_norm=mix_norm, mem_norm=mem_norm, w_in=w_in, ssd_conv_w=ssd_conv_w, ssd_conv_b=ssd_conv_b, ssd_dt_bias=ssd_dt_bias, ssd_a_log=ssd_a_log, ssd_d=ssd_d, ssd_norm=ssd_norm, gmlp_v_norm=gmlp_v_norm, gmlp_w_s=gmlp_w_s, gmlp_b_s=gmlp_b_s, w_mem_kv=w_mem_kv, w_branch_ssd=w_branch_ssd, w_branch_gmlp=w_branch_gmlp, w_branch_mem=w_branch_mem, w_out=w_out, ffn2_norm=ffn2_norm, ffn2_w_gate=ffn2_w_gate, ffn2_w_up=ffn2_w_up, ffn2_w_down=ffn2_w_down, final_norm=final_norm, loss_target=loss_target, m_ffn1_norm=m_ffn1_norm, m_ffn1_w_gate=m_ffn1_w_gate, m_ffn1_w_up=m_ffn1_w_up, m_ffn1_w_down=m_ffn1_w_down, m_mix_norm=m_mix_norm, m_mem_norm=m_mem_norm, m_w_in=m_w_in, m_ssd_conv_w=m_ssd_conv_w, m_ssd_conv_b=m_ssd_conv_b, m_ssd_dt_bias=m_ssd_dt_bias, m_ssd_a_log=m_ssd_a_log, m_ssd_d=m_ssd_d, m_ssd_norm=m_ssd_norm, m_gmlp_v_norm=m_gmlp_v_norm, m_gmlp_w_s=m_gmlp_w_s, m_gmlp_b_s=m_gmlp_b_s, m_w_mem_kv=m_w_mem_kv, m_w_branch_ssd=m_w_branch_ssd, m_w_branch_gmlp=m_w_branch_gmlp, m_w_branch_mem=m_w_branch_mem, m_w_out=m_w_out, m_ffn2_norm=m_ffn2_norm, m_ffn2_w_gate=m_ffn2_w_gate, m_ffn2_w_up=m_ffn2_w_up, m_ffn2_w_down=m_ffn2_w_down, m_final_norm=m_final_norm, v_ffn1_norm=v_ffn1_norm, v_ffn1_w_gate=v_ffn1_w_gate, v_ffn1_w_up=v_ffn1_w_up, v_ffn1_w_down=v_ffn1_w_down, v_mix_norm=v_mix_norm, v_mem_norm=v_mem_norm, v_w_in=v_w_in, v_ssd_conv_w=v_ssd_conv_w, v_ssd_conv_b=v_ssd_conv_b, v_ssd_dt_bias=v_ssd_dt_bias, v_ssd_a_log=v_ssd_a_log, v_ssd_d=v_ssd_d, v_ssd_norm=v_ssd_norm, v_gmlp_v_norm=v_gmlp_v_norm, v_gmlp_w_s=v_gmlp_w_s, v_gmlp_b_s=v_gmlp_b_s, v_w_mem_kv=v_w_mem_kv, v_w_branch_ssd=v_w_branch_ssd, v_w_branch_gmlp=v_w_branch_gmlp, v_w_branch_mem=v_w_branch_mem, v_w_out=v_w_out, v_ffn2_norm=v_ffn2_norm, v_ffn2_w_gate=v_ffn2_w_gate, v_ffn2_w_up=v_ffn2_w_up, v_ffn2_w_down=v_ffn2_w_down, v_final_norm=v_final_norm)
    weights = [n for n in given if n not in ("x", "mem", "loss_target") and not n.startswith(("m_", "v_"))]
    xi, yi, ci = lax.axis_index("x"), lax.axis_index("y"), lax.axis_index("c")
    chip = (2 * xi + yi).astype(jnp.int32)
    core = ci.astype(jnp.int32)
    conv_cols = XBC // N_SHARD

    first = _gather_weights(_pack_weights(given, FFN1))
    first, rest_slots = lax.optimization_barrier((first, _pack_weights(given, GATHER_GROUPS[1], conv=True)))
    sems, thru, token = _gather_start(rest_slots)
    w = _unpack_weights(first, FFN1)
    for name in ("ffn1_norm", "mix_norm", "mem_norm", "ssd_conv_b", "ssd_norm", "gmlp_v_norm", "ffn2_norm", "final_norm"):
        w[name] = given[name].reshape(1, -1)
    w["ffn1_norm"] = w["ffn1_norm"] + token[0:1, 0:1]
    for name in HEAD_VECS:
        w[name] = given[name].reshape(-1)
    w["gmlp_w_s"] = given["gmlp_w_s"][0]
    w["gmlp_b_s"] = given["gmlp_b_s"][0]

    def rest_weights(after):
        rest = _unpack_weights(_gather_forward(_gather_wait(sems, thru, after)), GATHER_GROUPS[1], conv=True)
        rest["w_in_p"], rest["w_dt"] = _split_w_in(rest.pop("w_in"))
        return rest

    pending = {}

    def push(k, group_grads):
        gp = _pack_grads(group_grads, REDUCE_GROUPS[k])
        pa = _rs_add(gp, _rs_swap(gp, k), core.reshape(1), k)
        pending[k], token = _scatter_start(pa, k)
        return token[0:1, 0:1]

    def reduced(k, after):
        pa, land = _scatter_wait(*pending[k], after, k)
        gsum = _rs_share(_rs_sum(pa, land, jnp.stack([chip, core]), k), k)
        rows = gsum.reshape(-1, LANES)
        out, r = {}, 0
        for name in REDUCE_GROUPS[k]:
            n = _shard_rows_of(name)
            a, b = given[name].shape[1:]
            out[name] = rows[r:r + n].reshape(b, a).T if BIG[name][1] == 1 else rows[r:r + n].reshape(a, b)
            r += n
        return out

    loss_part, grad_x, g = _local_step(x[0], mem[0], loss_target[0], w, rest_weights, push)

    grads, deltas, new_m, new_v = {}, {}, {}, {}

    def update(k, after):
        for name, gl in reduced(k, after).items():
            d, nm, nv = _adamw(given[name][0], gl, given["m_" + name][0], given["v_" + name][0], f"adamw_{name}")
            grads[name], deltas[name], new_m[name], new_v[name] = (a[None] for a in (gl, d, nm, nv))

    update(0, grad_x)
    update(1, deltas[REDUCE_GROUPS[0][-1]])

    small_vals = {k: g[k] for k, _ in SMALL if k != "heads"}
    small_vals.update({k: g[k] for k in HEAD_VECS})
    red = _unpack_small(_allreduce_small(_pack_small(small_vals, loss=loss_part[0, 0])))
    update(2, deltas[REDUCE_GROUPS[1][-1]])
    conv_g = lax.dynamic_slice_in_dim(red["ssd_conv_w"].reshape(SSD_CONV, XBC), chip * conv_cols, conv_cols, axis=1)
    d, nm, nv = _adamw(given["ssd_conv_w"][0], conv_g, given["m_ssd_conv_w"][0], given["v_ssd_conv_w"][0], "adamw_conv_w")
    grads["ssd_conv_w"], deltas["ssd_conv_w"], new_m["ssd_conv_w"], new_v["ssd_conv_w"] = (a[None] for a in (conv_g, d, nm, nv))
    for k in [k for k, _ in SMALL if k not in ("heads", "ssd_conv_w")] + list(HEAD_VECS):
        shape = given[k].shape
        as2d = lambda a: a.reshape(-1, shape[-1])
        d, nm, nv = _adamw(as2d(given[k]), as2d(red[k]), as2d(given["m_" + k]), as2d(given["v_" + k]), f"adamw_{k}")
        grads[k], deltas[k], new_m[k], new_v[k] = (a.reshape(shape) for a in (red[k], d, nm, nv))

    return (red["loss"], grad_x[None], *[grads[n] for n in weights], *[deltas[n] for n in weights],
            *[new_m[n] for n in weights], *[new_v[n] for n in weights])
```

```python
import functools
import math

import jax
import jax.numpy as jnp
from jax import lax
from jax.experimental import pallas as pl
from jax.experimental.pallas import tpu as pltpu

F32, BF16 = jnp.float32, jnp.bfloat16
HI = lax.Precision.HIGHEST
MESH = pl.DeviceIdType.MESH

D_MODEL = 1024
D_FF = 2816
MEM_LEN = 256
SSD_INNER = 2048
SSD_HEADS = 32
SSD_GROUPS = 4
SSD_STATE = 128
SSD_CONV = 4
CHUNK = 128
XBC = SSD_INNER + 2 * SSD_GROUPS * SSD_STATE
GMLP_W = 1024
GMLP_GROUPS = 8
MEM_W = 256
MEM_HEADS = 4
EPS = 1e-6
IN_WIDTH = 10528
IN_Z, IN_XBC, IN_DT, IN_UV, IN_Q, IN_GL = 0, 2048, 5120, 5152, 7200, 7456
P_GL, P_Z, P_XBC, P_UV, P_Q, P_W = 0, 3072, 5120, 8192, 10240, 10752
P_USED = 10496

ADAM_LR, ADAM_B1, ADAM_B2, ADAM_EPS, ADAM_WD, ADAM_STEP = 0.001, 0.9, 0.999, 1e-08, 0.01, 10

V7X_VMEM_LIMIT = 56 * 1024 * 1024
N_SHARD = 4
LANES = 1024


def _cparams(*sem):
    return pltpu.CompilerParams(dimension_semantics=sem, vmem_limit_bytes=V7X_VMEM_LIMIT)


ANY = pl.BlockSpec(memory_space=pl.ANY)


def _sigmoid(x):
    return 0.5 * jnp.tanh(0.5 * x) + 0.5


def _row_tile(t):
    return min(512, t)


_DIMS = {"nn": (((1,), (0,)), ((), ())), "nt": (((1,), (1,)), ((), ())), "tn": (((0,), (0,)), ((), ()))}


def _matmul(a, b, *, mode, out_dtype, tm, tn, tk, name, scale=1.0, addend=None):
    if mode == "tn":
        k_dim, m_dim = a.shape
    else:
        m_dim, k_dim = a.shape
    n_dim = b.shape[0] if mode == "nt" else b.shape[1]
    tm, tn, tk = min(tm, m_dim), min(tn, n_dim), min(tk, k_dim)
    assert m_dim % tm == 0 and n_dim % tn == 0 and k_dim % tk == 0, (name, a.shape, b.shape, tm, tn, tk)
    ni, nj, nk = m_dim // tm, n_dim // tn, k_dim // tk
    a_spec = pl.BlockSpec((tk, tm), lambda j, i, k: (k, i)) if mode == "tn" else pl.BlockSpec((tm, tk), lambda j, i, k: (i, k))
    b_spec = pl.BlockSpec((tn, tk), lambda j, i, k: (j, k)) if mode == "nt" else pl.BlockSpec((tk, tn), lambda j, i, k: (k, j))
    o_spec = pl.BlockSpec((tm, tn), lambda j, i, k: (i, j))
    dims = _DIMS[mode]
    has_add = addend is not None

    def body(*refs):
        a_ref, b_ref = refs[:2]
        r_ref = refs[2] if has_add else None
        o_ref = refs[2 + has_add]

        def finish(acc):
            r = acc * scale if scale != 1.0 else acc
            if has_add:
                r = r + r_ref[...].astype(F32)
            o_ref[...] = r.astype(o_ref.dtype)

        prod = lax.dot_general(a_ref[...].astype(BF16), b_ref[...].astype(BF16), dims, preferred_element_type=F32)
        if nk == 1:
            finish(prod)
            return
        acc_ref = refs[-1]
        k = pl.program_id(2)

        @pl.when(k == 0)
        def _():
            acc_ref[...] = prod

        @pl.when(k > 0)
        def _():
            acc_ref[...] += prod

        @pl.when(k == nk - 1)
        def _():
            finish(acc_ref[...])

    in_specs = [a_spec, b_spec] + ([o_spec] if has_add else [])
    args = (a, b) + ((addend,) if has_add else ())
    return pl.pallas_call(
        body, name=name, grid=(nj, ni, nk), in_specs=in_specs, out_specs=o_spec,
        out_shape=jax.ShapeDtypeStruct((m_dim, n_dim), out_dtype),
        scratch_shapes=[] if nk == 1 else [pltpu.VMEM((tm, tn), F32)],
        compiler_params=_cparams("parallel", "parallel", "arbitrary"),
    )(*args)


ROW_STRIP = 16


def _strips(tm, fn, init=None, rb=ROW_STRIP):
    def step(i, carry):
        return fn(pl.ds(pl.multiple_of(i * rb, rb), rb), carry)
    return lax.fori_loop(0, tm // rb, step, init, unroll=2)


def _rms_fwd(x, gain, name):
    t, d = x.shape
    tm = _row_tile(t)

    def body(x_ref, g_ref, o_ref):
        xv = x_ref[...]
        r = lax.rsqrt(jnp.mean(xv * xv, axis=-1, keepdims=True) + EPS)
        o_ref[...] = (xv * r * g_ref[...]).astype(o_ref.dtype)

    return pl.pallas_call(
        body, name=name, grid=(t // tm,),
        in_specs=[pl.BlockSpec((tm, d), lambda i: (i, 0)), pl.BlockSpec((1, d), lambda i: (0, 0))],
        out_specs=pl.BlockSpec((tm, d), lambda i: (i, 0)),
        out_shape=jax.ShapeDtypeStruct((t, d), BF16), compiler_params=_cparams("parallel"),
    )(x, gain)


def _rms_bwd(x, gain, dn, dres, name):
    t, d = x.shape
    tm = _row_tile(t)
    has_res = dres is not None

    def body(*refs):
        if has_res:
            x_ref, g_ref, dn_ref, r_ref, dx_ref, dg_ref = refs
        else:
            x_ref, g_ref, dn_ref, dx_ref, dg_ref = refs

        @pl.when(pl.program_id(0) == 0)
        def _():
            dg_ref[...] = jnp.zeros_like(dg_ref)

        xv = x_ref[...]
        r = lax.rsqrt(jnp.mean(xv * xv, axis=-1, keepdims=True) + EPS)
        xh = xv * r
        dnv = dn_ref[...].astype(F32)
        dg_ref[...] += jnp.sum(dnv * xh, axis=0, keepdims=True)
        dxh = dnv * g_ref[...]
        dx = r * (dxh - xh * jnp.mean(dxh * xh, axis=-1, keepdims=True))
        if has_res:
            dx = dx + r_ref[...]
        dx_ref[...] = dx

    row = pl.BlockSpec((tm, d), lambda i: (i, 0))
    vec = pl.BlockSpec((1, d), lambda i: (0, 0))
    in_specs = [row, vec, row] + ([row] if has_res else [])
    args = (x, gain, dn) + ((dres,) if has_res else ())
    return pl.pallas_call(
        body, name=name, grid=(t // tm,), in_specs=in_specs, out_specs=[row, vec],
        out_shape=[jax.ShapeDtypeStruct((t, d), F32), jax.ShapeDtypeStruct((1, d), F32)],
        compiler_params=_cparams("arbitrary"),
    )(*args)


def _loss_head(h, gain, target, name):
    t, d = h.shape
    tm = _row_tile(t)

    def body(h_ref, g_ref, t_ref, dh_ref, dg_ref, l_ref):
        @pl.when(pl.program_id(0) == 0)
        def _():
            dg_ref[...] = jnp.zeros_like(dg_ref)
            l_ref[...] = jnp.zeros_like(l_ref)

        xv = h_ref[...]
        g = g_ref[...]
        r = lax.rsqrt(jnp.mean(xv * xv, axis=-1, keepdims=True) + EPS)
        xh = xv * r
        err = xh * g - t_ref[...]
        l_ref[...] += 0.5 * jnp.sum(jnp.mean(err * err, axis=-1, keepdims=True), axis=0, keepdims=True)
        dy = err * (1.0 / d)
        dg_ref[...] += jnp.sum(dy * xh, axis=0, keepdims=True)
        dxh = dy * g
        dh_ref[...] = r * (dxh - xh * jnp.mean(dxh * xh, axis=-1, keepdims=True))

    row = pl.BlockSpec((tm, d), lambda i: (i, 0))
    vec = pl.BlockSpec((1, d), lambda i: (0, 0))
    return pl.pallas_call(
        body, name=name, grid=(t // tm,), in_specs=[row, vec, row],
        out_specs=[row, vec, pl.BlockSpec((1, 128), lambda i: (0, 0))],
        out_shape=[jax.ShapeDtypeStruct((t, d), F32), jax.ShapeDtypeStruct((1, d), F32), jax.ShapeDtypeStruct((1, 128), F32)],
        compiler_params=_cparams("arbitrary"),
    )(h, gain, target)


FF_TILE = 1408


def _ffn_fwd(n, x, wg, wu, wd, name):
    t, d = x.shape
    tm, tn = _row_tile(t), FF_TILE
    nj = D_FF // tn

    def body(n_ref, x_ref, wg_ref, wu_ref, wd_ref, h_ref, g_ref, u_ref, acc_ref):
        j = pl.program_id(1)

        @pl.when(j == 0)
        def _():
            acc_ref[...] = jnp.zeros_like(acc_ref)

        nb = n_ref[...]
        g = lax.dot_general(nb, wg_ref[...], _DIMS["nt"], preferred_element_type=F32)
        u = lax.dot_general(nb, wu_ref[...], _DIMS["nt"], preferred_element_type=F32)
        g_ref[...] = g.astype(BF16)
        u_ref[...] = u.astype(BF16)
        a = g * _sigmoid(g) * u
        acc_ref[...] += jnp.dot(a.astype(BF16), wd_ref[...], preferred_element_type=F32)

        @pl.when(j == nj - 1)
        def _():
            h_ref[...] = x_ref[...] + 0.5 * acc_ref[...]

    row = pl.BlockSpec((tm, d), lambda i, j: (i, 0))
    act = pl.BlockSpec((tm, tn), lambda i, j: (i, j))
    return pl.pallas_call(
        body, name=name, grid=(t // tm, nj),
        in_specs=[row, row] + [pl.BlockSpec((tn, d), lambda i, j: (j, 0))] * 3,
        out_specs=[row, act, act],
        out_shape=[jax.ShapeDtypeStruct((t, d), F32), jax.ShapeDtypeStruct((t, D_FF), BF16), jax.ShapeDtypeStruct((t, D_FF), BF16)],
        scratch_shapes=[pltpu.VMEM((tm, d), F32)], compiler_params=_cparams("parallel", "arbitrary"),
    )(n, x, wg, wu, wd)


def _ffn_bwd_act(dh, g, u, wg, wu, wd, name):
    t, d = dh.shape
    tm, tn = _row_tile(t), FF_TILE
    nj = D_FF // tn

    def body(dh_ref, g_ref, u_ref, wg_ref, wu_ref, wd_ref, dn_ref, dg_ref, du_ref, a_ref, acc_ref):
        j = pl.program_id(1)

        @pl.when(j == 0)
        def _():
            acc_ref[...] = jnp.zeros_like(acc_ref)

        dhb = (0.5 * dh_ref[...]).astype(BF16)
        da = lax.dot_general(dhb, wd_ref[...], _DIMS["nt"], preferred_element_type=F32)
        gv = g_ref[...].astype(F32)
        uv = u_ref[...].astype(F32)
        sg = _sigmoid(gv)
        s = gv * sg
        dg = (da * uv * (sg * (1.0 + gv * (1.0 - sg)))).astype(BF16)
        du = (da * s).astype(BF16)
        dg_ref[...] = dg
        du_ref[...] = du
        a_ref[...] = (s * uv).astype(BF16)
        acc_ref[...] += (jnp.dot(dg, wg_ref[...], preferred_element_type=F32)
                         + jnp.dot(du, wu_ref[...], preferred_element_type=F32))

        @pl.when(j == nj - 1)
        def _():
            dn_ref[...] = acc_ref[...]

    row = pl.BlockSpec((tm, d), lambda i, j: (i, 0))
    act = pl.BlockSpec((tm, tn), lambda i, j: (i, j))
    return pl.pallas_call(
        body, name=name, grid=(t // tm, nj),
        in_specs=[row, act, act] + [pl.BlockSpec((tn, d), lambda i, j: (j, 0))] * 3,
        out_specs=[row, act, act, act],
        out_shape=[jax.ShapeDtypeStruct((t, d), F32)] + [jax.ShapeDtypeStruct((t, D_FF), BF16)] * 3,
        scratch_shapes=[pltpu.VMEM((tm, d), F32)], compiler_params=_cparams("parallel", "arbitrary"),
    )(dh, g, u, wg, wu, wd)


def _ffn_forward(x, gain, weights, tag):
    n = _rms_fwd(x, gain, f"{tag}_norm")
    h, g, u = _ffn_fwd(n, x, *weights(n), f"{tag}_fwd")
    return h, (n, g, u)


def _ffn_backward(dh, x, gain, wg, wu, wd, saved, tag, push):
    n, g, u = saved
    dn, dg, du, a = _ffn_bwd_act(dh, g, u, wg, wu, wd, f"{tag}_bwd_act")
    kw = dict(mode="tn", out_dtype=BF16, tm=FF_TILE, tn=1024, tk=2048)
    d_wg = _matmul(dg, n, name=f"{tag}_dwg", **kw)
    d_wu = _matmul(du, n, name=f"{tag}_dwu", **kw)
    d_wd = _matmul(a, dh, scale=0.5, name=f"{tag}_dwd", **kw)
    token = push({f"{tag}_w_gate": d_wg, f"{tag}_w_up": d_wu, f"{tag}_w_down": d_wd})
    return _rms_bwd(x, gain + token, dn, dh, f"{tag}_norm_bwd")


CONV_COLS = 512
HALO = 8
CONV_STRIP = 32
CONV_ROWS = 1024


def _conv_fwd(p, w, b, name):
    t = p.shape[0]
    tm = min(CONV_ROWS, t)
    c0 = P_XBC // CONV_COLS

    def body(x_ref, halo_ref, w_ref, b_ref, o_ref, s_ref):
        i = pl.program_id(1)
        s_ref[0:HALO, :] = jnp.where(i > 0, halo_ref[...].astype(F32), 0.0)
        s_ref[HALO:HALO + tm, :] = x_ref[...].astype(F32)
        wv = w_ref[...]
        bv = b_ref[...]
        for r0 in range(0, tm, CONV_STRIP):
            acc = bv + wv[0:1, :] * s_ref[HALO - 3 + r0:HALO - 3 + r0 + CONV_STRIP, :]
            for k in range(1, SSD_CONV):
                acc = acc + wv[k:k + 1, :] * s_ref[HALO - 3 + k + r0:HALO - 3 + k + r0 + CONV_STRIP, :]
            o_ref[r0:r0 + CONV_STRIP, :] = (acc * _sigmoid(acc)).astype(o_ref.dtype)

    return pl.pallas_call(
        body, name=name, grid=(XBC // CONV_COLS, t // tm),
        in_specs=[pl.BlockSpec((tm, CONV_COLS), lambda j, i: (i, c0 + j)),
                  pl.BlockSpec((HALO, CONV_COLS), lambda j, i: (jnp.maximum(i * (tm // HALO) - 1, 0), c0 + j)),
                  pl.BlockSpec((SSD_CONV, CONV_COLS), lambda j, i: (0, j)), pl.BlockSpec((1, CONV_COLS), lambda j, i: (0, j))],
        out_specs=pl.BlockSpec((tm, CONV_COLS), lambda j, i: (i, j)),
        out_shape=jax.ShapeDtypeStruct((t, XBC), BF16),
        scratch_shapes=[pltpu.VMEM((tm + HALO, CONV_COLS), F32)], compiler_params=_cparams("parallel", "parallel"),
    )(p, p, w, b)


def _conv_bwd_act(p, dy, w, b, col0, name):
    t, cols = dy.shape
    tm = min(CONV_ROWS, t)
    c0 = (P_XBC + col0) // CONV_COLS
    w0 = col0 // CONV_COLS

    def body(x_ref, halo_ref, dy_ref, w_ref, b_ref, da_ref, dw_ref, db_ref, s_ref):
        i = pl.program_id(1)

        @pl.when(i == 0)
        def _():
            dw_ref[...] = jnp.zeros_like(dw_ref)
            db_ref[...] = jnp.zeros_like(db_ref)

        s_ref[0:HALO, :] = jnp.where(i > 0, halo_ref[...].astype(F32), 0.0)
        s_ref[HALO:HALO + tm, :] = x_ref[...].astype(F32)
        wv = w_ref[...]
        bv = b_ref[...]
        fold = lambda v: jnp.sum(v.reshape(CONV_STRIP // 8, 8, CONV_COLS), axis=0)
        sums = [jnp.zeros((8, CONV_COLS), F32) for _ in range(SSD_CONV + 1)]
        for r0 in range(0, tm, CONV_STRIP):
            taps = [s_ref[HALO - 3 + k + r0:HALO - 3 + k + r0 + CONV_STRIP, :] for k in range(SSD_CONV)]
            acc = bv + wv[0:1, :] * taps[0]
            for k in range(1, SSD_CONV):
                acc = acc + wv[k:k + 1, :] * taps[k]
            sg = _sigmoid(acc)
            dacc = dy_ref[r0:r0 + CONV_STRIP, :].astype(F32) * (sg * (1.0 + acc * (1.0 - sg)))
            da_ref[r0:r0 + CONV_STRIP, :] = dacc.astype(BF16)
            for k in range(SSD_CONV):
                sums[k] = sums[k] + fold(dacc * taps[k])
            sums[SSD_CONV] = sums[SSD_CONV] + fold(dacc)
        for k in range(SSD_CONV):
            dw_ref[k:k + 1, :] += jnp.sum(sums[k], axis=0, keepdims=True)
        db_ref[...] += jnp.sum(sums[SSD_CONV], axis=0, keepdims=True)

    return pl.pallas_call(
        body, name=name, grid=(cols // CONV_COLS, t // tm),
        in_specs=[pl.BlockSpec((tm, CONV_COLS), lambda j, i: (i, c0 + j)),
                  pl.BlockSpec((HALO, CONV_COLS), lambda j, i: (jnp.maximum(i * (tm // HALO) - 1, 0), c0 + j)),
                  pl.BlockSpec((tm, CONV_COLS), lambda j, i: (i, j)),
                  pl.BlockSpec((SSD_CONV, CONV_COLS), lambda j, i: (0, w0 + j)), pl.BlockSpec((1, CONV_COLS), lambda j, i: (0, w0 + j))],
        out_specs=[pl.BlockSpec((tm, CONV_COLS), lambda j, i: (i, j)), pl.BlockSpec((SSD_CONV, CONV_COLS), lambda j, i: (0, j)),
                   pl.BlockSpec((1, CONV_COLS), lambda j, i: (0, j))],
        out_shape=[jax.ShapeDtypeStruct((t, cols), BF16), jax.ShapeDtypeStruct((SSD_CONV, cols), F32), jax.ShapeDtypeStruct((1, cols), F32)],
        scratch_shapes=[pltpu.VMEM((tm + HALO, CONV_COLS), F32)], compiler_params=_cparams("parallel", "arbitrary"),
    )(p, p, dy, w, b)


def _conv_bwd_dx(dacc, w, col0, dp, name):
    t, cols = dacc.shape
    tm = min(CONV_ROWS, t)
    nt = t // tm
    w0 = col0 // CONV_COLS
    c0 = (P_XBC + col0) // CONV_COLS

    def body(d_ref, halo_ref, w_ref, dp_ref, o_ref, s_ref):
        del dp_ref
        i = pl.program_id(1)
        s_ref[0:tm, :] = d_ref[...].astype(F32)
        s_ref[tm:tm + HALO, :] = jnp.where(i < nt - 1, halo_ref[...].astype(F32), 0.0)
        wv = w_ref[...]
        for r0 in range(0, tm, CONV_STRIP):
            acc = wv[3:4, :] * s_ref[r0:r0 + CONV_STRIP, :]
            for k in range(SSD_CONV - 1):
                acc = acc + wv[k:k + 1, :] * s_ref[3 - k + r0:3 - k + r0 + CONV_STRIP, :]
            o_ref[r0:r0 + CONV_STRIP, :] = acc.astype(o_ref.dtype)

    return pl.pallas_call(
        body, name=name, grid=(cols // CONV_COLS, nt),
        in_specs=[pl.BlockSpec((tm, CONV_COLS), lambda j, i: (i, j)),
                  pl.BlockSpec((HALO, CONV_COLS), lambda j, i: (jnp.minimum((i + 1) * (tm // HALO), t // HALO - 1), j)),
                  pl.BlockSpec((SSD_CONV, CONV_COLS), lambda j, i: (0, w0 + j)), ANY],
        out_specs=pl.BlockSpec((tm, CONV_COLS), lambda j, i: (i, c0 + j)),
        out_shape=jax.ShapeDtypeStruct(dp.shape, dp.dtype), input_output_aliases={3: 0},
        scratch_shapes=[pltpu.VMEM((tm + HALO, CONV_COLS), F32)], compiler_params=_cparams("parallel", "parallel"),
    )(dacc, dacc, w, dp)


GROUP_COLS = SSD_INNER // SSD_GROUPS
PAIRS = GROUP_COLS // 128
HEADS_PER_GROUP = SSD_HEADS // SSD_GROUPS


def _dt_fwd(dt_raw, bias, name):
    t, n = dt_raw.shape
    tm = _row_tile(t)

    def body(x_ref, b_ref, o_ref):
        v = x_ref[...] + b_ref[...]
        o_ref[...] = jnp.maximum(v, 0.0) + jnp.log1p(jnp.exp(-jnp.abs(v)))

    row = pl.BlockSpec((tm, n), lambda i: (i, 0))
    vec = pl.BlockSpec((1, n), lambda i: (0, 0))
    return pl.pallas_call(body, name=name, grid=(t // tm,), in_specs=[row, vec], out_specs=row,
                          out_shape=jax.ShapeDtypeStruct((t, n), F32), compiler_params=_cparams("parallel"))(dt_raw, bias)


def _dt_bwd(ddt, dt_raw, bias, name):
    t, n = dt_raw.shape
    tm = _row_tile(t)

    def body(d_ref, x_ref, b_ref, o_ref, db_ref):
        @pl.when(pl.program_id(0) == 0)
        def _():
            db_ref[...] = jnp.zeros_like(db_ref)

        dr = d_ref[...] * _sigmoid(x_ref[...] + b_ref[...])
        o_ref[...] = dr.astype(o_ref.dtype)
        db_ref[...] += jnp.sum(dr, axis=0, keepdims=True)

    row = pl.BlockSpec((tm, n), lambda i: (i, 0))
    vec = pl.BlockSpec((1, n), lambda i: (0, 0))
    return pl.pallas_call(body, name=name, grid=(t // tm,), in_specs=[row, row, vec], out_specs=[row, vec],
                          out_shape=[jax.ShapeDtypeStruct((t, n), BF16), jax.ShapeDtypeStruct((1, n), F32)],
                          compiler_params=_cparams("arbitrary"))(ddt, dt_raw, bias)


SSD_STEP = 4


def _ssd_common(dt, dtt, a_log_w, a_log_t):
    l = CHUNK
    a = -jnp.exp(a_log_w)
    at = -jnp.exp(a_log_t)
    rowi = lax.broadcasted_iota(jnp.int32, (l, l), 0)
    coli = lax.broadcasted_iota(jnp.int32, (l, l), 1)
    tri = rowi >= coli
    lower = tri.astype(F32)
    upper = (rowi <= coli).astype(F32)
    acs = jnp.dot(lower, dt * a, precision=HI, preferred_element_type=F32)
    acst = jnp.dot(dtt * at, upper, precision=HI, preferred_element_type=F32)
    return a, acs, acst, tri, upper


def _pair_bc(w, lo, p):
    return jnp.where(lo, w[:, 2 * p:2 * p + 1], w[:, 2 * p + 1:2 * p + 2])


def _ssd_specs(t):
    rows = SSD_STEP * CHUNK
    assert t % rows == 0
    return t // rows, dict(
        xs=lambda cm: pl.BlockSpec((rows, GROUP_COLS), lambda g, c: (cm(c), g)),
        bm=lambda cm: pl.BlockSpec((rows, SSD_STATE), lambda g, c: (cm(c), SSD_INNER // SSD_STATE + g)),
        cmat=lambda cm: pl.BlockSpec((rows, SSD_STATE), lambda g, c: (cm(c), SSD_INNER // SSD_STATE + SSD_GROUPS + g)),
        dtw=lambda cm: pl.BlockSpec((1, rows, 128), lambda g, c: (g, cm(c), 0)),
        dtt=lambda cm: pl.BlockSpec((1, HEADS_PER_GROUP, rows), lambda g, c: (g, 0, cm(c))),
        wide=lambda cm: pl.BlockSpec((1, 1, 128), lambda g, c: (g, 0, 0)),
        tall=lambda cm: pl.BlockSpec((1, HEADS_PER_GROUP, 1), lambda g, c: (g, 0, 0)),
        grp=lambda cm: pl.BlockSpec((rows, GROUP_COLS), lambda g, c: (cm(c), g)),
        zp=lambda cm: pl.BlockSpec((rows, GROUP_COLS), lambda g, c: (cm(c), P_Z // GROUP_COLS + g)),
        vec=lambda cm: pl.BlockSpec((1, GROUP_COLS), lambda g, c: (0, g)),
        state=lambda cm: pl.BlockSpec((1, SSD_STEP, PAIRS, SSD_STATE, 128), lambda g, c: (g, cm(c), 0, 0, 0)),
    )


def _ssd_fwd(xc, p, hv, norm_g, name):
    t = xc.shape[0]
    nc, sp = _ssd_specs(t)
    ident = lambda c: c

    def body(xs_ref, b_ref, c_ref, dtw_ref, dtt_ref, aw_ref, at_ref, dk_ref, z_ref, ng_ref,
             y_ref, ys_ref, h_ref, st_ref):
        @pl.when(pl.program_id(1) == 0)
        def _():
            st_ref[...] = jnp.zeros_like(st_ref)

        lo = lax.broadcasted_iota(jnp.int32, (1, 128), 1) < 64
        dskip = dk_ref[0]
        for s in range(SSD_STEP):
            rows = slice(s * CHUNK, (s + 1) * CHUNK)
            dt = dtw_ref[0, rows, :]
            a, acs, acst, tri, _ = _ssd_common(dt, dtt_ref[0, :, rows], aw_ref[0], at_ref[0])
            ecs = jnp.exp(acs)
            alast = acs[CHUNK - 1:CHUNK, :]
            bmat, cmat = b_ref[rows, :], c_ref[rows, :]
            cb = lax.dot_general(cmat, bmat, _DIMS["nt"], preferred_element_type=F32)
            for pi in range(PAIRS):
                cols = slice(pi * 128, (pi + 1) * 128)
                x = xs_ref[rows, cols].astype(F32)
                xdt = x * _pair_bc(dt, lo, pi)
                ydiag = jnp.zeros((CHUNK, 128), F32)
                for r, mask in ((2 * pi, lo), (2 * pi + 1, jnp.logical_not(lo))):
                    lam = jnp.exp(jnp.where(tri, acs[:, r:r + 1] - acst[r:r + 1, :], -1e30))
                    m = (cb * lam).astype(BF16)
                    ydiag = ydiag + jnp.dot(m, jnp.where(mask, xdt, 0.0).astype(BF16), preferred_element_type=F32)
                ht = st_ref[pi]
                h_ref[0, s, pi] = ht
                yoff = jnp.dot(cmat, ht.astype(BF16), preferred_element_type=F32) * _pair_bc(ecs, lo, pi)
                y_ref[rows, cols] = (ydiag + yoff + _pair_bc(dskip, lo, pi) * x).astype(y_ref.dtype)
                alp = _pair_bc(alast, lo, pi)
                e = jnp.exp(alp - _pair_bc(acs, lo, pi))
                st = lax.dot_general(bmat, (xdt * e).astype(BF16), _DIMS["tn"], preferred_element_type=F32)
                st_ref[pi] = ht * jnp.exp(alp) + st
            zf = z_ref[rows, :].astype(F32)
            yg = y_ref[rows, :].astype(F32) * (zf * _sigmoid(zf))
            rstd = lax.rsqrt(jnp.mean(yg * yg, axis=-1, keepdims=True) + EPS)
            ys_ref[rows, :] = (yg * rstd * ng_ref[...]).astype(ys_ref.dtype)

    ins = ["xs", "bm", "cmat", "dtw", "dtt", "wide", "tall", "wide", "zp", "vec"]
    return pl.pallas_call(
        body, name=name, grid=(SSD_GROUPS, nc),
        in_specs=[sp[k](ident) for k in ins],
        out_specs=[sp["grp"](ident), sp["grp"](ident), sp["state"](ident)],
        out_shape=[jax.ShapeDtypeStruct((t, SSD_INNER), BF16), jax.ShapeDtypeStruct((t, SSD_INNER), BF16),
                   jax.ShapeDtypeStruct((SSD_GROUPS, t // CHUNK, PAIRS, SSD_STATE, 128), F32)],
        scratch_shapes=[pltpu.VMEM((PAIRS, SSD_STATE, 128), F32)], compiler_params=_cparams("parallel", "arbitrary"),
    )(xc, xc, xc, hv["dtw"], hv["dtt"], hv["alog_w"], hv["alog_t"], hv["dskip_w"], p, norm_g)


def _ssd_bwd(xc, p, hv, norm_g, y, dys, states, dp, name):
    t = xc.shape[0]
    nc, sp = _ssd_specs(t)
    rev = lambda c: nc - 1 - c

    def body(xs_ref, b_ref, c_ref, dtw_ref, dtt_ref, aw_ref, at_ref, dk_ref, z_ref, ng_ref,
             y_ref, dys_ref, h_ref, dp_ref,
             dxs_ref, db_ref, dc_ref, dz_ref, ddt_ref, hsum_ref, dng_ref, dst_ref):
        del dp_ref
        @pl.when(pl.program_id(1) == 0)
        def _():
            dst_ref[...] = jnp.zeros_like(dst_ref)
            hsum_ref[...] = jnp.zeros_like(hsum_ref)
            dng_ref[...] = jnp.zeros_like(dng_ref)

        lane = lax.broadcasted_iota(jnp.int32, (1, 128), 1)
        lo = lane < 64
        dskip = dk_ref[0]
        sel_r = lax.broadcasted_iota(jnp.int32, (128, 128), 0)
        sel_c = lax.broadcasted_iota(jnp.int32, (128, 128), 1)
        refs = (xs_ref, b_ref, c_ref, dtw_ref, dtt_ref, aw_ref, at_ref, dk_ref, z_ref, ng_ref, y_ref, dys_ref, h_ref,
                dxs_ref, db_ref, dc_ref, dz_ref, ddt_ref, hsum_ref, dng_ref, dst_ref)
        for s in reversed(range(SSD_STEP)):
            chunk_bwd(refs, slice(s * CHUNK, (s + 1) * CHUNK), s, lane, lo, dskip, sel_r, sel_c)

    def chunk_bwd(refs, rows, s, lane, lo, dskip, sel_r, sel_c):
        (xs_ref, b_ref, c_ref, dtw_ref, dtt_ref, aw_ref, at_ref, dk_ref, z_ref, ng_ref, y_ref, dys_ref, h_ref,
         dxs_ref, db_ref, dc_ref, dz_ref, ddt_ref, hsum_ref, dng_ref, dst_ref) = refs
        dt = dtw_ref[0, rows, :]
        a, acs, acst, tri, upper = _ssd_common(dt, dtt_ref[0, :, rows], aw_ref[0], at_ref[0])
        ecs = jnp.exp(acs)
        alast = acs[CHUNK - 1:CHUNK, :]
        bmat, cmat = b_ref[rows, :], c_ref[rows, :]
        cb = lax.dot_general(cmat, bmat, _DIMS["nt"], preferred_element_type=F32)

        zf = z_ref[rows, :].astype(F32)
        sg = _sigmoid(zf)
        sz = zf * sg
        yv = y_ref[rows, :].astype(F32)
        yg = yv * sz
        rstd = lax.rsqrt(jnp.mean(yg * yg, axis=-1, keepdims=True) + EPS)
        yhat = yg * rstd
        dysv = dys_ref[rows, :].astype(F32)
        dng_ref[...] += jnp.sum(dysv * yhat, axis=0, keepdims=True)
        dyh = dysv * ng_ref[...]
        dyg = rstd * (dyh - yhat * jnp.mean(dyh * yhat, axis=-1, keepdims=True))
        dz_ref[rows, :] = (dyg * yv * (sg * (1.0 + zf * (1.0 - sg)))).astype(dz_ref.dtype)
        dy_all = dyg * sz

        dal = jnp.zeros((CHUNK, 128), F32)
        ddtm = jnp.zeros((CHUNK, 128), F32)
        dalast = jnp.zeros((8, 128), F32)
        ddsk = jnp.zeros((8, 128), F32)
        dcb = jnp.zeros((CHUNK, CHUNK), F32)
        qcol = jnp.zeros((8, CHUNK), F32)
        sub8 = lax.broadcasted_iota(jnp.int32, (8, CHUNK), 0)
        dc_acc = jnp.zeros((CHUNK, SSD_STATE), F32)
        db_acc = jnp.zeros((CHUNK, SSD_STATE), F32)
        for pi in range(PAIRS):
            sel = (sel_c == 2 * pi + (sel_r >= 64).astype(jnp.int32)).astype(BF16)

            def hsum(v, sel=sel):
                return jnp.dot(v.astype(BF16), sel, preferred_element_type=F32)

            dyp = dy_all[:, pi * 128:(pi + 1) * 128]
            x = xs_ref[rows, pi * 128:(pi + 1) * 128].astype(F32)
            dtp = _pair_bc(dt, lo, pi)
            xdt = x * dtp
            dxdt = jnp.zeros((CHUNK, 128), F32)
            for r, mask in ((2 * pi, lo), (2 * pi + 1, jnp.logical_not(lo))):
                lam = jnp.exp(jnp.where(tri, acs[:, r:r + 1] - acst[r:r + 1, :], -1e30))
                m32 = cb * lam
                m = m32.astype(BF16)
                dyr = jnp.where(mask, dyp, 0.0).astype(BF16)
                xr = jnp.where(mask, xdt, 0.0).astype(BF16)
                dm = lax.dot_general(dyr, xr, _DIMS["nt"], preferred_element_type=F32)
                dcb = dcb + dm * lam
                q = dm * m32
                dal = dal + jnp.sum(q, axis=1, keepdims=True) * (lane == r).astype(F32)
                qcol = qcol + jnp.where(sub8 == r, jnp.sum(q, axis=0, keepdims=True), 0.0)
                dxdt = dxdt + lax.dot_general(m, dyr, _DIMS["tn"], preferred_element_type=F32)
            ht = h_ref[0, s, pi]
            htb = ht.astype(BF16)
            ecp = _pair_bc(ecs, lo, pi)
            yoff = jnp.dot(cmat, htb, preferred_element_type=F32) * ecp
            dg = (dyp * ecp).astype(BF16)
            dc_acc = dc_acc + lax.dot_general(dg, htb, _DIMS["nt"], preferred_element_type=F32)
            dht = lax.dot_general(cmat, dg, _DIMS["tn"], preferred_element_type=F32)
            dal = dal + hsum(dyp * yoff)
            dhn = dst_ref[pi]
            dhnb = dhn.astype(BF16)
            alp = _pair_bc(alast, lo, pi)
            e = jnp.exp(alp - _pair_bc(acs, lo, pi))
            xe = xdt * e
            db_acc = db_acc + lax.dot_general(xe.astype(BF16), dhnb, _DIMS["nt"], preferred_element_type=F32)
            dxe = jnp.dot(bmat, dhnb, preferred_element_type=F32)
            dxdt = dxdt + dxe * e
            tt = hsum(dxe * xe)
            dal = dal - tt
            dec = jnp.exp(alp)
            dalast = dalast + jnp.sum(tt, axis=0, keepdims=True) + hsum(
                jnp.broadcast_to(jnp.sum(dhn * ht, axis=0, keepdims=True) * dec, (8, 128)))
            dst_ref[pi] = dht + dhn * dec
            dxs_ref[rows, pi * 128:(pi + 1) * 128] = (_pair_bc(dskip, lo, pi) * dyp + dxdt * dtp).astype(dxs_ref.dtype)
            ddtm = ddtm + hsum(dxdt * x)
            ddsk = ddsk + hsum(jnp.broadcast_to(jnp.sum(dyp * x, axis=0, keepdims=True), (8, 128)))
        rowi = lax.broadcasted_iota(jnp.int32, (CHUNK, 128), 0)
        qcol_w = lax.dot_general(jnp.concatenate([qcol, jnp.zeros((CHUNK - 8, CHUNK), F32)], axis=0), (sel_r == sel_c).astype(F32),
                                 _DIMS["tn"], precision=HI, preferred_element_type=F32)
        dal = dal - qcol_w + jnp.where(rowi == CHUNK - 1, dalast[0:1, :], 0.0)
        dda = jnp.dot(upper, dal, precision=HI, preferred_element_type=F32)
        ddt_ref[0, rows, :] = ddtm + dda * a
        hsum_ref[0, 1:2, :] += jnp.sum(dda * dt, axis=0, keepdims=True) * a
        hsum_ref[0, 2:3, :] += ddsk[0:1, :]
        dcbb = dcb.astype(BF16)
        dc_ref[rows, :] = (jnp.dot(dcbb, bmat, preferred_element_type=F32) + dc_acc).astype(dc_ref.dtype)
        db_ref[rows, :] = (lax.dot_general(dcbb, cmat, _DIMS["tn"], preferred_element_type=F32) + db_acc).astype(db_ref.dtype)

    ins = ["xs", "bm", "cmat", "dtw", "dtt", "wide", "tall", "wide", "zp", "vec", "grp", "grp", "state"]
    col = lambda: pl.BlockSpec((SSD_STEP * CHUNK, SSD_STATE), lambda g, c: (rev(c), g))
    return pl.pallas_call(
        body, name=name, grid=(SSD_GROUPS, nc),
        in_specs=[sp[k](rev) for k in ins] + [ANY],
        out_specs=[sp["grp"](rev), col(), col(), sp["zp"](rev), sp["dtw"](rev),
                   pl.BlockSpec((1, 8, 128), lambda g, c: (g, 0, 0)), sp["vec"](rev)],
        out_shape=[jax.ShapeDtypeStruct((t, SSD_INNER), BF16), jax.ShapeDtypeStruct((t, SSD_GROUPS * SSD_STATE), BF16),
                   jax.ShapeDtypeStruct((t, SSD_GROUPS * SSD_STATE), BF16), jax.ShapeDtypeStruct(dp.shape, dp.dtype),
                   jax.ShapeDtypeStruct((SSD_GROUPS, t, 128), F32), jax.ShapeDtypeStruct((SSD_GROUPS, 8, 128), F32),
                   jax.ShapeDtypeStruct((1, SSD_INNER), F32)],
        input_output_aliases={len(ins): 3},
        scratch_shapes=[pltpu.VMEM((PAIRS, SSD_STATE, 128), F32)], compiler_params=_cparams("parallel", "arbitrary"),
    )(xc, xc, xc, hv["dtw"], hv["dtt"], hv["alog_w"], hv["alog_t"], hv["dskip_w"], p, norm_g, y, dys, states, dp)


def _wide(v):
    return jnp.pad(v.reshape(SSD_GROUPS, 1, HEADS_PER_GROUP), ((0, 0), (0, 0), (0, 128 - HEADS_PER_GROUP)))


def _head_inputs(dt, a_log, d_skip):
    t = dt.shape[0]
    g = dt[:, :SSD_HEADS].reshape(t, SSD_GROUPS, HEADS_PER_GROUP)
    return dict(
        dtw=jnp.pad(jnp.transpose(g, (1, 0, 2)), ((0, 0), (0, 0), (0, 128 - HEADS_PER_GROUP))),
        dtt=jnp.transpose(g, (1, 2, 0)),
        alog_w=_wide(a_log), alog_t=a_log.reshape(SSD_GROUPS, HEADS_PER_GROUP, 1),
        dskip_w=_wide(d_skip),
    )


def _gelu(x):
    return 0.5 * x * (1.0 + lax.erf(x * (1.0 / math.sqrt(2.0))))


def _gelu_grad(x):
    return 0.5 * (1.0 + lax.erf(x * (1.0 / math.sqrt(2.0)))) + x * jnp.exp(-0.5 * x * x) * (1.0 / math.sqrt(2.0 * math.pi))


def _tril_mask():
    r = lax.broadcasted_iota(jnp.int32, (CHUNK, CHUNK), 0)
    c = lax.broadcasted_iota(jnp.int32, (CHUNK, CHUNK), 1)
    return r >= c


def _gmlp_fwd(p, v_gain, w_s, b_col, name):
    t = p.shape[0]
    tm = _row_tile(t)
    u0 = P_UV // GMLP_W

    def body(u_ref, v_ref, gn_ref, ws_ref, bs_ref, o_ref):
        v = _gelu(v_ref[...].astype(F32))
        v = (v * lax.rsqrt(jnp.mean(v * v, axis=-1, keepdims=True) + EPS) * gn_ref[...]).astype(BF16)
        tril = _tril_mask()
        wm = [jnp.where(tril, ws_ref[g], 0.0).astype(BF16) for g in range(GMLP_GROUPS)]
        for k in range(tm // CHUNK):
            rows = slice(k * CHUNK, (k + 1) * CHUNK)
            for g in range(GMLP_GROUPS):
                cols = slice(g * 128, (g + 1) * 128)
                mixed = jnp.dot(wm[g], v[rows, cols], preferred_element_type=F32) + bs_ref[g]
                o_ref[rows, cols] = (_gelu(u_ref[rows, cols].astype(F32)) * mixed).astype(o_ref.dtype)

    return pl.pallas_call(
        body, name=name, grid=(t // tm,),
        in_specs=[pl.BlockSpec((tm, GMLP_W), lambda i: (i, u0)), pl.BlockSpec((tm, GMLP_W), lambda i: (i, u0 + 1)),
                  pl.BlockSpec((1, GMLP_W), lambda i: (0, 0)), pl.BlockSpec((GMLP_GROUPS, CHUNK, CHUNK), lambda i: (0, 0, 0)),
                  pl.BlockSpec((GMLP_GROUPS, CHUNK, 1), lambda i: (0, 0, 0))],
        out_specs=pl.BlockSpec((tm, GMLP_W), lambda i: (i, 0)),
        out_shape=jax.ShapeDtypeStruct((t, GMLP_W), BF16), compiler_params=_cparams("parallel"),
    )(p, p, v_gain, w_s, b_col)


def _gmlp_bwd(p, dy, v_gain, w_s, b_col, dp, name):
    t = p.shape[0]
    tm = _row_tile(t)
    u0 = P_UV // GMLP_W

    def body(u_ref, v_ref, dy_ref, gn_ref, ws_ref, bs_ref, dp_ref, duv_ref, dws_ref, dbs_ref, dgn_ref, dvn_ref):
        del dp_ref

        @pl.when(pl.program_id(0) == 0)
        def _():
            dws_ref[...] = jnp.zeros_like(dws_ref)
            dbs_ref[...] = jnp.zeros_like(dbs_ref)
            dgn_ref[...] = jnp.zeros_like(dgn_ref)

        vraw = v_ref[...].astype(F32)
        va = _gelu(vraw)
        rstd = lax.rsqrt(jnp.mean(va * va, axis=-1, keepdims=True) + EPS)
        vhat = va * rstd
        gain = gn_ref[...]
        vn = (vhat * gain).astype(BF16)
        tril = _tril_mask()
        wm = [jnp.where(tril, ws_ref[g], 0.0).astype(BF16) for g in range(GMLP_GROUPS)]
        for k in range(tm // CHUNK):
            rows = slice(k * CHUNK, (k + 1) * CHUNK)
            for g in range(GMLP_GROUPS):
                cols = slice(g * 128, (g + 1) * 128)
                uraw = u_ref[rows, cols].astype(F32)
                vb = vn[rows, cols]
                mixed = jnp.dot(wm[g], vb, preferred_element_type=F32) + bs_ref[g]
                dyb = dy_ref[rows, cols].astype(F32)
                duv_ref[rows, cols] = (dyb * mixed * _gelu_grad(uraw)).astype(duv_ref.dtype)
                dmix = dyb * _gelu(uraw)
                dmb = dmix.astype(BF16)
                dws_ref[g] += jnp.where(tril, lax.dot_general(dmb, vb, _DIMS["nt"], preferred_element_type=F32), 0.0)
                dbs_ref[g] += jnp.sum(dmix, axis=1, keepdims=True)
                dvn_ref[rows, cols] = lax.dot_general(wm[g], dmb, _DIMS["tn"], preferred_element_type=F32)
        dvn = dvn_ref[...]
        dgn_ref[...] += jnp.sum(dvn * vhat, axis=0, keepdims=True)
        dvh = dvn * gain
        dva = rstd * (dvh - vhat * jnp.mean(dvh * vhat, axis=-1, keepdims=True))
        duv_ref[:, GMLP_W:2 * GMLP_W] = (dva * _gelu_grad(vraw)).astype(duv_ref.dtype)

    return pl.pallas_call(
        body, name=name, grid=(t // tm,),
        in_specs=[pl.BlockSpec((tm, GMLP_W), lambda i: (i, u0)), pl.BlockSpec((tm, GMLP_W), lambda i: (i, u0 + 1)),
                  pl.BlockSpec((tm, GMLP_W), lambda i: (i, 0)),
                  pl.BlockSpec((1, GMLP_W), lambda i: (0, 0)), pl.BlockSpec((GMLP_GROUPS, CHUNK, CHUNK), lambda i: (0, 0, 0)),
                  pl.BlockSpec((GMLP_GROUPS, CHUNK, 1), lambda i: (0, 0, 0)), ANY],
        out_specs=[pl.BlockSpec((tm, 2 * GMLP_W), lambda i: (i, P_UV // (2 * GMLP_W))),
                   pl.BlockSpec((GMLP_GROUPS, CHUNK, CHUNK), lambda i: (0, 0, 0)),
                   pl.BlockSpec((GMLP_GROUPS, CHUNK, 1), lambda i: (0, 0, 0)), pl.BlockSpec((1, GMLP_W), lambda i: (0, 0))],
        out_shape=[jax.ShapeDtypeStruct(dp.shape, dp.dtype), jax.ShapeDtypeStruct((GMLP_GROUPS, CHUNK, CHUNK), F32),
                   jax.ShapeDtypeStruct((GMLP_GROUPS, CHUNK, 1), F32), jax.ShapeDtypeStruct((1, GMLP_W), F32)],
        input_output_aliases={6: 0},
        scratch_shapes=[pltpu.VMEM((tm, GMLP_W), F32)], compiler_params=_cparams("arbitrary"),
    )(p, p, dy, v_gain, w_s, b_col, dp)


def _head_masks():
    lane = lax.broadcasted_iota(jnp.int32, (1, MEM_W), 1)
    return [(lane >= h * 64) & (lane < (h + 1) * 64) for h in range(MEM_HEADS)]


def _mem_fwd(p, kv, name):
    t = p.shape[0]
    tm = _row_tile(t)
    q0 = P_Q // MEM_W

    def body(q_ref, kv_ref, o_ref):
        q = q_ref[...]
        k = kv_ref[:, 0:MEM_W].astype(BF16)
        v = kv_ref[:, MEM_W:2 * MEM_W].astype(BF16)
        out = jnp.zeros((tm, MEM_W), F32)
        for mask in _head_masks():
            s = lax.dot_general(jnp.where(mask, q, 0), k, _DIMS["nt"], preferred_element_type=F32) * 0.125
            e = jnp.exp(s - jnp.max(s, axis=-1, keepdims=True))
            pr = (e * (1.0 / jnp.sum(e, axis=-1, keepdims=True))).astype(BF16)
            out = out + jnp.where(mask, jnp.dot(pr, v, preferred_element_type=F32), 0.0)
        o_ref[...] = out.astype(o_ref.dtype)

    return pl.pallas_call(
        body, name=name, grid=(t // tm,),
        in_specs=[pl.BlockSpec((tm, MEM_W), lambda i: (i, q0)), pl.BlockSpec((MEM_LEN, 2 * MEM_W), lambda i: (0, 0))],
        out_specs=pl.BlockSpec((tm, MEM_W), lambda i: (i, 0)),
        out_shape=jax.ShapeDtypeStruct((t, MEM_W), BF16), compiler_params=_cparams("parallel"),
    )(p, kv)


def _mem_bwd(p, kv, dy, dp, name):
    t = p.shape[0]
    tm = _row_tile(t)
    q0 = P_Q // MEM_W
    assert P_W - P_Q == 2 * MEM_W

    def body(q_ref, kv_ref, dy_ref, dp_ref, dq_ref, dkv_ref):
        del dp_ref

        @pl.when(pl.program_id(0) == 0)
        def _():
            dkv_ref[...] = jnp.zeros_like(dkv_ref)

        q = q_ref[...]
        dy = dy_ref[...]
        k = kv_ref[:, 0:MEM_W].astype(BF16)
        v = kv_ref[:, MEM_W:2 * MEM_W].astype(BF16)
        dq = jnp.zeros((tm, MEM_W), F32)
        dk = jnp.zeros((MEM_LEN, MEM_W), F32)
        dv = jnp.zeros((MEM_LEN, MEM_W), F32)
        for mask in _head_masks():
            qh = jnp.where(mask, q, 0)
            dyh = jnp.where(mask, dy, 0)
            s = lax.dot_general(qh, k, _DIMS["nt"], preferred_element_type=F32) * 0.125
            e = jnp.exp(s - jnp.max(s, axis=-1, keepdims=True))
            pr = e * (1.0 / jnp.sum(e, axis=-1, keepdims=True))
            prb = pr.astype(BF16)
            dp = lax.dot_general(dyh, v, _DIMS["nt"], preferred_element_type=F32)
            ds = (pr * (dp - jnp.sum(dp * pr, axis=-1, keepdims=True)) * 0.125).astype(BF16)
            dq = dq + jnp.where(mask, jnp.dot(ds, k, preferred_element_type=F32), 0.0)
            dk = dk + lax.dot_general(ds, qh, _DIMS["tn"], preferred_element_type=F32)
            dv = dv + lax.dot_general(prb, dyh, _DIMS["tn"], preferred_element_type=F32)
        dq_ref[:, 0:MEM_W] = dq.astype(dq_ref.dtype)
        dq_ref[:, MEM_W:2 * MEM_W] = jnp.zeros((tm, MEM_W), dq_ref.dtype)
        dkv_ref[:, 0:MEM_W] += dk
        dkv_ref[:, MEM_W:2 * MEM_W] += dv

    return pl.pallas_call(
        body, name=name, grid=(t // tm,),
        in_specs=[pl.BlockSpec((tm, MEM_W), lambda i: (i, q0)), pl.BlockSpec((MEM_LEN, 2 * MEM_W), lambda i: (0, 0)),
                  pl.BlockSpec((tm, MEM_W), lambda i: (i, 0)), ANY],
        out_specs=[pl.BlockSpec((tm, 2 * MEM_W), lambda i: (i, P_Q // (2 * MEM_W))),
                   pl.BlockSpec((MEM_LEN, 2 * MEM_W), lambda i: (0, 0))],
        out_shape=[jax.ShapeDtypeStruct(dp.shape, dp.dtype), jax.ShapeDtypeStruct((MEM_LEN, 2 * MEM_W), F32)],
        input_output_aliases={3: 0}, compiler_params=_cparams("arbitrary"),
    )(p, kv, dy, dp)


def _merge_fwd(p, b_ssd, b_gmlp, b_mem, name):
    t = p.shape[0]
    tm = _row_tile(t)
    g0 = P_GL // D_MODEL

    def body(g1, g2, g3, b1, b2, b3, o_ref):
        def strip(rows, carry):
            acc = _sigmoid(g1[rows, :].astype(F32)) * b1[rows, :].astype(F32)
            acc = acc + _sigmoid(g2[rows, :].astype(F32)) * b2[rows, :].astype(F32)
            acc = acc + _sigmoid(g3[rows, :].astype(F32)) * b3[rows, :].astype(F32)
            o_ref[rows, :] = acc.astype(o_ref.dtype)
            return carry

        _strips(tm, strip, 0)

    row = pl.BlockSpec((tm, D_MODEL), lambda i: (i, 0))
    return pl.pallas_call(
        body, name=name, grid=(t // tm,),
        in_specs=[pl.BlockSpec((tm, D_MODEL), lambda i, k=k: (i, g0 + k)) for k in range(3)] + [row] * 3,
        out_specs=row, out_shape=jax.ShapeDtypeStruct((t, D_MODEL), BF16), compiler_params=_cparams("parallel"),
    )(p, p, p, b_ssd, b_gmlp, b_mem)


def _merge_bwd(p, dm, b_ssd, b_gmlp, b_mem, dp, name):
    t = p.shape[0]
    tm = _row_tile(t)
    g0 = P_GL // D_MODEL

    def body(g1, g2, g3, dm_ref, b1, b2, b3, dp_ref, d1, d2, d3, dgl_ref):
        del dp_ref

        def strip(rows, carry):
            dmv = dm_ref[rows, :].astype(F32)
            for k, (g_ref, b_ref, d_ref) in enumerate(((g1, b1, d1), (g2, b2, d2), (g3, b3, d3))):
                sg = _sigmoid(g_ref[rows, :].astype(F32))
                d_ref[rows, :] = (dmv * sg).astype(d_ref.dtype)
                dgl_ref[rows, k * D_MODEL:(k + 1) * D_MODEL] = (dmv * b_ref[rows, :].astype(F32) * sg * (1.0 - sg)).astype(dgl_ref.dtype)
            return carry

        _strips(tm, strip, 0)

    row = pl.BlockSpec((tm, D_MODEL), lambda i: (i, 0))
    return pl.pallas_call(
        body, name=name, grid=(t // tm,),
        in_specs=[pl.BlockSpec((tm, D_MODEL), lambda i, k=k: (i, g0 + k)) for k in range(3)] + [row] * 4 + [ANY],
        out_specs=[row, row, row, pl.BlockSpec((tm, 3 * D_MODEL), lambda i: (i, P_GL // (3 * D_MODEL)))],
        out_shape=[jax.ShapeDtypeStruct((t, D_MODEL), BF16)] * 3 + [jax.ShapeDtypeStruct(dp.shape, dp.dtype)],
        input_output_aliases={7: 3}, compiler_params=_cparams("parallel"),
    )(p, p, p, dm, b_ssd, b_gmlp, b_mem, dp)


def _local_step(x, mem, target, w, first_weights, rest_weights, push):
    t = x.shape[0]
    mm = functools.partial(_matmul, tk=1024)

    ffn_weights = lambda tag: (w[f"{tag}_w_gate"], w[f"{tag}_w_up"], w[f"{tag}_w_down"])

    def ffn1_weights(n):
        w.update(first_weights(n))
        return ffn_weights("ffn1")

    w = dict(w)
    h1, ffn1_saved = _ffn_forward(x, w["ffn1_norm"], ffn1_weights, "ffn1")
    w.update(rest_weights(h1))
    n2 = _rms_fwd(h1, w["mix_norm"], "mix_norm")
    p = mm(n2, w["w_in_p"], mode="nt", out_dtype=BF16, tm=1024, tn=1536, name="in_proj")
    dt_raw = mm(n2, w["w_dt"], mode="nt", out_dtype=F32, tm=2048, tn=128, name="dt_proj")
    dt_bias = jnp.pad(w["ssd_dt_bias"], (0, 128 - SSD_HEADS)).reshape(1, 128)
    hv = _head_inputs(_dt_fwd(dt_raw, dt_bias, "dt_fwd"), w["ssd_a_log"], w["ssd_d"])
    xc = _conv_fwd(p, w["ssd_conv_w"], w["ssd_conv_b"], "conv_fwd")
    y_ssd_raw, y_ssd, states = _ssd_fwd(xc, p, hv, w["ssd_norm"], "ssd_fwd")
    b_col = w["gmlp_b_s"].reshape(GMLP_GROUPS, CHUNK, 1)
    y_gmlp = _gmlp_fwd(p, w["gmlp_v_norm"], w["gmlp_w_s"], b_col, "gmlp_fwd")
    mem_n = _rms_fwd(mem, w["mem_norm"], "mem_norm")
    kv = mm(mem_n, w["w_mem_kv"], mode="nn", out_dtype=F32, tm=256, tn=512, name="mem_kv")
    y_mem = _mem_fwd(p, kv, "mem_fwd")
    b_ssd = mm(y_ssd, w["w_branch_ssd"], mode="nn", out_dtype=BF16, tm=1024, tn=1024, tk=2048, name="branch_ssd")
    b_gmlp = mm(y_gmlp, w["w_branch_gmlp"], mode="nn", out_dtype=BF16, tm=1024, tn=1024, name="branch_gmlp")
    b_mem = mm(y_mem, w["w_branch_mem"], mode="nt", out_dtype=BF16, tm=2048, tn=1024, tk=MEM_W, name="branch_mem")
    merged = _merge_fwd(p, b_ssd, b_gmlp, b_mem, "merge_fwd")
    h2 = mm(merged, w["w_out"], mode="nn", out_dtype=F32, tm=1024, tn=1024, addend=h1, name="out_proj")
    h3, ffn2_saved = _ffn_forward(h2, w["ffn2_norm"], lambda n: ffn_weights("ffn2"), "ffn2")
    dh3, d_final, loss = _loss_head(h3, w["final_norm"], target, "loss_head")

    g = {"final_norm": d_final}
    big = {}
    dh2, g["ffn2_norm"] = _ffn_backward(dh3, h2, w["ffn2_norm"], w["ffn2_w_gate"], w["ffn2_w_up"], w["ffn2_w_down"], ffn2_saved,
                                        "ffn2", functools.partial(push, 0))
    dmerged = mm(dh2, w["w_out"], mode="nt", out_dtype=BF16, tm=1024, tn=1024, name="out_proj_dx")
    big["w_out"] = mm(merged, dh2, mode="tn", out_dtype=BF16, tm=1024, tn=1024, tk=2048, name="out_proj_dw")
    db_ssd, db_gmlp, db_mem, dp = _merge_bwd(p, dmerged, b_ssd, b_gmlp, b_mem, lax.empty((t, P_W), BF16), "merge_bwd")
    dy_ssd = mm(db_ssd, w["w_branch_ssd"], mode="nt", out_dtype=BF16, tm=1024, tn=2048, name="branch_ssd_dx")
    dy_gmlp = mm(db_gmlp, w["w_branch_gmlp"], mode="nt", out_dtype=BF16, tm=1024, tn=1024, name="branch_gmlp_dx")
    dy_mem = mm(db_mem, w["w_branch_mem"], mode="nn", out_dtype=BF16, tm=2048, tn=256, name="branch_mem_dx")
    big["w_branch_ssd"] = mm(y_ssd, db_ssd, mode="tn", out_dtype=BF16, tm=1024, tn=1024, tk=2048, name="branch_ssd_dw")
    big["w_branch_gmlp"] = mm(y_gmlp, db_gmlp, mode="tn", out_dtype=BF16, tm=1024, tn=1024, tk=2048, name="branch_gmlp_dw")
    big["w_branch_mem"] = mm(db_mem, y_mem, mode="tn", out_dtype=BF16, tm=1024, tn=256, tk=2048, name="branch_mem_dw")
    dp, dkv = _mem_bwd(p, kv, dy_mem, dp, "mem_bwd")
    big["w_mem_kv"] = mm(mem_n, dkv, mode="tn", out_dtype=BF16, tm=1024, tn=512, tk=256, name="mem_kv_dw")
    dmem_n = mm(dkv, w["w_mem_kv"], mode="nt", out_dtype=F32, tm=256, tn=1024, tk=512, name="mem_kv_dx")
    _, g["mem_norm"] = _rms_bwd(mem, w["mem_norm"], dmem_n, None, "mem_norm_bwd")
    dp, d_ws, d_bs, g["gmlp_v_norm"] = _gmlp_bwd(p, dy_gmlp, w["gmlp_v_norm"], w["gmlp_w_s"], b_col, dp, "gmlp_bwd")
    g["gmlp_w_s"] = d_ws
    g["gmlp_b_s"] = d_bs.reshape(GMLP_GROUPS, CHUNK)
    dxs, d_bm, d_cm, dp, ddt_w, hsums, g["ssd_norm"] = _ssd_bwd(xc, p, hv, w["ssd_norm"], y_ssd_raw, dy_ssd, states, dp, "ssd_bwd")
    heads = hsums[:, :, :HEADS_PER_GROUP]
    g["ssd_a_log"] = heads[:, 1, :].reshape(1, SSD_HEADS)
    g["ssd_d"] = heads[:, 2, :].reshape(1, SSD_HEADS)
    ddt = jnp.transpose(ddt_w[:, :, :HEADS_PER_GROUP], (1, 0, 2)).reshape(t, SSD_HEADS)
    ddt, d_bias = _dt_bwd(jnp.pad(ddt, ((0, 0), (0, 128 - SSD_HEADS))), dt_raw, dt_bias, "dt_bwd")
    g["ssd_dt_bias"] = d_bias[:, :SSD_HEADS]
    dws, dbs = [], []
    for dyc, col0, tag in ((dxs, 0, "x"), (d_bm, SSD_INNER, "b"), (d_cm, SSD_INNER + SSD_GROUPS * SSD_STATE, "c")):
        dacc, dw_c, db_c = _conv_bwd_act(p, dyc, w["ssd_conv_w"], w["ssd_conv_b"], col0, f"conv_bwd_act_{tag}")
        dp = _conv_bwd_dx(dacc, w["ssd_conv_w"], col0, dp, f"conv_bwd_dx_{tag}")
        dws.append(dw_c)
        dbs.append(db_c)
    g["ssd_conv_w"] = jnp.concatenate(dws, axis=1)
    g["ssd_conv_b"] = jnp.concatenate(dbs, axis=1)
    d_win_p = _matmul(dp, n2, mode="tn", out_dtype=BF16, tm=1536, tn=1024, tk=2048, name="in_proj_dw")
    d_wdt = mm(ddt, n2, mode="tn", out_dtype=BF16, tm=128, tn=1024, tk=2048, name="dt_proj_dw")
    sl = lambda a, o, n: a[o:o + n]
    big["w_in"] = jnp.concatenate([sl(d_win_p, P_Z, 2048), sl(d_win_p, P_XBC, XBC), d_wdt[:SSD_HEADS], sl(d_win_p, P_UV, 2048),
                                   sl(d_win_p, P_Q, MEM_W), sl(d_win_p, P_GL, 3 * D_MODEL)], axis=0)
    token = push(1, big)
    dn2 = mm(dp, w["w_in_p"], mode="nn", out_dtype=F32, tm=1024, tn=1024, tk=3584, name="in_proj_dx")
    dn2 = _matmul(ddt, w["w_dt"], mode="nn", out_dtype=F32, tm=1024, tn=1024, tk=128, addend=dn2, name="dt_proj_dx")
    dh1, g["mix_norm"] = _rms_bwd(h1, w["mix_norm"] + token, dn2, dh2, "mix_norm_bwd")
    dx, g["ffn1_norm"] = _ffn_backward(dh1, x, w["ffn1_norm"], w["ffn1_w_gate"], w["ffn1_w_up"], w["ffn1_w_down"], ffn1_saved,
                                       "ffn1", functools.partial(push, 2))
    return loss, dx, g


def _split_w_in(w_in_t):
    sl = lambda o, n: w_in_t[o:o + n]
    w_p = jnp.concatenate([sl(IN_GL, 3 * D_MODEL), sl(IN_Z, 2048), sl(IN_XBC, XBC), sl(IN_UV, 2048), sl(IN_Q, MEM_W),
                           jnp.zeros((P_W - P_USED, D_MODEL), w_in_t.dtype)], axis=0)
    w_dt = jnp.pad(sl(IN_DT, SSD_HEADS), ((0, 128 - SSD_HEADS), (0, 0)))
    return w_p, w_dt


def _pick_tile(rows, cap=512):
    best = None
    for tile in range(8, min(rows, cap) + 1, 8):
        if rows % tile == 0:
            best = tile
    return best if best is not None else rows


def _adamw(w, g, m, v, name):
    rows, lanes = w.shape
    tile = _pick_tile(rows, cap=max(8, (512 * 1024 // lanes) // 8 * 8))
    c1 = 1.0 / (1.0 - ADAM_B1 ** ADAM_STEP)
    c2 = 1.0 / (1.0 - ADAM_B2 ** ADAM_STEP)

    def body(w_ref, g_ref, m_ref, v_ref, d_ref, nm_ref, nv_ref):
        gv = g_ref[...]
        nm = ADAM_B1 * m_ref[...] + (1.0 - ADAM_B1) * gv
        nv = ADAM_B2 * v_ref[...] + (1.0 - ADAM_B2) * (gv * gv)
        nm_ref[...] = nm
        nv_ref[...] = nv
        d_ref[...] = -ADAM_LR * ((nm * c1) / (jnp.sqrt(nv * c2) + ADAM_EPS) + ADAM_WD * w_ref[...])

    blk = pl.BlockSpec((tile, lanes), lambda i: (i, 0))
    return pl.pallas_call(
        body, name=name, grid=(rows // tile,), in_specs=[blk] * 4, out_specs=[blk] * 3,
        out_shape=[jax.ShapeDtypeStruct((rows, lanes), F32)] * 3, compiler_params=_cparams("parallel"),
    )(w, g, m, v)


HBM = pl.BlockSpec(memory_space=pltpu.HBM)


def _place():
    x, y, c = lax.axis_index("x"), lax.axis_index("y"), lax.axis_index("c")
    chips = [(1 - x, y), (x, 1 - y), (1 - x, 1 - y)]
    return x, y, c, chips


SEM = pl.BlockSpec(memory_space=pltpu.SEMAPHORE)
EFFECT = pltpu.SideEffectType.DATAFLOW_SIDE_EFFECTING
N_PEER = 3


def _sem_outs():
    return tuple(pltpu.SemaphoreType.DMA(()) for _ in range(2 * N_PEER))


def _gather_start(slots, tag):
    def body(in_ref, *refs):
        del in_ref
        sems, thru, token = refs[:2 * N_PEER], refs[2 * N_PEER], refs[2 * N_PEER + 1]
        x, y, c, chips = _place()
        own = thru.at[2 * x + y, c]
        for j, chip in enumerate(chips):
            pltpu.make_async_remote_copy(src_ref=own, dst_ref=own, send_sem=sems[j], recv_sem=sems[N_PEER + j],
                                         device_id=(*chip, c), device_id_type=MESH).start()
        token[...] = jnp.zeros_like(token)

    out = pl.pallas_call(
        body, name=f"gather_{tag}_start",
        out_shape=_sem_outs() + (pltpu.HBM(slots.shape, slots.dtype), jax.ShapeDtypeStruct((8, 128), F32)),
        in_specs=(HBM,), out_specs=(SEM,) * (2 * N_PEER) + (HBM, pl.BlockSpec(memory_space=pltpu.VMEM)),
        input_output_aliases={0: 2 * N_PEER}, compiler_params=pltpu.CompilerParams(has_side_effects=EFFECT),
    )(pltpu.with_memory_space_constraint(slots, pltpu.HBM))
    return out[:2 * N_PEER], out[2 * N_PEER], out[2 * N_PEER + 1]


def _gather_wait(sems, thru, after, tag):
    def body(in_ref, *refs):
        del in_ref
        sems, out_ref = refs[:2 * N_PEER], refs[2 * N_PEER + 1]
        x, y, c, chips = _place()
        own = out_ref.at[2 * x + y, c]
        for j, (cx, cy) in enumerate(chips):
            cp = pltpu.make_async_remote_copy(src_ref=own, dst_ref=out_ref.at[2 * cx + cy, c], send_sem=sems[j],
                                              recv_sem=sems[N_PEER + j], device_id=(cx, cy, c), device_id_type=MESH)
            cp.wait_send()
            cp.wait_recv()

    return pl.pallas_call(
        body, name=f"gather_{tag}_wait", out_shape=pltpu.HBM(thru.shape, thru.dtype),
        in_specs=(HBM,) + (SEM,) * (2 * N_PEER) + (pl.BlockSpec(memory_space=pl.ANY),), out_specs=HBM,
        input_output_aliases={0: 0}, compiler_params=pltpu.CompilerParams(has_side_effects=EFFECT),
    )(thru, *sems, after)


def _gather_forward(slots, tag):
    def body(in_ref, out_ref, send_sems, recv_sems):
        del in_ref
        x, y, c, chips = _place()
        cps = []
        for j, (cx, cy) in enumerate(chips):
            landed = out_ref.at[2 * cx + cy, c]
            cps.append(pltpu.make_async_remote_copy(src_ref=landed, dst_ref=landed, send_sem=send_sems.at[j], recv_sem=recv_sems.at[j],
                                                    device_id=(x, y, 1 - c), device_id_type=MESH))
        for cp in cps:
            cp.start()
        for j, (cx, cy) in enumerate(chips):
            other = out_ref.at[2 * cx + cy, 1 - c]
            pltpu.make_async_remote_copy(src_ref=other, dst_ref=other, send_sem=send_sems.at[j], recv_sem=recv_sems.at[j],
                                         device_id=(x, y, 1 - c), device_id_type=MESH).wait_recv()
        for cp in cps:
            cp.wait_send()

    return pl.pallas_call(
        body, name=f"gather_{tag}_forward", out_shape=jax.ShapeDtypeStruct(slots.shape, slots.dtype),
        in_specs=[HBM], out_specs=HBM, input_output_aliases={0: 0},
        scratch_shapes=[pltpu.SemaphoreType.DMA((N_PEER,)), pltpu.SemaphoreType.DMA((N_PEER,))],
    )(slots)


def _scatter_start(pa, tag):
    ns, rh, lanes = pa.shape
    land = pltpu.with_memory_space_constraint(lax.empty((N_PEER, rh, lanes), pa.dtype), pltpu.HBM)

    def body(pa_ref, land_ref, *refs):
        x, y, c, chips = _place()
        for j, (cx, cy) in enumerate(chips):
            pltpu.make_async_remote_copy(src_ref=pa_ref.at[2 * cx + cy], dst_ref=land_ref.at[j], send_sem=refs[j],
                                         recv_sem=refs[N_PEER + j], device_id=(cx, cy, c), device_id_type=MESH).start()
        refs[-1][...] = jnp.zeros_like(refs[-1])

    out = pl.pallas_call(
        body, name=f"scatter_start_{tag}",
        out_shape=_sem_outs() + (pltpu.HBM(pa.shape, pa.dtype), pltpu.HBM(land.shape, land.dtype), jax.ShapeDtypeStruct((8, 128), F32)),
        in_specs=(HBM, HBM), out_specs=(SEM,) * (2 * N_PEER) + (HBM, HBM, pl.BlockSpec(memory_space=pltpu.VMEM)),
        input_output_aliases={0: 2 * N_PEER, 1: 2 * N_PEER + 1}, compiler_params=pltpu.CompilerParams(has_side_effects=EFFECT),
    )(pltpu.with_memory_space_constraint(pa, pltpu.HBM), land)
    return (out[:2 * N_PEER], out[2 * N_PEER], out[2 * N_PEER + 1]), out[2 * N_PEER + 2]


def _scatter_wait(sems, pa_thru, land_thru, after, tag):
    def body(pa_ref, land_ref, *refs):
        sems = refs[:2 * N_PEER]
        x, y, c, chips = _place()
        for j, (cx, cy) in enumerate(chips):
            cp = pltpu.make_async_remote_copy(src_ref=pa_ref.at[2 * cx + cy], dst_ref=land_ref.at[j], send_sem=sems[j],
                                              recv_sem=sems[N_PEER + j], device_id=(cx, cy, c), device_id_type=MESH)
            cp.wait_send()
            cp.wait_recv()

    return pl.pallas_call(
        body, name=f"scatter_wait_{tag}",
        out_shape=(pltpu.HBM(pa_thru.shape, pa_thru.dtype), pltpu.HBM(land_thru.shape, land_thru.dtype)),
        in_specs=(HBM, HBM) + (SEM,) * (2 * N_PEER) + (pl.BlockSpec(memory_space=pl.ANY),), out_specs=(HBM, HBM),
        input_output_aliases={0: 0, 1: 1}, compiler_params=pltpu.CompilerParams(has_side_effects=EFFECT),
    )(pa_thru, land_thru, *sems, after)


def _rs_swap(gp, tag):
    ns, _, rh, lanes = gp.shape

    def body(in_ref, out_ref, send_sem, recv_sem):
        x, y, c, _ = _place()
        cp = pltpu.make_async_remote_copy(src_ref=in_ref.at[:, 1 - c], dst_ref=out_ref, send_sem=send_sem, recv_sem=recv_sem,
                                          device_id=(x, y, 1 - c), device_id_type=MESH)
        cp.start()
        cp.wait_send()
        cp.wait_recv()

    return pl.pallas_call(
        body, name=f"rs_swap_{tag}", out_shape=jax.ShapeDtypeStruct((ns, rh, lanes), gp.dtype), in_specs=[HBM], out_specs=HBM,
        scratch_shapes=[pltpu.SemaphoreType.DMA, pltpu.SemaphoreType.DMA],
    )(gp)


def _rs_tile(rh):
    return _pick_tile(rh, cap=512)


def _rs_add(gp, recv, c, tag):
    ns, _, rh, lanes = gp.shape
    tile = _rs_tile(rh)

    def body(c_ref, a_ref, b_ref, o_ref):
        o_ref[...] = (a_ref[...].astype(F32) + b_ref[...].astype(F32)).astype(o_ref.dtype)

    return pl.pallas_call(
        body, name=f"rs_add_{tag}", out_shape=jax.ShapeDtypeStruct((ns, rh, lanes), gp.dtype),
        grid_spec=pltpu.PrefetchScalarGridSpec(
            num_scalar_prefetch=1, grid=(ns, rh // tile),
            in_specs=[pl.BlockSpec((None, None, tile, lanes), lambda s, i, c_ref: (s, c_ref[0], i, 0)),
                      pl.BlockSpec((None, tile, lanes), lambda s, i, c_ref: (s, i, 0))],
            out_specs=pl.BlockSpec((None, tile, lanes), lambda s, i, c_ref: (s, i, 0))),
        compiler_params=_cparams("parallel", "parallel"),
    )(c, gp, recv)


def _rs_sum(pa, recv, place, tag):
    ns, rh, lanes = pa.shape
    tile = _rs_tile(rh)

    def body(place_ref, a_ref, r_ref, o_ref):
        acc = a_ref[...].astype(F32)
        for j in range(ns - 1):
            acc = acc + r_ref[j].astype(F32)
        o_ref[...] = acc

    return pl.pallas_call(
        body, name=f"rs_sum_{tag}", out_shape=jax.ShapeDtypeStruct((2, rh, lanes), F32),
        grid_spec=pltpu.PrefetchScalarGridSpec(
            num_scalar_prefetch=1, grid=(rh // tile,),
            in_specs=[pl.BlockSpec((None, tile, lanes), lambda i, place_ref: (place_ref[0], i, 0)),
                      pl.BlockSpec((ns - 1, tile, lanes), lambda i, place_ref: (0, i, 0))],
            out_specs=pl.BlockSpec((None, tile, lanes), lambda i, place_ref: (place_ref[1], i, 0))),
        compiler_params=_cparams("parallel"),
    )(place, pa, recv)


def _rs_share(halves, tag):
    def body(in_ref, out_ref, send_sem, recv_sem):
        del in_ref
        x, y, c, _ = _place()
        cp = pltpu.make_async_remote_copy(src_ref=out_ref.at[c], dst_ref=out_ref.at[c], send_sem=send_sem, recv_sem=recv_sem,
                                          device_id=(x, y, 1 - c), device_id_type=MESH)
        cp.start()
        other = out_ref.at[1 - c]
        pltpu.make_async_remote_copy(src_ref=other, dst_ref=other, send_sem=send_sem, recv_sem=recv_sem,
                                     device_id=(x, y, 1 - c), device_id_type=MESH).wait_recv()
        cp.wait_send()

    return pl.pallas_call(
        body, name=f"rs_share_{tag}", out_shape=jax.ShapeDtypeStruct(halves.shape, halves.dtype), in_specs=[HBM], out_specs=HBM,
        input_output_aliases={0: 0}, scratch_shapes=[pltpu.SemaphoreType.DMA, pltpu.SemaphoreType.DMA],
    )(halves)


N_DEV = 8
SMALL_ROWS = 160


def _allreduce_small(v):
    m_per, n = v.shape

    def body(x_ref, out_ref, all_ref, send_sems, recv_sems, local_sem):
        x, y, c, chips = _place()
        me, sibling = (x, y, c), (x, y, 1 - c)

        def rows(px, py, pc):
            return all_ref.at[pl.ds((4 * px + 2 * py + pc) * m_per, m_per), :]

        def copy(k, block, to, src=None):
            return pltpu.make_async_remote_copy(src_ref=rows(*block) if src is None else src, dst_ref=rows(*block),
                                                send_sem=send_sems.at[k], recv_sem=recv_sems.at[k], device_id=to, device_id_type=MESH)

        mine = pltpu.make_async_copy(x_ref, rows(*me), local_sem)
        mine.start()
        first = [copy(0, me, sibling, src=x_ref)]
        first += [copy(1 + j, me, (*chip, c), src=x_ref) for j, chip in enumerate(chips)]
        for cp in first:
            cp.start()
        passed = [copy(4 + j, (*chip, c), sibling) for j, chip in enumerate(chips)]
        for j, chip in enumerate(chips):
            copy(1 + j, (*chip, c), me).wait_recv()
            passed[j].start()
        copy(0, sibling, me).wait_recv()
        for j, chip in enumerate(chips):
            copy(4 + j, (*chip, 1 - c), me).wait_recv()
        for cp in first + passed:
            cp.wait_send()
        mine.wait()
        step = 32
        for r in range(0, m_per, step):
            acc = all_ref[r:r + step, :]
            for d in range(1, N_DEV):
                acc = acc + all_ref[d * m_per + r:d * m_per + r + step, :]
            out_ref[r:r + step, :] = acc

    vm = pl.BlockSpec(memory_space=pltpu.VMEM)
    return pl.pallas_call(
        body, name="allreduce_small", out_shape=jax.ShapeDtypeStruct((m_per, n), v.dtype), in_specs=[vm], out_specs=vm,
        scratch_shapes=[pltpu.VMEM((N_DEV * m_per, n), v.dtype), pltpu.SemaphoreType.DMA((7,)), pltpu.SemaphoreType.DMA((7,)),
                        pltpu.SemaphoreType.DMA],
        compiler_params=pltpu.CompilerParams(vmem_limit_bytes=V7X_VMEM_LIMIT),
    )(v)


BIG = {"ffn1_w_gate": ((D_MODEL, D_FF), 1), "ffn1_w_up": ((D_MODEL, D_FF), 1), "ffn1_w_down": ((D_FF, D_MODEL), 0),
       "ffn2_w_gate": ((D_MODEL, D_FF), 1), "ffn2_w_up": ((D_MODEL, D_FF), 1), "ffn2_w_down": ((D_FF, D_MODEL), 0),
       "w_in": ((D_MODEL, IN_WIDTH), 1), "w_mem_kv": ((D_MODEL, 2 * MEM_W), 0), "w_branch_ssd": ((SSD_INNER, D_MODEL), 0),
       "w_branch_gmlp": ((GMLP_W, D_MODEL), 0), "w_branch_mem": ((MEM_W, D_MODEL), 1), "w_out": ((D_MODEL, D_MODEL), 0)}
FFN1 = ("ffn1_w_gate", "ffn1_w_up", "ffn1_w_down")
FFN2 = ("ffn2_w_gate", "ffn2_w_up", "ffn2_w_down")
MIXER = ("w_out", "w_branch_ssd", "w_branch_gmlp", "w_branch_mem", "w_mem_kv", "w_in")
GATHER_GROUPS = (FFN1, FFN2 + MIXER)
REDUCE_GROUPS = (FFN2, MIXER, FFN1)
CONV_W_ROWS = 8


def _shard_rows_of(name):
    (a, b), _ = BIG[name]
    return a * b // N_SHARD // LANES


def _group_rows(names, extra=0):
    return -(-(sum(_shard_rows_of(n) for n in names) + extra) // 32) * 32

SMALL = [("ffn1_norm", 1), ("mix_norm", 1), ("mem_norm", 1), ("ssd_conv_b", 3), ("heads", 1), ("ssd_norm", 2),
         ("gmlp_v_norm", 1), ("gmlp_w_s", 128), ("gmlp_b_s", 1), ("ffn2_norm", 1), ("final_norm", 1), ("ssd_conv_w", 12)]
assert sum(n for _, n in SMALL) <= SMALL_ROWS
HEAD_VECS = ("ssd_dt_bias", "ssd_a_log", "ssd_d")


def _pack_small(vals, loss=None):
    parts = []
    for name, nrows in SMALL:
        if name == "heads":
            row = jnp.concatenate([vals[k].reshape(-1) for k in HEAD_VECS]
                                  + [jnp.zeros((1,), F32) if loss is None else loss.reshape(1)])
            parts.append(jnp.pad(row, (0, LANES - row.shape[0])).reshape(1, LANES))
        elif name in vals:
            parts.append(vals[name].reshape(nrows, LANES))
        else:
            parts.append(jnp.zeros((nrows, LANES), F32))
    buf = jnp.concatenate(parts, axis=0)
    return jnp.pad(buf, ((0, SMALL_ROWS - buf.shape[0]), (0, 0)))


def _unpack_small(buf):
    out, r = {}, 0
    for name, nrows in SMALL:
        blk = buf[r:r + nrows]
        r += nrows
        if name == "heads":
            for i, k in enumerate(HEAD_VECS):
                out[k] = blk[0, i * SSD_HEADS:(i + 1) * SSD_HEADS]
            out["loss"] = blk[0, 3 * SSD_HEADS]
        else:
            out[name] = blk
    return out


def _wire_shape(name):
    (a, b), axis = BIG[name]
    return (b, a) if axis == 1 else (a, b)


def _pack_weights(given, names, conv=False):
    parts = [(given[n][0].T if BIG[n][1] == 1 else given[n][0]).astype(BF16).reshape(_shard_rows_of(n), LANES) for n in names]
    if conv:
        pairs = lax.bitcast_convert_type(given["ssd_conv_w"], BF16).reshape(-1)
        parts.append(jnp.pad(pairs, (0, CONV_W_ROWS * LANES - pairs.shape[0])).reshape(CONV_W_ROWS, LANES))
    total = _group_rows(names, CONV_W_ROWS if conv else 0)
    packed = jnp.concatenate(parts, axis=0)
    packed = jnp.pad(packed, ((0, total - packed.shape[0]), (0, 0))).reshape(1, 2, total // 2, LANES)
    return jnp.broadcast_to(packed, (N_SHARD, 2, total // 2, LANES))


def _unpack_weights(slots, names, conv=False):
    rows = slots.reshape(N_SHARD, -1, LANES)
    out, r = {}, 0
    for name in names:
        n = _shard_rows_of(name)
        out[name] = rows[:, r:r + n].reshape(_wire_shape(name))
        r += n
    if conv:
        cols = XBC // N_SHARD
        pairs = rows[:, r:r + CONV_W_ROWS].reshape(N_SHARD, -1)[:, :SSD_CONV * cols * 2].reshape(N_SHARD, SSD_CONV, cols, 2)
        out["ssd_conv_w"] = jnp.transpose(lax.bitcast_convert_type(pairs, F32), (1, 0, 2)).reshape(SSD_CONV, XBC)
    return out


def _pack_grads(grads, names):
    total = _group_rows(names)
    parts = [grads[n].astype(BF16).reshape(N_SHARD, _shard_rows_of(n), LANES) for n in names]
    pad = total - sum(p.shape[1] for p in parts)
    if pad:
        parts.append(jnp.zeros((N_SHARD, pad, LANES), BF16))
    return jnp.concatenate(parts, axis=1).reshape(N_SHARD, 2, total // 2, LANES)


def kernel(x, mem, ffn1_norm, ffn1_w_gate, ffn1_w_up, ffn1_w_down, mix_norm, mem_norm, w_in, ssd_conv_w, ssd_conv_b, ssd_dt_bias, ssd_a_log, ssd_d, ssd_norm, gmlp_v_norm, gmlp_w_s, gmlp_b_s, w_mem_kv, w_branch_ssd, w_branch_gmlp, w_branch_mem, w_out, ffn2_norm, ffn2_w_gate, ffn2_w_up, ffn2_w_down, final_norm, loss_target, m_ffn1_norm, m_ffn1_w_gate, m_ffn1_w_up, m_ffn1_w_down, m_mix_norm, m_mem_norm, m_w_in, m_ssd_conv_w, m_ssd_conv_b, m_ssd_dt_bias, m_ssd_a_log, m_ssd_d, m_ssd_norm, m_gmlp_v_norm, m_gmlp_w_s, m_gmlp_b_s, m_w_mem_kv, m_w_branch_ssd, m_w_branch_gmlp, m_w_branch_mem, m_w_out, m_ffn2_norm, m_ffn2_w_gate, m_ffn2_w_up, m_ffn2_w_down, m_final_norm, v_ffn1_norm, v_ffn1_w_gate, v_ffn1_w_up, v_ffn1_w_down, v_mix_norm, v_mem_norm, v_w_in, v_ssd_conv_w, v_ssd_conv_b, v_ssd_dt_bias, v_ssd_a_log, v_ssd_d, v_ssd_norm, v_gmlp_v_norm, v_gmlp_w_s, v_gmlp_b_s, v_w_mem_kv, v_w_branch_ssd, v_w_branch_gmlp, v_w_branch_mem, v_w_out, v_ffn2_norm, v_ffn2_w_gate, v_ffn2_w_up, v_ffn2_w_down, v_final_norm):
    given = dict(x=x, mem=mem, ffn1_norm=ffn1_norm, ffn1_w_gate=ffn1_w_gate, ffn1_w_up=ffn1_w_up, ffn1_w_down=ffn1_w_down, mix_norm=mix_norm, mem_norm=mem_norm, w_in=w_in, ssd_conv_w=ssd_conv_w, ssd_conv_b=ssd_conv_b, ssd_dt_bias=ssd_dt_bias, ssd_a_log=ssd_a_log, ssd_d=ssd_d, ssd_norm=ssd_norm, gmlp_v_norm=gmlp_v_norm, gmlp_w_s=gmlp_w_s, gmlp_b_s=gmlp_b_s, w_mem_kv=w_mem_kv, w_branch_ssd=w_branch_ssd, w_branch_gmlp=w_branch_gmlp, w_branch_mem=w_branch_mem, w_out=w_out, ffn2_norm=ffn2_norm, ffn2_w_gate=ffn2_w_gate, ffn2_w_up=ffn2_w_up, ffn2_w_down=ffn2_w_down, final_norm=final_norm, loss_target=loss_target, m_ffn1_norm=m_ffn1_norm, m_ffn1_w_gate=m_ffn1_w_gate, m_ffn1_w_up=m_ffn1_w_up, m_ffn1_w_down=m_ffn1_w_down, m_mix_norm=m_mix_norm, m_mem_norm=m_mem_norm, m_w_in=m_w_in, m_ssd_conv_w=m_ssd_conv_w, m_ssd_conv_b=m_ssd_conv_b, m_ssd_dt_bias=m_ssd_dt_bias, m_ssd_a_log=m_ssd_a_log, m_ssd_d=m_ssd_d, m_ssd_norm=m_ssd_norm, m_gmlp_v_norm=m_gmlp_v_norm, m_gmlp_w_s=m_gmlp_w_s, m_gmlp_b_s=m_gmlp_b_s, m_w_mem_kv=m_w_mem_kv, m_w_branch_ssd=m_w_branch_ssd, m_w_branch_gmlp=m_w_branch_gmlp, m_w_branch_mem=m_w_branch_mem, m_w_out=m_w_out, m_ffn2_norm=m_ffn2_norm, m_ffn2_w_gate=m_ffn2_w_gate, m_ffn2_w_up=m_ffn2_w_up, m_ffn2_w_down=m_ffn2_w_down, m_final_norm=m_final_norm, v_ffn1_norm=v_ffn1_norm, v_ffn1_w_gate=v_ffn1_w_gate, v_ffn1_w_up=v_ffn1_w_up, v_ffn1_w_down=v_ffn1_w_down, v_mix_norm=v_mix_norm, v_mem_norm=v_mem_norm, v_w_in=v_w_in, v_ssd_conv_w=v_ssd_conv_w, v_ssd_conv_b=v_ssd_conv_b, v_ssd_dt_bias=v_ssd_dt_bias, v_ssd_a_log=v_ssd_a_log, v_ssd_d=v_ssd_d, v_ssd_norm=v_ssd_norm, v_gmlp_v_norm=v_gmlp_v_norm, v_gmlp_w_s=v_gmlp_w_s, v_gmlp_b_s=v_gmlp_b_s, v_w_mem_kv=v_w_mem_kv, v_w_branch_ssd=v_w_branch_ssd, v_w_branch_gmlp=v_w_branch_gmlp, v_w_branch_mem=v_w_branch_mem, v_w_out=v_w_out, v_ffn2_norm=v_ffn2_norm, v_ffn2_w_gate=v_ffn2_w_gate, v_ffn2_w_up=v_ffn2_w_up, v_ffn2_w_down=v_ffn2_w_down, v_final_norm=v_final_norm)
    weights = [n for n in given if n not in ("x", "mem", "loss_target") and not n.startswith(("m_", "v_"))]
    xi, yi, ci = lax.axis_index("x"), lax.axis_index("y"), lax.axis_index("c")
    chip = (2 * xi + yi).astype(jnp.int32)
    core = ci.astype(jnp.int32)
    conv_cols = XBC // N_SHARD

    copies = {"first": _gather_start(_pack_weights(given, FFN1), "first")}
    tied = dict(given)
    tied["w_in"], _ = lax.optimization_barrier((given["w_in"], copies["first"][2]))
    rest_slots = _pack_weights(tied, GATHER_GROUPS[1], conv=True)
    w = {}
    for name in ("ffn1_norm", "mix_norm", "mem_norm", "ssd_conv_b", "ssd_norm", "gmlp_v_norm", "ffn2_norm", "final_norm"):
        w[name] = given[name].reshape(1, -1)
    w["ffn1_norm"] = w["ffn1_norm"] + copies["first"][2][0:1, 0:1]
    for name in HEAD_VECS:
        w[name] = given[name].reshape(-1)
    w["gmlp_w_s"] = given["gmlp_w_s"][0]
    w["gmlp_b_s"] = given["gmlp_b_s"][0]

    def arrived(tag, after):
        sems, thru, _ = copies[tag]
        return _gather_forward(_gather_wait(sems, thru, after, tag), tag)

    def first_weights(after):
        slots = arrived("first", after)
        slots, rest = lax.optimization_barrier((slots, rest_slots))
        copies["rest"] = _gather_start(rest, "rest")
        first = _unpack_weights(slots, FFN1)
        name = FFN1[0]
        first[name], _ = lax.optimization_barrier((first[name], copies["rest"][2]))
        return first

    def rest_weights(after):
        rest = _unpack_weights(arrived("rest", after), GATHER_GROUPS[1], conv=True)
        rest["w_in_p"], rest["w_dt"] = _split_w_in(rest.pop("w_in"))
        return rest

    pending = {}

    def push(k, group_grads):
        gp = _pack_grads(group_grads, REDUCE_GROUPS[k])
        pa = _rs_add(gp, _rs_swap(gp, k), core.reshape(1), k)
        pending[k], token = _scatter_start(pa, k)
        return token[0:1, 0:1]

    def reduced(k, after):
        pa, land = _scatter_wait(*pending[k], after, k)
        gsum = _rs_share(_rs_sum(pa, land, jnp.stack([chip, core]), k), k)
        rows = gsum.reshape(-1, LANES)
        out, r = {}, 0
        for name in REDUCE_GROUPS[k]:
            n = _shard_rows_of(name)
            a, b = given[name].shape[1:]
            out[name] = rows[r:r + n].reshape(b, a).T if BIG[name][1] == 1 else rows[r:r + n].reshape(a, b)
            r += n
        return out

    loss_part, grad_x, g = _local_step(x[0], mem[0], loss_target[0], w, first_weights, rest_weights, push)

    grads, deltas, new_m, new_v = {}, {}, {}, {}

    def update(k, after):
        for name, gl in reduced(k, after).items():
            d, nm, nv = _adamw(given[name][0], gl, given["m_" + name][0], given["v_" + name][0], f"adamw_{name}")
            grads[name], deltas[name], new_m[name], new_v[name] = (a[None] for a in (gl, d, nm, nv))

    update(0, grad_x)
    update(1, deltas[REDUCE_GROUPS[0][-1]])

    small_vals = {k: g[k] for k, _ in SMALL if k != "heads"}
    small_vals.update({k: g[k] for k in HEAD_VECS})
    red = _unpack_small(_allreduce_small(_pack_small(small_vals, loss=loss_part[0, 0])))
    update(2, deltas[REDUCE_GROUPS[1][-1]])
    conv_g = lax.dynamic_slice_in_dim(red["ssd_conv_w"].reshape(SSD_CONV, XBC), chip * conv_cols, conv_cols, axis=1)
    d, nm, nv = _adamw(given["ssd_conv_w"][0], conv_g, given["m_ssd_conv_w"][0], given["v_ssd_conv_w"][0], "adamw_conv_w")
    grads["ssd_conv_w"], deltas["ssd_conv_w"], new_m["ssd_conv_w"], new_v["ssd_conv_w"] = (a[None] for a in (conv_g, d, nm, nv))
    for k in [k for k, _ in SMALL if k not in ("heads", "ssd_conv_w")] + list(HEAD_VECS):
        shape = given[k].shape
        as2d = lambda a: a.reshape(-1, shape[-1])
        d, nm, nv = _adamw(as2d(given[k]), as2d(red[k]), as2d(given["m_" + k]), as2d(given["v_" + k]), f"adamw_{k}")
        grads[k], deltas[k], new_m[k], new_v[k] = (a.reshape(shape) for a in (red[k], d, nm, nv))

    return (red["loss"], grad_x[None], *[grads[n] for n in weights], *[deltas[n] for n in weights],
            *[new_m[n] for n in weights], *[new_v[n] for n in weights])
```

```python
import functools
import math

import jax
import jax.numpy as jnp
from jax import lax
from jax.experimental import pallas as pl
from jax.experimental.pallas import tpu as pltpu

F32, BF16 = jnp.float32, jnp.bfloat16
HI = lax.Precision.HIGHEST
MESH = pl.DeviceIdType.MESH

D_MODEL = 1024
D_FF = 2816
MEM_LEN = 256
SSD_INNER = 2048
SSD_HEADS = 32
SSD_GROUPS = 4
SSD_STATE = 128
SSD_CONV = 4
CHUNK = 128
XBC = SSD_INNER + 2 * SSD_GROUPS * SSD_STATE
GMLP_W = 1024
GMLP_GROUPS = 8
MEM_W = 256
MEM_HEADS = 4
EPS = 1e-6
IN_WIDTH = 10528
IN_Z, IN_XBC, IN_DT, IN_UV, IN_Q, IN_GL = 0, 2048, 5120, 5152, 7200, 7456
P_GL, P_Z, P_XBC, P_UV, P_Q, P_W = 0, 3072, 5120, 8192, 10240, 10752
P_USED = 10496

ADAM_LR, ADAM_B1, ADAM_B2, ADAM_EPS, ADAM_WD, ADAM_STEP = 0.001, 0.9, 0.999, 1e-08, 0.01, 10

V7X_VMEM_LIMIT = 56 * 1024 * 1024
N_SHARD = 4
LANES = 1024


def _cparams(*sem):
    return pltpu.CompilerParams(dimension_semantics=sem, vmem_limit_bytes=V7X_VMEM_LIMIT)


ANY = pl.BlockSpec(memory_space=pl.ANY)


def _sigmoid(x):
    return 0.5 * jnp.tanh(0.5 * x) + 0.5


def _row_tile(t):
    return min(512, t)


_DIMS = {"nn": (((1,), (0,)), ((), ())), "nt": (((1,), (1,)), ((), ())), "tn": (((0,), (0,)), ((), ()))}


def _matmul(a, b, *, mode, out_dtype, tm, tn, tk, name, scale=1.0, addend=None):
    if mode == "tn":
        k_dim, m_dim = a.shape
    else:
        m_dim, k_dim = a.shape
    n_dim = b.shape[0] if mode == "nt" else b.shape[1]
    tm, tn, tk = min(tm, m_dim), min(tn, n_dim), min(tk, k_dim)
    assert m_dim % tm == 0 and n_dim % tn == 0 and k_dim % tk == 0, (name, a.shape, b.shape, tm, tn, tk)
    ni, nj, nk = m_dim // tm, n_dim // tn, k_dim // tk
    a_spec = pl.BlockSpec((tk, tm), lambda j, i, k: (k, i)) if mode == "tn" else pl.BlockSpec((tm, tk), lambda j, i, k: (i, k))
    b_spec = pl.BlockSpec((tn, tk), lambda j, i, k: (j, k)) if mode == "nt" else pl.BlockSpec((tk, tn), lambda j, i, k: (k, j))
    o_spec = pl.BlockSpec((tm, tn), lambda j, i, k: (i, j))
    dims = _DIMS[mode]
    has_add = addend is not None

    def body(*refs):
        a_ref, b_ref = refs[:2]
        r_ref = refs[2] if has_add else None
        o_ref = refs[2 + has_add]

        def finish(acc):
            r = acc * scale if scale != 1.0 else acc
            if has_add:
                r = r + r_ref[...].astype(F32)
            o_ref[...] = r.astype(o_ref.dtype)

        prod = lax.dot_general(a_ref[...].astype(BF16), b_ref[...].astype(BF16), dims, preferred_element_type=F32)
        if nk == 1:
            finish(prod)
            return
        acc_ref = refs[-1]
        k = pl.program_id(2)

        @pl.when(k == 0)
        def _():
            acc_ref[...] = prod

        @pl.when(k > 0)
        def _():
            acc_ref[...] += prod

        @pl.when(k == nk - 1)
        def _():
            finish(acc_ref[...])

    in_specs = [a_spec, b_spec] + ([o_spec] if has_add else [])
    args = (a, b) + ((addend,) if has_add else ())
    return pl.pallas_call(
        body, name=name, grid=(nj, ni, nk), in_specs=in_specs, out_specs=o_spec,
        out_shape=jax.ShapeDtypeStruct((m_dim, n_dim), out_dtype),
        scratch_shapes=[] if nk == 1 else [pltpu.VMEM((tm, tn), F32)],
        compiler_params=_cparams("parallel", "parallel", "arbitrary"),
    )(*args)


ROW_STRIP = 16


def _strips(tm, fn, init=None, rb=ROW_STRIP):
    def step(i, carry):
        return fn(pl.ds(pl.multiple_of(i * rb, rb), rb), carry)
    return lax.fori_loop(0, tm // rb, step, init, unroll=2)


def _rms_fwd(x, gain, name):
    t, d = x.shape
    tm = _row_tile(t)

    def body(x_ref, g_ref, o_ref):
        xv = x_ref[...]
        r = lax.rsqrt(jnp.mean(xv * xv, axis=-1, keepdims=True) + EPS)
        o_ref[...] = (xv * r * g_ref[...]).astype(o_ref.dtype)

    return pl.pallas_call(
        body, name=name, grid=(t // tm,),
        in_specs=[pl.BlockSpec((tm, d), lambda i: (i, 0)), pl.BlockSpec((1, d), lambda i: (0, 0))],
        out_specs=pl.BlockSpec((tm, d), lambda i: (i, 0)),
        out_shape=jax.ShapeDtypeStruct((t, d), BF16), compiler_params=_cparams("parallel"),
    )(x, gain)


def _rms_bwd(x, gain, dn, dres, name):
    t, d = x.shape
    tm = _row_tile(t)
    has_res = dres is not None

    def body(*refs):
        if has_res:
            x_ref, g_ref, dn_ref, r_ref, dx_ref, dg_ref = refs
        else:
            x_ref, g_ref, dn_ref, dx_ref, dg_ref = refs

        @pl.when(pl.program_id(0) == 0)
        def _():
            dg_ref[...] = jnp.zeros_like(dg_ref)

        xv = x_ref[...]
        r = lax.rsqrt(jnp.mean(xv * xv, axis=-1, keepdims=True) + EPS)
        xh = xv * r
        dnv = dn_ref[...].astype(F32)
        dg_ref[...] += jnp.sum(dnv * xh, axis=0, keepdims=True)
        dxh = dnv * g_ref[...]
        dx = r * (dxh - xh * jnp.mean(dxh * xh, axis=-1, keepdims=True))
        if has_res:
            dx = dx + r_ref[...]
        dx_ref[...] = dx

    row = pl.BlockSpec((tm, d), lambda i: (i, 0))
    vec = pl.BlockSpec((1, d), lambda i: (0, 0))
    in_specs = [row, vec, row] + ([row] if has_res else [])
    args = (x, gain, dn) + ((dres,) if has_res else ())
    return pl.pallas_call(
        body, name=name, grid=(t // tm,), in_specs=in_specs, out_specs=[row, vec],
        out_shape=[jax.ShapeDtypeStruct((t, d), F32), jax.ShapeDtypeStruct((1, d), F32)],
        compiler_params=_cparams("arbitrary"),
    )(*args)


def _loss_head(h, gain, target, name):
    t, d = h.shape
    tm = _row_tile(t)

    def body(h_ref, g_ref, t_ref, dh_ref, dg_ref, l_ref):
        @pl.when(pl.program_id(0) == 0)
        def _():
            dg_ref[...] = jnp.zeros_like(dg_ref)
            l_ref[...] = jnp.zeros_like(l_ref)

        xv = h_ref[...]
        g = g_ref[...]
        r = lax.rsqrt(jnp.mean(xv * xv, axis=-1, keepdims=True) + EPS)
        xh = xv * r
        err = xh * g - t_ref[...]
        l_ref[...] += 0.5 * jnp.sum(jnp.mean(err * err, axis=-1, keepdims=True), axis=0, keepdims=True)
        dy = err * (1.0 / d)
        dg_ref[...] += jnp.sum(dy * xh, axis=0, keepdims=True)
        dxh = dy * g
        dh_ref[...] = r * (dxh - xh * jnp.mean(dxh * xh, axis=-1, keepdims=True))

    row = pl.BlockSpec((tm, d), lambda i: (i, 0))
    vec = pl.BlockSpec((1, d), lambda i: (0, 0))
    return pl.pallas_call(
        body, name=name, grid=(t // tm,), in_specs=[row, vec, row],
        out_specs=[row, vec, pl.BlockSpec((1, 128), lambda i: (0, 0))],
        out_shape=[jax.ShapeDtypeStruct((t, d), F32), jax.ShapeDtypeStruct((1, d), F32), jax.ShapeDtypeStruct((1, 128), F32)],
        compiler_params=_cparams("arbitrary"),
    )(h, gain, target)


FF_TILE = 1408


def _ffn_fwd(n, x, wg, wu, wd, name):
    t, d = x.shape
    tm, tn = _row_tile(t), FF_TILE
    nj = D_FF // tn

    def body(n_ref, x_ref, wg_ref, wu_ref, wd_ref, h_ref, g_ref, u_ref, acc_ref):
        j = pl.program_id(1)

        @pl.when(j == 0)
        def _():
            acc_ref[...] = jnp.zeros_like(acc_ref)

        nb = n_ref[...]
        g = lax.dot_general(nb, wg_ref[...], _DIMS["nt"], preferred_element_type=F32)
        u = lax.dot_general(nb, wu_ref[...], _DIMS["nt"], preferred_element_type=F32)
        g_ref[...] = g.astype(BF16)
        u_ref[...] = u.astype(BF16)
        a = g * _sigmoid(g) * u
        acc_ref[...] += jnp.dot(a.astype(BF16), wd_ref[...], preferred_element_type=F32)

        @pl.when(j == nj - 1)
        def _():
            h_ref[...] = x_ref[...] + 0.5 * acc_ref[...]

    row = pl.BlockSpec((tm, d), lambda i, j: (i, 0))
    act = pl.BlockSpec((tm, tn), lambda i, j: (i, j))
    return pl.pallas_call(
        body, name=name, grid=(t // tm, nj),
        in_specs=[row, row] + [pl.BlockSpec((tn, d), lambda i, j: (j, 0))] * 3,
        out_specs=[row, act, act],
        out_shape=[jax.ShapeDtypeStruct((t, d), F32), jax.ShapeDtypeStruct((t, D_FF), BF16), jax.ShapeDtypeStruct((t, D_FF), BF16)],
        scratch_shapes=[pltpu.VMEM((tm, d), F32)], compiler_params=_cparams("parallel", "arbitrary"),
    )(n, x, wg, wu, wd)


def _ffn_bwd_act(dh, g, u, wg, wu, wd, name):
    t, d = dh.shape
    tm, tn = _row_tile(t), FF_TILE
    nj = D_FF // tn

    def body(dh_ref, g_ref, u_ref, wg_ref, wu_ref, wd_ref, dn_ref, dg_ref, du_ref, a_ref, acc_ref):
        j = pl.program_id(1)

        @pl.when(j == 0)
        def _():
            acc_ref[...] = jnp.zeros_like(acc_ref)

        dhb = (0.5 * dh_ref[...]).astype(BF16)
        da = lax.dot_general(dhb, wd_ref[...], _DIMS["nt"], preferred_element_type=F32)
        gv = g_ref[...].astype(F32)
        uv = u_ref[...].astype(F32)
        sg = _sigmoid(gv)
        s = gv * sg
        dg = (da * uv * (sg * (1.0 + gv * (1.0 - sg)))).astype(BF16)
        du = (da * s).astype(BF16)
        dg_ref[...] = dg
        du_ref[...] = du
        a_ref[...] = (s * uv).astype(BF16)
        acc_ref[...] += (jnp.dot(dg, wg_ref[...], preferred_element_type=F32)
                         + jnp.dot(du, wu_ref[...], preferred_element_type=F32))

        @pl.when(j == nj - 1)
        def _():
            dn_ref[...] = acc_ref[...]

    row = pl.BlockSpec((tm, d), lambda i, j: (i, 0))
    act = pl.BlockSpec((tm, tn), lambda i, j: (i, j))
    return pl.pallas_call(
        body, name=name, grid=(t // tm, nj),
        in_specs=[row, act, act] + [pl.BlockSpec((tn, d), lambda i, j: (j, 0))] * 3,
        out_specs=[row, act, act, act],
        out_shape=[jax.ShapeDtypeStruct((t, d), F32)] + [jax.ShapeDtypeStruct((t, D_FF), BF16)] * 3,
        scratch_shapes=[pltpu.VMEM((tm, d), F32)], compiler_params=_cparams("parallel", "arbitrary"),
    )(dh, g, u, wg, wu, wd)


def _ffn_forward(x, gain, weights, tag):
    n = _rms_fwd(x, gain, f"{tag}_norm")
    h, g, u = _ffn_fwd(n, x, *weights(n), f"{tag}_fwd")
    return h, (n, g, u)


def _ffn_backward(dh, x, gain, wg, wu, wd, saved, tag, push):
    n, g, u = saved
    dn, dg, du, a = _ffn_bwd_act(dh, g, u, wg, wu, wd, f"{tag}_bwd_act")
    kw = dict(mode="tn", out_dtype=BF16, tm=FF_TILE, tn=1024, tk=2048)
    d_wg = _matmul(dg, n, name=f"{tag}_dwg", **kw)
    d_wu = _matmul(du, n, name=f"{tag}_dwu", **kw)
    d_wd = _matmul(a, dh, scale=0.5, name=f"{tag}_dwd", **kw)
    token = push({f"{tag}_w_gate": d_wg, f"{tag}_w_up": d_wu, f"{tag}_w_down": d_wd})
    return _rms_bwd(x, gain + token, dn, dh, f"{tag}_norm_bwd")


CONV_COLS = 512
HALO = 8
CONV_STRIP = 32
CONV_ROWS = 1024


def _conv_fwd(p, w, b, name):
    t = p.shape[0]
    tm = min(CONV_ROWS, t)
    c0 = P_XBC // CONV_COLS

    def body(x_ref, halo_ref, w_ref, b_ref, o_ref, s_ref):
        i = pl.program_id(1)
        s_ref[0:HALO, :] = jnp.where(i > 0, halo_ref[...].astype(F32), 0.0)
        s_ref[HALO:HALO + tm, :] = x_ref[...].astype(F32)
        wv = w_ref[...]
        bv = b_ref[...]
        for r0 in range(0, tm, CONV_STRIP):
            acc = bv + wv[0:1, :] * s_ref[HALO - 3 + r0:HALO - 3 + r0 + CONV_STRIP, :]
            for k in range(1, SSD_CONV):
                acc = acc + wv[k:k + 1, :] * s_ref[HALO - 3 + k + r0:HALO - 3 + k + r0 + CONV_STRIP, :]
            o_ref[r0:r0 + CONV_STRIP, :] = (acc * _sigmoid(acc)).astype(o_ref.dtype)

    return pl.pallas_call(
        body, name=name, grid=(XBC // CONV_COLS, t // tm),
        in_specs=[pl.BlockSpec((tm, CONV_COLS), lambda j, i: (i, c0 + j)),
                  pl.BlockSpec((HALO, CONV_COLS), lambda j, i: (jnp.maximum(i * (tm // HALO) - 1, 0), c0 + j)),
                  pl.BlockSpec((SSD_CONV, CONV_COLS), lambda j, i: (0, j)), pl.BlockSpec((1, CONV_COLS), lambda j, i: (0, j))],
        out_specs=pl.BlockSpec((tm, CONV_COLS), lambda j, i: (i, j)),
        out_shape=jax.ShapeDtypeStruct((t, XBC), BF16),
        scratch_shapes=[pltpu.VMEM((tm + HALO, CONV_COLS), F32)], compiler_params=_cparams("parallel", "parallel"),
    )(p, p, w, b)


def _conv_bwd_act(p, dy, w, b, col0, name):
    t, cols = dy.shape
    tm = min(CONV_ROWS, t)
    c0 = (P_XBC + col0) // CONV_COLS
    w0 = col0 // CONV_COLS

    def body(x_ref, halo_ref, dy_ref, w_ref, b_ref, da_ref, dw_ref, db_ref, s_ref):
        i = pl.program_id(1)

        @pl.when(i == 0)
        def _():
            dw_ref[...] = jnp.zeros_like(dw_ref)
            db_ref[...] = jnp.zeros_like(db_ref)

        s_ref[0:HALO, :] = jnp.where(i > 0, halo_ref[...].astype(F32), 0.0)
        s_ref[HALO:HALO + tm, :] = x_ref[...].astype(F32)
        wv = w_ref[...]
        bv = b_ref[...]
        fold = lambda v: jnp.sum(v.reshape(CONV_STRIP // 8, 8, CONV_COLS), axis=0)
        sums = [jnp.zeros((8, CONV_COLS), F32) for _ in range(SSD_CONV + 1)]
        for r0 in range(0, tm, CONV_STRIP):
            taps = [s_ref[HALO - 3 + k + r0:HALO - 3 + k + r0 + CONV_STRIP, :] for k in range(SSD_CONV)]
            acc = bv + wv[0:1, :] * taps[0]
            for k in range(1, SSD_CONV):
                acc = acc + wv[k:k + 1, :] * taps[k]
            sg = _sigmoid(acc)
            dacc = dy_ref[r0:r0 + CONV_STRIP, :].astype(F32) * (sg * (1.0 + acc * (1.0 - sg)))
            da_ref[r0:r0 + CONV_STRIP, :] = dacc.astype(BF16)
            for k in range(SSD_CONV):
                sums[k] = sums[k] + fold(dacc * taps[k])
            sums[SSD_CONV] = sums[SSD_CONV] + fold(dacc)
        for k in range(SSD_CONV):
            dw_ref[k:k + 1, :] += jnp.sum(sums[k], axis=0, keepdims=True)
        db_ref[...] += jnp.sum(sums[SSD_CONV], axis=0, keepdims=True)

    return pl.pallas_call(
        body, name=name, grid=(cols // CONV_COLS, t // tm),
        in_specs=[pl.BlockSpec((tm, CONV_COLS), lambda j, i: (i, c0 + j)),
                  pl.BlockSpec((HALO, CONV_COLS), lambda j, i: (jnp.maximum(i * (tm // HALO) - 1, 0), c0 + j)),
                  pl.BlockSpec((tm, CONV_COLS), lambda j, i: (i, j)),
                  pl.BlockSpec((SSD_CONV, CONV_COLS), lambda j, i: (0, w0 + j)), pl.BlockSpec((1, CONV_COLS), lambda j, i: (0, w0 + j))],
        out_specs=[pl.BlockSpec((tm, CONV_COLS), lambda j, i: (i, j)), pl.BlockSpec((SSD_CONV, CONV_COLS), lambda j, i: (0, j)),
                   pl.BlockSpec((1, CONV_COLS), lambda j, i: (0, j))],
        out_shape=[jax.ShapeDtypeStruct((t, cols), BF16), jax.ShapeDtypeStruct((SSD_CONV, cols), F32), jax.ShapeDtypeStruct((1, cols), F32)],
        scratch_shapes=[pltpu.VMEM((tm + HALO, CONV_COLS), F32)], compiler_params=_cparams("parallel", "arbitrary"),
    )(p, p, dy, w, b)


def _conv_bwd_dx(dacc, w, col0, dp, name):
    t, cols = dacc.shape
    tm = min(CONV_ROWS, t)
    nt = t // tm
    w0 = col0 // CONV_COLS
    c0 = (P_XBC + col0) // CONV_COLS

    def body(d_ref, halo_ref, w_ref, dp_ref, o_ref, s_ref):
        del dp_ref
        i = pl.program_id(1)
        s_ref[0:tm, :] = d_ref[...].astype(F32)
        s_ref[tm:tm + HALO, :] = jnp.where(i < nt - 1, halo_ref[...].astype(F32), 0.0)
        wv = w_ref[...]
        for r0 in range(0, tm, CONV_STRIP):
            acc = wv[3:4, :] * s_ref[r0:r0 + CONV_STRIP, :]
            for k in range(SSD_CONV - 1):
                acc = acc + wv[k:k + 1, :] * s_ref[3 - k + r0:3 - k + r0 + CONV_STRIP, :]
            o_ref[r0:r0 + CONV_STRIP, :] = acc.astype(o_ref.dtype)

    return pl.pallas_call(
        body, name=name, grid=(cols // CONV_COLS, nt),
        in_specs=[pl.BlockSpec((tm, CONV_COLS), lambda j, i: (i, j)),
                  pl.BlockSpec((HALO, CONV_COLS), lambda j, i: (jnp.minimum((i + 1) * (tm // HALO), t // HALO - 1), j)),
                  pl.BlockSpec((SSD_CONV, CONV_COLS), lambda j, i: (0, w0 + j)), ANY],
        out_specs=pl.BlockSpec((tm, CONV_COLS), lambda j, i: (i, c0 + j)),
        out_shape=jax.ShapeDtypeStruct(dp.shape, dp.dtype), input_output_aliases={3: 0},
        scratch_shapes=[pltpu.VMEM((tm + HALO, CONV_COLS), F32)], compiler_params=_cparams("parallel", "parallel"),
    )(dacc, dacc, w, dp)


GROUP_COLS = SSD_INNER // SSD_GROUPS
PAIRS = GROUP_COLS // 128
HEADS_PER_GROUP = SSD_HEADS // SSD_GROUPS


def _dt_fwd(dt_raw, bias, name):
    t, n = dt_raw.shape
    tm = _row_tile(t)

    def body(x_ref, b_ref, o_ref):
        v = x_ref[...] + b_ref[...]
        o_ref[...] = jnp.maximum(v, 0.0) + jnp.log1p(jnp.exp(-jnp.abs(v)))

    row = pl.BlockSpec((tm, n), lambda i: (i, 0))
    vec = pl.BlockSpec((1, n), lambda i: (0, 0))
    return pl.pallas_call(body, name=name, grid=(t // tm,), in_specs=[row, vec], out_specs=row,
                          out_shape=jax.ShapeDtypeStruct((t, n), F32), compiler_params=_cparams("parallel"))(dt_raw, bias)


def _dt_bwd(ddt, dt_raw, bias, name):
    t, n = dt_raw.shape
    tm = _row_tile(t)

    def body(d_ref, x_ref, b_ref, o_ref, db_ref):
        @pl.when(pl.program_id(0) == 0)
        def _():
            db_ref[...] = jnp.zeros_like(db_ref)

        dr = d_ref[...] * _sigmoid(x_ref[...] + b_ref[...])
        o_ref[...] = dr.astype(o_ref.dtype)
        db_ref[...] += jnp.sum(dr, axis=0, keepdims=True)

    row = pl.BlockSpec((tm, n), lambda i: (i, 0))
    vec = pl.BlockSpec((1, n), lambda i: (0, 0))
    return pl.pallas_call(body, name=name, grid=(t // tm,), in_specs=[row, row, vec], out_specs=[row, vec],
                          out_shape=[jax.ShapeDtypeStruct((t, n), BF16), jax.ShapeDtypeStruct((1, n), F32)],
                          compiler_params=_cparams("arbitrary"))(ddt, dt_raw, bias)


SSD_STEP = 4


def _ssd_common(dt, dtt, a_log_w, a_log_t):
    l = CHUNK
    a = -jnp.exp(a_log_w)
    at = -jnp.exp(a_log_t)
    rowi = lax.broadcasted_iota(jnp.int32, (l, l), 0)
    coli = lax.broadcasted_iota(jnp.int32, (l, l), 1)
    tri = rowi >= coli
    lower = tri.astype(F32)
    upper = (rowi <= coli).astype(F32)
    acs = jnp.dot(lower, dt * a, precision=HI, preferred_element_type=F32)
    acst = jnp.dot(dtt * at, upper, precision=HI, preferred_element_type=F32)
    return a, acs, acst, tri, upper


def _pair_bc(w, lo, p):
    return jnp.where(lo, w[:, 2 * p:2 * p + 1], w[:, 2 * p + 1:2 * p + 2])


def _ssd_specs(t):
    rows = SSD_STEP * CHUNK
    assert t % rows == 0
    return t // rows, dict(
        xs=lambda cm: pl.BlockSpec((rows, GROUP_COLS), lambda g, c: (cm(c), g)),
        bm=lambda cm: pl.BlockSpec((rows, SSD_STATE), lambda g, c: (cm(c), SSD_INNER // SSD_STATE + g)),
        cmat=lambda cm: pl.BlockSpec((rows, SSD_STATE), lambda g, c: (cm(c), SSD_INNER // SSD_STATE + SSD_GROUPS + g)),
        dtw=lambda cm: pl.BlockSpec((1, rows, 128), lambda g, c: (g, cm(c), 0)),
        dtt=lambda cm: pl.BlockSpec((1, HEADS_PER_GROUP, rows), lambda g, c: (g, 0, cm(c))),
        wide=lambda cm: pl.BlockSpec((1, 1, 128), lambda g, c: (g, 0, 0)),
        tall=lambda cm: pl.BlockSpec((1, HEADS_PER_GROUP, 1), lambda g, c: (g, 0, 0)),
        grp=lambda cm: pl.BlockSpec((rows, GROUP_COLS), lambda g, c: (cm(c), g)),
        zp=lambda cm: pl.BlockSpec((rows, GROUP_COLS), lambda g, c: (cm(c), P_Z // GROUP_COLS + g)),
        vec=lambda cm: pl.BlockSpec((1, GROUP_COLS), lambda g, c: (0, g)),
        state=lambda cm: pl.BlockSpec((1, SSD_STEP, PAIRS, SSD_STATE, 128), lambda g, c: (g, cm(c), 0, 0, 0)),
    )


def _ssd_fwd(xc, p, hv, norm_g, name):
    t = xc.shape[0]
    nc, sp = _ssd_specs(t)
    ident = lambda c: c

    def body(xs_ref, b_ref, c_ref, dtw_ref, dtt_ref, aw_ref, at_ref, dk_ref, z_ref, ng_ref,
             y_ref, ys_ref, h_ref, st_ref):
        @pl.when(pl.program_id(1) == 0)
        def _():
            st_ref[...] = jnp.zeros_like(st_ref)

        lo = lax.broadcasted_iota(jnp.int32, (1, 128), 1) < 64
        dskip = dk_ref[0]
        for s in range(SSD_STEP):
            rows = slice(s * CHUNK, (s + 1) * CHUNK)
            dt = dtw_ref[0, rows, :]
            a, acs, acst, tri, _ = _ssd_common(dt, dtt_ref[0, :, rows], aw_ref[0], at_ref[0])
            ecs = jnp.exp(acs)
            alast = acs[CHUNK - 1:CHUNK, :]
            bmat, cmat = b_ref[rows, :], c_ref[rows, :]
            cb = lax.dot_general(cmat, bmat, _DIMS["nt"], preferred_element_type=F32)
            for pi in range(PAIRS):
                cols = slice(pi * 128, (pi + 1) * 128)
                x = xs_ref[rows, cols].astype(F32)
                xdt = x * _pair_bc(dt, lo, pi)
                ydiag = jnp.zeros((CHUNK, 128), F32)
                for r, mask in ((2 * pi, lo), (2 * pi + 1, jnp.logical_not(lo))):
                    lam = jnp.exp(jnp.where(tri, acs[:, r:r + 1] - acst[r:r + 1, :], -1e30))
                    m = (cb * lam).astype(BF16)
                    ydiag = ydiag + jnp.dot(m, jnp.where(mask, xdt, 0.0).astype(BF16), preferred_element_type=F32)
                ht = st_ref[pi]
                h_ref[0, s, pi] = ht
                yoff = jnp.dot(cmat, ht.astype(BF16), preferred_element_type=F32) * _pair_bc(ecs, lo, pi)
                y_ref[rows, cols] = (ydiag + yoff + _pair_bc(dskip, lo, pi) * x).astype(y_ref.dtype)
                alp = _pair_bc(alast, lo, pi)
                e = jnp.exp(alp - _pair_bc(acs, lo, pi))
                st = lax.dot_general(bmat, (xdt * e).astype(BF16), _DIMS["tn"], preferred_element_type=F32)
                st_ref[pi] = ht * jnp.exp(alp) + st
            zf = z_ref[rows, :].astype(F32)
            yg = y_ref[rows, :].astype(F32) * (zf * _sigmoid(zf))
            rstd = lax.rsqrt(jnp.mean(yg * yg, axis=-1, keepdims=True) + EPS)
            ys_ref[rows, :] = (yg * rstd * ng_ref[...]).astype(ys_ref.dtype)

    ins = ["xs", "bm", "cmat", "dtw", "dtt", "wide", "tall", "wide", "zp", "vec"]
    return pl.pallas_call(
        body, name=name, grid=(SSD_GROUPS, nc),
        in_specs=[sp[k](ident) for k in ins],
        out_specs=[sp["grp"](ident), sp["grp"](ident), sp["state"](ident)],
        out_shape=[jax.ShapeDtypeStruct((t, SSD_INNER), BF16), jax.ShapeDtypeStruct((t, SSD_INNER), BF16),
                   jax.ShapeDtypeStruct((SSD_GROUPS, t // CHUNK, PAIRS, SSD_STATE, 128), F32)],
        scratch_shapes=[pltpu.VMEM((PAIRS, SSD_STATE, 128), F32)], compiler_params=_cparams("parallel", "arbitrary"),
    )(xc, xc, xc, hv["dtw"], hv["dtt"], hv["alog_w"], hv["alog_t"], hv["dskip_w"], p, norm_g)


def _ssd_bwd(xc, p, hv, norm_g, y, dys, states, dp, name):
    t = xc.shape[0]
    nc, sp = _ssd_specs(t)
    rev = lambda c: nc - 1 - c

    def body(xs_ref, b_ref, c_ref, dtw_ref, dtt_ref, aw_ref, at_ref, dk_ref, z_ref, ng_ref,
             y_ref, dys_ref, h_ref, dp_ref,
             dxs_ref, db_ref, dc_ref, dz_ref, ddt_ref, hsum_ref, dng_ref, dst_ref):
        del dp_ref
        @pl.when(pl.program_id(1) == 0)
        def _():
            dst_ref[...] = jnp.zeros_like(dst_ref)
            hsum_ref[...] = jnp.zeros_like(hsum_ref)
            dng_ref[...] = jnp.zeros_like(dng_ref)

        lane = lax.broadcasted_iota(jnp.int32, (1, 128), 1)
        lo = lane < 64
        dskip = dk_ref[0]
        sel_r = lax.broadcasted_iota(jnp.int32, (128, 128), 0)
        sel_c = lax.broadcasted_iota(jnp.int32, (128, 128), 1)
        refs = (xs_ref, b_ref, c_ref, dtw_ref, dtt_ref, aw_ref, at_ref, dk_ref, z_ref, ng_ref, y_ref, dys_ref, h_ref,
                dxs_ref, db_ref, dc_ref, dz_ref, ddt_ref, hsum_ref, dng_ref, dst_ref)
        for s in reversed(range(SSD_STEP)):
            chunk_bwd(refs, slice(s * CHUNK, (s + 1) * CHUNK), s, lane, lo, dskip, sel_r, sel_c)

    def chunk_bwd(refs, rows, s, lane, lo, dskip, sel_r, sel_c):
        (xs_ref, b_ref, c_ref, dtw_ref, dtt_ref, aw_ref, at_ref, dk_ref, z_ref, ng_ref, y_ref, dys_ref, h_ref,
         dxs_ref, db_ref, dc_ref, dz_ref, ddt_ref, hsum_ref, dng_ref, dst_ref) = refs
        dt = dtw_ref[0, rows, :]
        a, acs, acst, tri, upper = _ssd_common(dt, dtt_ref[0, :, rows], aw_ref[0], at_ref[0])
        ecs = jnp.exp(acs)
        alast = acs[CHUNK - 1:CHUNK, :]
        bmat, cmat = b_ref[rows, :], c_ref[rows, :]
        cb = lax.dot_general(cmat, bmat, _DIMS["nt"], preferred_element_type=F32)

        zf = z_ref[rows, :].astype(F32)
        sg = _sigmoid(zf)
        sz = zf * sg
        yv = y_ref[rows, :].astype(F32)
        yg = yv * sz
        rstd = lax.rsqrt(jnp.mean(yg * yg, axis=-1, keepdims=True) + EPS)
        yhat = yg * rstd
        dysv = dys_ref[rows, :].astype(F32)
        dng_ref[...] += jnp.sum(dysv * yhat, axis=0, keepdims=True)
        dyh = dysv * ng_ref[...]
        dyg = rstd * (dyh - yhat * jnp.mean(dyh * yhat, axis=-1, keepdims=True))
        dz_ref[rows, :] = (dyg * yv * (sg * (1.0 + zf * (1.0 - sg)))).astype(dz_ref.dtype)
        dy_all = dyg * sz

        dal = jnp.zeros((CHUNK, 128), F32)
        ddtm = jnp.zeros((CHUNK, 128), F32)
        dalast = jnp.zeros((8, 128), F32)
        ddsk = jnp.zeros((8, 128), F32)
        dcb = jnp.zeros((CHUNK, CHUNK), F32)
        qcol = jnp.zeros((8, CHUNK), F32)
        sub8 = lax.broadcasted_iota(jnp.int32, (8, CHUNK), 0)
        dc_acc = jnp.zeros((CHUNK, SSD_STATE), F32)
        db_acc = jnp.zeros((CHUNK, SSD_STATE), F32)
        for pi in range(PAIRS):
            sel = (sel_c == 2 * pi + (sel_r >= 64).astype(jnp.int32)).astype(BF16)

            def hsum(v, sel=sel):
                return jnp.dot(v.astype(BF16), sel, preferred_element_type=F32)

            dyp = dy_all[:, pi * 128:(pi + 1) * 128]
            x = xs_ref[rows, pi * 128:(pi + 1) * 128].astype(F32)
            dtp = _pair_bc(dt, lo, pi)
            xdt = x * dtp
            dxdt = jnp.zeros((CHUNK, 128), F32)
            for r, mask in ((2 * pi, lo), (2 * pi + 1, jnp.logical_not(lo))):
                lam = jnp.exp(jnp.where(tri, acs[:, r:r + 1] - acst[r:r + 1, :], -1e30))
                m32 = cb * lam
                m = m32.astype(BF16)
                dyr = jnp.where(mask, dyp, 0.0).astype(BF16)
                xr = jnp.where(mask, xdt, 0.0).astype(BF16)
                dm = lax.dot_general(dyr, xr, _DIMS["nt"], preferred_element_type=F32)
                dcb = dcb + dm * lam
                q = dm * m32
                dal = dal + jnp.sum(q, axis=1, keepdims=True) * (lane == r).astype(F32)
                qcol = qcol + jnp.where(sub8 == r, jnp.sum(q, axis=0, keepdims=True), 0.0)
                dxdt = dxdt + lax.dot_general(m, dyr, _DIMS["tn"], preferred_element_type=F32)
            ht = h_ref[0, s, pi]
            htb = ht.astype(BF16)
            ecp = _pair_bc(ecs, lo, pi)
            yoff = jnp.dot(cmat, htb, preferred_element_type=F32) * ecp
            dg = (dyp * ecp).astype(BF16)
            dc_acc = dc_acc + lax.dot_general(dg, htb, _DIMS["nt"], preferred_element_type=F32)
            dht = lax.dot_general(cmat, dg, _DIMS["tn"], preferred_element_type=F32)
            dal = dal + hsum(dyp * yoff)
            dhn = dst_ref[pi]
            dhnb = dhn.astype(BF16)
            alp = _pair_bc(alast, lo, pi)
            e = jnp.exp(alp - _pair_bc(acs, lo, pi))
            xe = xdt * e
            db_acc = db_acc + lax.dot_general(xe.astype(BF16), dhnb, _DIMS["nt"], preferred_element_type=F32)
            dxe = jnp.dot(bmat, dhnb, preferred_element_type=F32)
            dxdt = dxdt + dxe * e
            tt = hsum(dxe * xe)
            dal = dal - tt
            dec = jnp.exp(alp)
            dalast = dalast + jnp.sum(tt, axis=0, keepdims=True) + hsum(
                jnp.broadcast_to(jnp.sum(dhn * ht, axis=0, keepdims=True) * dec, (8, 128)))
            dst_ref[pi] = dht + dhn * dec
            dxs_ref[rows, pi * 128:(pi + 1) * 128] = (_pair_bc(dskip, lo, pi) * dyp + dxdt * dtp).astype(dxs_ref.dtype)
            ddtm = ddtm + hsum(dxdt * x)
            ddsk = ddsk + hsum(jnp.broadcast_to(jnp.sum(dyp * x, axis=0, keepdims=True), (8, 128)))
        rowi = lax.broadcasted_iota(jnp.int32, (CHUNK, 128), 0)
        qcol_w = lax.dot_general(jnp.concatenate([qcol, jnp.zeros((CHUNK - 8, CHUNK), F32)], axis=0), (sel_r == sel_c).astype(F32),
                                 _DIMS["tn"], precision=HI, preferred_element_type=F32)
        dal = dal - qcol_w + jnp.where(rowi == CHUNK - 1, dalast[0:1, :], 0.0)
        dda = jnp.dot(upper, dal, precision=HI, preferred_element_type=F32)
        ddt_ref[0, rows, :] = ddtm + dda * a
        hsum_ref[0, 1:2, :] += jnp.sum(dda * dt, axis=0, keepdims=True) * a
        hsum_ref[0, 2:3, :] += ddsk[0:1, :]
        dcbb = dcb.astype(BF16)
        dc_ref[rows, :] = (jnp.dot(dcbb, bmat, preferred_element_type=F32) + dc_acc).astype(dc_ref.dtype)
        db_ref[rows, :] = (lax.dot_general(dcbb, cmat, _DIMS["tn"], preferred_element_type=F32) + db_acc).astype(db_ref.dtype)

    ins = ["xs", "bm", "cmat", "dtw", "dtt", "wide", "tall", "wide", "zp", "vec", "grp", "grp", "state"]
    col = lambda: pl.BlockSpec((SSD_STEP * CHUNK, SSD_STATE), lambda g, c: (rev(c), g))
    return pl.pallas_call(
        body, name=name, grid=(SSD_GROUPS, nc),
        in_specs=[sp[k](rev) for k in ins] + [ANY],
        out_specs=[sp["grp"](rev), col(), col(), sp["zp"](rev), sp["dtw"](rev),
                   pl.BlockSpec((1, 8, 128), lambda g, c: (g, 0, 0)), sp["vec"](rev)],
        out_shape=[jax.ShapeDtypeStruct((t, SSD_INNER), BF16), jax.ShapeDtypeStruct((t, SSD_GROUPS * SSD_STATE), BF16),
                   jax.ShapeDtypeStruct((t, SSD_GROUPS * SSD_STATE), BF16), jax.ShapeDtypeStruct(dp.shape, dp.dtype),
                   jax.ShapeDtypeStruct((SSD_GROUPS, t, 128), F32), jax.ShapeDtypeStruct((SSD_GROUPS, 8, 128), F32),
                   jax.ShapeDtypeStruct((1, SSD_INNER), F32)],
        input_output_aliases={len(ins): 3},
        scratch_shapes=[pltpu.VMEM((PAIRS, SSD_STATE, 128), F32)], compiler_params=_cparams("parallel", "arbitrary"),
    )(xc, xc, xc, hv["dtw"], hv["dtt"], hv["alog_w"], hv["alog_t"], hv["dskip_w"], p, norm_g, y, dys, states, dp)


def _wide(v):
    return jnp.pad(v.reshape(SSD_GROUPS, 1, HEADS_PER_GROUP), ((0, 0), (0, 0), (0, 128 - HEADS_PER_GROUP)))


def _head_inputs(dt, a_log, d_skip):
    t = dt.shape[0]
    g = dt[:, :SSD_HEADS].reshape(t, SSD_GROUPS, HEADS_PER_GROUP)
    return dict(
        dtw=jnp.pad(jnp.transpose(g, (1, 0, 2)), ((0, 0), (0, 0), (0, 128 - HEADS_PER_GROUP))),
        dtt=jnp.transpose(g, (1, 2, 0)),
        alog_w=_wide(a_log), alog_t=a_log.reshape(SSD_GROUPS, HEADS_PER_GROUP, 1),
        dskip_w=_wide(d_skip),
    )


def _gelu(x):
    return 0.5 * x * (1.0 + lax.erf(x * (1.0 / math.sqrt(2.0))))


def _gelu_grad(x):
    return 0.5 * (1.0 + lax.erf(x * (1.0 / math.sqrt(2.0)))) + x * jnp.exp(-0.5 * x * x) * (1.0 / math.sqrt(2.0 * math.pi))


def _tril_mask():
    r = lax.broadcasted_iota(jnp.int32, (CHUNK, CHUNK), 0)
    c = lax.broadcasted_iota(jnp.int32, (CHUNK, CHUNK), 1)
    return r >= c


def _gmlp_fwd(p, v_gain, w_s, b_col, name):
    t = p.shape[0]
    tm = _row_tile(t)
    u0 = P_UV // GMLP_W

    def body(u_ref, v_ref, gn_ref, ws_ref, bs_ref, o_ref):
        v = _gelu(v_ref[...].astype(F32))
        v = (v * lax.rsqrt(jnp.mean(v * v, axis=-1, keepdims=True) + EPS) * gn_ref[...]).astype(BF16)
        tril = _tril_mask()
        wm = [jnp.where(tril, ws_ref[g], 0.0).astype(BF16) for g in range(GMLP_GROUPS)]
        for k in range(tm // CHUNK):
            rows = slice(k * CHUNK, (k + 1) * CHUNK)
            for g in range(GMLP_GROUPS):
                cols = slice(g * 128, (g + 1) * 128)
                mixed = jnp.dot(wm[g], v[rows, cols], preferred_element_type=F32) + bs_ref[g]
                o_ref[rows, cols] = (_gelu(u_ref[rows, cols].astype(F32)) * mixed).astype(o_ref.dtype)

    return pl.pallas_call(
        body, name=name, grid=(t // tm,),
        in_specs=[pl.BlockSpec((tm, GMLP_W), lambda i: (i, u0)), pl.BlockSpec((tm, GMLP_W), lambda i: (i, u0 + 1)),
                  pl.BlockSpec((1, GMLP_W), lambda i: (0, 0)), pl.BlockSpec((GMLP_GROUPS, CHUNK, CHUNK), lambda i: (0, 0, 0)),
                  pl.BlockSpec((GMLP_GROUPS, CHUNK, 1), lambda i: (0, 0, 0))],
        out_specs=pl.BlockSpec((tm, GMLP_W), lambda i: (i, 0)),
        out_shape=jax.ShapeDtypeStruct((t, GMLP_W), BF16), compiler_params=_cparams("parallel"),
    )(p, p, v_gain, w_s, b_col)


def _gmlp_bwd(p, dy, v_gain, w_s, b_col, dp, name):
    t = p.shape[0]
    tm = _row_tile(t)
    u0 = P_UV // GMLP_W

    def body(u_ref, v_ref, dy_ref, gn_ref, ws_ref, bs_ref, dp_ref, duv_ref, dws_ref, dbs_ref, dgn_ref, dvn_ref):
        del dp_ref

        @pl.when(pl.program_id(0) == 0)
        def _():
            dws_ref[...] = jnp.zeros_like(dws_ref)
            dbs_ref[...] = jnp.zeros_like(dbs_ref)
            dgn_ref[...] = jnp.zeros_like(dgn_ref)

        vraw = v_ref[...].astype(F32)
        va = _gelu(vraw)
        rstd = lax.rsqrt(jnp.mean(va * va, axis=-1, keepdims=True) + EPS)
        vhat = va * rstd
        gain = gn_ref[...]
        vn = (vhat * gain).astype(BF16)
        tril = _tril_mask()
        wm = [jnp.where(tril, ws_ref[g], 0.0).astype(BF16) for g in range(GMLP_GROUPS)]
        for k in range(tm // CHUNK):
            rows = slice(k * CHUNK, (k + 1) * CHUNK)
            for g in range(GMLP_GROUPS):
                cols = slice(g * 128, (g + 1) * 128)
                uraw = u_ref[rows, cols].astype(F32)
                vb = vn[rows, cols]
                mixed = jnp.dot(wm[g], vb, preferred_element_type=F32) + bs_ref[g]
                dyb = dy_ref[rows, cols].astype(F32)
                duv_ref[rows, cols] = (dyb * mixed * _gelu_grad(uraw)).astype(duv_ref.dtype)
                dmix = dyb * _gelu(uraw)
                dmb = dmix.astype(BF16)
                dws_ref[g] += jnp.where(tril, lax.dot_general(dmb, vb, _DIMS["nt"], preferred_element_type=F32), 0.0)
                dbs_ref[g] += jnp.sum(dmix, axis=1, keepdims=True)
                dvn_ref[rows, cols] = lax.dot_general(wm[g], dmb, _DIMS["tn"], preferred_element_type=F32)
        dvn = dvn_ref[...]
        dgn_ref[...] += jnp.sum(dvn * vhat, axis=0, keepdims=True)
        dvh = dvn * gain
        dva = rstd * (dvh - vhat * jnp.mean(dvh * vhat, axis=-1, keepdims=True))
        duv_ref[:, GMLP_W:2 * GMLP_W] = (dva * _gelu_grad(vraw)).astype(duv_ref.dtype)

    return pl.pallas_call(
        body, name=name, grid=(t // tm,),
        in_specs=[pl.BlockSpec((tm, GMLP_W), lambda i: (i, u0)), pl.BlockSpec((tm, GMLP_W), lambda i: (i, u0 + 1)),
                  pl.BlockSpec((tm, GMLP_W), lambda i: (i, 0)),
                  pl.BlockSpec((1, GMLP_W), lambda i: (0, 0)), pl.BlockSpec((GMLP_GROUPS, CHUNK, CHUNK), lambda i: (0, 0, 0)),
                  pl.BlockSpec((GMLP_GROUPS, CHUNK, 1), lambda i: (0, 0, 0)), ANY],
        out_specs=[pl.BlockSpec((tm, 2 * GMLP_W), lambda i: (i, P_UV // (2 * GMLP_W))),
                   pl.BlockSpec((GMLP_GROUPS, CHUNK, CHUNK), lambda i: (0, 0, 0)),
                   pl.BlockSpec((GMLP_GROUPS, CHUNK, 1), lambda i: (0, 0, 0)), pl.BlockSpec((1, GMLP_W), lambda i: (0, 0))],
        out_shape=[jax.ShapeDtypeStruct(dp.shape, dp.dtype), jax.ShapeDtypeStruct((GMLP_GROUPS, CHUNK, CHUNK), F32),
                   jax.ShapeDtypeStruct((GMLP_GROUPS, CHUNK, 1), F32), jax.ShapeDtypeStruct((1, GMLP_W), F32)],
        input_output_aliases={6: 0},
        scratch_shapes=[pltpu.VMEM((tm, GMLP_W), F32)], compiler_params=_cparams("arbitrary"),
    )(p, p, dy, v_gain, w_s, b_col, dp)


def _head_masks():
    lane = lax.broadcasted_iota(jnp.int32, (1, MEM_W), 1)
    return [(lane >= h * 64) & (lane < (h + 1) * 64) for h in range(MEM_HEADS)]


def _mem_fwd(p, kv, name):
    t = p.shape[0]
    tm = _row_tile(t)
    q0 = P_Q // MEM_W

    def body(q_ref, kv_ref, o_ref):
        q = q_ref[...]
        k = kv_ref[:, 0:MEM_W].astype(BF16)
        v = kv_ref[:, MEM_W:2 * MEM_W].astype(BF16)
        out = jnp.zeros((tm, MEM_W), F32)
        for mask in _head_masks():
            s = lax.dot_general(jnp.where(mask, q, 0), k, _DIMS["nt"], preferred_element_type=F32) * 0.125
            e = jnp.exp(s - jnp.max(s, axis=-1, keepdims=True))
            pr = (e * (1.0 / jnp.sum(e, axis=-1, keepdims=True))).astype(BF16)
            out = out + jnp.where(mask, jnp.dot(pr, v, preferred_element_type=F32), 0.0)
        o_ref[...] = out.astype(o_ref.dtype)

    return pl.pallas_call(
        body, name=name, grid=(t // tm,),
        in_specs=[pl.BlockSpec((tm, MEM_W), lambda i: (i, q0)), pl.BlockSpec((MEM_LEN, 2 * MEM_W), lambda i: (0, 0))],
        out_specs=pl.BlockSpec((tm, MEM_W), lambda i: (i, 0)),
        out_shape=jax.ShapeDtypeStruct((t, MEM_W), BF16), compiler_params=_cparams("parallel"),
    )(p, kv)


def _mem_bwd(p, kv, dy, dp, name):
    t = p.shape[0]
    tm = _row_tile(t)
    q0 = P_Q // MEM_W
    assert P_W - P_Q == 2 * MEM_W

    def body(q_ref, kv_ref, dy_ref, dp_ref, dq_ref, dkv_ref):
        del dp_ref

        @pl.when(pl.program_id(0) == 0)
        def _():
            dkv_ref[...] = jnp.zeros_like(dkv_ref)

        q = q_ref[...]
        dy = dy_ref[...]
        k = kv_ref[:, 0:MEM_W].astype(BF16)
        v = kv_ref[:, MEM_W:2 * MEM_W].astype(BF16)
        dq = jnp.zeros((tm, MEM_W), F32)
        dk = jnp.zeros((MEM_LEN, MEM_W), F32)
        dv = jnp.zeros((MEM_LEN, MEM_W), F32)
        for mask in _head_masks():
            qh = jnp.where(mask, q, 0)
            dyh = jnp.where(mask, dy, 0)
            s = lax.dot_general(qh, k, _DIMS["nt"], preferred_element_type=F32) * 0.125
            e = jnp.exp(s - jnp.max(s, axis=-1, keepdims=True))
            pr = e * (1.0 / jnp.sum(e, axis=-1, keepdims=True))
            prb = pr.astype(BF16)
            dp = lax.dot_general(dyh, v, _DIMS["nt"], preferred_element_type=F32)
            ds = (pr * (dp - jnp.sum(dp * pr, axis=-1, keepdims=True)) * 0.125).astype(BF16)
            dq = dq + jnp.where(mask, jnp.dot(ds, k, preferred_element_type=F32), 0.0)
            dk = dk + lax.dot_general(ds, qh, _DIMS["tn"], preferred_element_type=F32)
            dv = dv + lax.dot_general(prb, dyh, _DIMS["tn"], preferred_element_type=F32)
        dq_ref[:, 0:MEM_W] = dq.astype(dq_ref.dtype)
        dq_ref[:, MEM_W:2 * MEM_W] = jnp.zeros((tm, MEM_W), dq_ref.dtype)
        dkv_ref[:, 0:MEM_W] += dk
        dkv_ref[:, MEM_W:2 * MEM_W] += dv

    return pl.pallas_call(
        body, name=name, grid=(t // tm,),
        in_specs=[pl.BlockSpec((tm, MEM_W), lambda i: (i, q0)), pl.BlockSpec((MEM_LEN, 2 * MEM_W), lambda i: (0, 0)),
                  pl.BlockSpec((tm, MEM_W), lambda i: (i, 0)), ANY],
        out_specs=[pl.BlockSpec((tm, 2 * MEM_W), lambda i: (i, P_Q // (2 * MEM_W))),
                   pl.BlockSpec((MEM_LEN, 2 * MEM_W), lambda i: (0, 0))],
        out_shape=[jax.ShapeDtypeStruct(dp.shape, dp.dtype), jax.ShapeDtypeStruct((MEM_LEN, 2 * MEM_W), F32)],
        input_output_aliases={3: 0}, compiler_params=_cparams("arbitrary"),
    )(p, kv, dy, dp)


def _merge_fwd(p, b_ssd, b_gmlp, b_mem, name):
    t = p.shape[0]
    tm = _row_tile(t)
    g0 = P_GL // D_MODEL

    def body(g1, g2, g3, b1, b2, b3, o_ref):
        def strip(rows, carry):
            acc = _sigmoid(g1[rows, :].astype(F32)) * b1[rows, :].astype(F32)
            acc = acc + _sigmoid(g2[rows, :].astype(F32)) * b2[rows, :].astype(F32)
            acc = acc + _sigmoid(g3[rows, :].astype(F32)) * b3[rows, :].astype(F32)
            o_ref[rows, :] = acc.astype(o_ref.dtype)
            return carry

        _strips(tm, strip, 0)

    row = pl.BlockSpec((tm, D_MODEL), lambda i: (i, 0))
    return pl.pallas_call(
        body, name=name, grid=(t // tm,),
        in_specs=[pl.BlockSpec((tm, D_MODEL), lambda i, k=k: (i, g0 + k)) for k in range(3)] + [row] * 3,
        out_specs=row, out_shape=jax.ShapeDtypeStruct((t, D_MODEL), BF16), compiler_params=_cparams("parallel"),
    )(p, p, p, b_ssd, b_gmlp, b_mem)


def _merge_bwd(p, dm, b_ssd, b_gmlp, b_mem, dp, name):
    t = p.shape[0]
    tm = _row_tile(t)
    g0 = P_GL // D_MODEL

    def body(g1, g2, g3, dm_ref, b1, b2, b3, dp_ref, d1, d2, d3, dgl_ref):
        del dp_ref

        def strip(rows, carry):
            dmv = dm_ref[rows, :].astype(F32)
            for k, (g_ref, b_ref, d_ref) in enumerate(((g1, b1, d1), (g2, b2, d2), (g3, b3, d3))):
                sg = _sigmoid(g_ref[rows, :].astype(F32))
                d_ref[rows, :] = (dmv * sg).astype(d_ref.dtype)
                dgl_ref[rows, k * D_MODEL:(k + 1) * D_MODEL] = (dmv * b_ref[rows, :].astype(F32) * sg * (1.0 - sg)).astype(dgl_ref.dtype)
            return carry

        _strips(tm, strip, 0)

    row = pl.BlockSpec((tm, D_MODEL), lambda i: (i, 0))
    return pl.pallas_call(
        body, name=name, grid=(t // tm,),
        in_specs=[pl.BlockSpec((tm, D_MODEL), lambda i, k=k: (i, g0 + k)) for k in range(3)] + [row] * 4 + [ANY],
        out_specs=[row, row, row, pl.BlockSpec((tm, 3 * D_MODEL), lambda i: (i, P_GL // (3 * D_MODEL)))],
        out_shape=[jax.ShapeDtypeStruct((t, D_MODEL), BF16)] * 3 + [jax.ShapeDtypeStruct(dp.shape, dp.dtype)],
        input_output_aliases={7: 3}, compiler_params=_cparams("parallel"),
    )(p, p, p, dm, b_ssd, b_gmlp, b_mem, dp)


def _local_step(x, mem, target, w, first_weights, mixer_weights, second_weights, push):
    t = x.shape[0]
    mm = functools.partial(_matmul, tk=1024)

    ffn_weights = lambda tag: (w[f"{tag}_w_gate"], w[f"{tag}_w_up"], w[f"{tag}_w_down"])

    def arriving(tag, fetch):
        def weights(n):
            w.update(fetch(n))
            return ffn_weights(tag)
        return weights

    w = dict(w)
    h1, ffn1_saved = _ffn_forward(x, w["ffn1_norm"], arriving("ffn1", first_weights), "ffn1")
    w.update(mixer_weights(h1))
    n2 = _rms_fwd(h1, w["mix_norm"], "mix_norm")
    p = mm(n2, w["w_in_p"], mode="nt", out_dtype=BF16, tm=1024, tn=1536, name="in_proj")
    dt_raw = mm(n2, w["w_dt"], mode="nt", out_dtype=F32, tm=2048, tn=128, name="dt_proj")
    dt_bias = jnp.pad(w["ssd_dt_bias"], (0, 128 - SSD_HEADS)).reshape(1, 128)
    hv = _head_inputs(_dt_fwd(dt_raw, dt_bias, "dt_fwd"), w["ssd_a_log"], w["ssd_d"])
    xc = _conv_fwd(p, w["ssd_conv_w"], w["ssd_conv_b"], "conv_fwd")
    y_ssd_raw, y_ssd, states = _ssd_fwd(xc, p, hv, w["ssd_norm"], "ssd_fwd")
    b_col = w["gmlp_b_s"].reshape(GMLP_GROUPS, CHUNK, 1)
    y_gmlp = _gmlp_fwd(p, w["gmlp_v_norm"], w["gmlp_w_s"], b_col, "gmlp_fwd")
    mem_n = _rms_fwd(mem, w["mem_norm"], "mem_norm")
    kv = mm(mem_n, w["w_mem_kv"], mode="nn", out_dtype=F32, tm=256, tn=512, name="mem_kv")
    y_mem = _mem_fwd(p, kv, "mem_fwd")
    b_ssd = mm(y_ssd, w["w_branch_ssd"], mode="nn", out_dtype=BF16, tm=1024, tn=1024, tk=2048, name="branch_ssd")
    b_gmlp = mm(y_gmlp, w["w_branch_gmlp"], mode="nn", out_dtype=BF16, tm=1024, tn=1024, name="branch_gmlp")
    b_mem = mm(y_mem, w["w_branch_mem"], mode="nt", out_dtype=BF16, tm=2048, tn=1024, tk=MEM_W, name="branch_mem")
    merged = _merge_fwd(p, b_ssd, b_gmlp, b_mem, "merge_fwd")
    h2 = mm(merged, w["w_out"], mode="nn", out_dtype=F32, tm=1024, tn=1024, addend=h1, name="out_proj")
    h3, ffn2_saved = _ffn_forward(h2, w["ffn2_norm"], arriving("ffn2", second_weights), "ffn2")
    dh3, d_final, loss = _loss_head(h3, w["final_norm"], target, "loss_head")

    g = {"final_norm": d_final}
    big = {}
    dh2, g["ffn2_norm"] = _ffn_backward(dh3, h2, w["ffn2_norm"], w["ffn2_w_gate"], w["ffn2_w_up"], w["ffn2_w_down"], ffn2_saved,
                                        "ffn2", functools.partial(push, 0))
    dmerged = mm(dh2, w["w_out"], mode="nt", out_dtype=BF16, tm=1024, tn=1024, name="out_proj_dx")
    big["w_out"] = mm(merged, dh2, mode="tn", out_dtype=BF16, tm=1024, tn=1024, tk=2048, name="out_proj_dw")
    db_ssd, db_gmlp, db_mem, dp = _merge_bwd(p, dmerged, b_ssd, b_gmlp, b_mem, lax.empty((t, P_W), BF16), "merge_bwd")
    dy_ssd = mm(db_ssd, w["w_branch_ssd"], mode="nt", out_dtype=BF16, tm=1024, tn=2048, name="branch_ssd_dx")
    dy_gmlp = mm(db_gmlp, w["w_branch_gmlp"], mode="nt", out_dtype=BF16, tm=1024, tn=1024, name="branch_gmlp_dx")
    dy_mem = mm(db_mem, w["w_branch_mem"], mode="nn", out_dtype=BF16, tm=2048, tn=256, name="branch_mem_dx")
    big["w_branch_ssd"] = mm(y_ssd, db_ssd, mode="tn", out_dtype=BF16, tm=1024, tn=1024, tk=2048, name="branch_ssd_dw")
    big["w_branch_gmlp"] = mm(y_gmlp, db_gmlp, mode="tn", out_dtype=BF16, tm=1024, tn=1024, tk=2048, name="branch_gmlp_dw")
    big["w_branch_mem"] = mm(db_mem, y_mem, mode="tn", out_dtype=BF16, tm=1024, tn=256, tk=2048, name="branch_mem_dw")
    dp, dkv = _mem_bwd(p, kv, dy_mem, dp, "mem_bwd")
    big["w_mem_kv"] = mm(mem_n, dkv, mode="tn", out_dtype=BF16, tm=1024, tn=512, tk=256, name="mem_kv_dw")
    dmem_n = mm(dkv, w["w_mem_kv"], mode="nt", out_dtype=F32, tm=256, tn=1024, tk=512, name="mem_kv_dx")
    _, g["mem_norm"] = _rms_bwd(mem, w["mem_norm"], dmem_n, None, "mem_norm_bwd")
    dp, d_ws, d_bs, g["gmlp_v_norm"] = _gmlp_bwd(p, dy_gmlp, w["gmlp_v_norm"], w["gmlp_w_s"], b_col, dp, "gmlp_bwd")
    g["gmlp_w_s"] = d_ws
    g["gmlp_b_s"] = d_bs.reshape(GMLP_GROUPS, CHUNK)
    dxs, d_bm, d_cm, dp, ddt_w, hsums, g["ssd_norm"] = _ssd_bwd(xc, p, hv, w["ssd_norm"], y_ssd_raw, dy_ssd, states, dp, "ssd_bwd")
    heads = hsums[:, :, :HEADS_PER_GROUP]
    g["ssd_a_log"] = heads[:, 1, :].reshape(1, SSD_HEADS)
    g["ssd_d"] = heads[:, 2, :].reshape(1, SSD_HEADS)
    ddt = jnp.transpose(ddt_w[:, :, :HEADS_PER_GROUP], (1, 0, 2)).reshape(t, SSD_HEADS)
    ddt, d_bias = _dt_bwd(jnp.pad(ddt, ((0, 0), (0, 128 - SSD_HEADS))), dt_raw, dt_bias, "dt_bwd")
    g["ssd_dt_bias"] = d_bias[:, :SSD_HEADS]
    dws, dbs = [], []
    for dyc, col0, tag in ((dxs, 0, "x"), (d_bm, SSD_INNER, "b"), (d_cm, SSD_INNER + SSD_GROUPS * SSD_STATE, "c")):
        dacc, dw_c, db_c = _conv_bwd_act(p, dyc, w["ssd_conv_w"], w["ssd_conv_b"], col0, f"conv_bwd_act_{tag}")
        dp = _conv_bwd_dx(dacc, w["ssd_conv_w"], col0, dp, f"conv_bwd_dx_{tag}")
        dws.append(dw_c)
        dbs.append(db_c)
    g["ssd_conv_w"] = jnp.concatenate(dws, axis=1)
    g["ssd_conv_b"] = jnp.concatenate(dbs, axis=1)
    d_win_p = _matmul(dp, n2, mode="tn", out_dtype=BF16, tm=1536, tn=1024, tk=2048, name="in_proj_dw")
    d_wdt = mm(ddt, n2, mode="tn", out_dtype=BF16, tm=128, tn=1024, tk=2048, name="dt_proj_dw")
    sl = lambda a, o, n: a[o:o + n]
    big["w_in"] = jnp.concatenate([sl(d_win_p, P_Z, 2048), sl(d_win_p, P_XBC, XBC), d_wdt[:SSD_HEADS], sl(d_win_p, P_UV, 2048),
                                   sl(d_win_p, P_Q, MEM_W), sl(d_win_p, P_GL, 3 * D_MODEL)], axis=0)
    token = push(1, big)
    dn2 = mm(dp, w["w_in_p"], mode="nn", out_dtype=F32, tm=1024, tn=1024, tk=3584, name="in_proj_dx")
    dn2 = _matmul(ddt, w["w_dt"], mode="nn", out_dtype=F32, tm=1024, tn=1024, tk=128, addend=dn2, name="dt_proj_dx")
    dh1, g["mix_norm"] = _rms_bwd(h1, w["mix_norm"] + token, dn2, dh2, "mix_norm_bwd")
    dx, g["ffn1_norm"] = _ffn_backward(dh1, x, w["ffn1_norm"], w["ffn1_w_gate"], w["ffn1_w_up"], w["ffn1_w_down"], ffn1_saved,
                                       "ffn1", functools.partial(push, 2))
    return loss, dx, g


def _split_w_in(w_in_t):
    sl = lambda o, n: w_in_t[o:o + n]
    w_p = jnp.concatenate([sl(IN_GL, 3 * D_MODEL), sl(IN_Z, 2048), sl(IN_XBC, XBC), sl(IN_UV, 2048), sl(IN_Q, MEM_W),
                           jnp.zeros((P_W - P_USED, D_MODEL), w_in_t.dtype)], axis=0)
    w_dt = jnp.pad(sl(IN_DT, SSD_HEADS), ((0, 128 - SSD_HEADS), (0, 0)))
    return w_p, w_dt


def _pick_tile(rows, cap=512):
    best = None
    for tile in range(8, min(rows, cap) + 1, 8):
        if rows % tile == 0:
            best = tile
    return best if best is not None else rows


def _adamw(w, g, m, v, name):
    rows, lanes = w.shape
    tile = _pick_tile(rows, cap=max(8, (512 * 1024 // lanes) // 8 * 8))
    c1 = 1.0 / (1.0 - ADAM_B1 ** ADAM_STEP)
    c2 = 1.0 / (1.0 - ADAM_B2 ** ADAM_STEP)

    def body(w_ref, g_ref, m_ref, v_ref, d_ref, nm_ref, nv_ref):
        gv = g_ref[...]
        nm = ADAM_B1 * m_ref[...] + (1.0 - ADAM_B1) * gv
        nv = ADAM_B2 * v_ref[...] + (1.0 - ADAM_B2) * (gv * gv)
        nm_ref[...] = nm
        nv_ref[...] = nv
        d_ref[...] = -ADAM_LR * ((nm * c1) / (jnp.sqrt(nv * c2) + ADAM_EPS) + ADAM_WD * w_ref[...])

    blk = pl.BlockSpec((tile, lanes), lambda i: (i, 0))
    return pl.pallas_call(
        body, name=name, grid=(rows // tile,), in_specs=[blk] * 4, out_specs=[blk] * 3,
        out_shape=[jax.ShapeDtypeStruct((rows, lanes), F32)] * 3, compiler_params=_cparams("parallel"),
    )(w, g, m, v)


HBM = pl.BlockSpec(memory_space=pltpu.HBM)


def _place():
    x, y, c = lax.axis_index("x"), lax.axis_index("y"), lax.axis_index("c")
    chips = [(1 - x, y), (x, 1 - y), (1 - x, 1 - y)]
    return x, y, c, chips


SEM = pl.BlockSpec(memory_space=pltpu.SEMAPHORE)
EFFECT = pltpu.SideEffectType.DATAFLOW_SIDE_EFFECTING
N_PEER = 3


def _sem_outs():
    return tuple(pltpu.SemaphoreType.DMA(()) for _ in range(2 * N_PEER))


def _gather_start(slots, tag):
    def body(in_ref, *refs):
        del in_ref
        sems, thru, token = refs[:2 * N_PEER], refs[2 * N_PEER], refs[2 * N_PEER + 1]
        x, y, c, chips = _place()
        own = thru.at[2 * x + y, c]
        for j, chip in enumerate(chips):
            pltpu.make_async_remote_copy(src_ref=own, dst_ref=own, send_sem=sems[j], recv_sem=sems[N_PEER + j],
                                         device_id=(*chip, c), device_id_type=MESH).start()
        token[...] = jnp.zeros_like(token)

    out = pl.pallas_call(
        body, name=f"gather_{tag}_start",
        out_shape=_sem_outs() + (pltpu.HBM(slots.shape, slots.dtype), jax.ShapeDtypeStruct((8, 128), F32)),
        in_specs=(HBM,), out_specs=(SEM,) * (2 * N_PEER) + (HBM, pl.BlockSpec(memory_space=pltpu.VMEM)),
        input_output_aliases={0: 2 * N_PEER}, compiler_params=pltpu.CompilerParams(has_side_effects=EFFECT),
    )(pltpu.with_memory_space_constraint(slots, pltpu.HBM))
    return out[:2 * N_PEER], out[2 * N_PEER], out[2 * N_PEER + 1]


def _gather_wait(sems, thru, after, tag):
    def body(in_ref, *refs):
        del in_ref
        sems, out_ref = refs[:2 * N_PEER], refs[2 * N_PEER + 1]
        x, y, c, chips = _place()
        own = out_ref.at[2 * x + y, c]
        for j, (cx, cy) in enumerate(chips):
            cp = pltpu.make_async_remote_copy(src_ref=own, dst_ref=out_ref.at[2 * cx + cy, c], send_sem=sems[j],
                                              recv_sem=sems[N_PEER + j], device_id=(cx, cy, c), device_id_type=MESH)
            cp.wait_send()
            cp.wait_recv()

    return pl.pallas_call(
        body, name=f"gather_{tag}_wait", out_shape=pltpu.HBM(thru.shape, thru.dtype),
        in_specs=(HBM,) + (SEM,) * (2 * N_PEER) + (pl.BlockSpec(memory_space=pl.ANY),), out_specs=HBM,
        input_output_aliases={0: 0}, compiler_params=pltpu.CompilerParams(has_side_effects=EFFECT),
    )(thru, *sems, after)


def _gather_forward(slots, tag):
    def body(in_ref, out_ref, send_sems, recv_sems):
        del in_ref
        x, y, c, chips = _place()
        cps = []
        for j, (cx, cy) in enumerate(chips):
            landed = out_ref.at[2 * cx + cy, c]
            cps.append(pltpu.make_async_remote_copy(src_ref=landed, dst_ref=landed, send_sem=send_sems.at[j], recv_sem=recv_sems.at[j],
                                                    device_id=(x, y, 1 - c), device_id_type=MESH))
        for cp in cps:
            cp.start()
        for j, (cx, cy) in enumerate(chips):
            other = out_ref.at[2 * cx + cy, 1 - c]
            pltpu.make_async_remote_copy(src_ref=other, dst_ref=other, send_sem=send_sems.at[j], recv_sem=recv_sems.at[j],
                                         device_id=(x, y, 1 - c), device_id_type=MESH).wait_recv()
        for cp in cps:
            cp.wait_send()

    return pl.pallas_call(
        body, name=f"gather_{tag}_forward", out_shape=jax.ShapeDtypeStruct(slots.shape, slots.dtype),
        in_specs=[HBM], out_specs=HBM, input_output_aliases={0: 0},
        scratch_shapes=[pltpu.SemaphoreType.DMA((N_PEER,)), pltpu.SemaphoreType.DMA((N_PEER,))],
    )(slots)


def _scatter_start(pa, tag):
    ns, rh, lanes = pa.shape
    land = pltpu.with_memory_space_constraint(lax.empty((N_PEER, rh, lanes), pa.dtype), pltpu.HBM)

    def body(pa_ref, land_ref, *refs):
        x, y, c, chips = _place()
        for j, (cx, cy) in enumerate(chips):
            pltpu.make_async_remote_copy(src_ref=pa_ref.at[2 * cx + cy], dst_ref=land_ref.at[j], send_sem=refs[j],
                                         recv_sem=refs[N_PEER + j], device_id=(cx, cy, c), device_id_type=MESH).start()
        refs[-1][...] = jnp.zeros_like(refs[-1])

    out = pl.pallas_call(
        body, name=f"scatter_start_{tag}",
        out_shape=_sem_outs() + (pltpu.HBM(pa.shape, pa.dtype), pltpu.HBM(land.shape, land.dtype), jax.ShapeDtypeStruct((8, 128), F32)),
        in_specs=(HBM, HBM), out_specs=(SEM,) * (2 * N_PEER) + (HBM, HBM, pl.BlockSpec(memory_space=pltpu.VMEM)),
        input_output_aliases={0: 2 * N_PEER, 1: 2 * N_PEER + 1}, compiler_params=pltpu.CompilerParams(has_side_effects=EFFECT),
    )(pltpu.with_memory_space_constraint(pa, pltpu.HBM), land)
    return (out[:2 * N_PEER], out[2 * N_PEER], out[2 * N_PEER + 1]), out[2 * N_PEER + 2]


def _scatter_wait(sems, pa_thru, land_thru, after, tag):
    def body(pa_ref, land_ref, *refs):
        sems = refs[:2 * N_PEER]
        x, y, c, chips = _place()
        for j, (cx, cy) in enumerate(chips):
            cp = pltpu.make_async_remote_copy(src_ref=pa_ref.at[2 * cx + cy], dst_ref=land_ref.at[j], send_sem=sems[j],
                                              recv_sem=sems[N_PEER + j], device_id=(cx, cy, c), device_id_type=MESH)
            cp.wait_send()
            cp.wait_recv()

    return pl.pallas_call(
        body, name=f"scatter_wait_{tag}",
        out_shape=(pltpu.HBM(pa_thru.shape, pa_thru.dtype), pltpu.HBM(land_thru.shape, land_thru.dtype)),
        in_specs=(HBM, HBM) + (SEM,) * (2 * N_PEER) + (pl.BlockSpec(memory_space=pl.ANY),), out_specs=(HBM, HBM),
        input_output_aliases={0: 0, 1: 1}, compiler_params=pltpu.CompilerParams(has_side_effects=EFFECT),
    )(pa_thru, land_thru, *sems, after)


def _rs_swap(gp, tag):
    ns, _, rh, lanes = gp.shape

    def body(in_ref, out_ref, send_sem, recv_sem):
        x, y, c, _ = _place()
        cp = pltpu.make_async_remote_copy(src_ref=in_ref.at[:, 1 - c], dst_ref=out_ref, send_sem=send_sem, recv_sem=recv_sem,
                                          device_id=(x, y, 1 - c), device_id_type=MESH)
        cp.start()
        cp.wait_send()
        cp.wait_recv()

    return pl.pallas_call(
        body, name=f"rs_swap_{tag}", out_shape=jax.ShapeDtypeStruct((ns, rh, lanes), gp.dtype), in_specs=[HBM], out_specs=HBM,
        scratch_shapes=[pltpu.SemaphoreType.DMA, pltpu.SemaphoreType.DMA],
    )(gp)


def _rs_tile(rh):
    return _pick_tile(rh, cap=512)


def _rs_add(gp, recv, c, tag):
    ns, _, rh, lanes = gp.shape
    tile = _rs_tile(rh)

    def body(c_ref, a_ref, b_ref, o_ref):
        o_ref[...] = (a_ref[...].astype(F32) + b_ref[...].astype(F32)).astype(o_ref.dtype)

    return pl.pallas_call(
        body, name=f"rs_add_{tag}", out_shape=jax.ShapeDtypeStruct((ns, rh, lanes), gp.dtype),
        grid_spec=pltpu.PrefetchScalarGridSpec(
            num_scalar_prefetch=1, grid=(ns, rh // tile),
            in_specs=[pl.BlockSpec((None, None, tile, lanes), lambda s, i, c_ref: (s, c_ref[0], i, 0)),
                      pl.BlockSpec((None, tile, lanes), lambda s, i, c_ref: (s, i, 0))],
            out_specs=pl.BlockSpec((None, tile, lanes), lambda s, i, c_ref: (s, i, 0))),
        compiler_params=_cparams("parallel", "parallel"),
    )(c, gp, recv)


def _rs_sum(pa, recv, place, tag):
    ns, rh, lanes = pa.shape
    tile = _rs_tile(rh)

    def body(place_ref, a_ref, r_ref, o_ref):
        acc = a_ref[...].astype(F32)
        for j in range(ns - 1):
            acc = acc + r_ref[j].astype(F32)
        o_ref[...] = acc

    return pl.pallas_call(
        body, name=f"rs_sum_{tag}", out_shape=jax.ShapeDtypeStruct((2, rh, lanes), F32),
        grid_spec=pltpu.PrefetchScalarGridSpec(
            num_scalar_prefetch=1, grid=(rh // tile,),
            in_specs=[pl.BlockSpec((None, tile, lanes), lambda i, place_ref: (place_ref[0], i, 0)),
                      pl.BlockSpec((ns - 1, tile, lanes), lambda i, place_ref: (0, i, 0))],
            out_specs=pl.BlockSpec((None, tile, lanes), lambda i, place_ref: (place_ref[1], i, 0))),
        compiler_params=_cparams("parallel"),
    )(place, pa, recv)


def _rs_share(halves, tag):
    def body(in_ref, out_ref, send_sem, recv_sem):
        del in_ref
        x, y, c, _ = _place()
        cp = pltpu.make_async_remote_copy(src_ref=out_ref.at[c], dst_ref=out_ref.at[c], send_sem=send_sem, recv_sem=recv_sem,
                                          device_id=(x, y, 1 - c), device_id_type=MESH)
        cp.start()
        other = out_ref.at[1 - c]
        pltpu.make_async_remote_copy(src_ref=other, dst_ref=other, send_sem=send_sem, recv_sem=recv_sem,
                                     device_id=(x, y, 1 - c), device_id_type=MESH).wait_recv()
        cp.wait_send()

    return pl.pallas_call(
        body, name=f"rs_share_{tag}", out_shape=jax.ShapeDtypeStruct(halves.shape, halves.dtype), in_specs=[HBM], out_specs=HBM,
        input_output_aliases={0: 0}, scratch_shapes=[pltpu.SemaphoreType.DMA, pltpu.SemaphoreType.DMA],
    )(halves)


N_DEV = 8
SMALL_ROWS = 160


def _allreduce_small(v):
    m_per, n = v.shape

    def body(x_ref, out_ref, all_ref, send_sems, recv_sems, local_sem):
        x, y, c, chips = _place()
        me, sibling = (x, y, c), (x, y, 1 - c)

        def rows(px, py, pc):
            return all_ref.at[pl.ds((4 * px + 2 * py + pc) * m_per, m_per), :]

        def copy(k, block, to, src=None):
            return pltpu.make_async_remote_copy(src_ref=rows(*block) if src is None else src, dst_ref=rows(*block),
                                                send_sem=send_sems.at[k], recv_sem=recv_sems.at[k], device_id=to, device_id_type=MESH)

        mine = pltpu.make_async_copy(x_ref, rows(*me), local_sem)
        mine.start()
        first = [copy(0, me, sibling, src=x_ref)]
        first += [copy(1 + j, me, (*chip, c), src=x_ref) for j, chip in enumerate(chips)]
        for cp in first:
            cp.start()
        passed = [copy(4 + j, (*chip, c), sibling) for j, chip in enumerate(chips)]
        for j, chip in enumerate(chips):
            copy(1 + j, (*chip, c), me).wait_recv()
            passed[j].start()
        copy(0, sibling, me).wait_recv()
        for j, chip in enumerate(chips):
            copy(4 + j, (*chip, 1 - c), me).wait_recv()
        for cp in first + passed:
            cp.wait_send()
        mine.wait()
        step = 32
        for r in range(0, m_per, step):
            acc = all_ref[r:r + step, :]
            for d in range(1, N_DEV):
                acc = acc + all_ref[d * m_per + r:d * m_per + r + step, :]
            out_ref[r:r + step, :] = acc

    vm = pl.BlockSpec(memory_space=pltpu.VMEM)
    return pl.pallas_call(
        body, name="allreduce_small", out_shape=jax.ShapeDtypeStruct((m_per, n), v.dtype), in_specs=[vm], out_specs=vm,
        scratch_shapes=[pltpu.VMEM((N_DEV * m_per, n), v.dtype), pltpu.SemaphoreType.DMA((7,)), pltpu.SemaphoreType.DMA((7,)),
                        pltpu.SemaphoreType.DMA],
        compiler_params=pltpu.CompilerParams(vmem_limit_bytes=V7X_VMEM_LIMIT),
    )(v)


BIG = {"ffn1_w_gate": ((D_MODEL, D_FF), 1), "ffn1_w_up": ((D_MODEL, D_FF), 1), "ffn1_w_down": ((D_FF, D_MODEL), 0),
       "ffn2_w_gate": ((D_MODEL, D_FF), 1), "ffn2_w_up": ((D_MODEL, D_FF), 1), "ffn2_w_down": ((D_FF, D_MODEL), 0),
       "w_in": ((D_MODEL, IN_WIDTH), 1), "w_mem_kv": ((D_MODEL, 2 * MEM_W), 0), "w_branch_ssd": ((SSD_INNER, D_MODEL), 0),
       "w_branch_gmlp": ((GMLP_W, D_MODEL), 0), "w_branch_mem": ((MEM_W, D_MODEL), 1), "w_out": ((D_MODEL, D_MODEL), 0)}
FFN1 = ("ffn1_w_gate", "ffn1_w_up", "ffn1_w_down")
FFN2 = ("ffn2_w_gate", "ffn2_w_up", "ffn2_w_down")
MIXER = ("w_out", "w_branch_ssd", "w_branch_gmlp", "w_branch_mem", "w_mem_kv", "w_in")
REDUCE_GROUPS = (FFN2, MIXER, FFN1)
CONV_W_ROWS = 8


def _shard_rows_of(name):
    (a, b), _ = BIG[name]
    return a * b // N_SHARD // LANES


def _group_rows(names, extra=0):
    return -(-(sum(_shard_rows_of(n) for n in names) + extra) // 32) * 32

SMALL = [("ffn1_norm", 1), ("mix_norm", 1), ("mem_norm", 1), ("ssd_conv_b", 3), ("heads", 1), ("ssd_norm", 2),
         ("gmlp_v_norm", 1), ("gmlp_w_s", 128), ("gmlp_b_s", 1), ("ffn2_norm", 1), ("final_norm", 1), ("ssd_conv_w", 12)]
assert sum(n for _, n in SMALL) <= SMALL_ROWS
HEAD_VECS = ("ssd_dt_bias", "ssd_a_log", "ssd_d")


def _pack_small(vals, loss=None):
    parts = []
    for name, nrows in SMALL:
        if name == "heads":
            row = jnp.concatenate([vals[k].reshape(-1) for k in HEAD_VECS]
                                  + [jnp.zeros((1,), F32) if loss is None else loss.reshape(1)])
            parts.append(jnp.pad(row, (0, LANES - row.shape[0])).reshape(1, LANES))
        elif name in vals:
            parts.append(vals[name].reshape(nrows, LANES))
        else:
            parts.append(jnp.zeros((nrows, LANES), F32))
    buf = jnp.concatenate(parts, axis=0)
    return jnp.pad(buf, ((0, SMALL_ROWS - buf.shape[0]), (0, 0)))


def _unpack_small(buf):
    out, r = {}, 0
    for name, nrows in SMALL:
        blk = buf[r:r + nrows]
        r += nrows
        if name == "heads":
            for i, k in enumerate(HEAD_VECS):
                out[k] = blk[0, i * SSD_HEADS:(i + 1) * SSD_HEADS]
            out["loss"] = blk[0, 3 * SSD_HEADS]
        else:
            out[name] = blk
    return out


def _wire_shape(name):
    (a, b), axis = BIG[name]
    return (b, a) if axis == 1 else (a, b)


def _pack_weights(given, names, conv=False):
    parts = [(given[n][0].T if BIG[n][1] == 1 else given[n][0]).astype(BF16).reshape(_shard_rows_of(n), LANES) for n in names]
    if conv:
        pairs = lax.bitcast_convert_type(given["ssd_conv_w"], BF16).reshape(-1)
        parts.append(jnp.pad(pairs, (0, CONV_W_ROWS * LANES - pairs.shape[0])).reshape(CONV_W_ROWS, LANES))
    total = _group_rows(names, CONV_W_ROWS if conv else 0)
    packed = jnp.concatenate(parts, axis=0)
    packed = jnp.pad(packed, ((0, total - packed.shape[0]), (0, 0))).reshape(1, 2, total // 2, LANES)
    return jnp.broadcast_to(packed, (N_SHARD, 2, total // 2, LANES))


def _unpack_weights(slots, names, conv=False):
    rows = slots.reshape(N_SHARD, -1, LANES)
    out, r = {}, 0
    for name in names:
        n = _shard_rows_of(name)
        out[name] = rows[:, r:r + n].reshape(_wire_shape(name))
        r += n
    if conv:
        cols = XBC // N_SHARD
        pairs = rows[:, r:r + CONV_W_ROWS].reshape(N_SHARD, -1)[:, :SSD_CONV * cols * 2].reshape(N_SHARD, SSD_CONV, cols, 2)
        out["ssd_conv_w"] = jnp.transpose(lax.bitcast_convert_type(pairs, F32), (1, 0, 2)).reshape(SSD_CONV, XBC)
    return out


def _pack_grads(grads, names):
    total = _group_rows(names)
    parts = [grads[n].astype(BF16).reshape(N_SHARD, _shard_rows_of(n), LANES) for n in names]
    pad = total - sum(p.shape[1] for p in parts)
    if pad:
        parts.append(jnp.zeros((N_SHARD, pad, LANES), BF16))
    return jnp.concatenate(parts, axis=1).reshape(N_SHARD, 2, total // 2, LANES)


def kernel(x, mem, ffn1_norm, ffn1_w_gate, ffn1_w_up, ffn1_w_down, mix_norm, mem_norm, w_in, ssd_conv_w, ssd_conv_b, ssd_dt_bias, ssd_a_log, ssd_d, ssd_norm, gmlp_v_norm, gmlp_w_s, gmlp_b_s, w_mem_kv, w_branch_ssd, w_branch_gmlp, w_branch_mem, w_out, ffn2_norm, ffn2_w_gate, ffn2_w_up, ffn2_w_down, final_norm, loss_target, m_ffn1_norm, m_ffn1_w_gate, m_ffn1_w_up, m_ffn1_w_down, m_mix_norm, m_mem_norm, m_w_in, m_ssd_conv_w, m_ssd_conv_b, m_ssd_dt_bias, m_ssd_a_log, m_ssd_d, m_ssd_norm, m_gmlp_v_norm, m_gmlp_w_s, m_gmlp_b_s, m_w_mem_kv, m_w_branch_ssd, m_w_branch_gmlp, m_w_branch_mem, m_w_out, m_ffn2_norm, m_ffn2_w_gate, m_ffn2_w_up, m_ffn2_w_down, m_final_norm, v_ffn1_norm, v_ffn1_w_gate, v_ffn1_w_up, v_ffn1_w_down, v_mix_norm, v_mem_norm, v_w_in, v_ssd_conv_w, v_ssd_conv_b, v_ssd_dt_bias, v_ssd_a_log, v_ssd_d, v_ssd_norm, v_gmlp_v_norm, v_gmlp_w_s, v_gmlp_b_s, v_w_mem_kv, v_w_branch_ssd, v_w_branch_gmlp, v_w_branch_mem, v_w_out, v_ffn2_norm, v_ffn2_w_gate, v_ffn2_w_up, v_ffn2_w_down, v_final_norm):
    given = dict(x=x, mem=mem, ffn1_norm=ffn1_norm, ffn1_w_gate=ffn1_w_gate, ffn1_w_up=ffn1_w_up, ffn1_w_down=ffn1_w_down, mix_norm=mix_norm, mem_norm=mem_norm, w_in=w_in, ssd_conv_w=ssd_conv_w, ssd_conv_b=ssd_conv_b, ssd_dt_bias=ssd_dt_bias, ssd_a_log=ssd_a_log, ssd_d=ssd_d, ssd_norm=ssd_norm, gmlp_v_norm=gmlp_v_norm, gmlp_w_s=gmlp_w_s, gmlp_b_s=gmlp_b_s, w_mem_kv=w_mem_kv, w_branch_ssd=w_branch_ssd, w_branch_gmlp=w_branch_gmlp, w_branch_mem=w_branch_mem, w_out=w_out, ffn2_norm=ffn2_norm, ffn2_w_gate=ffn2_w_gate, ffn2_w_up=ffn2_w_up, ffn2_w_down=ffn2_w_down, final_norm=final_norm, loss_target=loss_target, m_ffn1_norm=m_ffn1_norm, m_ffn1_w_gate=m_ffn1_w_gate, m_ffn1_w_up=m_ffn1_w_up, m_ffn1_w_down=m_ffn1_w_down, m_mix_norm=m_mix_norm, m_mem_norm=m_mem_norm, m_w_in=m_w_in, m_ssd_conv_w=m_ssd_conv_w, m_ssd_conv_b=m_ssd_conv_b, m_ssd_dt_bias=m_ssd_dt_bias, m_ssd_a_log=m_ssd_a_log, m_ssd_d=m_ssd_d, m_ssd_norm=m_ssd_norm, m_gmlp_v_norm=m_gmlp_v_norm, m_gmlp_w_s=m_gmlp_w_s, m_gmlp_b_s=m_gmlp_b_s, m_w_mem_kv=m_w_mem_kv, m_w_branch_ssd=m_w_branch_ssd, m_w_branch_gmlp=m_w_branch_gmlp, m_w_branch_mem=m_w_branch_mem, m_w_out=m_w_out, m_ffn2_norm=m_ffn2_norm, m_ffn2_w_gate=m_ffn2_w_gate, m_ffn2_w_up=m_ffn2_w_up, m_ffn2_w_down=m_ffn2_w_down, m_final_norm=m_final_norm, v_ffn1_norm=v_ffn1_norm, v_ffn1_w_gate=v_ffn1_w_gate, v_ffn1_w_up=v_ffn1_w_up, v_ffn1_w_down=v_ffn1_w_down, v_mix_norm=v_mix_norm, v_mem_norm=v_mem_norm, v_w_in=v_w_in, v_ssd_conv_w=v_ssd_conv_w, v_ssd_conv_b=v_ssd_conv_b, v_ssd_dt_bias=v_ssd_dt_bias, v_ssd_a_log=v_ssd_a_log, v_ssd_d=v_ssd_d, v_ssd_norm=v_ssd_norm, v_gmlp_v_norm=v_gmlp_v_norm, v_gmlp_w_s=v_gmlp_w_s, v_gmlp_b_s=v_gmlp_b_s, v_w_mem_kv=v_w_mem_kv, v_w_branch_ssd=v_w_branch_ssd, v_w_branch_gmlp=v_w_branch_gmlp, v_w_branch_mem=v_w_branch_mem, v_w_out=v_w_out, v_ffn2_norm=v_ffn2_norm, v_ffn2_w_gate=v_ffn2_w_gate, v_ffn2_w_up=v_ffn2_w_up, v_ffn2_w_down=v_ffn2_w_down, v_final_norm=v_final_norm)
    weights = [n for n in given if n not in ("x", "mem", "loss_target") and not n.startswith(("m_", "v_"))]
    xi, yi, ci = lax.axis_index("x"), lax.axis_index("y"), lax.axis_index("c")
    chip = (2 * xi + yi).astype(jnp.int32)
    core = ci.astype(jnp.int32)
    conv_cols = XBC // N_SHARD

    copies = {"first": _gather_start(_pack_weights(given, FFN1), "first")}
    tied = dict(given)
    tied["w_in"], tied["ffn2_w_gate"], _ = lax.optimization_barrier((given["w_in"], given["ffn2_w_gate"], copies["first"][2]))
    packed = {"mixer": _pack_weights(tied, MIXER, conv=True), "ffn2": _pack_weights(tied, FFN2)}
    w = {}
    for name in ("ffn1_norm", "mix_norm", "mem_norm", "ssd_conv_b", "ssd_norm", "gmlp_v_norm", "ffn2_norm", "final_norm"):
        w[name] = given[name].reshape(1, -1)
    w["ffn1_norm"] = w["ffn1_norm"] + copies["first"][2][0:1, 0:1]
    for name in HEAD_VECS:
        w[name] = given[name].reshape(-1)
    w["gmlp_w_s"] = given["gmlp_w_s"][0]
    w["gmlp_b_s"] = given["gmlp_b_s"][0]

    def arrived(tag, after):
        sems, thru, _ = copies[tag]
        return _gather_forward(_gather_wait(sems, thru, after, tag), tag)

    def then_start(slots, tag, weights, name):
        slots, nxt = lax.optimization_barrier((slots, packed[tag]))
        copies[tag] = _gather_start(nxt, tag)
        unpacked = weights(slots)
        unpacked[name], _ = lax.optimization_barrier((unpacked[name], copies[tag][2]))
        return unpacked

    def first_weights(after):
        after, packed["mixer"], packed["ffn2"] = lax.optimization_barrier((after, packed["mixer"], packed["ffn2"]))
        return then_start(arrived("first", after), "mixer", lambda s: _unpack_weights(s, FFN1), FFN1[0])

    def mixer_weights(after):
        def unpack(slots):
            rest = _unpack_weights(slots, MIXER, conv=True)
            rest["w_in_p"], rest["w_dt"] = _split_w_in(rest.pop("w_in"))
            return rest
        return then_start(arrived("mixer", after), "ffn2", unpack, "w_in_p")

    def second_weights(after):
        return _unpack_weights(arrived("ffn2", after), FFN2)

    pending = {}

    def push(k, group_grads):
        gp = _pack_grads(group_grads, REDUCE_GROUPS[k])
        pa = _rs_add(gp, _rs_swap(gp, k), core.reshape(1), k)
        pending[k], token = _scatter_start(pa, k)
        return token[0:1, 0:1]

    def reduced(k, after):
        pa, land = _scatter_wait(*pending[k], after, k)
        gsum = _rs_share(_rs_sum(pa, land, jnp.stack([chip, core]), k), k)
        rows = gsum.reshape(-1, LANES)
        out, r = {}, 0
        for name in REDUCE_GROUPS[k]:
            n = _shard_rows_of(name)
            a, b = given[name].shape[1:]
            out[name] = rows[r:r + n].reshape(b, a).T if BIG[name][1] == 1 else rows[r:r + n].reshape(a, b)
            r += n
        return out

    loss_part, grad_x, g = _local_step(x[0], mem[0], loss_target[0], w, first_weights, mixer_weights, second_weights, push)

    grads, deltas, new_m, new_v = {}, {}, {}, {}

    def update(k, after):
        for name, gl in reduced(k, after).items():
            d, nm, nv = _adamw(given[name][0], gl, given["m_" + name][0], given["v_" + name][0], f"adamw_{name}")
            grads[name], deltas[name], new_m[name], new_v[name] = (a[None] for a in (gl, d, nm, nv))

    update(0, grad_x)
    update(1, deltas[REDUCE_GROUPS[0][-1]])

    small_vals = {k: g[k] for k, _ in SMALL if k != "heads"}
    small_vals.update({k: g[k] for k in HEAD_VECS})
    red = _unpack_small(_allreduce_small(_pack_small(small_vals, loss=loss_part[0, 0])))
    update(2, deltas[REDUCE_GROUPS[1][-1]])
    conv_g = lax.dynamic_slice_in_dim(red["ssd_conv_w"].reshape(SSD_CONV, XBC), chip * conv_cols, conv_cols, axis=1)
    d, nm, nv = _adamw(given["ssd_conv_w"][0], conv_g, given["m_ssd_conv_w"][0], given["v_ssd_conv_w"][0], "adamw_conv_w")
    grads["ssd_conv_w"], deltas["ssd_conv_w"], new_m["ssd_conv_w"], new_v["ssd_conv_w"] = (a[None] for a in (conv_g, d, nm, nv))
    for k in [k for k, _ in SMALL if k not in ("heads", "ssd_conv_w")] + list(HEAD_VECS):
        shape = given[k].shape
        as2d = lambda a: a.reshape(-1, shape[-1])
        d, nm, nv = _adamw(as2d(given[k]), as2d(red[k]), as2d(given["m_" + k]), as2d(given["v_" + k]), f"adamw_{k}")
        grads[k], deltas[k], new_m[k], new_v[k] = (a.reshape(shape) for a in (red[k], d, nm, nv))

    return (red["loss"], grad_x[None], *[grads[n] for n in weights], *[deltas[n] for n in weights],
            *[new_m[n] for n in weights], *[new_v[n] for n in weights])
```

```python
import functools
import math

import jax
import jax.numpy as jnp
from jax import lax
from jax.experimental import pallas as pl
from jax.experimental.pallas import tpu as pltpu

F32, BF16 = jnp.float32, jnp.bfloat16
HI = lax.Precision.HIGHEST
MESH = pl.DeviceIdType.MESH

D_MODEL = 1024
D_FF = 2816
MEM_LEN = 256
SSD_INNER = 2048
SSD_HEADS = 32
SSD_GROUPS = 4
SSD_STATE = 128
SSD_CONV = 4
CHUNK = 128
XBC = SSD_INNER + 2 * SSD_GROUPS * SSD_STATE
GMLP_W = 1024
GMLP_GROUPS = 8
MEM_W = 256
MEM_HEADS = 4
EPS = 1e-6
IN_WIDTH = 10528
IN_Z, IN_XBC, IN_DT, IN_UV, IN_Q, IN_GL = 0, 2048, 5120, 5152, 7200, 7456
P_GL, P_Z, P_XBC, P_UV, P_Q, P_W = 0, 3072, 5120, 8192, 10240, 10752
P_USED = 10496

ADAM_LR, ADAM_B1, ADAM_B2, ADAM_EPS, ADAM_WD, ADAM_STEP = 0.001, 0.9, 0.999, 1e-08, 0.01, 10

V7X_VMEM_LIMIT = 56 * 1024 * 1024
N_SHARD = 4
LANES = 1024


def _cparams(*sem):
    return pltpu.CompilerParams(dimension_semantics=sem, vmem_limit_bytes=V7X_VMEM_LIMIT)


ANY = pl.BlockSpec(memory_space=pl.ANY)


def _sigmoid(x):
    return 0.5 * jnp.tanh(0.5 * x) + 0.5


def _row_tile(t):
    return min(512, t)


_DIMS = {"nn": (((1,), (0,)), ((), ())), "nt": (((1,), (1,)), ((), ())), "tn": (((0,), (0,)), ((), ()))}


def _matmul(a, b, *, mode, out_dtype, tm, tn, tk, name, scale=1.0, addend=None):
    if mode == "tn":
        k_dim, m_dim = a.shape
    else:
        m_dim, k_dim = a.shape
    n_dim = b.shape[0] if mode == "nt" else b.shape[1]
    tm, tn, tk = min(tm, m_dim), min(tn, n_dim), min(tk, k_dim)
    assert m_dim % tm == 0 and n_dim % tn == 0 and k_dim % tk == 0, (name, a.shape, b.shape, tm, tn, tk)
    ni, nj, nk = m_dim // tm, n_dim // tn, k_dim // tk
    a_spec = pl.BlockSpec((tk, tm), lambda j, i, k: (k, i)) if mode == "tn" else pl.BlockSpec((tm, tk), lambda j, i, k: (i, k))
    b_spec = pl.BlockSpec((tn, tk), lambda j, i, k: (j, k)) if mode == "nt" else pl.BlockSpec((tk, tn), lambda j, i, k: (k, j))
    o_spec = pl.BlockSpec((tm, tn), lambda j, i, k: (i, j))
    dims = _DIMS[mode]
    has_add = addend is not None

    def body(*refs):
        a_ref, b_ref = refs[:2]
        r_ref = refs[2] if has_add else None
        o_ref = refs[2 + has_add]

        def finish(acc):
            r = acc * scale if scale != 1.0 else acc
            if has_add:
                r = r + r_ref[...].astype(F32)
            o_ref[...] = r.astype(o_ref.dtype)

        prod = lax.dot_general(a_ref[...].astype(BF16), b_ref[...].astype(BF16), dims, preferred_element_type=F32)
        if nk == 1:
            finish(prod)
            return
        acc_ref = refs[-1]
        k = pl.program_id(2)

        @pl.when(k == 0)
        def _():
            acc_ref[...] = prod

        @pl.when(k > 0)
        def _():
            acc_ref[...] += prod

        @pl.when(k == nk - 1)
        def _():
            finish(acc_ref[...])

    in_specs = [a_spec, b_spec] + ([o_spec] if has_add else [])
    args = (a, b) + ((addend,) if has_add else ())
    return pl.pallas_call(
        body, name=name, grid=(nj, ni, nk), in_specs=in_specs, out_specs=o_spec,
        out_shape=jax.ShapeDtypeStruct((m_dim, n_dim), out_dtype),
        scratch_shapes=[] if nk == 1 else [pltpu.VMEM((tm, tn), F32)],
        compiler_params=_cparams("parallel", "parallel", "arbitrary"),
    )(*args)


ROW_STRIP = 16


def _strips(tm, fn, init=None, rb=ROW_STRIP):
    def step(i, carry):
        return fn(pl.ds(pl.multiple_of(i * rb, rb), rb), carry)
    return lax.fori_loop(0, tm // rb, step, init, unroll=2)


def _rms_fwd(x, gain, name):
    t, d = x.shape
    tm = _row_tile(t)

    def body(x_ref, g_ref, o_ref):
        xv = x_ref[...]
        r = lax.rsqrt(jnp.mean(xv * xv, axis=-1, keepdims=True) + EPS)
        o_ref[...] = (xv * r * g_ref[...]).astype(o_ref.dtype)

    return pl.pallas_call(
        body, name=name, grid=(t // tm,),
        in_specs=[pl.BlockSpec((tm, d), lambda i: (i, 0)), pl.BlockSpec((1, d), lambda i: (0, 0))],
        out_specs=pl.BlockSpec((tm, d), lambda i: (i, 0)),
        out_shape=jax.ShapeDtypeStruct((t, d), BF16), compiler_params=_cparams("parallel"),
    )(x, gain)


def _rms_bwd(x, gain, dn, dres, name):
    t, d = x.shape
    tm = _row_tile(t)
    has_res = dres is not None

    def body(*refs):
        if has_res:
            x_ref, g_ref, dn_ref, r_ref, dx_ref, dg_ref = refs
        else:
            x_ref, g_ref, dn_ref, dx_ref, dg_ref = refs

        @pl.when(pl.program_id(0) == 0)
        def _():
            dg_ref[...] = jnp.zeros_like(dg_ref)

        xv = x_ref[...]
        r = lax.rsqrt(jnp.mean(xv * xv, axis=-1, keepdims=True) + EPS)
        xh = xv * r
        dnv = dn_ref[...].astype(F32)
        dg_ref[...] += jnp.sum(dnv * xh, axis=0, keepdims=True)
        dxh = dnv * g_ref[...]
        dx = r * (dxh - xh * jnp.mean(dxh * xh, axis=-1, keepdims=True))
        if has_res:
            dx = dx + r_ref[...]
        dx_ref[...] = dx

    row = pl.BlockSpec((tm, d), lambda i: (i, 0))
    vec = pl.BlockSpec((1, d), lambda i: (0, 0))
    in_specs = [row, vec, row] + ([row] if has_res else [])
    args = (x, gain, dn) + ((dres,) if has_res else ())
    return pl.pallas_call(
        body, name=name, grid=(t // tm,), in_specs=in_specs, out_specs=[row, vec],
        out_shape=[jax.ShapeDtypeStruct((t, d), F32), jax.ShapeDtypeStruct((1, d), F32)],
        compiler_params=_cparams("arbitrary"),
    )(*args)


def _loss_head(h, gain, target, name):
    t, d = h.shape
    tm = _row_tile(t)

    def body(h_ref, g_ref, t_ref, dh_ref, dg_ref, l_ref):
        @pl.when(pl.program_id(0) == 0)
        def _():
            dg_ref[...] = jnp.zeros_like(dg_ref)
            l_ref[...] = jnp.zeros_like(l_ref)

        xv = h_ref[...]
        g = g_ref[...]
        r = lax.rsqrt(jnp.mean(xv * xv, axis=-1, keepdims=True) + EPS)
        xh = xv * r
        err = xh * g - t_ref[...]
        l_ref[...] += 0.5 * jnp.sum(jnp.mean(err * err, axis=-1, keepdims=True), axis=0, keepdims=True)
        dy = err * (1.0 / d)
        dg_ref[...] += jnp.sum(dy * xh, axis=0, keepdims=True)
        dxh = dy * g
        dh_ref[...] = r * (dxh - xh * jnp.mean(dxh * xh, axis=-1, keepdims=True))

    row = pl.BlockSpec((tm, d), lambda i: (i, 0))
    vec = pl.BlockSpec((1, d), lambda i: (0, 0))
    return pl.pallas_call(
        body, name=name, grid=(t // tm,), in_specs=[row, vec, row],
        out_specs=[row, vec, pl.BlockSpec((1, 128), lambda i: (0, 0))],
        out_shape=[jax.ShapeDtypeStruct((t, d), F32), jax.ShapeDtypeStruct((1, d), F32), jax.ShapeDtypeStruct((1, 128), F32)],
        compiler_params=_cparams("arbitrary"),
    )(h, gain, target)


FF_TILE = 1408


def _ffn_fwd(n, x, wg, wu, wd, name):
    t, d = x.shape
    tm, tn = _row_tile(t), FF_TILE
    nj = D_FF // tn

    def body(n_ref, x_ref, wg_ref, wu_ref, wd_ref, h_ref, g_ref, u_ref, acc_ref):
        j = pl.program_id(1)

        @pl.when(j == 0)
        def _():
            acc_ref[...] = jnp.zeros_like(acc_ref)

        nb = n_ref[...]
        g = lax.dot_general(nb, wg_ref[...], _DIMS["nt"], preferred_element_type=F32)
        u = lax.dot_general(nb, wu_ref[...], _DIMS["nt"], preferred_element_type=F32)
        g_ref[...] = g.astype(BF16)
        u_ref[...] = u.astype(BF16)
        a = g * _sigmoid(g) * u
        acc_ref[...] += jnp.dot(a.astype(BF16), wd_ref[...], preferred_element_type=F32)

        @pl.when(j == nj - 1)
        def _():
            h_ref[...] = x_ref[...] + 0.5 * acc_ref[...]

    row = pl.BlockSpec((tm, d), lambda i, j: (i, 0))
    act = pl.BlockSpec((tm, tn), lambda i, j: (i, j))
    return pl.pallas_call(
        body, name=name, grid=(t // tm, nj),
        in_specs=[row, row] + [pl.BlockSpec((tn, d), lambda i, j: (j, 0))] * 3,
        out_specs=[row, act, act],
        out_shape=[jax.ShapeDtypeStruct((t, d), F32), jax.ShapeDtypeStruct((t, D_FF), BF16), jax.ShapeDtypeStruct((t, D_FF), BF16)],
        scratch_shapes=[pltpu.VMEM((tm, d), F32)], compiler_params=_cparams("parallel", "arbitrary"),
    )(n, x, wg, wu, wd)


def _ffn_bwd_act(dh, g, u, wg, wu, wd, name):
    t, d = dh.shape
    tm, tn = _row_tile(t), FF_TILE
    nj = D_FF // tn

    def body(dh_ref, g_ref, u_ref, wg_ref, wu_ref, wd_ref, dn_ref, dg_ref, du_ref, a_ref, acc_ref):
        j = pl.program_id(1)

        @pl.when(j == 0)
        def _():
            acc_ref[...] = jnp.zeros_like(acc_ref)

        dhb = (0.5 * dh_ref[...]).astype(BF16)
        da = lax.dot_general(dhb, wd_ref[...], _DIMS["nt"], preferred_element_type=F32)
        gv = g_ref[...].astype(F32)
        uv = u_ref[...].astype(F32)
        sg = _sigmoid(gv)
        s = gv * sg
        dg = (da * uv * (sg * (1.0 + gv * (1.0 - sg)))).astype(BF16)
        du = (da * s).astype(BF16)
        dg_ref[...] = dg
        du_ref[...] = du
        a_ref[...] = (s * uv).astype(BF16)
        acc_ref[...] += (jnp.dot(dg, wg_ref[...], preferred_element_type=F32)
                         + jnp.dot(du, wu_ref[...], preferred_element_type=F32))

        @pl.when(j == nj - 1)
        def _():
            dn_ref[...] = acc_ref[...]

    row = pl.BlockSpec((tm, d), lambda i, j: (i, 0))
    act = pl.BlockSpec((tm, tn), lambda i, j: (i, j))
    return pl.pallas_call(
        body, name=name, grid=(t // tm, nj),
        in_specs=[row, act, act] + [pl.BlockSpec((tn, d), lambda i, j: (j, 0))] * 3,
        out_specs=[row, act, act, act],
        out_shape=[jax.ShapeDtypeStruct((t, d), F32)] + [jax.ShapeDtypeStruct((t, D_FF), BF16)] * 3,
        scratch_shapes=[pltpu.VMEM((tm, d), F32)], compiler_params=_cparams("parallel", "arbitrary"),
    )(dh, g, u, wg, wu, wd)


def _ffn_forward(x, gain, weights, tag):
    n = _rms_fwd(x, gain, f"{tag}_norm")
    h, g, u = _ffn_fwd(n, x, *weights(n), f"{tag}_fwd")
    return h, (n, g, u)


def _ffn_backward(dh, x, gain, wg, wu, wd, saved, tag, push):
    n, g, u = saved
    dn, dg, du, a = _ffn_bwd_act(dh, g, u, wg, wu, wd, f"{tag}_bwd_act")
    kw = dict(mode="tn", out_dtype=BF16, tm=FF_TILE, tn=1024, tk=2048)
    d_wg = _matmul(dg, n, name=f"{tag}_dwg", **kw)
    d_wu = _matmul(du, n, name=f"{tag}_dwu", **kw)
    d_wd = _matmul(a, dh, scale=0.5, name=f"{tag}_dwd", **kw)
    token = push({f"{tag}_w_gate": d_wg, f"{tag}_w_up": d_wu, f"{tag}_w_down": d_wd})
    return _rms_bwd(x, gain + token, dn, dh, f"{tag}_norm_bwd")


CONV_COLS = 512
HALO = 8
CONV_STRIP = 32
CONV_ROWS = 1024


def _conv_fwd(p, w, b, name):
    t = p.shape[0]
    tm = min(CONV_ROWS, t)
    c0 = P_XBC // CONV_COLS

    def body(x_ref, halo_ref, w_ref, b_ref, o_ref, s_ref):
        i = pl.program_id(1)
        s_ref[0:HALO, :] = jnp.where(i > 0, halo_ref[...].astype(F32), 0.0)
        s_ref[HALO:HALO + tm, :] = x_ref[...].astype(F32)
        wv = w_ref[...]
        bv = b_ref[...]
        for r0 in range(0, tm, CONV_STRIP):
            acc = bv + wv[0:1, :] * s_ref[HALO - 3 + r0:HALO - 3 + r0 + CONV_STRIP, :]
            for k in range(1, SSD_CONV):
                acc = acc + wv[k:k + 1, :] * s_ref[HALO - 3 + k + r0:HALO - 3 + k + r0 + CONV_STRIP, :]
            o_ref[r0:r0 + CONV_STRIP, :] = (acc * _sigmoid(acc)).astype(o_ref.dtype)

    return pl.pallas_call(
        body, name=name, grid=(XBC // CONV_COLS, t // tm),
        in_specs=[pl.BlockSpec((tm, CONV_COLS), lambda j, i: (i, c0 + j)),
                  pl.BlockSpec((HALO, CONV_COLS), lambda j, i: (jnp.maximum(i * (tm // HALO) - 1, 0), c0 + j)),
                  pl.BlockSpec((SSD_CONV, CONV_COLS), lambda j, i: (0, j)), pl.BlockSpec((1, CONV_COLS), lambda j, i: (0, j))],
        out_specs=pl.BlockSpec((tm, CONV_COLS), lambda j, i: (i, j)),
        out_shape=jax.ShapeDtypeStruct((t, XBC), BF16),
        scratch_shapes=[pltpu.VMEM((tm + HALO, CONV_COLS), F32)], compiler_params=_cparams("parallel", "parallel"),
    )(p, p, w, b)


def _conv_bwd_act(p, dy, w, b, col0, name):
    t, cols = dy.shape
    tm = min(CONV_ROWS, t)
    c0 = (P_XBC + col0) // CONV_COLS
    w0 = col0 // CONV_COLS

    def body(x_ref, halo_ref, dy_ref, w_ref, b_ref, da_ref, dw_ref, db_ref, s_ref):
        i = pl.program_id(1)

        @pl.when(i == 0)
        def _():
            dw_ref[...] = jnp.zeros_like(dw_ref)
            db_ref[...] = jnp.zeros_like(db_ref)

        s_ref[0:HALO, :] = jnp.where(i > 0, halo_ref[...].astype(F32), 0.0)
        s_ref[HALO:HALO + tm, :] = x_ref[...].astype(F32)
        wv = w_ref[...]
        bv = b_ref[...]
        fold = lambda v: jnp.sum(v.reshape(CONV_STRIP // 8, 8, CONV_COLS), axis=0)
        sums = [jnp.zeros((8, CONV_COLS), F32) for _ in range(SSD_CONV + 1)]
        for r0 in range(0, tm, CONV_STRIP):
            taps = [s_ref[HALO - 3 + k + r0:HALO - 3 + k + r0 + CONV_STRIP, :] for k in range(SSD_CONV)]
            acc = bv + wv[0:1, :] * taps[0]
            for k in range(1, SSD_CONV):
                acc = acc + wv[k:k + 1, :] * taps[k]
            sg = _sigmoid(acc)
            dacc = dy_ref[r0:r0 + CONV_STRIP, :].astype(F32) * (sg * (1.0 + acc * (1.0 - sg)))
            da_ref[r0:r0 + CONV_STRIP, :] = dacc.astype(BF16)
            for k in range(SSD_CONV):
                sums[k] = sums[k] + fold(dacc * taps[k])
            sums[SSD_CONV] = sums[SSD_CONV] + fold(dacc)
        for k in range(SSD_CONV):
            dw_ref[k:k + 1, :] += jnp.sum(sums[k], axis=0, keepdims=True)
        db_ref[...] += jnp.sum(sums[SSD_CONV], axis=0, keepdims=True)

    return pl.pallas_call(
        body, name=name, grid=(cols // CONV_COLS, t // tm),
        in_specs=[pl.BlockSpec((tm, CONV_COLS), lambda j, i: (i, c0 + j)),
                  pl.BlockSpec((HALO, CONV_COLS), lambda j, i: (jnp.maximum(i * (tm // HALO) - 1, 0), c0 + j)),
                  pl.BlockSpec((tm, CONV_COLS), lambda j, i: (i, j)),
                  pl.BlockSpec((SSD_CONV, CONV_COLS), lambda j, i: (0, w0 + j)), pl.BlockSpec((1, CONV_COLS), lambda j, i: (0, w0 + j))],
        out_specs=[pl.BlockSpec((tm, CONV_COLS), lambda j, i: (i, j)), pl.BlockSpec((SSD_CONV, CONV_COLS), lambda j, i: (0, j)),
                   pl.BlockSpec((1, CONV_COLS), lambda j, i: (0, j))],
        out_shape=[jax.ShapeDtypeStruct((t, cols), BF16), jax.ShapeDtypeStruct((SSD_CONV, cols), F32), jax.ShapeDtypeStruct((1, cols), F32)],
        scratch_shapes=[pltpu.VMEM((tm + HALO, CONV_COLS), F32)], compiler_params=_cparams("parallel", "arbitrary"),
    )(p, p, dy, w, b)


def _conv_bwd_dx(dacc, w, col0, dp, name):
    t, cols = dacc.shape
    tm = min(CONV_ROWS, t)
    nt = t // tm
    w0 = col0 // CONV_COLS
    c0 = (P_XBC + col0) // CONV_COLS

    def body(d_ref, halo_ref, w_ref, dp_ref, o_ref, s_ref):
        del dp_ref
        i = pl.program_id(1)
        s_ref[0:tm, :] = d_ref[...].astype(F32)
        s_ref[tm:tm + HALO, :] = jnp.where(i < nt - 1, halo_ref[...].astype(F32), 0.0)
        wv = w_ref[...]
        for r0 in range(0, tm, CONV_STRIP):
            acc = wv[3:4, :] * s_ref[r0:r0 + CONV_STRIP, :]
            for k in range(SSD_CONV - 1):
                acc = acc + wv[k:k + 1, :] * s_ref[3 - k + r0:3 - k + r0 + CONV_STRIP, :]
            o_ref[r0:r0 + CONV_STRIP, :] = acc.astype(o_ref.dtype)

    return pl.pallas_call(
        body, name=name, grid=(cols // CONV_COLS, nt),
        in_specs=[pl.BlockSpec((tm, CONV_COLS), lambda j, i: (i, j)),
                  pl.BlockSpec((HALO, CONV_COLS), lambda j, i: (jnp.minimum((i + 1) * (tm // HALO), t // HALO - 1), j)),
                  pl.BlockSpec((SSD_CONV, CONV_COLS), lambda j, i: (0, w0 + j)), ANY],
        out_specs=pl.BlockSpec((tm, CONV_COLS), lambda j, i: (i, c0 + j)),
        out_shape=jax.ShapeDtypeStruct(dp.shape, dp.dtype), input_output_aliases={3: 0},
        scratch_shapes=[pltpu.VMEM((tm + HALO, CONV_COLS), F32)], compiler_params=_cparams("parallel", "parallel"),
    )(dacc, dacc, w, dp)


GROUP_COLS = SSD_INNER // SSD_GROUPS
PAIRS = GROUP_COLS // 128
HEADS_PER_GROUP = SSD_HEADS // SSD_GROUPS


def _dt_fwd(dt_raw, bias, name):
    t, n = dt_raw.shape
    tm = _row_tile(t)

    def body(x_ref, b_ref, o_ref):
        v = x_ref[...] + b_ref[...]
        o_ref[...] = jnp.maximum(v, 0.0) + jnp.log1p(jnp.exp(-jnp.abs(v)))

    row = pl.BlockSpec((tm, n), lambda i: (i, 0))
    vec = pl.BlockSpec((1, n), lambda i: (0, 0))
    return pl.pallas_call(body, name=name, grid=(t // tm,), in_specs=[row, vec], out_specs=row,
                          out_shape=jax.ShapeDtypeStruct((t, n), F32), compiler_params=_cparams("parallel"))(dt_raw, bias)


def _dt_bwd(ddt, dt_raw, bias, name):
    t, n = dt_raw.shape
    tm = _row_tile(t)

    def body(d_ref, x_ref, b_ref, o_ref, db_ref):
        @pl.when(pl.program_id(0) == 0)
        def _():
            db_ref[...] = jnp.zeros_like(db_ref)

        dr = d_ref[...] * _sigmoid(x_ref[...] + b_ref[...])
        o_ref[...] = dr.astype(o_ref.dtype)
        db_ref[...] += jnp.sum(dr, axis=0, keepdims=True)

    row = pl.BlockSpec((tm, n), lambda i: (i, 0))
    vec = pl.BlockSpec((1, n), lambda i: (0, 0))
    return pl.pallas_call(body, name=name, grid=(t // tm,), in_specs=[row, row, vec], out_specs=[row, vec],
                          out_shape=[jax.ShapeDtypeStruct((t, n), BF16), jax.ShapeDtypeStruct((1, n), F32)],
                          compiler_params=_cparams("arbitrary"))(ddt, dt_raw, bias)


SSD_STEP = 4


def _ssd_common(dt, dtt, a_log_w, a_log_t):
    l = CHUNK
    a = -jnp.exp(a_log_w)
    at = -jnp.exp(a_log_t)
    rowi = lax.broadcasted_iota(jnp.int32, (l, l), 0)
    coli = lax.broadcasted_iota(jnp.int32, (l, l), 1)
    tri = rowi >= coli
    lower = tri.astype(F32)
    upper = (rowi <= coli).astype(F32)
    acs = jnp.dot(lower, dt * a, precision=HI, preferred_element_type=F32)
    acst = jnp.dot(dtt * at, upper, precision=HI, preferred_element_type=F32)
    return a, acs, acst, jnp.where(tri, 0.0, -1e30), upper


def _pair_bc(w, lo, p):
    return jnp.where(lo, w[:, 2 * p:2 * p + 1], w[:, 2 * p + 1:2 * p + 2])


def _ssd_specs(t):
    rows = SSD_STEP * CHUNK
    assert t % rows == 0
    return t // rows, dict(
        xs=lambda cm: pl.BlockSpec((rows, GROUP_COLS), lambda g, c: (cm(c), g)),
        bm=lambda cm: pl.BlockSpec((rows, SSD_STATE), lambda g, c: (cm(c), SSD_INNER // SSD_STATE + g)),
        cmat=lambda cm: pl.BlockSpec((rows, SSD_STATE), lambda g, c: (cm(c), SSD_INNER // SSD_STATE + SSD_GROUPS + g)),
        dtw=lambda cm: pl.BlockSpec((1, rows, 128), lambda g, c: (g, cm(c), 0)),
        dtt=lambda cm: pl.BlockSpec((1, HEADS_PER_GROUP, rows), lambda g, c: (g, 0, cm(c))),
        wide=lambda cm: pl.BlockSpec((1, 1, 128), lambda g, c: (g, 0, 0)),
        tall=lambda cm: pl.BlockSpec((1, HEADS_PER_GROUP, 1), lambda g, c: (g, 0, 0)),
        grp=lambda cm: pl.BlockSpec((rows, GROUP_COLS), lambda g, c: (cm(c), g)),
        zp=lambda cm: pl.BlockSpec((rows, GROUP_COLS), lambda g, c: (cm(c), P_Z // GROUP_COLS + g)),
        vec=lambda cm: pl.BlockSpec((1, GROUP_COLS), lambda g, c: (0, g)),
        state=lambda cm: pl.BlockSpec((1, SSD_STEP, PAIRS, SSD_STATE, 128), lambda g, c: (g, cm(c), 0, 0, 0)),
    )


def _ssd_fwd(xc, p, hv, norm_g, name):
    t = xc.shape[0]
    nc, sp = _ssd_specs(t)
    ident = lambda c: c

    def body(xs_ref, b_ref, c_ref, dtw_ref, dtt_ref, aw_ref, at_ref, dk_ref, z_ref, ng_ref,
             y_ref, ys_ref, h_ref, st_ref):
        @pl.when(pl.program_id(1) == 0)
        def _():
            st_ref[...] = jnp.zeros_like(st_ref)

        lo = lax.broadcasted_iota(jnp.int32, (1, 128), 1) < 64
        dskip = dk_ref[0]
        for s in range(SSD_STEP):
            rows = slice(s * CHUNK, (s + 1) * CHUNK)
            dt = dtw_ref[0, rows, :]
            a, acs, acst, causal, _ = _ssd_common(dt, dtt_ref[0, :, rows], aw_ref[0], at_ref[0])
            ecs = jnp.exp(acs)
            alast = acs[CHUNK - 1:CHUNK, :]
            bmat, cmat = b_ref[rows, :], c_ref[rows, :]
            cb = lax.dot_general(cmat, bmat, _DIMS["nt"], preferred_element_type=F32)
            for pi in range(PAIRS):
                cols = slice(pi * 128, (pi + 1) * 128)
                x = xs_ref[rows, cols].astype(F32)
                xdt = x * _pair_bc(dt, lo, pi)
                xdtb = xdt.astype(BF16)
                ydiag = jnp.zeros((CHUNK, 128), F32)
                for r, mask in ((2 * pi, lo), (2 * pi + 1, jnp.logical_not(lo))):
                    lam = jnp.exp(acs[:, r:r + 1] - acst[r:r + 1, :] + causal)
                    m = (cb * lam).astype(BF16)
                    ydiag = ydiag + jnp.dot(m, jnp.where(mask, xdtb, 0), preferred_element_type=F32)
                ht = st_ref[pi]
                h_ref[0, s, pi] = ht
                yoff = jnp.dot(cmat, ht.astype(BF16), preferred_element_type=F32) * _pair_bc(ecs, lo, pi)
                y_ref[rows, cols] = (ydiag + yoff + _pair_bc(dskip, lo, pi) * x).astype(y_ref.dtype)
                alp = _pair_bc(alast, lo, pi)
                e = jnp.exp(alp - _pair_bc(acs, lo, pi))
                st = lax.dot_general(bmat, (xdt * e).astype(BF16), _DIMS["tn"], preferred_element_type=F32)
                st_ref[pi] = ht * jnp.exp(alp) + st
            zf = z_ref[rows, :].astype(F32)
            yg = y_ref[rows, :].astype(F32) * (zf * _sigmoid(zf))
            rstd = lax.rsqrt(jnp.mean(yg * yg, axis=-1, keepdims=True) + EPS)
            ys_ref[rows, :] = (yg * rstd * ng_ref[...]).astype(ys_ref.dtype)

    ins = ["xs", "bm", "cmat", "dtw", "dtt", "wide", "tall", "wide", "zp", "vec"]
    return pl.pallas_call(
        body, name=name, grid=(SSD_GROUPS, nc),
        in_specs=[sp[k](ident) for k in ins],
        out_specs=[sp["grp"](ident), sp["grp"](ident), sp["state"](ident)],
        out_shape=[jax.ShapeDtypeStruct((t, SSD_INNER), BF16), jax.ShapeDtypeStruct((t, SSD_INNER), BF16),
                   jax.ShapeDtypeStruct((SSD_GROUPS, t // CHUNK, PAIRS, SSD_STATE, 128), F32)],
        scratch_shapes=[pltpu.VMEM((PAIRS, SSD_STATE, 128), F32)], compiler_params=_cparams("parallel", "arbitrary"),
    )(xc, xc, xc, hv["dtw"], hv["dtt"], hv["alog_w"], hv["alog_t"], hv["dskip_w"], p, norm_g)


def _ssd_bwd(xc, p, hv, norm_g, y, dys, states, dp, name):
    t = xc.shape[0]
    nc, sp = _ssd_specs(t)
    rev = lambda c: nc - 1 - c

    def body(xs_ref, b_ref, c_ref, dtw_ref, dtt_ref, aw_ref, at_ref, dk_ref, z_ref, ng_ref,
             y_ref, dys_ref, h_ref, dp_ref,
             dxs_ref, db_ref, dc_ref, dz_ref, ddt_ref, hsum_ref, dng_ref, dst_ref):
        del dp_ref
        @pl.when(pl.program_id(1) == 0)
        def _():
            dst_ref[...] = jnp.zeros_like(dst_ref)
            hsum_ref[...] = jnp.zeros_like(hsum_ref)
            dng_ref[...] = jnp.zeros_like(dng_ref)

        lane = lax.broadcasted_iota(jnp.int32, (1, 128), 1)
        lo = lane < 64
        dskip = dk_ref[0]
        sel_r = lax.broadcasted_iota(jnp.int32, (128, 128), 0)
        sel_c = lax.broadcasted_iota(jnp.int32, (128, 128), 1)
        refs = (xs_ref, b_ref, c_ref, dtw_ref, dtt_ref, aw_ref, at_ref, dk_ref, z_ref, ng_ref, y_ref, dys_ref, h_ref,
                dxs_ref, db_ref, dc_ref, dz_ref, ddt_ref, hsum_ref, dng_ref, dst_ref)
        for s in reversed(range(SSD_STEP)):
            chunk_bwd(refs, slice(s * CHUNK, (s + 1) * CHUNK), s, lane, lo, dskip, sel_r, sel_c)

    def chunk_bwd(refs, rows, s, lane, lo, dskip, sel_r, sel_c):
        (xs_ref, b_ref, c_ref, dtw_ref, dtt_ref, aw_ref, at_ref, dk_ref, z_ref, ng_ref, y_ref, dys_ref, h_ref,
         dxs_ref, db_ref, dc_ref, dz_ref, ddt_ref, hsum_ref, dng_ref, dst_ref) = refs
        dt = dtw_ref[0, rows, :]
        a, acs, acst, causal, upper = _ssd_common(dt, dtt_ref[0, :, rows], aw_ref[0], at_ref[0])
        ecs = jnp.exp(acs)
        alast = acs[CHUNK - 1:CHUNK, :]
        bmat, cmat = b_ref[rows, :], c_ref[rows, :]
        cb = lax.dot_general(cmat, bmat, _DIMS["nt"], preferred_element_type=F32)

        zf = z_ref[rows, :].astype(F32)
        sg = _sigmoid(zf)
        sz = zf * sg
        yv = y_ref[rows, :].astype(F32)
        yg = yv * sz
        rstd = lax.rsqrt(jnp.mean(yg * yg, axis=-1, keepdims=True) + EPS)
        yhat = yg * rstd
        dysv = dys_ref[rows, :].astype(F32)
        dng_ref[...] += jnp.sum(dysv * yhat, axis=0, keepdims=True)
        dyh = dysv * ng_ref[...]
        dyg = rstd * (dyh - yhat * jnp.mean(dyh * yhat, axis=-1, keepdims=True))
        dz_ref[rows, :] = (dyg * yv * (sg * (1.0 + zf * (1.0 - sg)))).astype(dz_ref.dtype)
        dy_all = dyg * sz

        dal = jnp.zeros((CHUNK, 128), F32)
        ddtm = jnp.zeros((CHUNK, 128), F32)
        dalast = jnp.zeros((8, 128), F32)
        ddsk = jnp.zeros((8, 128), F32)
        dcb = jnp.zeros((CHUNK, CHUNK), F32)
        qcol = jnp.zeros((8, CHUNK), F32)
        sub8 = lax.broadcasted_iota(jnp.int32, (8, CHUNK), 0)
        dc_acc = jnp.zeros((CHUNK, SSD_STATE), F32)
        db_acc = jnp.zeros((CHUNK, SSD_STATE), F32)
        for pi in range(PAIRS):
            sel = (sel_c == 2 * pi + (sel_r >= 64).astype(jnp.int32)).astype(BF16)

            def hsum(v, sel=sel):
                return jnp.dot(v.astype(BF16), sel, preferred_element_type=F32)

            dyp = dy_all[:, pi * 128:(pi + 1) * 128]
            x = xs_ref[rows, pi * 128:(pi + 1) * 128].astype(F32)
            dtp = _pair_bc(dt, lo, pi)
            xdt = x * dtp
            dxdt = jnp.zeros((CHUNK, 128), F32)
            dypb, xdtb = dyp.astype(BF16), xdt.astype(BF16)
            for r, mask in ((2 * pi, lo), (2 * pi + 1, jnp.logical_not(lo))):
                lam = jnp.exp(acs[:, r:r + 1] - acst[r:r + 1, :] + causal)
                m32 = cb * lam
                m = m32.astype(BF16)
                dyr = jnp.where(mask, dypb, 0)
                xr = jnp.where(mask, xdtb, 0)
                dm = lax.dot_general(dyr, xr, _DIMS["nt"], preferred_element_type=F32)
                dcb = dcb + dm * lam
                q = dm * m32
                dal = dal + jnp.sum(q, axis=1, keepdims=True) * (lane == r).astype(F32)
                qcol = qcol + jnp.where(sub8 == r, jnp.sum(q, axis=0, keepdims=True), 0.0)
                dxdt = dxdt + lax.dot_general(m, dyr, _DIMS["tn"], preferred_element_type=F32)
            ht = h_ref[0, s, pi]
            htb = ht.astype(BF16)
            ecp = _pair_bc(ecs, lo, pi)
            yoff = jnp.dot(cmat, htb, preferred_element_type=F32) * ecp
            dg = (dyp * ecp).astype(BF16)
            dc_acc = dc_acc + lax.dot_general(dg, htb, _DIMS["nt"], preferred_element_type=F32)
            dht = lax.dot_general(cmat, dg, _DIMS["tn"], preferred_element_type=F32)
            dal = dal + hsum(dyp * yoff)
            dhn = dst_ref[pi]
            dhnb = dhn.astype(BF16)
            alp = _pair_bc(alast, lo, pi)
            e = jnp.exp(alp - _pair_bc(acs, lo, pi))
            xe = xdt * e
            db_acc = db_acc + lax.dot_general(xe.astype(BF16), dhnb, _DIMS["nt"], preferred_element_type=F32)
            dxe = jnp.dot(bmat, dhnb, preferred_element_type=F32)
            dxdt = dxdt + dxe * e
            tt = hsum(dxe * xe)
            dal = dal - tt
            dec = jnp.exp(alp)
            dalast = dalast + jnp.sum(tt, axis=0, keepdims=True) + hsum(
                jnp.broadcast_to(jnp.sum(dhn * ht, axis=0, keepdims=True) * dec, (8, 128)))
            dst_ref[pi] = dht + dhn * dec
            dxs_ref[rows, pi * 128:(pi + 1) * 128] = (_pair_bc(dskip, lo, pi) * dyp + dxdt * dtp).astype(dxs_ref.dtype)
            ddtm = ddtm + hsum(dxdt * x)
            ddsk = ddsk + hsum(jnp.broadcast_to(jnp.sum(dyp * x, axis=0, keepdims=True), (8, 128)))
        rowi = lax.broadcasted_iota(jnp.int32, (CHUNK, 128), 0)
        qcol_w = lax.dot_general(jnp.concatenate([qcol, jnp.zeros((CHUNK - 8, CHUNK), F32)], axis=0), (sel_r == sel_c).astype(F32),
                                 _DIMS["tn"], precision=HI, preferred_element_type=F32)
        dal = dal - qcol_w + jnp.where(rowi == CHUNK - 1, dalast[0:1, :], 0.0)
        dda = jnp.dot(upper, dal, precision=HI, preferred_element_type=F32)
        ddt_ref[0, rows, :] = ddtm + dda * a
        hsum_ref[0, 1:2, :] += jnp.sum(dda * dt, axis=0, keepdims=True) * a
        hsum_ref[0, 2:3, :] += ddsk[0:1, :]
        dcbb = dcb.astype(BF16)
        dc_ref[rows, :] = (jnp.dot(dcbb, bmat, preferred_element_type=F32) + dc_acc).astype(dc_ref.dtype)
        db_ref[rows, :] = (lax.dot_general(dcbb, cmat, _DIMS["tn"], preferred_element_type=F32) + db_acc).astype(db_ref.dtype)

    ins = ["xs", "bm", "cmat", "dtw", "dtt", "wide", "tall", "wide", "zp", "vec", "grp", "grp", "state"]
    col = lambda: pl.BlockSpec((SSD_STEP * CHUNK, SSD_STATE), lambda g, c: (rev(c), g))
    return pl.pallas_call(
        body, name=name, grid=(SSD_GROUPS, nc),
        in_specs=[sp[k](rev) for k in ins] + [ANY],
        out_specs=[sp["grp"](rev), col(), col(), sp["zp"](rev), sp["dtw"](rev),
                   pl.BlockSpec((1, 8, 128), lambda g, c: (g, 0, 0)), sp["vec"](rev)],
        out_shape=[jax.ShapeDtypeStruct((t, SSD_INNER), BF16), jax.ShapeDtypeStruct((t, SSD_GROUPS * SSD_STATE), BF16),
                   jax.ShapeDtypeStruct((t, SSD_GROUPS * SSD_STATE), BF16), jax.ShapeDtypeStruct(dp.shape, dp.dtype),
                   jax.ShapeDtypeStruct((SSD_GROUPS, t, 128), F32), jax.ShapeDtypeStruct((SSD_GROUPS, 8, 128), F32),
                   jax.ShapeDtypeStruct((1, SSD_INNER), F32)],
        input_output_aliases={len(ins): 3},
        scratch_shapes=[pltpu.VMEM((PAIRS, SSD_STATE, 128), F32)], compiler_params=_cparams("parallel", "arbitrary"),
    )(xc, xc, xc, hv["dtw"], hv["dtt"], hv["alog_w"], hv["alog_t"], hv["dskip_w"], p, norm_g, y, dys, states, dp)


def _wide(v):
    return jnp.pad(v.reshape(SSD_GROUPS, 1, HEADS_PER_GROUP), ((0, 0), (0, 0), (0, 128 - HEADS_PER_GROUP)))


def _head_inputs(dt, a_log, d_skip):
    t = dt.shape[0]
    g = dt[:, :SSD_HEADS].reshape(t, SSD_GROUPS, HEADS_PER_GROUP)
    return dict(
        dtw=jnp.pad(jnp.transpose(g, (1, 0, 2)), ((0, 0), (0, 0), (0, 128 - HEADS_PER_GROUP))),
        dtt=jnp.transpose(g, (1, 2, 0)),
        alog_w=_wide(a_log), alog_t=a_log.reshape(SSD_GROUPS, HEADS_PER_GROUP, 1),
        dskip_w=_wide(d_skip),
    )


def _gelu(x):
    return 0.5 * x * (1.0 + lax.erf(x * (1.0 / math.sqrt(2.0))))


def _gelu_grad(x):
    return 0.5 * (1.0 + lax.erf(x * (1.0 / math.sqrt(2.0)))) + x * jnp.exp(-0.5 * x * x) * (1.0 / math.sqrt(2.0 * math.pi))


def _tril_mask():
    r = lax.broadcasted_iota(jnp.int32, (CHUNK, CHUNK), 0)
    c = lax.broadcasted_iota(jnp.int32, (CHUNK, CHUNK), 1)
    return r >= c


def _gmlp_fwd(p, v_gain, w_s, b_col, name):
    t = p.shape[0]
    tm = _row_tile(t)
    u0 = P_UV // GMLP_W

    def body(u_ref, v_ref, gn_ref, ws_ref, bs_ref, o_ref):
        v = _gelu(v_ref[...].astype(F32))
        v = (v * lax.rsqrt(jnp.mean(v * v, axis=-1, keepdims=True) + EPS) * gn_ref[...]).astype(BF16)
        tril = _tril_mask()
        wm = [jnp.where(tril, ws_ref[g], 0.0).astype(BF16) for g in range(GMLP_GROUPS)]
        for k in range(tm // CHUNK):
            rows = slice(k * CHUNK, (k + 1) * CHUNK)
            for g in range(GMLP_GROUPS):
                cols = slice(g * 128, (g + 1) * 128)
                mixed = jnp.dot(wm[g], v[rows, cols], preferred_element_type=F32) + bs_ref[g]
                o_ref[rows, cols] = (_gelu(u_ref[rows, cols].astype(F32)) * mixed).astype(o_ref.dtype)

    return pl.pallas_call(
        body, name=name, grid=(t // tm,),
        in_specs=[pl.BlockSpec((tm, GMLP_W), lambda i: (i, u0)), pl.BlockSpec((tm, GMLP_W), lambda i: (i, u0 + 1)),
                  pl.BlockSpec((1, GMLP_W), lambda i: (0, 0)), pl.BlockSpec((GMLP_GROUPS, CHUNK, CHUNK), lambda i: (0, 0, 0)),
                  pl.BlockSpec((GMLP_GROUPS, CHUNK, 1), lambda i: (0, 0, 0))],
        out_specs=pl.BlockSpec((tm, GMLP_W), lambda i: (i, 0)),
        out_shape=jax.ShapeDtypeStruct((t, GMLP_W), BF16), compiler_params=_cparams("parallel"),
    )(p, p, v_gain, w_s, b_col)


def _gmlp_bwd(p, dy, v_gain, w_s, b_col, dp, name):
    t = p.shape[0]
    tm = _row_tile(t)
    u0 = P_UV // GMLP_W

    def body(u_ref, v_ref, dy_ref, gn_ref, ws_ref, bs_ref, dp_ref, duv_ref, dws_ref, dbs_ref, dgn_ref, dvn_ref):
        del dp_ref

        @pl.when(pl.program_id(0) == 0)
        def _():
            dws_ref[...] = jnp.zeros_like(dws_ref)
            dbs_ref[...] = jnp.zeros_like(dbs_ref)
            dgn_ref[...] = jnp.zeros_like(dgn_ref)

        vraw = v_ref[...].astype(F32)
        va = _gelu(vraw)
        rstd = lax.rsqrt(jnp.mean(va * va, axis=-1, keepdims=True) + EPS)
        vhat = va * rstd
        gain = gn_ref[...]
        vn = (vhat * gain).astype(BF16)
        tril = _tril_mask()
        wm = [jnp.where(tril, ws_ref[g], 0.0).astype(BF16) for g in range(GMLP_GROUPS)]
        for k in range(tm // CHUNK):
            rows = slice(k * CHUNK, (k + 1) * CHUNK)
            for g in range(GMLP_GROUPS):
                cols = slice(g * 128, (g + 1) * 128)
                uraw = u_ref[rows, cols].astype(F32)
                vb = vn[rows, cols]
                mixed = jnp.dot(wm[g], vb, preferred_element_type=F32) + bs_ref[g]
                dyb = dy_ref[rows, cols].astype(F32)
                duv_ref[rows, cols] = (dyb * mixed * _gelu_grad(uraw)).astype(duv_ref.dtype)
                dmix = dyb * _gelu(uraw)
                dmb = dmix.astype(BF16)
                dws_ref[g] += jnp.where(tril, lax.dot_general(dmb, vb, _DIMS["nt"], preferred_element_type=F32), 0.0)
                dbs_ref[g] += jnp.sum(dmix, axis=1, keepdims=True)
                dvn_ref[rows, cols] = lax.dot_general(wm[g], dmb, _DIMS["tn"], preferred_element_type=F32)
        dvn = dvn_ref[...]
        dgn_ref[...] += jnp.sum(dvn * vhat, axis=0, keepdims=True)
        dvh = dvn * gain
        dva = rstd * (dvh - vhat * jnp.mean(dvh * vhat, axis=-1, keepdims=True))
        duv_ref[:, GMLP_W:2 * GMLP_W] = (dva * _gelu_grad(vraw)).astype(duv_ref.dtype)

    return pl.pallas_call(
        body, name=name, grid=(t // tm,),
        in_specs=[pl.BlockSpec((tm, GMLP_W), lambda i: (i, u0)), pl.BlockSpec((tm, GMLP_W), lambda i: (i, u0 + 1)),
                  pl.BlockSpec((tm, GMLP_W), lambda i: (i, 0)),
                  pl.BlockSpec((1, GMLP_W), lambda i: (0, 0)), pl.BlockSpec((GMLP_GROUPS, CHUNK, CHUNK), lambda i: (0, 0, 0)),
                  pl.BlockSpec((GMLP_GROUPS, CHUNK, 1), lambda i: (0, 0, 0)), ANY],
        out_specs=[pl.BlockSpec((tm, 2 * GMLP_W), lambda i: (i, P_UV // (2 * GMLP_W))),
                   pl.BlockSpec((GMLP_GROUPS, CHUNK, CHUNK), lambda i: (0, 0, 0)),
                   pl.BlockSpec((GMLP_GROUPS, CHUNK, 1), lambda i: (0, 0, 0)), pl.BlockSpec((1, GMLP_W), lambda i: (0, 0))],
        out_shape=[jax.ShapeDtypeStruct(dp.shape, dp.dtype), jax.ShapeDtypeStruct((GMLP_GROUPS, CHUNK, CHUNK), F32),
                   jax.ShapeDtypeStruct((GMLP_GROUPS, CHUNK, 1), F32), jax.ShapeDtypeStruct((1, GMLP_W), F32)],
        input_output_aliases={6: 0},
        scratch_shapes=[pltpu.VMEM((tm, GMLP_W), F32)], compiler_params=_cparams("arbitrary"),
    )(p, p, dy, v_gain, w_s, b_col, dp)


def _head_masks():
    lane = lax.broadcasted_iota(jnp.int32, (1, MEM_W), 1)
    return [(lane >= h * 64) & (lane < (h + 1) * 64) for h in range(MEM_HEADS)]


def _mem_fwd(p, kv, name):
    t = p.shape[0]
    tm = _row_tile(t)
    q0 = P_Q // MEM_W

    def body(q_ref, kv_ref, o_ref):
        q = q_ref[...]
        k = kv_ref[:, 0:MEM_W].astype(BF16)
        v = kv_ref[:, MEM_W:2 * MEM_W].astype(BF16)
        out = jnp.zeros((tm, MEM_W), F32)
        for mask in _head_masks():
            s = lax.dot_general(jnp.where(mask, q, 0), k, _DIMS["nt"], preferred_element_type=F32) * 0.125
            e = jnp.exp(s - jnp.max(s, axis=-1, keepdims=True))
            pr = (e * (1.0 / jnp.sum(e, axis=-1, keepdims=True))).astype(BF16)
            out = out + jnp.where(mask, jnp.dot(pr, v, preferred_element_type=F32), 0.0)
        o_ref[...] = out.astype(o_ref.dtype)

    return pl.pallas_call(
        body, name=name, grid=(t // tm,),
        in_specs=[pl.BlockSpec((tm, MEM_W), lambda i: (i, q0)), pl.BlockSpec((MEM_LEN, 2 * MEM_W), lambda i: (0, 0))],
        out_specs=pl.BlockSpec((tm, MEM_W), lambda i: (i, 0)),
        out_shape=jax.ShapeDtypeStruct((t, MEM_W), BF16), compiler_params=_cparams("parallel"),
    )(p, kv)


def _mem_bwd(p, kv, dy, dp, name):
    t = p.shape[0]
    tm = _row_tile(t)
    q0 = P_Q // MEM_W
    assert P_W - P_Q == 2 * MEM_W

    def body(q_ref, kv_ref, dy_ref, dp_ref, dq_ref, dkv_ref):
        del dp_ref

        @pl.when(pl.program_id(0) == 0)
        def _():
            dkv_ref[...] = jnp.zeros_like(dkv_ref)

        q = q_ref[...]
        dy = dy_ref[...]
        k = kv_ref[:, 0:MEM_W].astype(BF16)
        v = kv_ref[:, MEM_W:2 * MEM_W].astype(BF16)
        dq = jnp.zeros((tm, MEM_W), F32)
        dk = jnp.zeros((MEM_LEN, MEM_W), F32)
        dv = jnp.zeros((MEM_LEN, MEM_W), F32)
        for mask in _head_masks():
            qh = jnp.where(mask, q, 0)
            dyh = jnp.where(mask, dy, 0)
            s = lax.dot_general(qh, k, _DIMS["nt"], preferred_element_type=F32) * 0.125
            e = jnp.exp(s - jnp.max(s, axis=-1, keepdims=True))
            pr = e * (1.0 / jnp.sum(e, axis=-1, keepdims=True))
            prb = pr.astype(BF16)
            dp = lax.dot_general(dyh, v, _DIMS["nt"], preferred_element_type=F32)
            ds = (pr * (dp - jnp.sum(dp * pr, axis=-1, keepdims=True)) * 0.125).astype(BF16)
            dq = dq + jnp.where(mask, jnp.dot(ds, k, preferred_element_type=F32), 0.0)
            dk = dk + lax.dot_general(ds, qh, _DIMS["tn"], preferred_element_type=F32)
            dv = dv + lax.dot_general(prb, dyh, _DIMS["tn"], preferred_element_type=F32)
        dq_ref[:, 0:MEM_W] = dq.astype(dq_ref.dtype)
        dq_ref[:, MEM_W:2 * MEM_W] = jnp.zeros((tm, MEM_W), dq_ref.dtype)
        dkv_ref[:, 0:MEM_W] += dk
        dkv_ref[:, MEM_W:2 * MEM_W] += dv

    return pl.pallas_call(
        body, name=name, grid=(t // tm,),
        in_specs=[pl.BlockSpec((tm, MEM_W), lambda i: (i, q0)), pl.BlockSpec((MEM_LEN, 2 * MEM_W), lambda i: (0, 0)),
                  pl.BlockSpec((tm, MEM_W), lambda i: (i, 0)), ANY],
        out_specs=[pl.BlockSpec((tm, 2 * MEM_W), lambda i: (i, P_Q // (2 * MEM_W))),
                   pl.BlockSpec((MEM_LEN, 2 * MEM_W), lambda i: (0, 0))],
        out_shape=[jax.ShapeDtypeStruct(dp.shape, dp.dtype), jax.ShapeDtypeStruct((MEM_LEN, 2 * MEM_W), F32)],
        input_output_aliases={3: 0}, compiler_params=_cparams("arbitrary"),
    )(p, kv, dy, dp)


def _merge_fwd(p, b_ssd, b_gmlp, b_mem, name):
    t = p.shape[0]
    tm = _row_tile(t)
    g0 = P_GL // D_MODEL

    def body(g1, g2, g3, b1, b2, b3, o_ref):
        def strip(rows, carry):
            acc = _sigmoid(g1[rows, :].astype(F32)) * b1[rows, :].astype(F32)
            acc = acc + _sigmoid(g2[rows, :].astype(F32)) * b2[rows, :].astype(F32)
            acc = acc + _sigmoid(g3[rows, :].astype(F32)) * b3[rows, :].astype(F32)
            o_ref[rows, :] = acc.astype(o_ref.dtype)
            return carry

        _strips(tm, strip, 0)

    row = pl.BlockSpec((tm, D_MODEL), lambda i: (i, 0))
    return pl.pallas_call(
        body, name=name, grid=(t // tm,),
        in_specs=[pl.BlockSpec((tm, D_MODEL), lambda i, k=k: (i, g0 + k)) for k in range(3)] + [row] * 3,
        out_specs=row, out_shape=jax.ShapeDtypeStruct((t, D_MODEL), BF16), compiler_params=_cparams("parallel"),
    )(p, p, p, b_ssd, b_gmlp, b_mem)


def _merge_bwd(p, dm, b_ssd, b_gmlp, b_mem, dp, name):
    t = p.shape[0]
    tm = _row_tile(t)
    g0 = P_GL // D_MODEL

    def body(g1, g2, g3, dm_ref, b1, b2, b3, dp_ref, d1, d2, d3, dgl_ref):
        del dp_ref

        def strip(rows, carry):
            dmv = dm_ref[rows, :].astype(F32)
            for k, (g_ref, b_ref, d_ref) in enumerate(((g1, b1, d1), (g2, b2, d2), (g3, b3, d3))):
                sg = _sigmoid(g_ref[rows, :].astype(F32))
                d_ref[rows, :] = (dmv * sg).astype(d_ref.dtype)
                dgl_ref[rows, k * D_MODEL:(k + 1) * D_MODEL] = (dmv * b_ref[rows, :].astype(F32) * sg * (1.0 - sg)).astype(dgl_ref.dtype)
            return carry

        _strips(tm, strip, 0)

    row = pl.BlockSpec((tm, D_MODEL), lambda i: (i, 0))
    return pl.pallas_call(
        body, name=name, grid=(t // tm,),
        in_specs=[pl.BlockSpec((tm, D_MODEL), lambda i, k=k: (i, g0 + k)) for k in range(3)] + [row] * 4 + [ANY],
        out_specs=[row, row, row, pl.BlockSpec((tm, 3 * D_MODEL), lambda i: (i, P_GL // (3 * D_MODEL)))],
        out_shape=[jax.ShapeDtypeStruct((t, D_MODEL), BF16)] * 3 + [jax.ShapeDtypeStruct(dp.shape, dp.dtype)],
        input_output_aliases={7: 3}, compiler_params=_cparams("parallel"),
    )(p, p, p, dm, b_ssd, b_gmlp, b_mem, dp)


def _local_step(x, mem, target, w, first_weights, mixer_weights, second_weights, push):
    t = x.shape[0]
    mm = functools.partial(_matmul, tk=1024)

    ffn_weights = lambda tag: (w[f"{tag}_w_gate"], w[f"{tag}_w_up"], w[f"{tag}_w_down"])

    def arriving(tag, fetch):
        def weights(n):
            w.update(fetch(n))
            return ffn_weights(tag)
        return weights

    w = dict(w)
    h1, ffn1_saved = _ffn_forward(x, w["ffn1_norm"], arriving("ffn1", first_weights), "ffn1")
    w.update(mixer_weights(h1))
    n2 = _rms_fwd(h1, w["mix_norm"], "mix_norm")
    p = mm(n2, w["w_in_p"], mode="nt", out_dtype=BF16, tm=1024, tn=1536, name="in_proj")
    dt_raw = mm(n2, w["w_dt"], mode="nt", out_dtype=F32, tm=2048, tn=128, name="dt_proj")
    dt_bias = jnp.pad(w["ssd_dt_bias"], (0, 128 - SSD_HEADS)).reshape(1, 128)
    hv = _head_inputs(_dt_fwd(dt_raw, dt_bias, "dt_fwd"), w["ssd_a_log"], w["ssd_d"])
    xc = _conv_fwd(p, w["ssd_conv_w"], w["ssd_conv_b"], "conv_fwd")
    y_ssd_raw, y_ssd, states = _ssd_fwd(xc, p, hv, w["ssd_norm"], "ssd_fwd")
    b_col = w["gmlp_b_s"].reshape(GMLP_GROUPS, CHUNK, 1)
    y_gmlp = _gmlp_fwd(p, w["gmlp_v_norm"], w["gmlp_w_s"], b_col, "gmlp_fwd")
    mem_n = _rms_fwd(mem, w["mem_norm"], "mem_norm")
    kv = mm(mem_n, w["w_mem_kv"], mode="nn", out_dtype=F32, tm=256, tn=512, name="mem_kv")
    y_mem = _mem_fwd(p, kv, "mem_fwd")
    b_ssd = mm(y_ssd, w["w_branch_ssd"], mode="nn", out_dtype=BF16, tm=1024, tn=1024, tk=2048, name="branch_ssd")
    b_gmlp = mm(y_gmlp, w["w_branch_gmlp"], mode="nn", out_dtype=BF16, tm=1024, tn=1024, name="branch_gmlp")
    b_mem = mm(y_mem, w["w_branch_mem"], mode="nt", out_dtype=BF16, tm=2048, tn=1024, tk=MEM_W, name="branch_mem")
    merged = _merge_fwd(p, b_ssd, b_gmlp, b_mem, "merge_fwd")
    h2 = mm(merged, w["w_out"], mode="nn", out_dtype=F32, tm=1024, tn=1024, addend=h1, name="out_proj")
    h3, ffn2_saved = _ffn_forward(h2, w["ffn2_norm"], arriving("ffn2", second_weights), "ffn2")
    dh3, d_final, loss = _loss_head(h3, w["final_norm"], target, "loss_head")

    g = {"final_norm": d_final}
    big = {}
    dh2, g["ffn2_norm"] = _ffn_backward(dh3, h2, w["ffn2_norm"], w["ffn2_w_gate"], w["ffn2_w_up"], w["ffn2_w_down"], ffn2_saved,
                                        "ffn2", functools.partial(push, 0))
    dmerged = mm(dh2, w["w_out"], mode="nt", out_dtype=BF16, tm=1024, tn=1024, name="out_proj_dx")
    big["w_out"] = mm(merged, dh2, mode="tn", out_dtype=BF16, tm=1024, tn=1024, tk=2048, name="out_proj_dw")
    db_ssd, db_gmlp, db_mem, dp = _merge_bwd(p, dmerged, b_ssd, b_gmlp, b_mem, lax.empty((t, P_W), BF16), "merge_bwd")
    dy_ssd = mm(db_ssd, w["w_branch_ssd"], mode="nt", out_dtype=BF16, tm=1024, tn=2048, name="branch_ssd_dx")
    dy_gmlp = mm(db_gmlp, w["w_branch_gmlp"], mode="nt", out_dtype=BF16, tm=1024, tn=1024, name="branch_gmlp_dx")
    dy_mem = mm(db_mem, w["w_branch_mem"], mode="nn", out_dtype=BF16, tm=2048, tn=256, name="branch_mem_dx")
    big["w_branch_ssd"] = mm(y_ssd, db_ssd, mode="tn", out_dtype=BF16, tm=1024, tn=1024, tk=2048, name="branch_ssd_dw")
    big["w_branch_gmlp"] = mm(y_gmlp, db_gmlp, mode="tn", out_dtype=BF16, tm=1024, tn=1024, tk=2048, name="branch_gmlp_dw")
    big["w_branch_mem"] = mm(db_mem, y_mem, mode="tn", out_dtype=BF16, tm=1024, tn=256, tk=2048, name="branch_mem_dw")
    dp, dkv = _mem_bwd(p, kv, dy_mem, dp, "mem_bwd")
    big["w_mem_kv"] = mm(mem_n, dkv, mode="tn", out_dtype=BF16, tm=1024, tn=512, tk=256, name="mem_kv_dw")
    dmem_n = mm(dkv, w["w_mem_kv"], mode="nt", out_dtype=F32, tm=256, tn=1024, tk=512, name="mem_kv_dx")
    _, g["mem_norm"] = _rms_bwd(mem, w["mem_norm"], dmem_n, None, "mem_norm_bwd")
    dp, d_ws, d_bs, g["gmlp_v_norm"] = _gmlp_bwd(p, dy_gmlp, w["gmlp_v_norm"], w["gmlp_w_s"], b_col, dp, "gmlp_bwd")
    g["gmlp_w_s"] = d_ws
    g["gmlp_b_s"] = d_bs.reshape(GMLP_GROUPS, CHUNK)
    dxs, d_bm, d_cm, dp, ddt_w, hsums, g["ssd_norm"] = _ssd_bwd(xc, p, hv, w["ssd_norm"], y_ssd_raw, dy_ssd, states, dp, "ssd_bwd")
    heads = hsums[:, :, :HEADS_PER_GROUP]
    g["ssd_a_log"] = heads[:, 1, :].reshape(1, SSD_HEADS)
    g["ssd_d"] = heads[:, 2, :].reshape(1, SSD_HEADS)
    ddt = jnp.transpose(ddt_w[:, :, :HEADS_PER_GROUP], (1, 0, 2)).reshape(t, SSD_HEADS)
    ddt, d_bias = _dt_bwd(jnp.pad(ddt, ((0, 0), (0, 128 - SSD_HEADS))), dt_raw, dt_bias, "dt_bwd")
    g["ssd_dt_bias"] = d_bias[:, :SSD_HEADS]
    dws, dbs = [], []
    for dyc, col0, tag in ((dxs, 0, "x"), (d_bm, SSD_INNER, "b"), (d_cm, SSD_INNER + SSD_GROUPS * SSD_STATE, "c")):
        dacc, dw_c, db_c = _conv_bwd_act(p, dyc, w["ssd_conv_w"], w["ssd_conv_b"], col0, f"conv_bwd_act_{tag}")
        dp = _conv_bwd_dx(dacc, w["ssd_conv_w"], col0, dp, f"conv_bwd_dx_{tag}")
        dws.append(dw_c)
        dbs.append(db_c)
    g["ssd_conv_w"] = jnp.concatenate(dws, axis=1)
    g["ssd_conv_b"] = jnp.concatenate(dbs, axis=1)
    d_win_p = _matmul(dp, n2, mode="tn", out_dtype=BF16, tm=1536, tn=1024, tk=2048, name="in_proj_dw")
    d_wdt = mm(ddt, n2, mode="tn", out_dtype=BF16, tm=128, tn=1024, tk=2048, name="dt_proj_dw")
    sl = lambda a, o, n: a[o:o + n]
    big["w_in"] = jnp.concatenate([sl(d_win_p, P_Z, 2048), sl(d_win_p, P_XBC, XBC), d_wdt[:SSD_HEADS], sl(d_win_p, P_UV, 2048),
                                   sl(d_win_p, P_Q, MEM_W), sl(d_win_p, P_GL, 3 * D_MODEL)], axis=0)
    token = push(1, big)
    dn2 = mm(dp, w["w_in_p"], mode="nn", out_dtype=F32, tm=1024, tn=1024, tk=3584, name="in_proj_dx")
    dn2 = _matmul(ddt, w["w_dt"], mode="nn", out_dtype=F32, tm=1024, tn=1024, tk=128, addend=dn2, name="dt_proj_dx")
    dh1, g["mix_norm"] = _rms_bwd(h1, w["mix_norm"] + token, dn2, dh2, "mix_norm_bwd")
    dx, g["ffn1_norm"] = _ffn_backward(dh1, x, w["ffn1_norm"], w["ffn1_w_gate"], w["ffn1_w_up"], w["ffn1_w_down"], ffn1_saved,
                                       "ffn1", functools.partial(push, 2))
    return loss, dx, g


def _split_w_in(w_in_t):
    sl = lambda o, n: w_in_t[o:o + n]
    w_p = jnp.concatenate([sl(IN_GL, 3 * D_MODEL), sl(IN_Z, 2048), sl(IN_XBC, XBC), sl(IN_UV, 2048), sl(IN_Q, MEM_W),
                           jnp.zeros((P_W - P_USED, D_MODEL), w_in_t.dtype)], axis=0)
    w_dt = jnp.pad(sl(IN_DT, SSD_HEADS), ((0, 128 - SSD_HEADS), (0, 0)))
    return w_p, w_dt


def _pick_tile(rows, cap=512):
    best = None
    for tile in range(8, min(rows, cap) + 1, 8):
        if rows % tile == 0:
            best = tile
    return best if best is not None else rows


def _adamw(w, g, m, v, name):
    rows, lanes = w.shape
    tile = _pick_tile(rows, cap=max(8, (512 * 1024 // lanes) // 8 * 8))
    c1 = 1.0 / (1.0 - ADAM_B1 ** ADAM_STEP)
    c2 = 1.0 / (1.0 - ADAM_B2 ** ADAM_STEP)

    def body(w_ref, g_ref, m_ref, v_ref, d_ref, nm_ref, nv_ref):
        gv = g_ref[...]
        nm = ADAM_B1 * m_ref[...] + (1.0 - ADAM_B1) * gv
        nv = ADAM_B2 * v_ref[...] + (1.0 - ADAM_B2) * (gv * gv)
        nm_ref[...] = nm
        nv_ref[...] = nv
        d_ref[...] = -ADAM_LR * ((nm * c1) / (jnp.sqrt(nv * c2) + ADAM_EPS) + ADAM_WD * w_ref[...])

    blk = pl.BlockSpec((tile, lanes), lambda i: (i, 0))
    return pl.pallas_call(
        body, name=name, grid=(rows // tile,), in_specs=[blk] * 4, out_specs=[blk] * 3,
        out_shape=[jax.ShapeDtypeStruct((rows, lanes), F32)] * 3, compiler_params=_cparams("parallel"),
    )(w, g, m, v)


HBM = pl.BlockSpec(memory_space=pltpu.HBM)


def _place():
    x, y, c = lax.axis_index("x"), lax.axis_index("y"), lax.axis_index("c")
    chips = [(1 - x, y), (x, 1 - y), (1 - x, 1 - y)]
    return x, y, c, chips


SEM = pl.BlockSpec(memory_space=pltpu.SEMAPHORE)
EFFECT = pltpu.SideEffectType.DATAFLOW_SIDE_EFFECTING
N_PEER = 3


def _sem_outs():
    return tuple(pltpu.SemaphoreType.DMA(()) for _ in range(2 * N_PEER))


def _gather_start(slots, tag):
    def body(in_ref, *refs):
        del in_ref
        sems, thru, token = refs[:2 * N_PEER], refs[2 * N_PEER], refs[2 * N_PEER + 1]
        x, y, c, chips = _place()
        own = thru.at[2 * x + y, c]
        for j, chip in enumerate(chips):
            pltpu.make_async_remote_copy(src_ref=own, dst_ref=own, send_sem=sems[j], recv_sem=sems[N_PEER + j],
                                         device_id=(*chip, c), device_id_type=MESH).start()
        token[...] = jnp.zeros_like(token)

    out = pl.pallas_call(
        body, name=f"gather_{tag}_start",
        out_shape=_sem_outs() + (pltpu.HBM(slots.shape, slots.dtype), jax.ShapeDtypeStruct((8, 128), F32)),
        in_specs=(HBM,), out_specs=(SEM,) * (2 * N_PEER) + (HBM, pl.BlockSpec(memory_space=pltpu.VMEM)),
        input_output_aliases={0: 2 * N_PEER}, compiler_params=pltpu.CompilerParams(has_side_effects=EFFECT),
    )(pltpu.with_memory_space_constraint(slots, pltpu.HBM))
    return out[:2 * N_PEER], out[2 * N_PEER], out[2 * N_PEER + 1]


def _gather_wait(sems, thru, after, tag):
    def body(in_ref, *refs):
        del in_ref
        sems, out_ref = refs[:2 * N_PEER], refs[2 * N_PEER + 1]
        x, y, c, chips = _place()
        own = out_ref.at[2 * x + y, c]
        for j, (cx, cy) in enumerate(chips):
            cp = pltpu.make_async_remote_copy(src_ref=own, dst_ref=out_ref.at[2 * cx + cy, c], send_sem=sems[j],
                                              recv_sem=sems[N_PEER + j], device_id=(cx, cy, c), device_id_type=MESH)
            cp.wait_send()
            cp.wait_recv()

    return pl.pallas_call(
        body, name=f"gather_{tag}_wait", out_shape=pltpu.HBM(thru.shape, thru.dtype),
        in_specs=(HBM,) + (SEM,) * (2 * N_PEER) + (pl.BlockSpec(memory_space=pl.ANY),), out_specs=HBM,
        input_output_aliases={0: 0}, compiler_params=pltpu.CompilerParams(has_side_effects=EFFECT),
    )(thru, *sems, after)


def _gather_forward(slots, tag):
    def body(in_ref, out_ref, send_sems, recv_sems):
        del in_ref
        x, y, c, chips = _place()
        cps = []
        for j, (cx, cy) in enumerate(chips):
            landed = out_ref.at[2 * cx + cy, c]
            cps.append(pltpu.make_async_remote_copy(src_ref=landed, dst_ref=landed, send_sem=send_sems.at[j], recv_sem=recv_sems.at[j],
                                                    device_id=(x, y, 1 - c), device_id_type=MESH))
        for cp in cps:
            cp.start()
        for j, (cx, cy) in enumerate(chips):
            other = out_ref.at[2 * cx + cy, 1 - c]
            pltpu.make_async_remote_copy(src_ref=other, dst_ref=other, send_sem=send_sems.at[j], recv_sem=recv_sems.at[j],
                                         device_id=(x, y, 1 - c), device_id_type=MESH).wait_recv()
        for cp in cps:
            cp.wait_send()

    return pl.pallas_call(
        body, name=f"gather_{tag}_forward", out_shape=jax.ShapeDtypeStruct(slots.shape, slots.dtype),
        in_specs=[HBM], out_specs=HBM, input_output_aliases={0: 0},
        scratch_shapes=[pltpu.SemaphoreType.DMA((N_PEER,)), pltpu.SemaphoreType.DMA((N_PEER,))],
    )(slots)


def _scatter_start(pa, tag):
    ns, rh, lanes = pa.shape
    land = pltpu.with_memory_space_constraint(lax.empty((N_PEER, rh, lanes), pa.dtype), pltpu.HBM)

    def body(pa_ref, land_ref, *refs):
        x, y, c, chips = _place()
        for j, (cx, cy) in enumerate(chips):
            pltpu.make_async_remote_copy(src_ref=pa_ref.at[2 * cx + cy], dst_ref=land_ref.at[j], send_sem=refs[j],
                                         recv_sem=refs[N_PEER + j], device_id=(cx, cy, c), device_id_type=MESH).start()
        refs[-1][...] = jnp.zeros_like(refs[-1])

    out = pl.pallas_call(
        body, name=f"scatter_start_{tag}",
        out_shape=_sem_outs() + (pltpu.HBM(pa.shape, pa.dtype), pltpu.HBM(land.shape, land.dtype), jax.ShapeDtypeStruct((8, 128), F32)),
        in_specs=(HBM, HBM), out_specs=(SEM,) * (2 * N_PEER) + (HBM, HBM, pl.BlockSpec(memory_space=pltpu.VMEM)),
        input_output_aliases={0: 2 * N_PEER, 1: 2 * N_PEER + 1}, compiler_params=pltpu.CompilerParams(has_side_effects=EFFECT),
    )(pltpu.with_memory_space_constraint(pa, pltpu.HBM), land)
    return (out[:2 * N_PEER], out[2 * N_PEER], out[2 * N_PEER + 1]), out[2 * N_PEER + 2]


def _scatter_wait(sems, pa_thru, land_thru, after, tag):
    def body(pa_ref, land_ref, *refs):
        sems = refs[:2 * N_PEER]
        x, y, c, chips = _place()
        for j, (cx, cy) in enumerate(chips):
            cp = pltpu.make_async_remote_copy(src_ref=pa_ref.at[2 * cx + cy], dst_ref=land_ref.at[j], send_sem=sems[j],
                                              recv_sem=sems[N_PEER + j], device_id=(cx, cy, c), device_id_type=MESH)
            cp.wait_send()
            cp.wait_recv()

    return pl.pallas_call(
        body, name=f"scatter_wait_{tag}",
        out_shape=(pltpu.HBM(pa_thru.shape, pa_thru.dtype), pltpu.HBM(land_thru.shape, land_thru.dtype)),
        in_specs=(HBM, HBM) + (SEM,) * (2 * N_PEER) + (pl.BlockSpec(memory_space=pl.ANY),), out_specs=(HBM, HBM),
        input_output_aliases={0: 0, 1: 1}, compiler_params=pltpu.CompilerParams(has_side_effects=EFFECT),
    )(pa_thru, land_thru, *sems, after)


def _rs_swap(gp, tag):
    ns, _, rh, lanes = gp.shape

    def body(in_ref, out_ref, send_sem, recv_sem):
        x, y, c, _ = _place()
        cp = pltpu.make_async_remote_copy(src_ref=in_ref.at[:, 1 - c], dst_ref=out_ref, send_sem=send_sem, recv_sem=recv_sem,
                                          device_id=(x, y, 1 - c), device_id_type=MESH)
        cp.start()
        cp.wait_send()
        cp.wait_recv()

    return pl.pallas_call(
        body, name=f"rs_swap_{tag}", out_shape=jax.ShapeDtypeStruct((ns, rh, lanes), gp.dtype), in_specs=[HBM], out_specs=HBM,
        scratch_shapes=[pltpu.SemaphoreType.DMA, pltpu.SemaphoreType.DMA],
    )(gp)


def _rs_tile(rh):
    return _pick_tile(rh, cap=512)


def _rs_add(gp, recv, c, tag):
    ns, _, rh, lanes = gp.shape
    tile = _rs_tile(rh)

    def body(c_ref, a_ref, b_ref, o_ref):
        o_ref[...] = (a_ref[...].astype(F32) + b_ref[...].astype(F32)).astype(o_ref.dtype)

    return pl.pallas_call(
        body, name=f"rs_add_{tag}", out_shape=jax.ShapeDtypeStruct((ns, rh, lanes), gp.dtype),
        grid_spec=pltpu.PrefetchScalarGridSpec(
            num_scalar_prefetch=1, grid=(ns, rh // tile),
            in_specs=[pl.BlockSpec((None, None, tile, lanes), lambda s, i, c_ref: (s, c_ref[0], i, 0)),
                      pl.BlockSpec((None, tile, lanes), lambda s, i, c_ref: (s, i, 0))],
            out_specs=pl.BlockSpec((None, tile, lanes), lambda s, i, c_ref: (s, i, 0))),
        compiler_params=_cparams("parallel", "parallel"),
    )(c, gp, recv)


def _rs_sum(pa, recv, place, tag):
    ns, rh, lanes = pa.shape
    tile = _rs_tile(rh)

    def body(place_ref, a_ref, r_ref, o_ref):
        acc = a_ref[...].astype(F32)
        for j in range(ns - 1):
            acc = acc + r_ref[j].astype(F32)
        o_ref[...] = acc

    return pl.pallas_call(
        body, name=f"rs_sum_{tag}", out_shape=jax.ShapeDtypeStruct((2, rh, lanes), F32),
        grid_spec=pltpu.PrefetchScalarGridSpec(
            num_scalar_prefetch=1, grid=(rh // tile,),
            in_specs=[pl.BlockSpec((None, tile, lanes), lambda i, place_ref: (place_ref[0], i, 0)),
                      pl.BlockSpec((ns - 1, tile, lanes), lambda i, place_ref: (0, i, 0))],
            out_specs=pl.BlockSpec((None, tile, lanes), lambda i, place_ref: (place_ref[1], i, 0))),
        compiler_params=_cparams("parallel"),
    )(place, pa, recv)


def _rs_share(halves, tag):
    def body(in_ref, out_ref, send_sem, recv_sem):
        del in_ref
        x, y, c, _ = _place()
        cp = pltpu.make_async_remote_copy(src_ref=out_ref.at[c], dst_ref=out_ref.at[c], send_sem=send_sem, recv_sem=recv_sem,
                                          device_id=(x, y, 1 - c), device_id_type=MESH)
        cp.start()
        other = out_ref.at[1 - c]
        pltpu.make_async_remote_copy(src_ref=other, dst_ref=other, send_sem=send_sem, recv_sem=recv_sem,
                                     device_id=(x, y, 1 - c), device_id_type=MESH).wait_recv()
        cp.wait_send()

    return pl.pallas_call(
        body, name=f"rs_share_{tag}", out_shape=jax.ShapeDtypeStruct(halves.shape, halves.dtype), in_specs=[HBM], out_specs=HBM,
        input_output_aliases={0: 0}, scratch_shapes=[pltpu.SemaphoreType.DMA, pltpu.SemaphoreType.DMA],
    )(halves)


N_DEV = 8
SMALL_ROWS = 160


def _allreduce_small(v):
    m_per, n = v.shape

    def body(x_ref, out_ref, all_ref, send_sems, recv_sems, local_sem):
        x, y, c, chips = _place()
        me, sibling = (x, y, c), (x, y, 1 - c)

        def rows(px, py, pc):
            return all_ref.at[pl.ds((4 * px + 2 * py + pc) * m_per, m_per), :]

        def copy(k, block, to, src=None):
            return pltpu.make_async_remote_copy(src_ref=rows(*block) if src is None else src, dst_ref=rows(*block),
                                                send_sem=send_sems.at[k], recv_sem=recv_sems.at[k], device_id=to, device_id_type=MESH)

        mine = pltpu.make_async_copy(x_ref, rows(*me), local_sem)
        mine.start()
        first = [copy(0, me, sibling, src=x_ref)]
        first += [copy(1 + j, me, (*chip, c), src=x_ref) for j, chip in enumerate(chips)]
        for cp in first:
            cp.start()
        passed = [copy(4 + j, (*chip, c), sibling) for j, chip in enumerate(chips)]
        for j, chip in enumerate(chips):
            copy(1 + j, (*chip, c), me).wait_recv()
            passed[j].start()
        copy(0, sibling, me).wait_recv()
        for j, chip in enumerate(chips):
            copy(4 + j, (*chip, 1 - c), me).wait_recv()
        for cp in first + passed:
            cp.wait_send()
        mine.wait()
        step = 32
        for r in range(0, m_per, step):
            acc = all_ref[r:r + step, :]
            for d in range(1, N_DEV):
                acc = acc + all_ref[d * m_per + r:d * m_per + r + step, :]
            out_ref[r:r + step, :] = acc

    vm = pl.BlockSpec(memory_space=pltpu.VMEM)
    return pl.pallas_call(
        body, name="allreduce_small", out_shape=jax.ShapeDtypeStruct((m_per, n), v.dtype), in_specs=[vm], out_specs=vm,
        scratch_shapes=[pltpu.VMEM((N_DEV * m_per, n), v.dtype), pltpu.SemaphoreType.DMA((7,)), pltpu.SemaphoreType.DMA((7,)),
                        pltpu.SemaphoreType.DMA],
        compiler_params=pltpu.CompilerParams(vmem_limit_bytes=V7X_VMEM_LIMIT),
    )(v)


BIG = {"ffn1_w_gate": ((D_MODEL, D_FF), 1), "ffn1_w_up": ((D_MODEL, D_FF), 1), "ffn1_w_down": ((D_FF, D_MODEL), 0),
       "ffn2_w_gate": ((D_MODEL, D_FF), 1), "ffn2_w_up": ((D_MODEL, D_FF), 1), "ffn2_w_down": ((D_FF, D_MODEL), 0),
       "w_in": ((D_MODEL, IN_WIDTH), 1), "w_mem_kv": ((D_MODEL, 2 * MEM_W), 0), "w_branch_ssd": ((SSD_INNER, D_MODEL), 0),
       "w_branch_gmlp": ((GMLP_W, D_MODEL), 0), "w_branch_mem": ((MEM_W, D_MODEL), 1), "w_out": ((D_MODEL, D_MODEL), 0)}
FFN1 = ("ffn1_w_gate", "ffn1_w_up", "ffn1_w_down")
FFN2 = ("ffn2_w_gate", "ffn2_w_up", "ffn2_w_down")
MIXER = ("w_out", "w_branch_ssd", "w_branch_gmlp", "w_branch_mem", "w_mem_kv", "w_in")
REDUCE_GROUPS = (FFN2, MIXER, FFN1)
CONV_W_ROWS = 8


def _shard_rows_of(name):
    (a, b), _ = BIG[name]
    return a * b // N_SHARD // LANES


def _group_rows(names, extra=0):
    return -(-(sum(_shard_rows_of(n) for n in names) + extra) // 32) * 32

SMALL = [("ffn1_norm", 1), ("mix_norm", 1), ("mem_norm", 1), ("ssd_conv_b", 3), ("heads", 1), ("ssd_norm", 2),
         ("gmlp_v_norm", 1), ("gmlp_w_s", 128), ("gmlp_b_s", 1), ("ffn2_norm", 1), ("final_norm", 1), ("ssd_conv_w", 12)]
assert sum(n for _, n in SMALL) <= SMALL_ROWS
HEAD_VECS = ("ssd_dt_bias", "ssd_a_log", "ssd_d")


def _pack_small(vals, loss=None):
    parts = []
    for name, nrows in SMALL:
        if name == "heads":
            row = jnp.concatenate([vals[k].reshape(-1) for k in HEAD_VECS]
                                  + [jnp.zeros((1,), F32) if loss is None else loss.reshape(1)])
            parts.append(jnp.pad(row, (0, LANES - row.shape[0])).reshape(1, LANES))
        elif name in vals:
            parts.append(vals[name].reshape(nrows, LANES))
        else:
            parts.append(jnp.zeros((nrows, LANES), F32))
    buf = jnp.concatenate(parts, axis=0)
    return jnp.pad(buf, ((0, SMALL_ROWS - buf.shape[0]), (0, 0)))


def _unpack_small(buf):
    out, r = {}, 0
    for name, nrows in SMALL:
        blk = buf[r:r + nrows]
        r += nrows
        if name == "heads":
            for i, k in enumerate(HEAD_VECS):
                out[k] = blk[0, i * SSD_HEADS:(i + 1) * SSD_HEADS]
            out["loss"] = blk[0, 3 * SSD_HEADS]
        else:
            out[name] = blk
    return out


def _wire_shape(name):
    (a, b), axis = BIG[name]
    return (b, a) if axis == 1 else (a, b)


def _pack_weights(given, names, conv=False):
    parts = [(given[n][0].T if BIG[n][1] == 1 else given[n][0]).astype(BF16).reshape(_shard_rows_of(n), LANES) for n in names]
    if conv:
        pairs = lax.bitcast_convert_type(given["ssd_conv_w"], BF16).reshape(-1)
        parts.append(jnp.pad(pairs, (0, CONV_W_ROWS * LANES - pairs.shape[0])).reshape(CONV_W_ROWS, LANES))
    total = _group_rows(names, CONV_W_ROWS if conv else 0)
    packed = jnp.concatenate(parts, axis=0)
    packed = jnp.pad(packed, ((0, total - packed.shape[0]), (0, 0))).reshape(1, 2, total // 2, LANES)
    return jnp.broadcast_to(packed, (N_SHARD, 2, total // 2, LANES))


def _unpack_weights(slots, names, conv=False):
    rows = slots.reshape(N_SHARD, -1, LANES)
    out, r = {}, 0
    for name in names:
        n = _shard_rows_of(name)
        out[name] = rows[:, r:r + n].reshape(_wire_shape(name))
        r += n
    if conv:
        cols = XBC // N_SHARD
        pairs = rows[:, r:r + CONV_W_ROWS].reshape(N_SHARD, -1)[:, :SSD_CONV * cols * 2].reshape(N_SHARD, SSD_CONV, cols, 2)
        out["ssd_conv_w"] = jnp.transpose(lax.bitcast_convert_type(pairs, F32), (1, 0, 2)).reshape(SSD_CONV, XBC)
    return out


def _pack_grads(grads, names):
    total = _group_rows(names)
    parts = [grads[n].astype(BF16).reshape(N_SHARD, _shard_rows_of(n), LANES) for n in names]
    pad = total - sum(p.shape[1] for p in parts)
    if pad:
        parts.append(jnp.zeros((N_SHARD, pad, LANES), BF16))
    return jnp.concatenate(parts, axis=1).reshape(N_SHARD, 2, total // 2, LANES)


def kernel(x, mem, ffn1_norm, ffn1_w_gate, ffn1_w_up, ffn1_w_down, mix_norm, mem_norm, w_in, ssd_conv_w, ssd_conv_b, ssd_dt_bias, ssd_a_log, ssd_d, ssd_norm, gmlp_v_norm, gmlp_w_s, gmlp_b_s, w_mem_kv, w_branch_ssd, w_branch_gmlp, w_branch_mem, w_out, ffn2_norm, ffn2_w_gate, ffn2_w_up, ffn2_w_down, final_norm, loss_target, m_ffn1_norm, m_ffn1_w_gate, m_ffn1_w_up, m_ffn1_w_down, m_mix_norm, m_mem_norm, m_w_in, m_ssd_conv_w, m_ssd_conv_b, m_ssd_dt_bias, m_ssd_a_log, m_ssd_d, m_ssd_norm, m_gmlp_v_norm, m_gmlp_w_s, m_gmlp_b_s, m_w_mem_kv, m_w_branch_ssd, m_w_branch_gmlp, m_w_branch_mem, m_w_out, m_ffn2_norm, m_ffn2_w_gate, m_ffn2_w_up, m_ffn2_w_down, m_final_norm, v_ffn1_norm, v_ffn1_w_gate, v_ffn1_w_up, v_ffn1_w_down, v_mix_norm, v_mem_norm, v_w_in, v_ssd_conv_w, v_ssd_conv_b, v_ssd_dt_bias, v_ssd_a_log, v_ssd_d, v_ssd_norm, v_gmlp_v_norm, v_gmlp_w_s, v_gmlp_b_s, v_w_mem_kv, v_w_branch_ssd, v_w_branch_gmlp, v_w_branch_mem, v_w_out, v_ffn2_norm, v_ffn2_w_gate, v_ffn2_w_up, v_ffn2_w_down, v_final_norm):
    given = dict(x=x, mem=mem, ffn1_norm=ffn1_norm, ffn1_w_gate=ffn1_w_gate, ffn1_w_up=ffn1_w_up, ffn1_w_down=ffn1_w_down, mix_norm=mix_norm, mem_norm=mem_norm, w_in=w_in, ssd_conv_w=ssd_conv_w, ssd_conv_b=ssd_conv_b, ssd_dt_bias=ssd_dt_bias, ssd_a_log=ssd_a_log, ssd_d=ssd_d, ssd_norm=ssd_norm, gmlp_v_norm=gmlp_v_norm, gmlp_w_s=gmlp_w_s, gmlp_b_s=gmlp_b_s, w_mem_kv=w_mem_kv, w_branch_ssd=w_branch_ssd, w_branch_gmlp=w_branch_gmlp, w_branch_mem=w_branch_mem, w_out=w_out, ffn2_norm=ffn2_norm, ffn2_w_gate=ffn2_w_gate, ffn2_w_up=ffn2_w_up, ffn2_w_down=ffn2_w_down, final_norm=final_norm, loss_target=loss_target, m_ffn1_norm=m_ffn1_norm, m_ffn1_w_gate=m_ffn1_w_gate, m_ffn1_w_up=m_ffn1_w_up, m_ffn1_w_down=m_ffn1_w_down, m_mix_norm=m_mix_norm, m_mem_norm=m_mem_norm, m_w_in=m_w_in, m_ssd_conv_w=m_ssd_conv_w, m_ssd_conv_b=m_ssd_conv_b, m_ssd_dt_bias=m_ssd_dt_bias, m_ssd_a_log=m_ssd_a_log, m_ssd_d=m_ssd_d, m_ssd_norm=m_ssd_norm, m_gmlp_v_norm=m_gmlp_v_norm, m_gmlp_w_s=m_gmlp_w_s, m_gmlp_b_s=m_gmlp_b_s, m_w_mem_kv=m_w_mem_kv, m_w_branch_ssd=m_w_branch_ssd, m_w_branch_gmlp=m_w_branch_gmlp, m_w_branch_mem=m_w_branch_mem, m_w_out=m_w_out, m_ffn2_norm=m_ffn2_norm, m_ffn2_w_gate=m_ffn2_w_gate, m_ffn2_w_up=m_ffn2_w_up, m_ffn2_w_down=m_ffn2_w_down, m_final_norm=m_final_norm, v_ffn1_norm=v_ffn1_norm, v_ffn1_w_gate=v_ffn1_w_gate, v_ffn1_w_up=v_ffn1_w_up, v_ffn1_w_down=v_ffn1_w_down, v_mix_norm=v_mix_norm, v_mem_norm=v_mem_norm, v_w_in=v_w_in, v_ssd_conv_w=v_ssd_conv_w, v_ssd_conv_b=v_ssd_conv_b, v_ssd_dt_bias=v_ssd_dt_bias, v_ssd_a_log=v_ssd_a_log, v_ssd_d=v_ssd_d, v_ssd_norm=v_ssd_norm, v_gmlp_v_norm=v_gmlp_v_norm, v_gmlp_w_s=v_gmlp_w_s, v_gmlp_b_s=v_gmlp_b_s, v_w_mem_kv=v_w_mem_kv, v_w_branch_ssd=v_w_branch_ssd, v_w_branch_gmlp=v_w_branch_gmlp, v_w_branch_mem=v_w_branch_mem, v_w_out=v_w_out, v_ffn2_norm=v_ffn2_norm, v_ffn2_w_gate=v_ffn2_w_gate, v_ffn2_w_up=v_ffn2_w_up, v_ffn2_w_down=v_ffn2_w_down, v_final_norm=v_final_norm)
    weights = [n for n in given if n not in ("x", "mem", "loss_target") and not n.startswith(("m_", "v_"))]
    xi, yi, ci = lax.axis_index("x"), lax.axis_index("y"), lax.axis_index("c")
    chip = (2 * xi + yi).astype(jnp.int32)
    core = ci.astype(jnp.int32)
    conv_cols = XBC // N_SHARD

    copies = {"first": _gather_start(_pack_weights(given, FFN1), "first")}
    tied = dict(given)
    tied["w_in"], tied["ffn2_w_gate"], _ = lax.optimization_barrier((given["w_in"], given["ffn2_w_gate"], copies["first"][2]))
    packed = {"mixer": _pack_weights(tied, MIXER, conv=True), "ffn2": _pack_weights(tied, FFN2)}
    w = {}
    for name in ("ffn1_norm", "mix_norm", "mem_norm", "ssd_conv_b", "ssd_norm", "gmlp_v_norm", "ffn2_norm", "final_norm"):
        w[name] = given[name].reshape(1, -1)
    w["ffn1_norm"] = w["ffn1_norm"] + copies["first"][2][0:1, 0:1]
    for name in HEAD_VECS:
        w[name] = given[name].reshape(-1)
    w["gmlp_w_s"] = given["gmlp_w_s"][0]
    w["gmlp_b_s"] = given["gmlp_b_s"][0]

    def arrived(tag, after):
        sems, thru, _ = copies[tag]
        return _gather_forward(_gather_wait(sems, thru, after, tag), tag)

    def then_start(slots, tag, weights, name):
        slots, nxt = lax.optimization_barrier((slots, packed[tag]))
        copies[tag] = _gather_start(nxt, tag)
        unpacked = weights(slots)
        unpacked[name], _ = lax.optimization_barrier((unpacked[name], copies[tag][2]))
        return unpacked

    def first_weights(after):
        after, packed["mixer"], packed["ffn2"] = lax.optimization_barrier((after, packed["mixer"], packed["ffn2"]))
        return then_start(arrived("first", after), "mixer", lambda s: _unpack_weights(s, FFN1), FFN1[0])

    def mixer_weights(after):
        def unpack(slots):
            rest = _unpack_weights(slots, MIXER, conv=True)
            rest["w_in_p"], rest["w_dt"] = _split_w_in(rest.pop("w_in"))
            return rest
        return then_start(arrived("mixer", after), "ffn2", unpack, "w_in_p")

    def second_weights(after):
        return _unpack_weights(arrived("ffn2", after), FFN2)

    pending = {}

    def push(k, group_grads):
        gp = _pack_grads(group_grads, REDUCE_GROUPS[k])
        pa = _rs_add(gp, _rs_swap(gp, k), core.reshape(1), k)
        pending[k], token = _scatter_start(pa, k)
        return token[0:1, 0:1]

    def reduced(k, after):
        pa, land = _scatter_wait(*pending[k], after, k)
        gsum = _rs_share(_rs_sum(pa, land, jnp.stack([chip, core]), k), k)
        rows = gsum.reshape(-1, LANES)
        out, r = {}, 0
        for name in REDUCE_GROUPS[k]:
            n = _shard_rows_of(name)
            a, b = given[name].shape[1:]
            out[name] = rows[r:r + n].reshape(b, a).T if BIG[name][1] == 1 else rows[r:r + n].reshape(a, b)
            r += n
        return out

    loss_part, grad_x, g = _local_step(x[0], mem[0], loss_target[0], w, first_weights, mixer_weights, second_weights, push)

    grads, deltas, new_m, new_v = {}, {}, {}, {}

    def update(k, after):
        for name, gl in reduced(k, after).items():
            d, nm, nv = _adamw(given[name][0], gl, given["m_" + name][0], given["v_" + name][0], f"adamw_{name}")
            grads[name], deltas[name], new_m[name], new_v[name] = (a[None] for a in (gl, d, nm, nv))

    update(0, grad_x)
    update(1, deltas[REDUCE_GROUPS[0][-1]])

    small_vals = {k: g[k] for k, _ in SMALL if k != "heads"}
    small_vals.update({k: g[k] for k in HEAD_VECS})
    red = _unpack_small(_allreduce_small(_pack_small(small_vals, loss=loss_part[0, 0])))
    update(2, deltas[REDUCE_GROUPS[1][-1]])
    conv_g = lax.dynamic_slice_in_dim(red["ssd_conv_w"].reshape(SSD_CONV, XBC), chip * conv_cols, conv_cols, axis=1)
    d, nm, nv = _adamw(given["ssd_conv_w"][0], conv_g, given["m_ssd_conv_w"][0], given["v_ssd_conv_w"][0], "adamw_conv_w")
    grads["ssd_conv_w"], deltas["ssd_conv_w"], new_m["ssd_conv_w"], new_v["ssd_conv_w"] = (a[None] for a in (conv_g, d, nm, nv))
    for k in [k for k, _ in SMALL if k not in ("heads", "ssd_conv_w")] + list(HEAD_VECS):
        shape = given[k].shape
        as2d = lambda a: a.reshape(-1, shape[-1])
        d, nm, nv = _adamw(as2d(given[k]), as2d(red[k]), as2d(given["m_" + k]), as2d(given["v_" + k]), f"adamw_{k}")
        grads[k], deltas[k], new_m[k], new_v[k] = (a.reshape(shape) for a in (red[k], d, nm, nv))

    return (red["loss"], grad_x[None], *[grads[n] for n in weights], *[deltas[n] for n in weights],
            *[new_m[n] for n in weights], *[new_v[n] for n in weights])
```

```python
import functools
import math

import jax
import jax.numpy as jnp
from jax import lax
from jax.experimental import pallas as pl
from jax.experimental.pallas import tpu as pltpu

F32, BF16 = jnp.float32, jnp.bfloat16
HI = lax.Precision.HIGHEST
MESH = pl.DeviceIdType.MESH

D_MODEL = 1024
D_FF = 2816
MEM_LEN = 256
SSD_INNER = 2048
SSD_HEADS = 32
SSD_GROUPS = 4
SSD_STATE = 128
SSD_CONV = 4
CHUNK = 128
XBC = SSD_INNER + 2 * SSD_GROUPS * SSD_STATE
GMLP_W = 1024
GMLP_GROUPS = 8
MEM_W = 256
MEM_HEADS = 4
EPS = 1e-6
IN_WIDTH = 10528
IN_Z, IN_XBC, IN_DT, IN_UV, IN_Q, IN_GL = 0, 2048, 5120, 5152, 7200, 7456
P_GL, P_Z, P_XBC, P_UV, P_Q, P_W = 0, 3072, 5120, 8192, 10240, 10752
P_USED = 10496

ADAM_LR, ADAM_B1, ADAM_B2, ADAM_EPS, ADAM_WD, ADAM_STEP = 0.001, 0.9, 0.999, 1e-08, 0.01, 10

V7X_VMEM_LIMIT = 56 * 1024 * 1024
N_SHARD = 4
LANES = 1024


def _cparams(*sem):
    return pltpu.CompilerParams(dimension_semantics=sem, vmem_limit_bytes=V7X_VMEM_LIMIT)


ANY = pl.BlockSpec(memory_space=pl.ANY)


def _sigmoid(x):
    return 0.5 * jnp.tanh(0.5 * x) + 0.5


def _row_tile(t):
    return min(512, t)


_DIMS = {"nn": (((1,), (0,)), ((), ())), "nt": (((1,), (1,)), ((), ())), "tn": (((0,), (0,)), ((), ()))}


def _matmul(a, b, *, mode, out_dtype, tm, tn, tk, name, scale=1.0, addend=None):
    if mode == "tn":
        k_dim, m_dim = a.shape
    else:
        m_dim, k_dim = a.shape
    n_dim = b.shape[0] if mode == "nt" else b.shape[1]
    tm, tn, tk = min(tm, m_dim), min(tn, n_dim), min(tk, k_dim)
    assert m_dim % tm == 0 and n_dim % tn == 0 and k_dim % tk == 0, (name, a.shape, b.shape, tm, tn, tk)
    ni, nj, nk = m_dim // tm, n_dim // tn, k_dim // tk
    a_spec = pl.BlockSpec((tk, tm), lambda j, i, k: (k, i)) if mode == "tn" else pl.BlockSpec((tm, tk), lambda j, i, k: (i, k))
    b_spec = pl.BlockSpec((tn, tk), lambda j, i, k: (j, k)) if mode == "nt" else pl.BlockSpec((tk, tn), lambda j, i, k: (k, j))
    o_spec = pl.BlockSpec((tm, tn), lambda j, i, k: (i, j))
    dims = _DIMS[mode]
    has_add = addend is not None

    def body(*refs):
        a_ref, b_ref = refs[:2]
        r_ref = refs[2] if has_add else None
        o_ref = refs[2 + has_add]

        def finish(acc):
            r = acc * scale if scale != 1.0 else acc
            if has_add:
                r = r + r_ref[...].astype(F32)
            o_ref[...] = r.astype(o_ref.dtype)

        prod = lax.dot_general(a_ref[...].astype(BF16), b_ref[...].astype(BF16), dims, preferred_element_type=F32)
        if nk == 1:
            finish(prod)
            return
        acc_ref = refs[-1]
        k = pl.program_id(2)

        @pl.when(k == 0)
        def _():
            acc_ref[...] = prod

        @pl.when(k > 0)
        def _():
            acc_ref[...] += prod

        @pl.when(k == nk - 1)
        def _():
            finish(acc_ref[...])

    in_specs = [a_spec, b_spec] + ([o_spec] if has_add else [])
    args = (a, b) + ((addend,) if has_add else ())
    return pl.pallas_call(
        body, name=name, grid=(nj, ni, nk), in_specs=in_specs, out_specs=o_spec,
        out_shape=jax.ShapeDtypeStruct((m_dim, n_dim), out_dtype),
        scratch_shapes=[] if nk == 1 else [pltpu.VMEM((tm, tn), F32)],
        compiler_params=_cparams("parallel", "parallel", "arbitrary"),
    )(*args)


ROW_STRIP = 16


def _strips(tm, fn, init=None, rb=ROW_STRIP):
    def step(i, carry):
        return fn(pl.ds(pl.multiple_of(i * rb, rb), rb), carry)
    return lax.fori_loop(0, tm // rb, step, init, unroll=2)


def _rms_fwd(x, gain, name):
    t, d = x.shape
    tm = _row_tile(t)

    def body(x_ref, g_ref, o_ref):
        xv = x_ref[...]
        r = lax.rsqrt(jnp.mean(xv * xv, axis=-1, keepdims=True) + EPS)
        o_ref[...] = (xv * r * g_ref[...]).astype(o_ref.dtype)

    return pl.pallas_call(
        body, name=name, grid=(t // tm,),
        in_specs=[pl.BlockSpec((tm, d), lambda i: (i, 0)), pl.BlockSpec((1, d), lambda i: (0, 0))],
        out_specs=pl.BlockSpec((tm, d), lambda i: (i, 0)),
        out_shape=jax.ShapeDtypeStruct((t, d), BF16), compiler_params=_cparams("parallel"),
    )(x, gain)


def _rms_bwd(x, gain, dn, dres, name):
    t, d = x.shape
    tm = _row_tile(t)
    has_res = dres is not None

    def body(*refs):
        if has_res:
            x_ref, g_ref, dn_ref, r_ref, dx_ref, dg_ref = refs
        else:
            x_ref, g_ref, dn_ref, dx_ref, dg_ref = refs

        @pl.when(pl.program_id(0) == 0)
        def _():
            dg_ref[...] = jnp.zeros_like(dg_ref)

        xv = x_ref[...]
        r = lax.rsqrt(jnp.mean(xv * xv, axis=-1, keepdims=True) + EPS)
        xh = xv * r
        dnv = dn_ref[...].astype(F32)
        dg_ref[...] += jnp.sum(dnv * xh, axis=0, keepdims=True)
        dxh = dnv * g_ref[...]
        dx = r * (dxh - xh * jnp.mean(dxh * xh, axis=-1, keepdims=True))
        if has_res:
            dx = dx + r_ref[...]
        dx_ref[...] = dx

    row = pl.BlockSpec((tm, d), lambda i: (i, 0))
    vec = pl.BlockSpec((1, d), lambda i: (0, 0))
    in_specs = [row, vec, row] + ([row] if has_res else [])
    args = (x, gain, dn) + ((dres,) if has_res else ())
    return pl.pallas_call(
        body, name=name, grid=(t // tm,), in_specs=in_specs, out_specs=[row, vec],
        out_shape=[jax.ShapeDtypeStruct((t, d), F32), jax.ShapeDtypeStruct((1, d), F32)],
        compiler_params=_cparams("arbitrary"),
    )(*args)


def _loss_head(h, gain, target, name):
    t, d = h.shape
    tm = _row_tile(t)

    def body(h_ref, g_ref, t_ref, dh_ref, dg_ref, l_ref):
        @pl.when(pl.program_id(0) == 0)
        def _():
            dg_ref[...] = jnp.zeros_like(dg_ref)
            l_ref[...] = jnp.zeros_like(l_ref)

        xv = h_ref[...]
        g = g_ref[...]
        r = lax.rsqrt(jnp.mean(xv * xv, axis=-1, keepdims=True) + EPS)
        xh = xv * r
        err = xh * g - t_ref[...]
        l_ref[...] += 0.5 * jnp.sum(jnp.mean(err * err, axis=-1, keepdims=True), axis=0, keepdims=True)
        dy = err * (1.0 / d)
        dg_ref[...] += jnp.sum(dy * xh, axis=0, keepdims=True)
        dxh = dy * g
        dh_ref[...] = r * (dxh - xh * jnp.mean(dxh * xh, axis=-1, keepdims=True))

    row = pl.BlockSpec((tm, d), lambda i: (i, 0))
    vec = pl.BlockSpec((1, d), lambda i: (0, 0))
    return pl.pallas_call(
        body, name=name, grid=(t // tm,), in_specs=[row, vec, row],
        out_specs=[row, vec, pl.BlockSpec((1, 128), lambda i: (0, 0))],
        out_shape=[jax.ShapeDtypeStruct((t, d), F32), jax.ShapeDtypeStruct((1, d), F32), jax.ShapeDtypeStruct((1, 128), F32)],
        compiler_params=_cparams("arbitrary"),
    )(h, gain, target)


FF_TILE = 1408


def _ffn_fwd(n, x, wg, wu, wd, name):
    t, d = x.shape
    tm, tn = _row_tile(t), FF_TILE
    nj = D_FF // tn

    def body(n_ref, x_ref, wg_ref, wu_ref, wd_ref, h_ref, g_ref, u_ref, acc_ref):
        j = pl.program_id(1)

        @pl.when(j == 0)
        def _():
            acc_ref[...] = jnp.zeros_like(acc_ref)

        nb = n_ref[...]
        g = lax.dot_general(nb, wg_ref[...], _DIMS["nt"], preferred_element_type=F32)
        u = lax.dot_general(nb, wu_ref[...], _DIMS["nt"], preferred_element_type=F32)
        g_ref[...] = g.astype(BF16)
        u_ref[...] = u.astype(BF16)
        a = g * _sigmoid(g) * u
        acc_ref[...] += jnp.dot(a.astype(BF16), wd_ref[...], preferred_element_type=F32)

        @pl.when(j == nj - 1)
        def _():
            h_ref[...] = x_ref[...] + 0.5 * acc_ref[...]

    row = pl.BlockSpec((tm, d), lambda i, j: (i, 0))
    act = pl.BlockSpec((tm, tn), lambda i, j: (i, j))
    return pl.pallas_call(
        body, name=name, grid=(t // tm, nj),
        in_specs=[row, row] + [pl.BlockSpec((tn, d), lambda i, j: (j, 0))] * 3,
        out_specs=[row, act, act],
        out_shape=[jax.ShapeDtypeStruct((t, d), F32), jax.ShapeDtypeStruct((t, D_FF), BF16), jax.ShapeDtypeStruct((t, D_FF), BF16)],
        scratch_shapes=[pltpu.VMEM((tm, d), F32)], compiler_params=_cparams("parallel", "arbitrary"),
    )(n, x, wg, wu, wd)


def _ffn_bwd_act(dh, g, u, wg, wu, wd, name):
    t, d = dh.shape
    tm, tn = _row_tile(t), FF_TILE
    nj = D_FF // tn

    def body(dh_ref, g_ref, u_ref, wg_ref, wu_ref, wd_ref, dn_ref, dg_ref, du_ref, a_ref, acc_ref):
        j = pl.program_id(1)

        @pl.when(j == 0)
        def _():
            acc_ref[...] = jnp.zeros_like(acc_ref)

        dhb = (0.5 * dh_ref[...]).astype(BF16)
        da = lax.dot_general(dhb, wd_ref[...], _DIMS["nt"], preferred_element_type=F32)
        gv = g_ref[...].astype(F32)
        uv = u_ref[...].astype(F32)
        sg = _sigmoid(gv)
        s = gv * sg
        dg = (da * uv * (sg * (1.0 + gv * (1.0 - sg)))).astype(BF16)
        du = (da * s).astype(BF16)
        dg_ref[...] = dg
        du_ref[...] = du
        a_ref[...] = (s * uv).astype(BF16)
        acc_ref[...] += (jnp.dot(dg, wg_ref[...], preferred_element_type=F32)
                         + jnp.dot(du, wu_ref[...], preferred_element_type=F32))

        @pl.when(j == nj - 1)
        def _():
            dn_ref[...] = acc_ref[...]

    row = pl.BlockSpec((tm, d), lambda i, j: (i, 0))
    act = pl.BlockSpec((tm, tn), lambda i, j: (i, j))
    return pl.pallas_call(
        body, name=name, grid=(t // tm, nj),
        in_specs=[row, act, act] + [pl.BlockSpec((tn, d), lambda i, j: (j, 0))] * 3,
        out_specs=[row, act, act, act],
        out_shape=[jax.ShapeDtypeStruct((t, d), F32)] + [jax.ShapeDtypeStruct((t, D_FF), BF16)] * 3,
        scratch_shapes=[pltpu.VMEM((tm, d), F32)], compiler_params=_cparams("parallel", "arbitrary"),
    )(dh, g, u, wg, wu, wd)


def _ffn_forward(x, gain, weights, tag):
    n = _rms_fwd(x, gain, f"{tag}_norm")
    h, g, u = _ffn_fwd(n, x, *weights(n), f"{tag}_fwd")
    return h, (n, g, u)


def _ffn_backward(dh, x, gain, wg, wu, wd, saved, tag, push):
    n, g, u = saved
    dn, dg, du, a = _ffn_bwd_act(dh, g, u, wg, wu, wd, f"{tag}_bwd_act")
    kw = dict(mode="tn", out_dtype=BF16, tm=FF_TILE, tn=1024, tk=2048)
    d_wg = _matmul(dg, n, name=f"{tag}_dwg", **kw)
    d_wu = _matmul(du, n, name=f"{tag}_dwu", **kw)
    d_wd = _matmul(a, dh, scale=0.5, name=f"{tag}_dwd", **kw)
    token = push({f"{tag}_w_gate": d_wg, f"{tag}_w_up": d_wu, f"{tag}_w_down": d_wd})
    return _rms_bwd(x, gain + token, dn, dh, f"{tag}_norm_bwd")


CONV_COLS = 512
HALO = 8
CONV_STRIP = 32
CONV_ROWS = 1024


def _conv_fwd(p, w, b, name):
    t = p.shape[0]
    tm = min(CONV_ROWS, t)
    c0 = P_XBC // CONV_COLS

    def body(x_ref, halo_ref, w_ref, b_ref, o_ref, s_ref):
        i = pl.program_id(1)
        s_ref[0:HALO, :] = jnp.where(i > 0, halo_ref[...].astype(F32), 0.0)
        s_ref[HALO:HALO + tm, :] = x_ref[...].astype(F32)
        wv = w_ref[...]
        bv = b_ref[...]
        for r0 in range(0, tm, CONV_STRIP):
            acc = bv + wv[0:1, :] * s_ref[HALO - 3 + r0:HALO - 3 + r0 + CONV_STRIP, :]
            for k in range(1, SSD_CONV):
                acc = acc + wv[k:k + 1, :] * s_ref[HALO - 3 + k + r0:HALO - 3 + k + r0 + CONV_STRIP, :]
            o_ref[r0:r0 + CONV_STRIP, :] = (acc * _sigmoid(acc)).astype(o_ref.dtype)

    return pl.pallas_call(
        body, name=name, grid=(XBC // CONV_COLS, t // tm),
        in_specs=[pl.BlockSpec((tm, CONV_COLS), lambda j, i: (i, c0 + j)),
                  pl.BlockSpec((HALO, CONV_COLS), lambda j, i: (jnp.maximum(i * (tm // HALO) - 1, 0), c0 + j)),
                  pl.BlockSpec((SSD_CONV, CONV_COLS), lambda j, i: (0, j)), pl.BlockSpec((1, CONV_COLS), lambda j, i: (0, j))],
        out_specs=pl.BlockSpec((tm, CONV_COLS), lambda j, i: (i, j)),
        out_shape=jax.ShapeDtypeStruct((t, XBC), BF16),
        scratch_shapes=[pltpu.VMEM((tm + HALO, CONV_COLS), F32)], compiler_params=_cparams("parallel", "parallel"),
    )(p, p, w, b)


def _conv_bwd_act(p, dy, w, b, col0, name):
    t, cols = dy.shape
    tm = min(CONV_ROWS, t)
    c0 = (P_XBC + col0) // CONV_COLS
    w0 = col0 // CONV_COLS

    def body(x_ref, halo_ref, dy_ref, w_ref, b_ref, da_ref, dw_ref, db_ref, s_ref):
        i = pl.program_id(1)

        @pl.when(i == 0)
        def _():
            dw_ref[...] = jnp.zeros_like(dw_ref)
            db_ref[...] = jnp.zeros_like(db_ref)

        s_ref[0:HALO, :] = jnp.where(i > 0, halo_ref[...].astype(F32), 0.0)
        s_ref[HALO:HALO + tm, :] = x_ref[...].astype(F32)
        wv = w_ref[...]
        bv = b_ref[...]
        fold = lambda v: jnp.sum(v.reshape(CONV_STRIP // 8, 8, CONV_COLS), axis=0)
        sums = [jnp.zeros((8, CONV_COLS), F32) for _ in range(SSD_CONV + 1)]
        for r0 in range(0, tm, CONV_STRIP):
            taps = [s_ref[HALO - 3 + k + r0:HALO - 3 + k + r0 + CONV_STRIP, :] for k in range(SSD_CONV)]
            acc = bv + wv[0:1, :] * taps[0]
            for k in range(1, SSD_CONV):
                acc = acc + wv[k:k + 1, :] * taps[k]
            sg = _sigmoid(acc)
            dacc = dy_ref[r0:r0 + CONV_STRIP, :].astype(F32) * (sg * (1.0 + acc * (1.0 - sg)))
            da_ref[r0:r0 + CONV_STRIP, :] = dacc.astype(BF16)
            for k in range(SSD_CONV):
                sums[k] = sums[k] + fold(dacc * taps[k])
            sums[SSD_CONV] = sums[SSD_CONV] + fold(dacc)
        for k in range(SSD_CONV):
            dw_ref[k:k + 1, :] += jnp.sum(sums[k], axis=0, keepdims=True)
        db_ref[...] += jnp.sum(sums[SSD_CONV], axis=0, keepdims=True)

    return pl.pallas_call(
        body, name=name, grid=(cols // CONV_COLS, t // tm),
        in_specs=[pl.BlockSpec((tm, CONV_COLS), lambda j, i: (i, c0 + j)),
                  pl.BlockSpec((HALO, CONV_COLS), lambda j, i: (jnp.maximum(i * (tm // HALO) - 1, 0), c0 + j)),
                  pl.BlockSpec((tm, CONV_COLS), lambda j, i: (i, j)),
                  pl.BlockSpec((SSD_CONV, CONV_COLS), lambda j, i: (0, w0 + j)), pl.BlockSpec((1, CONV_COLS), lambda j, i: (0, w0 + j))],
        out_specs=[pl.BlockSpec((tm, CONV_COLS), lambda j, i: (i, j)), pl.BlockSpec((SSD_CONV, CONV_COLS), lambda j, i: (0, j)),
                   pl.BlockSpec((1, CONV_COLS), lambda j, i: (0, j))],
        out_shape=[jax.ShapeDtypeStruct((t, cols), BF16), jax.ShapeDtypeStruct((SSD_CONV, cols), F32), jax.ShapeDtypeStruct((1, cols), F32)],
        scratch_shapes=[pltpu.VMEM((tm + HALO, CONV_COLS), F32)], compiler_params=_cparams("parallel", "arbitrary"),
    )(p, p, dy, w, b)


def _conv_bwd_dx(dacc, w, col0, dp, name):
    t, cols = dacc.shape
    tm = min(CONV_ROWS, t)
    nt = t // tm
    w0 = col0 // CONV_COLS
    c0 = (P_XBC + col0) // CONV_COLS

    def body(d_ref, halo_ref, w_ref, dp_ref, o_ref, s_ref):
        del dp_ref
        i = pl.program_id(1)
        s_ref[0:tm, :] = d_ref[...].astype(F32)
        s_ref[tm:tm + HALO, :] = jnp.where(i < nt - 1, halo_ref[...].astype(F32), 0.0)
        wv = w_ref[...]
        for r0 in range(0, tm, CONV_STRIP):
            acc = wv[3:4, :] * s_ref[r0:r0 + CONV_STRIP, :]
            for k in range(SSD_CONV - 1):
                acc = acc + wv[k:k + 1, :] * s_ref[3 - k + r0:3 - k + r0 + CONV_STRIP, :]
            o_ref[r0:r0 + CONV_STRIP, :] = acc.astype(o_ref.dtype)

    return pl.pallas_call(
        body, name=name, grid=(cols // CONV_COLS, nt),
        in_specs=[pl.BlockSpec((tm, CONV_COLS), lambda j, i: (i, j)),
                  pl.BlockSpec((HALO, CONV_COLS), lambda j, i: (jnp.minimum((i + 1) * (tm // HALO), t // HALO - 1), j)),
                  pl.BlockSpec((SSD_CONV, CONV_COLS), lambda j, i: (0, w0 + j)), ANY],
        out_specs=pl.BlockSpec((tm, CONV_COLS), lambda j, i: (i, c0 + j)),
        out_shape=jax.ShapeDtypeStruct(dp.shape, dp.dtype), input_output_aliases={3: 0},
        scratch_shapes=[pltpu.VMEM((tm + HALO, CONV_COLS), F32)], compiler_params=_cparams("parallel", "parallel"),
    )(dacc, dacc, w, dp)


GROUP_COLS = SSD_INNER // SSD_GROUPS
PAIRS = GROUP_COLS // 128
HEADS_PER_GROUP = SSD_HEADS // SSD_GROUPS


def _dt_fwd(dt_raw, bias, name):
    t, n = dt_raw.shape
    tm = _row_tile(t)

    def body(x_ref, b_ref, o_ref):
        v = x_ref[...] + b_ref[...]
        o_ref[...] = jnp.maximum(v, 0.0) + jnp.log1p(jnp.exp(-jnp.abs(v)))

    row = pl.BlockSpec((tm, n), lambda i: (i, 0))
    vec = pl.BlockSpec((1, n), lambda i: (0, 0))
    return pl.pallas_call(body, name=name, grid=(t // tm,), in_specs=[row, vec], out_specs=row,
                          out_shape=jax.ShapeDtypeStruct((t, n), F32), compiler_params=_cparams("parallel"))(dt_raw, bias)


def _dt_bwd(ddt, dt_raw, bias, name):
    t, n = dt_raw.shape
    tm = _row_tile(t)

    def body(d_ref, x_ref, b_ref, o_ref, db_ref):
        @pl.when(pl.program_id(0) == 0)
        def _():
            db_ref[...] = jnp.zeros_like(db_ref)

        dr = d_ref[...] * _sigmoid(x_ref[...] + b_ref[...])
        o_ref[...] = dr.astype(o_ref.dtype)
        db_ref[...] += jnp.sum(dr, axis=0, keepdims=True)

    row = pl.BlockSpec((tm, n), lambda i: (i, 0))
    vec = pl.BlockSpec((1, n), lambda i: (0, 0))
    return pl.pallas_call(body, name=name, grid=(t // tm,), in_specs=[row, row, vec], out_specs=[row, vec],
                          out_shape=[jax.ShapeDtypeStruct((t, n), BF16), jax.ShapeDtypeStruct((1, n), F32)],
                          compiler_params=_cparams("arbitrary"))(ddt, dt_raw, bias)


SSD_STEP = 4


def _ssd_common(dt, dtt, a_log_w, a_log_t):
    l = CHUNK
    a = -jnp.exp(a_log_w)
    at = -jnp.exp(a_log_t)
    rowi = lax.broadcasted_iota(jnp.int32, (l, l), 0)
    coli = lax.broadcasted_iota(jnp.int32, (l, l), 1)
    tri = rowi >= coli
    lower = tri.astype(F32)
    upper = (rowi <= coli).astype(F32)
    acs = jnp.dot(lower, dt * a, precision=HI, preferred_element_type=F32)
    acst = jnp.dot(dtt * at, upper, precision=HI, preferred_element_type=F32)
    return a, acs, acst, jnp.where(tri, 0.0, -1e30), upper


def _pair_bc(w, lo, p):
    return jnp.where(lo, w[:, 2 * p:2 * p + 1], w[:, 2 * p + 1:2 * p + 2])


def _ssd_specs(t):
    rows = SSD_STEP * CHUNK
    assert t % rows == 0
    return t // rows, dict(
        xs=lambda cm: pl.BlockSpec((rows, GROUP_COLS), lambda g, c: (cm(c), g)),
        bm=lambda cm: pl.BlockSpec((rows, SSD_STATE), lambda g, c: (cm(c), SSD_INNER // SSD_STATE + g)),
        cmat=lambda cm: pl.BlockSpec((rows, SSD_STATE), lambda g, c: (cm(c), SSD_INNER // SSD_STATE + SSD_GROUPS + g)),
        dtw=lambda cm: pl.BlockSpec((1, rows, 128), lambda g, c: (g, cm(c), 0)),
        dtt=lambda cm: pl.BlockSpec((1, HEADS_PER_GROUP, rows), lambda g, c: (g, 0, cm(c))),
        wide=lambda cm: pl.BlockSpec((1, 1, 128), lambda g, c: (g, 0, 0)),
        tall=lambda cm: pl.BlockSpec((1, HEADS_PER_GROUP, 1), lambda g, c: (g, 0, 0)),
        grp=lambda cm: pl.BlockSpec((rows, GROUP_COLS), lambda g, c: (cm(c), g)),
        zp=lambda cm: pl.BlockSpec((rows, GROUP_COLS), lambda g, c: (cm(c), P_Z // GROUP_COLS + g)),
        vec=lambda cm: pl.BlockSpec((1, GROUP_COLS), lambda g, c: (0, g)),
        state=lambda cm: pl.BlockSpec((1, SSD_STEP, PAIRS, SSD_STATE, 128), lambda g, c: (g, cm(c), 0, 0, 0)),
    )


def _ssd_fwd(xc, p, hv, norm_g, name):
    t = xc.shape[0]
    nc, sp = _ssd_specs(t)
    ident = lambda c: c

    def body(xs_ref, b_ref, c_ref, dtw_ref, dtt_ref, aw_ref, at_ref, dk_ref, z_ref, ng_ref,
             y_ref, ys_ref, h_ref, st_ref):
        @pl.when(pl.program_id(1) == 0)
        def _():
            st_ref[...] = jnp.zeros_like(st_ref)

        lo = lax.broadcasted_iota(jnp.int32, (1, 128), 1) < 64
        dskip = dk_ref[0]
        for s in range(SSD_STEP):
            rows = slice(s * CHUNK, (s + 1) * CHUNK)
            dt = dtw_ref[0, rows, :]
            a, acs, acst, causal, _ = _ssd_common(dt, dtt_ref[0, :, rows], aw_ref[0], at_ref[0])
            ecs = jnp.exp(acs)
            alast = acs[CHUNK - 1:CHUNK, :]
            bmat, cmat = b_ref[rows, :], c_ref[rows, :]
            cb = lax.dot_general(cmat, bmat, _DIMS["nt"], preferred_element_type=F32)
            for pi in range(PAIRS):
                cols = slice(pi * 128, (pi + 1) * 128)
                x = xs_ref[rows, cols].astype(F32)
                xdt = x * _pair_bc(dt, lo, pi)
                xdtb = xdt.astype(BF16)
                ydiag = jnp.zeros((CHUNK, 128), F32)
                for r, mask in ((2 * pi, lo), (2 * pi + 1, jnp.logical_not(lo))):
                    lam = jnp.exp(acs[:, r:r + 1] - acst[r:r + 1, :] + causal)
                    m = (cb * lam).astype(BF16)
                    ydiag = ydiag + jnp.dot(m, jnp.where(mask, xdtb, 0), preferred_element_type=F32)
                ht = st_ref[pi]
                h_ref[0, s, pi] = ht
                yoff = jnp.dot(cmat, ht.astype(BF16), preferred_element_type=F32) * _pair_bc(ecs, lo, pi)
                y_ref[rows, cols] = (ydiag + yoff + _pair_bc(dskip, lo, pi) * x).astype(y_ref.dtype)
                alp = _pair_bc(alast, lo, pi)
                e = jnp.exp(alp - _pair_bc(acs, lo, pi))
                st = lax.dot_general(bmat, (xdt * e).astype(BF16), _DIMS["tn"], preferred_element_type=F32)
                st_ref[pi] = ht * jnp.exp(alp) + st
            zf = z_ref[rows, :].astype(F32)
            yg = y_ref[rows, :].astype(F32) * (zf * _sigmoid(zf))
            rstd = lax.rsqrt(jnp.mean(yg * yg, axis=-1, keepdims=True) + EPS)
            ys_ref[rows, :] = (yg * rstd * ng_ref[...]).astype(ys_ref.dtype)

    ins = ["xs", "bm", "cmat", "dtw", "dtt", "wide", "tall", "wide", "zp", "vec"]
    return pl.pallas_call(
        body, name=name, grid=(SSD_GROUPS, nc),
        in_specs=[sp[k](ident) for k in ins],
        out_specs=[sp["grp"](ident), sp["grp"](ident), sp["state"](ident)],
        out_shape=[jax.ShapeDtypeStruct((t, SSD_INNER), BF16), jax.ShapeDtypeStruct((t, SSD_INNER), BF16),
                   jax.ShapeDtypeStruct((SSD_GROUPS, t // CHUNK, PAIRS, SSD_STATE, 128), F32)],
        scratch_shapes=[pltpu.VMEM((PAIRS, SSD_STATE, 128), F32)], compiler_params=_cparams("parallel", "arbitrary"),
    )(xc, xc, xc, hv["dtw"], hv["dtt"], hv["alog_w"], hv["alog_t"], hv["dskip_w"], p, norm_g)


def _ssd_bwd(xc, p, hv, norm_g, y, dys, states, dp, name):
    t = xc.shape[0]
    nc, sp = _ssd_specs(t)
    rev = lambda c: nc - 1 - c

    def body(xs_ref, b_ref, c_ref, dtw_ref, dtt_ref, aw_ref, at_ref, dk_ref, z_ref, ng_ref,
             y_ref, dys_ref, h_ref, dp_ref,
             dxs_ref, db_ref, dc_ref, dz_ref, ddt_ref, hsum_ref, dng_ref, dst_ref):
        del dp_ref
        @pl.when(pl.program_id(1) == 0)
        def _():
            dst_ref[...] = jnp.zeros_like(dst_ref)
            hsum_ref[...] = jnp.zeros_like(hsum_ref)
            dng_ref[...] = jnp.zeros_like(dng_ref)

        lane = lax.broadcasted_iota(jnp.int32, (1, 128), 1)
        lo = lane < 64
        dskip = dk_ref[0]
        sel_r = lax.broadcasted_iota(jnp.int32, (128, 128), 0)
        sel_c = lax.broadcasted_iota(jnp.int32, (128, 128), 1)
        refs = (xs_ref, b_ref, c_ref, dtw_ref, dtt_ref, aw_ref, at_ref, dk_ref, z_ref, ng_ref, y_ref, dys_ref, h_ref,
                dxs_ref, db_ref, dc_ref, dz_ref, ddt_ref, hsum_ref, dng_ref, dst_ref)
        for s in reversed(range(SSD_STEP)):
            chunk_bwd(refs, slice(s * CHUNK, (s + 1) * CHUNK), s, lane, lo, dskip, sel_r, sel_c)

    def chunk_bwd(refs, rows, s, lane, lo, dskip, sel_r, sel_c):
        (xs_ref, b_ref, c_ref, dtw_ref, dtt_ref, aw_ref, at_ref, dk_ref, z_ref, ng_ref, y_ref, dys_ref, h_ref,
         dxs_ref, db_ref, dc_ref, dz_ref, ddt_ref, hsum_ref, dng_ref, dst_ref) = refs
        dt = dtw_ref[0, rows, :]
        a, acs, acst, causal, upper = _ssd_common(dt, dtt_ref[0, :, rows], aw_ref[0], at_ref[0])
        ecs = jnp.exp(acs)
        alast = acs[CHUNK - 1:CHUNK, :]
        bmat, cmat = b_ref[rows, :], c_ref[rows, :]
        cb = lax.dot_general(cmat, bmat, _DIMS["nt"], preferred_element_type=F32)

        zf = z_ref[rows, :].astype(F32)
        sg = _sigmoid(zf)
        sz = zf * sg
        yv = y_ref[rows, :].astype(F32)
        yg = yv * sz
        rstd = lax.rsqrt(jnp.mean(yg * yg, axis=-1, keepdims=True) + EPS)
        yhat = yg * rstd
        dysv = dys_ref[rows, :].astype(F32)
        dng_ref[...] += jnp.sum(dysv * yhat, axis=0, keepdims=True)
        dyh = dysv * ng_ref[...]
        dyg = rstd * (dyh - yhat * jnp.mean(dyh * yhat, axis=-1, keepdims=True))
        dz_ref[rows, :] = (dyg * yv * (sg * (1.0 + zf * (1.0 - sg)))).astype(dz_ref.dtype)
        dy_all = dyg * sz

        dal = jnp.zeros((CHUNK, 128), F32)
        ddtm = jnp.zeros((CHUNK, 128), F32)
        dalast = jnp.zeros((8, 128), F32)
        ddsk = jnp.zeros((8, 128), F32)
        dcb = jnp.zeros((CHUNK, CHUNK), F32)
        qcol = jnp.zeros((8, CHUNK), F32)
        sub8 = lax.broadcasted_iota(jnp.int32, (8, CHUNK), 0)
        dc_acc = jnp.zeros((CHUNK, SSD_STATE), F32)
        db_acc = jnp.zeros((CHUNK, SSD_STATE), F32)
        for pi in range(PAIRS):
            sel = (sel_c == 2 * pi + (sel_r >= 64).astype(jnp.int32)).astype(BF16)

            def hsum(v, sel=sel):
                return jnp.dot(v.astype(BF16), sel, preferred_element_type=F32)

            dyp = dy_all[:, pi * 128:(pi + 1) * 128]
            x = xs_ref[rows, pi * 128:(pi + 1) * 128].astype(F32)
            dtp = _pair_bc(dt, lo, pi)
            xdt = x * dtp
            dxdt = jnp.zeros((CHUNK, 128), F32)
            dypb, xdtb = dyp.astype(BF16), xdt.astype(BF16)
            for r, mask in ((2 * pi, lo), (2 * pi + 1, jnp.logical_not(lo))):
                lam = jnp.exp(acs[:, r:r + 1] - acst[r:r + 1, :] + causal)
                m32 = cb * lam
                m = m32.astype(BF16)
                dyr = jnp.where(mask, dypb, 0)
                xr = jnp.where(mask, xdtb, 0)
                dm = lax.dot_general(dyr, xr, _DIMS["nt"], preferred_element_type=F32)
                dcb = dcb + dm * lam
                q = dm * m32
                dal = dal + jnp.sum(q, axis=1, keepdims=True) * (lane == r).astype(F32)
                qcol = qcol + jnp.where(sub8 == r, jnp.sum(q, axis=0, keepdims=True), 0.0)
                dxdt = dxdt + lax.dot_general(m, dyr, _DIMS["tn"], preferred_element_type=F32)
            ht = h_ref[0, s, pi]
            htb = ht.astype(BF16)
            ecp = _pair_bc(ecs, lo, pi)
            yoff = jnp.dot(cmat, htb, preferred_element_type=F32) * ecp
            dg = (dyp * ecp).astype(BF16)
            dc_acc = dc_acc + lax.dot_general(dg, htb, _DIMS["nt"], preferred_element_type=F32)
            dht = lax.dot_general(cmat, dg, _DIMS["tn"], preferred_element_type=F32)
            dal = dal + hsum(dyp * yoff)
            dhn = dst_ref[pi]
            dhnb = dhn.astype(BF16)
            alp = _pair_bc(alast, lo, pi)
            e = jnp.exp(alp - _pair_bc(acs, lo, pi))
            xe = xdt * e
            db_acc = db_acc + lax.dot_general(xe.astype(BF16), dhnb, _DIMS["nt"], preferred_element_type=F32)
            dxe = jnp.dot(bmat, dhnb, preferred_element_type=F32)
            dxdt = dxdt + dxe * e
            tt = hsum(dxe * xe)
            dal = dal - tt
            dec = jnp.exp(alp)
            dalast = dalast + jnp.sum(tt, axis=0, keepdims=True) + hsum(
                jnp.broadcast_to(jnp.sum(dhn * ht, axis=0, keepdims=True) * dec, (8, 128)))
            dst_ref[pi] = dht + dhn * dec
            dxs_ref[rows, pi * 128:(pi + 1) * 128] = (_pair_bc(dskip, lo, pi) * dyp + dxdt * dtp).astype(dxs_ref.dtype)
            ddtm = ddtm + hsum(dxdt * x)
            ddsk = ddsk + hsum(jnp.broadcast_to(jnp.sum(dyp * x, axis=0, keepdims=True), (8, 128)))
        rowi = lax.broadcasted_iota(jnp.int32, (CHUNK, 128), 0)
        qcol_w = lax.dot_general(jnp.concatenate([qcol, jnp.zeros((CHUNK - 8, CHUNK), F32)], axis=0), (sel_r == sel_c).astype(F32),
                                 _DIMS["tn"], precision=HI, preferred_element_type=F32)
        dal = dal - qcol_w + jnp.where(rowi == CHUNK - 1, dalast[0:1, :], 0.0)
        dda = jnp.dot(upper, dal, precision=HI, preferred_element_type=F32)
        ddt_ref[0, rows, :] = ddtm + dda * a
        hsum_ref[0, 1:2, :] += jnp.sum(dda * dt, axis=0, keepdims=True) * a
        hsum_ref[0, 2:3, :] += ddsk[0:1, :]
        dcbb = dcb.astype(BF16)
        dc_ref[rows, :] = (jnp.dot(dcbb, bmat, preferred_element_type=F32) + dc_acc).astype(dc_ref.dtype)
        db_ref[rows, :] = (lax.dot_general(dcbb, cmat, _DIMS["tn"], preferred_element_type=F32) + db_acc).astype(db_ref.dtype)

    ins = ["xs", "bm", "cmat", "dtw", "dtt", "wide", "tall", "wide", "zp", "vec", "grp", "grp", "state"]
    col = lambda: pl.BlockSpec((SSD_STEP * CHUNK, SSD_STATE), lambda g, c: (rev(c), g))
    return pl.pallas_call(
        body, name=name, grid=(SSD_GROUPS, nc),
        in_specs=[sp[k](rev) for k in ins] + [ANY],
        out_specs=[sp["grp"](rev), col(), col(), sp["zp"](rev), sp["dtw"](rev),
                   pl.BlockSpec((1, 8, 128), lambda g, c: (g, 0, 0)), sp["vec"](rev)],
        out_shape=[jax.ShapeDtypeStruct((t, SSD_INNER), BF16), jax.ShapeDtypeStruct((t, SSD_GROUPS * SSD_STATE), BF16),
                   jax.ShapeDtypeStruct((t, SSD_GROUPS * SSD_STATE), BF16), jax.ShapeDtypeStruct(dp.shape, dp.dtype),
                   jax.ShapeDtypeStruct((SSD_GROUPS, t, 128), F32), jax.ShapeDtypeStruct((SSD_GROUPS, 8, 128), F32),
                   jax.ShapeDtypeStruct((1, SSD_INNER), F32)],
        input_output_aliases={len(ins): 3},
        scratch_shapes=[pltpu.VMEM((PAIRS, SSD_STATE, 128), F32)], compiler_params=_cparams("parallel", "arbitrary"),
    )(xc, xc, xc, hv["dtw"], hv["dtt"], hv["alog_w"], hv["alog_t"], hv["dskip_w"], p, norm_g, y, dys, states, dp)


def _wide(v):
    return jnp.pad(v.reshape(SSD_GROUPS, 1, HEADS_PER_GROUP), ((0, 0), (0, 0), (0, 128 - HEADS_PER_GROUP)))


def _head_inputs(dt, a_log, d_skip):
    t = dt.shape[0]
    g = dt[:, :SSD_HEADS].reshape(t, SSD_GROUPS, HEADS_PER_GROUP)
    return dict(
        dtw=jnp.pad(jnp.transpose(g, (1, 0, 2)), ((0, 0), (0, 0), (0, 128 - HEADS_PER_GROUP))),
        dtt=jnp.transpose(g, (1, 2, 0)),
        alog_w=_wide(a_log), alog_t=a_log.reshape(SSD_GROUPS, HEADS_PER_GROUP, 1),
        dskip_w=_wide(d_skip),
    )


def _gelu(x):
    return 0.5 * x * (1.0 + lax.erf(x * (1.0 / math.sqrt(2.0))))


def _gelu_grad(x):
    return 0.5 * (1.0 + lax.erf(x * (1.0 / math.sqrt(2.0)))) + x * jnp.exp(-0.5 * x * x) * (1.0 / math.sqrt(2.0 * math.pi))


def _tril_mask():
    r = lax.broadcasted_iota(jnp.int32, (CHUNK, CHUNK), 0)
    c = lax.broadcasted_iota(jnp.int32, (CHUNK, CHUNK), 1)
    return r >= c


def _gmlp_fwd(p, v_gain, w_s, b_col, name):
    t = p.shape[0]
    tm = _row_tile(t)
    u0 = P_UV // GMLP_W

    def body(u_ref, v_ref, gn_ref, ws_ref, bs_ref, o_ref):
        v = _gelu(v_ref[...].astype(F32))
        v = (v * lax.rsqrt(jnp.mean(v * v, axis=-1, keepdims=True) + EPS) * gn_ref[...]).astype(BF16)
        tril = _tril_mask()
        wm = [jnp.where(tril, ws_ref[g], 0.0).astype(BF16) for g in range(GMLP_GROUPS)]
        for k in range(tm // CHUNK):
            rows = slice(k * CHUNK, (k + 1) * CHUNK)
            for g in range(GMLP_GROUPS):
                cols = slice(g * 128, (g + 1) * 128)
                mixed = jnp.dot(wm[g], v[rows, cols], preferred_element_type=F32) + bs_ref[g]
                o_ref[rows, cols] = (_gelu(u_ref[rows, cols].astype(F32)) * mixed).astype(o_ref.dtype)

    return pl.pallas_call(
        body, name=name, grid=(t // tm,),
        in_specs=[pl.BlockSpec((tm, GMLP_W), lambda i: (i, u0)), pl.BlockSpec((tm, GMLP_W), lambda i: (i, u0 + 1)),
                  pl.BlockSpec((1, GMLP_W), lambda i: (0, 0)), pl.BlockSpec((GMLP_GROUPS, CHUNK, CHUNK), lambda i: (0, 0, 0)),
                  pl.BlockSpec((GMLP_GROUPS, CHUNK, 1), lambda i: (0, 0, 0))],
        out_specs=pl.BlockSpec((tm, GMLP_W), lambda i: (i, 0)),
        out_shape=jax.ShapeDtypeStruct((t, GMLP_W), BF16), compiler_params=_cparams("parallel"),
    )(p, p, v_gain, w_s, b_col)


def _gmlp_bwd(p, dy, v_gain, w_s, b_col, dp, name):
    t = p.shape[0]
    tm = _row_tile(t)
    u0 = P_UV // GMLP_W

    def body(u_ref, v_ref, dy_ref, gn_ref, ws_ref, bs_ref, dp_ref, duv_ref, dws_ref, dbs_ref, dgn_ref, dvn_ref):
        del dp_ref

        @pl.when(pl.program_id(0) == 0)
        def _():
            dws_ref[...] = jnp.zeros_like(dws_ref)
            dbs_ref[...] = jnp.zeros_like(dbs_ref)
            dgn_ref[...] = jnp.zeros_like(dgn_ref)

        vraw = v_ref[...].astype(F32)
        va = _gelu(vraw)
        rstd = lax.rsqrt(jnp.mean(va * va, axis=-1, keepdims=True) + EPS)
        vhat = va * rstd
        gain = gn_ref[...]
        vn = (vhat * gain).astype(BF16)
        tril = _tril_mask()
        wm = [jnp.where(tril, ws_ref[g], 0.0).astype(BF16) for g in range(GMLP_GROUPS)]
        for k in range(tm // CHUNK):
            rows = slice(k * CHUNK, (k + 1) * CHUNK)
            for g in range(GMLP_GROUPS):
                cols = slice(g * 128, (g + 1) * 128)
                uraw = u_ref[rows, cols].astype(F32)
                vb = vn[rows, cols]
                mixed = jnp.dot(wm[g], vb, preferred_element_type=F32) + bs_ref[g]
                dyb = dy_ref[rows, cols].astype(F32)
                duv_ref[rows, cols] = (dyb * mixed * _gelu_grad(uraw)).astype(duv_ref.dtype)
                dmix = dyb * _gelu(uraw)
                dmb = dmix.astype(BF16)
                dws_ref[g] += jnp.where(tril, lax.dot_general(dmb, vb, _DIMS["nt"], preferred_element_type=F32), 0.0)
                dbs_ref[g] += jnp.sum(dmix, axis=1, keepdims=True)
                dvn_ref[rows, cols] = lax.dot_general(wm[g], dmb, _DIMS["tn"], preferred_element_type=F32)
        dvn = dvn_ref[...]
        dgn_ref[...] += jnp.sum(dvn * vhat, axis=0, keepdims=True)
        dvh = dvn * gain
        dva = rstd * (dvh - vhat * jnp.mean(dvh * vhat, axis=-1, keepdims=True))
        duv_ref[:, GMLP_W:2 * GMLP_W] = (dva * _gelu_grad(vraw)).astype(duv_ref.dtype)

    return pl.pallas_call(
        body, name=name, grid=(t // tm,),
        in_specs=[pl.BlockSpec((tm, GMLP_W), lambda i: (i, u0)), pl.BlockSpec((tm, GMLP_W), lambda i: (i, u0 + 1)),
                  pl.BlockSpec((tm, GMLP_W), lambda i: (i, 0)),
                  pl.BlockSpec((1, GMLP_W), lambda i: (0, 0)), pl.BlockSpec((GMLP_GROUPS, CHUNK, CHUNK), lambda i: (0, 0, 0)),
                  pl.BlockSpec((GMLP_GROUPS, CHUNK, 1), lambda i: (0, 0, 0)), ANY],
        out_specs=[pl.BlockSpec((tm, 2 * GMLP_W), lambda i: (i, P_UV // (2 * GMLP_W))),
                   pl.BlockSpec((GMLP_GROUPS, CHUNK, CHUNK), lambda i: (0, 0, 0)),
                   pl.BlockSpec((GMLP_GROUPS, CHUNK, 1), lambda i: (0, 0, 0)), pl.BlockSpec((1, GMLP_W), lambda i: (0, 0))],
        out_shape=[jax.ShapeDtypeStruct(dp.shape, dp.dtype), jax.ShapeDtypeStruct((GMLP_GROUPS, CHUNK, CHUNK), F32),
                   jax.ShapeDtypeStruct((GMLP_GROUPS, CHUNK, 1), F32), jax.ShapeDtypeStruct((1, GMLP_W), F32)],
        input_output_aliases={6: 0},
        scratch_shapes=[pltpu.VMEM((tm, GMLP_W), F32)], compiler_params=_cparams("arbitrary"),
    )(p, p, dy, v_gain, w_s, b_col, dp)


def _head_masks():
    lane = lax.broadcasted_iota(jnp.int32, (1, MEM_W), 1)
    return [(lane >= h * 64) & (lane < (h + 1) * 64) for h in range(MEM_HEADS)]


def _mem_fwd(p, kv, name):
    t = p.shape[0]
    tm = _row_tile(t)
    q0 = P_Q // MEM_W

    def body(q_ref, kv_ref, o_ref):
        q = q_ref[...]
        k = kv_ref[:, 0:MEM_W].astype(BF16)
        v = kv_ref[:, MEM_W:2 * MEM_W].astype(BF16)
        out = jnp.zeros((tm, MEM_W), F32)
        for mask in _head_masks():
            s = lax.dot_general(jnp.where(mask, q, 0), k, _DIMS["nt"], preferred_element_type=F32) * 0.125
            e = jnp.exp(s - jnp.max(s, axis=-1, keepdims=True))
            pr = (e * (1.0 / jnp.sum(e, axis=-1, keepdims=True))).astype(BF16)
            out = out + jnp.where(mask, jnp.dot(pr, v, preferred_element_type=F32), 0.0)
        o_ref[...] = out.astype(o_ref.dtype)

    return pl.pallas_call(
        body, name=name, grid=(t // tm,),
        in_specs=[pl.BlockSpec((tm, MEM_W), lambda i: (i, q0)), pl.BlockSpec((MEM_LEN, 2 * MEM_W), lambda i: (0, 0))],
        out_specs=pl.BlockSpec((tm, MEM_W), lambda i: (i, 0)),
        out_shape=jax.ShapeDtypeStruct((t, MEM_W), BF16), compiler_params=_cparams("parallel"),
    )(p, kv)


def _mem_bwd(p, kv, dy, dp, name):
    t = p.shape[0]
    tm = _row_tile(t)
    q0 = P_Q // MEM_W
    assert P_W - P_Q == 2 * MEM_W

    def body(q_ref, kv_ref, dy_ref, dp_ref, dq_ref, dkv_ref):
        del dp_ref

        @pl.when(pl.program_id(0) == 0)
        def _():
            dkv_ref[...] = jnp.zeros_like(dkv_ref)

        q = q_ref[...]
        dy = dy_ref[...]
        k = kv_ref[:, 0:MEM_W].astype(BF16)
        v = kv_ref[:, MEM_W:2 * MEM_W].astype(BF16)
        dq = jnp.zeros((tm, MEM_W), F32)
        dk = jnp.zeros((MEM_LEN, MEM_W), F32)
        dv = jnp.zeros((MEM_LEN, MEM_W), F32)
        for mask in _head_masks():
            qh = jnp.where(mask, q, 0)
            dyh = jnp.where(mask, dy, 0)
            s = lax.dot_general(qh, k, _DIMS["nt"], preferred_element_type=F32) * 0.125
            e = jnp.exp(s - jnp.max(s, axis=-1, keepdims=True))
            pr = e * (1.0 / jnp.sum(e, axis=-1, keepdims=True))
            prb = pr.astype(BF16)
            dp = lax.dot_general(dyh, v, _DIMS["nt"], preferred_element_type=F32)
            ds = (pr * (dp - jnp.sum(dp * pr, axis=-1, keepdims=True)) * 0.125).astype(BF16)
            dq = dq + jnp.where(mask, jnp.dot(ds, k, preferred_element_type=F32), 0.0)
            dk = dk + lax.dot_general(ds, qh, _DIMS["tn"], preferred_element_type=F32)
            dv = dv + lax.dot_general(prb, dyh, _DIMS["tn"], preferred_element_type=F32)
        dq_ref[:, 0:MEM_W] = dq.astype(dq_ref.dtype)
        dq_ref[:, MEM_W:2 * MEM_W] = jnp.zeros((tm, MEM_W), dq_ref.dtype)
        dkv_ref[:, 0:MEM_W] += dk
        dkv_ref[:, MEM_W:2 * MEM_W] += dv

    return pl.pallas_call(
        body, name=name, grid=(t // tm,),
        in_specs=[pl.BlockSpec((tm, MEM_W), lambda i: (i, q0)), pl.BlockSpec((MEM_LEN, 2 * MEM_W), lambda i: (0, 0)),
                  pl.BlockSpec((tm, MEM_W), lambda i: (i, 0)), ANY],
        out_specs=[pl.BlockSpec((tm, 2 * MEM_W), lambda i: (i, P_Q // (2 * MEM_W))),
                   pl.BlockSpec((MEM_LEN, 2 * MEM_W), lambda i: (0, 0))],
        out_shape=[jax.ShapeDtypeStruct(dp.shape, dp.dtype), jax.ShapeDtypeStruct((MEM_LEN, 2 * MEM_W), F32)],
        input_output_aliases={3: 0}, compiler_params=_cparams("arbitrary"),
    )(p, kv, dy, dp)


def _merge_fwd(p, b_ssd, b_gmlp, b_mem, name):
    t = p.shape[0]
    tm = _row_tile(t)
    g0 = P_GL // D_MODEL

    def body(g1, g2, g3, b1, b2, b3, o_ref):
        def strip(rows, carry):
            acc = _sigmoid(g1[rows, :].astype(F32)) * b1[rows, :].astype(F32)
            acc = acc + _sigmoid(g2[rows, :].astype(F32)) * b2[rows, :].astype(F32)
            acc = acc + _sigmoid(g3[rows, :].astype(F32)) * b3[rows, :].astype(F32)
            o_ref[rows, :] = acc.astype(o_ref.dtype)
            return carry

        _strips(tm, strip, 0)

    row = pl.BlockSpec((tm, D_MODEL), lambda i: (i, 0))
    return pl.pallas_call(
        body, name=name, grid=(t // tm,),
        in_specs=[pl.BlockSpec((tm, D_MODEL), lambda i, k=k: (i, g0 + k)) for k in range(3)] + [row] * 3,
        out_specs=row, out_shape=jax.ShapeDtypeStruct((t, D_MODEL), BF16), compiler_params=_cparams("parallel"),
    )(p, p, p, b_ssd, b_gmlp, b_mem)


def _merge_bwd(p, dm, b_ssd, b_gmlp, b_mem, dp, name):
    t = p.shape[0]
    tm = _row_tile(t)
    g0 = P_GL // D_MODEL

    def body(g1, g2, g3, dm_ref, b1, b2, b3, dp_ref, d1, d2, d3, dgl_ref):
        del dp_ref

        def strip(rows, carry):
            dmv = dm_ref[rows, :].astype(F32)
            for k, (g_ref, b_ref, d_ref) in enumerate(((g1, b1, d1), (g2, b2, d2), (g3, b3, d3))):
                sg = _sigmoid(g_ref[rows, :].astype(F32))
                d_ref[rows, :] = (dmv * sg).astype(d_ref.dtype)
                dgl_ref[rows, k * D_MODEL:(k + 1) * D_MODEL] = (dmv * b_ref[rows, :].astype(F32) * sg * (1.0 - sg)).astype(dgl_ref.dtype)
            return carry

        _strips(tm, strip, 0)

    row = pl.BlockSpec((tm, D_MODEL), lambda i: (i, 0))
    return pl.pallas_call(
        body, name=name, grid=(t // tm,),
        in_specs=[pl.BlockSpec((tm, D_MODEL), lambda i, k=k: (i, g0 + k)) for k in range(3)] + [row] * 4 + [ANY],
        out_specs=[row, row, row, pl.BlockSpec((tm, 3 * D_MODEL), lambda i: (i, P_GL // (3 * D_MODEL)))],
        out_shape=[jax.ShapeDtypeStruct((t, D_MODEL), BF16)] * 3 + [jax.ShapeDtypeStruct(dp.shape, dp.dtype)],
        input_output_aliases={7: 3}, compiler_params=_cparams("parallel"),
    )(p, p, p, dm, b_ssd, b_gmlp, b_mem, dp)


def _local_step(x, mem, target, w, first_weights, mixer_weights, second_weights, push):
    t = x.shape[0]
    mm = functools.partial(_matmul, tk=1024)

    ffn_weights = lambda tag: (w[f"{tag}_w_gate"], w[f"{tag}_w_up"], w[f"{tag}_w_down"])

    def arriving(tag, fetch):
        def weights(n):
            w.update(fetch(n))
            return ffn_weights(tag)
        return weights

    w = dict(w)
    h1, ffn1_saved = _ffn_forward(x, w["ffn1_norm"], arriving("ffn1", first_weights), "ffn1")
    w.update(mixer_weights(h1))
    n2 = _rms_fwd(h1, w["mix_norm"], "mix_norm")
    p = mm(n2, w["w_in_p"], mode="nt", out_dtype=BF16, tm=2048, tn=1536, name="in_proj")
    dt_raw = mm(n2, w["w_dt"], mode="nt", out_dtype=F32, tm=2048, tn=128, name="dt_proj")
    dt_bias = jnp.pad(w["ssd_dt_bias"], (0, 128 - SSD_HEADS)).reshape(1, 128)
    hv = _head_inputs(_dt_fwd(dt_raw, dt_bias, "dt_fwd"), w["ssd_a_log"], w["ssd_d"])
    xc = _conv_fwd(p, w["ssd_conv_w"], w["ssd_conv_b"], "conv_fwd")
    y_ssd_raw, y_ssd, states = _ssd_fwd(xc, p, hv, w["ssd_norm"], "ssd_fwd")
    b_col = w["gmlp_b_s"].reshape(GMLP_GROUPS, CHUNK, 1)
    y_gmlp = _gmlp_fwd(p, w["gmlp_v_norm"], w["gmlp_w_s"], b_col, "gmlp_fwd")
    mem_n = _rms_fwd(mem, w["mem_norm"], "mem_norm")
    kv = mm(mem_n, w["w_mem_kv"], mode="nn", out_dtype=F32, tm=256, tn=512, name="mem_kv")
    y_mem = _mem_fwd(p, kv, "mem_fwd")
    b_ssd = mm(y_ssd, w["w_branch_ssd"], mode="nn", out_dtype=BF16, tm=1024, tn=1024, tk=2048, name="branch_ssd")
    b_gmlp = mm(y_gmlp, w["w_branch_gmlp"], mode="nn", out_dtype=BF16, tm=1024, tn=1024, name="branch_gmlp")
    b_mem = mm(y_mem, w["w_branch_mem"], mode="nt", out_dtype=BF16, tm=2048, tn=1024, tk=MEM_W, name="branch_mem")
    merged = _merge_fwd(p, b_ssd, b_gmlp, b_mem, "merge_fwd")
    h2 = mm(merged, w["w_out"], mode="nn", out_dtype=F32, tm=1024, tn=1024, addend=h1, name="out_proj")
    h3, ffn2_saved = _ffn_forward(h2, w["ffn2_norm"], arriving("ffn2", second_weights), "ffn2")
    dh3, d_final, loss = _loss_head(h3, w["final_norm"], target, "loss_head")

    g = {"final_norm": d_final}
    big = {}
    dh2, g["ffn2_norm"] = _ffn_backward(dh3, h2, w["ffn2_norm"], w["ffn2_w_gate"], w["ffn2_w_up"], w["ffn2_w_down"], ffn2_saved,
                                        "ffn2", functools.partial(push, 0))
    dmerged = mm(dh2, w["w_out"], mode="nt", out_dtype=BF16, tm=1024, tn=1024, name="out_proj_dx")
    big["w_out"] = mm(merged, dh2, mode="tn", out_dtype=BF16, tm=1024, tn=1024, tk=2048, name="out_proj_dw")
    db_ssd, db_gmlp, db_mem, dp = _merge_bwd(p, dmerged, b_ssd, b_gmlp, b_mem, lax.empty((t, P_W), BF16), "merge_bwd")
    dy_ssd = mm(db_ssd, w["w_branch_ssd"], mode="nt", out_dtype=BF16, tm=1024, tn=2048, name="branch_ssd_dx")
    dy_gmlp = mm(db_gmlp, w["w_branch_gmlp"], mode="nt", out_dtype=BF16, tm=1024, tn=1024, name="branch_gmlp_dx")
    dy_mem = mm(db_mem, w["w_branch_mem"], mode="nn", out_dtype=BF16, tm=2048, tn=256, name="branch_mem_dx")
    big["w_branch_ssd"] = mm(y_ssd, db_ssd, mode="tn", out_dtype=BF16, tm=1024, tn=1024, tk=2048, name="branch_ssd_dw")
    big["w_branch_gmlp"] = mm(y_gmlp, db_gmlp, mode="tn", out_dtype=BF16, tm=1024, tn=1024, tk=2048, name="branch_gmlp_dw")
    big["w_branch_mem"] = mm(db_mem, y_mem, mode="tn", out_dtype=BF16, tm=1024, tn=256, tk=2048, name="branch_mem_dw")
    dp, dkv = _mem_bwd(p, kv, dy_mem, dp, "mem_bwd")
    big["w_mem_kv"] = mm(mem_n, dkv, mode="tn", out_dtype=BF16, tm=1024, tn=512, tk=256, name="mem_kv_dw")
    dmem_n = mm(dkv, w["w_mem_kv"], mode="nt", out_dtype=F32, tm=256, tn=1024, tk=512, name="mem_kv_dx")
    _, g["mem_norm"] = _rms_bwd(mem, w["mem_norm"], dmem_n, None, "mem_norm_bwd")
    dp, d_ws, d_bs, g["gmlp_v_norm"] = _gmlp_bwd(p, dy_gmlp, w["gmlp_v_norm"], w["gmlp_w_s"], b_col, dp, "gmlp_bwd")
    g["gmlp_w_s"] = d_ws
    g["gmlp_b_s"] = d_bs.reshape(GMLP_GROUPS, CHUNK)
    dxs, d_bm, d_cm, dp, ddt_w, hsums, g["ssd_norm"] = _ssd_bwd(xc, p, hv, w["ssd_norm"], y_ssd_raw, dy_ssd, states, dp, "ssd_bwd")
    heads = hsums[:, :, :HEADS_PER_GROUP]
    g["ssd_a_log"] = heads[:, 1, :].reshape(1, SSD_HEADS)
    g["ssd_d"] = heads[:, 2, :].reshape(1, SSD_HEADS)
    ddt = jnp.transpose(ddt_w[:, :, :HEADS_PER_GROUP], (1, 0, 2)).reshape(t, SSD_HEADS)
    ddt, d_bias = _dt_bwd(jnp.pad(ddt, ((0, 0), (0, 128 - SSD_HEADS))), dt_raw, dt_bias, "dt_bwd")
    g["ssd_dt_bias"] = d_bias[:, :SSD_HEADS]
    dws, dbs = [], []
    for dyc, col0, tag in ((dxs, 0, "x"), (d_bm, SSD_INNER, "b"), (d_cm, SSD_INNER + SSD_GROUPS * SSD_STATE, "c")):
        dacc, dw_c, db_c = _conv_bwd_act(p, dyc, w["ssd_conv_w"], w["ssd_conv_b"], col0, f"conv_bwd_act_{tag}")
        dp = _conv_bwd_dx(dacc, w["ssd_conv_w"], col0, dp, f"conv_bwd_dx_{tag}")
        dws.append(dw_c)
        dbs.append(db_c)
    g["ssd_conv_w"] = jnp.concatenate(dws, axis=1)
    g["ssd_conv_b"] = jnp.concatenate(dbs, axis=1)
    d_win_p = _matmul(dp, n2, mode="tn", out_dtype=BF16, tm=1536, tn=1024, tk=2048, name="in_proj_dw")
    d_wdt = mm(ddt, n2, mode="tn", out_dtype=BF16, tm=128, tn=1024, tk=2048, name="dt_proj_dw")
    sl = lambda a, o, n: a[o:o + n]
    big["w_in"] = jnp.concatenate([sl(d_win_p, P_Z, 2048), sl(d_win_p, P_XBC, XBC), d_wdt[:SSD_HEADS], sl(d_win_p, P_UV, 2048),
                                   sl(d_win_p, P_Q, MEM_W), sl(d_win_p, P_GL, 3 * D_MODEL)], axis=0)
    token = push(1, big)
    dn2 = mm(dp, w["w_in_p"], mode="nn", out_dtype=F32, tm=1024, tn=1024, tk=3584, name="in_proj_dx")
    dn2 = _matmul(ddt, w["w_dt"], mode="nn", out_dtype=F32, tm=1024, tn=1024, tk=128, addend=dn2, name="dt_proj_dx")
    dh1, g["mix_norm"] = _rms_bwd(h1, w["mix_norm"] + token, dn2, dh2, "mix_norm_bwd")
    dx, g["ffn1_norm"] = _ffn_backward(dh1, x, w["ffn1_norm"], w["ffn1_w_gate"], w["ffn1_w_up"], w["ffn1_w_down"], ffn1_saved,
                                       "ffn1", functools.partial(push, 2))
    return loss, dx, g


def _split_w_in(w_in_t):
    sl = lambda o, n: w_in_t[o:o + n]
    w_p = jnp.concatenate([sl(IN_GL, 3 * D_MODEL), sl(IN_Z, 2048), sl(IN_XBC, XBC), sl(IN_UV, 2048), sl(IN_Q, MEM_W),
                           jnp.zeros((P_W - P_USED, D_MODEL), w_in_t.dtype)], axis=0)
    w_dt = jnp.pad(sl(IN_DT, SSD_HEADS), ((0, 128 - SSD_HEADS), (0, 0)))
    return w_p, w_dt


def _pick_tile(rows, cap=512):
    best = None
    for tile in range(8, min(rows, cap) + 1, 8):
        if rows % tile == 0:
            best = tile
    return best if best is not None else rows


def _adamw(w, g, m, v, name):
    rows, lanes = w.shape
    tile = _pick_tile(rows, cap=max(8, (512 * 1024 // lanes) // 8 * 8))
    c1 = 1.0 / (1.0 - ADAM_B1 ** ADAM_STEP)
    c2 = 1.0 / (1.0 - ADAM_B2 ** ADAM_STEP)

    def body(w_ref, g_ref, m_ref, v_ref, d_ref, nm_ref, nv_ref):
        gv = g_ref[...]
        nm = ADAM_B1 * m_ref[...] + (1.0 - ADAM_B1) * gv
        nv = ADAM_B2 * v_ref[...] + (1.0 - ADAM_B2) * (gv * gv)
        nm_ref[...] = nm
        nv_ref[...] = nv
        d_ref[...] = -ADAM_LR * ((nm * c1) / (jnp.sqrt(nv * c2) + ADAM_EPS) + ADAM_WD * w_ref[...])

    blk = pl.BlockSpec((tile, lanes), lambda i: (i, 0))
    return pl.pallas_call(
        body, name=name, grid=(rows // tile,), in_specs=[blk] * 4, out_specs=[blk] * 3,
        out_shape=[jax.ShapeDtypeStruct((rows, lanes), F32)] * 3, compiler_params=_cparams("parallel"),
    )(w, g, m, v)


HBM = pl.BlockSpec(memory_space=pltpu.HBM)


def _place():
    x, y, c = lax.axis_index("x"), lax.axis_index("y"), lax.axis_index("c")
    chips = [(1 - x, y), (x, 1 - y), (1 - x, 1 - y)]
    return x, y, c, chips


SEM = pl.BlockSpec(memory_space=pltpu.SEMAPHORE)
EFFECT = pltpu.SideEffectType.DATAFLOW_SIDE_EFFECTING
N_PEER = 3


def _sem_outs():
    return tuple(pltpu.SemaphoreType.DMA(()) for _ in range(2 * N_PEER))


def _gather_start(slots, tag):
    def body(in_ref, *refs):
        del in_ref
        sems, thru, token = refs[:2 * N_PEER], refs[2 * N_PEER], refs[2 * N_PEER + 1]
        x, y, c, chips = _place()
        own = thru.at[2 * x + y, c]
        for j, chip in enumerate(chips):
            pltpu.make_async_remote_copy(src_ref=own, dst_ref=own, send_sem=sems[j], recv_sem=sems[N_PEER + j],
                                         device_id=(*chip, c), device_id_type=MESH).start()
        token[...] = jnp.zeros_like(token)

    out = pl.pallas_call(
        body, name=f"gather_{tag}_start",
        out_shape=_sem_outs() + (pltpu.HBM(slots.shape, slots.dtype), jax.ShapeDtypeStruct((8, 128), F32)),
        in_specs=(HBM,), out_specs=(SEM,) * (2 * N_PEER) + (HBM, pl.BlockSpec(memory_space=pltpu.VMEM)),
        input_output_aliases={0: 2 * N_PEER}, compiler_params=pltpu.CompilerParams(has_side_effects=EFFECT),
    )(pltpu.with_memory_space_constraint(slots, pltpu.HBM))
    return out[:2 * N_PEER], out[2 * N_PEER], out[2 * N_PEER + 1]


def _gather_wait(sems, thru, after, tag):
    def body(in_ref, *refs):
        del in_ref
        sems, out_ref = refs[:2 * N_PEER], refs[2 * N_PEER + 1]
        x, y, c, chips = _place()
        own = out_ref.at[2 * x + y, c]
        for j, (cx, cy) in enumerate(chips):
            cp = pltpu.make_async_remote_copy(src_ref=own, dst_ref=out_ref.at[2 * cx + cy, c], send_sem=sems[j],
                                              recv_sem=sems[N_PEER + j], device_id=(cx, cy, c), device_id_type=MESH)
            cp.wait_send()
            cp.wait_recv()

    return pl.pallas_call(
        body, name=f"gather_{tag}_wait", out_shape=pltpu.HBM(thru.shape, thru.dtype),
        in_specs=(HBM,) + (SEM,) * (2 * N_PEER) + (pl.BlockSpec(memory_space=pl.ANY),), out_specs=HBM,
        input_output_aliases={0: 0}, compiler_params=pltpu.CompilerParams(has_side_effects=EFFECT),
    )(thru, *sems, after)


def _gather_forward(slots, tag):
    def body(in_ref, out_ref, send_sems, recv_sems):
        del in_ref
        x, y, c, chips = _place()
        cps = []
        for j, (cx, cy) in enumerate(chips):
            landed = out_ref.at[2 * cx + cy, c]
            cps.append(pltpu.make_async_remote_copy(src_ref=landed, dst_ref=landed, send_sem=send_sems.at[j], recv_sem=recv_sems.at[j],
                                                    device_id=(x, y, 1 - c), device_id_type=MESH))
        for cp in cps:
            cp.start()
        for j, (cx, cy) in enumerate(chips):
            other = out_ref.at[2 * cx + cy, 1 - c]
            pltpu.make_async_remote_copy(src_ref=other, dst_ref=other, send_sem=send_sems.at[j], recv_sem=recv_sems.at[j],
                                         device_id=(x, y, 1 - c), device_id_type=MESH).wait_recv()
        for cp in cps:
            cp.wait_send()

    return pl.pallas_call(
        body, name=f"gather_{tag}_forward", out_shape=jax.ShapeDtypeStruct(slots.shape, slots.dtype),
        in_specs=[HBM], out_specs=HBM, input_output_aliases={0: 0},
        scratch_shapes=[pltpu.SemaphoreType.DMA((N_PEER,)), pltpu.SemaphoreType.DMA((N_PEER,))],
    )(slots)


def _scatter_start(pa, tag):
    ns, rh, lanes = pa.shape
    land = pltpu.with_memory_space_constraint(lax.empty((N_PEER, rh, lanes), pa.dtype), pltpu.HBM)

    def body(pa_ref, land_ref, *refs):
        x, y, c, chips = _place()
        for j, (cx, cy) in enumerate(chips):
            pltpu.make_async_remote_copy(src_ref=pa_ref.at[2 * cx + cy], dst_ref=land_ref.at[j], send_sem=refs[j],
                                         recv_sem=refs[N_PEER + j], device_id=(cx, cy, c), device_id_type=MESH).start()
        refs[-1][...] = jnp.zeros_like(refs[-1])

    out = pl.pallas_call(
        body, name=f"scatter_start_{tag}",
        out_shape=_sem_outs() + (pltpu.HBM(pa.shape, pa.dtype), pltpu.HBM(land.shape, land.dtype), jax.ShapeDtypeStruct((8, 128), F32)),
        in_specs=(HBM, HBM), out_specs=(SEM,) * (2 * N_PEER) + (HBM, HBM, pl.BlockSpec(memory_space=pltpu.VMEM)),
        input_output_aliases={0: 2 * N_PEER, 1: 2 * N_PEER + 1}, compiler_params=pltpu.CompilerParams(has_side_effects=EFFECT),
    )(pltpu.with_memory_space_constraint(pa, pltpu.HBM), land)
    return (out[:2 * N_PEER], out[2 * N_PEER], out[2 * N_PEER + 1]), out[2 * N_PEER + 2]


def _scatter_wait(sems, pa_thru, land_thru, after, tag):
    def body(pa_ref, land_ref, *refs):
        sems = refs[:2 * N_PEER]
        x, y, c, chips = _place()
        for j, (cx, cy) in enumerate(chips):
            cp = pltpu.make_async_remote_copy(src_ref=pa_ref.at[2 * cx + cy], dst_ref=land_ref.at[j], send_sem=sems[j],
                                              recv_sem=sems[N_PEER + j], device_id=(cx, cy, c), device_id_type=MESH)
            cp.wait_send()
            cp.wait_recv()

    return pl.pallas_call(
        body, name=f"scatter_wait_{tag}",
        out_shape=(pltpu.HBM(pa_thru.shape, pa_thru.dtype), pltpu.HBM(land_thru.shape, land_thru.dtype)),
        in_specs=(HBM, HBM) + (SEM,) * (2 * N_PEER) + (pl.BlockSpec(memory_space=pl.ANY),), out_specs=(HBM, HBM),
        input_output_aliases={0: 0, 1: 1}, compiler_params=pltpu.CompilerParams(has_side_effects=EFFECT),
    )(pa_thru, land_thru, *sems, after)


def _rs_swap(gp, tag):
    ns, _, rh, lanes = gp.shape

    def body(in_ref, out_ref, send_sem, recv_sem):
        x, y, c, _ = _place()
        cp = pltpu.make_async_remote_copy(src_ref=in_ref.at[:, 1 - c], dst_ref=out_ref, send_sem=send_sem, recv_sem=recv_sem,
                                          device_id=(x, y, 1 - c), device_id_type=MESH)
        cp.start()
        cp.wait_send()
        cp.wait_recv()

    return pl.pallas_call(
        body, name=f"rs_swap_{tag}", out_shape=jax.ShapeDtypeStruct((ns, rh, lanes), gp.dtype), in_specs=[HBM], out_specs=HBM,
        scratch_shapes=[pltpu.SemaphoreType.DMA, pltpu.SemaphoreType.DMA],
    )(gp)


def _rs_tile(rh):
    return _pick_tile(rh, cap=512)


def _rs_add(gp, recv, c, tag):
    ns, _, rh, lanes = gp.shape
    tile = _rs_tile(rh)

    def body(c_ref, a_ref, b_ref, o_ref):
        o_ref[...] = (a_ref[...].astype(F32) + b_ref[...].astype(F32)).astype(o_ref.dtype)

    return pl.pallas_call(
        body, name=f"rs_add_{tag}", out_shape=jax.ShapeDtypeStruct((ns, rh, lanes), gp.dtype),
        grid_spec=pltpu.PrefetchScalarGridSpec(
            num_scalar_prefetch=1, grid=(ns, rh // tile),
            in_specs=[pl.BlockSpec((None, None, tile, lanes), lambda s, i, c_ref: (s, c_ref[0], i, 0)),
                      pl.BlockSpec((None, tile, lanes), lambda s, i, c_ref: (s, i, 0))],
            out_specs=pl.BlockSpec((None, tile, lanes), lambda s, i, c_ref: (s, i, 0))),
        compiler_params=_cparams("parallel", "parallel"),
    )(c, gp, recv)


def _rs_sum(pa, recv, place, tag):
    ns, rh, lanes = pa.shape
    tile = _rs_tile(rh)

    def body(place_ref, a_ref, r_ref, o_ref):
        acc = a_ref[...].astype(F32)
        for j in range(ns - 1):
            acc = acc + r_ref[j].astype(F32)
        o_ref[...] = acc

    return pl.pallas_call(
        body, name=f"rs_sum_{tag}", out_shape=jax.ShapeDtypeStruct((2, rh, lanes), F32),
        grid_spec=pltpu.PrefetchScalarGridSpec(
            num_scalar_prefetch=1, grid=(rh // tile,),
            in_specs=[pl.BlockSpec((None, tile, lanes), lambda i, place_ref: (place_ref[0], i, 0)),
                      pl.BlockSpec((ns - 1, tile, lanes), lambda i, place_ref: (0, i, 0))],
            out_specs=pl.BlockSpec((None, tile, lanes), lambda i, place_ref: (place_ref[1], i, 0))),
        compiler_params=_cparams("parallel"),
    )(place, pa, recv)


def _rs_share(halves, tag):
    def body(in_ref, out_ref, send_sem, recv_sem):
        del in_ref
        x, y, c, _ = _place()
        cp = pltpu.make_async_remote_copy(src_ref=out_ref.at[c], dst_ref=out_ref.at[c], send_sem=send_sem, recv_sem=recv_sem,
                                          device_id=(x, y, 1 - c), device_id_type=MESH)
        cp.start()
        other = out_ref.at[1 - c]
        pltpu.make_async_remote_copy(src_ref=other, dst_ref=other, send_sem=send_sem, recv_sem=recv_sem,
                                     device_id=(x, y, 1 - c), device_id_type=MESH).wait_recv()
        cp.wait_send()

    return pl.pallas_call(
        body, name=f"rs_share_{tag}", out_shape=jax.ShapeDtypeStruct(halves.shape, halves.dtype), in_specs=[HBM], out_specs=HBM,
        input_output_aliases={0: 0}, scratch_shapes=[pltpu.SemaphoreType.DMA, pltpu.SemaphoreType.DMA],
    )(halves)


N_DEV = 8
SMALL_ROWS = 160


def _allreduce_small(v):
    m_per, n = v.shape

    def body(x_ref, out_ref, all_ref, send_sems, recv_sems, local_sem):
        x, y, c, chips = _place()
        me, sibling = (x, y, c), (x, y, 1 - c)

        def rows(px, py, pc):
            return all_ref.at[pl.ds((4 * px + 2 * py + pc) * m_per, m_per), :]

        def copy(k, block, to, src=None):
            return pltpu.make_async_remote_copy(src_ref=rows(*block) if src is None else src, dst_ref=rows(*block),
                                                send_sem=send_sems.at[k], recv_sem=recv_sems.at[k], device_id=to, device_id_type=MESH)

        mine = pltpu.make_async_copy(x_ref, rows(*me), local_sem)
        mine.start()
        first = [copy(0, me, sibling, src=x_ref)]
        first += [copy(1 + j, me, (*chip, c), src=x_ref) for j, chip in enumerate(chips)]
        for cp in first:
            cp.start()
        passed = [copy(4 + j, (*chip, c), sibling) for j, chip in enumerate(chips)]
        for j, chip in enumerate(chips):
            copy(1 + j, (*chip, c), me).wait_recv()
            passed[j].start()
        copy(0, sibling, me).wait_recv()
        for j, chip in enumerate(chips):
            copy(4 + j, (*chip, 1 - c), me).wait_recv()
        for cp in first + passed:
            cp.wait_send()
        mine.wait()
        step = 32
        for r in range(0, m_per, step):
            acc = all_ref[r:r + step, :]
            for d in range(1, N_DEV):
                acc = acc + all_ref[d * m_per + r:d * m_per + r + step, :]
            out_ref[r:r + step, :] = acc

    vm = pl.BlockSpec(memory_space=pltpu.VMEM)
    return pl.pallas_call(
        body, name="allreduce_small", out_shape=jax.ShapeDtypeStruct((m_per, n), v.dtype), in_specs=[vm], out_specs=vm,
        scratch_shapes=[pltpu.VMEM((N_DEV * m_per, n), v.dtype), pltpu.SemaphoreType.DMA((7,)), pltpu.SemaphoreType.DMA((7,)),
                        pltpu.SemaphoreType.DMA],
        compiler_params=pltpu.CompilerParams(vmem_limit_bytes=V7X_VMEM_LIMIT),
    )(v)


BIG = {"ffn1_w_gate": ((D_MODEL, D_FF), 1), "ffn1_w_up": ((D_MODEL, D_FF), 1), "ffn1_w_down": ((D_FF, D_MODEL), 0),
       "ffn2_w_gate": ((D_MODEL, D_FF), 1), "ffn2_w_up": ((D_MODEL, D_FF), 1), "ffn2_w_down": ((D_FF, D_MODEL), 0),
       "w_in": ((D_MODEL, IN_WIDTH), 1), "w_mem_kv": ((D_MODEL, 2 * MEM_W), 0), "w_branch_ssd": ((SSD_INNER, D_MODEL), 0),
       "w_branch_gmlp": ((GMLP_W, D_MODEL), 0), "w_branch_mem": ((MEM_W, D_MODEL), 1), "w_out": ((D_MODEL, D_MODEL), 0)}
FFN1 = ("ffn1_w_gate", "ffn1_w_up", "ffn1_w_down")
FFN2 = ("ffn2_w_gate", "ffn2_w_up", "ffn2_w_down")
MIXER = ("w_out", "w_branch_ssd", "w_branch_gmlp", "w_branch_mem", "w_mem_kv", "w_in")
REDUCE_GROUPS = (FFN2, MIXER, FFN1)
CONV_W_ROWS = 8


def _shard_rows_of(name):
    (a, b), _ = BIG[name]
    return a * b // N_SHARD // LANES


def _group_rows(names, extra=0):
    return -(-(sum(_shard_rows_of(n) for n in names) + extra) // 32) * 32

SMALL = [("ffn1_norm", 1), ("mix_norm", 1), ("mem_norm", 1), ("ssd_conv_b", 3), ("heads", 1), ("ssd_norm", 2),
         ("gmlp_v_norm", 1), ("gmlp_w_s", 128), ("gmlp_b_s", 1), ("ffn2_norm", 1), ("final_norm", 1), ("ssd_conv_w", 12)]
assert sum(n for _, n in SMALL) <= SMALL_ROWS
HEAD_VECS = ("ssd_dt_bias", "ssd_a_log", "ssd_d")


def _pack_small(vals, loss=None):
    parts = []
    for name, nrows in SMALL:
        if name == "heads":
            row = jnp.concatenate([vals[k].reshape(-1) for k in HEAD_VECS]
                                  + [jnp.zeros((1,), F32) if loss is None else loss.reshape(1)])
            parts.append(jnp.pad(row, (0, LANES - row.shape[0])).reshape(1, LANES))
        elif name in vals:
            parts.append(vals[name].reshape(nrows, LANES))
        else:
            parts.append(jnp.zeros((nrows, LANES), F32))
    buf = jnp.concatenate(parts, axis=0)
    return jnp.pad(buf, ((0, SMALL_ROWS - buf.shape[0]), (0, 0)))


def _unpack_small(buf):
    out, r = {}, 0
    for name, nrows in SMALL:
        blk = buf[r:r + nrows]
        r += nrows
        if name == "heads":
            for i, k in enumerate(HEAD_VECS):
                out[k] = blk[0, i * SSD_HEADS:(i + 1) * SSD_HEADS]
            out["loss"] = blk[0, 3 * SSD_HEADS]
        else:
            out[name] = blk
    return out


def _wire_shape(name):
    (a, b), axis = BIG[name]
    return (b, a) if axis == 1 else (a, b)


def _pack_weights(given, names, conv=False):
    parts = [(given[n][0].T if BIG[n][1] == 1 else given[n][0]).astype(BF16).reshape(_shard_rows_of(n), LANES) for n in names]
    if conv:
        pairs = lax.bitcast_convert_type(given["ssd_conv_w"], BF16).reshape(-1)
        parts.append(jnp.pad(pairs, (0, CONV_W_ROWS * LANES - pairs.shape[0])).reshape(CONV_W_ROWS, LANES))
    total = _group_rows(names, CONV_W_ROWS if conv else 0)
    packed = jnp.concatenate(parts, axis=0)
    packed = jnp.pad(packed, ((0, total - packed.shape[0]), (0, 0))).reshape(1, 2, total // 2, LANES)
    return jnp.broadcast_to(packed, (N_SHARD, 2, total // 2, LANES))


def _unpack_weights(slots, names, conv=False):
    rows = slots.reshape(N_SHARD, -1, LANES)
    out, r = {}, 0
    for name in names:
        n = _shard_rows_of(name)
        out[name] = rows[:, r:r + n].reshape(_wire_shape(name))
        r += n
    if conv:
        cols = XBC // N_SHARD
        pairs = rows[:, r:r + CONV_W_ROWS].reshape(N_SHARD, -1)[:, :SSD_CONV * cols * 2].reshape(N_SHARD, SSD_CONV, cols, 2)
        out["ssd_conv_w"] = jnp.transpose(lax.bitcast_convert_type(pairs, F32), (1, 0, 2)).reshape(SSD_CONV, XBC)
    return out


def _pack_grads(grads, names):
    total = _group_rows(names)
    parts = [grads[n].astype(BF16).reshape(N_SHARD, _shard_rows_of(n), LANES) for n in names]
    pad = total - sum(p.shape[1] for p in parts)
    if pad:
        parts.append(jnp.zeros((N_SHARD, pad, LANES), BF16))
    return jnp.concatenate(parts, axis=1).reshape(N_SHARD, 2, total // 2, LANES)


def kernel(x, mem, ffn1_norm, ffn1_w_gate, ffn1_w_up, ffn1_w_down, mix_norm, mem_norm, w_in, ssd_conv_w, ssd_conv_b, ssd_dt_bias, ssd_a_log, ssd_d, ssd_norm, gmlp_v_norm, gmlp_w_s, gmlp_b_s, w_mem_kv, w_branch_ssd, w_branch_gmlp, w_branch_mem, w_out, ffn2_norm, ffn2_w_gate, ffn2_w_up, ffn2_w_down, final_norm, loss_target, m_ffn1_norm, m_ffn1_w_gate, m_ffn1_w_up, m_ffn1_w_down, m_mix_norm, m_mem_norm, m_w_in, m_ssd_conv_w, m_ssd_conv_b, m_ssd_dt_bias, m_ssd_a_log, m_ssd_d, m_ssd_norm, m_gmlp_v_norm, m_gmlp_w_s, m_gmlp_b_s, m_w_mem_kv, m_w_branch_ssd, m_w_branch_gmlp, m_w_branch_mem, m_w_out, m_ffn2_norm, m_ffn2_w_gate, m_ffn2_w_up, m_ffn2_w_down, m_final_norm, v_ffn1_norm, v_ffn1_w_gate, v_ffn1_w_up, v_ffn1_w_down, v_mix_norm, v_mem_norm, v_w_in, v_ssd_conv_w, v_ssd_conv_b, v_ssd_dt_bias, v_ssd_a_log, v_ssd_d, v_ssd_norm, v_gmlp_v_norm, v_gmlp_w_s, v_gmlp_b_s, v_w_mem_kv, v_w_branch_ssd, v_w_branch_gmlp, v_w_branch_mem, v_w_out, v_ffn2_norm, v_ffn2_w_gate, v_ffn2_w_up, v_ffn2_w_down, v_final_norm):
    given = dict(x=x, mem=mem, ffn1_norm=ffn1_norm, ffn1_w_gate=ffn1_w_gate, ffn1_w_up=ffn1_w_up, ffn1_w_down=ffn1_w_down, mix_norm=mix_norm, mem_norm=mem_norm, w_in=w_in, ssd_conv_w=ssd_conv_w, ssd_conv_b=ssd_conv_b, ssd_dt_bias=ssd_dt_bias, ssd_a_log=ssd_a_log, ssd_d=ssd_d, ssd_norm=ssd_norm, gmlp_v_norm=gmlp_v_norm, gmlp_w_s=gmlp_w_s, gmlp_b_s=gmlp_b_s, w_mem_kv=w_mem_kv, w_branch_ssd=w_branch_ssd, w_branch_gmlp=w_branch_gmlp, w_branch_mem=w_branch_mem, w_out=w_out, ffn2_norm=ffn2_norm, ffn2_w_gate=ffn2_w_gate, ffn2_w_up=ffn2_w_up, ffn2_w_down=ffn2_w_down, final_norm=final_norm, loss_target=loss_target, m_ffn1_norm=m_ffn1_norm, m_ffn1_w_gate=m_ffn1_w_gate, m_ffn1_w_up=m_ffn1_w_up, m_ffn1_w_down=m_ffn1_w_down, m_mix_norm=m_mix_norm, m_mem_norm=m_mem_norm, m_w_in=m_w_in, m_ssd_conv_w=m_ssd_conv_w, m_ssd_conv_b=m_ssd_conv_b, m_ssd_dt_bias=m_ssd_dt_bias, m_ssd_a_log=m_ssd_a_log, m_ssd_d=m_ssd_d, m_ssd_norm=m_ssd_norm, m_gmlp_v_norm=m_gmlp_v_norm, m_gmlp_w_s=m_gmlp_w_s, m_gmlp_b_s=m_gmlp_b_s, m_w_mem_kv=m_w_mem_kv, m_w_branch_ssd=m_w_branch_ssd, m_w_branch_gmlp=m_w_branch_gmlp, m_w_branch_mem=m_w_branch_mem, m_w_out=m_w_out, m_ffn2_norm=m_ffn2_norm, m_ffn2_w_gate=m_ffn2_w_gate, m_ffn2_w_up=m_ffn2_w_up, m_ffn2_w_down=m_ffn2_w_down, m_final_norm=m_final_norm, v_ffn1_norm=v_ffn1_norm, v_ffn1_w_gate=v_ffn1_w_gate, v_ffn1_w_up=v_ffn1_w_up, v_ffn1_w_down=v_ffn1_w_down, v_mix_norm=v_mix_norm, v_mem_norm=v_mem_norm, v_w_in=v_w_in, v_ssd_conv_w=v_ssd_conv_w, v_ssd_conv_b=v_ssd_conv_b, v_ssd_dt_bias=v_ssd_dt_bias, v_ssd_a_log=v_ssd_a_log, v_ssd_d=v_ssd_d, v_ssd_norm=v_ssd_norm, v_gmlp_v_norm=v_gmlp_v_norm, v_gmlp_w_s=v_gmlp_w_s, v_gmlp_b_s=v_gmlp_b_s, v_w_mem_kv=v_w_mem_kv, v_w_branch_ssd=v_w_branch_ssd, v_w_branch_gmlp=v_w_branch_gmlp, v_w_branch_mem=v_w_branch_mem, v_w_out=v_w_out, v_ffn2_norm=v_ffn2_norm, v_ffn2_w_gate=v_ffn2_w_gate, v_ffn2_w_up=v_ffn2_w_up, v_ffn2_w_down=v_ffn2_w_down, v_final_norm=v_final_norm)
    weights = [n for n in given if n not in ("x", "mem", "loss_target") and not n.startswith(("m_", "v_"))]
    xi, yi, ci = lax.axis_index("x"), lax.axis_index("y"), lax.axis_index("c")
    chip = (2 * xi + yi).astype(jnp.int32)
    core = ci.astype(jnp.int32)
    conv_cols = XBC // N_SHARD

    copies = {"first": _gather_start(_pack_weights(given, FFN1), "first")}
    tied = dict(given)
    tied["w_in"], tied["ffn2_w_gate"], _ = lax.optimization_barrier((given["w_in"], given["ffn2_w_gate"], copies["first"][2]))
    packed = {"mixer": _pack_weights(tied, MIXER, conv=True), "ffn2": _pack_weights(tied, FFN2)}
    w = {}
    for name in ("ffn1_norm", "mix_norm", "mem_norm", "ssd_conv_b", "ssd_norm", "gmlp_v_norm", "ffn2_norm", "final_norm"):
        w[name] = given[name].reshape(1, -1)
    w["ffn1_norm"] = w["ffn1_norm"] + copies["first"][2][0:1, 0:1]
    for name in HEAD_VECS:
        w[name] = given[name].reshape(-1)
    w["gmlp_w_s"] = given["gmlp_w_s"][0]
    w["gmlp_b_s"] = given["gmlp_b_s"][0]

    def arrived(tag, after):
        sems, thru, _ = copies[tag]
        return _gather_forward(_gather_wait(sems, thru, after, tag), tag)

    def then_start(slots, tag, weights, name):
        slots, nxt = lax.optimization_barrier((slots, packed[tag]))
        copies[tag] = _gather_start(nxt, tag)
        unpacked = weights(slots)
        unpacked[name], _ = lax.optimization_barrier((unpacked[name], copies[tag][2]))
        return unpacked

    def first_weights(after):
        after, packed["mixer"], packed["ffn2"] = lax.optimization_barrier((after, packed["mixer"], packed["ffn2"]))
        return then_start(arrived("first", after), "mixer", lambda s: _unpack_weights(s, FFN1), FFN1[0])

    def mixer_weights(after):
        def unpack(slots):
            rest = _unpack_weights(slots, MIXER, conv=True)
            rest["w_in_p"], rest["w_dt"] = _split_w_in(rest.pop("w_in"))
            return rest
        return then_start(arrived("mixer", after), "ffn2", unpack, "w_in_p")

    def second_weights(after):
        return _unpack_weights(arrived("ffn2", after), FFN2)

    pending = {}

    def push(k, group_grads):
        gp = _pack_grads(group_grads, REDUCE_GROUPS[k])
        pa = _rs_add(gp, _rs_swap(gp, k), core.reshape(1), k)
        pending[k], token = _scatter_start(pa, k)
        return token[0:1, 0:1]

    def reduced(k, after):
        pa, land = _scatter_wait(*pending[k], after, k)
        gsum = _rs_share(_rs_sum(pa, land, jnp.stack([chip, core]), k), k)
        rows = gsum.reshape(-1, LANES)
        out, r = {}, 0
        for name in REDUCE_GROUPS[k]:
            n = _shard_rows_of(name)
            a, b = given[name].shape[1:]
            out[name] = rows[r:r + n].reshape(b, a).T if BIG[name][1] == 1 else rows[r:r + n].reshape(a, b)
            r += n
        return out

    loss_part, grad_x, g = _local_step(x[0], mem[0], loss_target[0], w, first_weights, mixer_weights, second_weights, push)

    grads, deltas, new_m, new_v = {}, {}, {}, {}

    def update(k, after):
        for name, gl in reduced(k, after).items():
            d, nm, nv = _adamw(given[name][0], gl, given["m_" + name][0], given["v_" + name][0], f"adamw_{name}")
            grads[name], deltas[name], new_m[name], new_v[name] = (a[None] for a in (gl, d, nm, nv))

    update(0, grad_x)
    update(1, deltas[REDUCE_GROUPS[0][-1]])

    small_vals = {k: g[k] for k, _ in SMALL if k != "heads"}
    small_vals.update({k: g[k] for k in HEAD_VECS})
    red = _unpack_small(_allreduce_small(_pack_small(small_vals, loss=loss_part[0, 0])))
    update(2, deltas[REDUCE_GROUPS[1][-1]])
    conv_g = lax.dynamic_slice_in_dim(red["ssd_conv_w"].reshape(SSD_CONV, XBC), chip * conv_cols, conv_cols, axis=1)
    d, nm, nv = _adamw(given["ssd_conv_w"][0], conv_g, given["m_ssd_conv_w"][0], given["v_ssd_conv_w"][0], "adamw_conv_w")
    grads["ssd_conv_w"], deltas["ssd_conv_w"], new_m["ssd_conv_w"], new_v["ssd_conv_w"] = (a[None] for a in (conv_g, d, nm, nv))
    for k in [k for k, _ in SMALL if k not in ("heads", "ssd_conv_w")] + list(HEAD_VECS):
        shape = given[k].shape
        as2d = lambda a: a.reshape(-1, shape[-1])
        d, nm, nv = _adamw(as2d(given[k]), as2d(red[k]), as2d(given["m_" + k]), as2d(given["v_" + k]), f"adamw_{k}")
        grads[k], deltas[k], new_m[k], new_v[k] = (a.reshape(shape) for a in (red[k], d, nm, nv))

    return (red["loss"], grad_x[None], *[grads[n] for n in weights], *[deltas[n] for n in weights],
            *[new_m[n] for n in weights], *[new_v[n] for n in weights])
```

```python
import functools
import math

import jax
import jax.numpy as jnp
from jax import lax
from jax.experimental import pallas as pl
from jax.experimental.pallas import tpu as pltpu

F32, BF16 = jnp.float32, jnp.bfloat16
HI = lax.Precision.HIGHEST
MESH = pl.DeviceIdType.MESH

D_MODEL = 1024
D_FF = 2816
MEM_LEN = 256
SSD_INNER = 2048
SSD_HEADS = 32
SSD_GROUPS = 4
SSD_STATE = 128
SSD_CONV = 4
CHUNK = 128
XBC = SSD_INNER + 2 * SSD_GROUPS * SSD_STATE
GMLP_W = 1024
GMLP_GROUPS = 8
MEM_W = 256
MEM_HEADS = 4
EPS = 1e-6
IN_WIDTH = 10528
IN_Z, IN_XBC, IN_DT, IN_UV, IN_Q, IN_GL = 0, 2048, 5120, 5152, 7200, 7456
P_GL, P_Z, P_XBC, P_UV, P_Q, P_W = 0, 3072, 5120, 8192, 10240, 10752
P_USED = 10496

ADAM_LR, ADAM_B1, ADAM_B2, ADAM_EPS, ADAM_WD, ADAM_STEP = 0.001, 0.9, 0.999, 1e-08, 0.01, 10

V7X_VMEM_LIMIT = 56 * 1024 * 1024
N_SHARD = 4
LANES = 1024


def _cparams(*sem):
    return pltpu.CompilerParams(dimension_semantics=sem, vmem_limit_bytes=V7X_VMEM_LIMIT)


ANY = pl.BlockSpec(memory_space=pl.ANY)


def _sigmoid(x):
    return 0.5 * jnp.tanh(0.5 * x) + 0.5


def _row_tile(t):
    return min(512, t)


_DIMS = {"nn": (((1,), (0,)), ((), ())), "nt": (((1,), (1,)), ((), ())), "tn": (((0,), (0,)), ((), ()))}


def _matmul(a, b, *, mode, out_dtype, tm, tn, tk, name, scale=1.0, addend=None):
    if mode == "tn":
        k_dim, m_dim = a.shape
    else:
        m_dim, k_dim = a.shape
    n_dim = b.shape[0] if mode == "nt" else b.shape[1]
    tm, tn, tk = min(tm, m_dim), min(tn, n_dim), min(tk, k_dim)
    assert m_dim % tm == 0 and n_dim % tn == 0 and k_dim % tk == 0, (name, a.shape, b.shape, tm, tn, tk)
    ni, nj, nk = m_dim // tm, n_dim // tn, k_dim // tk
    a_spec = pl.BlockSpec((tk, tm), lambda j, i, k: (k, i)) if mode == "tn" else pl.BlockSpec((tm, tk), lambda j, i, k: (i, k))
    b_spec = pl.BlockSpec((tn, tk), lambda j, i, k: (j, k)) if mode == "nt" else pl.BlockSpec((tk, tn), lambda j, i, k: (k, j))
    o_spec = pl.BlockSpec((tm, tn), lambda j, i, k: (i, j))
    dims = _DIMS[mode]
    has_add = addend is not None

    def body(*refs):
        a_ref, b_ref = refs[:2]
        r_ref = refs[2] if has_add else None
        o_ref = refs[2 + has_add]

        def finish(acc):
            r = acc * scale if scale != 1.0 else acc
            if has_add:
                r = r + r_ref[...].astype(F32)
            o_ref[...] = r.astype(o_ref.dtype)

        prod = lax.dot_general(a_ref[...].astype(BF16), b_ref[...].astype(BF16), dims, preferred_element_type=F32)
        if nk == 1:
            finish(prod)
            return
        acc_ref = refs[-1]
        k = pl.program_id(2)

        @pl.when(k == 0)
        def _():
            acc_ref[...] = prod

        @pl.when(k > 0)
        def _():
            acc_ref[...] += prod

        @pl.when(k == nk - 1)
        def _():
            finish(acc_ref[...])

    in_specs = [a_spec, b_spec] + ([o_spec] if has_add else [])
    args = (a, b) + ((addend,) if has_add else ())
    return pl.pallas_call(
        body, name=name, grid=(nj, ni, nk), in_specs=in_specs, out_specs=o_spec,
        out_shape=jax.ShapeDtypeStruct((m_dim, n_dim), out_dtype),
        scratch_shapes=[] if nk == 1 else [pltpu.VMEM((tm, tn), F32)],
        compiler_params=_cparams("parallel", "parallel", "arbitrary"),
    )(*args)


ROW_STRIP = 16


def _strips(tm, fn, init=None, rb=ROW_STRIP):
    def step(i, carry):
        return fn(pl.ds(pl.multiple_of(i * rb, rb), rb), carry)
    return lax.fori_loop(0, tm // rb, step, init, unroll=2)


def _rms_fwd(x, gain, name):
    t, d = x.shape
    tm = _row_tile(t)

    def body(x_ref, g_ref, o_ref):
        xv = x_ref[...]
        r = lax.rsqrt(jnp.mean(xv * xv, axis=-1, keepdims=True) + EPS)
        o_ref[...] = (xv * r * g_ref[...]).astype(o_ref.dtype)

    return pl.pallas_call(
        body, name=name, grid=(t // tm,),
        in_specs=[pl.BlockSpec((tm, d), lambda i: (i, 0)), pl.BlockSpec((1, d), lambda i: (0, 0))],
        out_specs=pl.BlockSpec((tm, d), lambda i: (i, 0)),
        out_shape=jax.ShapeDtypeStruct((t, d), BF16), compiler_params=_cparams("parallel"),
    )(x, gain)


def _rms_bwd(x, gain, dn, dres, name):
    t, d = x.shape
    tm = _row_tile(t)
    has_res = dres is not None

    def body(*refs):
        if has_res:
            x_ref, g_ref, dn_ref, r_ref, dx_ref, dg_ref = refs
        else:
            x_ref, g_ref, dn_ref, dx_ref, dg_ref = refs

        @pl.when(pl.program_id(0) == 0)
        def _():
            dg_ref[...] = jnp.zeros_like(dg_ref)

        xv = x_ref[...]
        r = lax.rsqrt(jnp.mean(xv * xv, axis=-1, keepdims=True) + EPS)
        xh = xv * r
        dnv = dn_ref[...].astype(F32)
        dg_ref[...] += jnp.sum(dnv * xh, axis=0, keepdims=True)
        dxh = dnv * g_ref[...]
        dx = r * (dxh - xh * jnp.mean(dxh * xh, axis=-1, keepdims=True))
        if has_res:
            dx = dx + r_ref[...]
        dx_ref[...] = dx

    row = pl.BlockSpec((tm, d), lambda i: (i, 0))
    vec = pl.BlockSpec((1, d), lambda i: (0, 0))
    in_specs = [row, vec, row] + ([row] if has_res else [])
    args = (x, gain, dn) + ((dres,) if has_res else ())
    return pl.pallas_call(
        body, name=name, grid=(t // tm,), in_specs=in_specs, out_specs=[row, vec],
        out_shape=[jax.ShapeDtypeStruct((t, d), F32), jax.ShapeDtypeStruct((1, d), F32)],
        compiler_params=_cparams("arbitrary"),
    )(*args)


def _loss_head(h, gain, target, name):
    t, d = h.shape
    tm = _row_tile(t)

    def body(h_ref, g_ref, t_ref, dh_ref, dg_ref, l_ref):
        @pl.when(pl.program_id(0) == 0)
        def _():
            dg_ref[...] = jnp.zeros_like(dg_ref)
            l_ref[...] = jnp.zeros_like(l_ref)

        xv = h_ref[...]
        g = g_ref[...]
        r = lax.rsqrt(jnp.mean(xv * xv, axis=-1, keepdims=True) + EPS)
        xh = xv * r
        err = xh * g - t_ref[...]
        l_ref[...] += 0.5 * jnp.sum(jnp.mean(err * err, axis=-1, keepdims=True), axis=0, keepdims=True)
        dy = err * (1.0 / d)
        dg_ref[...] += jnp.sum(dy * xh, axis=0, keepdims=True)
        dxh = dy * g
        dh_ref[...] = r * (dxh - xh * jnp.mean(dxh * xh, axis=-1, keepdims=True))

    row = pl.BlockSpec((tm, d), lambda i: (i, 0))
    vec = pl.BlockSpec((1, d), lambda i: (0, 0))
    return pl.pallas_call(
        body, name=name, grid=(t // tm,), in_specs=[row, vec, row],
        out_specs=[row, vec, pl.BlockSpec((1, 128), lambda i: (0, 0))],
        out_shape=[jax.ShapeDtypeStruct((t, d), F32), jax.ShapeDtypeStruct((1, d), F32), jax.ShapeDtypeStruct((1, 128), F32)],
        compiler_params=_cparams("arbitrary"),
    )(h, gain, target)


FF_TILE = 1408


def _ffn_fwd(n, x, wg, wu, wd, name):
    t, d = x.shape
    tm, tn = _row_tile(t), FF_TILE
    nj = D_FF // tn

    def body(n_ref, x_ref, wg_ref, wu_ref, wd_ref, h_ref, g_ref, u_ref, acc_ref):
        j = pl.program_id(1)

        @pl.when(j == 0)
        def _():
            acc_ref[...] = jnp.zeros_like(acc_ref)

        nb = n_ref[...]
        g = lax.dot_general(nb, wg_ref[...], _DIMS["nt"], preferred_element_type=F32)
        u = lax.dot_general(nb, wu_ref[...], _DIMS["nt"], preferred_element_type=F32)
        g_ref[...] = g.astype(BF16)
        u_ref[...] = u.astype(BF16)
        a = g * _sigmoid(g) * u
        acc_ref[...] += jnp.dot(a.astype(BF16), wd_ref[...], preferred_element_type=F32)

        @pl.when(j == nj - 1)
        def _():
            h_ref[...] = x_ref[...] + 0.5 * acc_ref[...]

    row = pl.BlockSpec((tm, d), lambda i, j: (i, 0))
    act = pl.BlockSpec((tm, tn), lambda i, j: (i, j))
    return pl.pallas_call(
        body, name=name, grid=(t // tm, nj),
        in_specs=[row, row] + [pl.BlockSpec((tn, d), lambda i, j: (j, 0))] * 3,
        out_specs=[row, act, act],
        out_shape=[jax.ShapeDtypeStruct((t, d), F32), jax.ShapeDtypeStruct((t, D_FF), BF16), jax.ShapeDtypeStruct((t, D_FF), BF16)],
        scratch_shapes=[pltpu.VMEM((tm, d), F32)], compiler_params=_cparams("parallel", "arbitrary"),
    )(n, x, wg, wu, wd)


def _ffn_bwd_act(dh, g, u, wg, wu, wd, name):
    t, d = dh.shape
    tm, tn = _row_tile(t), FF_TILE
    nj = D_FF // tn

    def body(dh_ref, g_ref, u_ref, wg_ref, wu_ref, wd_ref, dn_ref, dg_ref, du_ref, a_ref, acc_ref):
        j = pl.program_id(1)

        @pl.when(j == 0)
        def _():
            acc_ref[...] = jnp.zeros_like(acc_ref)

        dhb = (0.5 * dh_ref[...]).astype(BF16)
        da = lax.dot_general(dhb, wd_ref[...], _DIMS["nt"], preferred_element_type=F32)
        gv = g_ref[...].astype(F32)
        uv = u_ref[...].astype(F32)
        sg = _sigmoid(gv)
        s = gv * sg
        dg = (da * uv * (sg * (1.0 + gv * (1.0 - sg)))).astype(BF16)
        du = (da * s).astype(BF16)
        dg_ref[...] = dg
        du_ref[...] = du
        a_ref[...] = (s * uv).astype(BF16)
        acc_ref[...] += (jnp.dot(dg, wg_ref[...], preferred_element_type=F32)
                         + jnp.dot(du, wu_ref[...], preferred_element_type=F32))

        @pl.when(j == nj - 1)
        def _():
            dn_ref[...] = acc_ref[...]

    row = pl.BlockSpec((tm, d), lambda i, j: (i, 0))
    act = pl.BlockSpec((tm, tn), lambda i, j: (i, j))
    return pl.pallas_call(
        body, name=name, grid=(t // tm, nj),
        in_specs=[row, act, act] + [pl.BlockSpec((tn, d), lambda i, j: (j, 0))] * 3,
        out_specs=[row, act, act, act],
        out_shape=[jax.ShapeDtypeStruct((t, d), F32)] + [jax.ShapeDtypeStruct((t, D_FF), BF16)] * 3,
        scratch_shapes=[pltpu.VMEM((tm, d), F32)], compiler_params=_cparams("parallel", "arbitrary"),
    )(dh, g, u, wg, wu, wd)


def _ffn_forward(x, gain, weights, tag):
    n = _rms_fwd(x, gain, f"{tag}_norm")
    h, g, u = _ffn_fwd(n, x, *weights(n), f"{tag}_fwd")
    return h, (n, g, u)


def _ffn_backward(dh, x, gain, wg, wu, wd, saved, tag, push):
    n, g, u = saved
    dn, dg, du, a = _ffn_bwd_act(dh, g, u, wg, wu, wd, f"{tag}_bwd_act")
    kw = dict(mode="tn", out_dtype=BF16, tm=FF_TILE, tn=1024, tk=2048)
    d_wg = _matmul(dg, n, name=f"{tag}_dwg", **kw)
    d_wu = _matmul(du, n, name=f"{tag}_dwu", **kw)
    d_wd = _matmul(a, dh, scale=0.5, name=f"{tag}_dwd", **kw)
    token = push({f"{tag}_w_gate": d_wg, f"{tag}_w_up": d_wu, f"{tag}_w_down": d_wd})
    return _rms_bwd(x, gain + token, dn, dh, f"{tag}_norm_bwd")


CONV_COLS = 512
HALO = 8
CONV_STRIP = 32
CONV_ROWS = 2048


def _conv_fwd(p, w, b, name):
    t = p.shape[0]
    tm = min(CONV_ROWS, t)
    c0 = P_XBC // CONV_COLS

    def body(x_ref, halo_ref, w_ref, b_ref, o_ref, s_ref):
        i = pl.program_id(1)
        s_ref[0:HALO, :] = jnp.where(i > 0, halo_ref[...].astype(F32), 0.0)
        s_ref[HALO:HALO + tm, :] = x_ref[...].astype(F32)
        wv = w_ref[...]
        bv = b_ref[...]
        for r0 in range(0, tm, CONV_STRIP):
            acc = bv + wv[0:1, :] * s_ref[HALO - 3 + r0:HALO - 3 + r0 + CONV_STRIP, :]
            for k in range(1, SSD_CONV):
                acc = acc + wv[k:k + 1, :] * s_ref[HALO - 3 + k + r0:HALO - 3 + k + r0 + CONV_STRIP, :]
            o_ref[r0:r0 + CONV_STRIP, :] = (acc * _sigmoid(acc)).astype(o_ref.dtype)

    return pl.pallas_call(
        body, name=name, grid=(XBC // CONV_COLS, t // tm),
        in_specs=[pl.BlockSpec((tm, CONV_COLS), lambda j, i: (i, c0 + j)),
                  pl.BlockSpec((HALO, CONV_COLS), lambda j, i: (jnp.maximum(i * (tm // HALO) - 1, 0), c0 + j)),
                  pl.BlockSpec((SSD_CONV, CONV_COLS), lambda j, i: (0, j)), pl.BlockSpec((1, CONV_COLS), lambda j, i: (0, j))],
        out_specs=pl.BlockSpec((tm, CONV_COLS), lambda j, i: (i, j)),
        out_shape=jax.ShapeDtypeStruct((t, XBC), BF16),
        scratch_shapes=[pltpu.VMEM((tm + HALO, CONV_COLS), F32)], compiler_params=_cparams("parallel", "parallel"),
    )(p, p, w, b)


def _conv_bwd_act(p, dy, w, b, col0, name):
    t, cols = dy.shape
    tm = min(CONV_ROWS, t)
    c0 = (P_XBC + col0) // CONV_COLS
    w0 = col0 // CONV_COLS

    def body(x_ref, halo_ref, dy_ref, w_ref, b_ref, da_ref, dw_ref, db_ref, s_ref):
        i = pl.program_id(1)

        @pl.when(i == 0)
        def _():
            dw_ref[...] = jnp.zeros_like(dw_ref)
            db_ref[...] = jnp.zeros_like(db_ref)

        s_ref[0:HALO, :] = jnp.where(i > 0, halo_ref[...].astype(F32), 0.0)
        s_ref[HALO:HALO + tm, :] = x_ref[...].astype(F32)
        wv = w_ref[...]
        bv = b_ref[...]
        fold = lambda v: jnp.sum(v.reshape(CONV_STRIP // 8, 8, CONV_COLS), axis=0)
        sums = [jnp.zeros((8, CONV_COLS), F32) for _ in range(SSD_CONV + 1)]
        for r0 in range(0, tm, CONV_STRIP):
            taps = [s_ref[HALO - 3 + k + r0:HALO - 3 + k + r0 + CONV_STRIP, :] for k in range(SSD_CONV)]
            acc = bv + wv[0:1, :] * taps[0]
            for k in range(1, SSD_CONV):
                acc = acc + wv[k:k + 1, :] * taps[k]
            sg = _sigmoid(acc)
            dacc = dy_ref[r0:r0 + CONV_STRIP, :].astype(F32) * (sg * (1.0 + acc * (1.0 - sg)))
            da_ref[r0:r0 + CONV_STRIP, :] = dacc.astype(BF16)
            for k in range(SSD_CONV):
                sums[k] = sums[k] + fold(dacc * taps[k])
            sums[SSD_CONV] = sums[SSD_CONV] + fold(dacc)
        for k in range(SSD_CONV):
            dw_ref[k:k + 1, :] += jnp.sum(sums[k], axis=0, keepdims=True)
        db_ref[...] += jnp.sum(sums[SSD_CONV], axis=0, keepdims=True)

    return pl.pallas_call(
        body, name=name, grid=(cols // CONV_COLS, t // tm),
        in_specs=[pl.BlockSpec((tm, CONV_COLS), lambda j, i: (i, c0 + j)),
                  pl.BlockSpec((HALO, CONV_COLS), lambda j, i: (jnp.maximum(i * (tm // HALO) - 1, 0), c0 + j)),
                  pl.BlockSpec((tm, CONV_COLS), lambda j, i: (i, j)),
                  pl.BlockSpec((SSD_CONV, CONV_COLS), lambda j, i: (0, w0 + j)), pl.BlockSpec((1, CONV_COLS), lambda j, i: (0, w0 + j))],
        out_specs=[pl.BlockSpec((tm, CONV_COLS), lambda j, i: (i, j)), pl.BlockSpec((SSD_CONV, CONV_COLS), lambda j, i: (0, j)),
                   pl.BlockSpec((1, CONV_COLS), lambda j, i: (0, j))],
        out_shape=[jax.ShapeDtypeStruct((t, cols), BF16), jax.ShapeDtypeStruct((SSD_CONV, cols), F32), jax.ShapeDtypeStruct((1, cols), F32)],
        scratch_shapes=[pltpu.VMEM((tm + HALO, CONV_COLS), F32)], compiler_params=_cparams("parallel", "arbitrary"),
    )(p, p, dy, w, b)


def _conv_bwd_dx(dacc, w, col0, dp, name):
    t, cols = dacc.shape
    tm = min(CONV_ROWS, t)
    nt = t // tm
    w0 = col0 // CONV_COLS
    c0 = (P_XBC + col0) // CONV_COLS

    def body(d_ref, halo_ref, w_ref, dp_ref, o_ref, s_ref):
        del dp_ref
        i = pl.program_id(1)
        s_ref[0:tm, :] = d_ref[...].astype(F32)
        s_ref[tm:tm + HALO, :] = jnp.where(i < nt - 1, halo_ref[...].astype(F32), 0.0)
        wv = w_ref[...]
        for r0 in range(0, tm, CONV_STRIP):
            acc = wv[3:4, :] * s_ref[r0:r0 + CONV_STRIP, :]
            for k in range(SSD_CONV - 1):
                acc = acc + wv[k:k + 1, :] * s_ref[3 - k + r0:3 - k + r0 + CONV_STRIP, :]
            o_ref[r0:r0 + CONV_STRIP, :] = acc.astype(o_ref.dtype)

    return pl.pallas_call(
        body, name=name, grid=(cols // CONV_COLS, nt),
        in_specs=[pl.BlockSpec((tm, CONV_COLS), lambda j, i: (i, j)),
                  pl.BlockSpec((HALO, CONV_COLS), lambda j, i: (jnp.minimum((i + 1) * (tm // HALO), t // HALO - 1), j)),
                  pl.BlockSpec((SSD_CONV, CONV_COLS), lambda j, i: (0, w0 + j)), ANY],
        out_specs=pl.BlockSpec((tm, CONV_COLS), lambda j, i: (i, c0 + j)),
        out_shape=jax.ShapeDtypeStruct(dp.shape, dp.dtype), input_output_aliases={3: 0},
        scratch_shapes=[pltpu.VMEM((tm + HALO, CONV_COLS), F32)], compiler_params=_cparams("parallel", "parallel"),
    )(dacc, dacc, w, dp)


GROUP_COLS = SSD_INNER // SSD_GROUPS
PAIRS = GROUP_COLS // 128
HEADS_PER_GROUP = SSD_HEADS // SSD_GROUPS


def _dt_fwd(dt_raw, bias, name):
    t, n = dt_raw.shape
    tm = _row_tile(t)

    def body(x_ref, b_ref, o_ref):
        v = x_ref[...] + b_ref[...]
        o_ref[...] = jnp.maximum(v, 0.0) + jnp.log1p(jnp.exp(-jnp.abs(v)))

    row = pl.BlockSpec((tm, n), lambda i: (i, 0))
    vec = pl.BlockSpec((1, n), lambda i: (0, 0))
    return pl.pallas_call(body, name=name, grid=(t // tm,), in_specs=[row, vec], out_specs=row,
                          out_shape=jax.ShapeDtypeStruct((t, n), F32), compiler_params=_cparams("parallel"))(dt_raw, bias)


def _dt_bwd(ddt, dt_raw, bias, name):
    t, n = dt_raw.shape
    tm = _row_tile(t)

    def body(d_ref, x_ref, b_ref, o_ref, db_ref):
        @pl.when(pl.program_id(0) == 0)
        def _():
            db_ref[...] = jnp.zeros_like(db_ref)

        dr = d_ref[...] * _sigmoid(x_ref[...] + b_ref[...])
        o_ref[...] = dr.astype(o_ref.dtype)
        db_ref[...] += jnp.sum(dr, axis=0, keepdims=True)

    row = pl.BlockSpec((tm, n), lambda i: (i, 0))
    vec = pl.BlockSpec((1, n), lambda i: (0, 0))
    return pl.pallas_call(body, name=name, grid=(t // tm,), in_specs=[row, row, vec], out_specs=[row, vec],
                          out_shape=[jax.ShapeDtypeStruct((t, n), BF16), jax.ShapeDtypeStruct((1, n), F32)],
                          compiler_params=_cparams("arbitrary"))(ddt, dt_raw, bias)


SSD_STEP = 4


def _ssd_common(dt, dtt, a_log_w, a_log_t):
    l = CHUNK
    a = -jnp.exp(a_log_w)
    at = -jnp.exp(a_log_t)
    rowi = lax.broadcasted_iota(jnp.int32, (l, l), 0)
    coli = lax.broadcasted_iota(jnp.int32, (l, l), 1)
    tri = rowi >= coli
    lower = tri.astype(F32)
    upper = (rowi <= coli).astype(F32)
    acs = jnp.dot(lower, dt * a, precision=HI, preferred_element_type=F32)
    acst = jnp.dot(dtt * at, upper, precision=HI, preferred_element_type=F32)
    return a, acs, acst, jnp.where(tri, 0.0, -1e30), upper


def _pair_bc(w, lo, p):
    return jnp.where(lo, w[:, 2 * p:2 * p + 1], w[:, 2 * p + 1:2 * p + 2])


def _ssd_specs(t):
    rows = SSD_STEP * CHUNK
    assert t % rows == 0
    return t // rows, dict(
        xs=lambda cm: pl.BlockSpec((rows, GROUP_COLS), lambda g, c: (cm(c), g)),
        bm=lambda cm: pl.BlockSpec((rows, SSD_STATE), lambda g, c: (cm(c), SSD_INNER // SSD_STATE + g)),
        cmat=lambda cm: pl.BlockSpec((rows, SSD_STATE), lambda g, c: (cm(c), SSD_INNER // SSD_STATE + SSD_GROUPS + g)),
        dtw=lambda cm: pl.BlockSpec((1, rows, 128), lambda g, c: (g, cm(c), 0)),
        dtt=lambda cm: pl.BlockSpec((1, HEADS_PER_GROUP, rows), lambda g, c: (g, 0, cm(c))),
        wide=lambda cm: pl.BlockSpec((1, 1, 128), lambda g, c: (g, 0, 0)),
        tall=lambda cm: pl.BlockSpec((1, HEADS_PER_GROUP, 1), lambda g, c: (g, 0, 0)),
        grp=lambda cm: pl.BlockSpec((rows, GROUP_COLS), lambda g, c: (cm(c), g)),
        zp=lambda cm: pl.BlockSpec((rows, GROUP_COLS), lambda g, c: (cm(c), P_Z // GROUP_COLS + g)),
        vec=lambda cm: pl.BlockSpec((1, GROUP_COLS), lambda g, c: (0, g)),
        state=lambda cm: pl.BlockSpec((1, SSD_STEP, PAIRS, SSD_STATE, 128), lambda g, c: (g, cm(c), 0, 0, 0)),
    )


def _ssd_fwd(xc, p, hv, norm_g, name):
    t = xc.shape[0]
    nc, sp = _ssd_specs(t)
    ident = lambda c: c

    def body(xs_ref, b_ref, c_ref, dtw_ref, dtt_ref, aw_ref, at_ref, dk_ref, z_ref, ng_ref,
             y_ref, ys_ref, h_ref, st_ref):
        @pl.when(pl.program_id(1) == 0)
        def _():
            st_ref[...] = jnp.zeros_like(st_ref)

        lo = lax.broadcasted_iota(jnp.int32, (1, 128), 1) < 64
        dskip = dk_ref[0]
        for s in range(SSD_STEP):
            rows = slice(s * CHUNK, (s + 1) * CHUNK)
            dt = dtw_ref[0, rows, :]
            a, acs, acst, causal, _ = _ssd_common(dt, dtt_ref[0, :, rows], aw_ref[0], at_ref[0])
            ecs = jnp.exp(acs)
            alast = acs[CHUNK - 1:CHUNK, :]
            bmat, cmat = b_ref[rows, :], c_ref[rows, :]
            cb = lax.dot_general(cmat, bmat, _DIMS["nt"], preferred_element_type=F32)
            for pi in range(PAIRS):
                cols = slice(pi * 128, (pi + 1) * 128)
                x = xs_ref[rows, cols].astype(F32)
                xdt = x * _pair_bc(dt, lo, pi)
                xdtb = xdt.astype(BF16)
                ydiag = jnp.zeros((CHUNK, 128), F32)
                for r, mask in ((2 * pi, lo), (2 * pi + 1, jnp.logical_not(lo))):
                    lam = jnp.exp(acs[:, r:r + 1] - acst[r:r + 1, :] + causal)
                    m = (cb * lam).astype(BF16)
                    ydiag = ydiag + jnp.dot(m, jnp.where(mask, xdtb, 0), preferred_element_type=F32)
                ht = st_ref[pi]
                h_ref[0, s, pi] = ht
                yoff = jnp.dot(cmat, ht.astype(BF16), preferred_element_type=F32) * _pair_bc(ecs, lo, pi)
                y_ref[rows, cols] = (ydiag + yoff + _pair_bc(dskip, lo, pi) * x).astype(y_ref.dtype)
                alp = _pair_bc(alast, lo, pi)
                e = jnp.exp(alp - _pair_bc(acs, lo, pi))
                st = lax.dot_general(bmat, (xdt * e).astype(BF16), _DIMS["tn"], preferred_element_type=F32)
                st_ref[pi] = ht * jnp.exp(alp) + st
            zf = z_ref[rows, :].astype(F32)
            yg = y_ref[rows, :].astype(F32) * (zf * _sigmoid(zf))
            rstd = lax.rsqrt(jnp.mean(yg * yg, axis=-1, keepdims=True) + EPS)
            ys_ref[rows, :] = (yg * rstd * ng_ref[...]).astype(ys_ref.dtype)

    ins = ["xs", "bm", "cmat", "dtw", "dtt", "wide", "tall", "wide", "zp", "vec"]
    return pl.pallas_call(
        body, name=name, grid=(SSD_GROUPS, nc),
        in_specs=[sp[k](ident) for k in ins],
        out_specs=[sp["grp"](ident), sp["grp"](ident), sp["state"](ident)],
        out_shape=[jax.ShapeDtypeStruct((t, SSD_INNER), BF16), jax.ShapeDtypeStruct((t, SSD_INNER), BF16),
                   jax.ShapeDtypeStruct((SSD_GROUPS, t // CHUNK, PAIRS, SSD_STATE, 128), F32)],
        scratch_shapes=[pltpu.VMEM((PAIRS, SSD_STATE, 128), F32)], compiler_params=_cparams("parallel", "arbitrary"),
    )(xc, xc, xc, hv["dtw"], hv["dtt"], hv["alog_w"], hv["alog_t"], hv["dskip_w"], p, norm_g)


def _ssd_bwd(xc, p, hv, norm_g, y, dys, states, dp, name):
    t = xc.shape[0]
    nc, sp = _ssd_specs(t)
    rev = lambda c: nc - 1 - c

    def body(xs_ref, b_ref, c_ref, dtw_ref, dtt_ref, aw_ref, at_ref, dk_ref, z_ref, ng_ref,
             y_ref, dys_ref, h_ref, dp_ref,
             dxs_ref, db_ref, dc_ref, dz_ref, ddt_ref, hsum_ref, dng_ref, dst_ref):
        del dp_ref
        @pl.when(pl.program_id(1) == 0)
        def _():
            dst_ref[...] = jnp.zeros_like(dst_ref)
            hsum_ref[...] = jnp.zeros_like(hsum_ref)
            dng_ref[...] = jnp.zeros_like(dng_ref)

        lane = lax.broadcasted_iota(jnp.int32, (1, 128), 1)
        lo = lane < 64
        dskip = dk_ref[0]
        sel_r = lax.broadcasted_iota(jnp.int32, (128, 128), 0)
        sel_c = lax.broadcasted_iota(jnp.int32, (128, 128), 1)
        refs = (xs_ref, b_ref, c_ref, dtw_ref, dtt_ref, aw_ref, at_ref, dk_ref, z_ref, ng_ref, y_ref, dys_ref, h_ref,
                dxs_ref, db_ref, dc_ref, dz_ref, ddt_ref, hsum_ref, dng_ref, dst_ref)
        for s in reversed(range(SSD_STEP)):
            chunk_bwd(refs, slice(s * CHUNK, (s + 1) * CHUNK), s, lane, lo, dskip, sel_r, sel_c)

    def chunk_bwd(refs, rows, s, lane, lo, dskip, sel_r, sel_c):
        (xs_ref, b_ref, c_ref, dtw_ref, dtt_ref, aw_ref, at_ref, dk_ref, z_ref, ng_ref, y_ref, dys_ref, h_ref,
         dxs_ref, db_ref, dc_ref, dz_ref, ddt_ref, hsum_ref, dng_ref, dst_ref) = refs
        dt = dtw_ref[0, rows, :]
        a, acs, acst, causal, upper = _ssd_common(dt, dtt_ref[0, :, rows], aw_ref[0], at_ref[0])
        ecs = jnp.exp(acs)
        alast = acs[CHUNK - 1:CHUNK, :]
        bmat, cmat = b_ref[rows, :], c_ref[rows, :]
        cb = lax.dot_general(cmat, bmat, _DIMS["nt"], preferred_element_type=F32)

        zf = z_ref[rows, :].astype(F32)
        sg = _sigmoid(zf)
        sz = zf * sg
        yv = y_ref[rows, :].astype(F32)
        yg = yv * sz
        rstd = lax.rsqrt(jnp.mean(yg * yg, axis=-1, keepdims=True) + EPS)
        yhat = yg * rstd
        dysv = dys_ref[rows, :].astype(F32)
        dng_ref[...] += jnp.sum(dysv * yhat, axis=0, keepdims=True)
        dyh = dysv * ng_ref[...]
        dyg = rstd * (dyh - yhat * jnp.mean(dyh * yhat, axis=-1, keepdims=True))
        dz_ref[rows, :] = (dyg * yv * (sg * (1.0 + zf * (1.0 - sg)))).astype(dz_ref.dtype)
        dy_all = dyg * sz

        dal = jnp.zeros((CHUNK, 128), F32)
        ddtm = jnp.zeros((CHUNK, 128), F32)
        dalast = jnp.zeros((8, 128), F32)
        ddsk = jnp.zeros((8, 128), F32)
        dcb = jnp.zeros((CHUNK, CHUNK), F32)
        qcol = jnp.zeros((8, CHUNK), F32)
        sub8 = lax.broadcasted_iota(jnp.int32, (8, CHUNK), 0)
        dc_acc = jnp.zeros((CHUNK, SSD_STATE), F32)
        db_acc = jnp.zeros((CHUNK, SSD_STATE), F32)
        for pi in range(PAIRS):
            sel = (sel_c == 2 * pi + (sel_r >= 64).astype(jnp.int32)).astype(BF16)

            def hsum(v, sel=sel):
                return jnp.dot(v.astype(BF16), sel, preferred_element_type=F32)

            dyp = dy_all[:, pi * 128:(pi + 1) * 128]
            x = xs_ref[rows, pi * 128:(pi + 1) * 128].astype(F32)
            dtp = _pair_bc(dt, lo, pi)
            xdt = x * dtp
            dxdt = jnp.zeros((CHUNK, 128), F32)
            dypb, xdtb = dyp.astype(BF16), xdt.astype(BF16)
            for r, mask in ((2 * pi, lo), (2 * pi + 1, jnp.logical_not(lo))):
                lam = jnp.exp(acs[:, r:r + 1] - acst[r:r + 1, :] + causal)
                m32 = cb * lam
                m = m32.astype(BF16)
                dyr = jnp.where(mask, dypb, 0)
                xr = jnp.where(mask, xdtb, 0)
                dm = lax.dot_general(dyr, xr, _DIMS["nt"], preferred_element_type=F32)
                dcb = dcb + dm * lam
                q = dm * m32
                dal = dal + jnp.sum(q, axis=1, keepdims=True) * (lane == r).astype(F32)
                qcol = qcol + jnp.where(sub8 == r, jnp.sum(q, axis=0, keepdims=True), 0.0)
                dxdt = dxdt + lax.dot_general(m, dyr, _DIMS["tn"], preferred_element_type=F32)
            ht = h_ref[0, s, pi]
            htb = ht.astype(BF16)
            ecp = _pair_bc(ecs, lo, pi)
            yoff = jnp.dot(cmat, htb, preferred_element_type=F32) * ecp
            dg = (dyp * ecp).astype(BF16)
            dc_acc = dc_acc + lax.dot_general(dg, htb, _DIMS["nt"], preferred_element_type=F32)
            dht = lax.dot_general(cmat, dg, _DIMS["tn"], preferred_element_type=F32)
            dal = dal + hsum(dyp * yoff)
            dhn = dst_ref[pi]
            dhnb = dhn.astype(BF16)
            alp = _pair_bc(alast, lo, pi)
            e = jnp.exp(alp - _pair_bc(acs, lo, pi))
            xe = xdt * e
            db_acc = db_acc + lax.dot_general(xe.astype(BF16), dhnb, _DIMS["nt"], preferred_element_type=F32)
            dxe = jnp.dot(bmat, dhnb, preferred_element_type=F32)
            dxdt = dxdt + dxe * e
            tt = hsum(dxe * xe)
            dal = dal - tt
            dec = jnp.exp(alp)
            dalast = dalast + jnp.sum(tt, axis=0, keepdims=True) + hsum(
                jnp.broadcast_to(jnp.sum(dhn * ht, axis=0, keepdims=True) * dec, (8, 128)))
            dst_ref[pi] = dht + dhn * dec
            dxs_ref[rows, pi * 128:(pi + 1) * 128] = (_pair_bc(dskip, lo, pi) * dyp + dxdt * dtp).astype(dxs_ref.dtype)
            ddtm = ddtm + hsum(dxdt * x)
            ddsk = ddsk + hsum(jnp.broadcast_to(jnp.sum(dyp * x, axis=0, keepdims=True), (8, 128)))
        rowi = lax.broadcasted_iota(jnp.int32, (CHUNK, 128), 0)
        qcol_w = lax.dot_general(jnp.concatenate([qcol, jnp.zeros((CHUNK - 8, CHUNK), F32)], axis=0), (sel_r == sel_c).astype(F32),
                                 _DIMS["tn"], precision=HI, preferred_element_type=F32)
        dal = dal - qcol_w + jnp.where(rowi == CHUNK - 1, dalast[0:1, :], 0.0)
        dda = jnp.dot(upper, dal, precision=HI, preferred_element_type=F32)
        ddt_ref[0, rows, :] = ddtm + dda * a
        hsum_ref[0, 1:2, :] += jnp.sum(dda * dt, axis=0, keepdims=True) * a
        hsum_ref[0, 2:3, :] += ddsk[0:1, :]
        dcbb = dcb.astype(BF16)
        dc_ref[rows, :] = (jnp.dot(dcbb, bmat, preferred_element_type=F32) + dc_acc).astype(dc_ref.dtype)
        db_ref[rows, :] = (lax.dot_general(dcbb, cmat, _DIMS["tn"], preferred_element_type=F32) + db_acc).astype(db_ref.dtype)

    ins = ["xs", "bm", "cmat", "dtw", "dtt", "wide", "tall", "wide", "zp", "vec", "grp", "grp", "state"]
    col = lambda: pl.BlockSpec((SSD_STEP * CHUNK, SSD_STATE), lambda g, c: (rev(c), g))
    return pl.pallas_call(
        body, name=name, grid=(SSD_GROUPS, nc),
        in_specs=[sp[k](rev) for k in ins] + [ANY],
        out_specs=[sp["grp"](rev), col(), col(), sp["zp"](rev), sp["dtw"](rev),
                   pl.BlockSpec((1, 8, 128), lambda g, c: (g, 0, 0)), sp["vec"](rev)],
        out_shape=[jax.ShapeDtypeStruct((t, SSD_INNER), BF16), jax.ShapeDtypeStruct((t, SSD_GROUPS * SSD_STATE), BF16),
                   jax.ShapeDtypeStruct((t, SSD_GROUPS * SSD_STATE), BF16), jax.ShapeDtypeStruct(dp.shape, dp.dtype),
                   jax.ShapeDtypeStruct((SSD_GROUPS, t, 128), F32), jax.ShapeDtypeStruct((SSD_GROUPS, 8, 128), F32),
                   jax.ShapeDtypeStruct((1, SSD_INNER), F32)],
        input_output_aliases={len(ins): 3},
        scratch_shapes=[pltpu.VMEM((PAIRS, SSD_STATE, 128), F32)], compiler_params=_cparams("parallel", "arbitrary"),
    )(xc, xc, xc, hv["dtw"], hv["dtt"], hv["alog_w"], hv["alog_t"], hv["dskip_w"], p, norm_g, y, dys, states, dp)


def _wide(v):
    return jnp.pad(v.reshape(SSD_GROUPS, 1, HEADS_PER_GROUP), ((0, 0), (0, 0), (0, 128 - HEADS_PER_GROUP)))


def _head_inputs(dt, a_log, d_skip):
    t = dt.shape[0]
    g = dt[:, :SSD_HEADS].reshape(t, SSD_GROUPS, HEADS_PER_GROUP)
    return dict(
        dtw=jnp.pad(jnp.transpose(g, (1, 0, 2)), ((0, 0), (0, 0), (0, 128 - HEADS_PER_GROUP))),
        dtt=jnp.transpose(g, (1, 2, 0)),
        alog_w=_wide(a_log), alog_t=a_log.reshape(SSD_GROUPS, HEADS_PER_GROUP, 1),
        dskip_w=_wide(d_skip),
    )


def _gelu(x):
    return 0.5 * x * (1.0 + lax.erf(x * (1.0 / math.sqrt(2.0))))


def _gelu_grad(x):
    return 0.5 * (1.0 + lax.erf(x * (1.0 / math.sqrt(2.0)))) + x * jnp.exp(-0.5 * x * x) * (1.0 / math.sqrt(2.0 * math.pi))


def _tril_mask():
    r = lax.broadcasted_iota(jnp.int32, (CHUNK, CHUNK), 0)
    c = lax.broadcasted_iota(jnp.int32, (CHUNK, CHUNK), 1)
    return r >= c


def _gmlp_fwd(p, v_gain, w_s, b_col, name):
    t = p.shape[0]
    tm = _row_tile(t)
    u0 = P_UV // GMLP_W

    def body(u_ref, v_ref, gn_ref, ws_ref, bs_ref, o_ref):
        v = _gelu(v_ref[...].astype(F32))
        v = (v * lax.rsqrt(jnp.mean(v * v, axis=-1, keepdims=True) + EPS) * gn_ref[...]).astype(BF16)
        tril = _tril_mask()
        wm = [jnp.where(tril, ws_ref[g], 0.0).astype(BF16) for g in range(GMLP_GROUPS)]
        for k in range(tm // CHUNK):
            rows = slice(k * CHUNK, (k + 1) * CHUNK)
            for g in range(GMLP_GROUPS):
                cols = slice(g * 128, (g + 1) * 128)
                mixed = jnp.dot(wm[g], v[rows, cols], preferred_element_type=F32) + bs_ref[g]
                o_ref[rows, cols] = (_gelu(u_ref[rows, cols].astype(F32)) * mixed).astype(o_ref.dtype)

    return pl.pallas_call(
        body, name=name, grid=(t // tm,),
        in_specs=[pl.BlockSpec((tm, GMLP_W), lambda i: (i, u0)), pl.BlockSpec((tm, GMLP_W), lambda i: (i, u0 + 1)),
                  pl.BlockSpec((1, GMLP_W), lambda i: (0, 0)), pl.BlockSpec((GMLP_GROUPS, CHUNK, CHUNK), lambda i: (0, 0, 0)),
                  pl.BlockSpec((GMLP_GROUPS, CHUNK, 1), lambda i: (0, 0, 0))],
        out_specs=pl.BlockSpec((tm, GMLP_W), lambda i: (i, 0)),
        out_shape=jax.ShapeDtypeStruct((t, GMLP_W), BF16), compiler_params=_cparams("parallel"),
    )(p, p, v_gain, w_s, b_col)


def _gmlp_bwd(p, dy, v_gain, w_s, b_col, dp, name):
    t = p.shape[0]
    tm = _row_tile(t)
    u0 = P_UV // GMLP_W

    def body(u_ref, v_ref, dy_ref, gn_ref, ws_ref, bs_ref, dp_ref, duv_ref, dws_ref, dbs_ref, dgn_ref, dvn_ref):
        del dp_ref

        @pl.when(pl.program_id(0) == 0)
        def _():
            dws_ref[...] = jnp.zeros_like(dws_ref)
            dbs_ref[...] = jnp.zeros_like(dbs_ref)
            dgn_ref[...] = jnp.zeros_like(dgn_ref)

        vraw = v_ref[...].astype(F32)
        va = _gelu(vraw)
        rstd = lax.rsqrt(jnp.mean(va * va, axis=-1, keepdims=True) + EPS)
        vhat = va * rstd
        gain = gn_ref[...]
        vn = (vhat * gain).astype(BF16)
        tril = _tril_mask()
        wm = [jnp.where(tril, ws_ref[g], 0.0).astype(BF16) for g in range(GMLP_GROUPS)]
        for k in range(tm // CHUNK):
            rows = slice(k * CHUNK, (k + 1) * CHUNK)
            for g in range(GMLP_GROUPS):
                cols = slice(g * 128, (g + 1) * 128)
                uraw = u_ref[rows, cols].astype(F32)
                vb = vn[rows, cols]
                mixed = jnp.dot(wm[g], vb, preferred_element_type=F32) + bs_ref[g]
                dyb = dy_ref[rows, cols].astype(F32)
                duv_ref[rows, cols] = (dyb * mixed * _gelu_grad(uraw)).astype(duv_ref.dtype)
                dmix = dyb * _gelu(uraw)
                dmb = dmix.astype(BF16)
                dws_ref[g] += jnp.where(tril, lax.dot_general(dmb, vb, _DIMS["nt"], preferred_element_type=F32), 0.0)
                dbs_ref[g] += jnp.sum(dmix, axis=1, keepdims=True)
                dvn_ref[rows, cols] = lax.dot_general(wm[g], dmb, _DIMS["tn"], preferred_element_type=F32)
        dvn = dvn_ref[...]
        dgn_ref[...] += jnp.sum(dvn * vhat, axis=0, keepdims=True)
        dvh = dvn * gain
        dva = rstd * (dvh - vhat * jnp.mean(dvh * vhat, axis=-1, keepdims=True))
        duv_ref[:, GMLP_W:2 * GMLP_W] = (dva * _gelu_grad(vraw)).astype(duv_ref.dtype)

    return pl.pallas_call(
        body, name=name, grid=(t // tm,),
        in_specs=[pl.BlockSpec((tm, GMLP_W), lambda i: (i, u0)), pl.BlockSpec((tm, GMLP_W), lambda i: (i, u0 + 1)),
                  pl.BlockSpec((tm, GMLP_W), lambda i: (i, 0)),
                  pl.BlockSpec((1, GMLP_W), lambda i: (0, 0)), pl.BlockSpec((GMLP_GROUPS, CHUNK, CHUNK), lambda i: (0, 0, 0)),
                  pl.BlockSpec((GMLP_GROUPS, CHUNK, 1), lambda i: (0, 0, 0)), ANY],
        out_specs=[pl.BlockSpec((tm, 2 * GMLP_W), lambda i: (i, P_UV // (2 * GMLP_W))),
                   pl.BlockSpec((GMLP_GROUPS, CHUNK, CHUNK), lambda i: (0, 0, 0)),
                   pl.BlockSpec((GMLP_GROUPS, CHUNK, 1), lambda i: (0, 0, 0)), pl.BlockSpec((1, GMLP_W), lambda i: (0, 0))],
        out_shape=[jax.ShapeDtypeStruct(dp.shape, dp.dtype), jax.ShapeDtypeStruct((GMLP_GROUPS, CHUNK, CHUNK), F32),
                   jax.ShapeDtypeStruct((GMLP_GROUPS, CHUNK, 1), F32), jax.ShapeDtypeStruct((1, GMLP_W), F32)],
        input_output_aliases={6: 0},
        scratch_shapes=[pltpu.VMEM((tm, GMLP_W), F32)], compiler_params=_cparams("arbitrary"),
    )(p, p, dy, v_gain, w_s, b_col, dp)


def _head_masks():
    lane = lax.broadcasted_iota(jnp.int32, (1, MEM_W), 1)
    return [(lane >= h * 64) & (lane < (h + 1) * 64) for h in range(MEM_HEADS)]


def _mem_fwd(p, kv, name):
    t = p.shape[0]
    tm = _row_tile(t)
    q0 = P_Q // MEM_W

    def body(q_ref, kv_ref, o_ref):
        q = q_ref[...]
        k = kv_ref[:, 0:MEM_W].astype(BF16)
        v = kv_ref[:, MEM_W:2 * MEM_W].astype(BF16)
        out = jnp.zeros((tm, MEM_W), F32)
        for mask in _head_masks():
            s = lax.dot_general(jnp.where(mask, q, 0), k, _DIMS["nt"], preferred_element_type=F32) * 0.125
            e = jnp.exp(s - jnp.max(s, axis=-1, keepdims=True))
            pr = (e * (1.0 / jnp.sum(e, axis=-1, keepdims=True))).astype(BF16)
            out = out + jnp.where(mask, jnp.dot(pr, v, preferred_element_type=F32), 0.0)
        o_ref[...] = out.astype(o_ref.dtype)

    return pl.pallas_call(
        body, name=name, grid=(t // tm,),
        in_specs=[pl.BlockSpec((tm, MEM_W), lambda i: (i, q0)), pl.BlockSpec((MEM_LEN, 2 * MEM_W), lambda i: (0, 0))],
        out_specs=pl.BlockSpec((tm, MEM_W), lambda i: (i, 0)),
        out_shape=jax.ShapeDtypeStruct((t, MEM_W), BF16), compiler_params=_cparams("parallel"),
    )(p, kv)


def _mem_bwd(p, kv, dy, dp, name):
    t = p.shape[0]
    tm = _row_tile(t)
    q0 = P_Q // MEM_W
    assert P_W - P_Q == 2 * MEM_W

    def body(q_ref, kv_ref, dy_ref, dp_ref, dq_ref, dkv_ref):
        del dp_ref

        @pl.when(pl.program_id(0) == 0)
        def _():
            dkv_ref[...] = jnp.zeros_like(dkv_ref)

        q = q_ref[...]
        dy = dy_ref[...]
        k = kv_ref[:, 0:MEM_W].astype(BF16)
        v = kv_ref[:, MEM_W:2 * MEM_W].astype(BF16)
        dq = jnp.zeros((tm, MEM_W), F32)
        dk = jnp.zeros((MEM_LEN, MEM_W), F32)
        dv = jnp.zeros((MEM_LEN, MEM_W), F32)
        for mask in _head_masks():
            qh = jnp.where(mask, q, 0)
            dyh = jnp.where(mask, dy, 0)
            s = lax.dot_general(qh, k, _DIMS["nt"], preferred_element_type=F32) * 0.125
            e = jnp.exp(s - jnp.max(s, axis=-1, keepdims=True))
            pr = e * (1.0 / jnp.sum(e, axis=-1, keepdims=True))
            prb = pr.astype(BF16)
            dp = lax.dot_general(dyh, v, _DIMS["nt"], preferred_element_type=F32)
            ds = (pr * (dp - jnp.sum(dp * pr, axis=-1, keepdims=True)) * 0.125).astype(BF16)
            dq = dq + jnp.where(mask, jnp.dot(ds, k, preferred_element_type=F32), 0.0)
            dk = dk + lax.dot_general(ds, qh, _DIMS["tn"], preferred_element_type=F32)
            dv = dv + lax.dot_general(prb, dyh, _DIMS["tn"], preferred_element_type=F32)
        dq_ref[:, 0:MEM_W] = dq.astype(dq_ref.dtype)
        dq_ref[:, MEM_W:2 * MEM_W] = jnp.zeros((tm, MEM_W), dq_ref.dtype)
        dkv_ref[:, 0:MEM_W] += dk
        dkv_ref[:, MEM_W:2 * MEM_W] += dv

    return pl.pallas_call(
        body, name=name, grid=(t // tm,),
        in_specs=[pl.BlockSpec((tm, MEM_W), lambda i: (i, q0)), pl.BlockSpec((MEM_LEN, 2 * MEM_W), lambda i: (0, 0)),
                  pl.BlockSpec((tm, MEM_W), lambda i: (i, 0)), ANY],
        out_specs=[pl.BlockSpec((tm, 2 * MEM_W), lambda i: (i, P_Q // (2 * MEM_W))),
                   pl.BlockSpec((MEM_LEN, 2 * MEM_W), lambda i: (0, 0))],
        out_shape=[jax.ShapeDtypeStruct(dp.shape, dp.dtype), jax.ShapeDtypeStruct((MEM_LEN, 2 * MEM_W), F32)],
        input_output_aliases={3: 0}, compiler_params=_cparams("arbitrary"),
    )(p, kv, dy, dp)


def _merge_fwd(p, b_ssd, b_gmlp, b_mem, name):
    t = p.shape[0]
    tm = _row_tile(t)
    g0 = P_GL // D_MODEL

    def body(g1, g2, g3, b1, b2, b3, o_ref):
        def strip(rows, carry):
            acc = _sigmoid(g1[rows, :].astype(F32)) * b1[rows, :].astype(F32)
            acc = acc + _sigmoid(g2[rows, :].astype(F32)) * b2[rows, :].astype(F32)
            acc = acc + _sigmoid(g3[rows, :].astype(F32)) * b3[rows, :].astype(F32)
            o_ref[rows, :] = acc.astype(o_ref.dtype)
            return carry

        _strips(tm, strip, 0)

    row = pl.BlockSpec((tm, D_MODEL), lambda i: (i, 0))
    return pl.pallas_call(
        body, name=name, grid=(t // tm,),
        in_specs=[pl.BlockSpec((tm, D_MODEL), lambda i, k=k: (i, g0 + k)) for k in range(3)] + [row] * 3,
        out_specs=row, out_shape=jax.ShapeDtypeStruct((t, D_MODEL), BF16), compiler_params=_cparams("parallel"),
    )(p, p, p, b_ssd, b_gmlp, b_mem)


def _merge_bwd(p, dm, b_ssd, b_gmlp, b_mem, dp, name):
    t = p.shape[0]
    tm = _row_tile(t)
    g0 = P_GL // D_MODEL

    def body(g1, g2, g3, dm_ref, b1, b2, b3, dp_ref, d1, d2, d3, dgl_ref):
        del dp_ref

        def strip(rows, carry):
            dmv = dm_ref[rows, :].astype(F32)
            for k, (g_ref, b_ref, d_ref) in enumerate(((g1, b1, d1), (g2, b2, d2), (g3, b3, d3))):
                sg = _sigmoid(g_ref[rows, :].astype(F32))
                d_ref[rows, :] = (dmv * sg).astype(d_ref.dtype)
                dgl_ref[rows, k * D_MODEL:(k + 1) * D_MODEL] = (dmv * b_ref[rows, :].astype(F32) * sg * (1.0 - sg)).astype(dgl_ref.dtype)
            return carry

        _strips(tm, strip, 0)

    row = pl.BlockSpec((tm, D_MODEL), lambda i: (i, 0))
    return pl.pallas_call(
        body, name=name, grid=(t // tm,),
        in_specs=[pl.BlockSpec((tm, D_MODEL), lambda i, k=k: (i, g0 + k)) for k in range(3)] + [row] * 4 + [ANY],
        out_specs=[row, row, row, pl.BlockSpec((tm, 3 * D_MODEL), lambda i: (i, P_GL // (3 * D_MODEL)))],
        out_shape=[jax.ShapeDtypeStruct((t, D_MODEL), BF16)] * 3 + [jax.ShapeDtypeStruct(dp.shape, dp.dtype)],
        input_output_aliases={7: 3}, compiler_params=_cparams("parallel"),
    )(p, p, p, dm, b_ssd, b_gmlp, b_mem, dp)


def _local_step(x, mem, target, w, first_weights, mixer_weights, second_weights, push):
    t = x.shape[0]
    mm = functools.partial(_matmul, tk=1024)

    ffn_weights = lambda tag: (w[f"{tag}_w_gate"], w[f"{tag}_w_up"], w[f"{tag}_w_down"])

    def arriving(tag, fetch):
        def weights(n):
            w.update(fetch(n))
            return ffn_weights(tag)
        return weights

    w = dict(w)
    h1, ffn1_saved = _ffn_forward(x, w["ffn1_norm"], arriving("ffn1", first_weights), "ffn1")
    w.update(mixer_weights(h1))
    n2 = _rms_fwd(h1, w["mix_norm"], "mix_norm")
    p = mm(n2, w["w_in_p"], mode="nt", out_dtype=BF16, tm=2048, tn=1536, name="in_proj")
    dt_raw = mm(n2, w["w_dt"], mode="nt", out_dtype=F32, tm=2048, tn=128, name="dt_proj")
    dt_bias = jnp.pad(w["ssd_dt_bias"], (0, 128 - SSD_HEADS)).reshape(1, 128)
    hv = _head_inputs(_dt_fwd(dt_raw, dt_bias, "dt_fwd"), w["ssd_a_log"], w["ssd_d"])
    xc = _conv_fwd(p, w["ssd_conv_w"], w["ssd_conv_b"], "conv_fwd")
    y_ssd_raw, y_ssd, states = _ssd_fwd(xc, p, hv, w["ssd_norm"], "ssd_fwd")
    b_col = w["gmlp_b_s"].reshape(GMLP_GROUPS, CHUNK, 1)
    y_gmlp = _gmlp_fwd(p, w["gmlp_v_norm"], w["gmlp_w_s"], b_col, "gmlp_fwd")
    mem_n = _rms_fwd(mem, w["mem_norm"], "mem_norm")
    kv = mm(mem_n, w["w_mem_kv"], mode="nn", out_dtype=F32, tm=256, tn=512, name="mem_kv")
    y_mem = _mem_fwd(p, kv, "mem_fwd")
    b_ssd = mm(y_ssd, w["w_branch_ssd"], mode="nn", out_dtype=BF16, tm=2048, tn=1024, tk=2048, name="branch_ssd")
    b_gmlp = mm(y_gmlp, w["w_branch_gmlp"], mode="nn", out_dtype=BF16, tm=2048, tn=1024, name="branch_gmlp")
    b_mem = mm(y_mem, w["w_branch_mem"], mode="nt", out_dtype=BF16, tm=2048, tn=1024, tk=MEM_W, name="branch_mem")
    merged = _merge_fwd(p, b_ssd, b_gmlp, b_mem, "merge_fwd")
    h2 = mm(merged, w["w_out"], mode="nn", out_dtype=F32, tm=2048, tn=1024, addend=h1, name="out_proj")
    h3, ffn2_saved = _ffn_forward(h2, w["ffn2_norm"], arriving("ffn2", second_weights), "ffn2")
    dh3, d_final, loss = _loss_head(h3, w["final_norm"], target, "loss_head")

    g = {"final_norm": d_final}
    big = {}
    dh2, g["ffn2_norm"] = _ffn_backward(dh3, h2, w["ffn2_norm"], w["ffn2_w_gate"], w["ffn2_w_up"], w["ffn2_w_down"], ffn2_saved,
                                        "ffn2", functools.partial(push, 0))
    dmerged = mm(dh2, w["w_out"], mode="nt", out_dtype=BF16, tm=2048, tn=1024, name="out_proj_dx")
    big["w_out"] = mm(merged, dh2, mode="tn", out_dtype=BF16, tm=1024, tn=1024, tk=2048, name="out_proj_dw")
    db_ssd, db_gmlp, db_mem, dp = _merge_bwd(p, dmerged, b_ssd, b_gmlp, b_mem, lax.empty((t, P_W), BF16), "merge_bwd")
    dy_ssd = mm(db_ssd, w["w_branch_ssd"], mode="nt", out_dtype=BF16, tm=2048, tn=2048, name="branch_ssd_dx")
    dy_gmlp = mm(db_gmlp, w["w_branch_gmlp"], mode="nt", out_dtype=BF16, tm=2048, tn=1024, name="branch_gmlp_dx")
    dy_mem = mm(db_mem, w["w_branch_mem"], mode="nn", out_dtype=BF16, tm=2048, tn=256, name="branch_mem_dx")
    big["w_branch_ssd"] = mm(y_ssd, db_ssd, mode="tn", out_dtype=BF16, tm=1024, tn=1024, tk=2048, name="branch_ssd_dw")
    big["w_branch_gmlp"] = mm(y_gmlp, db_gmlp, mode="tn", out_dtype=BF16, tm=1024, tn=1024, tk=2048, name="branch_gmlp_dw")
    big["w_branch_mem"] = mm(db_mem, y_mem, mode="tn", out_dtype=BF16, tm=1024, tn=256, tk=2048, name="branch_mem_dw")
    dp, dkv = _mem_bwd(p, kv, dy_mem, dp, "mem_bwd")
    big["w_mem_kv"] = mm(mem_n, dkv, mode="tn", out_dtype=BF16, tm=1024, tn=512, tk=256, name="mem_kv_dw")
    dmem_n = mm(dkv, w["w_mem_kv"], mode="nt", out_dtype=F32, tm=256, tn=1024, tk=512, name="mem_kv_dx")
    _, g["mem_norm"] = _rms_bwd(mem, w["mem_norm"], dmem_n, None, "mem_norm_bwd")
    dp, d_ws, d_bs, g["gmlp_v_norm"] = _gmlp_bwd(p, dy_gmlp, w["gmlp_v_norm"], w["gmlp_w_s"], b_col, dp, "gmlp_bwd")
    g["gmlp_w_s"] = d_ws
    g["gmlp_b_s"] = d_bs.reshape(GMLP_GROUPS, CHUNK)
    dxs, d_bm, d_cm, dp, ddt_w, hsums, g["ssd_norm"] = _ssd_bwd(xc, p, hv, w["ssd_norm"], y_ssd_raw, dy_ssd, states, dp, "ssd_bwd")
    heads = hsums[:, :, :HEADS_PER_GROUP]
    g["ssd_a_log"] = heads[:, 1, :].reshape(1, SSD_HEADS)
    g["ssd_d"] = heads[:, 2, :].reshape(1, SSD_HEADS)
    ddt = jnp.transpose(ddt_w[:, :, :HEADS_PER_GROUP], (1, 0, 2)).reshape(t, SSD_HEADS)
    ddt, d_bias = _dt_bwd(jnp.pad(ddt, ((0, 0), (0, 128 - SSD_HEADS))), dt_raw, dt_bias, "dt_bwd")
    g["ssd_dt_bias"] = d_bias[:, :SSD_HEADS]
    dws, dbs = [], []
    for dyc, col0, tag in ((dxs, 0, "x"), (d_bm, SSD_INNER, "b"), (d_cm, SSD_INNER + SSD_GROUPS * SSD_STATE, "c")):
        dacc, dw_c, db_c = _conv_bwd_act(p, dyc, w["ssd_conv_w"], w["ssd_conv_b"], col0, f"conv_bwd_act_{tag}")
        dp = _conv_bwd_dx(dacc, w["ssd_conv_w"], col0, dp, f"conv_bwd_dx_{tag}")
        dws.append(dw_c)
        dbs.append(db_c)
    g["ssd_conv_w"] = jnp.concatenate(dws, axis=1)
    g["ssd_conv_b"] = jnp.concatenate(dbs, axis=1)
    d_win_p = _matmul(dp, n2, mode="tn", out_dtype=BF16, tm=1536, tn=1024, tk=2048, name="in_proj_dw")
    d_wdt = mm(ddt, n2, mode="tn", out_dtype=BF16, tm=128, tn=1024, tk=2048, name="dt_proj_dw")
    sl = lambda a, o, n: a[o:o + n]
    big["w_in"] = jnp.concatenate([sl(d_win_p, P_Z, 2048), sl(d_win_p, P_XBC, XBC), d_wdt[:SSD_HEADS], sl(d_win_p, P_UV, 2048),
                                   sl(d_win_p, P_Q, MEM_W), sl(d_win_p, P_GL, 3 * D_MODEL)], axis=0)
    token = push(1, big)
    dn2 = mm(dp, w["w_in_p"], mode="nn", out_dtype=F32, tm=1024, tn=1024, tk=3584, name="in_proj_dx")
    dn2 = _matmul(ddt, w["w_dt"], mode="nn", out_dtype=F32, tm=2048, tn=1024, tk=128, addend=dn2, name="dt_proj_dx")
    dh1, g["mix_norm"] = _rms_bwd(h1, w["mix_norm"] + token, dn2, dh2, "mix_norm_bwd")
    dx, g["ffn1_norm"] = _ffn_backward(dh1, x, w["ffn1_norm"], w["ffn1_w_gate"], w["ffn1_w_up"], w["ffn1_w_down"], ffn1_saved,
                                       "ffn1", functools.partial(push, 2))
    return loss, dx, g


def _split_w_in(w_in_t):
    sl = lambda o, n: w_in_t[o:o + n]
    w_p = jnp.concatenate([sl(IN_GL, 3 * D_MODEL), sl(IN_Z, 2048), sl(IN_XBC, XBC), sl(IN_UV, 2048), sl(IN_Q, MEM_W),
                           jnp.zeros((P_W - P_USED, D_MODEL), w_in_t.dtype)], axis=0)
    w_dt = jnp.pad(sl(IN_DT, SSD_HEADS), ((0, 128 - SSD_HEADS), (0, 0)))
    return w_p, w_dt


def _pick_tile(rows, cap=512):
    best = None
    for tile in range(8, min(rows, cap) + 1, 8):
        if rows % tile == 0:
            best = tile
    return best if best is not None else rows


def _adamw(w, g, m, v, name):
    rows, lanes = w.shape
    tile = _pick_tile(rows, cap=max(8, (512 * 1024 // lanes) // 8 * 8))
    c1 = 1.0 / (1.0 - ADAM_B1 ** ADAM_STEP)
    c2 = 1.0 / (1.0 - ADAM_B2 ** ADAM_STEP)

    def body(w_ref, g_ref, m_ref, v_ref, d_ref, nm_ref, nv_ref):
        gv = g_ref[...]
        nm = ADAM_B1 * m_ref[...] + (1.0 - ADAM_B1) * gv
        nv = ADAM_B2 * v_ref[...] + (1.0 - ADAM_B2) * (gv * gv)
        nm_ref[...] = nm
        nv_ref[...] = nv
        d_ref[...] = -ADAM_LR * ((nm * c1) / (jnp.sqrt(nv * c2) + ADAM_EPS) + ADAM_WD * w_ref[...])

    blk = pl.BlockSpec((tile, lanes), lambda i: (i, 0))
    return pl.pallas_call(
        body, name=name, grid=(rows // tile,), in_specs=[blk] * 4, out_specs=[blk] * 3,
        out_shape=[jax.ShapeDtypeStruct((rows, lanes), F32)] * 3, compiler_params=_cparams("parallel"),
    )(w, g, m, v)


HBM = pl.BlockSpec(memory_space=pltpu.HBM)


def _place():
    x, y, c = lax.axis_index("x"), lax.axis_index("y"), lax.axis_index("c")
    chips = [(1 - x, y), (x, 1 - y), (1 - x, 1 - y)]
    return x, y, c, chips


SEM = pl.BlockSpec(memory_space=pltpu.SEMAPHORE)
EFFECT = pltpu.SideEffectType.DATAFLOW_SIDE_EFFECTING
N_PEER = 3


def _sem_outs():
    return tuple(pltpu.SemaphoreType.DMA(()) for _ in range(2 * N_PEER))


def _gather_start(slots, tag):
    def body(in_ref, *refs):
        del in_ref
        sems, thru, token = refs[:2 * N_PEER], refs[2 * N_PEER], refs[2 * N_PEER + 1]
        x, y, c, chips = _place()
        own = thru.at[2 * x + y, c]
        for j, chip in enumerate(chips):
            pltpu.make_async_remote_copy(src_ref=own, dst_ref=own, send_sem=sems[j], recv_sem=sems[N_PEER + j],
                                         device_id=(*chip, c), device_id_type=MESH).start()
        token[...] = jnp.zeros_like(token)

    out = pl.pallas_call(
        body, name=f"gather_{tag}_start",
        out_shape=_sem_outs() + (pltpu.HBM(slots.shape, slots.dtype), jax.ShapeDtypeStruct((8, 128), F32)),
        in_specs=(HBM,), out_specs=(SEM,) * (2 * N_PEER) + (HBM, pl.BlockSpec(memory_space=pltpu.VMEM)),
        input_output_aliases={0: 2 * N_PEER}, compiler_params=pltpu.CompilerParams(has_side_effects=EFFECT),
    )(pltpu.with_memory_space_constraint(slots, pltpu.HBM))
    return out[:2 * N_PEER], out[2 * N_PEER], out[2 * N_PEER + 1]


def _gather_wait(sems, thru, after, tag):
    def body(in_ref, *refs):
        del in_ref
        sems, out_ref = refs[:2 * N_PEER], refs[2 * N_PEER + 1]
        x, y, c, chips = _place()
        own = out_ref.at[2 * x + y, c]
        for j, (cx, cy) in enumerate(chips):
            cp = pltpu.make_async_remote_copy(src_ref=own, dst_ref=out_ref.at[2 * cx + cy, c], send_sem=sems[j],
                                              recv_sem=sems[N_PEER + j], device_id=(cx, cy, c), device_id_type=MESH)
            cp.wait_send()
            cp.wait_recv()

    return pl.pallas_call(
        body, name=f"gather_{tag}_wait", out_shape=pltpu.HBM(thru.shape, thru.dtype),
        in_specs=(HBM,) + (SEM,) * (2 * N_PEER) + (pl.BlockSpec(memory_space=pl.ANY),), out_specs=HBM,
        input_output_aliases={0: 0}, compiler_params=pltpu.CompilerParams(has_side_effects=EFFECT),
    )(thru, *sems, after)


def _gather_forward(slots, tag):
    def body(in_ref, out_ref, send_sems, recv_sems):
        del in_ref
        x, y, c, chips = _place()
        cps = []
        for j, (cx, cy) in enumerate(chips):
            landed = out_ref.at[2 * cx + cy, c]
            cps.append(pltpu.make_async_remote_copy(src_ref=landed, dst_ref=landed, send_sem=send_sems.at[j], recv_sem=recv_sems.at[j],
                                                    device_id=(x, y, 1 - c), device_id_type=MESH))
        for cp in cps:
            cp.start()
        for j, (cx, cy) in enumerate(chips):
            other = out_ref.at[2 * cx + cy, 1 - c]
            pltpu.make_async_remote_copy(src_ref=other, dst_ref=other, send_sem=send_sems.at[j], recv_sem=recv_sems.at[j],
                                         device_id=(x, y, 1 - c), device_id_type=MESH).wait_recv()
        for cp in cps:
            cp.wait_send()

    return pl.pallas_call(
        body, name=f"gather_{tag}_forward", out_shape=jax.ShapeDtypeStruct(slots.shape, slots.dtype),
        in_specs=[HBM], out_specs=HBM, input_output_aliases={0: 0},
        scratch_shapes=[pltpu.SemaphoreType.DMA((N_PEER,)), pltpu.SemaphoreType.DMA((N_PEER,))],
    )(slots)


def _scatter_start(pa, tag):
    ns, rh, lanes = pa.shape
    land = pltpu.with_memory_space_constraint(lax.empty((N_PEER, rh, lanes), pa.dtype), pltpu.HBM)

    def body(pa_ref, land_ref, *refs):
        x, y, c, chips = _place()
        for j, (cx, cy) in enumerate(chips):
            pltpu.make_async_remote_copy(src_ref=pa_ref.at[2 * cx + cy], dst_ref=land_ref.at[j], send_sem=refs[j],
                                         recv_sem=refs[N_PEER + j], device_id=(cx, cy, c), device_id_type=MESH).start()
        refs[-1][...] = jnp.zeros_like(refs[-1])

    out = pl.pallas_call(
        body, name=f"scatter_start_{tag}",
        out_shape=_sem_outs() + (pltpu.HBM(pa.shape, pa.dtype), pltpu.HBM(land.shape, land.dtype), jax.ShapeDtypeStruct((8, 128), F32)),
        in_specs=(HBM, HBM), out_specs=(SEM,) * (2 * N_PEER) + (HBM, HBM, pl.BlockSpec(memory_space=pltpu.VMEM)),
        input_output_aliases={0: 2 * N_PEER, 1: 2 * N_PEER + 1}, compiler_params=pltpu.CompilerParams(has_side_effects=EFFECT),
    )(pltpu.with_memory_space_constraint(pa, pltpu.HBM), land)
    return (out[:2 * N_PEER], out[2 * N_PEER], out[2 * N_PEER + 1]), out[2 * N_PEER + 2]


def _scatter_wait(sems, pa_thru, land_thru, after, tag):
    def body(pa_ref, land_ref, *refs):
        sems = refs[:2 * N_PEER]
        x, y, c, chips = _place()
        for j, (cx, cy) in enumerate(chips):
            cp = pltpu.make_async_remote_copy(src_ref=pa_ref.at[2 * cx + cy], dst_ref=land_ref.at[j], send_sem=sems[j],
                                              recv_sem=sems[N_PEER + j], device_id=(cx, cy, c), device_id_type=MESH)
            cp.wait_send()
            cp.wait_recv()

    return pl.pallas_call(
        body, name=f"scatter_wait_{tag}",
        out_shape=(pltpu.HBM(pa_thru.shape, pa_thru.dtype), pltpu.HBM(land_thru.shape, land_thru.dtype)),
        in_specs=(HBM, HBM) + (SEM,) * (2 * N_PEER) + (pl.BlockSpec(memory_space=pl.ANY),), out_specs=(HBM, HBM),
        input_output_aliases={0: 0, 1: 1}, compiler_params=pltpu.CompilerParams(has_side_effects=EFFECT),
    )(pa_thru, land_thru, *sems, after)


def _rs_swap(gp, tag):
    ns, _, rh, lanes = gp.shape

    def body(in_ref, out_ref, send_sem, recv_sem):
        x, y, c, _ = _place()
        cp = pltpu.make_async_remote_copy(src_ref=in_ref.at[:, 1 - c], dst_ref=out_ref, send_sem=send_sem, recv_sem=recv_sem,
                                          device_id=(x, y, 1 - c), device_id_type=MESH)
        cp.start()
        cp.wait_send()
        cp.wait_recv()

    return pl.pallas_call(
        body, name=f"rs_swap_{tag}", out_shape=jax.ShapeDtypeStruct((ns, rh, lanes), gp.dtype), in_specs=[HBM], out_specs=HBM,
        scratch_shapes=[pltpu.SemaphoreType.DMA, pltpu.SemaphoreType.DMA],
    )(gp)


def _rs_tile(rh):
    return _pick_tile(rh, cap=512)


def _rs_add(gp, recv, c, tag):
    ns, _, rh, lanes = gp.shape
    tile = _rs_tile(rh)

    def body(c_ref, a_ref, b_ref, o_ref):
        o_ref[...] = (a_ref[...].astype(F32) + b_ref[...].astype(F32)).astype(o_ref.dtype)

    return pl.pallas_call(
        body, name=f"rs_add_{tag}", out_shape=jax.ShapeDtypeStruct((ns, rh, lanes), gp.dtype),
        grid_spec=pltpu.PrefetchScalarGridSpec(
            num_scalar_prefetch=1, grid=(ns, rh // tile),
            in_specs=[pl.BlockSpec((None, None, tile, lanes), lambda s, i, c_ref: (s, c_ref[0], i, 0)),
                      pl.BlockSpec((None, tile, lanes), lambda s, i, c_ref: (s, i, 0))],
            out_specs=pl.BlockSpec((None, tile, lanes), lambda s, i, c_ref: (s, i, 0))),
        compiler_params=_cparams("parallel", "parallel"),
    )(c, gp, recv)


def _rs_sum(pa, recv, place, tag):
    ns, rh, lanes = pa.shape
    tile = _rs_tile(rh)

    def body(place_ref, a_ref, r_ref, o_ref):
        acc = a_ref[...].astype(F32)
        for j in range(ns - 1):
            acc = acc + r_ref[j].astype(F32)
        o_ref[...] = acc

    return pl.pallas_call(
        body, name=f"rs_sum_{tag}", out_shape=jax.ShapeDtypeStruct((2, rh, lanes), F32),
        grid_spec=pltpu.PrefetchScalarGridSpec(
            num_scalar_prefetch=1, grid=(rh // tile,),
            in_specs=[pl.BlockSpec((None, tile, lanes), lambda i, place_ref: (place_ref[0], i, 0)),
                      pl.BlockSpec((ns - 1, tile, lanes), lambda i, place_ref: (0, i, 0))],
            out_specs=pl.BlockSpec((None, tile, lanes), lambda i, place_ref: (place_ref[1], i, 0))),
        compiler_params=_cparams("parallel"),
    )(place, pa, recv)


def _rs_share(halves, tag):
    def body(in_ref, out_ref, send_sem, recv_sem):
        del in_ref
        x, y, c, _ = _place()
        cp = pltpu.make_async_remote_copy(src_ref=out_ref.at[c], dst_ref=out_ref.at[c], send_sem=send_sem, recv_sem=recv_sem,
                                          device_id=(x, y, 1 - c), device_id_type=MESH)
        cp.start()
        other = out_ref.at[1 - c]
        pltpu.make_async_remote_copy(src_ref=other, dst_ref=other, send_sem=send_sem, recv_sem=recv_sem,
                                     device_id=(x, y, 1 - c), device_id_type=MESH).wait_recv()
        cp.wait_send()

    return pl.pallas_call(
        body, name=f"rs_share_{tag}", out_shape=jax.ShapeDtypeStruct(halves.shape, halves.dtype), in_specs=[HBM], out_specs=HBM,
        input_output_aliases={0: 0}, scratch_shapes=[pltpu.SemaphoreType.DMA, pltpu.SemaphoreType.DMA],
    )(halves)


N_DEV = 8
SMALL_ROWS = 160


def _allreduce_small(v):
    m_per, n = v.shape

    def body(x_ref, out_ref, all_ref, send_sems, recv_sems, local_sem):
        x, y, c, chips = _place()
        me, sibling = (x, y, c), (x, y, 1 - c)

        def rows(px, py, pc):
            return all_ref.at[pl.ds((4 * px + 2 * py + pc) * m_per, m_per), :]

        def copy(k, block, to, src=None):
            return pltpu.make_async_remote_copy(src_ref=rows(*block) if src is None else src, dst_ref=rows(*block),
                                                send_sem=send_sems.at[k], recv_sem=recv_sems.at[k], device_id=to, device_id_type=MESH)

        mine = pltpu.make_async_copy(x_ref, rows(*me), local_sem)
        mine.start()
        first = [copy(0, me, sibling, src=x_ref)]
        first += [copy(1 + j, me, (*chip, c), src=x_ref) for j, chip in enumerate(chips)]
        for cp in first:
            cp.start()
        passed = [copy(4 + j, (*chip, c), sibling) for j, chip in enumerate(chips)]
        for j, chip in enumerate(chips):
            copy(1 + j, (*chip, c), me).wait_recv()
            passed[j].start()
        copy(0, sibling, me).wait_recv()
        for j, chip in enumerate(chips):
            copy(4 + j, (*chip, 1 - c), me).wait_recv()
        for cp in first + passed:
            cp.wait_send()
        mine.wait()
        step = 32
        for r in range(0, m_per, step):
            acc = all_ref[r:r + step, :]
            for d in range(1, N_DEV):
                acc = acc + all_ref[d * m_per + r:d * m_per + r + step, :]
            out_ref[r:r + step, :] = acc

    vm = pl.BlockSpec(memory_space=pltpu.VMEM)
    return pl.pallas_call(
        body, name="allreduce_small", out_shape=jax.ShapeDtypeStruct((m_per, n), v.dtype), in_specs=[vm], out_specs=vm,
        scratch_shapes=[pltpu.VMEM((N_DEV * m_per, n), v.dtype), pltpu.SemaphoreType.DMA((7,)), pltpu.SemaphoreType.DMA((7,)),
                        pltpu.SemaphoreType.DMA],
        compiler_params=pltpu.CompilerParams(vmem_limit_bytes=V7X_VMEM_LIMIT),
    )(v)


BIG = {"ffn1_w_gate": ((D_MODEL, D_FF), 1), "ffn1_w_up": ((D_MODEL, D_FF), 1), "ffn1_w_down": ((D_FF, D_MODEL), 0),
       "ffn2_w_gate": ((D_MODEL, D_FF), 1), "ffn2_w_up": ((D_MODEL, D_FF), 1), "ffn2_w_down": ((D_FF, D_MODEL), 0),
       "w_in": ((D_MODEL, IN_WIDTH), 1), "w_mem_kv": ((D_MODEL, 2 * MEM_W), 0), "w_branch_ssd": ((SSD_INNER, D_MODEL), 0),
       "w_branch_gmlp": ((GMLP_W, D_MODEL), 0), "w_branch_mem": ((MEM_W, D_MODEL), 1), "w_out": ((D_MODEL, D_MODEL), 0)}
FFN1 = ("ffn1_w_gate", "ffn1_w_up", "ffn1_w_down")
FFN2 = ("ffn2_w_gate", "ffn2_w_up", "ffn2_w_down")
MIXER = ("w_out", "w_branch_ssd", "w_branch_gmlp", "w_branch_mem", "w_mem_kv", "w_in")
REDUCE_GROUPS = (FFN2, MIXER, FFN1)
CONV_W_ROWS = 8


def _shard_rows_of(name):
    (a, b), _ = BIG[name]
    return a * b // N_SHARD // LANES


def _group_rows(names, extra=0):
    return -(-(sum(_shard_rows_of(n) for n in names) + extra) // 32) * 32

SMALL = [("ffn1_norm", 1), ("mix_norm", 1), ("mem_norm", 1), ("ssd_conv_b", 3), ("heads", 1), ("ssd_norm", 2),
         ("gmlp_v_norm", 1), ("gmlp_w_s", 128), ("gmlp_b_s", 1), ("ffn2_norm", 1), ("final_norm", 1), ("ssd_conv_w", 12)]
assert sum(n for _, n in SMALL) <= SMALL_ROWS
HEAD_VECS = ("ssd_dt_bias", "ssd_a_log", "ssd_d")


def _pack_small(vals, loss=None):
    parts = []
    for name, nrows in SMALL:
        if name == "heads":
            row = jnp.concatenate([vals[k].reshape(-1) for k in HEAD_VECS]
                                  + [jnp.zeros((1,), F32) if loss is None else loss.reshape(1)])
            parts.append(jnp.pad(row, (0, LANES - row.shape[0])).reshape(1, LANES))
        elif name in vals:
            parts.append(vals[name].reshape(nrows, LANES))
        else:
            parts.append(jnp.zeros((nrows, LANES), F32))
    buf = jnp.concatenate(parts, axis=0)
    return jnp.pad(buf, ((0, SMALL_ROWS - buf.shape[0]), (0, 0)))


def _unpack_small(buf):
    out, r = {}, 0
    for name, nrows in SMALL:
        blk = buf[r:r + nrows]
        r += nrows
        if name == "heads":
            for i, k in enumerate(HEAD_VECS):
                out[k] = blk[0, i * SSD_HEADS:(i + 1) * SSD_HEADS]
            out["loss"] = blk[0, 3 * SSD_HEADS]
        else:
            out[name] = blk
    return out


def _wire_shape(name):
    (a, b), axis = BIG[name]
    return (b, a) if axis == 1 else (a, b)


def _pack_weights(given, names, conv=False):
    parts = [(given[n][0].T if BIG[n][1] == 1 else given[n][0]).astype(BF16).reshape(_shard_rows_of(n), LANES) for n in names]
    if conv:
        pairs = lax.bitcast_convert_type(given["ssd_conv_w"], BF16).reshape(-1)
        parts.append(jnp.pad(pairs, (0, CONV_W_ROWS * LANES - pairs.shape[0])).reshape(CONV_W_ROWS, LANES))
    total = _group_rows(names, CONV_W_ROWS if conv else 0)
    packed = jnp.concatenate(parts, axis=0)
    packed = jnp.pad(packed, ((0, total - packed.shape[0]), (0, 0))).reshape(1, 2, total // 2, LANES)
    return jnp.broadcast_to(packed, (N_SHARD, 2, total // 2, LANES))


def _unpack_weights(slots, names, conv=False):
    rows = slots.reshape(N_SHARD, -1, LANES)
    out, r = {}, 0
    for name in names:
        n = _shard_rows_of(name)
        out[name] = rows[:, r:r + n].reshape(_wire_shape(name))
        r += n
    if conv:
        cols = XBC // N_SHARD
        pairs = rows[:, r:r + CONV_W_ROWS].reshape(N_SHARD, -1)[:, :SSD_CONV * cols * 2].reshape(N_SHARD, SSD_CONV, cols, 2)
        out["ssd_conv_w"] = jnp.transpose(lax.bitcast_convert_type(pairs, F32), (1, 0, 2)).reshape(SSD_CONV, XBC)
    return out


def _pack_grads(grads, names):
    total = _group_rows(names)
    parts = [grads[n].astype(BF16).reshape(N_SHARD, _shard_rows_of(n), LANES) for n in names]
    pad = total - sum(p.shape[1] for p in parts)
    if pad:
        parts.append(jnp.zeros((N_SHARD, pad, LANES), BF16))
    return jnp.concatenate(parts, axis=1).reshape(N_SHARD, 2, total // 2, LANES)


def kernel(x, mem, ffn1_norm, ffn1_w_gate, ffn1_w_up, ffn1_w_down, mix_norm, mem_norm, w_in, ssd_conv_w, ssd_conv_b, ssd_dt_bias, ssd_a_log, ssd_d, ssd_norm, gmlp_v_norm, gmlp_w_s, gmlp_b_s, w_mem_kv, w_branch_ssd, w_branch_gmlp, w_branch_mem, w_out, ffn2_norm, ffn2_w_gate, ffn2_w_up, ffn2_w_down, final_norm, loss_target, m_ffn1_norm, m_ffn1_w_gate, m_ffn1_w_up, m_ffn1_w_down, m_mix_norm, m_mem_norm, m_w_in, m_ssd_conv_w, m_ssd_conv_b, m_ssd_dt_bias, m_ssd_a_log, m_ssd_d, m_ssd_norm, m_gmlp_v_norm, m_gmlp_w_s, m_gmlp_b_s, m_w_mem_kv, m_w_branch_ssd, m_w_branch_gmlp, m_w_branch_mem, m_w_out, m_ffn2_norm, m_ffn2_w_gate, m_ffn2_w_up, m_ffn2_w_down, m_final_norm, v_ffn1_norm, v_ffn1_w_gate, v_ffn1_w_up, v_ffn1_w_down, v_mix_norm, v_mem_norm, v_w_in, v_ssd_conv_w, v_ssd_conv_b, v_ssd_dt_bias, v_ssd_a_log, v_ssd_d, v_ssd_norm, v_gmlp_v_norm, v_gmlp_w_s, v_gmlp_b_s, v_w_mem_kv, v_w_branch_ssd, v_w_branch_gmlp, v_w_branch_mem, v_w_out, v_ffn2_norm, v_ffn2_w_gate, v_ffn2_w_up, v_ffn2_w_down, v_final_norm):
    given = dict(x=x, mem=mem, ffn1_norm=ffn1_norm, ffn1_w_gate=ffn1_w_gate, ffn1_w_up=ffn1_w_up, ffn1_w_down=ffn1_w_down, mix_norm=mix_norm, mem_norm=mem_norm, w_in=w_in, ssd_conv_w=ssd_conv_w, ssd_conv_b=ssd_conv_b, ssd_dt_bias=ssd_dt_bias, ssd_a_log=ssd_a_log, ssd_d=ssd_d, ssd_norm=ssd_norm, gmlp_v_norm=gmlp_v_norm, gmlp_w_s=gmlp_w_s, gmlp_b_s=gmlp_b_s, w_mem_kv=w_mem_kv, w_branch_ssd=w_branch_ssd, w_branch_gmlp=w_branch_gmlp, w_branch_mem=w_branch_mem, w_out=w_out, ffn2_norm=ffn2_norm, ffn2_w_gate=ffn2_w_gate, ffn2_w_up=ffn2_w_up, ffn2_w_down=ffn2_w_down, final_norm=final_norm, loss_target=loss_target, m_ffn1_norm=m_ffn1_norm, m_ffn1_w_gate=m_ffn1_w_gate, m_ffn1_w_up=m_ffn1_w_up, m_ffn1_w_down=m_ffn1_w_down, m_mix_norm=m_mix_norm, m_mem_norm=m_mem_norm, m_w_in=m_w_in, m_ssd_conv_w=m_ssd_conv_w, m_ssd_conv_b=m_ssd_conv_b, m_ssd_dt_bias=m_ssd_dt_bias, m_ssd_a_log=m_ssd_a_log, m_ssd_d=m_ssd_d, m_ssd_norm=m_ssd_norm, m_gmlp_v_norm=m_gmlp_v_norm, m_gmlp_w_s=m_gmlp_w_s, m_gmlp_b_s=m_gmlp_b_s, m_w_mem_kv=m_w_mem_kv, m_w_branch_ssd=m_w_branch_ssd, m_w_branch_gmlp=m_w_branch_gmlp, m_w_branch_mem=m_w_branch_mem, m_w_out=m_w_out, m_ffn2_norm=m_ffn2_norm, m_ffn2_w_gate=m_ffn2_w_gate, m_ffn2_w_up=m_ffn2_w_up, m_ffn2_w_down=m_ffn2_w_down, m_final_norm=m_final_norm, v_ffn1_norm=v_ffn1_norm, v_ffn1_w_gate=v_ffn1_w_gate, v_ffn1_w_up=v_ffn1_w_up, v_ffn1_w_down=v_ffn1_w_down, v_mix_norm=v_mix_norm, v_mem_norm=v_mem_norm, v_w_in=v_w_in, v_ssd_conv_w=v_ssd_conv_w, v_ssd_conv_b=v_ssd_conv_b, v_ssd_dt_bias=v_ssd_dt_bias, v_ssd_a_log=v_ssd_a_log, v_ssd_d=v_ssd_d, v_ssd_norm=v_ssd_norm, v_gmlp_v_norm=v_gmlp_v_norm, v_gmlp_w_s=v_gmlp_w_s, v_gmlp_b_s=v_gmlp_b_s, v_w_mem_kv=v_w_mem_kv, v_w_branch_ssd=v_w_branch_ssd, v_w_branch_gmlp=v_w_branch_gmlp, v_w_branch_mem=v_w_branch_mem, v_w_out=v_w_out, v_ffn2_norm=v_ffn2_norm, v_ffn2_w_gate=v_ffn2_w_gate, v_ffn2_w_up=v_ffn2_w_up, v_ffn2_w_down=v_ffn2_w_down, v_final_norm=v_final_norm)
    weights = [n for n in given if n not in ("x", "mem", "loss_target") and not n.startswith(("m_", "v_"))]
    xi, yi, ci = lax.axis_index("x"), lax.axis_index("y"), lax.axis_index("c")
    chip = (2 * xi + yi).astype(jnp.int32)
    core = ci.astype(jnp.int32)
    conv_cols = XBC // N_SHARD

    copies = {"first": _gather_start(_pack_weights(given, FFN1), "first")}
    tied = dict(given)
    tied["w_in"], tied["ffn2_w_gate"], _ = lax.optimization_barrier((given["w_in"], given["ffn2_w_gate"], copies["first"][2]))
    packed = {"mixer": _pack_weights(tied, MIXER, conv=True), "ffn2": _pack_weights(tied, FFN2)}
    w = {}
    for name in ("ffn1_norm", "mix_norm", "mem_norm", "ssd_conv_b", "ssd_norm", "gmlp_v_norm", "ffn2_norm", "final_norm"):
        w[name] = given[name].reshape(1, -1)
    w["ffn1_norm"] = w["ffn1_norm"] + copies["first"][2][0:1, 0:1]
    for name in HEAD_VECS:
        w[name] = given[name].reshape(-1)
    w["gmlp_w_s"] = given["gmlp_w_s"][0]
    w["gmlp_b_s"] = given["gmlp_b_s"][0]

    def arrived(tag, after):
        sems, thru, _ = copies[tag]
        return _gather_forward(_gather_wait(sems, thru, after, tag), tag)

    def then_start(slots, tag, weights, name):
        slots, nxt = lax.optimization_barrier((slots, packed[tag]))
        copies[tag] = _gather_start(nxt, tag)
        unpacked = weights(slots)
        unpacked[name], _ = lax.optimization_barrier((unpacked[name], copies[tag][2]))
        return unpacked

    def first_weights(after):
        after, packed["mixer"], packed["ffn2"] = lax.optimization_barrier((after, packed["mixer"], packed["ffn2"]))
        return then_start(arrived("first", after), "mixer", lambda s: _unpack_weights(s, FFN1), FFN1[0])

    def mixer_weights(after):
        def unpack(slots):
            rest = _unpack_weights(slots, MIXER, conv=True)
            rest["w_in_p"], rest["w_dt"] = _split_w_in(rest.pop("w_in"))
            return rest
        return then_start(arrived("mixer", after), "ffn2", unpack, "w_in_p")

    def second_weights(after):
        return _unpack_weights(arrived("ffn2", after), FFN2)

    pending = {}

    def push(k, group_grads):
        gp = _pack_grads(group_grads, REDUCE_GROUPS[k])
        pa = _rs_add(gp, _rs_swap(gp, k), core.reshape(1), k)
        pending[k], token = _scatter_start(pa, k)
        return token[0:1, 0:1]

    def reduced(k, after):
        pa, land = _scatter_wait(*pending[k], after, k)
        gsum = _rs_share(_rs_sum(pa, land, jnp.stack([chip, core]), k), k)
        rows = gsum.reshape(-1, LANES)
        out, r = {}, 0
        for name in REDUCE_GROUPS[k]:
            n = _shard_rows_of(name)
            a, b = given[name].shape[1:]
            out[name] = rows[r:r + n].reshape(b, a).T if BIG[name][1] == 1 else rows[r:r + n].reshape(a, b)
            r += n
        return out

    loss_part, grad_x, g = _local_step(x[0], mem[0], loss_target[0], w, first_weights, mixer_weights, second_weights, push)

    grads, deltas, new_m, new_v = {}, {}, {}, {}

    def update(k, after):
        for name, gl in reduced(k, after).items():
            d, nm, nv = _adamw(given[name][0], gl, given["m_" + name][0], given["v_" + name][0], f"adamw_{name}")
            grads[name], deltas[name], new_m[name], new_v[name] = (a[None] for a in (gl, d, nm, nv))

    update(0, grad_x)
    update(1, deltas[REDUCE_GROUPS[0][-1]])

    small_vals = {k: g[k] for k, _ in SMALL if k != "heads"}
    small_vals.update({k: g[k] for k in HEAD_VECS})
    red = _unpack_small(_allreduce_small(_pack_small(small_vals, loss=loss_part[0, 0])))
    update(2, deltas[REDUCE_GROUPS[1][-1]])
    conv_g = lax.dynamic_slice_in_dim(red["ssd_conv_w"].reshape(SSD_CONV, XBC), chip * conv_cols, conv_cols, axis=1)
    d, nm, nv = _adamw(given["ssd_conv_w"][0], conv_g, given["m_ssd_conv_w"][0], given["v_ssd_conv_w"][0], "adamw_conv_w")
    grads["ssd_conv_w"], deltas["ssd_conv_w"], new_m["ssd_conv_w"], new_v["ssd_conv_w"] = (a[None] for a in (conv_g, d, nm, nv))
    for k in [k for k, _ in SMALL if k not in ("heads", "ssd_conv_w")] + list(HEAD_VECS):
        shape = given[k].shape
        as2d = lambda a: a.reshape(-1, shape[-1])
        d, nm, nv = _adamw(as2d(given[k]), as2d(red[k]), as2d(given["m_" + k]), as2d(given["v_" + k]), f"adamw_{k}")
        grads[k], deltas[k], new_m[k], new_v[k] = (a.reshape(shape) for a in (red[k], d, nm, nv))

    return (red["loss"], grad_x[None], *[grads[n] for n in weights], *[deltas[n] for n in weights],
            *[new_m[n] for n in weights], *[new_v[n] for n in weights])
```

```python
import functools
import math

import jax
import jax.numpy as jnp
from jax import lax
from jax.experimental import pallas as pl
from jax.experimental.pallas import tpu as pltpu

F32, BF16 = jnp.float32, jnp.bfloat16
HI = lax.Precision.HIGHEST
MESH = pl.DeviceIdType.MESH

D_MODEL = 1024
D_FF = 2816
MEM_LEN = 256
SSD_INNER = 2048
SSD_HEADS = 32
SSD_GROUPS = 4
SSD_STATE = 128
SSD_CONV = 4
CHUNK = 128
XBC = SSD_INNER + 2 * SSD_GROUPS * SSD_STATE
GMLP_W = 1024
GMLP_GROUPS = 8
MEM_W = 256
MEM_HEADS = 4
EPS = 1e-6
IN_WIDTH = 10528
IN_Z, IN_XBC, IN_DT, IN_UV, IN_Q, IN_GL = 0, 2048, 5120, 5152, 7200, 7456
P_GL, P_Z, P_XBC, P_UV, P_Q, P_W = 0, 3072, 5120, 8192, 10240, 10752
P_USED = 10496

ADAM_LR, ADAM_B1, ADAM_B2, ADAM_EPS, ADAM_WD, ADAM_STEP = 0.001, 0.9, 0.999, 1e-08, 0.01, 10

V7X_VMEM_LIMIT = 56 * 1024 * 1024
N_SHARD = 4
LANES = 1024


def _cparams(*sem):
    return pltpu.CompilerParams(dimension_semantics=sem, vmem_limit_bytes=V7X_VMEM_LIMIT)


ANY = pl.BlockSpec(memory_space=pl.ANY)


def _sigmoid(x):
    return 0.5 * jnp.tanh(0.5 * x) + 0.5


def _row_tile(t):
    return min(512, t)


_DIMS = {"nn": (((1,), (0,)), ((), ())), "nt": (((1,), (1,)), ((), ())), "tn": (((0,), (0,)), ((), ()))}


def _matmul(a, b, *, mode, out_dtype, tm, tn, tk, name, scale=1.0, addend=None):
    if mode == "tn":
        k_dim, m_dim = a.shape
    else:
        m_dim, k_dim = a.shape
    n_dim = b.shape[0] if mode == "nt" else b.shape[1]
    tm, tn, tk = min(tm, m_dim), min(tn, n_dim), min(tk, k_dim)
    assert m_dim % tm == 0 and n_dim % tn == 0 and k_dim % tk == 0, (name, a.shape, b.shape, tm, tn, tk)
    ni, nj, nk = m_dim // tm, n_dim // tn, k_dim // tk
    a_spec = pl.BlockSpec((tk, tm), lambda j, i, k: (k, i)) if mode == "tn" else pl.BlockSpec((tm, tk), lambda j, i, k: (i, k))
    b_spec = pl.BlockSpec((tn, tk), lambda j, i, k: (j, k)) if mode == "nt" else pl.BlockSpec((tk, tn), lambda j, i, k: (k, j))
    o_spec = pl.BlockSpec((tm, tn), lambda j, i, k: (i, j))
    dims = _DIMS[mode]
    has_add = addend is not None

    def body(*refs):
        a_ref, b_ref = refs[:2]
        r_ref = refs[2] if has_add else None
        o_ref = refs[2 + has_add]

        def finish(acc):
            r = acc * scale if scale != 1.0 else acc
            if has_add:
                r = r + r_ref[...].astype(F32)
            o_ref[...] = r.astype(o_ref.dtype)

        prod = lax.dot_general(a_ref[...].astype(BF16), b_ref[...].astype(BF16), dims, preferred_element_type=F32)
        if nk == 1:
            finish(prod)
            return
        acc_ref = refs[-1]
        k = pl.program_id(2)

        @pl.when(k == 0)
        def _():
            acc_ref[...] = prod

        @pl.when(k > 0)
        def _():
            acc_ref[...] += prod

        @pl.when(k == nk - 1)
        def _():
            finish(acc_ref[...])

    in_specs = [a_spec, b_spec] + ([o_spec] if has_add else [])
    args = (a, b) + ((addend,) if has_add else ())
    return pl.pallas_call(
        body, name=name, grid=(nj, ni, nk), in_specs=in_specs, out_specs=o_spec,
        out_shape=jax.ShapeDtypeStruct((m_dim, n_dim), out_dtype),
        scratch_shapes=[] if nk == 1 else [pltpu.VMEM((tm, tn), F32)],
        compiler_params=_cparams("parallel", "parallel", "arbitrary"),
    )(*args)


ROW_STRIP = 16


def _strips(tm, fn, init=None, rb=ROW_STRIP):
    def step(i, carry):
        return fn(pl.ds(pl.multiple_of(i * rb, rb), rb), carry)
    return lax.fori_loop(0, tm // rb, step, init, unroll=2)


def _rms_fwd(x, gain, name):
    t, d = x.shape
    tm = _row_tile(t)

    def body(x_ref, g_ref, o_ref):
        xv = x_ref[...]
        r = lax.rsqrt(jnp.mean(xv * xv, axis=-1, keepdims=True) + EPS)
        o_ref[...] = (xv * r * g_ref[...]).astype(o_ref.dtype)

    return pl.pallas_call(
        body, name=name, grid=(t // tm,),
        in_specs=[pl.BlockSpec((tm, d), lambda i: (i, 0)), pl.BlockSpec((1, d), lambda i: (0, 0))],
        out_specs=pl.BlockSpec((tm, d), lambda i: (i, 0)),
        out_shape=jax.ShapeDtypeStruct((t, d), BF16), compiler_params=_cparams("parallel"),
    )(x, gain)


def _rms_bwd(x, gain, dn, dres, name):
    t, d = x.shape
    tm = _row_tile(t)
    has_res = dres is not None

    def body(*refs):
        if has_res:
            x_ref, g_ref, dn_ref, r_ref, dx_ref, dg_ref = refs
        else:
            x_ref, g_ref, dn_ref, dx_ref, dg_ref = refs

        @pl.when(pl.program_id(0) == 0)
        def _():
            dg_ref[...] = jnp.zeros_like(dg_ref)

        xv = x_ref[...]
        r = lax.rsqrt(jnp.mean(xv * xv, axis=-1, keepdims=True) + EPS)
        xh = xv * r
        dnv = dn_ref[...].astype(F32)
        dg_ref[...] += jnp.sum(dnv * xh, axis=0, keepdims=True)
        dxh = dnv * g_ref[...]
        dx = r * (dxh - xh * jnp.mean(dxh * xh, axis=-1, keepdims=True))
        if has_res:
            dx = dx + r_ref[...]
        dx_ref[...] = dx

    row = pl.BlockSpec((tm, d), lambda i: (i, 0))
    vec = pl.BlockSpec((1, d), lambda i: (0, 0))
    in_specs = [row, vec, row] + ([row] if has_res else [])
    args = (x, gain, dn) + ((dres,) if has_res else ())
    return pl.pallas_call(
        body, name=name, grid=(t // tm,), in_specs=in_specs, out_specs=[row, vec],
        out_shape=[jax.ShapeDtypeStruct((t, d), F32), jax.ShapeDtypeStruct((1, d), F32)],
        compiler_params=_cparams("arbitrary"),
    )(*args)


def _loss_head(h, gain, target, name):
    t, d = h.shape
    tm = _row_tile(t)

    def body(h_ref, g_ref, t_ref, dh_ref, dg_ref, l_ref):
        @pl.when(pl.program_id(0) == 0)
        def _():
            dg_ref[...] = jnp.zeros_like(dg_ref)
            l_ref[...] = jnp.zeros_like(l_ref)

        xv = h_ref[...]
        g = g_ref[...]
        r = lax.rsqrt(jnp.mean(xv * xv, axis=-1, keepdims=True) + EPS)
        xh = xv * r
        err = xh * g - t_ref[...]
        l_ref[...] += 0.5 * jnp.sum(jnp.mean(err * err, axis=-1, keepdims=True), axis=0, keepdims=True)
        dy = err * (1.0 / d)
        dg_ref[...] += jnp.sum(dy * xh, axis=0, keepdims=True)
        dxh = dy * g
        dh_ref[...] = r * (dxh - xh * jnp.mean(dxh * xh, axis=-1, keepdims=True))

    row = pl.BlockSpec((tm, d), lambda i: (i, 0))
    vec = pl.BlockSpec((1, d), lambda i: (0, 0))
    return pl.pallas_call(
        body, name=name, grid=(t // tm,), in_specs=[row, vec, row],
        out_specs=[row, vec, pl.BlockSpec((1, 128), lambda i: (0, 0))],
        out_shape=[jax.ShapeDtypeStruct((t, d), F32), jax.ShapeDtypeStruct((1, d), F32), jax.ShapeDtypeStruct((1, 128), F32)],
        compiler_params=_cparams("arbitrary"),
    )(h, gain, target)


FF_TILE = 1408


def _ffn_fwd(n, x, wg, wu, wd, name):
    t, d = x.shape
    tm, tn = _row_tile(t), FF_TILE
    nj = D_FF // tn

    def body(n_ref, x_ref, wg_ref, wu_ref, wd_ref, h_ref, g_ref, u_ref, acc_ref):
        j = pl.program_id(1)

        @pl.when(j == 0)
        def _():
            acc_ref[...] = jnp.zeros_like(acc_ref)

        nb = n_ref[...]
        g = lax.dot_general(nb, wg_ref[...], _DIMS["nt"], preferred_element_type=F32)
        u = lax.dot_general(nb, wu_ref[...], _DIMS["nt"], preferred_element_type=F32)
        g_ref[...] = g.astype(BF16)
        u_ref[...] = u.astype(BF16)
        a = g * _sigmoid(g) * u
        acc_ref[...] += jnp.dot(a.astype(BF16), wd_ref[...], preferred_element_type=F32)

        @pl.when(j == nj - 1)
        def _():
            h_ref[...] = x_ref[...] + 0.5 * acc_ref[...]

    row = pl.BlockSpec((tm, d), lambda i, j: (i, 0))
    act = pl.BlockSpec((tm, tn), lambda i, j: (i, j))
    return pl.pallas_call(
        body, name=name, grid=(t // tm, nj),
        in_specs=[row, row] + [pl.BlockSpec((tn, d), lambda i, j: (j, 0))] * 3,
        out_specs=[row, act, act],
        out_shape=[jax.ShapeDtypeStruct((t, d), F32), jax.ShapeDtypeStruct((t, D_FF), BF16), jax.ShapeDtypeStruct((t, D_FF), BF16)],
        scratch_shapes=[pltpu.VMEM((tm, d), F32)], compiler_params=_cparams("parallel", "arbitrary"),
    )(n, x, wg, wu, wd)


def _ffn_bwd_act(dh, g, u, wg, wu, wd, name):
    t, d = dh.shape
    tm, tn = _row_tile(t), FF_TILE
    nj = D_FF // tn

    def body(dh_ref, g_ref, u_ref, wg_ref, wu_ref, wd_ref, dn_ref, dg_ref, du_ref, a_ref, acc_ref):
        j = pl.program_id(1)

        @pl.when(j == 0)
        def _():
            acc_ref[...] = jnp.zeros_like(acc_ref)

        dhb = (0.5 * dh_ref[...]).astype(BF16)
        da = lax.dot_general(dhb, wd_ref[...], _DIMS["nt"], preferred_element_type=F32)
        gv = g_ref[...].astype(F32)
        uv = u_ref[...].astype(F32)
        sg = _sigmoid(gv)
        s = gv * sg
        dg = (da * uv * (sg * (1.0 + gv * (1.0 - sg)))).astype(BF16)
        du = (da * s).astype(BF16)
        dg_ref[...] = dg
        du_ref[...] = du
        a_ref[...] = (s * uv).astype(BF16)
        acc_ref[...] += (jnp.dot(dg, wg_ref[...], preferred_element_type=F32)
                         + jnp.dot(du, wu_ref[...], preferred_element_type=F32))

        @pl.when(j == nj - 1)
        def _():
            dn_ref[...] = acc_ref[...]

    row = pl.BlockSpec((tm, d), lambda i, j: (i, 0))
    act = pl.BlockSpec((tm, tn), lambda i, j: (i, j))
    return pl.pallas_call(
        body, name=name, grid=(t // tm, nj),
        in_specs=[row, act, act] + [pl.BlockSpec((tn, d), lambda i, j: (j, 0))] * 3,
        out_specs=[row, act, act, act],
        out_shape=[jax.ShapeDtypeStruct((t, d), F32)] + [jax.ShapeDtypeStruct((t, D_FF), BF16)] * 3,
        scratch_shapes=[pltpu.VMEM((tm, d), F32)], compiler_params=_cparams("parallel", "arbitrary"),
    )(dh, g, u, wg, wu, wd)


def _ffn_forward(x, gain, weights, tag):
    n = _rms_fwd(x, gain, f"{tag}_norm")
    h, g, u = _ffn_fwd(n, x, *weights(n), f"{tag}_fwd")
    return h, (n, g, u)


def _ffn_backward(dh, x, gain, wg, wu, wd, saved, tag, push):
    n, g, u = saved
    dn, dg, du, a = _ffn_bwd_act(dh, g, u, wg, wu, wd, f"{tag}_bwd_act")
    kw = dict(mode="tn", out_dtype=BF16, tm=FF_TILE, tn=1024, tk=2048)
    d_wg = _matmul(dg, n, name=f"{tag}_dwg", **kw)
    d_wu = _matmul(du, n, name=f"{tag}_dwu", **kw)
    d_wd = _matmul(a, dh, scale=0.5, name=f"{tag}_dwd", **kw)
    token = push({f"{tag}_w_gate": d_wg, f"{tag}_w_up": d_wu, f"{tag}_w_down": d_wd})
    return _rms_bwd(x, gain + token, dn, dh, f"{tag}_norm_bwd")


CONV_COLS = 512
HALO = 8
CONV_STRIP = 32
CONV_ROWS = 2048


def _conv_fwd(p, w, b, name):
    t = p.shape[0]
    tm = min(CONV_ROWS, t)
    c0 = P_XBC // CONV_COLS

    def body(x_ref, halo_ref, w_ref, b_ref, o_ref, s_ref):
        i = pl.program_id(1)
        s_ref[0:HALO, :] = jnp.where(i > 0, halo_ref[...].astype(F32), 0.0)
        s_ref[HALO:HALO + tm, :] = x_ref[...].astype(F32)
        wv = w_ref[...]
        bv = b_ref[...]
        for r0 in range(0, tm, CONV_STRIP):
            acc = bv + wv[0:1, :] * s_ref[HALO - 3 + r0:HALO - 3 + r0 + CONV_STRIP, :]
            for k in range(1, SSD_CONV):
                acc = acc + wv[k:k + 1, :] * s_ref[HALO - 3 + k + r0:HALO - 3 + k + r0 + CONV_STRIP, :]
            o_ref[r0:r0 + CONV_STRIP, :] = (acc * _sigmoid(acc)).astype(o_ref.dtype)

    return pl.pallas_call(
        body, name=name, grid=(XBC // CONV_COLS, t // tm),
        in_specs=[pl.BlockSpec((tm, CONV_COLS), lambda j, i: (i, c0 + j)),
                  pl.BlockSpec((HALO, CONV_COLS), lambda j, i: (jnp.maximum(i * (tm // HALO) - 1, 0), c0 + j)),
                  pl.BlockSpec((SSD_CONV, CONV_COLS), lambda j, i: (0, j)), pl.BlockSpec((1, CONV_COLS), lambda j, i: (0, j))],
        out_specs=pl.BlockSpec((tm, CONV_COLS), lambda j, i: (i, j)),
        out_shape=jax.ShapeDtypeStruct((t, XBC), BF16),
        scratch_shapes=[pltpu.VMEM((tm + HALO, CONV_COLS), F32)], compiler_params=_cparams("parallel", "parallel"),
    )(p, p, w, b)


def _conv_bwd_act(p, dy, w, b, col0, name):
    t, cols = dy.shape
    tm = min(CONV_ROWS, t)
    c0 = (P_XBC + col0) // CONV_COLS
    w0 = col0 // CONV_COLS

    def body(x_ref, halo_ref, dy_ref, w_ref, b_ref, da_ref, dw_ref, db_ref, s_ref):
        i = pl.program_id(1)

        @pl.when(i == 0)
        def _():
            dw_ref[...] = jnp.zeros_like(dw_ref)
            db_ref[...] = jnp.zeros_like(db_ref)

        s_ref[0:HALO, :] = jnp.where(i > 0, halo_ref[...].astype(F32), 0.0)
        s_ref[HALO:HALO + tm, :] = x_ref[...].astype(F32)
        wv = w_ref[...]
        bv = b_ref[...]
        fold = lambda v: jnp.sum(v.reshape(CONV_STRIP // 8, 8, CONV_COLS), axis=0)
        sums = [jnp.zeros((8, CONV_COLS), F32) for _ in range(SSD_CONV + 1)]
        for r0 in range(0, tm, CONV_STRIP):
            taps = [s_ref[HALO - 3 + k + r0:HALO - 3 + k + r0 + CONV_STRIP, :] for k in range(SSD_CONV)]
            acc = bv + wv[0:1, :] * taps[0]
            for k in range(1, SSD_CONV):
                acc = acc + wv[k:k + 1, :] * taps[k]
            sg = _sigmoid(acc)
            dacc = dy_ref[r0:r0 + CONV_STRIP, :].astype(F32) * (sg * (1.0 + acc * (1.0 - sg)))
            da_ref[r0:r0 + CONV_STRIP, :] = dacc.astype(BF16)
            for k in range(SSD_CONV):
                sums[k] = sums[k] + fold(dacc * taps[k])
            sums[SSD_CONV] = sums[SSD_CONV] + fold(dacc)
        for k in range(SSD_CONV):
            dw_ref[k:k + 1, :] += jnp.sum(sums[k], axis=0, keepdims=True)
        db_ref[...] += jnp.sum(sums[SSD_CONV], axis=0, keepdims=True)

    return pl.pallas_call(
        body, name=name, grid=(cols // CONV_COLS, t // tm),
        in_specs=[pl.BlockSpec((tm, CONV_COLS), lambda j, i: (i, c0 + j)),
                  pl.BlockSpec((HALO, CONV_COLS), lambda j, i: (jnp.maximum(i * (tm // HALO) - 1, 0), c0 + j)),
                  pl.BlockSpec((tm, CONV_COLS), lambda j, i: (i, j)),
                  pl.BlockSpec((SSD_CONV, CONV_COLS), lambda j, i: (0, w0 + j)), pl.BlockSpec((1, CONV_COLS), lambda j, i: (0, w0 + j))],
        out_specs=[pl.BlockSpec((tm, CONV_COLS), lambda j, i: (i, j)), pl.BlockSpec((SSD_CONV, CONV_COLS), lambda j, i: (0, j)),
                   pl.BlockSpec((1, CONV_COLS), lambda j, i: (0, j))],
        out_shape=[jax.ShapeDtypeStruct((t, cols), BF16), jax.ShapeDtypeStruct((SSD_CONV, cols), F32), jax.ShapeDtypeStruct((1, cols), F32)],
        scratch_shapes=[pltpu.VMEM((tm + HALO, CONV_COLS), F32)], compiler_params=_cparams("parallel", "arbitrary"),
    )(p, p, dy, w, b)


def _conv_bwd_dx(dacc, w, col0, dp, name):
    t, cols = dacc.shape
    tm = min(CONV_ROWS, t)
    nt = t // tm
    w0 = col0 // CONV_COLS
    c0 = (P_XBC + col0) // CONV_COLS

    def body(d_ref, halo_ref, w_ref, dp_ref, o_ref, s_ref):
        del dp_ref
        i = pl.program_id(1)
        s_ref[0:tm, :] = d_ref[...].astype(F32)
        s_ref[tm:tm + HALO, :] = jnp.where(i < nt - 1, halo_ref[...].astype(F32), 0.0)
        wv = w_ref[...]
        for r0 in range(0, tm, CONV_STRIP):
            acc = wv[3:4, :] * s_ref[r0:r0 + CONV_STRIP, :]
            for k in range(SSD_CONV - 1):
                acc = acc + wv[k:k + 1, :] * s_ref[3 - k + r0:3 - k + r0 + CONV_STRIP, :]
            o_ref[r0:r0 + CONV_STRIP, :] = acc.astype(o_ref.dtype)

    return pl.pallas_call(
        body, name=name, grid=(cols // CONV_COLS, nt),
        in_specs=[pl.BlockSpec((tm, CONV_COLS), lambda j, i: (i, j)),
                  pl.BlockSpec((HALO, CONV_COLS), lambda j, i: (jnp.minimum((i + 1) * (tm // HALO), t // HALO - 1), j)),
                  pl.BlockSpec((SSD_CONV, CONV_COLS), lambda j, i: (0, w0 + j)), ANY],
        out_specs=pl.BlockSpec((tm, CONV_COLS), lambda j, i: (i, c0 + j)),
        out_shape=jax.ShapeDtypeStruct(dp.shape, dp.dtype), input_output_aliases={3: 0},
        scratch_shapes=[pltpu.VMEM((tm + HALO, CONV_COLS), F32)], compiler_params=_cparams("parallel", "parallel"),
    )(dacc, dacc, w, dp)


GROUP_COLS = SSD_INNER // SSD_GROUPS
PAIRS = GROUP_COLS // 128
HEADS_PER_GROUP = SSD_HEADS // SSD_GROUPS


def _dt_fwd(dt_raw, bias, name):
    t, n = dt_raw.shape
    tm = _row_tile(t)

    def body(x_ref, b_ref, o_ref):
        v = x_ref[...] + b_ref[...]
        o_ref[...] = jnp.maximum(v, 0.0) + jnp.log1p(jnp.exp(-jnp.abs(v)))

    row = pl.BlockSpec((tm, n), lambda i: (i, 0))
    vec = pl.BlockSpec((1, n), lambda i: (0, 0))
    return pl.pallas_call(body, name=name, grid=(t // tm,), in_specs=[row, vec], out_specs=row,
                          out_shape=jax.ShapeDtypeStruct((t, n), F32), compiler_params=_cparams("parallel"))(dt_raw, bias)


def _dt_bwd(ddt, dt_raw, bias, name):
    t, n = dt_raw.shape
    tm = _row_tile(t)

    def body(d_ref, x_ref, b_ref, o_ref, db_ref):
        @pl.when(pl.program_id(0) == 0)
        def _():
            db_ref[...] = jnp.zeros_like(db_ref)

        dr = d_ref[...] * _sigmoid(x_ref[...] + b_ref[...])
        o_ref[...] = dr.astype(o_ref.dtype)
        db_ref[...] += jnp.sum(dr, axis=0, keepdims=True)

    row = pl.BlockSpec((tm, n), lambda i: (i, 0))
    vec = pl.BlockSpec((1, n), lambda i: (0, 0))
    return pl.pallas_call(body, name=name, grid=(t // tm,), in_specs=[row, row, vec], out_specs=[row, vec],
                          out_shape=[jax.ShapeDtypeStruct((t, n), BF16), jax.ShapeDtypeStruct((1, n), F32)],
                          compiler_params=_cparams("arbitrary"))(ddt, dt_raw, bias)


SSD_STEP = 8


def _ssd_common(dt, dtt, a_log_w, a_log_t):
    l = CHUNK
    a = -jnp.exp(a_log_w)
    at = -jnp.exp(a_log_t)
    rowi = lax.broadcasted_iota(jnp.int32, (l, l), 0)
    coli = lax.broadcasted_iota(jnp.int32, (l, l), 1)
    tri = rowi >= coli
    lower = tri.astype(F32)
    upper = (rowi <= coli).astype(F32)
    acs = jnp.dot(lower, dt * a, precision=HI, preferred_element_type=F32)
    acst = jnp.dot(dtt * at, upper, precision=HI, preferred_element_type=F32)
    return a, acs, acst, jnp.where(tri, 0.0, -1e30), upper


def _pair_bc(w, lo, p):
    return jnp.where(lo, w[:, 2 * p:2 * p + 1], w[:, 2 * p + 1:2 * p + 2])


def _ssd_specs(t):
    rows = SSD_STEP * CHUNK
    assert t % rows == 0
    return t // rows, dict(
        xs=lambda cm: pl.BlockSpec((rows, GROUP_COLS), lambda g, c: (cm(c), g)),
        bm=lambda cm: pl.BlockSpec((rows, SSD_STATE), lambda g, c: (cm(c), SSD_INNER // SSD_STATE + g)),
        cmat=lambda cm: pl.BlockSpec((rows, SSD_STATE), lambda g, c: (cm(c), SSD_INNER // SSD_STATE + SSD_GROUPS + g)),
        dtw=lambda cm: pl.BlockSpec((1, rows, 128), lambda g, c: (g, cm(c), 0)),
        dtt=lambda cm: pl.BlockSpec((1, HEADS_PER_GROUP, rows), lambda g, c: (g, 0, cm(c))),
        wide=lambda cm: pl.BlockSpec((1, 1, 128), lambda g, c: (g, 0, 0)),
        tall=lambda cm: pl.BlockSpec((1, HEADS_PER_GROUP, 1), lambda g, c: (g, 0, 0)),
        grp=lambda cm: pl.BlockSpec((rows, GROUP_COLS), lambda g, c: (cm(c), g)),
        zp=lambda cm: pl.BlockSpec((rows, GROUP_COLS), lambda g, c: (cm(c), P_Z // GROUP_COLS + g)),
        vec=lambda cm: pl.BlockSpec((1, GROUP_COLS), lambda g, c: (0, g)),
        state=lambda cm: pl.BlockSpec((1, SSD_STEP, PAIRS, SSD_STATE, 128), lambda g, c: (g, cm(c), 0, 0, 0)),
    )


def _ssd_fwd(xc, p, hv, norm_g, name):
    t = xc.shape[0]
    nc, sp = _ssd_specs(t)
    ident = lambda c: c

    def body(xs_ref, b_ref, c_ref, dtw_ref, dtt_ref, aw_ref, at_ref, dk_ref, z_ref, ng_ref,
             y_ref, ys_ref, h_ref, st_ref):
        @pl.when(pl.program_id(1) == 0)
        def _():
            st_ref[...] = jnp.zeros_like(st_ref)

        lo = lax.broadcasted_iota(jnp.int32, (1, 128), 1) < 64
        dskip = dk_ref[0]
        for s in range(SSD_STEP):
            rows = slice(s * CHUNK, (s + 1) * CHUNK)
            dt = dtw_ref[0, rows, :]
            a, acs, acst, causal, _ = _ssd_common(dt, dtt_ref[0, :, rows], aw_ref[0], at_ref[0])
            ecs = jnp.exp(acs)
            alast = acs[CHUNK - 1:CHUNK, :]
            bmat, cmat = b_ref[rows, :], c_ref[rows, :]
            cb = lax.dot_general(cmat, bmat, _DIMS["nt"], preferred_element_type=F32)
            for pi in range(PAIRS):
                cols = slice(pi * 128, (pi + 1) * 128)
                x = xs_ref[rows, cols].astype(F32)
                xdt = x * _pair_bc(dt, lo, pi)
                xdtb = xdt.astype(BF16)
                ydiag = jnp.zeros((CHUNK, 128), F32)
                for r, mask in ((2 * pi, lo), (2 * pi + 1, jnp.logical_not(lo))):
                    lam = jnp.exp(acs[:, r:r + 1] - acst[r:r + 1, :] + causal)
                    m = (cb * lam).astype(BF16)
                    ydiag = ydiag + jnp.dot(m, jnp.where(mask, xdtb, 0), preferred_element_type=F32)
                ht = st_ref[pi]
                h_ref[0, s, pi] = ht
                yoff = jnp.dot(cmat, ht.astype(BF16), preferred_element_type=F32) * _pair_bc(ecs, lo, pi)
                y_ref[rows, cols] = (ydiag + yoff + _pair_bc(dskip, lo, pi) * x).astype(y_ref.dtype)
                alp = _pair_bc(alast, lo, pi)
                e = jnp.exp(alp - _pair_bc(acs, lo, pi))
                st = lax.dot_general(bmat, (xdt * e).astype(BF16), _DIMS["tn"], preferred_element_type=F32)
                st_ref[pi] = ht * jnp.exp(alp) + st
            zf = z_ref[rows, :].astype(F32)
            yg = y_ref[rows, :].astype(F32) * (zf * _sigmoid(zf))
            rstd = lax.rsqrt(jnp.mean(yg * yg, axis=-1, keepdims=True) + EPS)
            ys_ref[rows, :] = (yg * rstd * ng_ref[...]).astype(ys_ref.dtype)

    ins = ["xs", "bm", "cmat", "dtw", "dtt", "wide", "tall", "wide", "zp", "vec"]
    return pl.pallas_call(
        body, name=name, grid=(SSD_GROUPS, nc),
        in_specs=[sp[k](ident) for k in ins],
        out_specs=[sp["grp"](ident), sp["grp"](ident), sp["state"](ident)],
        out_shape=[jax.ShapeDtypeStruct((t, SSD_INNER), BF16), jax.ShapeDtypeStruct((t, SSD_INNER), BF16),
                   jax.ShapeDtypeStruct((SSD_GROUPS, t // CHUNK, PAIRS, SSD_STATE, 128), F32)],
        scratch_shapes=[pltpu.VMEM((PAIRS, SSD_STATE, 128), F32)], compiler_params=_cparams("parallel", "arbitrary"),
    )(xc, xc, xc, hv["dtw"], hv["dtt"], hv["alog_w"], hv["alog_t"], hv["dskip_w"], p, norm_g)


def _ssd_bwd(xc, p, hv, norm_g, y, dys, states, dp, name):
    t = xc.shape[0]
    nc, sp = _ssd_specs(t)
    rev = lambda c: nc - 1 - c

    def body(xs_ref, b_ref, c_ref, dtw_ref, dtt_ref, aw_ref, at_ref, dk_ref, z_ref, ng_ref,
             y_ref, dys_ref, h_ref, dp_ref,
             dxs_ref, db_ref, dc_ref, dz_ref, ddt_ref, hsum_ref, dng_ref, dst_ref):
        del dp_ref
        @pl.when(pl.program_id(1) == 0)
        def _():
            dst_ref[...] = jnp.zeros_like(dst_ref)
            hsum_ref[...] = jnp.zeros_like(hsum_ref)
            dng_ref[...] = jnp.zeros_like(dng_ref)

        lane = lax.broadcasted_iota(jnp.int32, (1, 128), 1)
        lo = lane < 64
        dskip = dk_ref[0]
        sel_r = lax.broadcasted_iota(jnp.int32, (128, 128), 0)
        sel_c = lax.broadcasted_iota(jnp.int32, (128, 128), 1)
        refs = (xs_ref, b_ref, c_ref, dtw_ref, dtt_ref, aw_ref, at_ref, dk_ref, z_ref, ng_ref, y_ref, dys_ref, h_ref,
                dxs_ref, db_ref, dc_ref, dz_ref, ddt_ref, hsum_ref, dng_ref, dst_ref)
        for s in reversed(range(SSD_STEP)):
            chunk_bwd(refs, slice(s * CHUNK, (s + 1) * CHUNK), s, lane, lo, dskip, sel_r, sel_c)

    def chunk_bwd(refs, rows, s, lane, lo, dskip, sel_r, sel_c):
        (xs_ref, b_ref, c_ref, dtw_ref, dtt_ref, aw_ref, at_ref, dk_ref, z_ref, ng_ref, y_ref, dys_ref, h_ref,
         dxs_ref, db_ref, dc_ref, dz_ref, ddt_ref, hsum_ref, dng_ref, dst_ref) = refs
        dt = dtw_ref[0, rows, :]
        a, acs, acst, causal, upper = _ssd_common(dt, dtt_ref[0, :, rows], aw_ref[0], at_ref[0])
        ecs = jnp.exp(acs)
        alast = acs[CHUNK - 1:CHUNK, :]
        bmat, cmat = b_ref[rows, :], c_ref[rows, :]
        cb = lax.dot_general(cmat, bmat, _DIMS["nt"], preferred_element_type=F32)

        zf = z_ref[rows, :].astype(F32)
        sg = _sigmoid(zf)
        sz = zf * sg
        yv = y_ref[rows, :].astype(F32)
        yg = yv * sz
        rstd = lax.rsqrt(jnp.mean(yg * yg, axis=-1, keepdims=True) + EPS)
        yhat = yg * rstd
        dysv = dys_ref[rows, :].astype(F32)
        dng_ref[...] += jnp.sum(dysv * yhat, axis=0, keepdims=True)
        dyh = dysv * ng_ref[...]
        dyg = rstd * (dyh - yhat * jnp.mean(dyh * yhat, axis=-1, keepdims=True))
        dz_ref[rows, :] = (dyg * yv * (sg * (1.0 + zf * (1.0 - sg)))).astype(dz_ref.dtype)
        dy_all = dyg * sz

        dal = jnp.zeros((CHUNK, 128), F32)
        ddtm = jnp.zeros((CHUNK, 128), F32)
        dalast = jnp.zeros((8, 128), F32)
        ddsk = jnp.zeros((8, 128), F32)
        dcb = jnp.zeros((CHUNK, CHUNK), F32)
        qcol = jnp.zeros((8, CHUNK), F32)
        sub8 = lax.broadcasted_iota(jnp.int32, (8, CHUNK), 0)
        dc_acc = jnp.zeros((CHUNK, SSD_STATE), F32)
        db_acc = jnp.zeros((CHUNK, SSD_STATE), F32)
        for pi in range(PAIRS):
            sel = (sel_c == 2 * pi + (sel_r >= 64).astype(jnp.int32)).astype(BF16)

            def hsum(v, sel=sel):
                return jnp.dot(v.astype(BF16), sel, preferred_element_type=F32)

            dyp = dy_all[:, pi * 128:(pi + 1) * 128]
            x = xs_ref[rows, pi * 128:(pi + 1) * 128].astype(F32)
            dtp = _pair_bc(dt, lo, pi)
            xdt = x * dtp
            dxdt = jnp.zeros((CHUNK, 128), F32)
            dypb, xdtb = dyp.astype(BF16), xdt.astype(BF16)
            for r, mask in ((2 * pi, lo), (2 * pi + 1, jnp.logical_not(lo))):
                lam = jnp.exp(acs[:, r:r + 1] - acst[r:r + 1, :] + causal)
                m32 = cb * lam
                m = m32.astype(BF16)
                dyr = jnp.where(mask, dypb, 0)
                xr = jnp.where(mask, xdtb, 0)
                dm = lax.dot_general(dyr, xr, _DIMS["nt"], preferred_element_type=F32)
                dcb = dcb + dm * lam
                q = dm * m32
                dal = dal + jnp.sum(q, axis=1, keepdims=True) * (lane == r).astype(F32)
                qcol = qcol + jnp.where(sub8 == r, jnp.sum(q, axis=0, keepdims=True), 0.0)
                dxdt = dxdt + lax.dot_general(m, dyr, _DIMS["tn"], preferred_element_type=F32)
            ht = h_ref[0, s, pi]
            htb = ht.astype(BF16)
            ecp = _pair_bc(ecs, lo, pi)
            yoff = jnp.dot(cmat, htb, preferred_element_type=F32) * ecp
            dg = (dyp * ecp).astype(BF16)
            dc_acc = dc_acc + lax.dot_general(dg, htb, _DIMS["nt"], preferred_element_type=F32)
            dht = lax.dot_general(cmat, dg, _DIMS["tn"], preferred_element_type=F32)
            dal = dal + hsum(dyp * yoff)
            dhn = dst_ref[pi]
            dhnb = dhn.astype(BF16)
            alp = _pair_bc(alast, lo, pi)
            e = jnp.exp(alp - _pair_bc(acs, lo, pi))
            xe = xdt * e
            db_acc = db_acc + lax.dot_general(xe.astype(BF16), dhnb, _DIMS["nt"], preferred_element_type=F32)
            dxe = jnp.dot(bmat, dhnb, preferred_element_type=F32)
            dxdt = dxdt + dxe * e
            tt = hsum(dxe * xe)
            dal = dal - tt
            dec = jnp.exp(alp)
            dalast = dalast + jnp.sum(tt, axis=0, keepdims=True) + hsum(
                jnp.broadcast_to(jnp.sum(dhn * ht, axis=0, keepdims=True) * dec, (8, 128)))
            dst_ref[pi] = dht + dhn * dec
            dxs_ref[rows, pi * 128:(pi + 1) * 128] = (_pair_bc(dskip, lo, pi) * dyp + dxdt * dtp).astype(dxs_ref.dtype)
            ddtm = ddtm + hsum(dxdt * x)
            ddsk = ddsk + hsum(jnp.broadcast_to(jnp.sum(dyp * x, axis=0, keepdims=True), (8, 128)))
        rowi = lax.broadcasted_iota(jnp.int32, (CHUNK, 128), 0)
        qcol_w = lax.dot_general(jnp.concatenate([qcol, jnp.zeros((CHUNK - 8, CHUNK), F32)], axis=0), (sel_r == sel_c).astype(F32),
                                 _DIMS["tn"], precision=HI, preferred_element_type=F32)
        dal = dal - qcol_w + jnp.where(rowi == CHUNK - 1, dalast[0:1, :], 0.0)
        dda = jnp.dot(upper, dal, precision=HI, preferred_element_type=F32)
        ddt_ref[0, rows, :] = ddtm + dda * a
        hsum_ref[0, 1:2, :] += jnp.sum(dda * dt, axis=0, keepdims=True) * a
        hsum_ref[0, 2:3, :] += ddsk[0:1, :]
        dcbb = dcb.astype(BF16)
        dc_ref[rows, :] = (jnp.dot(dcbb, bmat, preferred_element_type=F32) + dc_acc).astype(dc_ref.dtype)
        db_ref[rows, :] = (lax.dot_general(dcbb, cmat, _DIMS["tn"], preferred_element_type=F32) + db_acc).astype(db_ref.dtype)

    ins = ["xs", "bm", "cmat", "dtw", "dtt", "wide", "tall", "wide", "zp", "vec", "grp", "grp", "state"]
    col = lambda: pl.BlockSpec((SSD_STEP * CHUNK, SSD_STATE), lambda g, c: (rev(c), g))
    return pl.pallas_call(
        body, name=name, grid=(SSD_GROUPS, nc),
        in_specs=[sp[k](rev) for k in ins] + [ANY],
        out_specs=[sp["grp"](rev), col(), col(), sp["zp"](rev), sp["dtw"](rev),
                   pl.BlockSpec((1, 8, 128), lambda g, c: (g, 0, 0)), sp["vec"](rev)],
        out_shape=[jax.ShapeDtypeStruct((t, SSD_INNER), BF16), jax.ShapeDtypeStruct((t, SSD_GROUPS * SSD_STATE), BF16),
                   jax.ShapeDtypeStruct((t, SSD_GROUPS * SSD_STATE), BF16), jax.ShapeDtypeStruct(dp.shape, dp.dtype),
                   jax.ShapeDtypeStruct((SSD_GROUPS, t, 128), F32), jax.ShapeDtypeStruct((SSD_GROUPS, 8, 128), F32),
                   jax.ShapeDtypeStruct((1, SSD_INNER), F32)],
        input_output_aliases={len(ins): 3},
        scratch_shapes=[pltpu.VMEM((PAIRS, SSD_STATE, 128), F32)], compiler_params=_cparams("parallel", "arbitrary"),
    )(xc, xc, xc, hv["dtw"], hv["dtt"], hv["alog_w"], hv["alog_t"], hv["dskip_w"], p, norm_g, y, dys, states, dp)


def _wide(v):
    return jnp.pad(v.reshape(SSD_GROUPS, 1, HEADS_PER_GROUP), ((0, 0), (0, 0), (0, 128 - HEADS_PER_GROUP)))


def _head_inputs(dt, a_log, d_skip):
    t = dt.shape[0]
    g = dt[:, :SSD_HEADS].reshape(t, SSD_GROUPS, HEADS_PER_GROUP)
    return dict(
        dtw=jnp.pad(jnp.transpose(g, (1, 0, 2)), ((0, 0), (0, 0), (0, 128 - HEADS_PER_GROUP))),
        dtt=jnp.transpose(g, (1, 2, 0)),
        alog_w=_wide(a_log), alog_t=a_log.reshape(SSD_GROUPS, HEADS_PER_GROUP, 1),
        dskip_w=_wide(d_skip),
    )


def _gelu(x):
    return 0.5 * x * (1.0 + lax.erf(x * (1.0 / math.sqrt(2.0))))


def _gelu_grad(x):
    return 0.5 * (1.0 + lax.erf(x * (1.0 / math.sqrt(2.0)))) + x * jnp.exp(-0.5 * x * x) * (1.0 / math.sqrt(2.0 * math.pi))


def _tril_mask():
    r = lax.broadcasted_iota(jnp.int32, (CHUNK, CHUNK), 0)
    c = lax.broadcasted_iota(jnp.int32, (CHUNK, CHUNK), 1)
    return r >= c


def _gmlp_fwd(p, v_gain, w_s, b_col, name):
    t = p.shape[0]
    tm = _row_tile(t)
    u0 = P_UV // GMLP_W

    def body(u_ref, v_ref, gn_ref, ws_ref, bs_ref, o_ref):
        v = _gelu(v_ref[...].astype(F32))
        v = (v * lax.rsqrt(jnp.mean(v * v, axis=-1, keepdims=True) + EPS) * gn_ref[...]).astype(BF16)
        tril = _tril_mask()
        wm = [jnp.where(tril, ws_ref[g], 0.0).astype(BF16) for g in range(GMLP_GROUPS)]
        for k in range(tm // CHUNK):
            rows = slice(k * CHUNK, (k + 1) * CHUNK)
            for g in range(GMLP_GROUPS):
                cols = slice(g * 128, (g + 1) * 128)
                mixed = jnp.dot(wm[g], v[rows, cols], preferred_element_type=F32) + bs_ref[g]
                o_ref[rows, cols] = (_gelu(u_ref[rows, cols].astype(F32)) * mixed).astype(o_ref.dtype)

    return pl.pallas_call(
        body, name=name, grid=(t // tm,),
        in_specs=[pl.BlockSpec((tm, GMLP_W), lambda i: (i, u0)), pl.BlockSpec((tm, GMLP_W), lambda i: (i, u0 + 1)),
                  pl.BlockSpec((1, GMLP_W), lambda i: (0, 0)), pl.BlockSpec((GMLP_GROUPS, CHUNK, CHUNK), lambda i: (0, 0, 0)),
                  pl.BlockSpec((GMLP_GROUPS, CHUNK, 1), lambda i: (0, 0, 0))],
        out_specs=pl.BlockSpec((tm, GMLP_W), lambda i: (i, 0)),
        out_shape=jax.ShapeDtypeStruct((t, GMLP_W), BF16), compiler_params=_cparams("parallel"),
    )(p, p, v_gain, w_s, b_col)


def _gmlp_bwd(p, dy, v_gain, w_s, b_col, dp, name):
    t = p.shape[0]
    tm = _row_tile(t)
    u0 = P_UV // GMLP_W

    def body(u_ref, v_ref, dy_ref, gn_ref, ws_ref, bs_ref, dp_ref, duv_ref, dws_ref, dbs_ref, dgn_ref, dvn_ref):
        del dp_ref

        @pl.when(pl.program_id(0) == 0)
        def _():
            dws_ref[...] = jnp.zeros_like(dws_ref)
            dbs_ref[...] = jnp.zeros_like(dbs_ref)
            dgn_ref[...] = jnp.zeros_like(dgn_ref)

        vraw = v_ref[...].astype(F32)
        va = _gelu(vraw)
        rstd = lax.rsqrt(jnp.mean(va * va, axis=-1, keepdims=True) + EPS)
        vhat = va * rstd
        gain = gn_ref[...]
        vn = (vhat * gain).astype(BF16)
        tril = _tril_mask()
        wm = [jnp.where(tril, ws_ref[g], 0.0).astype(BF16) for g in range(GMLP_GROUPS)]
        for k in range(tm // CHUNK):
            rows = slice(k * CHUNK, (k + 1) * CHUNK)
            for g in range(GMLP_GROUPS):
                cols = slice(g * 128, (g + 1) * 128)
                uraw = u_ref[rows, cols].astype(F32)
                vb = vn[rows, cols]
                mixed = jnp.dot(wm[g], vb, preferred_element_type=F32) + bs_ref[g]
                dyb = dy_ref[rows, cols].astype(F32)
                duv_ref[rows, cols] = (dyb * mixed * _gelu_grad(uraw)).astype(duv_ref.dtype)
                dmix = dyb * _gelu(uraw)
                dmb = dmix.astype(BF16)
                dws_ref[g] += jnp.where(tril, lax.dot_general(dmb, vb, _DIMS["nt"], preferred_element_type=F32), 0.0)
                dbs_ref[g] += jnp.sum(dmix, axis=1, keepdims=True)
                dvn_ref[rows, cols] = lax.dot_general(wm[g], dmb, _DIMS["tn"], preferred_element_type=F32)
        dvn = dvn_ref[...]
        dgn_ref[...] += jnp.sum(dvn * vhat, axis=0, keepdims=True)
        dvh = dvn * gain
        dva = rstd * (dvh - vhat * jnp.mean(dvh * vhat, axis=-1, keepdims=True))
        duv_ref[:, GMLP_W:2 * GMLP_W] = (dva * _gelu_grad(vraw)).astype(duv_ref.dtype)

    return pl.pallas_call(
        body, name=name, grid=(t // tm,),
        in_specs=[pl.BlockSpec((tm, GMLP_W), lambda i: (i, u0)), pl.BlockSpec((tm, GMLP_W), lambda i: (i, u0 + 1)),
                  pl.BlockSpec((tm, GMLP_W), lambda i: (i, 0)),
                  pl.BlockSpec((1, GMLP_W), lambda i: (0, 0)), pl.BlockSpec((GMLP_GROUPS, CHUNK, CHUNK), lambda i: (0, 0, 0)),
                  pl.BlockSpec((GMLP_GROUPS, CHUNK, 1), lambda i: (0, 0, 0)), ANY],
        out_specs=[pl.BlockSpec((tm, 2 * GMLP_W), lambda i: (i, P_UV // (2 * GMLP_W))),
                   pl.BlockSpec((GMLP_GROUPS, CHUNK, CHUNK), lambda i: (0, 0, 0)),
                   pl.BlockSpec((GMLP_GROUPS, CHUNK, 1), lambda i: (0, 0, 0)), pl.BlockSpec((1, GMLP_W), lambda i: (0, 0))],
        out_shape=[jax.ShapeDtypeStruct(dp.shape, dp.dtype), jax.ShapeDtypeStruct((GMLP_GROUPS, CHUNK, CHUNK), F32),
                   jax.ShapeDtypeStruct((GMLP_GROUPS, CHUNK, 1), F32), jax.ShapeDtypeStruct((1, GMLP_W), F32)],
        input_output_aliases={6: 0},
        scratch_shapes=[pltpu.VMEM((tm, GMLP_W), F32)], compiler_params=_cparams("arbitrary"),
    )(p, p, dy, v_gain, w_s, b_col, dp)


def _head_masks():
    lane = lax.broadcasted_iota(jnp.int32, (1, MEM_W), 1)
    return [(lane >= h * 64) & (lane < (h + 1) * 64) for h in range(MEM_HEADS)]


def _mem_fwd(p, kv, name):
    t = p.shape[0]
    tm = _row_tile(t)
    q0 = P_Q // MEM_W

    def body(q_ref, kv_ref, o_ref):
        q = q_ref[...]
        k = kv_ref[:, 0:MEM_W].astype(BF16)
        v = kv_ref[:, MEM_W:2 * MEM_W].astype(BF16)
        out = jnp.zeros((tm, MEM_W), F32)
        for mask in _head_masks():
            s = lax.dot_general(jnp.where(mask, q, 0), k, _DIMS["nt"], preferred_element_type=F32) * 0.125
            e = jnp.exp(s - jnp.max(s, axis=-1, keepdims=True))
            pr = (e * (1.0 / jnp.sum(e, axis=-1, keepdims=True))).astype(BF16)
            out = out + jnp.where(mask, jnp.dot(pr, v, preferred_element_type=F32), 0.0)
        o_ref[...] = out.astype(o_ref.dtype)

    return pl.pallas_call(
        body, name=name, grid=(t // tm,),
        in_specs=[pl.BlockSpec((tm, MEM_W), lambda i: (i, q0)), pl.BlockSpec((MEM_LEN, 2 * MEM_W), lambda i: (0, 0))],
        out_specs=pl.BlockSpec((tm, MEM_W), lambda i: (i, 0)),
        out_shape=jax.ShapeDtypeStruct((t, MEM_W), BF16), compiler_params=_cparams("parallel"),
    )(p, kv)


def _mem_bwd(p, kv, dy, dp, name):
    t = p.shape[0]
    tm = _row_tile(t)
    q0 = P_Q // MEM_W
    assert P_W - P_Q == 2 * MEM_W

    def body(q_ref, kv_ref, dy_ref, dp_ref, dq_ref, dkv_ref):
        del dp_ref

        @pl.when(pl.program_id(0) == 0)
        def _():
            dkv_ref[...] = jnp.zeros_like(dkv_ref)

        q = q_ref[...]
        dy = dy_ref[...]
        k = kv_ref[:, 0:MEM_W].astype(BF16)
        v = kv_ref[:, MEM_W:2 * MEM_W].astype(BF16)
        dq = jnp.zeros((tm, MEM_W), F32)
        dk = jnp.zeros((MEM_LEN, MEM_W), F32)
        dv = jnp.zeros((MEM_LEN, MEM_W), F32)
        for mask in _head_masks():
            qh = jnp.where(mask, q, 0)
            dyh = jnp.where(mask, dy, 0)
            s = lax.dot_general(qh, k, _DIMS["nt"], preferred_element_type=F32) * 0.125
            e = jnp.exp(s - jnp.max(s, axis=-1, keepdims=True))
            pr = e * (1.0 / jnp.sum(e, axis=-1, keepdims=True))
            prb = pr.astype(BF16)
            dp = lax.dot_general(dyh, v, _DIMS["nt"], preferred_element_type=F32)
            ds = (pr * (dp - jnp.sum(dp * pr, axis=-1, keepdims=True)) * 0.125).astype(BF16)
            dq = dq + jnp.where(mask, jnp.dot(ds, k, preferred_element_type=F32), 0.0)
            dk = dk + lax.dot_general(ds, qh, _DIMS["tn"], preferred_element_type=F32)
            dv = dv + lax.dot_general(prb, dyh, _DIMS["tn"], preferred_element_type=F32)
        dq_ref[:, 0:MEM_W] = dq.astype(dq_ref.dtype)
        dq_ref[:, MEM_W:2 * MEM_W] = jnp.zeros((tm, MEM_W), dq_ref.dtype)
        dkv_ref[:, 0:MEM_W] += dk
        dkv_ref[:, MEM_W:2 * MEM_W] += dv

    return pl.pallas_call(
        body, name=name, grid=(t // tm,),
        in_specs=[pl.BlockSpec((tm, MEM_W), lambda i: (i, q0)), pl.BlockSpec((MEM_LEN, 2 * MEM_W), lambda i: (0, 0)),
                  pl.BlockSpec((tm, MEM_W), lambda i: (i, 0)), ANY],
        out_specs=[pl.BlockSpec((tm, 2 * MEM_W), lambda i: (i, P_Q // (2 * MEM_W))),
                   pl.BlockSpec((MEM_LEN, 2 * MEM_W), lambda i: (0, 0))],
        out_shape=[jax.ShapeDtypeStruct(dp.shape, dp.dtype), jax.ShapeDtypeStruct((MEM_LEN, 2 * MEM_W), F32)],
        input_output_aliases={3: 0}, compiler_params=_cparams("arbitrary"),
    )(p, kv, dy, dp)


def _merge_fwd(p, b_ssd, b_gmlp, b_mem, name):
    t = p.shape[0]
    tm = _row_tile(t)
    g0 = P_GL // D_MODEL

    def body(g1, g2, g3, b1, b2, b3, o_ref):
        def strip(rows, carry):
            acc = _sigmoid(g1[rows, :].astype(F32)) * b1[rows, :].astype(F32)
            acc = acc + _sigmoid(g2[rows, :].astype(F32)) * b2[rows, :].astype(F32)
            acc = acc + _sigmoid(g3[rows, :].astype(F32)) * b3[rows, :].astype(F32)
            o_ref[rows, :] = acc.astype(o_ref.dtype)
            return carry

        _strips(tm, strip, 0)

    row = pl.BlockSpec((tm, D_MODEL), lambda i: (i, 0))
    return pl.pallas_call(
        body, name=name, grid=(t // tm,),
        in_specs=[pl.BlockSpec((tm, D_MODEL), lambda i, k=k: (i, g0 + k)) for k in range(3)] + [row] * 3,
        out_specs=row, out_shape=jax.ShapeDtypeStruct((t, D_MODEL), BF16), compiler_params=_cparams("parallel"),
    )(p, p, p, b_ssd, b_gmlp, b_mem)


def _merge_bwd(p, dm, b_ssd, b_gmlp, b_mem, dp, name):
    t = p.shape[0]
    tm = _row_tile(t)
    g0 = P_GL // D_MODEL

    def body(g1, g2, g3, dm_ref, b1, b2, b3, dp_ref, d1, d2, d3, dgl_ref):
        del dp_ref

        def strip(rows, carry):
            dmv = dm_ref[rows, :].astype(F32)
            for k, (g_ref, b_ref, d_ref) in enumerate(((g1, b1, d1), (g2, b2, d2), (g3, b3, d3))):
                sg = _sigmoid(g_ref[rows, :].astype(F32))
                d_ref[rows, :] = (dmv * sg).astype(d_ref.dtype)
                dgl_ref[rows, k * D_MODEL:(k + 1) * D_MODEL] = (dmv * b_ref[rows, :].astype(F32) * sg * (1.0 - sg)).astype(dgl_ref.dtype)
            return carry

        _strips(tm, strip, 0)

    row = pl.BlockSpec((tm, D_MODEL), lambda i: (i, 0))
    return pl.pallas_call(
        body, name=name, grid=(t // tm,),
        in_specs=[pl.BlockSpec((tm, D_MODEL), lambda i, k=k: (i, g0 + k)) for k in range(3)] + [row] * 4 + [ANY],
        out_specs=[row, row, row, pl.BlockSpec((tm, 3 * D_MODEL), lambda i: (i, P_GL // (3 * D_MODEL)))],
        out_shape=[jax.ShapeDtypeStruct((t, D_MODEL), BF16)] * 3 + [jax.ShapeDtypeStruct(dp.shape, dp.dtype)],
        input_output_aliases={7: 3}, compiler_params=_cparams("parallel"),
    )(p, p, p, dm, b_ssd, b_gmlp, b_mem, dp)


def _local_step(x, mem, target, w, first_weights, mixer_weights, second_weights, push):
    t = x.shape[0]
    mm = functools.partial(_matmul, tk=1024)

    ffn_weights = lambda tag: (w[f"{tag}_w_gate"], w[f"{tag}_w_up"], w[f"{tag}_w_down"])

    def arriving(tag, fetch):
        def weights(n):
            w.update(fetch(n))
            return ffn_weights(tag)
        return weights

    w = dict(w)
    h1, ffn1_saved = _ffn_forward(x, w["ffn1_norm"], arriving("ffn1", first_weights), "ffn1")
    w.update(mixer_weights(h1))
    n2 = _rms_fwd(h1, w["mix_norm"], "mix_norm")
    p = mm(n2, w["w_in_p"], mode="nt", out_dtype=BF16, tm=2048, tn=1536, name="in_proj")
    dt_raw = mm(n2, w["w_dt"], mode="nt", out_dtype=F32, tm=2048, tn=128, name="dt_proj")
    dt_bias = jnp.pad(w["ssd_dt_bias"], (0, 128 - SSD_HEADS)).reshape(1, 128)
    hv = _head_inputs(_dt_fwd(dt_raw, dt_bias, "dt_fwd"), w["ssd_a_log"], w["ssd_d"])
    xc = _conv_fwd(p, w["ssd_conv_w"], w["ssd_conv_b"], "conv_fwd")
    y_ssd_raw, y_ssd, states = _ssd_fwd(xc, p, hv, w["ssd_norm"], "ssd_fwd")
    b_col = w["gmlp_b_s"].reshape(GMLP_GROUPS, CHUNK, 1)
    y_gmlp = _gmlp_fwd(p, w["gmlp_v_norm"], w["gmlp_w_s"], b_col, "gmlp_fwd")
    mem_n = _rms_fwd(mem, w["mem_norm"], "mem_norm")
    kv = mm(mem_n, w["w_mem_kv"], mode="nn", out_dtype=F32, tm=256, tn=512, name="mem_kv")
    y_mem = _mem_fwd(p, kv, "mem_fwd")
    b_ssd = mm(y_ssd, w["w_branch_ssd"], mode="nn", out_dtype=BF16, tm=2048, tn=1024, tk=2048, name="branch_ssd")
    b_gmlp = mm(y_gmlp, w["w_branch_gmlp"], mode="nn", out_dtype=BF16, tm=2048, tn=1024, name="branch_gmlp")
    b_mem = mm(y_mem, w["w_branch_mem"], mode="nt", out_dtype=BF16, tm=2048, tn=1024, tk=MEM_W, name="branch_mem")
    merged = _merge_fwd(p, b_ssd, b_gmlp, b_mem, "merge_fwd")
    h2 = mm(merged, w["w_out"], mode="nn", out_dtype=F32, tm=2048, tn=1024, addend=h1, name="out_proj")
    h3, ffn2_saved = _ffn_forward(h2, w["ffn2_norm"], arriving("ffn2", second_weights), "ffn2")
    dh3, d_final, loss = _loss_head(h3, w["final_norm"], target, "loss_head")

    g = {"final_norm": d_final}
    big = {}
    dh2, g["ffn2_norm"] = _ffn_backward(dh3, h2, w["ffn2_norm"], w["ffn2_w_gate"], w["ffn2_w_up"], w["ffn2_w_down"], ffn2_saved,
                                        "ffn2", functools.partial(push, 0))
    dmerged = mm(dh2, w["w_out"], mode="nt", out_dtype=BF16, tm=2048, tn=1024, name="out_proj_dx")
    big["w_out"] = mm(merged, dh2, mode="tn", out_dtype=BF16, tm=1024, tn=1024, tk=2048, name="out_proj_dw")
    db_ssd, db_gmlp, db_mem, dp = _merge_bwd(p, dmerged, b_ssd, b_gmlp, b_mem, lax.empty((t, P_W), BF16), "merge_bwd")
    dy_ssd = mm(db_ssd, w["w_branch_ssd"], mode="nt", out_dtype=BF16, tm=2048, tn=2048, name="branch_ssd_dx")
    dy_gmlp = mm(db_gmlp, w["w_branch_gmlp"], mode="nt", out_dtype=BF16, tm=2048, tn=1024, name="branch_gmlp_dx")
    dy_mem = mm(db_mem, w["w_branch_mem"], mode="nn", out_dtype=BF16, tm=2048, tn=256, name="branch_mem_dx")
    big["w_branch_ssd"] = mm(y_ssd, db_ssd, mode="tn", out_dtype=BF16, tm=1024, tn=1024, tk=2048, name="branch_ssd_dw")
    big["w_branch_gmlp"] = mm(y_gmlp, db_gmlp, mode="tn", out_dtype=BF16, tm=1024, tn=1024, tk=2048, name="branch_gmlp_dw")
    big["w_branch_mem"] = mm(db_mem, y_mem, mode="tn", out_dtype=BF16, tm=1024, tn=256, tk=2048, name="branch_mem_dw")
    dp, dkv = _mem_bwd(p, kv, dy_mem, dp, "mem_bwd")
    big["w_mem_kv"] = mm(mem_n, dkv, mode="tn", out_dtype=BF16, tm=1024, tn=512, tk=256, name="mem_kv_dw")
    dmem_n = mm(dkv, w["w_mem_kv"], mode="nt", out_dtype=F32, tm=256, tn=1024, tk=512, name="mem_kv_dx")
    _, g["mem_norm"] = _rms_bwd(mem, w["mem_norm"], dmem_n, None, "mem_norm_bwd")
    dp, d_ws, d_bs, g["gmlp_v_norm"] = _gmlp_bwd(p, dy_gmlp, w["gmlp_v_norm"], w["gmlp_w_s"], b_col, dp, "gmlp_bwd")
    g["gmlp_w_s"] = d_ws
    g["gmlp_b_s"] = d_bs.reshape(GMLP_GROUPS, CHUNK)
    dxs, d_bm, d_cm, dp, ddt_w, hsums, g["ssd_norm"] = _ssd_bwd(xc, p, hv, w["ssd_norm"], y_ssd_raw, dy_ssd, states, dp, "ssd_bwd")
    heads = hsums[:, :, :HEADS_PER_GROUP]
    g["ssd_a_log"] = heads[:, 1, :].reshape(1, SSD_HEADS)
    g["ssd_d"] = heads[:, 2, :].reshape(1, SSD_HEADS)
    ddt = jnp.transpose(ddt_w[:, :, :HEADS_PER_GROUP], (1, 0, 2)).reshape(t, SSD_HEADS)
    ddt, d_bias = _dt_bwd(jnp.pad(ddt, ((0, 0), (0, 128 - SSD_HEADS))), dt_raw, dt_bias, "dt_bwd")
    g["ssd_dt_bias"] = d_bias[:, :SSD_HEADS]
    dws, dbs = [], []
    for dyc, col0, tag in ((dxs, 0, "x"), (d_bm, SSD_INNER, "b"), (d_cm, SSD_INNER + SSD_GROUPS * SSD_STATE, "c")):
        dacc, dw_c, db_c = _conv_bwd_act(p, dyc, w["ssd_conv_w"], w["ssd_conv_b"], col0, f"conv_bwd_act_{tag}")
        dp = _conv_bwd_dx(dacc, w["ssd_conv_w"], col0, dp, f"conv_bwd_dx_{tag}")
        dws.append(dw_c)
        dbs.append(db_c)
    g["ssd_conv_w"] = jnp.concatenate(dws, axis=1)
    g["ssd_conv_b"] = jnp.concatenate(dbs, axis=1)
    d_win_p = _matmul(dp, n2, mode="tn", out_dtype=BF16, tm=1536, tn=1024, tk=2048, name="in_proj_dw")
    d_wdt = mm(ddt, n2, mode="tn", out_dtype=BF16, tm=128, tn=1024, tk=2048, name="dt_proj_dw")
    sl = lambda a, o, n: a[o:o + n]
    big["w_in"] = jnp.concatenate([sl(d_win_p, P_Z, 2048), sl(d_win_p, P_XBC, XBC), d_wdt[:SSD_HEADS], sl(d_win_p, P_UV, 2048),
                                   sl(d_win_p, P_Q, MEM_W), sl(d_win_p, P_GL, 3 * D_MODEL)], axis=0)
    token = push(1, big)
    dn2 = mm(dp, w["w_in_p"], mode="nn", out_dtype=F32, tm=1024, tn=1024, tk=3584, name="in_proj_dx")
    dn2 = _matmul(ddt, w["w_dt"], mode="nn", out_dtype=F32, tm=2048, tn=1024, tk=128, addend=dn2, name="dt_proj_dx")
    dh1, g["mix_norm"] = _rms_bwd(h1, w["mix_norm"] + token, dn2, dh2, "mix_norm_bwd")
    dx, g["ffn1_norm"] = _ffn_backward(dh1, x, w["ffn1_norm"], w["ffn1_w_gate"], w["ffn1_w_up"], w["ffn1_w_down"], ffn1_saved,
                                       "ffn1", functools.partial(push, 2))
    return loss, dx, g


def _split_w_in(w_in_t):
    sl = lambda o, n: w_in_t[o:o + n]
    w_p = jnp.concatenate([sl(IN_GL, 3 * D_MODEL), sl(IN_Z, 2048), sl(IN_XBC, XBC), sl(IN_UV, 2048), sl(IN_Q, MEM_W),
                           jnp.zeros((P_W - P_USED, D_MODEL), w_in_t.dtype)], axis=0)
    w_dt = jnp.pad(sl(IN_DT, SSD_HEADS), ((0, 128 - SSD_HEADS), (0, 0)))
    return w_p, w_dt


def _pick_tile(rows, cap=512):
    best = None
    for tile in range(8, min(rows, cap) + 1, 8):
        if rows % tile == 0:
            best = tile
    return best if best is not None else rows


def _adamw(w, g, m, v, name):
    rows, lanes = w.shape
    tile = _pick_tile(rows, cap=max(8, (512 * 1024 // lanes) // 8 * 8))
    c1 = 1.0 / (1.0 - ADAM_B1 ** ADAM_STEP)
    c2 = 1.0 / (1.0 - ADAM_B2 ** ADAM_STEP)

    def body(w_ref, g_ref, m_ref, v_ref, d_ref, nm_ref, nv_ref):
        gv = g_ref[...]
        nm = ADAM_B1 * m_ref[...] + (1.0 - ADAM_B1) * gv
        nv = ADAM_B2 * v_ref[...] + (1.0 - ADAM_B2) * (gv * gv)
        nm_ref[...] = nm
        nv_ref[...] = nv
        d_ref[...] = -ADAM_LR * ((nm * c1) / (jnp.sqrt(nv * c2) + ADAM_EPS) + ADAM_WD * w_ref[...])

    blk = pl.BlockSpec((tile, lanes), lambda i: (i, 0))
    return pl.pallas_call(
        body, name=name, grid=(rows // tile,), in_specs=[blk] * 4, out_specs=[blk] * 3,
        out_shape=[jax.ShapeDtypeStruct((rows, lanes), F32)] * 3, compiler_params=_cparams("parallel"),
    )(w, g, m, v)


HBM = pl.BlockSpec(memory_space=pltpu.HBM)


def _place():
    x, y, c = lax.axis_index("x"), lax.axis_index("y"), lax.axis_index("c")
    chips = [(1 - x, y), (x, 1 - y), (1 - x, 1 - y)]
    return x, y, c, chips


SEM = pl.BlockSpec(memory_space=pltpu.SEMAPHORE)
EFFECT = pltpu.SideEffectType.DATAFLOW_SIDE_EFFECTING
N_PEER = 3


def _sem_outs():
    return tuple(pltpu.SemaphoreType.DMA(()) for _ in range(2 * N_PEER))


def _gather_start(slots, tag):
    def body(in_ref, *refs):
        del in_ref
        sems, thru, token = refs[:2 * N_PEER], refs[2 * N_PEER], refs[2 * N_PEER + 1]
        x, y, c, chips = _place()
        own = thru.at[2 * x + y, c]
        for j, chip in enumerate(chips):
            pltpu.make_async_remote_copy(src_ref=own, dst_ref=own, send_sem=sems[j], recv_sem=sems[N_PEER + j],
                                         device_id=(*chip, c), device_id_type=MESH).start()
        token[...] = jnp.zeros_like(token)

    out = pl.pallas_call(
        body, name=f"gather_{tag}_start",
        out_shape=_sem_outs() + (pltpu.HBM(slots.shape, slots.dtype), jax.ShapeDtypeStruct((8, 128), F32)),
        in_specs=(HBM,), out_specs=(SEM,) * (2 * N_PEER) + (HBM, pl.BlockSpec(memory_space=pltpu.VMEM)),
        input_output_aliases={0: 2 * N_PEER}, compiler_params=pltpu.CompilerParams(has_side_effects=EFFECT),
    )(pltpu.with_memory_space_constraint(slots, pltpu.HBM))
    return out[:2 * N_PEER], out[2 * N_PEER], out[2 * N_PEER + 1]


def _gather_wait(sems, thru, after, tag):
    def body(in_ref, *refs):
        del in_ref
        sems, out_ref = refs[:2 * N_PEER], refs[2 * N_PEER + 1]
        x, y, c, chips = _place()
        own = out_ref.at[2 * x + y, c]
        for j, (cx, cy) in enumerate(chips):
            cp = pltpu.make_async_remote_copy(src_ref=own, dst_ref=out_ref.at[2 * cx + cy, c], send_sem=sems[j],
                                              recv_sem=sems[N_PEER + j], device_id=(cx, cy, c), device_id_type=MESH)
            cp.wait_send()
            cp.wait_recv()

    return pl.pallas_call(
        body, name=f"gather_{tag}_wait", out_shape=pltpu.HBM(thru.shape, thru.dtype),
        in_specs=(HBM,) + (SEM,) * (2 * N_PEER) + (pl.BlockSpec(memory_space=pl.ANY),), out_specs=HBM,
        input_output_aliases={0: 0}, compiler_params=pltpu.CompilerParams(has_side_effects=EFFECT),
    )(thru, *sems, after)


def _gather_forward(slots, tag):
    def body(in_ref, out_ref, send_sems, recv_sems):
        del in_ref
        x, y, c, chips = _place()
        cps = []
        for j, (cx, cy) in enumerate(chips):
            landed = out_ref.at[2 * cx + cy, c]
            cps.append(pltpu.make_async_remote_copy(src_ref=landed, dst_ref=landed, send_sem=send_sems.at[j], recv_sem=recv_sems.at[j],
                                                    device_id=(x, y, 1 - c), device_id_type=MESH))
        for cp in cps:
            cp.start()
        for j, (cx, cy) in enumerate(chips):
            other = out_ref.at[2 * cx + cy, 1 - c]
            pltpu.make_async_remote_copy(src_ref=other, dst_ref=other, send_sem=send_sems.at[j], recv_sem=recv_sems.at[j],
                                         device_id=(x, y, 1 - c), device_id_type=MESH).wait_recv()
        for cp in cps:
            cp.wait_send()

    return pl.pallas_call(
        body, name=f"gather_{tag}_forward", out_shape=jax.ShapeDtypeStruct(slots.shape, slots.dtype),
        in_specs=[HBM], out_specs=HBM, input_output_aliases={0: 0},
        scratch_shapes=[pltpu.SemaphoreType.DMA((N_PEER,)), pltpu.SemaphoreType.DMA((N_PEER,))],
    )(slots)


def _scatter_start(pa, tag):
    ns, rh, lanes = pa.shape
    land = pltpu.with_memory_space_constraint(lax.empty((N_PEER, rh, lanes), pa.dtype), pltpu.HBM)

    def body(pa_ref, land_ref, *refs):
        x, y, c, chips = _place()
        for j, (cx, cy) in enumerate(chips):
            pltpu.make_async_remote_copy(src_ref=pa_ref.at[2 * cx + cy], dst_ref=land_ref.at[j], send_sem=refs[j],
                                         recv_sem=refs[N_PEER + j], device_id=(cx, cy, c), device_id_type=MESH).start()
        refs[-1][...] = jnp.zeros_like(refs[-1])

    out = pl.pallas_call(
        body, name=f"scatter_start_{tag}",
        out_shape=_sem_outs() + (pltpu.HBM(pa.shape, pa.dtype), pltpu.HBM(land.shape, land.dtype), jax.ShapeDtypeStruct((8, 128), F32)),
        in_specs=(HBM, HBM), out_specs=(SEM,) * (2 * N_PEER) + (HBM, HBM, pl.BlockSpec(memory_space=pltpu.VMEM)),
        input_output_aliases={0: 2 * N_PEER, 1: 2 * N_PEER + 1}, compiler_params=pltpu.CompilerParams(has_side_effects=EFFECT),
    )(pltpu.with_memory_space_constraint(pa, pltpu.HBM), land)
    return (out[:2 * N_PEER], out[2 * N_PEER], out[2 * N_PEER + 1]), out[2 * N_PEER + 2]


def _scatter_wait(sems, pa_thru, land_thru, after, tag):
    def body(pa_ref, land_ref, *refs):
        sems = refs[:2 * N_PEER]
        x, y, c, chips = _place()
        for j, (cx, cy) in enumerate(chips):
            cp = pltpu.make_async_remote_copy(src_ref=pa_ref.at[2 * cx + cy], dst_ref=land_ref.at[j], send_sem=sems[j],
                                              recv_sem=sems[N_PEER + j], device_id=(cx, cy, c), device_id_type=MESH)
            cp.wait_send()
            cp.wait_recv()

    return pl.pallas_call(
        body, name=f"scatter_wait_{tag}",
        out_shape=(pltpu.HBM(pa_thru.shape, pa_thru.dtype), pltpu.HBM(land_thru.shape, land_thru.dtype)),
        in_specs=(HBM, HBM) + (SEM,) * (2 * N_PEER) + (pl.BlockSpec(memory_space=pl.ANY),), out_specs=(HBM, HBM),
        input_output_aliases={0: 0, 1: 1}, compiler_params=pltpu.CompilerParams(has_side_effects=EFFECT),
    )(pa_thru, land_thru, *sems, after)


def _rs_swap(gp, tag):
    ns, _, rh, lanes = gp.shape

    def body(in_ref, out_ref, send_sem, recv_sem):
        x, y, c, _ = _place()
        cp = pltpu.make_async_remote_copy(src_ref=in_ref.at[:, 1 - c], dst_ref=out_ref, send_sem=send_sem, recv_sem=recv_sem,
                                          device_id=(x, y, 1 - c), device_id_type=MESH)
        cp.start()
        cp.wait_send()
        cp.wait_recv()

    return pl.pallas_call(
        body, name=f"rs_swap_{tag}", out_shape=jax.ShapeDtypeStruct((ns, rh, lanes), gp.dtype), in_specs=[HBM], out_specs=HBM,
        scratch_shapes=[pltpu.SemaphoreType.DMA, pltpu.SemaphoreType.DMA],
    )(gp)


def _rs_tile(rh):
    return _pick_tile(rh, cap=512)


def _rs_add(gp, recv, c, tag):
    ns, _, rh, lanes = gp.shape
    tile = _rs_tile(rh)

    def body(c_ref, a_ref, b_ref, o_ref):
        o_ref[...] = (a_ref[...].astype(F32) + b_ref[...].astype(F32)).astype(o_ref.dtype)

    return pl.pallas_call(
        body, name=f"rs_add_{tag}", out_shape=jax.ShapeDtypeStruct((ns, rh, lanes), gp.dtype),
        grid_spec=pltpu.PrefetchScalarGridSpec(
            num_scalar_prefetch=1, grid=(ns, rh // tile),
            in_specs=[pl.BlockSpec((None, None, tile, lanes), lambda s, i, c_ref: (s, c_ref[0], i, 0)),
                      pl.BlockSpec((None, tile, lanes), lambda s, i, c_ref: (s, i, 0))],
            out_specs=pl.BlockSpec((None, tile, lanes), lambda s, i, c_ref: (s, i, 0))),
        compiler_params=_cparams("parallel", "parallel"),
    )(c, gp, recv)


def _rs_sum(pa, recv, place, tag):
    ns, rh, lanes = pa.shape
    tile = _rs_tile(rh)

    def body(place_ref, a_ref, r_ref, o_ref):
        acc = a_ref[...].astype(F32)
        for j in range(ns - 1):
            acc = acc + r_ref[j].astype(F32)
        o_ref[...] = acc

    return pl.pallas_call(
        body, name=f"rs_sum_{tag}", out_shape=jax.ShapeDtypeStruct((2, rh, lanes), F32),
        grid_spec=pltpu.PrefetchScalarGridSpec(
            num_scalar_prefetch=1, grid=(rh // tile,),
            in_specs=[pl.BlockSpec((None, tile, lanes), lambda i, place_ref: (place_ref[0], i, 0)),
                      pl.BlockSpec((ns - 1, tile, lanes), lambda i, place_ref: (0, i, 0))],
            out_specs=pl.BlockSpec((None, tile, lanes), lambda i, place_ref: (place_ref[1], i, 0))),
        compiler_params=_cparams("parallel"),
    )(place, pa, recv)


def _rs_share(halves, tag):
    def body(in_ref, out_ref, send_sem, recv_sem):
        del in_ref
        x, y, c, _ = _place()
        cp = pltpu.make_async_remote_copy(src_ref=out_ref.at[c], dst_ref=out_ref.at[c], send_sem=send_sem, recv_sem=recv_sem,
                                          device_id=(x, y, 1 - c), device_id_type=MESH)
        cp.start()
        other = out_ref.at[1 - c]
        pltpu.make_async_remote_copy(src_ref=other, dst_ref=other, send_sem=send_sem, recv_sem=recv_sem,
                                     device_id=(x, y, 1 - c), device_id_type=MESH).wait_recv()
        cp.wait_send()

    return pl.pallas_call(
        body, name=f"rs_share_{tag}", out_shape=jax.ShapeDtypeStruct(halves.shape, halves.dtype), in_specs=[HBM], out_specs=HBM,
        input_output_aliases={0: 0}, scratch_shapes=[pltpu.SemaphoreType.DMA, pltpu.SemaphoreType.DMA],
    )(halves)


N_DEV = 8
SMALL_ROWS = 160


def _allreduce_small(v):
    m_per, n = v.shape

    def body(x_ref, out_ref, all_ref, send_sems, recv_sems, local_sem):
        x, y, c, chips = _place()
        me, sibling = (x, y, c), (x, y, 1 - c)

        def rows(px, py, pc):
            return all_ref.at[pl.ds((4 * px + 2 * py + pc) * m_per, m_per), :]

        def copy(k, block, to, src=None):
            return pltpu.make_async_remote_copy(src_ref=rows(*block) if src is None else src, dst_ref=rows(*block),
                                                send_sem=send_sems.at[k], recv_sem=recv_sems.at[k], device_id=to, device_id_type=MESH)

        mine = pltpu.make_async_copy(x_ref, rows(*me), local_sem)
        mine.start()
        first = [copy(0, me, sibling, src=x_ref)]
        first += [copy(1 + j, me, (*chip, c), src=x_ref) for j, chip in enumerate(chips)]
        for cp in first:
            cp.start()
        passed = [copy(4 + j, (*chip, c), sibling) for j, chip in enumerate(chips)]
        for j, chip in enumerate(chips):
            copy(1 + j, (*chip, c), me).wait_recv()
            passed[j].start()
        copy(0, sibling, me).wait_recv()
        for j, chip in enumerate(chips):
            copy(4 + j, (*chip, 1 - c), me).wait_recv()
        for cp in first + passed:
            cp.wait_send()
        mine.wait()
        step = 32
        for r in range(0, m_per, step):
            acc = all_ref[r:r + step, :]
            for d in range(1, N_DEV):
                acc = acc + all_ref[d * m_per + r:d * m_per + r + step, :]
            out_ref[r:r + step, :] = acc

    vm = pl.BlockSpec(memory_space=pltpu.VMEM)
    return pl.pallas_call(
        body, name="allreduce_small", out_shape=jax.ShapeDtypeStruct((m_per, n), v.dtype), in_specs=[vm], out_specs=vm,
        scratch_shapes=[pltpu.VMEM((N_DEV * m_per, n), v.dtype), pltpu.SemaphoreType.DMA((7,)), pltpu.SemaphoreType.DMA((7,)),
                        pltpu.SemaphoreType.DMA],
        compiler_params=pltpu.CompilerParams(vmem_limit_bytes=V7X_VMEM_LIMIT),
    )(v)


BIG = {"ffn1_w_gate": ((D_MODEL, D_FF), 1), "ffn1_w_up": ((D_MODEL, D_FF), 1), "ffn1_w_down": ((D_FF, D_MODEL), 0),
       "ffn2_w_gate": ((D_MODEL, D_FF), 1), "ffn2_w_up": ((D_MODEL, D_FF), 1), "ffn2_w_down": ((D_FF, D_MODEL), 0),
       "w_in": ((D_MODEL, IN_WIDTH), 1), "w_mem_kv": ((D_MODEL, 2 * MEM_W), 0), "w_branch_ssd": ((SSD_INNER, D_MODEL), 0),
       "w_branch_gmlp": ((GMLP_W, D_MODEL), 0), "w_branch_mem": ((MEM_W, D_MODEL), 1), "w_out": ((D_MODEL, D_MODEL), 0)}
FFN1 = ("ffn1_w_gate", "ffn1_w_up", "ffn1_w_down")
FFN2 = ("ffn2_w_gate", "ffn2_w_up", "ffn2_w_down")
MIXER = ("w_out", "w_branch_ssd", "w_branch_gmlp", "w_branch_mem", "w_mem_kv", "w_in")
REDUCE_GROUPS = (FFN2, MIXER, FFN1)
CONV_W_ROWS = 8


def _shard_rows_of(name):
    (a, b), _ = BIG[name]
    return a * b // N_SHARD // LANES


def _group_rows(names, extra=0):
    return -(-(sum(_shard_rows_of(n) for n in names) + extra) // 32) * 32

SMALL = [("ffn1_norm", 1), ("mix_norm", 1), ("mem_norm", 1), ("ssd_conv_b", 3), ("heads", 1), ("ssd_norm", 2),
         ("gmlp_v_norm", 1), ("gmlp_w_s", 128), ("gmlp_b_s", 1), ("ffn2_norm", 1), ("final_norm", 1), ("ssd_conv_w", 12)]
assert sum(n for _, n in SMALL) <= SMALL_ROWS
HEAD_VECS = ("ssd_dt_bias", "ssd_a_log", "ssd_d")


def _pack_small(vals, loss=None):
    parts = []
    for name, nrows in SMALL:
        if name == "heads":
            row = jnp.concatenate([vals[k].reshape(-1) for k in HEAD_VECS]
                                  + [jnp.zeros((1,), F32) if loss is None else loss.reshape(1)])
            parts.append(jnp.pad(row, (0, LANES - row.shape[0])).reshape(1, LANES))
        elif name in vals:
            parts.append(vals[name].reshape(nrows, LANES))
        else:
            parts.append(jnp.zeros((nrows, LANES), F32))
    buf = jnp.concatenate(parts, axis=0)
    return jnp.pad(buf, ((0, SMALL_ROWS - buf.shape[0]), (0, 0)))


def _unpack_small(buf):
    out, r = {}, 0
    for name, nrows in SMALL:
        blk = buf[r:r + nrows]
        r += nrows
        if name == "heads":
            for i, k in enumerate(HEAD_VECS):
                out[k] = blk[0, i * SSD_HEADS:(i + 1) * SSD_HEADS]
            out["loss"] = blk[0, 3 * SSD_HEADS]
        else:
            out[name] = blk
    return out


def _wire_shape(name):
    (a, b), axis = BIG[name]
    return (b, a) if axis == 1 else (a, b)


def _pack_weights(given, names, conv=False):
    parts = [(given[n][0].T if BIG[n][1] == 1 else given[n][0]).astype(BF16).reshape(_shard_rows_of(n), LANES) for n in names]
    if conv:
        pairs = lax.bitcast_convert_type(given["ssd_conv_w"], BF16).reshape(-1)
        parts.append(jnp.pad(pairs, (0, CONV_W_ROWS * LANES - pairs.shape[0])).reshape(CONV_W_ROWS, LANES))
    total = _group_rows(names, CONV_W_ROWS if conv else 0)
    packed = jnp.concatenate(parts, axis=0)
    packed = jnp.pad(packed, ((0, total - packed.shape[0]), (0, 0))).reshape(1, 2, total // 2, LANES)
    return jnp.broadcast_to(packed, (N_SHARD, 2, total // 2, LANES))


def _unpack_weights(slots, names, conv=False):
    rows = slots.reshape(N_SHARD, -1, LANES)
    out, r = {}, 0
    for name in names:
        n = _shard_rows_of(name)
        out[name] = rows[:, r:r + n].reshape(_wire_shape(name))
        r += n
    if conv:
        cols = XBC // N_SHARD
        pairs = rows[:, r:r + CONV_W_ROWS].reshape(N_SHARD, -1)[:, :SSD_CONV * cols * 2].reshape(N_SHARD, SSD_CONV, cols, 2)
        out["ssd_conv_w"] = jnp.transpose(lax.bitcast_convert_type(pairs, F32), (1, 0, 2)).reshape(SSD_CONV, XBC)
    return out


def _pack_grads(grads, names):
    total = _group_rows(names)
    parts = [grads[n].astype(BF16).reshape(N_SHARD, _shard_rows_of(n), LANES) for n in names]
    pad = total - sum(p.shape[1] for p in parts)
    if pad:
        parts.append(jnp.zeros((N_SHARD, pad, LANES), BF16))
    return jnp.concatenate(parts, axis=1).reshape(N_SHARD, 2, total // 2, LANES)


def kernel(x, mem, ffn1_norm, ffn1_w_gate, ffn1_w_up, ffn1_w_down, mix_norm, mem_norm, w_in, ssd_conv_w, ssd_conv_b, ssd_dt_bias, ssd_a_log, ssd_d, ssd_norm, gmlp_v_norm, gmlp_w_s, gmlp_b_s, w_mem_kv, w_branch_ssd, w_branch_gmlp, w_branch_mem, w_out, ffn2_norm, ffn2_w_gate, ffn2_w_up, ffn2_w_down, final_norm, loss_target, m_ffn1_norm, m_ffn1_w_gate, m_ffn1_w_up, m_ffn1_w_down, m_mix_norm, m_mem_norm, m_w_in, m_ssd_conv_w, m_ssd_conv_b, m_ssd_dt_bias, m_ssd_a_log, m_ssd_d, m_ssd_norm, m_gmlp_v_norm, m_gmlp_w_s, m_gmlp_b_s, m_w_mem_kv, m_w_branch_ssd, m_w_branch_gmlp, m_w_branch_mem, m_w_out, m_ffn2_norm, m_ffn2_w_gate, m_ffn2_w_up, m_ffn2_w_down, m_final_norm, v_ffn1_norm, v_ffn1_w_gate, v_ffn1_w_up, v_ffn1_w_down, v_mix_norm, v_mem_norm, v_w_in, v_ssd_conv_w, v_ssd_conv_b, v_ssd_dt_bias, v_ssd_a_log, v_ssd_d, v_ssd_norm, v_gmlp_v_norm, v_gmlp_w_s, v_gmlp_b_s, v_w_mem_kv, v_w_branch_ssd, v_w_branch_gmlp, v_w_branch_mem, v_w_out, v_ffn2_norm, v_ffn2_w_gate, v_ffn2_w_up, v_ffn2_w_down, v_final_norm):
    given = dict(x=x, mem=mem, ffn1_norm=ffn1_norm, ffn1_w_gate=ffn1_w_gate, ffn1_w_up=ffn1_w_up, ffn1_w_down=ffn1_w_down, mix_norm=mix_norm, mem_norm=mem_norm, w_in=w_in, ssd_conv_w=ssd_conv_w, ssd_conv_b=ssd_conv_b, ssd_dt_bias=ssd_dt_bias, ssd_a_log=ssd_a_log, ssd_d=ssd_d, ssd_norm=ssd_norm, gmlp_v_norm=gmlp_v_norm, gmlp_w_s=gmlp_w_s, gmlp_b_s=gmlp_b_s, w_mem_kv=w_mem_kv, w_branch_ssd=w_branch_ssd, w_branch_gmlp=w_branch_gmlp, w_branch_mem=w_branch_mem, w_out=w_out, ffn2_norm=ffn2_norm, ffn2_w_gate=ffn2_w_gate, ffn2_w_up=ffn2_w_up, ffn2_w_down=ffn2_w_down, final_norm=final_norm, loss_target=loss_target, m_ffn1_norm=m_ffn1_norm, m_ffn1_w_gate=m_ffn1_w_gate, m_ffn1_w_up=m_ffn1_w_up, m_ffn1_w_down=m_ffn1_w_down, m_mix_norm=m_mix_norm, m_mem_norm=m_mem_norm, m_w_in=m_w_in, m_ssd_conv_w=m_ssd_conv_w, m_ssd_conv_b=m_ssd_conv_b, m_ssd_dt_bias=m_ssd_dt_bias, m_ssd_a_log=m_ssd_a_log, m_ssd_d=m_ssd_d, m_ssd_norm=m_ssd_norm, m_gmlp_v_norm=m_gmlp_v_norm, m_gmlp_w_s=m_gmlp_w_s, m_gmlp_b_s=m_gmlp_b_s, m_w_mem_kv=m_w_mem_kv, m_w_branch_ssd=m_w_branch_ssd, m_w_branch_gmlp=m_w_branch_gmlp, m_w_branch_mem=m_w_branch_mem, m_w_out=m_w_out, m_ffn2_norm=m_ffn2_norm, m_ffn2_w_gate=m_ffn2_w_gate, m_ffn2_w_up=m_ffn2_w_up, m_ffn2_w_down=m_ffn2_w_down, m_final_norm=m_final_norm, v_ffn1_norm=v_ffn1_norm, v_ffn1_w_gate=v_ffn1_w_gate, v_ffn1_w_up=v_ffn1_w_up, v_ffn1_w_down=v_ffn1_w_down, v_mix_norm=v_mix_norm, v_mem_norm=v_mem_norm, v_w_in=v_w_in, v_ssd_conv_w=v_ssd_conv_w, v_ssd_conv_b=v_ssd_conv_b, v_ssd_dt_bias=v_ssd_dt_bias, v_ssd_a_log=v_ssd_a_log, v_ssd_d=v_ssd_d, v_ssd_norm=v_ssd_norm, v_gmlp_v_norm=v_gmlp_v_norm, v_gmlp_w_s=v_gmlp_w_s, v_gmlp_b_s=v_gmlp_b_s, v_w_mem_kv=v_w_mem_kv, v_w_branch_ssd=v_w_branch_ssd, v_w_branch_gmlp=v_w_branch_gmlp, v_w_branch_mem=v_w_branch_mem, v_w_out=v_w_out, v_ffn2_norm=v_ffn2_norm, v_ffn2_w_gate=v_ffn2_w_gate, v_ffn2_w_up=v_ffn2_w_up, v_ffn2_w_down=v_ffn2_w_down, v_final_norm=v_final_norm)
    weights = [n for n in given if n not in ("x", "mem", "loss_target") and not n.startswith(("m_", "v_"))]
    xi, yi, ci = lax.axis_index("x"), lax.axis_index("y"), lax.axis_index("c")
    chip = (2 * xi + yi).astype(jnp.int32)
    core = ci.astype(jnp.int32)
    conv_cols = XBC // N_SHARD

    copies = {"first": _gather_start(_pack_weights(given, FFN1), "first")}
    tied = dict(given)
    tied["w_in"], tied["ffn2_w_gate"], _ = lax.optimization_barrier((given["w_in"], given["ffn2_w_gate"], copies["first"][2]))
    packed = {"mixer": _pack_weights(tied, MIXER, conv=True), "ffn2": _pack_weights(tied, FFN2)}
    w = {}
    for name in ("ffn1_norm", "mix_norm", "mem_norm", "ssd_conv_b", "ssd_norm", "gmlp_v_norm", "ffn2_norm", "final_norm"):
        w[name] = given[name].reshape(1, -1)
    w["ffn1_norm"] = w["ffn1_norm"] + copies["first"][2][0:1, 0:1]
    for name in HEAD_VECS:
        w[name] = given[name].reshape(-1)
    w["gmlp_w_s"] = given["gmlp_w_s"][0]
    w["gmlp_b_s"] = given["gmlp_b_s"][0]

    def arrived(tag, after):
        sems, thru, _ = copies[tag]
        return _gather_forward(_gather_wait(sems, thru, after, tag), tag)

    def then_start(slots, tag, weights, name):
        slots, nxt = lax.optimization_barrier((slots, packed[tag]))
        copies[tag] = _gather_start(nxt, tag)
        unpacked = weights(slots)
        unpacked[name], _ = lax.optimization_barrier((unpacked[name], copies[tag][2]))
        return unpacked

    def first_weights(after):
        after, packed["mixer"], packed["ffn2"] = lax.optimization_barrier((after, packed["mixer"], packed["ffn2"]))
        return then_start(arrived("first", after), "mixer", lambda s: _unpack_weights(s, FFN1), FFN1[0])

    def mixer_weights(after):
        def unpack(slots):
            rest = _unpack_weights(slots, MIXER, conv=True)
            rest["w_in_p"], rest["w_dt"] = _split_w_in(rest.pop("w_in"))
            return rest
        return then_start(arrived("mixer", after), "ffn2", unpack, "w_in_p")

    def second_weights(after):
        return _unpack_weights(arrived("ffn2", after), FFN2)

    pending = {}

    def push(k, group_grads):
        gp = _pack_grads(group_grads, REDUCE_GROUPS[k])
        pa = _rs_add(gp, _rs_swap(gp, k), core.reshape(1), k)
        pending[k], token = _scatter_start(pa, k)
        return token[0:1, 0:1]

    def reduced(k, after):
        pa, land = _scatter_wait(*pending[k], after, k)
        gsum = _rs_share(_rs_sum(pa, land, jnp.stack([chip, core]), k), k)
        rows = gsum.reshape(-1, LANES)
        out, r = {}, 0
        for name in REDUCE_GROUPS[k]:
            n = _shard_rows_of(name)
            a, b = given[name].shape[1:]
            out[name] = rows[r:r + n].reshape(b, a).T if BIG[name][1] == 1 else rows[r:r + n].reshape(a, b)
            r += n
        return out

    loss_part, grad_x, g = _local_step(x[0], mem[0], loss_target[0], w, first_weights, mixer_weights, second_weights, push)

    grads, deltas, new_m, new_v = {}, {}, {}, {}

    def update(k, after):
        for name, gl in reduced(k, after).items():
            d, nm, nv = _adamw(given[name][0], gl, given["m_" + name][0], given["v_" + name][0], f"adamw_{name}")
            grads[name], deltas[name], new_m[name], new_v[name] = (a[None] for a in (gl, d, nm, nv))

    update(0, grad_x)
    update(1, deltas[REDUCE_GROUPS[0][-1]])

    small_vals = {k: g[k] for k, _ in SMALL if k != "heads"}
    small_vals.update({k: g[k] for k in HEAD_VECS})
    red = _unpack_small(_allreduce_small(_pack_small(small_vals, loss=loss_part[0, 0])))
    update(2, deltas[REDUCE_GROUPS[1][-1]])
    conv_g = lax.dynamic_slice_in_dim(red["ssd_conv_w"].reshape(SSD_CONV, XBC), chip * conv_cols, conv_cols, axis=1)
    d, nm, nv = _adamw(given["ssd_conv_w"][0], conv_g, given["m_ssd_conv_w"][0], given["v_ssd_conv_w"][0], "adamw_conv_w")
    grads["ssd_conv_w"], deltas["ssd_conv_w"], new_m["ssd_conv_w"], new_v["ssd_conv_w"] = (a[None] for a in (conv_g, d, nm, nv))
    for k in [k for k, _ in SMALL if k not in ("heads", "ssd_conv_w")] + list(HEAD_VECS):
        shape = given[k].shape
        as2d = lambda a: a.reshape(-1, shape[-1])
        d, nm, nv = _adamw(as2d(given[k]), as2d(red[k]), as2d(given["m_" + k]), as2d(given["v_" + k]), f"adamw_{k}")
        grads[k], deltas[k], new_m[k], new_v[k] = (a.reshape(shape) for a in (red[k], d, nm, nv))

    return (red["loss"], grad_x[None], *[grads[n] for n in weights], *[deltas[n] for n in weights],
            *[new_m[n] for n in weights], *[new_v[n] for n in weights])
```

```python
import functools
import math

import jax
import jax.numpy as jnp
from jax import lax
from jax.experimental import pallas as pl
from jax.experimental.pallas import tpu as pltpu

F32, BF16 = jnp.float32, jnp.bfloat16
HI = lax.Precision.HIGHEST
MESH = pl.DeviceIdType.MESH

D_MODEL = 1024
D_FF = 2816
MEM_LEN = 256
SSD_INNER = 2048
SSD_HEADS = 32
SSD_GROUPS = 4
SSD_STATE = 128
SSD_CONV = 4
CHUNK = 128
XBC = SSD_INNER + 2 * SSD_GROUPS * SSD_STATE
GMLP_W = 1024
GMLP_GROUPS = 8
MEM_W = 256
MEM_HEADS = 4
EPS = 1e-6
IN_WIDTH = 10528
IN_Z, IN_XBC, IN_DT, IN_UV, IN_Q, IN_GL = 0, 2048, 5120, 5152, 7200, 7456
P_GL, P_Z, P_XBC, P_UV, P_Q, P_W = 0, 3072, 5120, 8192, 10240, 10752
P_USED = 10496

ADAM_LR, ADAM_B1, ADAM_B2, ADAM_EPS, ADAM_WD, ADAM_STEP = 0.001, 0.9, 0.999, 1e-08, 0.01, 10

V7X_VMEM_LIMIT = 56 * 1024 * 1024
N_SHARD = 4
LANES = 1024


def _cparams(*sem):
    return pltpu.CompilerParams(dimension_semantics=sem, vmem_limit_bytes=V7X_VMEM_LIMIT)


ANY = pl.BlockSpec(memory_space=pl.ANY)


def _sigmoid(x):
    return 0.5 * jnp.tanh(0.5 * x) + 0.5


def _row_tile(t):
    return min(512, t)


_DIMS = {"nn": (((1,), (0,)), ((), ())), "nt": (((1,), (1,)), ((), ())), "tn": (((0,), (0,)), ((), ()))}


def _matmul(a, b, *, mode, out_dtype, tm, tn, tk, name, scale=1.0, addend=None):
    if mode == "tn":
        k_dim, m_dim = a.shape
    else:
        m_dim, k_dim = a.shape
    n_dim = b.shape[0] if mode == "nt" else b.shape[1]
    tm, tn, tk = min(tm, m_dim), min(tn, n_dim), min(tk, k_dim)
    assert m_dim % tm == 0 and n_dim % tn == 0 and k_dim % tk == 0, (name, a.shape, b.shape, tm, tn, tk)
    ni, nj, nk = m_dim // tm, n_dim // tn, k_dim // tk
    a_spec = pl.BlockSpec((tk, tm), lambda j, i, k: (k, i)) if mode == "tn" else pl.BlockSpec((tm, tk), lambda j, i, k: (i, k))
    b_spec = pl.BlockSpec((tn, tk), lambda j, i, k: (j, k)) if mode == "nt" else pl.BlockSpec((tk, tn), lambda j, i, k: (k, j))
    o_spec = pl.BlockSpec((tm, tn), lambda j, i, k: (i, j))
    dims = _DIMS[mode]
    has_add = addend is not None

    def body(*refs):
        a_ref, b_ref = refs[:2]
        r_ref = refs[2] if has_add else None
        o_ref = refs[2 + has_add]

        def finish(acc):
            r = acc * scale if scale != 1.0 else acc
            if has_add:
                r = r + r_ref[...].astype(F32)
            o_ref[...] = r.astype(o_ref.dtype)

        prod = lax.dot_general(a_ref[...].astype(BF16), b_ref[...].astype(BF16), dims, preferred_element_type=F32)
        if nk == 1:
            finish(prod)
            return
        acc_ref = refs[-1]
        k = pl.program_id(2)

        @pl.when(k == 0)
        def _():
            acc_ref[...] = prod

        @pl.when(k > 0)
        def _():
            acc_ref[...] += prod

        @pl.when(k == nk - 1)
        def _():
            finish(acc_ref[...])

    in_specs = [a_spec, b_spec] + ([o_spec] if has_add else [])
    args = (a, b) + ((addend,) if has_add else ())
    return pl.pallas_call(
        body, name=name, grid=(nj, ni, nk), in_specs=in_specs, out_specs=o_spec,
        out_shape=jax.ShapeDtypeStruct((m_dim, n_dim), out_dtype),
        scratch_shapes=[] if nk == 1 else [pltpu.VMEM((tm, tn), F32)],
        compiler_params=_cparams("parallel", "parallel", "arbitrary"),
    )(*args)


ROW_STRIP = 16


def _strips(tm, fn, init=None, rb=ROW_STRIP):
    def step(i, carry):
        return fn(pl.ds(pl.multiple_of(i * rb, rb), rb), carry)
    return lax.fori_loop(0, tm // rb, step, init, unroll=2)


def _rms_fwd(x, gain, name):
    t, d = x.shape
    tm = _row_tile(t)

    def body(x_ref, g_ref, o_ref):
        xv = x_ref[...]
        r = lax.rsqrt(jnp.mean(xv * xv, axis=-1, keepdims=True) + EPS)
        o_ref[...] = (xv * r * g_ref[...]).astype(o_ref.dtype)

    return pl.pallas_call(
        body, name=name, grid=(t // tm,),
        in_specs=[pl.BlockSpec((tm, d), lambda i: (i, 0)), pl.BlockSpec((1, d), lambda i: (0, 0))],
        out_specs=pl.BlockSpec((tm, d), lambda i: (i, 0)),
        out_shape=jax.ShapeDtypeStruct((t, d), BF16), compiler_params=_cparams("parallel"),
    )(x, gain)


def _rms_bwd(x, gain, dn, dres, name):
    t, d = x.shape
    tm = _row_tile(t)
    has_res = dres is not None

    def body(*refs):
        if has_res:
            x_ref, g_ref, dn_ref, r_ref, dx_ref, dg_ref = refs
        else:
            x_ref, g_ref, dn_ref, dx_ref, dg_ref = refs

        @pl.when(pl.program_id(0) == 0)
        def _():
            dg_ref[...] = jnp.zeros_like(dg_ref)

        xv = x_ref[...]
        r = lax.rsqrt(jnp.mean(xv * xv, axis=-1, keepdims=True) + EPS)
        xh = xv * r
        dnv = dn_ref[...].astype(F32)
        dg_ref[...] += jnp.sum(dnv * xh, axis=0, keepdims=True)
        dxh = dnv * g_ref[...]
        dx = r * (dxh - xh * jnp.mean(dxh * xh, axis=-1, keepdims=True))
        if has_res:
            dx = dx + r_ref[...]
        dx_ref[...] = dx

    row = pl.BlockSpec((tm, d), lambda i: (i, 0))
    vec = pl.BlockSpec((1, d), lambda i: (0, 0))
    in_specs = [row, vec, row] + ([row] if has_res else [])
    args = (x, gain, dn) + ((dres,) if has_res else ())
    return pl.pallas_call(
        body, name=name, grid=(t // tm,), in_specs=in_specs, out_specs=[row, vec],
        out_shape=[jax.ShapeDtypeStruct((t, d), F32), jax.ShapeDtypeStruct((1, d), F32)],
        compiler_params=_cparams("arbitrary"),
    )(*args)


def _loss_head(h, gain, target, name):
    t, d = h.shape
    tm = _row_tile(t)

    def body(h_ref, g_ref, t_ref, dh_ref, dg_ref, l_ref):
        @pl.when(pl.program_id(0) == 0)
        def _():
            dg_ref[...] = jnp.zeros_like(dg_ref)
            l_ref[...] = jnp.zeros_like(l_ref)

        xv = h_ref[...]
        g = g_ref[...]
        r = lax.rsqrt(jnp.mean(xv * xv, axis=-1, keepdims=True) + EPS)
        xh = xv * r
        err = xh * g - t_ref[...]
        l_ref[...] += 0.5 * jnp.sum(jnp.mean(err * err, axis=-1, keepdims=True), axis=0, keepdims=True)
        dy = err * (1.0 / d)
        dg_ref[...] += jnp.sum(dy * xh, axis=0, keepdims=True)
        dxh = dy * g
        dh_ref[...] = r * (dxh - xh * jnp.mean(dxh * xh, axis=-1, keepdims=True))

    row = pl.BlockSpec((tm, d), lambda i: (i, 0))
    vec = pl.BlockSpec((1, d), lambda i: (0, 0))
    return pl.pallas_call(
        body, name=name, grid=(t // tm,), in_specs=[row, vec, row],
        out_specs=[row, vec, pl.BlockSpec((1, 128), lambda i: (0, 0))],
        out_shape=[jax.ShapeDtypeStruct((t, d), F32), jax.ShapeDtypeStruct((1, d), F32), jax.ShapeDtypeStruct((1, 128), F32)],
        compiler_params=_cparams("arbitrary"),
    )(h, gain, target)


FF_TILE = 1408


def _ffn_fwd(n, x, wg, wu, wd, name):
    t, d = x.shape
    tm, tn = _row_tile(t), FF_TILE
    nj = D_FF // tn

    def body(n_ref, x_ref, wg_ref, wu_ref, wd_ref, h_ref, g_ref, u_ref, acc_ref):
        j = pl.program_id(1)

        @pl.when(j == 0)
        def _():
            acc_ref[...] = jnp.zeros_like(acc_ref)

        nb = n_ref[...]
        g = lax.dot_general(nb, wg_ref[...], _DIMS["nt"], preferred_element_type=F32)
        u = lax.dot_general(nb, wu_ref[...], _DIMS["nt"], preferred_element_type=F32)
        g_ref[...] = g.astype(BF16)
        u_ref[...] = u.astype(BF16)
        a = g * _sigmoid(g) * u
        acc_ref[...] += jnp.dot(a.astype(BF16), wd_ref[...], preferred_element_type=F32)

        @pl.when(j == nj - 1)
        def _():
            h_ref[...] = x_ref[...] + 0.5 * acc_ref[...]

    row = pl.BlockSpec((tm, d), lambda i, j: (i, 0))
    act = pl.BlockSpec((tm, tn), lambda i, j: (i, j))
    return pl.pallas_call(
        body, name=name, grid=(t // tm, nj),
        in_specs=[row, row] + [pl.BlockSpec((tn, d), lambda i, j: (j, 0))] * 3,
        out_specs=[row, act, act],
        out_shape=[jax.ShapeDtypeStruct((t, d), F32), jax.ShapeDtypeStruct((t, D_FF), BF16), jax.ShapeDtypeStruct((t, D_FF), BF16)],
        scratch_shapes=[pltpu.VMEM((tm, d), F32)], compiler_params=_cparams("parallel", "arbitrary"),
    )(n, x, wg, wu, wd)


def _ffn_bwd_act(dh, g, u, wg, wu, wd, name):
    t, d = dh.shape
    tm, tn = _row_tile(t), FF_TILE
    nj = D_FF // tn

    def body(dh_ref, g_ref, u_ref, wg_ref, wu_ref, wd_ref, dn_ref, dg_ref, du_ref, a_ref, acc_ref):
        j = pl.program_id(1)

        @pl.when(j == 0)
        def _():
            acc_ref[...] = jnp.zeros_like(acc_ref)

        dhb = (0.5 * dh_ref[...]).astype(BF16)
        da = lax.dot_general(dhb, wd_ref[...], _DIMS["nt"], preferred_element_type=F32)
        gv = g_ref[...].astype(F32)
        uv = u_ref[...].astype(F32)
        sg = _sigmoid(gv)
        s = gv * sg
        dg = (da * uv * (sg * (1.0 + gv * (1.0 - sg)))).astype(BF16)
        du = (da * s).astype(BF16)
        dg_ref[...] = dg
        du_ref[...] = du
        a_ref[...] = (s * uv).astype(BF16)
        acc_ref[...] += (jnp.dot(dg, wg_ref[...], preferred_element_type=F32)
                         + jnp.dot(du, wu_ref[...], preferred_element_type=F32))

        @pl.when(j == nj - 1)
        def _():
            dn_ref[...] = acc_ref[...]

    row = pl.BlockSpec((tm, d), lambda i, j: (i, 0))
    act = pl.BlockSpec((tm, tn), lambda i, j: (i, j))
    return pl.pallas_call(
        body, name=name, grid=(t // tm, nj),
        in_specs=[row, act, act] + [pl.BlockSpec((tn, d), lambda i, j: (j, 0))] * 3,
        out_specs=[row, act, act, act],
        out_shape=[jax.ShapeDtypeStruct((t, d), F32)] + [jax.ShapeDtypeStruct((t, D_FF), BF16)] * 3,
        scratch_shapes=[pltpu.VMEM((tm, d), F32)], compiler_params=_cparams("parallel", "arbitrary"),
    )(dh, g, u, wg, wu, wd)


def _ffn_forward(x, gain, weights, tag):
    n = _rms_fwd(x, gain, f"{tag}_norm")
    h, g, u = _ffn_fwd(n, x, *weights(n), f"{tag}_fwd")
    return h, (n, g, u)


def _ffn_backward(dh, x, gain, wg, wu, wd, saved, tag, push):
    n, g, u = saved
    dn, dg, du, a = _ffn_bwd_act(dh, g, u, wg, wu, wd, f"{tag}_bwd_act")
    kw = dict(mode="tn", out_dtype=BF16, tm=FF_TILE, tn=1024, tk=2048)
    d_wg = _matmul(dg, n, name=f"{tag}_dwg", **kw)
    d_wu = _matmul(du, n, name=f"{tag}_dwu", **kw)
    d_wd = _matmul(a, dh, scale=0.5, name=f"{tag}_dwd", **kw)
    token = push({f"{tag}_w_gate": d_wg, f"{tag}_w_up": d_wu, f"{tag}_w_down": d_wd})
    return _rms_bwd(x, gain + token, dn, dh, f"{tag}_norm_bwd")


CONV_COLS = 512
HALO = 8
CONV_STRIP = 32
CONV_ROWS = 2048


def _conv_fwd(p, w, b, name):
    t = p.shape[0]
    tm = min(CONV_ROWS, t)
    c0 = P_XBC // CONV_COLS

    def body(x_ref, halo_ref, w_ref, b_ref, o_ref, s_ref):
        i = pl.program_id(1)
        s_ref[0:HALO, :] = jnp.where(i > 0, halo_ref[...].astype(F32), 0.0)
        s_ref[HALO:HALO + tm, :] = x_ref[...].astype(F32)
        wv = w_ref[...]
        bv = b_ref[...]
        for r0 in range(0, tm, CONV_STRIP):
            acc = bv + wv[0:1, :] * s_ref[HALO - 3 + r0:HALO - 3 + r0 + CONV_STRIP, :]
            for k in range(1, SSD_CONV):
                acc = acc + wv[k:k + 1, :] * s_ref[HALO - 3 + k + r0:HALO - 3 + k + r0 + CONV_STRIP, :]
            o_ref[r0:r0 + CONV_STRIP, :] = (acc * _sigmoid(acc)).astype(o_ref.dtype)

    return pl.pallas_call(
        body, name=name, grid=(XBC // CONV_COLS, t // tm),
        in_specs=[pl.BlockSpec((tm, CONV_COLS), lambda j, i: (i, c0 + j)),
                  pl.BlockSpec((HALO, CONV_COLS), lambda j, i: (jnp.maximum(i * (tm // HALO) - 1, 0), c0 + j)),
                  pl.BlockSpec((SSD_CONV, CONV_COLS), lambda j, i: (0, j)), pl.BlockSpec((1, CONV_COLS), lambda j, i: (0, j))],
        out_specs=pl.BlockSpec((tm, CONV_COLS), lambda j, i: (i, j)),
        out_shape=jax.ShapeDtypeStruct((t, XBC), BF16),
        scratch_shapes=[pltpu.VMEM((tm + HALO, CONV_COLS), F32)], compiler_params=_cparams("parallel", "parallel"),
    )(p, p, w, b)


def _conv_bwd_act(p, dy, w, b, col0, name):
    t, cols = dy.shape
    tm = min(CONV_ROWS, t)
    c0 = (P_XBC + col0) // CONV_COLS
    w0 = col0 // CONV_COLS

    def body(x_ref, halo_ref, dy_ref, w_ref, b_ref, da_ref, dw_ref, db_ref, s_ref):
        i = pl.program_id(1)

        @pl.when(i == 0)
        def _():
            dw_ref[...] = jnp.zeros_like(dw_ref)
            db_ref[...] = jnp.zeros_like(db_ref)

        s_ref[0:HALO, :] = jnp.where(i > 0, halo_ref[...].astype(F32), 0.0)
        s_ref[HALO:HALO + tm, :] = x_ref[...].astype(F32)
        wv = w_ref[...]
        bv = b_ref[...]
        fold = lambda v: jnp.sum(v.reshape(CONV_STRIP // 8, 8, CONV_COLS), axis=0)
        sums = [jnp.zeros((8, CONV_COLS), F32) for _ in range(SSD_CONV + 1)]
        for r0 in range(0, tm, CONV_STRIP):
            taps = [s_ref[HALO - 3 + k + r0:HALO - 3 + k + r0 + CONV_STRIP, :] for k in range(SSD_CONV)]
            acc = bv + wv[0:1, :] * taps[0]
            for k in range(1, SSD_CONV):
                acc = acc + wv[k:k + 1, :] * taps[k]
            sg = _sigmoid(acc)
            dacc = dy_ref[r0:r0 + CONV_STRIP, :].astype(F32) * (sg * (1.0 + acc * (1.0 - sg)))
            da_ref[r0:r0 + CONV_STRIP, :] = dacc.astype(BF16)
            for k in range(SSD_CONV):
                sums[k] = sums[k] + fold(dacc * taps[k])
            sums[SSD_CONV] = sums[SSD_CONV] + fold(dacc)
        for k in range(SSD_CONV):
            dw_ref[k:k + 1, :] += jnp.sum(sums[k], axis=0, keepdims=True)
        db_ref[...] += jnp.sum(sums[SSD_CONV], axis=0, keepdims=True)

    return pl.pallas_call(
        body, name=name, grid=(cols // CONV_COLS, t // tm),
        in_specs=[pl.BlockSpec((tm, CONV_COLS), lambda j, i: (i, c0 + j)),
                  pl.BlockSpec((HALO, CONV_COLS), lambda j, i: (jnp.maximum(i * (tm // HALO) - 1, 0), c0 + j)),
                  pl.BlockSpec((tm, CONV_COLS), lambda j, i: (i, j)),
                  pl.BlockSpec((SSD_CONV, CONV_COLS), lambda j, i: (0, w0 + j)), pl.BlockSpec((1, CONV_COLS), lambda j, i: (0, w0 + j))],
        out_specs=[pl.BlockSpec((tm, CONV_COLS), lambda j, i: (i, j)), pl.BlockSpec((SSD_CONV, CONV_COLS), lambda j, i: (0, j)),
                   pl.BlockSpec((1, CONV_COLS), lambda j, i: (0, j))],
        out_shape=[jax.ShapeDtypeStruct((t, cols), BF16), jax.ShapeDtypeStruct((SSD_CONV, cols), F32), jax.ShapeDtypeStruct((1, cols), F32)],
        scratch_shapes=[pltpu.VMEM((tm + HALO, CONV_COLS), F32)], compiler_params=_cparams("parallel", "arbitrary"),
    )(p, p, dy, w, b)


def _conv_bwd_dx(dacc, w, col0, dp, name):
    t, cols = dacc.shape
    tm = min(CONV_ROWS, t)
    nt = t // tm
    w0 = col0 // CONV_COLS
    c0 = (P_XBC + col0) // CONV_COLS

    def body(d_ref, halo_ref, w_ref, dp_ref, o_ref, s_ref):
        del dp_ref
        i = pl.program_id(1)
        s_ref[0:tm, :] = d_ref[...].astype(F32)
        s_ref[tm:tm + HALO, :] = jnp.where(i < nt - 1, halo_ref[...].astype(F32), 0.0)
        wv = w_ref[...]
        for r0 in range(0, tm, CONV_STRIP):
            acc = wv[3:4, :] * s_ref[r0:r0 + CONV_STRIP, :]
            for k in range(SSD_CONV - 1):
                acc = acc + wv[k:k + 1, :] * s_ref[3 - k + r0:3 - k + r0 + CONV_STRIP, :]
            o_ref[r0:r0 + CONV_STRIP, :] = acc.astype(o_ref.dtype)

    return pl.pallas_call(
        body, name=name, grid=(cols // CONV_COLS, nt),
        in_specs=[pl.BlockSpec((tm, CONV_COLS), lambda j, i: (i, j)),
                  pl.BlockSpec((HALO, CONV_COLS), lambda j, i: (jnp.minimum((i + 1) * (tm // HALO), t // HALO - 1), j)),
                  pl.BlockSpec((SSD_CONV, CONV_COLS), lambda j, i: (0, w0 + j)), ANY],
        out_specs=pl.BlockSpec((tm, CONV_COLS), lambda j, i: (i, c0 + j)),
        out_shape=jax.ShapeDtypeStruct(dp.shape, dp.dtype), input_output_aliases={3: 0},
        scratch_shapes=[pltpu.VMEM((tm + HALO, CONV_COLS), F32)], compiler_params=_cparams("parallel", "parallel"),
    )(dacc, dacc, w, dp)


GROUP_COLS = SSD_INNER // SSD_GROUPS
PAIRS = GROUP_COLS // 128
HEADS_PER_GROUP = SSD_HEADS // SSD_GROUPS


def _dt_fwd(dt_raw, bias, name):
    t, n = dt_raw.shape
    tm = _row_tile(t)

    def body(x_ref, b_ref, o_ref):
        v = x_ref[...] + b_ref[...]
        o_ref[...] = jnp.maximum(v, 0.0) + jnp.log1p(jnp.exp(-jnp.abs(v)))

    row = pl.BlockSpec((tm, n), lambda i: (i, 0))
    vec = pl.BlockSpec((1, n), lambda i: (0, 0))
    return pl.pallas_call(body, name=name, grid=(t // tm,), in_specs=[row, vec], out_specs=row,
                          out_shape=jax.ShapeDtypeStruct((t, n), F32), compiler_params=_cparams("parallel"))(dt_raw, bias)


def _dt_bwd(ddt, dt_raw, bias, name):
    t, n = dt_raw.shape
    tm = _row_tile(t)

    def body(d_ref, x_ref, b_ref, o_ref, db_ref):
        @pl.when(pl.program_id(0) == 0)
        def _():
            db_ref[...] = jnp.zeros_like(db_ref)

        dr = d_ref[...] * _sigmoid(x_ref[...] + b_ref[...])
        o_ref[...] = dr.astype(o_ref.dtype)
        db_ref[...] += jnp.sum(dr, axis=0, keepdims=True)

    row = pl.BlockSpec((tm, n), lambda i: (i, 0))
    vec = pl.BlockSpec((1, n), lambda i: (0, 0))
    return pl.pallas_call(body, name=name, grid=(t // tm,), in_specs=[row, row, vec], out_specs=[row, vec],
                          out_shape=[jax.ShapeDtypeStruct((t, n), BF16), jax.ShapeDtypeStruct((1, n), F32)],
                          compiler_params=_cparams("arbitrary"))(ddt, dt_raw, bias)


SSD_STEP = 4


def _ssd_common(dt, dtt, a_log_w, a_log_t):
    l = CHUNK
    a = -jnp.exp(a_log_w)
    at = -jnp.exp(a_log_t)
    rowi = lax.broadcasted_iota(jnp.int32, (l, l), 0)
    coli = lax.broadcasted_iota(jnp.int32, (l, l), 1)
    tri = rowi >= coli
    lower = tri.astype(F32)
    upper = (rowi <= coli).astype(F32)
    acs = jnp.dot(lower, dt * a, precision=HI, preferred_element_type=F32)
    acst = jnp.dot(dtt * at, upper, precision=HI, preferred_element_type=F32)
    return a, acs, acst, jnp.where(tri, 0.0, -1e30), upper


def _pair_bc(w, lo, p):
    return jnp.where(lo, w[:, 2 * p:2 * p + 1], w[:, 2 * p + 1:2 * p + 2])


def _ssd_specs(t):
    rows = SSD_STEP * CHUNK
    assert t % rows == 0
    return t // rows, dict(
        xs=lambda cm: pl.BlockSpec((rows, GROUP_COLS), lambda g, c: (cm(c), g)),
        bm=lambda cm: pl.BlockSpec((rows, SSD_STATE), lambda g, c: (cm(c), SSD_INNER // SSD_STATE + g)),
        cmat=lambda cm: pl.BlockSpec((rows, SSD_STATE), lambda g, c: (cm(c), SSD_INNER // SSD_STATE + SSD_GROUPS + g)),
        dtw=lambda cm: pl.BlockSpec((1, rows, 128), lambda g, c: (g, cm(c), 0)),
        dtt=lambda cm: pl.BlockSpec((1, HEADS_PER_GROUP, rows), lambda g, c: (g, 0, cm(c))),
        wide=lambda cm: pl.BlockSpec((1, 1, 128), lambda g, c: (g, 0, 0)),
        tall=lambda cm: pl.BlockSpec((1, HEADS_PER_GROUP, 1), lambda g, c: (g, 0, 0)),
        grp=lambda cm: pl.BlockSpec((rows, GROUP_COLS), lambda g, c: (cm(c), g)),
        zp=lambda cm: pl.BlockSpec((rows, GROUP_COLS), lambda g, c: (cm(c), P_Z // GROUP_COLS + g)),
        vec=lambda cm: pl.BlockSpec((1, GROUP_COLS), lambda g, c: (0, g)),
        state=lambda cm: pl.BlockSpec((1, SSD_STEP, PAIRS, SSD_STATE, 128), lambda g, c: (g, cm(c), 0, 0, 0)),
    )


def _ssd_fwd(xc, p, hv, norm_g, name):
    t = xc.shape[0]
    nc, sp = _ssd_specs(t)
    ident = lambda c: c

    def body(xs_ref, b_ref, c_ref, dtw_ref, dtt_ref, aw_ref, at_ref, dk_ref, z_ref, ng_ref,
             y_ref, ys_ref, h_ref, st_ref):
        @pl.when(pl.program_id(1) == 0)
        def _():
            st_ref[...] = jnp.zeros_like(st_ref)

        lo = lax.broadcasted_iota(jnp.int32, (1, 128), 1) < 64
        dskip = dk_ref[0]
        for s in range(SSD_STEP):
            rows = slice(s * CHUNK, (s + 1) * CHUNK)
            dt = dtw_ref[0, rows, :]
            a, acs, acst, causal, _ = _ssd_common(dt, dtt_ref[0, :, rows], aw_ref[0], at_ref[0])
            ecs = jnp.exp(acs)
            alast = acs[CHUNK - 1:CHUNK, :]
            bmat, cmat = b_ref[rows, :], c_ref[rows, :]
            cb = lax.dot_general(cmat, bmat, _DIMS["nt"], preferred_element_type=F32)
            for pi in range(PAIRS):
                cols = slice(pi * 128, (pi + 1) * 128)
                x = xs_ref[rows, cols].astype(F32)
                xdt = x * _pair_bc(dt, lo, pi)
                xdtb = xdt.astype(BF16)
                ydiag = jnp.zeros((CHUNK, 128), F32)
                for r, mask in ((2 * pi, lo), (2 * pi + 1, jnp.logical_not(lo))):
                    lam = jnp.exp(acs[:, r:r + 1] - acst[r:r + 1, :] + causal)
                    m = (cb * lam).astype(BF16)
                    ydiag = ydiag + jnp.dot(m, jnp.where(mask, xdtb, 0), preferred_element_type=F32)
                ht = st_ref[pi]
                h_ref[0, s, pi] = ht
                yoff = jnp.dot(cmat, ht.astype(BF16), preferred_element_type=F32) * _pair_bc(ecs, lo, pi)
                y_ref[rows, cols] = (ydiag + yoff + _pair_bc(dskip, lo, pi) * x).astype(y_ref.dtype)
                alp = _pair_bc(alast, lo, pi)
                e = jnp.exp(alp - _pair_bc(acs, lo, pi))
                st = lax.dot_general(bmat, (xdt * e).astype(BF16), _DIMS["tn"], preferred_element_type=F32)
                st_ref[pi] = ht * jnp.exp(alp) + st
            zf = z_ref[rows, :].astype(F32)
            yg = y_ref[rows, :].astype(F32) * (zf * _sigmoid(zf))
            rstd = lax.rsqrt(jnp.mean(yg * yg, axis=-1, keepdims=True) + EPS)
            ys_ref[rows, :] = (yg * rstd * ng_ref[...]).astype(ys_ref.dtype)

    ins = ["xs", "bm", "cmat", "dtw", "dtt", "wide", "tall", "wide", "zp", "vec"]
    return pl.pallas_call(
        body, name=name, grid=(SSD_GROUPS, nc),
        in_specs=[sp[k](ident) for k in ins],
        out_specs=[sp["grp"](ident), sp["grp"](ident), sp["state"](ident)],
        out_shape=[jax.ShapeDtypeStruct((t, SSD_INNER), BF16), jax.ShapeDtypeStruct((t, SSD_INNER), BF16),
                   jax.ShapeDtypeStruct((SSD_GROUPS, t // CHUNK, PAIRS, SSD_STATE, 128), F32)],
        scratch_shapes=[pltpu.VMEM((PAIRS, SSD_STATE, 128), F32)], compiler_params=_cparams("parallel", "arbitrary"),
    )(xc, xc, xc, hv["dtw"], hv["dtt"], hv["alog_w"], hv["alog_t"], hv["dskip_w"], p, norm_g)


def _ssd_bwd(xc, p, hv, norm_g, y, dys, states, dp, name):
    t = xc.shape[0]
    nc, sp = _ssd_specs(t)
    rev = lambda c: nc - 1 - c

    def body(xs_ref, b_ref, c_ref, dtw_ref, dtt_ref, aw_ref, at_ref, dk_ref, z_ref, ng_ref,
             y_ref, dys_ref, h_ref, dp_ref,
             dxs_ref, db_ref, dc_ref, dz_ref, ddt_ref, hsum_ref, dng_ref, dst_ref):
        del dp_ref
        @pl.when(pl.program_id(1) == 0)
        def _():
            dst_ref[...] = jnp.zeros_like(dst_ref)
            hsum_ref[...] = jnp.zeros_like(hsum_ref)
            dng_ref[...] = jnp.zeros_like(dng_ref)

        lane = lax.broadcasted_iota(jnp.int32, (1, 128), 1)
        lo = lane < 64
        dskip = dk_ref[0]
        sel_r = lax.broadcasted_iota(jnp.int32, (128, 128), 0)
        sel_c = lax.broadcasted_iota(jnp.int32, (128, 128), 1)
        refs = (xs_ref, b_ref, c_ref, dtw_ref, dtt_ref, aw_ref, at_ref, dk_ref, z_ref, ng_ref, y_ref, dys_ref, h_ref,
                dxs_ref, db_ref, dc_ref, dz_ref, ddt_ref, hsum_ref, dng_ref, dst_ref)
        for s in reversed(range(SSD_STEP)):
            chunk_bwd(refs, slice(s * CHUNK, (s + 1) * CHUNK), s, lane, lo, dskip, sel_r, sel_c)

    def chunk_bwd(refs, rows, s, lane, lo, dskip, sel_r, sel_c):
        (xs_ref, b_ref, c_ref, dtw_ref, dtt_ref, aw_ref, at_ref, dk_ref, z_ref, ng_ref, y_ref, dys_ref, h_ref,
         dxs_ref, db_ref, dc_ref, dz_ref, ddt_ref, hsum_ref, dng_ref, dst_ref) = refs
        dt = dtw_ref[0, rows, :]
        a, acs, acst, causal, upper = _ssd_common(dt, dtt_ref[0, :, rows], aw_ref[0], at_ref[0])
        ecs = jnp.exp(acs)
        alast = acs[CHUNK - 1:CHUNK, :]
        bmat, cmat = b_ref[rows, :], c_ref[rows, :]
        cb = lax.dot_general(cmat, bmat, _DIMS["nt"], preferred_element_type=F32)

        zf = z_ref[rows, :].astype(F32)
        sg = _sigmoid(zf)
        sz = zf * sg
        yv = y_ref[rows, :].astype(F32)
        yg = yv * sz
        rstd = lax.rsqrt(jnp.mean(yg * yg, axis=-1, keepdims=True) + EPS)
        yhat = yg * rstd
        dysv = dys_ref[rows, :].astype(F32)
        dng_ref[...] += jnp.sum(dysv * yhat, axis=0, keepdims=True)
        dyh = dysv * ng_ref[...]
        dyg = rstd * (dyh - yhat * jnp.mean(dyh * yhat, axis=-1, keepdims=True))
        dz_ref[rows, :] = (dyg * yv * (sg * (1.0 + zf * (1.0 - sg)))).astype(dz_ref.dtype)
        dy_all = dyg * sz

        dal = jnp.zeros((CHUNK, 128), F32)
        ddtm = jnp.zeros((CHUNK, 128), F32)
        dalast = jnp.zeros((8, 128), F32)
        ddsk = jnp.zeros((8, 128), F32)
        dcb = jnp.zeros((CHUNK, CHUNK), F32)
        qcol = jnp.zeros((8, CHUNK), F32)
        sub8 = lax.broadcasted_iota(jnp.int32, (8, CHUNK), 0)
        dc_acc = jnp.zeros((CHUNK, SSD_STATE), F32)
        db_acc = jnp.zeros((CHUNK, SSD_STATE), F32)
        for pi in range(PAIRS):
            sel = (sel_c == 2 * pi + (sel_r >= 64).astype(jnp.int32)).astype(BF16)

            def hsum(v, sel=sel):
                return jnp.dot(v.astype(BF16), sel, preferred_element_type=F32)

            dyp = dy_all[:, pi * 128:(pi + 1) * 128]
            x = xs_ref[rows, pi * 128:(pi + 1) * 128].astype(F32)
            dtp = _pair_bc(dt, lo, pi)
            xdt = x * dtp
            dxdt = jnp.zeros((CHUNK, 128), F32)
            dypb, xdtb = dyp.astype(BF16), xdt.astype(BF16)
            for r, mask in ((2 * pi, lo), (2 * pi + 1, jnp.logical_not(lo))):
                lam = jnp.exp(acs[:, r:r + 1] - acst[r:r + 1, :] + causal)
                m32 = cb * lam
                m = m32.astype(BF16)
                dyr = jnp.where(mask, dypb, 0)
                xr = jnp.where(mask, xdtb, 0)
                dm = lax.dot_general(dyr, xr, _DIMS["nt"], preferred_element_type=F32)
                dcb = dcb + dm * lam
                q = dm * m32
                dal = dal + jnp.sum(q, axis=1, keepdims=True) * (lane == r).astype(F32)
                qcol = qcol + jnp.where(sub8 == r, jnp.sum(q, axis=0, keepdims=True), 0.0)
                dxdt = dxdt + lax.dot_general(m, dyr, _DIMS["tn"], preferred_element_type=F32)
            ht = h_ref[0, s, pi]
            htb = ht.astype(BF16)
            ecp = _pair_bc(ecs, lo, pi)
            yoff = jnp.dot(cmat, htb, preferred_element_type=F32) * ecp
            dg = (dyp * ecp).astype(BF16)
            dc_acc = dc_acc + lax.dot_general(dg, htb, _DIMS["nt"], preferred_element_type=F32)
            dht = lax.dot_general(cmat, dg, _DIMS["tn"], preferred_element_type=F32)
            dal = dal + hsum(dyp * yoff)
            dhn = dst_ref[pi]
            dhnb = dhn.astype(BF16)
            alp = _pair_bc(alast, lo, pi)
            e = jnp.exp(alp - _pair_bc(acs, lo, pi))
            xe = xdt * e
            db_acc = db_acc + lax.dot_general(xe.astype(BF16), dhnb, _DIMS["nt"], preferred_element_type=F32)
            dxe = jnp.dot(bmat, dhnb, preferred_element_type=F32)
            dxdt = dxdt + dxe * e
            tt = hsum(dxe * xe)
            dal = dal - tt
            dec = jnp.exp(alp)
            dalast = dalast + jnp.sum(tt, axis=0, keepdims=True) + hsum(
                jnp.broadcast_to(jnp.sum(dhn * ht, axis=0, keepdims=True) * dec, (8, 128)))
            dst_ref[pi] = dht + dhn * dec
            dxs_ref[rows, pi * 128:(pi + 1) * 128] = (_pair_bc(dskip, lo, pi) * dyp + dxdt * dtp).astype(dxs_ref.dtype)
            ddtm = ddtm + hsum(dxdt * x)
            ddsk = ddsk + hsum(jnp.broadcast_to(jnp.sum(dyp * x, axis=0, keepdims=True), (8, 128)))
        rowi = lax.broadcasted_iota(jnp.int32, (CHUNK, 128), 0)
        qcol_w = lax.dot_general(jnp.concatenate([qcol, jnp.zeros((CHUNK - 8, CHUNK), F32)], axis=0), (sel_r == sel_c).astype(F32),
                                 _DIMS["tn"], precision=HI, preferred_element_type=F32)
        dal = dal - qcol_w + jnp.where(rowi == CHUNK - 1, dalast[0:1, :], 0.0)
        dda = jnp.dot(upper, dal, precision=HI, preferred_element_type=F32)
        ddt_ref[0, rows, :] = ddtm + dda * a
        hsum_ref[0, 1:2, :] += jnp.sum(dda * dt, axis=0, keepdims=True) * a
        hsum_ref[0, 2:3, :] += ddsk[0:1, :]
        dcbb = dcb.astype(BF16)
        dc_ref[rows, :] = (jnp.dot(dcbb, bmat, preferred_element_type=F32) + dc_acc).astype(dc_ref.dtype)
        db_ref[rows, :] = (lax.dot_general(dcbb, cmat, _DIMS["tn"], preferred_element_type=F32) + db_acc).astype(db_ref.dtype)

    ins = ["xs", "bm", "cmat", "dtw", "dtt", "wide", "tall", "wide", "zp", "vec", "grp", "grp", "state"]
    col = lambda: pl.BlockSpec((SSD_STEP * CHUNK, SSD_STATE), lambda g, c: (rev(c), g))
    return pl.pallas_call(
        body, name=name, grid=(SSD_GROUPS, nc),
        in_specs=[sp[k](rev) for k in ins] + [ANY],
        out_specs=[sp["grp"](rev), col(), col(), sp["zp"](rev), sp["dtw"](rev),
                   pl.BlockSpec((1, 8, 128), lambda g, c: (g, 0, 0)), sp["vec"](rev)],
        out_shape=[jax.ShapeDtypeStruct((t, SSD_INNER), BF16), jax.ShapeDtypeStruct((t, SSD_GROUPS * SSD_STATE), BF16),
                   jax.ShapeDtypeStruct((t, SSD_GROUPS * SSD_STATE), BF16), jax.ShapeDtypeStruct(dp.shape, dp.dtype),
                   jax.ShapeDtypeStruct((SSD_GROUPS, t, 128), F32), jax.ShapeDtypeStruct((SSD_GROUPS, 8, 128), F32),
                   jax.ShapeDtypeStruct((1, SSD_INNER), F32)],
        input_output_aliases={len(ins): 3},
        scratch_shapes=[pltpu.VMEM((PAIRS, SSD_STATE, 128), F32)], compiler_params=_cparams("parallel", "arbitrary"),
    )(xc, xc, xc, hv["dtw"], hv["dtt"], hv["alog_w"], hv["alog_t"], hv["dskip_w"], p, norm_g, y, dys, states, dp)


def _wide(v):
    return jnp.pad(v.reshape(SSD_GROUPS, 1, HEADS_PER_GROUP), ((0, 0), (0, 0), (0, 128 - HEADS_PER_GROUP)))


def _head_inputs(dt, a_log, d_skip):
    t = dt.shape[0]
    g = dt[:, :SSD_HEADS].reshape(t, SSD_GROUPS, HEADS_PER_GROUP)
    return dict(
        dtw=jnp.pad(jnp.transpose(g, (1, 0, 2)), ((0, 0), (0, 0), (0, 128 - HEADS_PER_GROUP))),
        dtt=jnp.transpose(g, (1, 2, 0)),
        alog_w=_wide(a_log), alog_t=a_log.reshape(SSD_GROUPS, HEADS_PER_GROUP, 1),
        dskip_w=_wide(d_skip),
    )


def _gelu(x):
    return 0.5 * x * (1.0 + lax.erf(x * (1.0 / math.sqrt(2.0))))


def _gelu_grad(x):
    return 0.5 * (1.0 + lax.erf(x * (1.0 / math.sqrt(2.0)))) + x * jnp.exp(-0.5 * x * x) * (1.0 / math.sqrt(2.0 * math.pi))


def _tril_mask():
    r = lax.broadcasted_iota(jnp.int32, (CHUNK, CHUNK), 0)
    c = lax.broadcasted_iota(jnp.int32, (CHUNK, CHUNK), 1)
    return r >= c


def _gmlp_fwd(p, v_gain, w_s, b_col, name):
    t = p.shape[0]
    tm = _row_tile(t)
    u0 = P_UV // GMLP_W

    def body(u_ref, v_ref, gn_ref, ws_ref, bs_ref, o_ref):
        v = _gelu(v_ref[...].astype(F32))
        v = (v * lax.rsqrt(jnp.mean(v * v, axis=-1, keepdims=True) + EPS) * gn_ref[...]).astype(BF16)
        tril = _tril_mask()
        wm = [jnp.where(tril, ws_ref[g], 0.0).astype(BF16) for g in range(GMLP_GROUPS)]
        for k in range(tm // CHUNK):
            rows = slice(k * CHUNK, (k + 1) * CHUNK)
            for g in range(GMLP_GROUPS):
                cols = slice(g * 128, (g + 1) * 128)
                mixed = jnp.dot(wm[g], v[rows, cols], preferred_element_type=F32) + bs_ref[g]
                o_ref[rows, cols] = (_gelu(u_ref[rows, cols].astype(F32)) * mixed).astype(o_ref.dtype)

    return pl.pallas_call(
        body, name=name, grid=(t // tm,),
        in_specs=[pl.BlockSpec((tm, GMLP_W), lambda i: (i, u0)), pl.BlockSpec((tm, GMLP_W), lambda i: (i, u0 + 1)),
                  pl.BlockSpec((1, GMLP_W), lambda i: (0, 0)), pl.BlockSpec((GMLP_GROUPS, CHUNK, CHUNK), lambda i: (0, 0, 0)),
                  pl.BlockSpec((GMLP_GROUPS, CHUNK, 1), lambda i: (0, 0, 0))],
        out_specs=pl.BlockSpec((tm, GMLP_W), lambda i: (i, 0)),
        out_shape=jax.ShapeDtypeStruct((t, GMLP_W), BF16), compiler_params=_cparams("parallel"),
    )(p, p, v_gain, w_s, b_col)


def _gmlp_bwd(p, dy, v_gain, w_s, b_col, dp, name):
    t = p.shape[0]
    tm = _row_tile(t)
    u0 = P_UV // GMLP_W

    def body(u_ref, v_ref, dy_ref, gn_ref, ws_ref, bs_ref, dp_ref, duv_ref, dws_ref, dbs_ref, dgn_ref, dvn_ref):
        del dp_ref

        @pl.when(pl.program_id(0) == 0)
        def _():
            dws_ref[...] = jnp.zeros_like(dws_ref)
            dbs_ref[...] = jnp.zeros_like(dbs_ref)
            dgn_ref[...] = jnp.zeros_like(dgn_ref)

        vraw = v_ref[...].astype(F32)
        va = _gelu(vraw)
        rstd = lax.rsqrt(jnp.mean(va * va, axis=-1, keepdims=True) + EPS)
        vhat = va * rstd
        gain = gn_ref[...]
        vn = (vhat * gain).astype(BF16)
        tril = _tril_mask()
        wm = [jnp.where(tril, ws_ref[g], 0.0).astype(BF16) for g in range(GMLP_GROUPS)]
        for k in range(tm // CHUNK):
            rows = slice(k * CHUNK, (k + 1) * CHUNK)
            for g in range(GMLP_GROUPS):
                cols = slice(g * 128, (g + 1) * 128)
                uraw = u_ref[rows, cols].astype(F32)
                vb = vn[rows, cols]
                mixed = jnp.dot(wm[g], vb, preferred_element_type=F32) + bs_ref[g]
                dyb = dy_ref[rows, cols].astype(F32)
                duv_ref[rows, cols] = (dyb * mixed * _gelu_grad(uraw)).astype(duv_ref.dtype)
                dmix = dyb * _gelu(uraw)
                dmb = dmix.astype(BF16)
                dws_ref[g] += jnp.where(tril, lax.dot_general(dmb, vb, _DIMS["nt"], preferred_element_type=F32), 0.0)
                dbs_ref[g] += jnp.sum(dmix, axis=1, keepdims=True)
                dvn_ref[rows, cols] = lax.dot_general(wm[g], dmb, _DIMS["tn"], preferred_element_type=F32)
        dvn = dvn_ref[...]
        dgn_ref[...] += jnp.sum(dvn * vhat, axis=0, keepdims=True)
        dvh = dvn * gain
        dva = rstd * (dvh - vhat * jnp.mean(dvh * vhat, axis=-1, keepdims=True))
        duv_ref[:, GMLP_W:2 * GMLP_W] = (dva * _gelu_grad(vraw)).astype(duv_ref.dtype)

    return pl.pallas_call(
        body, name=name, grid=(t // tm,),
        in_specs=[pl.BlockSpec((tm, GMLP_W), lambda i: (i, u0)), pl.BlockSpec((tm, GMLP_W), lambda i: (i, u0 + 1)),
                  pl.BlockSpec((tm, GMLP_W), lambda i: (i, 0)),
                  pl.BlockSpec((1, GMLP_W), lambda i: (0, 0)), pl.BlockSpec((GMLP_GROUPS, CHUNK, CHUNK), lambda i: (0, 0, 0)),
                  pl.BlockSpec((GMLP_GROUPS, CHUNK, 1), lambda i: (0, 0, 0)), ANY],
        out_specs=[pl.BlockSpec((tm, 2 * GMLP_W), lambda i: (i, P_UV // (2 * GMLP_W))),
                   pl.BlockSpec((GMLP_GROUPS, CHUNK, CHUNK), lambda i: (0, 0, 0)),
                   pl.BlockSpec((GMLP_GROUPS, CHUNK, 1), lambda i: (0, 0, 0)), pl.BlockSpec((1, GMLP_W), lambda i: (0, 0))],
        out_shape=[jax.ShapeDtypeStruct(dp.shape, dp.dtype), jax.ShapeDtypeStruct((GMLP_GROUPS, CHUNK, CHUNK), F32),
                   jax.ShapeDtypeStruct((GMLP_GROUPS, CHUNK, 1), F32), jax.ShapeDtypeStruct((1, GMLP_W), F32)],
        input_output_aliases={6: 0},
        scratch_shapes=[pltpu.VMEM((tm, GMLP_W), F32)], compiler_params=_cparams("arbitrary"),
    )(p, p, dy, v_gain, w_s, b_col, dp)


def _head_masks():
    lane = lax.broadcasted_iota(jnp.int32, (1, MEM_W), 1)
    return [(lane >= h * 64) & (lane < (h + 1) * 64) for h in range(MEM_HEADS)]


def _mem_fwd(p, kv, name):
    t = p.shape[0]
    tm = _row_tile(t)
    q0 = P_Q // MEM_W

    def body(q_ref, kv_ref, o_ref):
        q = q_ref[...]
        k = kv_ref[:, 0:MEM_W].astype(BF16)
        v = kv_ref[:, MEM_W:2 * MEM_W].astype(BF16)
        out = jnp.zeros((tm, MEM_W), F32)
        for mask in _head_masks():
            s = lax.dot_general(jnp.where(mask, q, 0), k, _DIMS["nt"], preferred_element_type=F32) * 0.125
            e = jnp.exp(s - jnp.max(s, axis=-1, keepdims=True))
            pr = (e * (1.0 / jnp.sum(e, axis=-1, keepdims=True))).astype(BF16)
            out = out + jnp.where(mask, jnp.dot(pr, v, preferred_element_type=F32), 0.0)
        o_ref[...] = out.astype(o_ref.dtype)

    return pl.pallas_call(
        body, name=name, grid=(t // tm,),
        in_specs=[pl.BlockSpec((tm, MEM_W), lambda i: (i, q0)), pl.BlockSpec((MEM_LEN, 2 * MEM_W), lambda i: (0, 0))],
        out_specs=pl.BlockSpec((tm, MEM_W), lambda i: (i, 0)),
        out_shape=jax.ShapeDtypeStruct((t, MEM_W), BF16), compiler_params=_cparams("parallel"),
    )(p, kv)


def _mem_bwd(p, kv, dy, dp, name):
    t = p.shape[0]
    tm = _row_tile(t)
    q0 = P_Q // MEM_W
    assert P_W - P_Q == 2 * MEM_W

    def body(q_ref, kv_ref, dy_ref, dp_ref, dq_ref, dkv_ref):
        del dp_ref

        @pl.when(pl.program_id(0) == 0)
        def _():
            dkv_ref[...] = jnp.zeros_like(dkv_ref)

        q = q_ref[...]
        dy = dy_ref[...]
        k = kv_ref[:, 0:MEM_W].astype(BF16)
        v = kv_ref[:, MEM_W:2 * MEM_W].astype(BF16)
        dq = jnp.zeros((tm, MEM_W), F32)
        dk = jnp.zeros((MEM_LEN, MEM_W), F32)
        dv = jnp.zeros((MEM_LEN, MEM_W), F32)
        for mask in _head_masks():
            qh = jnp.where(mask, q, 0)
            dyh = jnp.where(mask, dy, 0)
            s = lax.dot_general(qh, k, _DIMS["nt"], preferred_element_type=F32) * 0.125
            e = jnp.exp(s - jnp.max(s, axis=-1, keepdims=True))
            pr = e * (1.0 / jnp.sum(e, axis=-1, keepdims=True))
            prb = pr.astype(BF16)
            dp = lax.dot_general(dyh, v, _DIMS["nt"], preferred_element_type=F32)
            ds = (pr * (dp - jnp.sum(dp * pr, axis=-1, keepdims=True)) * 0.125).astype(BF16)
            dq = dq + jnp.where(mask, jnp.dot(ds, k, preferred_element_type=F32), 0.0)
            dk = dk + lax.dot_general(ds, qh, _DIMS["tn"], preferred_element_type=F32)
            dv = dv + lax.dot_general(prb, dyh, _DIMS["tn"], preferred_element_type=F32)
        dq_ref[:, 0:MEM_W] = dq.astype(dq_ref.dtype)
        dq_ref[:, MEM_W:2 * MEM_W] = jnp.zeros((tm, MEM_W), dq_ref.dtype)
        dkv_ref[:, 0:MEM_W] += dk
        dkv_ref[:, MEM_W:2 * MEM_W] += dv

    return pl.pallas_call(
        body, name=name, grid=(t // tm,),
        in_specs=[pl.BlockSpec((tm, MEM_W), lambda i: (i, q0)), pl.BlockSpec((MEM_LEN, 2 * MEM_W), lambda i: (0, 0)),
                  pl.BlockSpec((tm, MEM_W), lambda i: (i, 0)), ANY],
        out_specs=[pl.BlockSpec((tm, 2 * MEM_W), lambda i: (i, P_Q // (2 * MEM_W))),
                   pl.BlockSpec((MEM_LEN, 2 * MEM_W), lambda i: (0, 0))],
        out_shape=[jax.ShapeDtypeStruct(dp.shape, dp.dtype), jax.ShapeDtypeStruct((MEM_LEN, 2 * MEM_W), F32)],
        input_output_aliases={3: 0}, compiler_params=_cparams("arbitrary"),
    )(p, kv, dy, dp)


def _merge_fwd(p, b_ssd, b_gmlp, b_mem, name):
    t = p.shape[0]
    tm = _row_tile(t)
    g0 = P_GL // D_MODEL

    def body(g1, g2, g3, b1, b2, b3, o_ref):
        def strip(rows, carry):
            acc = _sigmoid(g1[rows, :].astype(F32)) * b1[rows, :].astype(F32)
            acc = acc + _sigmoid(g2[rows, :].astype(F32)) * b2[rows, :].astype(F32)
            acc = acc + _sigmoid(g3[rows, :].astype(F32)) * b3[rows, :].astype(F32)
            o_ref[rows, :] = acc.astype(o_ref.dtype)
            return carry

        _strips(tm, strip, 0)

    row = pl.BlockSpec((tm, D_MODEL), lambda i: (i, 0))
    return pl.pallas_call(
        body, name=name, grid=(t // tm,),
        in_specs=[pl.BlockSpec((tm, D_MODEL), lambda i, k=k: (i, g0 + k)) for k in range(3)] + [row] * 3,
        out_specs=row, out_shape=jax.ShapeDtypeStruct((t, D_MODEL), BF16), compiler_params=_cparams("parallel"),
    )(p, p, p, b_ssd, b_gmlp, b_mem)


def _merge_bwd(p, dm, b_ssd, b_gmlp, b_mem, dp, name):
    t = p.shape[0]
    tm = _row_tile(t)
    g0 = P_GL // D_MODEL

    def body(g1, g2, g3, dm_ref, b1, b2, b3, dp_ref, d1, d2, d3, dgl_ref):
        del dp_ref

        def strip(rows, carry):
            dmv = dm_ref[rows, :].astype(F32)
            for k, (g_ref, b_ref, d_ref) in enumerate(((g1, b1, d1), (g2, b2, d2), (g3, b3, d3))):
                sg = _sigmoid(g_ref[rows, :].astype(F32))
                d_ref[rows, :] = (dmv * sg).astype(d_ref.dtype)
                dgl_ref[rows, k * D_MODEL:(k + 1) * D_MODEL] = (dmv * b_ref[rows, :].astype(F32) * sg * (1.0 - sg)).astype(dgl_ref.dtype)
            return carry

        _strips(tm, strip, 0)

    row = pl.BlockSpec((tm, D_MODEL), lambda i: (i, 0))
    return pl.pallas_call(
        body, name=name, grid=(t // tm,),
        in_specs=[pl.BlockSpec((tm, D_MODEL), lambda i, k=k: (i, g0 + k)) for k in range(3)] + [row] * 4 + [ANY],
        out_specs=[row, row, row, pl.BlockSpec((tm, 3 * D_MODEL), lambda i: (i, P_GL // (3 * D_MODEL)))],
        out_shape=[jax.ShapeDtypeStruct((t, D_MODEL), BF16)] * 3 + [jax.ShapeDtypeStruct(dp.shape, dp.dtype)],
        input_output_aliases={7: 3}, compiler_params=_cparams("parallel"),
    )(p, p, p, dm, b_ssd, b_gmlp, b_mem, dp)


def _local_step(x, mem, target, w, first_weights, mixer_weights, second_weights, push):
    t = x.shape[0]
    mm = functools.partial(_matmul, tk=1024)

    ffn_weights = lambda tag: (w[f"{tag}_w_gate"], w[f"{tag}_w_up"], w[f"{tag}_w_down"])

    def arriving(tag, fetch):
        def weights(n):
            w.update(fetch(n))
            return ffn_weights(tag)
        return weights

    w = dict(w)
    h1, ffn1_saved = _ffn_forward(x, w["ffn1_norm"], arriving("ffn1", first_weights), "ffn1")
    w.update(mixer_weights(h1))
    n2 = _rms_fwd(h1, w["mix_norm"], "mix_norm")
    p = mm(n2, w["w_in_p"], mode="nt", out_dtype=BF16, tm=2048, tn=1536, name="in_proj")
    dt_raw = mm(n2, w["w_dt"], mode="nt", out_dtype=F32, tm=2048, tn=128, name="dt_proj")
    dt_bias = jnp.pad(w["ssd_dt_bias"], (0, 128 - SSD_HEADS)).reshape(1, 128)
    hv = _head_inputs(_dt_fwd(dt_raw, dt_bias, "dt_fwd"), w["ssd_a_log"], w["ssd_d"])
    xc = _conv_fwd(p, w["ssd_conv_w"], w["ssd_conv_b"], "conv_fwd")
    y_ssd_raw, y_ssd, states = _ssd_fwd(xc, p, hv, w["ssd_norm"], "ssd_fwd")
    b_col = w["gmlp_b_s"].reshape(GMLP_GROUPS, CHUNK, 1)
    y_gmlp = _gmlp_fwd(p, w["gmlp_v_norm"], w["gmlp_w_s"], b_col, "gmlp_fwd")
    mem_n = _rms_fwd(mem, w["mem_norm"], "mem_norm")
    kv = mm(mem_n, w["w_mem_kv"], mode="nn", out_dtype=F32, tm=256, tn=512, name="mem_kv")
    y_mem = _mem_fwd(p, kv, "mem_fwd")
    b_ssd = mm(y_ssd, w["w_branch_ssd"], mode="nn", out_dtype=BF16, tm=2048, tn=1024, tk=2048, name="branch_ssd")
    b_gmlp = mm(y_gmlp, w["w_branch_gmlp"], mode="nn", out_dtype=BF16, tm=2048, tn=1024, name="branch_gmlp")
    b_mem = mm(y_mem, w["w_branch_mem"], mode="nt", out_dtype=BF16, tm=2048, tn=1024, tk=MEM_W, name="branch_mem")
    merged = _merge_fwd(p, b_ssd, b_gmlp, b_mem, "merge_fwd")
    h2 = mm(merged, w["w_out"], mode="nn", out_dtype=F32, tm=2048, tn=1024, addend=h1, name="out_proj")
    h3, ffn2_saved = _ffn_forward(h2, w["ffn2_norm"], arriving("ffn2", second_weights), "ffn2")
    dh3, d_final, loss = _loss_head(h3, w["final_norm"], target, "loss_head")

    g = {"final_norm": d_final}
    big = {}
    dh2, g["ffn2_norm"] = _ffn_backward(dh3, h2, w["ffn2_norm"], w["ffn2_w_gate"], w["ffn2_w_up"], w["ffn2_w_down"], ffn2_saved,
                                        "ffn2", functools.partial(push, 0))
    dmerged = mm(dh2, w["w_out"], mode="nt", out_dtype=BF16, tm=2048, tn=1024, name="out_proj_dx")
    big["w_out"] = mm(merged, dh2, mode="tn", out_dtype=BF16, tm=1024, tn=1024, tk=2048, name="out_proj_dw")
    db_ssd, db_gmlp, db_mem, dp = _merge_bwd(p, dmerged, b_ssd, b_gmlp, b_mem, lax.empty((t, P_W), BF16), "merge_bwd")
    dy_ssd = mm(db_ssd, w["w_branch_ssd"], mode="nt", out_dtype=BF16, tm=2048, tn=2048, name="branch_ssd_dx")
    dy_gmlp = mm(db_gmlp, w["w_branch_gmlp"], mode="nt", out_dtype=BF16, tm=2048, tn=1024, name="branch_gmlp_dx")
    dy_mem = mm(db_mem, w["w_branch_mem"], mode="nn", out_dtype=BF16, tm=2048, tn=256, name="branch_mem_dx")
    big["w_branch_ssd"] = mm(y_ssd, db_ssd, mode="tn", out_dtype=BF16, tm=1024, tn=1024, tk=2048, name="branch_ssd_dw")
    big["w_branch_gmlp"] = mm(y_gmlp, db_gmlp, mode="tn", out_dtype=BF16, tm=1024, tn=1024, tk=2048, name="branch_gmlp_dw")
    big["w_branch_mem"] = mm(db_mem, y_mem, mode="tn", out_dtype=BF16, tm=1024, tn=256, tk=2048, name="branch_mem_dw")
    dp, dkv = _mem_bwd(p, kv, dy_mem, dp, "mem_bwd")
    big["w_mem_kv"] = mm(mem_n, dkv, mode="tn", out_dtype=BF16, tm=1024, tn=512, tk=256, name="mem_kv_dw")
    dmem_n = mm(dkv, w["w_mem_kv"], mode="nt", out_dtype=F32, tm=256, tn=1024, tk=512, name="mem_kv_dx")
    _, g["mem_norm"] = _rms_bwd(mem, w["mem_norm"], dmem_n, None, "mem_norm_bwd")
    dp, d_ws, d_bs, g["gmlp_v_norm"] = _gmlp_bwd(p, dy_gmlp, w["gmlp_v_norm"], w["gmlp_w_s"], b_col, dp, "gmlp_bwd")
    g["gmlp_w_s"] = d_ws
    g["gmlp_b_s"] = d_bs.reshape(GMLP_GROUPS, CHUNK)
    dxs, d_bm, d_cm, dp, ddt_w, hsums, g["ssd_norm"] = _ssd_bwd(xc, p, hv, w["ssd_norm"], y_ssd_raw, dy_ssd, states, dp, "ssd_bwd")
    heads = hsums[:, :, :HEADS_PER_GROUP]
    g["ssd_a_log"] = heads[:, 1, :].reshape(1, SSD_HEADS)
    g["ssd_d"] = heads[:, 2, :].reshape(1, SSD_HEADS)
    ddt = jnp.transpose(ddt_w[:, :, :HEADS_PER_GROUP], (1, 0, 2)).reshape(t, SSD_HEADS)
    ddt, d_bias = _dt_bwd(jnp.pad(ddt, ((0, 0), (0, 128 - SSD_HEADS))), dt_raw, dt_bias, "dt_bwd")
    g["ssd_dt_bias"] = d_bias[:, :SSD_HEADS]
    dws, dbs = [], []
    for dyc, col0, tag in ((dxs, 0, "x"), (d_bm, SSD_INNER, "b"), (d_cm, SSD_INNER + SSD_GROUPS * SSD_STATE, "c")):
        dacc, dw_c, db_c = _conv_bwd_act(p, dyc, w["ssd_conv_w"], w["ssd_conv_b"], col0, f"conv_bwd_act_{tag}")
        dp = _conv_bwd_dx(dacc, w["ssd_conv_w"], col0, dp, f"conv_bwd_dx_{tag}")
        dws.append(dw_c)
        dbs.append(db_c)
    g["ssd_conv_w"] = jnp.concatenate(dws, axis=1)
    g["ssd_conv_b"] = jnp.concatenate(dbs, axis=1)
    d_win_p = _matmul(dp, n2, mode="tn", out_dtype=BF16, tm=1536, tn=1024, tk=2048, name="in_proj_dw")
    d_wdt = mm(ddt, n2, mode="tn", out_dtype=BF16, tm=128, tn=1024, tk=2048, name="dt_proj_dw")
    sl = lambda a, o, n: a[o:o + n]
    big["w_in"] = jnp.concatenate([sl(d_win_p, P_Z, 2048), sl(d_win_p, P_XBC, XBC), d_wdt[:SSD_HEADS], sl(d_win_p, P_UV, 2048),
                                   sl(d_win_p, P_Q, MEM_W), sl(d_win_p, P_GL, 3 * D_MODEL)], axis=0)
    token = push(1, big)
    dn2 = mm(dp, w["w_in_p"], mode="nn", out_dtype=F32, tm=1024, tn=1024, tk=3584, name="in_proj_dx")
    dn2 = _matmul(ddt, w["w_dt"], mode="nn", out_dtype=F32, tm=2048, tn=1024, tk=128, addend=dn2, name="dt_proj_dx")
    dh1, g["mix_norm"] = _rms_bwd(h1, w["mix_norm"] + token, dn2, dh2, "mix_norm_bwd")
    dx, g["ffn1_norm"] = _ffn_backward(dh1, x, w["ffn1_norm"], w["ffn1_w_gate"], w["ffn1_w_up"], w["ffn1_w_down"], ffn1_saved,
                                       "ffn1", functools.partial(push, 2))
    return loss, dx, g


def _split_w_in(w_in_t):
    sl = lambda o, n: w_in_t[o:o + n]
    w_p = jnp.concatenate([sl(IN_GL, 3 * D_MODEL), sl(IN_Z, 2048), sl(IN_XBC, XBC), sl(IN_UV, 2048), sl(IN_Q, MEM_W),
                           jnp.zeros((P_W - P_USED, D_MODEL), w_in_t.dtype)], axis=0)
    w_dt = jnp.pad(sl(IN_DT, SSD_HEADS), ((0, 128 - SSD_HEADS), (0, 0)))
    return w_p, w_dt


def _pick_tile(rows, cap=512):
    best = None
    for tile in range(8, min(rows, cap) + 1, 8):
        if rows % tile == 0:
            best = tile
    return best if best is not None else rows


def _adamw(w, g, m, v, name):
    rows, lanes = w.shape
    tile = _pick_tile(rows, cap=max(8, (512 * 1024 // lanes) // 8 * 8))
    c1 = 1.0 / (1.0 - ADAM_B1 ** ADAM_STEP)
    c2 = 1.0 / (1.0 - ADAM_B2 ** ADAM_STEP)

    def body(w_ref, g_ref, m_ref, v_ref, d_ref, nm_ref, nv_ref):
        gv = g_ref[...]
        nm = ADAM_B1 * m_ref[...] + (1.0 - ADAM_B1) * gv
        nv = ADAM_B2 * v_ref[...] + (1.0 - ADAM_B2) * (gv * gv)
        nm_ref[...] = nm
        nv_ref[...] = nv
        d_ref[...] = -ADAM_LR * ((nm * c1) / (jnp.sqrt(nv * c2) + ADAM_EPS) + ADAM_WD * w_ref[...])

    blk = pl.BlockSpec((tile, lanes), lambda i: (i, 0))
    return pl.pallas_call(
        body, name=name, grid=(rows // tile,), in_specs=[blk] * 4, out_specs=[blk] * 3,
        out_shape=[jax.ShapeDtypeStruct((rows, lanes), F32)] * 3, compiler_params=_cparams("parallel"),
    )(w, g, m, v)


HBM = pl.BlockSpec(memory_space=pltpu.HBM)


def _place():
    x, y, c = lax.axis_index("x"), lax.axis_index("y"), lax.axis_index("c")
    chips = [(1 - x, y), (x, 1 - y), (1 - x, 1 - y)]
    return x, y, c, chips


SEM = pl.BlockSpec(memory_space=pltpu.SEMAPHORE)
EFFECT = pltpu.SideEffectType.DATAFLOW_SIDE_EFFECTING
N_PEER = 3


def _sem_outs():
    return tuple(pltpu.SemaphoreType.DMA(()) for _ in range(2 * N_PEER))


def _gather_start(slots, tag):
    def body(in_ref, *refs):
        del in_ref
        sems, thru, token = refs[:2 * N_PEER], refs[2 * N_PEER], refs[2 * N_PEER + 1]
        x, y, c, chips = _place()
        own = thru.at[2 * x + y, c]
        for j, chip in enumerate(chips):
            pltpu.make_async_remote_copy(src_ref=own, dst_ref=own, send_sem=sems[j], recv_sem=sems[N_PEER + j],
                                         device_id=(*chip, c), device_id_type=MESH).start()
        token[...] = jnp.zeros_like(token)

    out = pl.pallas_call(
        body, name=f"gather_{tag}_start",
        out_shape=_sem_outs() + (pltpu.HBM(slots.shape, slots.dtype), jax.ShapeDtypeStruct((8, 128), F32)),
        in_specs=(HBM,), out_specs=(SEM,) * (2 * N_PEER) + (HBM, pl.BlockSpec(memory_space=pltpu.VMEM)),
        input_output_aliases={0: 2 * N_PEER}, compiler_params=pltpu.CompilerParams(has_side_effects=EFFECT),
    )(pltpu.with_memory_space_constraint(slots, pltpu.HBM))
    return out[:2 * N_PEER], out[2 * N_PEER], out[2 * N_PEER + 1]


def _gather_wait(sems, thru, after, tag):
    def body(in_ref, *refs):
        del in_ref
        sems, out_ref = refs[:2 * N_PEER], refs[2 * N_PEER + 1]
        x, y, c, chips = _place()
        own = out_ref.at[2 * x + y, c]
        for j, (cx, cy) in enumerate(chips):
            cp = pltpu.make_async_remote_copy(src_ref=own, dst_ref=out_ref.at[2 * cx + cy, c], send_sem=sems[j],
                                              recv_sem=sems[N_PEER + j], device_id=(cx, cy, c), device_id_type=MESH)
            cp.wait_send()
            cp.wait_recv()

    return pl.pallas_call(
        body, name=f"gather_{tag}_wait", out_shape=pltpu.HBM(thru.shape, thru.dtype),
        in_specs=(HBM,) + (SEM,) * (2 * N_PEER) + (pl.BlockSpec(memory_space=pl.ANY),), out_specs=HBM,
        input_output_aliases={0: 0}, compiler_params=pltpu.CompilerParams(has_side_effects=EFFECT),
    )(thru, *sems, after)


def _gather_forward(slots, tag):
    def body(in_ref, out_ref, send_sems, recv_sems):
        del in_ref
        x, y, c, chips = _place()
        cps = []
        for j, (cx, cy) in enumerate(chips):
            landed = out_ref.at[2 * cx + cy, c]
            cps.append(pltpu.make_async_remote_copy(src_ref=landed, dst_ref=landed, send_sem=send_sems.at[j], recv_sem=recv_sems.at[j],
                                                    device_id=(x, y, 1 - c), device_id_type=MESH))
        for cp in cps:
            cp.start()
        for j, (cx, cy) in enumerate(chips):
            other = out_ref.at[2 * cx + cy, 1 - c]
            pltpu.make_async_remote_copy(src_ref=other, dst_ref=other, send_sem=send_sems.at[j], recv_sem=recv_sems.at[j],
                                         device_id=(x, y, 1 - c), device_id_type=MESH).wait_recv()
        for cp in cps:
            cp.wait_send()

    return pl.pallas_call(
        body, name=f"gather_{tag}_forward", out_shape=jax.ShapeDtypeStruct(slots.shape, slots.dtype),
        in_specs=[HBM], out_specs=HBM, input_output_aliases={0: 0},
        scratch_shapes=[pltpu.SemaphoreType.DMA((N_PEER,)), pltpu.SemaphoreType.DMA((N_PEER,))],
    )(slots)


def _scatter_start(pa, tag):
    ns, rh, lanes = pa.shape
    land = pltpu.with_memory_space_constraint(lax.empty((N_PEER, rh, lanes), pa.dtype), pltpu.HBM)

    def body(pa_ref, land_ref, *refs):
        x, y, c, chips = _place()
        for j, (cx, cy) in enumerate(chips):
            pltpu.make_async_remote_copy(src_ref=pa_ref.at[2 * cx + cy], dst_ref=land_ref.at[j], send_sem=refs[j],
                                         recv_sem=refs[N_PEER + j], device_id=(cx, cy, c), device_id_type=MESH).start()
        refs[-1][...] = jnp.zeros_like(refs[-1])

    out = pl.pallas_call(
        body, name=f"scatter_start_{tag}",
        out_shape=_sem_outs() + (pltpu.HBM(pa.shape, pa.dtype), pltpu.HBM(land.shape, land.dtype), jax.ShapeDtypeStruct((8, 128), F32)),
        in_specs=(HBM, HBM), out_specs=(SEM,) * (2 * N_PEER) + (HBM, HBM, pl.BlockSpec(memory_space=pltpu.VMEM)),
        input_output_aliases={0: 2 * N_PEER, 1: 2 * N_PEER + 1}, compiler_params=pltpu.CompilerParams(has_side_effects=EFFECT),
    )(pltpu.with_memory_space_constraint(pa, pltpu.HBM), land)
    return (out[:2 * N_PEER], out[2 * N_PEER], out[2 * N_PEER + 1]), out[2 * N_PEER + 2]


def _scatter_wait(sems, pa_thru, land_thru, after, tag):
    def body(pa_ref, land_ref, *refs):
        sems = refs[:2 * N_PEER]
        x, y, c, chips = _place()
        for j, (cx, cy) in enumerate(chips):
            cp = pltpu.make_async_remote_copy(src_ref=pa_ref.at[2 * cx + cy], dst_ref=land_ref.at[j], send_sem=sems[j],
                                              recv_sem=sems[N_PEER + j], device_id=(cx, cy, c), device_id_type=MESH)
            cp.wait_send()
            cp.wait_recv()

    return pl.pallas_call(
        body, name=f"scatter_wait_{tag}",
        out_shape=(pltpu.HBM(pa_thru.shape, pa_thru.dtype), pltpu.HBM(land_thru.shape, land_thru.dtype)),
        in_specs=(HBM, HBM) + (SEM,) * (2 * N_PEER) + (pl.BlockSpec(memory_space=pl.ANY),), out_specs=(HBM, HBM),
        input_output_aliases={0: 0, 1: 1}, compiler_params=pltpu.CompilerParams(has_side_effects=EFFECT),
    )(pa_thru, land_thru, *sems, after)


def _rs_swap(gp, tag):
    ns, _, rh, lanes = gp.shape

    def body(in_ref, out_ref, send_sem, recv_sem):
        x, y, c, _ = _place()
        cp = pltpu.make_async_remote_copy(src_ref=in_ref.at[:, 1 - c], dst_ref=out_ref, send_sem=send_sem, recv_sem=recv_sem,
                                          device_id=(x, y, 1 - c), device_id_type=MESH)
        cp.start()
        cp.wait_send()
        cp.wait_recv()

    return pl.pallas_call(
        body, name=f"rs_swap_{tag}", out_shape=jax.ShapeDtypeStruct((ns, rh, lanes), gp.dtype), in_specs=[HBM], out_specs=HBM,
        scratch_shapes=[pltpu.SemaphoreType.DMA, pltpu.SemaphoreType.DMA],
    )(gp)


def _rs_tile(rh):
    return _pick_tile(rh, cap=512)


def _rs_add(gp, recv, c, tag):
    ns, _, rh, lanes = gp.shape
    tile = rh

    def body(c_ref, a_ref, b_ref, o_ref):
        o_ref[...] = (a_ref[...].astype(F32) + b_ref[...].astype(F32)).astype(o_ref.dtype)

    return pl.pallas_call(
        body, name=f"rs_add_{tag}", out_shape=jax.ShapeDtypeStruct((ns, rh, lanes), gp.dtype),
        grid_spec=pltpu.PrefetchScalarGridSpec(
            num_scalar_prefetch=1, grid=(ns, rh // tile),
            in_specs=[pl.BlockSpec((None, None, tile, lanes), lambda s, i, c_ref: (s, c_ref[0], i, 0)),
                      pl.BlockSpec((None, tile, lanes), lambda s, i, c_ref: (s, i, 0))],
            out_specs=pl.BlockSpec((None, tile, lanes), lambda s, i, c_ref: (s, i, 0))),
        compiler_params=_cparams("parallel", "parallel"),
    )(c, gp, recv)


def _rs_sum(pa, recv, place, tag):
    ns, rh, lanes = pa.shape
    tile = _rs_tile(rh)

    def body(place_ref, a_ref, r_ref, o_ref):
        acc = a_ref[...].astype(F32)
        for j in range(ns - 1):
            acc = acc + r_ref[j].astype(F32)
        o_ref[...] = acc

    return pl.pallas_call(
        body, name=f"rs_sum_{tag}", out_shape=jax.ShapeDtypeStruct((2, rh, lanes), F32),
        grid_spec=pltpu.PrefetchScalarGridSpec(
            num_scalar_prefetch=1, grid=(rh // tile,),
            in_specs=[pl.BlockSpec((None, tile, lanes), lambda i, place_ref: (place_ref[0], i, 0)),
                      pl.BlockSpec((ns - 1, tile, lanes), lambda i, place_ref: (0, i, 0))],
            out_specs=pl.BlockSpec((None, tile, lanes), lambda i, place_ref: (place_ref[1], i, 0))),
        compiler_params=_cparams("parallel"),
    )(place, pa, recv)


def _rs_share(halves, tag):
    def body(in_ref, out_ref, send_sem, recv_sem):
        del in_ref
        x, y, c, _ = _place()
        cp = pltpu.make_async_remote_copy(src_ref=out_ref.at[c], dst_ref=out_ref.at[c], send_sem=send_sem, recv_sem=recv_sem,
                                          device_id=(x, y, 1 - c), device_id_type=MESH)
        cp.start()
        other = out_ref.at[1 - c]
        pltpu.make_async_remote_copy(src_ref=other, dst_ref=other, send_sem=send_sem, recv_sem=recv_sem,
                                     device_id=(x, y, 1 - c), device_id_type=MESH).wait_recv()
        cp.wait_send()

    return pl.pallas_call(
        body, name=f"rs_share_{tag}", out_shape=jax.ShapeDtypeStruct(halves.shape, halves.dtype), in_specs=[HBM], out_specs=HBM,
        input_output_aliases={0: 0}, scratch_shapes=[pltpu.SemaphoreType.DMA, pltpu.SemaphoreType.DMA],
    )(halves)


N_DEV = 8
SMALL_ROWS = 160


def _allreduce_small(v):
    m_per, n = v.shape

    def body(x_ref, out_ref, all_ref, send_sems, recv_sems, local_sem):
        x, y, c, chips = _place()
        me, sibling = (x, y, c), (x, y, 1 - c)

        def rows(px, py, pc):
            return all_ref.at[pl.ds((4 * px + 2 * py + pc) * m_per, m_per), :]

        def copy(k, block, to, src=None):
            return pltpu.make_async_remote_copy(src_ref=rows(*block) if src is None else src, dst_ref=rows(*block),
                                                send_sem=send_sems.at[k], recv_sem=recv_sems.at[k], device_id=to, device_id_type=MESH)

        mine = pltpu.make_async_copy(x_ref, rows(*me), local_sem)
        mine.start()
        first = [copy(0, me, sibling, src=x_ref)]
        first += [copy(1 + j, me, (*chip, c), src=x_ref) for j, chip in enumerate(chips)]
        for cp in first:
            cp.start()
        passed = [copy(4 + j, (*chip, c), sibling) for j, chip in enumerate(chips)]
        for j, chip in enumerate(chips):
            copy(1 + j, (*chip, c), me).wait_recv()
            passed[j].start()
        copy(0, sibling, me).wait_recv()
        for j, chip in enumerate(chips):
            copy(4 + j, (*chip, 1 - c), me).wait_recv()
        for cp in first + passed:
            cp.wait_send()
        mine.wait()
        step = 32
        for r in range(0, m_per, step):
            acc = all_ref[r:r + step, :]
            for d in range(1, N_DEV):
                acc = acc + all_ref[d * m_per + r:d * m_per + r + step, :]
            out_ref[r:r + step, :] = acc

    vm = pl.BlockSpec(memory_space=pltpu.VMEM)
    return pl.pallas_call(
        body, name="allreduce_small", out_shape=jax.ShapeDtypeStruct((m_per, n), v.dtype), in_specs=[vm], out_specs=vm,
        scratch_shapes=[pltpu.VMEM((N_DEV * m_per, n), v.dtype), pltpu.SemaphoreType.DMA((7,)), pltpu.SemaphoreType.DMA((7,)),
                        pltpu.SemaphoreType.DMA],
        compiler_params=pltpu.CompilerParams(vmem_limit_bytes=V7X_VMEM_LIMIT),
    )(v)


BIG = {"ffn1_w_gate": ((D_MODEL, D_FF), 1), "ffn1_w_up": ((D_MODEL, D_FF), 1), "ffn1_w_down": ((D_FF, D_MODEL), 0),
       "ffn2_w_gate": ((D_MODEL, D_FF), 1), "ffn2_w_up": ((D_MODEL, D_FF), 1), "ffn2_w_down": ((D_FF, D_MODEL), 0),
       "w_in": ((D_MODEL, IN_WIDTH), 1), "w_mem_kv": ((D_MODEL, 2 * MEM_W), 0), "w_branch_ssd": ((SSD_INNER, D_MODEL), 0),
       "w_branch_gmlp": ((GMLP_W, D_MODEL), 0), "w_branch_mem": ((MEM_W, D_MODEL), 1), "w_out": ((D_MODEL, D_MODEL), 0)}
FFN1 = ("ffn1_w_gate", "ffn1_w_up", "ffn1_w_down")
FFN2 = ("ffn2_w_gate", "ffn2_w_up", "ffn2_w_down")
MIXER = ("w_out", "w_branch_ssd", "w_branch_gmlp", "w_branch_mem", "w_mem_kv", "w_in")
REDUCE_GROUPS = (FFN2, MIXER, FFN1)
CONV_W_ROWS = 8


def _shard_rows_of(name):
    (a, b), _ = BIG[name]
    return a * b // N_SHARD // LANES


def _group_rows(names, extra=0):
    return -(-(sum(_shard_rows_of(n) for n in names) + extra) // 32) * 32

SMALL = [("ffn1_norm", 1), ("mix_norm", 1), ("mem_norm", 1), ("ssd_conv_b", 3), ("heads", 1), ("ssd_norm", 2),
         ("gmlp_v_norm", 1), ("gmlp_w_s", 128), ("gmlp_b_s", 1), ("ffn2_norm", 1), ("final_norm", 1), ("ssd_conv_w", 12)]
assert sum(n for _, n in SMALL) <= SMALL_ROWS
HEAD_VECS = ("ssd_dt_bias", "ssd_a_log", "ssd_d")


def _pack_small(vals, loss=None):
    parts = []
    for name, nrows in SMALL:
        if name == "heads":
            row = jnp.concatenate([vals[k].reshape(-1) for k in HEAD_VECS]
                                  + [jnp.zeros((1,), F32) if loss is None else loss.reshape(1)])
            parts.append(jnp.pad(row, (0, LANES - row.shape[0])).reshape(1, LANES))
        elif name in vals:
            parts.append(vals[name].reshape(nrows, LANES))
        else:
            parts.append(jnp.zeros((nrows, LANES), F32))
    buf = jnp.concatenate(parts, axis=0)
    return jnp.pad(buf, ((0, SMALL_ROWS - buf.shape[0]), (0, 0)))


def _unpack_small(buf):
    out, r = {}, 0
    for name, nrows in SMALL:
        blk = buf[r:r + nrows]
        r += nrows
        if name == "heads":
            for i, k in enumerate(HEAD_VECS):
                out[k] = blk[0, i * SSD_HEADS:(i + 1) * SSD_HEADS]
            out["loss"] = blk[0, 3 * SSD_HEADS]
        else:
            out[name] = blk
    return out


def _wire_shape(name):
    (a, b), axis = BIG[name]
    return (b, a) if axis == 1 else (a, b)


def _pack_weights(given, names, conv=False):
    parts = [(given[n][0].T if BIG[n][1] == 1 else given[n][0]).astype(BF16).reshape(_shard_rows_of(n), LANES) for n in names]
    if conv:
        pairs = lax.bitcast_convert_type(given["ssd_conv_w"], BF16).reshape(-1)
        parts.append(jnp.pad(pairs, (0, CONV_W_ROWS * LANES - pairs.shape[0])).reshape(CONV_W_ROWS, LANES))
    total = _group_rows(names, CONV_W_ROWS if conv else 0)
    packed = jnp.concatenate(parts, axis=0)
    packed = jnp.pad(packed, ((0, total - packed.shape[0]), (0, 0))).reshape(1, 2, total // 2, LANES)
    return jnp.broadcast_to(packed, (N_SHARD, 2, total // 2, LANES))


def _unpack_weights(slots, names, conv=False):
    rows = slots.reshape(N_SHARD, -1, LANES)
    out, r = {}, 0
    for name in names:
        n = _shard_rows_of(name)
        out[name] = rows[:, r:r + n].reshape(_wire_shape(name))
        r += n
    if conv:
        cols = XBC // N_SHARD
        pairs = rows[:, r:r + CONV_W_ROWS].reshape(N_SHARD, -1)[:, :SSD_CONV * cols * 2].reshape(N_SHARD, SSD_CONV, cols, 2)
        out["ssd_conv_w"] = jnp.transpose(lax.bitcast_convert_type(pairs, F32), (1, 0, 2)).reshape(SSD_CONV, XBC)
    return out


def _pack_grads(grads, names):
    total = _group_rows(names)
    parts = [grads[n].astype(BF16).reshape(N_SHARD, _shard_rows_of(n), LANES) for n in names]
    pad = total - sum(p.shape[1] for p in parts)
    if pad:
        parts.append(jnp.zeros((N_SHARD, pad, LANES), BF16))
    return jnp.concatenate(parts, axis=1).reshape(N_SHARD, 2, total // 2, LANES)


def kernel(x, mem, ffn1_norm, ffn1_w_gate, ffn1_w_up, ffn1_w_down, mix_norm, mem_norm, w_in, ssd_conv_w, ssd_conv_b, ssd_dt_bias, ssd_a_log, ssd_d, ssd_norm, gmlp_v_norm, gmlp_w_s, gmlp_b_s, w_mem_kv, w_branch_ssd, w_branch_gmlp, w_branch_mem, w_out, ffn2_norm, ffn2_w_gate, ffn2_w_up, ffn2_w_down, final_norm, loss_target, m_ffn1_norm, m_ffn1_w_gate, m_ffn1_w_up, m_ffn1_w_down, m_mix_norm, m_mem_norm, m_w_in, m_ssd_conv_w, m_ssd_conv_b, m_ssd_dt_bias, m_ssd_a_log, m_ssd_d, m_ssd_norm, m_gmlp_v_norm, m_gmlp_w_s, m_gmlp_b_s, m_w_mem_kv, m_w_branch_ssd, m_w_branch_gmlp, m_w_branch_mem, m_w_out, m_ffn2_norm, m_ffn2_w_gate, m_ffn2_w_up, m_ffn2_w_down, m_final_norm, v_ffn1_norm, v_ffn1_w_gate, v_ffn1_w_up, v_ffn1_w_down, v_mix_norm, v_mem_norm, v_w_in, v_ssd_conv_w, v_ssd_conv_b, v_ssd_dt_bias, v_ssd_a_log, v_ssd_d, v_ssd_norm, v_gmlp_v_norm, v_gmlp_w_s, v_gmlp_b_s, v_w_mem_kv, v_w_branch_ssd, v_w_branch_gmlp, v_w_branch_mem, v_w_out, v_ffn2_norm, v_ffn2_w_gate, v_ffn2_w_up, v_ffn2_w_down, v_final_norm):
    given = dict(x=x, mem=mem, ffn1_norm=ffn1_norm, ffn1_w_gate=ffn1_w_gate, ffn1_w_up=ffn1_w_up, ffn1_w_down=ffn1_w_down, mix_norm=mix_norm, mem_norm=mem_norm, w_in=w_in, ssd_conv_w=ssd_conv_w, ssd_conv_b=ssd_conv_b, ssd_dt_bias=ssd_dt_bias, ssd_a_log=ssd_a_log, ssd_d=ssd_d, ssd_norm=ssd_norm, gmlp_v_norm=gmlp_v_norm, gmlp_w_s=gmlp_w_s, gmlp_b_s=gmlp_b_s, w_mem_kv=w_mem_kv, w_branch_ssd=w_branch_ssd, w_branch_gmlp=w_branch_gmlp, w_branch_mem=w_branch_mem, w_out=w_out, ffn2_norm=ffn2_norm, ffn2_w_gate=ffn2_w_gate, ffn2_w_up=ffn2_w_up, ffn2_w_down=ffn2_w_down, final_norm=final_norm, loss_target=loss_target, m_ffn1_norm=m_ffn1_norm, m_ffn1_w_gate=m_ffn1_w_gate, m_ffn1_w_up=m_ffn1_w_up, m_ffn1_w_down=m_ffn1_w_down, m_mix_norm=m_mix_norm, m_mem_norm=m_mem_norm, m_w_in=m_w_in, m_ssd_conv_w=m_ssd_conv_w, m_ssd_conv_b=m_ssd_conv_b, m_ssd_dt_bias=m_ssd_dt_bias, m_ssd_a_log=m_ssd_a_log, m_ssd_d=m_ssd_d, m_ssd_norm=m_ssd_norm, m_gmlp_v_norm=m_gmlp_v_norm, m_gmlp_w_s=m_gmlp_w_s, m_gmlp_b_s=m_gmlp_b_s, m_w_mem_kv=m_w_mem_kv, m_w_branch_ssd=m_w_branch_ssd, m_w_branch_gmlp=m_w_branch_gmlp, m_w_branch_mem=m_w_branch_mem, m_w_out=m_w_out, m_ffn2_norm=m_ffn2_norm, m_ffn2_w_gate=m_ffn2_w_gate, m_ffn2_w_up=m_ffn2_w_up, m_ffn2_w_down=m_ffn2_w_down, m_final_norm=m_final_norm, v_ffn1_norm=v_ffn1_norm, v_ffn1_w_gate=v_ffn1_w_gate, v_ffn1_w_up=v_ffn1_w_up, v_ffn1_w_down=v_ffn1_w_down, v_mix_norm=v_mix_norm, v_mem_norm=v_mem_norm, v_w_in=v_w_in, v_ssd_conv_w=v_ssd_conv_w, v_ssd_conv_b=v_ssd_conv_b, v_ssd_dt_bias=v_ssd_dt_bias, v_ssd_a_log=v_ssd_a_log, v_ssd_d=v_ssd_d, v_ssd_norm=v_ssd_norm, v_gmlp_v_norm=v_gmlp_v_norm, v_gmlp_w_s=v_gmlp_w_s, v_gmlp_b_s=v_gmlp_b_s, v_w_mem_kv=v_w_mem_kv, v_w_branch_ssd=v_w_branch_ssd, v_w_branch_gmlp=v_w_branch_gmlp, v_w_branch_mem=v_w_branch_mem, v_w_out=v_w_out, v_ffn2_norm=v_ffn2_norm, v_ffn2_w_gate=v_ffn2_w_gate, v_ffn2_w_up=v_ffn2_w_up, v_ffn2_w_down=v_ffn2_w_down, v_final_norm=v_final_norm)
    weights = [n for n in given if n not in ("x", "mem", "loss_target") and not n.startswith(("m_", "v_"))]
    xi, yi, ci = lax.axis_index("x"), lax.axis_index("y"), lax.axis_index("c")
    chip = (2 * xi + yi).astype(jnp.int32)
    core = ci.astype(jnp.int32)
    conv_cols = XBC // N_SHARD

    copies = {"first": _gather_start(_pack_weights(given, FFN1), "first")}
    tied = dict(given)
    tied["w_in"], tied["ffn2_w_gate"], _ = lax.optimization_barrier((given["w_in"], given["ffn2_w_gate"], copies["first"][2]))
    packed = {"mixer": _pack_weights(tied, MIXER, conv=True), "ffn2": _pack_weights(tied, FFN2)}
    w = {}
    for name in ("ffn1_norm", "mix_norm", "mem_norm", "ssd_conv_b", "ssd_norm", "gmlp_v_norm", "ffn2_norm", "final_norm"):
        w[name] = given[name].reshape(1, -1)
    w["ffn1_norm"] = w["ffn1_norm"] + copies["first"][2][0:1, 0:1]
    for name in HEAD_VECS:
        w[name] = given[name].reshape(-1)
    w["gmlp_w_s"] = given["gmlp_w_s"][0]
    w["gmlp_b_s"] = given["gmlp_b_s"][0]

    def arrived(tag, after):
        sems, thru, _ = copies[tag]
        return _gather_forward(_gather_wait(sems, thru, after, tag), tag)

    def then_start(slots, tag, weights, name):
        slots, nxt = lax.optimization_barrier((slots, packed[tag]))
        copies[tag] = _gather_start(nxt, tag)
        unpacked = weights(slots)
        unpacked[name], _ = lax.optimization_barrier((unpacked[name], copies[tag][2]))
        return unpacked

    def first_weights(after):
        after, packed["mixer"], packed["ffn2"] = lax.optimization_barrier((after, packed["mixer"], packed["ffn2"]))
        return then_start(arrived("first", after), "mixer", lambda s: _unpack_weights(s, FFN1), FFN1[0])

    def mixer_weights(after):
        def unpack(slots):
            rest = _unpack_weights(slots, MIXER, conv=True)
            rest["w_in_p"], rest["w_dt"] = _split_w_in(rest.pop("w_in"))
            return rest
        return then_start(arrived("mixer", after), "ffn2", unpack, "w_in_p")

    def second_weights(after):
        return _unpack_weights(arrived("ffn2", after), FFN2)

    pending = {}

    def push(k, group_grads):
        gp = _pack_grads(group_grads, REDUCE_GROUPS[k])
        pa = _rs_add(gp, _rs_swap(gp, k), core.reshape(1), k)
        pending[k], token = _scatter_start(pa, k)
        return token[0:1, 0:1]

    def reduced(k, after):
        pa, land = _scatter_wait(*pending[k], after, k)
        gsum = _rs_share(_rs_sum(pa, land, jnp.stack([chip, core]), k), k)
        rows = gsum.reshape(-1, LANES)
        out, r = {}, 0
        for name in REDUCE_GROUPS[k]:
            n = _shard_rows_of(name)
            a, b = given[name].shape[1:]
            out[name] = rows[r:r + n].reshape(b, a).T if BIG[name][1] == 1 else rows[r:r + n].reshape(a, b)
            r += n
        return out

    loss_part, grad_x, g = _local_step(x[0], mem[0], loss_target[0], w, first_weights, mixer_weights, second_weights, push)

    grads, deltas, new_m, new_v = {}, {}, {}, {}

    def update(k, after):
        for name, gl in reduced(k, after).items():
            d, nm, nv = _adamw(given[name][0], gl, given["m_" + name][0], given["v_" + name][0], f"adamw_{name}")
            grads[name], deltas[name], new_m[name], new_v[name] = (a[None] for a in (gl, d, nm, nv))

    update(0, grad_x)
    update(1, deltas[REDUCE_GROUPS[0][-1]])

    small_vals = {k: g[k] for k, _ in SMALL if k != "heads"}
    small_vals.update({k: g[k] for k in HEAD_VECS})
    red = _unpack_small(_allreduce_small(_pack_small(small_vals, loss=loss_part[0, 0])))
    update(2, deltas[REDUCE_GROUPS[1][-1]])
    conv_g = lax.dynamic_slice_in_dim(red["ssd_conv_w"].reshape(SSD_CONV, XBC), chip * conv_cols, conv_cols, axis=1)
    d, nm, nv = _adamw(given["ssd_conv_w"][0], conv_g, given["m_ssd_conv_w"][0], given["v_ssd_conv_w"][0], "adamw_conv_w")
    grads["ssd_conv_w"], deltas["ssd_conv_w"], new_m["ssd_conv_w"], new_v["ssd_conv_w"] = (a[None] for a in (conv_g, d, nm, nv))
    for k in [k for k, _ in SMALL if k not in ("heads", "ssd_conv_w")] + list(HEAD_VECS):
        shape = given[k].shape
        as2d = lambda a: a.reshape(-1, shape[-1])
        d, nm, nv = _adamw(as2d(given[k]), as2d(red[k]), as2d(given["m_" + k]), as2d(given["v_" + k]), f"adamw_{k}")
        grads[k], deltas[k], new_m[k], new_v[k] = (a.reshape(shape) for a in (red[k], d, nm, nv))

    return (red["loss"], grad_x[None], *[grads[n] for n in weights], *[deltas[n] for n in weights],
            *[new_m[n] for n in weights], *[new_v[n] for n in weights])
```

```python
import functools
import math

import jax
import jax.numpy as jnp
from jax import lax
from jax.experimental import pallas as pl
from jax.experimental.pallas import tpu as pltpu

F32, BF16 = jnp.float32, jnp.bfloat16
HI = lax.Precision.HIGHEST
MESH = pl.DeviceIdType.MESH

D_MODEL = 1024
D_FF = 2816
MEM_LEN = 256
SSD_INNER = 2048
SSD_HEADS = 32
SSD_GROUPS = 4
SSD_STATE = 128
SSD_CONV = 4
CHUNK = 128
XBC = SSD_INNER + 2 * SSD_GROUPS * SSD_STATE
GMLP_W = 1024
GMLP_GROUPS = 8
MEM_W = 256
MEM_HEADS = 4
EPS = 1e-6
IN_WIDTH = 10528
IN_Z, IN_XBC, IN_DT, IN_UV, IN_Q, IN_GL = 0, 2048, 5120, 5152, 7200, 7456
P_GL, P_Z, P_XBC, P_UV, P_Q, P_W = 0, 3072, 5120, 8192, 10240, 10752
P_USED = 10496

ADAM_LR, ADAM_B1, ADAM_B2, ADAM_EPS, ADAM_WD, ADAM_STEP = 0.001, 0.9, 0.999, 1e-08, 0.01, 10

V7X_VMEM_LIMIT = 56 * 1024 * 1024
N_SHARD = 4
LANES = 1024


def _cparams(*sem):
    return pltpu.CompilerParams(dimension_semantics=sem, vmem_limit_bytes=V7X_VMEM_LIMIT)


ANY = pl.BlockSpec(memory_space=pl.ANY)


def _sigmoid(x):
    return 0.5 * jnp.tanh(0.5 * x) + 0.5


def _row_tile(t):
    return min(512, t)


_DIMS = {"nn": (((1,), (0,)), ((), ())), "nt": (((1,), (1,)), ((), ())), "tn": (((0,), (0,)), ((), ()))}


def _matmul(a, b, *, mode, out_dtype, tm, tn, tk, name, scale=1.0, addend=None):
    if mode == "tn":
        k_dim, m_dim = a.shape
    else:
        m_dim, k_dim = a.shape
    n_dim = b.shape[0] if mode == "nt" else b.shape[1]
    tm, tn, tk = min(tm, m_dim), min(tn, n_dim), min(tk, k_dim)
    assert m_dim % tm == 0 and n_dim % tn == 0 and k_dim % tk == 0, (name, a.shape, b.shape, tm, tn, tk)
    ni, nj, nk = m_dim // tm, n_dim // tn, k_dim // tk
    a_spec = pl.BlockSpec((tk, tm), lambda j, i, k: (k, i)) if mode == "tn" else pl.BlockSpec((tm, tk), lambda j, i, k: (i, k))
    b_spec = pl.BlockSpec((tn, tk), lambda j, i, k: (j, k)) if mode == "nt" else pl.BlockSpec((tk, tn), lambda j, i, k: (k, j))
    o_spec = pl.BlockSpec((tm, tn), lambda j, i, k: (i, j))
    dims = _DIMS[mode]
    has_add = addend is not None

    def body(*refs):
        a_ref, b_ref = refs[:2]
        r_ref = refs[2] if has_add else None
        o_ref = refs[2 + has_add]

        def finish(acc):
            r = acc * scale if scale != 1.0 else acc
            if has_add:
                r = r + r_ref[...].astype(F32)
            o_ref[...] = r.astype(o_ref.dtype)

        prod = lax.dot_general(a_ref[...].astype(BF16), b_ref[...].astype(BF16), dims, preferred_element_type=F32)
        if nk == 1:
            finish(prod)
            return
        acc_ref = refs[-1]
        k = pl.program_id(2)

        @pl.when(k == 0)
        def _():
            acc_ref[...] = prod

        @pl.when(k > 0)
        def _():
            acc_ref[...] += prod

        @pl.when(k == nk - 1)
        def _():
            finish(acc_ref[...])

    in_specs = [a_spec, b_spec] + ([o_spec] if has_add else [])
    args = (a, b) + ((addend,) if has_add else ())
    return pl.pallas_call(
        body, name=name, grid=(nj, ni, nk), in_specs=in_specs, out_specs=o_spec,
        out_shape=jax.ShapeDtypeStruct((m_dim, n_dim), out_dtype),
        scratch_shapes=[] if nk == 1 else [pltpu.VMEM((tm, tn), F32)],
        compiler_params=_cparams("parallel", "parallel", "arbitrary"),
    )(*args)


ROW_STRIP = 16


def _strips(tm, fn, init=None, rb=ROW_STRIP):
    def step(i, carry):
        return fn(pl.ds(pl.multiple_of(i * rb, rb), rb), carry)
    return lax.fori_loop(0, tm // rb, step, init, unroll=2)


def _rms_fwd(x, gain, name):
    t, d = x.shape
    tm = _row_tile(t)

    def body(x_ref, g_ref, o_ref):
        xv = x_ref[...]
        r = lax.rsqrt(jnp.mean(xv * xv, axis=-1, keepdims=True) + EPS)
        o_ref[...] = (xv * r * g_ref[...]).astype(o_ref.dtype)

    return pl.pallas_call(
        body, name=name, grid=(t // tm,),
        in_specs=[pl.BlockSpec((tm, d), lambda i: (i, 0)), pl.BlockSpec((1, d), lambda i: (0, 0))],
        out_specs=pl.BlockSpec((tm, d), lambda i: (i, 0)),
        out_shape=jax.ShapeDtypeStruct((t, d), BF16), compiler_params=_cparams("parallel"),
    )(x, gain)


def _rms_bwd(x, gain, dn, dres, name):
    t, d = x.shape
    tm = _row_tile(t)
    has_res = dres is not None

    def body(*refs):
        if has_res:
            x_ref, g_ref, dn_ref, r_ref, dx_ref, dg_ref = refs
        else:
            x_ref, g_ref, dn_ref, dx_ref, dg_ref = refs

        @pl.when(pl.program_id(0) == 0)
        def _():
            dg_ref[...] = jnp.zeros_like(dg_ref)

        xv = x_ref[...]
        r = lax.rsqrt(jnp.mean(xv * xv, axis=-1, keepdims=True) + EPS)
        xh = xv * r
        dnv = dn_ref[...].astype(F32)
        dg_ref[...] += jnp.sum(dnv * xh, axis=0, keepdims=True)
        dxh = dnv * g_ref[...]
        dx = r * (dxh - xh * jnp.mean(dxh * xh, axis=-1, keepdims=True))
        if has_res:
            dx = dx + r_ref[...]
        dx_ref[...] = dx

    row = pl.BlockSpec((tm, d), lambda i: (i, 0))
    vec = pl.BlockSpec((1, d), lambda i: (0, 0))
    in_specs = [row, vec, row] + ([row] if has_res else [])
    args = (x, gain, dn) + ((dres,) if has_res else ())
    return pl.pallas_call(
        body, name=name, grid=(t // tm,), in_specs=in_specs, out_specs=[row, vec],
        out_shape=[jax.ShapeDtypeStruct((t, d), F32), jax.ShapeDtypeStruct((1, d), F32)],
        compiler_params=_cparams("arbitrary"),
    )(*args)


def _loss_head(h, gain, target, name):
    t, d = h.shape
    tm = _row_tile(t)

    def body(h_ref, g_ref, t_ref, dh_ref, dg_ref, l_ref):
        @pl.when(pl.program_id(0) == 0)
        def _():
            dg_ref[...] = jnp.zeros_like(dg_ref)
            l_ref[...] = jnp.zeros_like(l_ref)

        xv = h_ref[...]
        g = g_ref[...]
        r = lax.rsqrt(jnp.mean(xv * xv, axis=-1, keepdims=True) + EPS)
        xh = xv * r
        err = xh * g - t_ref[...]
        l_ref[...] += 0.5 * jnp.sum(jnp.mean(err * err, axis=-1, keepdims=True), axis=0, keepdims=True)
        dy = err * (1.0 / d)
        dg_ref[...] += jnp.sum(dy * xh, axis=0, keepdims=True)
        dxh = dy * g
        dh_ref[...] = r * (dxh - xh * jnp.mean(dxh * xh, axis=-1, keepdims=True))

    row = pl.BlockSpec((tm, d), lambda i: (i, 0))
    vec = pl.BlockSpec((1, d), lambda i: (0, 0))
    return pl.pallas_call(
        body, name=name, grid=(t // tm,), in_specs=[row, vec, row],
        out_specs=[row, vec, pl.BlockSpec((1, 128), lambda i: (0, 0))],
        out_shape=[jax.ShapeDtypeStruct((t, d), F32), jax.ShapeDtypeStruct((1, d), F32), jax.ShapeDtypeStruct((1, 128), F32)],
        compiler_params=_cparams("arbitrary"),
    )(h, gain, target)


FF_TILE = 1408


def _ffn_fwd(n, x, wg, wu, wd, name):
    t, d = x.shape
    tm, tn = _row_tile(t), FF_TILE
    nj = D_FF // tn

    def body(n_ref, x_ref, wg_ref, wu_ref, wd_ref, h_ref, g_ref, u_ref, acc_ref):
        j = pl.program_id(1)

        @pl.when(j == 0)
        def _():
            acc_ref[...] = jnp.zeros_like(acc_ref)

        nb = n_ref[...]
        g = lax.dot_general(nb, wg_ref[...], _DIMS["nt"], preferred_element_type=F32)
        u = lax.dot_general(nb, wu_ref[...], _DIMS["nt"], preferred_element_type=F32)
        g_ref[...] = g.astype(BF16)
        u_ref[...] = u.astype(BF16)
        a = g * _sigmoid(g) * u
        acc_ref[...] += jnp.dot(a.astype(BF16), wd_ref[...], preferred_element_type=F32)

        @pl.when(j == nj - 1)
        def _():
            h_ref[...] = x_ref[...] + 0.5 * acc_ref[...]

    row = pl.BlockSpec((tm, d), lambda i, j: (i, 0))
    act = pl.BlockSpec((tm, tn), lambda i, j: (i, j))
    return pl.pallas_call(
        body, name=name, grid=(t // tm, nj),
        in_specs=[row, row] + [pl.BlockSpec((tn, d), lambda i, j: (j, 0))] * 3,
        out_specs=[row, act, act],
        out_shape=[jax.ShapeDtypeStruct((t, d), F32), jax.ShapeDtypeStruct((t, D_FF), BF16), jax.ShapeDtypeStruct((t, D_FF), BF16)],
        scratch_shapes=[pltpu.VMEM((tm, d), F32)], compiler_params=_cparams("parallel", "arbitrary"),
    )(n, x, wg, wu, wd)


def _ffn_bwd_act(dh, g, u, wg, wu, wd, name):
    t, d = dh.shape
    tm, tn = _row_tile(t), FF_TILE
    nj = D_FF // tn

    def body(dh_ref, g_ref, u_ref, wg_ref, wu_ref, wd_ref, dn_ref, dg_ref, du_ref, a_ref, acc_ref):
        j = pl.program_id(1)

        @pl.when(j == 0)
        def _():
            acc_ref[...] = jnp.zeros_like(acc_ref)

        dhb = (0.5 * dh_ref[...]).astype(BF16)
        da = lax.dot_general(dhb, wd_ref[...], _DIMS["nt"], preferred_element_type=F32)
        gv = g_ref[...].astype(F32)
        uv = u_ref[...].astype(F32)
        sg = _sigmoid(gv)
        s = gv * sg
        dg = (da * uv * (sg * (1.0 + gv * (1.0 - sg)))).astype(BF16)
        du = (da * s).astype(BF16)
        dg_ref[...] = dg
        du_ref[...] = du
        a_ref[...] = (s * uv).astype(BF16)
        acc_ref[...] += (jnp.dot(dg, wg_ref[...], preferred_element_type=F32)
                         + jnp.dot(du, wu_ref[...], preferred_element_type=F32))

        @pl.when(j == nj - 1)
        def _():
            dn_ref[...] = acc_ref[...]

    row = pl.BlockSpec((tm, d), lambda i, j: (i, 0))
    act = pl.BlockSpec((tm, tn), lambda i, j: (i, j))
    return pl.pallas_call(
        body, name=name, grid=(t // tm, nj),
        in_specs=[row, act, act] + [pl.BlockSpec((tn, d), lambda i, j: (j, 0))] * 3,
        out_specs=[row, act, act, act],
        out_shape=[jax.ShapeDtypeStruct((t, d), F32)] + [jax.ShapeDtypeStruct((t, D_FF), BF16)] * 3,
        scratch_shapes=[pltpu.VMEM((tm, d), F32)], compiler_params=_cparams("parallel", "arbitrary"),
    )(dh, g, u, wg, wu, wd)


def _ffn_forward(x, gain, weights, tag):
    n = _rms_fwd(x, gain, f"{tag}_norm")
    h, g, u = _ffn_fwd(n, x, *weights(n), f"{tag}_fwd")
    return h, (n, g, u)


def _ffn_backward(dh, x, gain, wg, wu, wd, saved, tag, push):
    n, g, u = saved
    dn, dg, du, a = _ffn_bwd_act(dh, g, u, wg, wu, wd, f"{tag}_bwd_act")
    kw = dict(mode="tn", out_dtype=BF16, tm=FF_TILE, tn=1024, tk=2048)
    d_wg = _matmul(dg, n, name=f"{tag}_dwg", **kw)
    d_wu = _matmul(du, n, name=f"{tag}_dwu", **kw)
    d_wd = _matmul(a, dh, scale=0.5, name=f"{tag}_dwd", **kw)
    token = push({f"{tag}_w_gate": d_wg, f"{tag}_w_up": d_wu, f"{tag}_w_down": d_wd})
    return _rms_bwd(x, gain + token, dn, dh, f"{tag}_norm_bwd")


CONV_COLS = 512
HALO = 8
CONV_STRIP = 32
CONV_ROWS = 2048


def _conv_fwd(p, w, b, name):
    t = p.shape[0]
    tm = min(CONV_ROWS, t)
    c0 = P_XBC // CONV_COLS

    def body(x_ref, halo_ref, w_ref, b_ref, o_ref, s_ref):
        i = pl.program_id(1)
        s_ref[0:HALO, :] = jnp.where(i > 0, halo_ref[...].astype(F32), 0.0)
        s_ref[HALO:HALO + tm, :] = x_ref[...].astype(F32)
        wv = w_ref[...]
        bv = b_ref[...]
        for r0 in range(0, tm, CONV_STRIP):
            acc = bv + wv[0:1, :] * s_ref[HALO - 3 + r0:HALO - 3 + r0 + CONV_STRIP, :]
            for k in range(1, SSD_CONV):
                acc = acc + wv[k:k + 1, :] * s_ref[HALO - 3 + k + r0:HALO - 3 + k + r0 + CONV_STRIP, :]
            o_ref[r0:r0 + CONV_STRIP, :] = (acc * _sigmoid(acc)).astype(o_ref.dtype)

    return pl.pallas_call(
        body, name=name, grid=(XBC // CONV_COLS, t // tm),
        in_specs=[pl.BlockSpec((tm, CONV_COLS), lambda j, i: (i, c0 + j)),
                  pl.BlockSpec((HALO, CONV_COLS), lambda j, i: (jnp.maximum(i * (tm // HALO) - 1, 0), c0 + j)),
                  pl.BlockSpec((SSD_CONV, CONV_COLS), lambda j, i: (0, j)), pl.BlockSpec((1, CONV_COLS), lambda j, i: (0, j))],
        out_specs=pl.BlockSpec((tm, CONV_COLS), lambda j, i: (i, j)),
        out_shape=jax.ShapeDtypeStruct((t, XBC), BF16),
        scratch_shapes=[pltpu.VMEM((tm + HALO, CONV_COLS), F32)], compiler_params=_cparams("parallel", "parallel"),
    )(p, p, w, b)


def _conv_bwd_act(p, dy, w, b, col0, name):
    t, cols = dy.shape
    tm = min(CONV_ROWS, t)
    c0 = (P_XBC + col0) // CONV_COLS
    w0 = col0 // CONV_COLS

    def body(x_ref, halo_ref, dy_ref, w_ref, b_ref, da_ref, dw_ref, db_ref, s_ref):
        i = pl.program_id(1)

        @pl.when(i == 0)
        def _():
            dw_ref[...] = jnp.zeros_like(dw_ref)
            db_ref[...] = jnp.zeros_like(db_ref)

        s_ref[0:HALO, :] = jnp.where(i > 0, halo_ref[...].astype(F32), 0.0)
        s_ref[HALO:HALO + tm, :] = x_ref[...].astype(F32)
        wv = w_ref[...]
        bv = b_ref[...]
        fold = lambda v: jnp.sum(v.reshape(CONV_STRIP // 8, 8, CONV_COLS), axis=0)
        sums = [jnp.zeros((8, CONV_COLS), F32) for _ in range(SSD_CONV + 1)]
        for r0 in range(0, tm, CONV_STRIP):
            taps = [s_ref[HALO - 3 + k + r0:HALO - 3 + k + r0 + CONV_STRIP, :] for k in range(SSD_CONV)]
            acc = bv + wv[0:1, :] * taps[0]
            for k in range(1, SSD_CONV):
                acc = acc + wv[k:k + 1, :] * taps[k]
            sg = _sigmoid(acc)
            dacc = dy_ref[r0:r0 + CONV_STRIP, :].astype(F32) * (sg * (1.0 + acc * (1.0 - sg)))
            da_ref[r0:r0 + CONV_STRIP, :] = dacc.astype(BF16)
            for k in range(SSD_CONV):
                sums[k] = sums[k] + fold(dacc * taps[k])
            sums[SSD_CONV] = sums[SSD_CONV] + fold(dacc)
        for k in range(SSD_CONV):
            dw_ref[k:k + 1, :] += jnp.sum(sums[k], axis=0, keepdims=True)
        db_ref[...] += jnp.sum(sums[SSD_CONV], axis=0, keepdims=True)

    return pl.pallas_call(
        body, name=name, grid=(cols // CONV_COLS, t // tm),
        in_specs=[pl.BlockSpec((tm, CONV_COLS), lambda j, i: (i, c0 + j)),
                  pl.BlockSpec((HALO, CONV_COLS), lambda j, i: (jnp.maximum(i * (tm // HALO) - 1, 0), c0 + j)),
                  pl.BlockSpec((tm, CONV_COLS), lambda j, i: (i, j)),
                  pl.BlockSpec((SSD_CONV, CONV_COLS), lambda j, i: (0, w0 + j)), pl.BlockSpec((1, CONV_COLS), lambda j, i: (0, w0 + j))],
        out_specs=[pl.BlockSpec((tm, CONV_COLS), lambda j, i: (i, j)), pl.BlockSpec((SSD_CONV, CONV_COLS), lambda j, i: (0, j)),
                   pl.BlockSpec((1, CONV_COLS), lambda j, i: (0, j))],
        out_shape=[jax.ShapeDtypeStruct((t, cols), BF16), jax.ShapeDtypeStruct((SSD_CONV, cols), F32), jax.ShapeDtypeStruct((1, cols), F32)],
        scratch_shapes=[pltpu.VMEM((tm + HALO, CONV_COLS), F32)], compiler_params=_cparams("parallel", "arbitrary"),
    )(p, p, dy, w, b)


def _conv_bwd_dx(dacc, w, col0, dp, name):
    t, cols = dacc.shape
    tm = min(CONV_ROWS, t)
    nt = t // tm
    w0 = col0 // CONV_COLS
    c0 = (P_XBC + col0) // CONV_COLS

    def body(d_ref, halo_ref, w_ref, dp_ref, o_ref, s_ref):
        del dp_ref
        i = pl.program_id(1)
        s_ref[0:tm, :] = d_ref[...].astype(F32)
        s_ref[tm:tm + HALO, :] = jnp.where(i < nt - 1, halo_ref[...].astype(F32), 0.0)
        wv = w_ref[...]
        for r0 in range(0, tm, CONV_STRIP):
            acc = wv[3:4, :] * s_ref[r0:r0 + CONV_STRIP, :]
            for k in range(SSD_CONV - 1):
                acc = acc + wv[k:k + 1, :] * s_ref[3 - k + r0:3 - k + r0 + CONV_STRIP, :]
            o_ref[r0:r0 + CONV_STRIP, :] = acc.astype(o_ref.dtype)

    return pl.pallas_call(
        body, name=name, grid=(cols // CONV_COLS, nt),
        in_specs=[pl.BlockSpec((tm, CONV_COLS), lambda j, i: (i, j)),
                  pl.BlockSpec((HALO, CONV_COLS), lambda j, i: (jnp.minimum((i + 1) * (tm // HALO), t // HALO - 1), j)),
                  pl.BlockSpec((SSD_CONV, CONV_COLS), lambda j, i: (0, w0 + j)), ANY],
        out_specs=pl.BlockSpec((tm, CONV_COLS), lambda j, i: (i, c0 + j)),
        out_shape=jax.ShapeDtypeStruct(dp.shape, dp.dtype), input_output_aliases={3: 0},
        scratch_shapes=[pltpu.VMEM((tm + HALO, CONV_COLS), F32)], compiler_params=_cparams("parallel", "parallel"),
    )(dacc, dacc, w, dp)


GROUP_COLS = SSD_INNER // SSD_GROUPS
PAIRS = GROUP_COLS // 128
HEADS_PER_GROUP = SSD_HEADS // SSD_GROUPS


def _dt_fwd(dt_raw, bias, name):
    t, n = dt_raw.shape
    tm = _row_tile(t)

    def body(x_ref, b_ref, o_ref):
        v = x_ref[...] + b_ref[...]
        o_ref[...] = jnp.maximum(v, 0.0) + jnp.log1p(jnp.exp(-jnp.abs(v)))

    row = pl.BlockSpec((tm, n), lambda i: (i, 0))
    vec = pl.BlockSpec((1, n), lambda i: (0, 0))
    return pl.pallas_call(body, name=name, grid=(t // tm,), in_specs=[row, vec], out_specs=row,
                          out_shape=jax.ShapeDtypeStruct((t, n), F32), compiler_params=_cparams("parallel"))(dt_raw, bias)


def _dt_bwd(ddt, dt_raw, bias, name):
    t, n = dt_raw.shape
    tm = _row_tile(t)

    def body(d_ref, x_ref, b_ref, o_ref, db_ref):
        @pl.when(pl.program_id(0) == 0)
        def _():
            db_ref[...] = jnp.zeros_like(db_ref)

        dr = d_ref[...] * _sigmoid(x_ref[...] + b_ref[...])
        o_ref[...] = dr.astype(o_ref.dtype)
        db_ref[...] += jnp.sum(dr, axis=0, keepdims=True)

    row = pl.BlockSpec((tm, n), lambda i: (i, 0))
    vec = pl.BlockSpec((1, n), lambda i: (0, 0))
    return pl.pallas_call(body, name=name, grid=(t // tm,), in_specs=[row, row, vec], out_specs=[row, vec],
                          out_shape=[jax.ShapeDtypeStruct((t, n), BF16), jax.ShapeDtypeStruct((1, n), F32)],
                          compiler_params=_cparams("arbitrary"))(ddt, dt_raw, bias)


SSD_STEP = 4


def _ssd_common(dt, dtt, a_log_w, a_log_t):
    l = CHUNK
    a = -jnp.exp(a_log_w)
    at = -jnp.exp(a_log_t)
    rowi = lax.broadcasted_iota(jnp.int32, (l, l), 0)
    coli = lax.broadcasted_iota(jnp.int32, (l, l), 1)
    tri = rowi >= coli
    lower = tri.astype(F32)
    upper = (rowi <= coli).astype(F32)
    acs = jnp.dot(lower, dt * a, precision=HI, preferred_element_type=F32)
    acst = jnp.dot(dtt * at, upper, precision=HI, preferred_element_type=F32)
    return a, acs, acst, jnp.where(tri, 0.0, -1e30), upper


def _pair_bc(w, lo, p):
    return jnp.where(lo, w[:, 2 * p:2 * p + 1], w[:, 2 * p + 1:2 * p + 2])


def _ssd_specs(t):
    rows = SSD_STEP * CHUNK
    assert t % rows == 0
    return t // rows, dict(
        xs=lambda cm: pl.BlockSpec((rows, GROUP_COLS), lambda g, c: (cm(c), g)),
        bm=lambda cm: pl.BlockSpec((rows, SSD_STATE), lambda g, c: (cm(c), SSD_INNER // SSD_STATE + g)),
        cmat=lambda cm: pl.BlockSpec((rows, SSD_STATE), lambda g, c: (cm(c), SSD_INNER // SSD_STATE + SSD_GROUPS + g)),
        dtw=lambda cm: pl.BlockSpec((1, rows, 128), lambda g, c: (g, cm(c), 0)),
        dtt=lambda cm: pl.BlockSpec((1, HEADS_PER_GROUP, rows), lambda g, c: (g, 0, cm(c))),
        wide=lambda cm: pl.BlockSpec((1, 1, 128), lambda g, c: (g, 0, 0)),
        tall=lambda cm: pl.BlockSpec((1, HEADS_PER_GROUP, 1), lambda g, c: (g, 0, 0)),
        grp=lambda cm: pl.BlockSpec((rows, GROUP_COLS), lambda g, c: (cm(c), g)),
        zp=lambda cm: pl.BlockSpec((rows, GROUP_COLS), lambda g, c: (cm(c), P_Z // GROUP_COLS + g)),
        vec=lambda cm: pl.BlockSpec((1, GROUP_COLS), lambda g, c: (0, g)),
        state=lambda cm: pl.BlockSpec((1, SSD_STEP, PAIRS, SSD_STATE, 128), lambda g, c: (g, cm(c), 0, 0, 0)),
    )


def _ssd_fwd(xc, p, hv, norm_g, name):
    t = xc.shape[0]
    nc, sp = _ssd_specs(t)
    ident = lambda c: c

    def body(xs_ref, b_ref, c_ref, dtw_ref, dtt_ref, aw_ref, at_ref, dk_ref, z_ref, ng_ref,
             y_ref, ys_ref, h_ref, st_ref):
        @pl.when(pl.program_id(1) == 0)
        def _():
            st_ref[...] = jnp.zeros_like(st_ref)

        lo = lax.broadcasted_iota(jnp.int32, (1, 128), 1) < 64
        dskip = dk_ref[0]
        for s in range(SSD_STEP):
            rows = slice(s * CHUNK, (s + 1) * CHUNK)
            dt = dtw_ref[0, rows, :]
            a, acs, acst, causal, _ = _ssd_common(dt, dtt_ref[0, :, rows], aw_ref[0], at_ref[0])
            ecs = jnp.exp(acs)
            alast = acs[CHUNK - 1:CHUNK, :]
            bmat, cmat = b_ref[rows, :], c_ref[rows, :]
            cb = lax.dot_general(cmat, bmat, _DIMS["nt"], preferred_element_type=F32)
            for pi in range(PAIRS):
                cols = slice(pi * 128, (pi + 1) * 128)
                x = xs_ref[rows, cols].astype(F32)
                xdt = x * _pair_bc(dt, lo, pi)
                xdtb = xdt.astype(BF16)
                ydiag = jnp.zeros((CHUNK, 128), F32)
                for r, mask in ((2 * pi, lo), (2 * pi + 1, jnp.logical_not(lo))):
                    lam = jnp.exp(acs[:, r:r + 1] - acst[r:r + 1, :] + causal)
                    m = (cb * lam).astype(BF16)
                    ydiag = ydiag + jnp.dot(m, jnp.where(mask, xdtb, 0), preferred_element_type=F32)
                ht = st_ref[pi]
                h_ref[0, s, pi] = ht
                yoff = jnp.dot(cmat, ht.astype(BF16), preferred_element_type=F32) * _pair_bc(ecs, lo, pi)
                y_ref[rows, cols] = (ydiag + yoff + _pair_bc(dskip, lo, pi) * x).astype(y_ref.dtype)
                alp = _pair_bc(alast, lo, pi)
                e = jnp.exp(alp - _pair_bc(acs, lo, pi))
                st = lax.dot_general(bmat, (xdt * e).astype(BF16), _DIMS["tn"], preferred_element_type=F32)
                st_ref[pi] = ht * jnp.exp(alp) + st
            zf = z_ref[rows, :].astype(F32)
            yg = y_ref[rows, :].astype(F32) * (zf * _sigmoid(zf))
            rstd = lax.rsqrt(jnp.mean(yg * yg, axis=-1, keepdims=True) + EPS)
            ys_ref[rows, :] = (yg * rstd * ng_ref[...]).astype(ys_ref.dtype)

    ins = ["xs", "bm", "cmat", "dtw", "dtt", "wide", "tall", "wide", "zp", "vec"]
    return pl.pallas_call(
        body, name=name, grid=(SSD_GROUPS, nc),
        in_specs=[sp[k](ident) for k in ins],
        out_specs=[sp["grp"](ident), sp["grp"](ident), sp["state"](ident)],
        out_shape=[jax.ShapeDtypeStruct((t, SSD_INNER), BF16), jax.ShapeDtypeStruct((t, SSD_INNER), BF16),
                   jax.ShapeDtypeStruct((SSD_GROUPS, t // CHUNK, PAIRS, SSD_STATE, 128), F32)],
        scratch_shapes=[pltpu.VMEM((PAIRS, SSD_STATE, 128), F32)], compiler_params=_cparams("parallel", "arbitrary"),
    )(xc, xc, xc, hv["dtw"], hv["dtt"], hv["alog_w"], hv["alog_t"], hv["dskip_w"], p, norm_g)


def _ssd_bwd(xc, p, hv, norm_g, y, dys, states, dp, name):
    t = xc.shape[0]
    nc, sp = _ssd_specs(t)
    rev = lambda c: nc - 1 - c

    def body(xs_ref, b_ref, c_ref, dtw_ref, dtt_ref, aw_ref, at_ref, dk_ref, z_ref, ng_ref,
             y_ref, dys_ref, h_ref, dp_ref,
             dxs_ref, db_ref, dc_ref, dz_ref, ddt_ref, hsum_ref, dng_ref, dst_ref):
        del dp_ref
        @pl.when(pl.program_id(1) == 0)
        def _():
            dst_ref[...] = jnp.zeros_like(dst_ref)
            hsum_ref[...] = jnp.zeros_like(hsum_ref)
            dng_ref[...] = jnp.zeros_like(dng_ref)

        lane = lax.broadcasted_iota(jnp.int32, (1, 128), 1)
        lo = lane < 64
        dskip = dk_ref[0]
        sel_r = lax.broadcasted_iota(jnp.int32, (128, 128), 0)
        sel_c = lax.broadcasted_iota(jnp.int32, (128, 128), 1)
        refs = (xs_ref, b_ref, c_ref, dtw_ref, dtt_ref, aw_ref, at_ref, dk_ref, z_ref, ng_ref, y_ref, dys_ref, h_ref,
                dxs_ref, db_ref, dc_ref, dz_ref, ddt_ref, hsum_ref, dng_ref, dst_ref)
        for s in reversed(range(SSD_STEP)):
            chunk_bwd(refs, slice(s * CHUNK, (s + 1) * CHUNK), s, lane, lo, dskip, sel_r, sel_c)

    def chunk_bwd(refs, rows, s, lane, lo, dskip, sel_r, sel_c):
        (xs_ref, b_ref, c_ref, dtw_ref, dtt_ref, aw_ref, at_ref, dk_ref, z_ref, ng_ref, y_ref, dys_ref, h_ref,
         dxs_ref, db_ref, dc_ref, dz_ref, ddt_ref, hsum_ref, dng_ref, dst_ref) = refs
        dt = dtw_ref[0, rows, :]
        a, acs, acst, causal, upper = _ssd_common(dt, dtt_ref[0, :, rows], aw_ref[0], at_ref[0])
        ecs = jnp.exp(acs)
        alast = acs[CHUNK - 1:CHUNK, :]
        bmat, cmat = b_ref[rows, :], c_ref[rows, :]
        cb = lax.dot_general(cmat, bmat, _DIMS["nt"], preferred_element_type=F32)

        zf = z_ref[rows, :].astype(F32)
        sg = _sigmoid(zf)
        sz = zf * sg
        yv = y_ref[rows, :].astype(F32)
        yg = yv * sz
        rstd = lax.rsqrt(jnp.mean(yg * yg, axis=-1, keepdims=True) + EPS)
        yhat = yg * rstd
        dysv = dys_ref[rows, :].astype(F32)
        dng_ref[...] += jnp.sum(dysv * yhat, axis=0, keepdims=True)
        dyh = dysv * ng_ref[...]
        dyg = rstd * (dyh - yhat * jnp.mean(dyh * yhat, axis=-1, keepdims=True))
        dz_ref[rows, :] = (dyg * yv * (sg * (1.0 + zf * (1.0 - sg)))).astype(dz_ref.dtype)
        dy_all = dyg * sz

        dal = jnp.zeros((CHUNK, 128), F32)
        ddtm = jnp.zeros((CHUNK, 128), F32)
        dalast = jnp.zeros((8, 128), F32)
        ddsk = jnp.zeros((8, 128), F32)
        dcb = jnp.zeros((CHUNK, CHUNK), F32)
        qcol = jnp.zeros((8, CHUNK), F32)
        sub8 = lax.broadcasted_iota(jnp.int32, (8, CHUNK), 0)
        dc_acc = jnp.zeros((CHUNK, SSD_STATE), F32)
        db_acc = jnp.zeros((CHUNK, SSD_STATE), F32)
        for pi in range(PAIRS):
            sel = (sel_c == 2 * pi + (sel_r >= 64).astype(jnp.int32)).astype(BF16)

            def hsum(v, sel=sel):
                return jnp.dot(v.astype(BF16), sel, preferred_element_type=F32)

            dyp = dy_all[:, pi * 128:(pi + 1) * 128]
            x = xs_ref[rows, pi * 128:(pi + 1) * 128].astype(F32)
            dtp = _pair_bc(dt, lo, pi)
            xdt = x * dtp
            dxdt = jnp.zeros((CHUNK, 128), F32)
            dypb, xdtb = dyp.astype(BF16), xdt.astype(BF16)
            for r, mask in ((2 * pi, lo), (2 * pi + 1, jnp.logical_not(lo))):
                lam = jnp.exp(acs[:, r:r + 1] - acst[r:r + 1, :] + causal)
                m32 = cb * lam
                m = m32.astype(BF16)
                dyr = jnp.where(mask, dypb, 0)
                xr = jnp.where(mask, xdtb, 0)
                dm = lax.dot_general(dyr, xr, _DIMS["nt"], preferred_element_type=F32)
                dcb = dcb + dm * lam
                q = dm * m32
                dal = dal + jnp.sum(q, axis=1, keepdims=True) * (lane == r).astype(F32)
                qcol = qcol + jnp.where(sub8 == r, jnp.sum(q, axis=0, keepdims=True), 0.0)
                dxdt = dxdt + lax.dot_general(m, dyr, _DIMS["tn"], preferred_element_type=F32)
            ht = h_ref[0, s, pi]
            htb = ht.astype(BF16)
            ecp = _pair_bc(ecs, lo, pi)
            yoff = jnp.dot(cmat, htb, preferred_element_type=F32) * ecp
            dg = (dyp * ecp).astype(BF16)
            dc_acc = dc_acc + lax.dot_general(dg, htb, _DIMS["nt"], preferred_element_type=F32)
            dht = lax.dot_general(cmat, dg, _DIMS["tn"], preferred_element_type=F32)
            dal = dal + hsum(dyp * yoff)
            dhn = dst_ref[pi]
            dhnb = dhn.astype(BF16)
            alp = _pair_bc(alast, lo, pi)
            e = jnp.exp(alp - _pair_bc(acs, lo, pi))
            xe = xdt * e
            db_acc = db_acc + lax.dot_general(xe.astype(BF16), dhnb, _DIMS["nt"], preferred_element_type=F32)
            dxe = jnp.dot(bmat, dhnb, preferred_element_type=F32)
            dxdt = dxdt + dxe * e
            tt = hsum(dxe * xe)
            dal = dal - tt
            dec = jnp.exp(alp)
            dalast = dalast + jnp.sum(tt, axis=0, keepdims=True) + hsum(
                jnp.broadcast_to(jnp.sum(dhn * ht, axis=0, keepdims=True) * dec, (8, 128)))
            dst_ref[pi] = dht + dhn * dec
            dxs_ref[rows, pi * 128:(pi + 1) * 128] = (_pair_bc(dskip, lo, pi) * dyp + dxdt * dtp).astype(dxs_ref.dtype)
            ddtm = ddtm + hsum(dxdt * x)
            ddsk = ddsk + hsum(jnp.broadcast_to(jnp.sum(dyp * x, axis=0, keepdims=True), (8, 128)))
        rowi = lax.broadcasted_iota(jnp.int32, (CHUNK, 128), 0)
        qcol_w = lax.dot_general(jnp.concatenate([qcol, jnp.zeros((CHUNK - 8, CHUNK), F32)], axis=0), (sel_r == sel_c).astype(F32),
                                 _DIMS["tn"], precision=HI, preferred_element_type=F32)
        dal = dal - qcol_w + jnp.where(rowi == CHUNK - 1, dalast[0:1, :], 0.0)
        dda = jnp.dot(upper, dal, precision=HI, preferred_element_type=F32)
        ddt_ref[0, rows, :] = ddtm + dda * a
        hsum_ref[0, 1:2, :] += jnp.sum(dda * dt, axis=0, keepdims=True) * a
        hsum_ref[0, 2:3, :] += ddsk[0:1, :]
        dcbb = dcb.astype(BF16)
        dc_ref[rows, :] = (jnp.dot(dcbb, bmat, preferred_element_type=F32) + dc_acc).astype(dc_ref.dtype)
        db_ref[rows, :] = (lax.dot_general(dcbb, cmat, _DIMS["tn"], preferred_element_type=F32) + db_acc).astype(db_ref.dtype)

    ins = ["xs", "bm", "cmat", "dtw", "dtt", "wide", "tall", "wide", "zp", "vec", "grp", "grp", "state"]
    col = lambda: pl.BlockSpec((SSD_STEP * CHUNK, SSD_STATE), lambda g, c: (rev(c), g))
    return pl.pallas_call(
        body, name=name, grid=(SSD_GROUPS, nc),
        in_specs=[sp[k](rev) for k in ins] + [ANY],
        out_specs=[sp["grp"](rev), col(), col(), sp["zp"](rev), sp["dtw"](rev),
                   pl.BlockSpec((1, 8, 128), lambda g, c: (g, 0, 0)), sp["vec"](rev)],
        out_shape=[jax.ShapeDtypeStruct((t, SSD_INNER), BF16), jax.ShapeDtypeStruct((t, SSD_GROUPS * SSD_STATE), BF16),
                   jax.ShapeDtypeStruct((t, SSD_GROUPS * SSD_STATE), BF16), jax.ShapeDtypeStruct(dp.shape, dp.dtype),
                   jax.ShapeDtypeStruct((SSD_GROUPS, t, 128), F32), jax.ShapeDtypeStruct((SSD_GROUPS, 8, 128), F32),
                   jax.ShapeDtypeStruct((1, SSD_INNER), F32)],
        input_output_aliases={len(ins): 3},
        scratch_shapes=[pltpu.VMEM((PAIRS, SSD_STATE, 128), F32)], compiler_params=_cparams("parallel", "arbitrary"),
    )(xc, xc, xc, hv["dtw"], hv["dtt"], hv["alog_w"], hv["alog_t"], hv["dskip_w"], p, norm_g, y, dys, states, dp)


def _wide(v):
    return jnp.pad(v.reshape(SSD_GROUPS, 1, HEADS_PER_GROUP), ((0, 0), (0, 0), (0, 128 - HEADS_PER_GROUP)))


def _head_inputs(dt, a_log, d_skip):
    t = dt.shape[0]
    g = dt[:, :SSD_HEADS].reshape(t, SSD_GROUPS, HEADS_PER_GROUP)
    return dict(
        dtw=jnp.pad(jnp.transpose(g, (1, 0, 2)), ((0, 0), (0, 0), (0, 128 - HEADS_PER_GROUP))),
        dtt=jnp.transpose(g, (1, 2, 0)),
        alog_w=_wide(a_log), alog_t=a_log.reshape(SSD_GROUPS, HEADS_PER_GROUP, 1),
        dskip_w=_wide(d_skip),
    )


def _gelu(x):
    return 0.5 * x * (1.0 + lax.erf(x * (1.0 / math.sqrt(2.0))))


def _gelu_grad(x):
    return 0.5 * (1.0 + lax.erf(x * (1.0 / math.sqrt(2.0)))) + x * jnp.exp(-0.5 * x * x) * (1.0 / math.sqrt(2.0 * math.pi))


def _tril_mask():
    r = lax.broadcasted_iota(jnp.int32, (CHUNK, CHUNK), 0)
    c = lax.broadcasted_iota(jnp.int32, (CHUNK, CHUNK), 1)
    return r >= c


def _gmlp_fwd(p, v_gain, w_s, b_col, name):
    t = p.shape[0]
    tm = _row_tile(t)
    u0 = P_UV // GMLP_W

    def body(u_ref, v_ref, gn_ref, ws_ref, bs_ref, o_ref):
        v = _gelu(v_ref[...].astype(F32))
        v = (v * lax.rsqrt(jnp.mean(v * v, axis=-1, keepdims=True) + EPS) * gn_ref[...]).astype(BF16)
        tril = _tril_mask()
        wm = [jnp.where(tril, ws_ref[g], 0.0).astype(BF16) for g in range(GMLP_GROUPS)]
        for k in range(tm // CHUNK):
            rows = slice(k * CHUNK, (k + 1) * CHUNK)
            for g in range(GMLP_GROUPS):
                cols = slice(g * 128, (g + 1) * 128)
                mixed = jnp.dot(wm[g], v[rows, cols], preferred_element_type=F32) + bs_ref[g]
                o_ref[rows, cols] = (_gelu(u_ref[rows, cols].astype(F32)) * mixed).astype(o_ref.dtype)

    return pl.pallas_call(
        body, name=name, grid=(t // tm,),
        in_specs=[pl.BlockSpec((tm, GMLP_W), lambda i: (i, u0)), pl.BlockSpec((tm, GMLP_W), lambda i: (i, u0 + 1)),
                  pl.BlockSpec((1, GMLP_W), lambda i: (0, 0)), pl.BlockSpec((GMLP_GROUPS, CHUNK, CHUNK), lambda i: (0, 0, 0)),
                  pl.BlockSpec((GMLP_GROUPS, CHUNK, 1), lambda i: (0, 0, 0))],
        out_specs=pl.BlockSpec((tm, GMLP_W), lambda i: (i, 0)),
        out_shape=jax.ShapeDtypeStruct((t, GMLP_W), BF16), compiler_params=_cparams("parallel"),
    )(p, p, v_gain, w_s, b_col)


def _gmlp_bwd(p, dy, v_gain, w_s, b_col, dp, name):
    t = p.shape[0]
    tm = _row_tile(t)
    u0 = P_UV // GMLP_W

    def body(u_ref, v_ref, dy_ref, gn_ref, ws_ref, bs_ref, dp_ref, duv_ref, dws_ref, dbs_ref, dgn_ref, dvn_ref):
        del dp_ref

        @pl.when(pl.program_id(0) == 0)
        def _():
            dws_ref[...] = jnp.zeros_like(dws_ref)
            dbs_ref[...] = jnp.zeros_like(dbs_ref)
            dgn_ref[...] = jnp.zeros_like(dgn_ref)

        vraw = v_ref[...].astype(F32)
        va = _gelu(vraw)
        rstd = lax.rsqrt(jnp.mean(va * va, axis=-1, keepdims=True) + EPS)
        vhat = va * rstd
        gain = gn_ref[...]
        vn = (vhat * gain).astype(BF16)
        tril = _tril_mask()
        wm = [jnp.where(tril, ws_ref[g], 0.0).astype(BF16) for g in range(GMLP_GROUPS)]
        for k in range(tm // CHUNK):
            rows = slice(k * CHUNK, (k + 1) * CHUNK)
            for g in range(GMLP_GROUPS):
                cols = slice(g * 128, (g + 1) * 128)
                uraw = u_ref[rows, cols].astype(F32)
                vb = vn[rows, cols]
                mixed = jnp.dot(wm[g], vb, preferred_element_type=F32) + bs_ref[g]
                dyb = dy_ref[rows, cols].astype(F32)
                duv_ref[rows, cols] = (dyb * mixed * _gelu_grad(uraw)).astype(duv_ref.dtype)
                dmix = dyb * _gelu(uraw)
                dmb = dmix.astype(BF16)
                dws_ref[g] += jnp.where(tril, lax.dot_general(dmb, vb, _DIMS["nt"], preferred_element_type=F32), 0.0)
                dbs_ref[g] += jnp.sum(dmix, axis=1, keepdims=True)
                dvn_ref[rows, cols] = lax.dot_general(wm[g], dmb, _DIMS["tn"], preferred_element_type=F32)
        dvn = dvn_ref[...]
        dgn_ref[...] += jnp.sum(dvn * vhat, axis=0, keepdims=True)
        dvh = dvn * gain
        dva = rstd * (dvh - vhat * jnp.mean(dvh * vhat, axis=-1, keepdims=True))
        duv_ref[:, GMLP_W:2 * GMLP_W] = (dva * _gelu_grad(vraw)).astype(duv_ref.dtype)

    return pl.pallas_call(
        body, name=name, grid=(t // tm,),
        in_specs=[pl.BlockSpec((tm, GMLP_W), lambda i: (i, u0)), pl.BlockSpec((tm, GMLP_W), lambda i: (i, u0 + 1)),
                  pl.BlockSpec((tm, GMLP_W), lambda i: (i, 0)),
                  pl.BlockSpec((1, GMLP_W), lambda i: (0, 0)), pl.BlockSpec((GMLP_GROUPS, CHUNK, CHUNK), lambda i: (0, 0, 0)),
                  pl.BlockSpec((GMLP_GROUPS, CHUNK, 1), lambda i: (0, 0, 0)), ANY],
        out_specs=[pl.BlockSpec((tm, 2 * GMLP_W), lambda i: (i, P_UV // (2 * GMLP_W))),
                   pl.BlockSpec((GMLP_GROUPS, CHUNK, CHUNK), lambda i: (0, 0, 0)),
                   pl.BlockSpec((GMLP_GROUPS, CHUNK, 1), lambda i: (0, 0, 0)), pl.BlockSpec((1, GMLP_W), lambda i: (0, 0))],
        out_shape=[jax.ShapeDtypeStruct(dp.shape, dp.dtype), jax.ShapeDtypeStruct((GMLP_GROUPS, CHUNK, CHUNK), F32),
                   jax.ShapeDtypeStruct((GMLP_GROUPS, CHUNK, 1), F32), jax.ShapeDtypeStruct((1, GMLP_W), F32)],
        input_output_aliases={6: 0},
        scratch_shapes=[pltpu.VMEM((tm, GMLP_W), F32)], compiler_params=_cparams("arbitrary"),
    )(p, p, dy, v_gain, w_s, b_col, dp)


def _head_masks():
    lane = lax.broadcasted_iota(jnp.int32, (1, MEM_W), 1)
    return [(lane >= h * 64) & (lane < (h + 1) * 64) for h in range(MEM_HEADS)]


def _mem_fwd(p, kv, name):
    t = p.shape[0]
    tm = _row_tile(t)
    q0 = P_Q // MEM_W

    def body(q_ref, kv_ref, o_ref):
        q = q_ref[...]
        k = kv_ref[:, 0:MEM_W].astype(BF16)
        v = kv_ref[:, MEM_W:2 * MEM_W].astype(BF16)
        out = jnp.zeros((tm, MEM_W), F32)
        for mask in _head_masks():
            s = lax.dot_general(jnp.where(mask, q, 0), k, _DIMS["nt"], preferred_element_type=F32) * 0.125
            e = jnp.exp(s - jnp.max(s, axis=-1, keepdims=True))
            pr = (e * (1.0 / jnp.sum(e, axis=-1, keepdims=True))).astype(BF16)
            out = out + jnp.where(mask, jnp.dot(pr, v, preferred_element_type=F32), 0.0)
        o_ref[...] = out.astype(o_ref.dtype)

    return pl.pallas_call(
        body, name=name, grid=(t // tm,),
        in_specs=[pl.BlockSpec((tm, MEM_W), lambda i: (i, q0)), pl.BlockSpec((MEM_LEN, 2 * MEM_W), lambda i: (0, 0))],
        out_specs=pl.BlockSpec((tm, MEM_W), lambda i: (i, 0)),
        out_shape=jax.ShapeDtypeStruct((t, MEM_W), BF16), compiler_params=_cparams("parallel"),
    )(p, kv)


def _mem_bwd(p, kv, dy, dp, name):
    t = p.shape[0]
    tm = _row_tile(t)
    q0 = P_Q // MEM_W
    assert P_W - P_Q == 2 * MEM_W

    def body(q_ref, kv_ref, dy_ref, dp_ref, dq_ref, dkv_ref):
        del dp_ref

        @pl.when(pl.program_id(0) == 0)
        def _():
            dkv_ref[...] = jnp.zeros_like(dkv_ref)

        q = q_ref[...]
        dy = dy_ref[...]
        k = kv_ref[:, 0:MEM_W].astype(BF16)
        v = kv_ref[:, MEM_W:2 * MEM_W].astype(BF16)
        dq = jnp.zeros((tm, MEM_W), F32)
        dk = jnp.zeros((MEM_LEN, MEM_W), F32)
        dv = jnp.zeros((MEM_LEN, MEM_W), F32)
        for mask in _head_masks():
            qh = jnp.where(mask, q, 0)
            dyh = jnp.where(mask, dy, 0)
            s = lax.dot_general(qh, k, _DIMS["nt"], preferred_element_type=F32) * 0.125
            e = jnp.exp(s - jnp.max(s, axis=-1, keepdims=True))
            pr = e * (1.0 / jnp.sum(e, axis=-1, keepdims=True))
            prb = pr.astype(BF16)
            dp = lax.dot_general(dyh, v, _DIMS["nt"], preferred_element_type=F32)
            ds = (pr * (dp - jnp.sum(dp * pr, axis=-1, keepdims=True)) * 0.125).astype(BF16)
            dq = dq + jnp.where(mask, jnp.dot(ds, k, preferred_element_type=F32), 0.0)
            dk = dk + lax.dot_general(ds, qh, _DIMS["tn"], preferred_element_type=F32)
            dv = dv + lax.dot_general(prb, dyh, _DIMS["tn"], preferred_element_type=F32)
        dq_ref[:, 0:MEM_W] = dq.astype(dq_ref.dtype)
        dq_ref[:, MEM_W:2 * MEM_W] = jnp.zeros((tm, MEM_W), dq_ref.dtype)
        dkv_ref[:, 0:MEM_W] += dk
        dkv_ref[:, MEM_W:2 * MEM_W] += dv

    return pl.pallas_call(
        body, name=name, grid=(t // tm,),
        in_specs=[pl.BlockSpec((tm, MEM_W), lambda i: (i, q0)), pl.BlockSpec((MEM_LEN, 2 * MEM_W), lambda i: (0, 0)),
                  pl.BlockSpec((tm, MEM_W), lambda i: (i, 0)), ANY],
        out_specs=[pl.BlockSpec((tm, 2 * MEM_W), lambda i: (i, P_Q // (2 * MEM_W))),
                   pl.BlockSpec((MEM_LEN, 2 * MEM_W), lambda i: (0, 0))],
        out_shape=[jax.ShapeDtypeStruct(dp.shape, dp.dtype), jax.ShapeDtypeStruct((MEM_LEN, 2 * MEM_W), F32)],
        input_output_aliases={3: 0}, compiler_params=_cparams("arbitrary"),
    )(p, kv, dy, dp)


def _merge_fwd(p, b_ssd, b_gmlp, b_mem, name):
    t = p.shape[0]
    tm = _row_tile(t)
    g0 = P_GL // D_MODEL

    def body(g1, g2, g3, b1, b2, b3, o_ref):
        def strip(rows, carry):
            acc = _sigmoid(g1[rows, :].astype(F32)) * b1[rows, :].astype(F32)
            acc = acc + _sigmoid(g2[rows, :].astype(F32)) * b2[rows, :].astype(F32)
            acc = acc + _sigmoid(g3[rows, :].astype(F32)) * b3[rows, :].astype(F32)
            o_ref[rows, :] = acc.astype(o_ref.dtype)
            return carry

        _strips(tm, strip, 0)

    row = pl.BlockSpec((tm, D_MODEL), lambda i: (i, 0))
    return pl.pallas_call(
        body, name=name, grid=(t // tm,),
        in_specs=[pl.BlockSpec((tm, D_MODEL), lambda i, k=k: (i, g0 + k)) for k in range(3)] + [row] * 3,
        out_specs=row, out_shape=jax.ShapeDtypeStruct((t, D_MODEL), BF16), compiler_params=_cparams("parallel"),
    )(p, p, p, b_ssd, b_gmlp, b_mem)


def _merge_bwd(p, dm, b_ssd, b_gmlp, b_mem, dp, name):
    t = p.shape[0]
    tm = _row_tile(t)
    g0 = P_GL // D_MODEL

    def body(g1, g2, g3, dm_ref, b1, b2, b3, dp_ref, d1, d2, d3, dgl_ref):
        del dp_ref

        def strip(rows, carry):
            dmv = dm_ref[rows, :].astype(F32)
            for k, (g_ref, b_ref, d_ref) in enumerate(((g1, b1, d1), (g2, b2, d2), (g3, b3, d3))):
                sg = _sigmoid(g_ref[rows, :].astype(F32))
                d_ref[rows, :] = (dmv * sg).astype(d_ref.dtype)
                dgl_ref[rows, k * D_MODEL:(k + 1) * D_MODEL] = (dmv * b_ref[rows, :].astype(F32) * sg * (1.0 - sg)).astype(dgl_ref.dtype)
            return carry

        _strips(tm, strip, 0)

    row = pl.BlockSpec((tm, D_MODEL), lambda i: (i, 0))
    return pl.pallas_call(
        body, name=name, grid=(t // tm,),
        in_specs=[pl.BlockSpec((tm, D_MODEL), lambda i, k=k: (i, g0 + k)) for k in range(3)] + [row] * 4 + [ANY],
        out_specs=[row, row, row, pl.BlockSpec((tm, 3 * D_MODEL), lambda i: (i, P_GL // (3 * D_MODEL)))],
        out_shape=[jax.ShapeDtypeStruct((t, D_MODEL), BF16)] * 3 + [jax.ShapeDtypeStruct(dp.shape, dp.dtype)],
        input_output_aliases={7: 3}, compiler_params=_cparams("parallel"),
    )(p, p, p, dm, b_ssd, b_gmlp, b_mem, dp)


def _local_step(x, mem, target, w, first_weights, mixer_weights, second_weights, push):
    t = x.shape[0]
    mm = functools.partial(_matmul, tk=1024)

    ffn_weights = lambda tag: (w[f"{tag}_w_gate"], w[f"{tag}_w_up"], w[f"{tag}_w_down"])

    def arriving(tag, fetch):
        def weights(n):
            w.update(fetch(n))
            return ffn_weights(tag)
        return weights

    w = dict(w)
    h1, ffn1_saved = _ffn_forward(x, w["ffn1_norm"], arriving("ffn1", first_weights), "ffn1")
    w.update(mixer_weights(h1))
    n2 = _rms_fwd(h1, w["mix_norm"], "mix_norm")
    p = mm(n2, w["w_in_p"], mode="nt", out_dtype=BF16, tm=2048, tn=1536, name="in_proj")
    dt_raw = mm(n2, w["w_dt"], mode="nt", out_dtype=F32, tm=2048, tn=128, name="dt_proj")
    dt_bias = jnp.pad(w["ssd_dt_bias"], (0, 128 - SSD_HEADS)).reshape(1, 128)
    hv = _head_inputs(_dt_fwd(dt_raw, dt_bias, "dt_fwd"), w["ssd_a_log"], w["ssd_d"])
    xc = _conv_fwd(p, w["ssd_conv_w"], w["ssd_conv_b"], "conv_fwd")
    y_ssd_raw, y_ssd, states = _ssd_fwd(xc, p, hv, w["ssd_norm"], "ssd_fwd")
    b_col = w["gmlp_b_s"].reshape(GMLP_GROUPS, CHUNK, 1)
    y_gmlp = _gmlp_fwd(p, w["gmlp_v_norm"], w["gmlp_w_s"], b_col, "gmlp_fwd")
    mem_n = _rms_fwd(mem, w["mem_norm"], "mem_norm")
    kv = mm(mem_n, w["w_mem_kv"], mode="nn", out_dtype=F32, tm=256, tn=512, name="mem_kv")
    y_mem = _mem_fwd(p, kv, "mem_fwd")
    b_ssd = mm(y_ssd, w["w_branch_ssd"], mode="nn", out_dtype=BF16, tm=2048, tn=1024, tk=2048, name="branch_ssd")
    b_gmlp = mm(y_gmlp, w["w_branch_gmlp"], mode="nn", out_dtype=BF16, tm=2048, tn=1024, name="branch_gmlp")
    b_mem = mm(y_mem, w["w_branch_mem"], mode="nt", out_dtype=BF16, tm=2048, tn=1024, tk=MEM_W, name="branch_mem")
    merged = _merge_fwd(p, b_ssd, b_gmlp, b_mem, "merge_fwd")
    h2 = mm(merged, w["w_out"], mode="nn", out_dtype=F32, tm=2048, tn=1024, addend=h1, name="out_proj")
    h3, ffn2_saved = _ffn_forward(h2, w["ffn2_norm"], arriving("ffn2", second_weights), "ffn2")
    dh3, d_final, loss = _loss_head(h3, w["final_norm"], target, "loss_head")

    g = {"final_norm": d_final}
    big = {}
    dh2, g["ffn2_norm"] = _ffn_backward(dh3, h2, w["ffn2_norm"], w["ffn2_w_gate"], w["ffn2_w_up"], w["ffn2_w_down"], ffn2_saved,
                                        "ffn2", functools.partial(push, 0))
    dmerged = mm(dh2, w["w_out"], mode="nt", out_dtype=BF16, tm=2048, tn=1024, name="out_proj_dx")
    big["w_out"] = mm(merged, dh2, mode="tn", out_dtype=BF16, tm=1024, tn=1024, tk=2048, name="out_proj_dw")
    db_ssd, db_gmlp, db_mem, dp = _merge_bwd(p, dmerged, b_ssd, b_gmlp, b_mem, lax.empty((t, P_W), BF16), "merge_bwd")
    dy_ssd = mm(db_ssd, w["w_branch_ssd"], mode="nt", out_dtype=BF16, tm=2048, tn=2048, name="branch_ssd_dx")
    dy_gmlp = mm(db_gmlp, w["w_branch_gmlp"], mode="nt", out_dtype=BF16, tm=2048, tn=1024, name="branch_gmlp_dx")
    dy_mem = mm(db_mem, w["w_branch_mem"], mode="nn", out_dtype=BF16, tm=2048, tn=256, name="branch_mem_dx")
    big["w_branch_ssd"] = mm(y_ssd, db_ssd, mode="tn", out_dtype=BF16, tm=1024, tn=1024, tk=2048, name="branch_ssd_dw")
    big["w_branch_gmlp"] = mm(y_gmlp, db_gmlp, mode="tn", out_dtype=BF16, tm=1024, tn=1024, tk=2048, name="branch_gmlp_dw")
    big["w_branch_mem"] = mm(db_mem, y_mem, mode="tn", out_dtype=BF16, tm=1024, tn=256, tk=2048, name="branch_mem_dw")
    dp, dkv = _mem_bwd(p, kv, dy_mem, dp, "mem_bwd")
    big["w_mem_kv"] = mm(mem_n, dkv, mode="tn", out_dtype=BF16, tm=1024, tn=512, tk=256, name="mem_kv_dw")
    dmem_n = mm(dkv, w["w_mem_kv"], mode="nt", out_dtype=F32, tm=256, tn=1024, tk=512, name="mem_kv_dx")
    _, g["mem_norm"] = _rms_bwd(mem, w["mem_norm"], dmem_n, None, "mem_norm_bwd")
    dp, d_ws, d_bs, g["gmlp_v_norm"] = _gmlp_bwd(p, dy_gmlp, w["gmlp_v_norm"], w["gmlp_w_s"], b_col, dp, "gmlp_bwd")
    g["gmlp_w_s"] = d_ws
    g["gmlp_b_s"] = d_bs.reshape(GMLP_GROUPS, CHUNK)
    dxs, d_bm, d_cm, dp, ddt_w, hsums, g["ssd_norm"] = _ssd_bwd(xc, p, hv, w["ssd_norm"], y_ssd_raw, dy_ssd, states, dp, "ssd_bwd")
    heads = hsums[:, :, :HEADS_PER_GROUP]
    g["ssd_a_log"] = heads[:, 1, :].reshape(1, SSD_HEADS)
    g["ssd_d"] = heads[:, 2, :].reshape(1, SSD_HEADS)
    ddt = jnp.transpose(ddt_w[:, :, :HEADS_PER_GROUP], (1, 0, 2)).reshape(t, SSD_HEADS)
    ddt, d_bias = _dt_bwd(jnp.pad(ddt, ((0, 0), (0, 128 - SSD_HEADS))), dt_raw, dt_bias, "dt_bwd")
    g["ssd_dt_bias"] = d_bias[:, :SSD_HEADS]
    dws, dbs = [], []
    for dyc, col0, tag in ((dxs, 0, "x"), (d_bm, SSD_INNER, "b"), (d_cm, SSD_INNER + SSD_GROUPS * SSD_STATE, "c")):
        dacc, dw_c, db_c = _conv_bwd_act(p, dyc, w["ssd_conv_w"], w["ssd_conv_b"], col0, f"conv_bwd_act_{tag}")
        dp = _conv_bwd_dx(dacc, w["ssd_conv_w"], col0, dp, f"conv_bwd_dx_{tag}")
        dws.append(dw_c)
        dbs.append(db_c)
    g["ssd_conv_w"] = jnp.concatenate(dws, axis=1)
    g["ssd_conv_b"] = jnp.concatenate(dbs, axis=1)
    d_win_p = _matmul(dp, n2, mode="tn", out_dtype=BF16, tm=1536, tn=1024, tk=2048, name="in_proj_dw")
    d_wdt = mm(ddt, n2, mode="tn", out_dtype=BF16, tm=128, tn=1024, tk=2048, name="dt_proj_dw")
    sl = lambda a, o, n: a[o:o + n]
    big["w_in"] = jnp.concatenate([sl(d_win_p, P_Z, 2048), sl(d_win_p, P_XBC, XBC), d_wdt[:SSD_HEADS], sl(d_win_p, P_UV, 2048),
                                   sl(d_win_p, P_Q, MEM_W), sl(d_win_p, P_GL, 3 * D_MODEL)], axis=0)
    token = push(1, big)
    dn2 = mm(dp, w["w_in_p"], mode="nn", out_dtype=F32, tm=1024, tn=1024, tk=3584, name="in_proj_dx")
    dn2 = _matmul(ddt, w["w_dt"], mode="nn", out_dtype=F32, tm=2048, tn=1024, tk=128, addend=dn2, name="dt_proj_dx")
    dh1, g["mix_norm"] = _rms_bwd(h1, w["mix_norm"] + token, dn2, dh2, "mix_norm_bwd")
    dx, g["ffn1_norm"] = _ffn_backward(dh1, x, w["ffn1_norm"], w["ffn1_w_gate"], w["ffn1_w_up"], w["ffn1_w_down"], ffn1_saved,
                                       "ffn1", functools.partial(push, 2))
    return loss, dx, g


def _split_w_in(w_in_t):
    sl = lambda o, n: w_in_t[o:o + n]
    w_p = jnp.concatenate([sl(IN_GL, 3 * D_MODEL), sl(IN_Z, 2048), sl(IN_XBC, XBC), sl(IN_UV, 2048), sl(IN_Q, MEM_W),
                           jnp.zeros((P_W - P_USED, D_MODEL), w_in_t.dtype)], axis=0)
    w_dt = jnp.pad(sl(IN_DT, SSD_HEADS), ((0, 128 - SSD_HEADS), (0, 0)))
    return w_p, w_dt


def _pick_tile(rows, cap=512):
    best = None
    for tile in range(8, min(rows, cap) + 1, 8):
        if rows % tile == 0:
            best = tile
    return best if best is not None else rows


def _adamw(w, g, m, v, name):
    rows, lanes = w.shape
    tile = _pick_tile(rows, cap=max(64, (128 * 1024 // lanes) // 8 * 8))
    c1 = 1.0 / (1.0 - ADAM_B1 ** ADAM_STEP)
    c2 = 1.0 / (1.0 - ADAM_B2 ** ADAM_STEP)

    def body(w_ref, g_ref, m_ref, v_ref, d_ref, nm_ref, nv_ref):
        gv = g_ref[...]
        nm = ADAM_B1 * m_ref[...] + (1.0 - ADAM_B1) * gv
        nv = ADAM_B2 * v_ref[...] + (1.0 - ADAM_B2) * (gv * gv)
        nm_ref[...] = nm
        nv_ref[...] = nv
        d_ref[...] = -ADAM_LR * ((nm * c1) / (jnp.sqrt(nv * c2) + ADAM_EPS) + ADAM_WD * w_ref[...])

    blk = pl.BlockSpec((tile, lanes), lambda i: (i, 0))
    return pl.pallas_call(
        body, name=name, grid=(rows // tile,), in_specs=[blk] * 4, out_specs=[blk] * 3,
        out_shape=[jax.ShapeDtypeStruct((rows, lanes), F32)] * 3, compiler_params=_cparams("parallel"),
    )(w, g, m, v)


HBM = pl.BlockSpec(memory_space=pltpu.HBM)


def _place():
    x, y, c = lax.axis_index("x"), lax.axis_index("y"), lax.axis_index("c")
    chips = [(1 - x, y), (x, 1 - y), (1 - x, 1 - y)]
    return x, y, c, chips


SEM = pl.BlockSpec(memory_space=pltpu.SEMAPHORE)
EFFECT = pltpu.SideEffectType.DATAFLOW_SIDE_EFFECTING
N_PEER = 3


def _sem_outs():
    return tuple(pltpu.SemaphoreType.DMA(()) for _ in range(2 * N_PEER))


def _gather_start(slots, tag):
    def body(in_ref, *refs):
        del in_ref
        sems, thru, token = refs[:2 * N_PEER], refs[2 * N_PEER], refs[2 * N_PEER + 1]
        x, y, c, chips = _place()
        own = thru.at[2 * x + y, c]
        for j, chip in enumerate(chips):
            pltpu.make_async_remote_copy(src_ref=own, dst_ref=own, send_sem=sems[j], recv_sem=sems[N_PEER + j],
                                         device_id=(*chip, c), device_id_type=MESH).start()
        token[...] = jnp.zeros_like(token)

    out = pl.pallas_call(
        body, name=f"gather_{tag}_start",
        out_shape=_sem_outs() + (pltpu.HBM(slots.shape, slots.dtype), jax.ShapeDtypeStruct((8, 128), F32)),
        in_specs=(HBM,), out_specs=(SEM,) * (2 * N_PEER) + (HBM, pl.BlockSpec(memory_space=pltpu.VMEM)),
        input_output_aliases={0: 2 * N_PEER}, compiler_params=pltpu.CompilerParams(has_side_effects=EFFECT),
    )(pltpu.with_memory_space_constraint(slots, pltpu.HBM))
    return out[:2 * N_PEER], out[2 * N_PEER], out[2 * N_PEER + 1]


def _gather_wait(sems, thru, after, tag):
    def body(in_ref, *refs):
        del in_ref
        sems, out_ref = refs[:2 * N_PEER], refs[2 * N_PEER + 1]
        x, y, c, chips = _place()
        own = out_ref.at[2 * x + y, c]
        for j, (cx, cy) in enumerate(chips):
            cp = pltpu.make_async_remote_copy(src_ref=own, dst_ref=out_ref.at[2 * cx + cy, c], send_sem=sems[j],
                                              recv_sem=sems[N_PEER + j], device_id=(cx, cy, c), device_id_type=MESH)
            cp.wait_send()
            cp.wait_recv()

    return pl.pallas_call(
        body, name=f"gather_{tag}_wait", out_shape=pltpu.HBM(thru.shape, thru.dtype),
        in_specs=(HBM,) + (SEM,) * (2 * N_PEER) + (pl.BlockSpec(memory_space=pl.ANY),), out_specs=HBM,
        input_output_aliases={0: 0}, compiler_params=pltpu.CompilerParams(has_side_effects=EFFECT),
    )(thru, *sems, after)


def _gather_forward(slots, tag):
    def body(in_ref, out_ref, send_sems, recv_sems):
        del in_ref
        x, y, c, chips = _place()
        cps = []
        for j, (cx, cy) in enumerate(chips):
            landed = out_ref.at[2 * cx + cy, c]
            cps.append(pltpu.make_async_remote_copy(src_ref=landed, dst_ref=landed, send_sem=send_sems.at[j], recv_sem=recv_sems.at[j],
                                                    device_id=(x, y, 1 - c), device_id_type=MESH))
        for cp in cps:
            cp.start()
        for j, (cx, cy) in enumerate(chips):
            other = out_ref.at[2 * cx + cy, 1 - c]
            pltpu.make_async_remote_copy(src_ref=other, dst_ref=other, send_sem=send_sems.at[j], recv_sem=recv_sems.at[j],
                                         device_id=(x, y, 1 - c), device_id_type=MESH).wait_recv()
        for cp in cps:
            cp.wait_send()

    return pl.pallas_call(
        body, name=f"gather_{tag}_forward", out_shape=jax.ShapeDtypeStruct(slots.shape, slots.dtype),
        in_specs=[HBM], out_specs=HBM, input_output_aliases={0: 0},
        scratch_shapes=[pltpu.SemaphoreType.DMA((N_PEER,)), pltpu.SemaphoreType.DMA((N_PEER,))],
    )(slots)


def _scatter_start(pa, tag):
    ns, rh, lanes = pa.shape
    land = pltpu.with_memory_space_constraint(lax.empty((N_PEER, rh, lanes), pa.dtype), pltpu.HBM)

    def body(pa_ref, land_ref, *refs):
        x, y, c, chips = _place()
        for j, (cx, cy) in enumerate(chips):
            pltpu.make_async_remote_copy(src_ref=pa_ref.at[2 * cx + cy], dst_ref=land_ref.at[j], send_sem=refs[j],
                                         recv_sem=refs[N_PEER + j], device_id=(cx, cy, c), device_id_type=MESH).start()
        refs[-1][...] = jnp.zeros_like(refs[-1])

    out = pl.pallas_call(
        body, name=f"scatter_start_{tag}",
        out_shape=_sem_outs() + (pltpu.HBM(pa.shape, pa.dtype), pltpu.HBM(land.shape, land.dtype), jax.ShapeDtypeStruct((8, 128), F32)),
        in_specs=(HBM, HBM), out_specs=(SEM,) * (2 * N_PEER) + (HBM, HBM, pl.BlockSpec(memory_space=pltpu.VMEM)),
        input_output_aliases={0: 2 * N_PEER, 1: 2 * N_PEER + 1}, compiler_params=pltpu.CompilerParams(has_side_effects=EFFECT),
    )(pltpu.with_memory_space_constraint(pa, pltpu.HBM), land)
    return (out[:2 * N_PEER], out[2 * N_PEER], out[2 * N_PEER + 1]), out[2 * N_PEER + 2]


def _scatter_wait(sems, pa_thru, land_thru, after, tag):
    def body(pa_ref, land_ref, *refs):
        sems = refs[:2 * N_PEER]
        x, y, c, chips = _place()
        for j, (cx, cy) in enumerate(chips):
            cp = pltpu.make_async_remote_copy(src_ref=pa_ref.at[2 * cx + cy], dst_ref=land_ref.at[j], send_sem=sems[j],
                                              recv_sem=sems[N_PEER + j], device_id=(cx, cy, c), device_id_type=MESH)
            cp.wait_send()
            cp.wait_recv()

    return pl.pallas_call(
        body, name=f"scatter_wait_{tag}",
        out_shape=(pltpu.HBM(pa_thru.shape, pa_thru.dtype), pltpu.HBM(land_thru.shape, land_thru.dtype)),
        in_specs=(HBM, HBM) + (SEM,) * (2 * N_PEER) + (pl.BlockSpec(memory_space=pl.ANY),), out_specs=(HBM, HBM),
        input_output_aliases={0: 0, 1: 1}, compiler_params=pltpu.CompilerParams(has_side_effects=EFFECT),
    )(pa_thru, land_thru, *sems, after)


def _rs_swap(gp, tag):
    ns, _, rh, lanes = gp.shape

    def body(in_ref, out_ref, send_sem, recv_sem):
        x, y, c, _ = _place()
        cp = pltpu.make_async_remote_copy(src_ref=in_ref.at[:, 1 - c], dst_ref=out_ref, send_sem=send_sem, recv_sem=recv_sem,
                                          device_id=(x, y, 1 - c), device_id_type=MESH)
        cp.start()
        cp.wait_send()
        cp.wait_recv()

    return pl.pallas_call(
        body, name=f"rs_swap_{tag}", out_shape=jax.ShapeDtypeStruct((ns, rh, lanes), gp.dtype), in_specs=[HBM], out_specs=HBM,
        scratch_shapes=[pltpu.SemaphoreType.DMA, pltpu.SemaphoreType.DMA],
    )(gp)


def _rs_tile(rh):
    return _pick_tile(rh, cap=512)


def _rs_add(gp, recv, c, tag):
    ns, _, rh, lanes = gp.shape
    tile = rh

    def body(c_ref, a_ref, b_ref, o_ref):
        o_ref[...] = (a_ref[...].astype(F32) + b_ref[...].astype(F32)).astype(o_ref.dtype)

    return pl.pallas_call(
        body, name=f"rs_add_{tag}", out_shape=jax.ShapeDtypeStruct((ns, rh, lanes), gp.dtype),
        grid_spec=pltpu.PrefetchScalarGridSpec(
            num_scalar_prefetch=1, grid=(ns, rh // tile),
            in_specs=[pl.BlockSpec((None, None, tile, lanes), lambda s, i, c_ref: (s, c_ref[0], i, 0)),
                      pl.BlockSpec((None, tile, lanes), lambda s, i, c_ref: (s, i, 0))],
            out_specs=pl.BlockSpec((None, tile, lanes), lambda s, i, c_ref: (s, i, 0))),
        compiler_params=_cparams("parallel", "parallel"),
    )(c, gp, recv)


def _rs_sum(pa, recv, place, tag):
    ns, rh, lanes = pa.shape
    tile = _rs_tile(rh)

    def body(place_ref, a_ref, r_ref, o_ref):
        acc = a_ref[...].astype(F32)
        for j in range(ns - 1):
            acc = acc + r_ref[j].astype(F32)
        o_ref[...] = acc

    return pl.pallas_call(
        body, name=f"rs_sum_{tag}", out_shape=jax.ShapeDtypeStruct((2, rh, lanes), F32),
        grid_spec=pltpu.PrefetchScalarGridSpec(
            num_scalar_prefetch=1, grid=(rh // tile,),
            in_specs=[pl.BlockSpec((None, tile, lanes), lambda i, place_ref: (place_ref[0], i, 0)),
                      pl.BlockSpec((ns - 1, tile, lanes), lambda i, place_ref: (0, i, 0))],
            out_specs=pl.BlockSpec((None, tile, lanes), lambda i, place_ref: (place_ref[1], i, 0))),
        compiler_params=_cparams("parallel"),
    )(place, pa, recv)


def _rs_share(halves, tag):
    def body(in_ref, out_ref, send_sem, recv_sem):
        del in_ref
        x, y, c, _ = _place()
        cp = pltpu.make_async_remote_copy(src_ref=out_ref.at[c], dst_ref=out_ref.at[c], send_sem=send_sem, recv_sem=recv_sem,
                                          device_id=(x, y, 1 - c), device_id_type=MESH)
        cp.start()
        other = out_ref.at[1 - c]
        pltpu.make_async_remote_copy(src_ref=other, dst_ref=other, send_sem=send_sem, recv_sem=recv_sem,
                                     device_id=(x, y, 1 - c), device_id_type=MESH).wait_recv()
        cp.wait_send()

    return pl.pallas_call(
        body, name=f"rs_share_{tag}", out_shape=jax.ShapeDtypeStruct(halves.shape, halves.dtype), in_specs=[HBM], out_specs=HBM,
        input_output_aliases={0: 0}, scratch_shapes=[pltpu.SemaphoreType.DMA, pltpu.SemaphoreType.DMA],
    )(halves)


N_DEV = 8
SMALL_ROWS = 160


def _allreduce_small(v):
    m_per, n = v.shape

    def body(x_ref, out_ref, all_ref, send_sems, recv_sems, local_sem):
        x, y, c, chips = _place()
        me, sibling = (x, y, c), (x, y, 1 - c)

        def rows(px, py, pc):
            return all_ref.at[pl.ds((4 * px + 2 * py + pc) * m_per, m_per), :]

        def copy(k, block, to, src=None):
            return pltpu.make_async_remote_copy(src_ref=rows(*block) if src is None else src, dst_ref=rows(*block),
                                                send_sem=send_sems.at[k], recv_sem=recv_sems.at[k], device_id=to, device_id_type=MESH)

        mine = pltpu.make_async_copy(x_ref, rows(*me), local_sem)
        mine.start()
        first = [copy(0, me, sibling, src=x_ref)]
        first += [copy(1 + j, me, (*chip, c), src=x_ref) for j, chip in enumerate(chips)]
        for cp in first:
            cp.start()
        passed = [copy(4 + j, (*chip, c), sibling) for j, chip in enumerate(chips)]
        for j, chip in enumerate(chips):
            copy(1 + j, (*chip, c), me).wait_recv()
            passed[j].start()
        copy(0, sibling, me).wait_recv()
        for j, chip in enumerate(chips):
            copy(4 + j, (*chip, 1 - c), me).wait_recv()
        for cp in first + passed:
            cp.wait_send()
        mine.wait()
        step = 32
        for r in range(0, m_per, step):
            acc = all_ref[r:r + step, :]
            for d in range(1, N_DEV):
                acc = acc + all_ref[d * m_per + r:d * m_per + r + step, :]
            out_ref[r:r + step, :] = acc

    vm = pl.BlockSpec(memory_space=pltpu.VMEM)
    return pl.pallas_call(
        body, name="allreduce_small", out_shape=jax.ShapeDtypeStruct((m_per, n), v.dtype), in_specs=[vm], out_specs=vm,
        scratch_shapes=[pltpu.VMEM((N_DEV * m_per, n), v.dtype), pltpu.SemaphoreType.DMA((7,)), pltpu.SemaphoreType.DMA((7,)),
                        pltpu.SemaphoreType.DMA],
        compiler_params=pltpu.CompilerParams(vmem_limit_bytes=V7X_VMEM_LIMIT),
    )(v)


BIG = {"ffn1_w_gate": ((D_MODEL, D_FF), 1), "ffn1_w_up": ((D_MODEL, D_FF), 1), "ffn1_w_down": ((D_FF, D_MODEL), 0),
       "ffn2_w_gate": ((D_MODEL, D_FF), 1), "ffn2_w_up": ((D_MODEL, D_FF), 1), "ffn2_w_down": ((D_FF, D_MODEL), 0),
       "w_in": ((D_MODEL, IN_WIDTH), 1), "w_mem_kv": ((D_MODEL, 2 * MEM_W), 0), "w_branch_ssd": ((SSD_INNER, D_MODEL), 0),
       "w_branch_gmlp": ((GMLP_W, D_MODEL), 0), "w_branch_mem": ((MEM_W, D_MODEL), 1), "w_out": ((D_MODEL, D_MODEL), 0)}
FFN1 = ("ffn1_w_gate", "ffn1_w_up", "ffn1_w_down")
FFN2 = ("ffn2_w_gate", "ffn2_w_up", "ffn2_w_down")
MIXER = ("w_out", "w_branch_ssd", "w_branch_gmlp", "w_branch_mem", "w_mem_kv", "w_in")
REDUCE_GROUPS = (FFN2, MIXER, FFN1)
CONV_W_ROWS = 8


def _shard_rows_of(name):
    (a, b), _ = BIG[name]
    return a * b // N_SHARD // LANES


def _group_rows(names, extra=0):
    return -(-(sum(_shard_rows_of(n) for n in names) + extra) // 32) * 32

SMALL = [("ffn1_norm", 1), ("mix_norm", 1), ("mem_norm", 1), ("ssd_conv_b", 3), ("heads", 1), ("ssd_norm", 2),
         ("gmlp_v_norm", 1), ("gmlp_w_s", 128), ("gmlp_b_s", 1), ("ffn2_norm", 1), ("final_norm", 1), ("ssd_conv_w", 12)]
assert sum(n for _, n in SMALL) <= SMALL_ROWS
HEAD_VECS = ("ssd_dt_bias", "ssd_a_log", "ssd_d")


def _pack_small(vals, loss=None):
    parts = []
    for name, nrows in SMALL:
        if name == "heads":
            row = jnp.concatenate([vals[k].reshape(-1) for k in HEAD_VECS]
                                  + [jnp.zeros((1,), F32) if loss is None else loss.reshape(1)])
            parts.append(jnp.pad(row, (0, LANES - row.shape[0])).reshape(1, LANES))
        elif name in vals:
            parts.append(vals[name].reshape(nrows, LANES))
        else:
            parts.append(jnp.zeros((nrows, LANES), F32))
    buf = jnp.concatenate(parts, axis=0)
    return jnp.pad(buf, ((0, SMALL_ROWS - buf.shape[0]), (0, 0)))


def _unpack_small(buf):
    out, r = {}, 0
    for name, nrows in SMALL:
        blk = buf[r:r + nrows]
        r += nrows
        if name == "heads":
            for i, k in enumerate(HEAD_VECS):
                out[k] = blk[0, i * SSD_HEADS:(i + 1) * SSD_HEADS]
            out["loss"] = blk[0, 3 * SSD_HEADS]
        else:
            out[name] = blk
    return out


def _wire_shape(name):
    (a, b), axis = BIG[name]
    return (b, a) if axis == 1 else (a, b)


def _pack_weights(given, names, conv=False):
    parts = [(given[n][0].T if BIG[n][1] == 1 else given[n][0]).astype(BF16).reshape(_shard_rows_of(n), LANES) for n in names]
    if conv:
        pairs = lax.bitcast_convert_type(given["ssd_conv_w"], BF16).reshape(-1)
        parts.append(jnp.pad(pairs, (0, CONV_W_ROWS * LANES - pairs.shape[0])).reshape(CONV_W_ROWS, LANES))
    total = _group_rows(names, CONV_W_ROWS if conv else 0)
    packed = jnp.concatenate(parts, axis=0)
    packed = jnp.pad(packed, ((0, total - packed.shape[0]), (0, 0))).reshape(1, 2, total // 2, LANES)
    return jnp.broadcast_to(packed, (N_SHARD, 2, total // 2, LANES))


def _unpack_weights(slots, names, conv=False):
    rows = slots.reshape(N_SHARD, -1, LANES)
    out, r = {}, 0
    for name in names:
        n = _shard_rows_of(name)
        out[name] = rows[:, r:r + n].reshape(_wire_shape(name))
        r += n
    if conv:
        cols = XBC // N_SHARD
        pairs = rows[:, r:r + CONV_W_ROWS].reshape(N_SHARD, -1)[:, :SSD_CONV * cols * 2].reshape(N_SHARD, SSD_CONV, cols, 2)
        out["ssd_conv_w"] = jnp.transpose(lax.bitcast_convert_type(pairs, F32), (1, 0, 2)).reshape(SSD_CONV, XBC)
    return out


def _pack_grads(grads, names):
    total = _group_rows(names)
    parts = [grads[n].astype(BF16).reshape(N_SHARD, _shard_rows_of(n), LANES) for n in names]
    pad = total - sum(p.shape[1] for p in parts)
    if pad:
        parts.append(jnp.zeros((N_SHARD, pad, LANES), BF16))
    return jnp.concatenate(parts, axis=1).reshape(N_SHARD, 2, total // 2, LANES)


def kernel(x, mem, ffn1_norm, ffn1_w_gate, ffn1_w_up, ffn1_w_down, mix_norm, mem_norm, w_in, ssd_conv_w, ssd_conv_b, ssd_dt_bias, ssd_a_log, ssd_d, ssd_norm, gmlp_v_norm, gmlp_w_s, gmlp_b_s, w_mem_kv, w_branch_ssd, w_branch_gmlp, w_branch_mem, w_out, ffn2_norm, ffn2_w_gate, ffn2_w_up, ffn2_w_down, final_norm, loss_target, m_ffn1_norm, m_ffn1_w_gate, m_ffn1_w_up, m_ffn1_w_down, m_mix_norm, m_mem_norm, m_w_in, m_ssd_conv_w, m_ssd_conv_b, m_ssd_dt_bias, m_ssd_a_log, m_ssd_d, m_ssd_norm, m_gmlp_v_norm, m_gmlp_w_s, m_gmlp_b_s, m_w_mem_kv, m_w_branch_ssd, m_w_branch_gmlp, m_w_branch_mem, m_w_out, m_ffn2_norm, m_ffn2_w_gate, m_ffn2_w_up, m_ffn2_w_down, m_final_norm, v_ffn1_norm, v_ffn1_w_gate, v_ffn1_w_up, v_ffn1_w_down, v_mix_norm, v_mem_norm, v_w_in, v_ssd_conv_w, v_ssd_conv_b, v_ssd_dt_bias, v_ssd_a_log, v_ssd_d, v_ssd_norm, v_gmlp_v_norm, v_gmlp_w_s, v_gmlp_b_s, v_w_mem_kv, v_w_branch_ssd, v_w_branch_gmlp, v_w_branch_mem, v_w_out, v_ffn2_norm, v_ffn2_w_gate, v_ffn2_w_up, v_ffn2_w_down, v_final_norm):
    given = dict(x=x, mem=mem, ffn1_norm=ffn1_norm, ffn1_w_gate=ffn1_w_gate, ffn1_w_up=ffn1_w_up, ffn1_w_down=ffn1_w_down, mix_norm=mix_norm, mem_norm=mem_norm, w_in=w_in, ssd_conv_w=ssd_conv_w, ssd_conv_b=ssd_conv_b, ssd_dt_bias=ssd_dt_bias, ssd_a_log=ssd_a_log, ssd_d=ssd_d, ssd_norm=ssd_norm, gmlp_v_norm=gmlp_v_norm, gmlp_w_s=gmlp_w_s, gmlp_b_s=gmlp_b_s, w_mem_kv=w_mem_kv, w_branch_ssd=w_branch_ssd, w_branch_gmlp=w_branch_gmlp, w_branch_mem=w_branch_mem, w_out=w_out, ffn2_norm=ffn2_norm, ffn2_w_gate=ffn2_w_gate, ffn2_w_up=ffn2_w_up, ffn2_w_down=ffn2_w_down, final_norm=final_norm, loss_target=loss_target, m_ffn1_norm=m_ffn1_norm, m_ffn1_w_gate=m_ffn1_w_gate, m_ffn1_w_up=m_ffn1_w_up, m_ffn1_w_down=m_ffn1_w_down, m_mix_norm=m_mix_norm, m_mem_norm=m_mem_norm, m_w_in=m_w_in, m_ssd_conv_w=m_ssd_conv_w, m_ssd_conv_b=m_ssd_conv_b, m_ssd_dt_bias=m_ssd_dt_bias, m_ssd_a_log=m_ssd_a_log, m_ssd_d=m_ssd_d, m_ssd_norm=m_ssd_norm, m_gmlp_v_norm=m_gmlp_v_norm, m_gmlp_w_s=m_gmlp_w_s, m_gmlp_b_s=m_gmlp_b_s, m_w_mem_kv=m_w_mem_kv, m_w_branch_ssd=m_w_branch_ssd, m_w_branch_gmlp=m_w_branch_gmlp, m_w_branch_mem=m_w_branch_mem, m_w_out=m_w_out, m_ffn2_norm=m_ffn2_norm, m_ffn2_w_gate=m_ffn2_w_gate, m_ffn2_w_up=m_ffn2_w_up, m_ffn2_w_down=m_ffn2_w_down, m_final_norm=m_final_norm, v_ffn1_norm=v_ffn1_norm, v_ffn1_w_gate=v_ffn1_w_gate, v_ffn1_w_up=v_ffn1_w_up, v_ffn1_w_down=v_ffn1_w_down, v_mix_norm=v_mix_norm, v_mem_norm=v_mem_norm, v_w_in=v_w_in, v_ssd_conv_w=v_ssd_conv_w, v_ssd_conv_b=v_ssd_conv_b, v_ssd_dt_bias=v_ssd_dt_bias, v_ssd_a_log=v_ssd_a_log, v_ssd_d=v_ssd_d, v_ssd_norm=v_ssd_norm, v_gmlp_v_norm=v_gmlp_v_norm, v_gmlp_w_s=v_gmlp_w_s, v_gmlp_b_s=v_gmlp_b_s, v_w_mem_kv=v_w_mem_kv, v_w_branch_ssd=v_w_branch_ssd, v_w_branch_gmlp=v_w_branch_gmlp, v_w_branch_mem=v_w_branch_mem, v_w_out=v_w_out, v_ffn2_norm=v_ffn2_norm, v_ffn2_w_gate=v_ffn2_w_gate, v_ffn2_w_up=v_ffn2_w_up, v_ffn2_w_down=v_ffn2_w_down, v_final_norm=v_final_norm)
    weights = [n for n in given if n not in ("x", "mem", "loss_target") and not n.startswith(("m_", "v_"))]
    xi, yi, ci = lax.axis_index("x"), lax.axis_index("y"), lax.axis_index("c")
    chip = (2 * xi + yi).astype(jnp.int32)
    core = ci.astype(jnp.int32)
    conv_cols = XBC // N_SHARD

    copies = {"first": _gather_start(_pack_weights(given, FFN1), "first")}
    tied = dict(given)
    tied["w_in"], tied["ffn2_w_gate"], _ = lax.optimization_barrier((given["w_in"], given["ffn2_w_gate"], copies["first"][2]))
    packed = {"mixer": _pack_weights(tied, MIXER, conv=True), "ffn2": _pack_weights(tied, FFN2)}
    w = {}
    for name in ("ffn1_norm", "mix_norm", "mem_norm", "ssd_conv_b", "ssd_norm", "gmlp_v_norm", "ffn2_norm", "final_norm"):
        w[name] = given[name].reshape(1, -1)
    w["ffn1_norm"] = w["ffn1_norm"] + copies["first"][2][0:1, 0:1]
    for name in HEAD_VECS:
        w[name] = given[name].reshape(-1)
    w["gmlp_w_s"] = given["gmlp_w_s"][0]
    w["gmlp_b_s"] = given["gmlp_b_s"][0]

    def arrived(tag, after):
        sems, thru, _ = copies[tag]
        return _gather_forward(_gather_wait(sems, thru, after, tag), tag)

    def then_start(slots, tag, weights, name):
        slots, nxt = lax.optimization_barrier((slots, packed[tag]))
        copies[tag] = _gather_start(nxt, tag)
        unpacked = weights(slots)
        unpacked[name], _ = lax.optimization_barrier((unpacked[name], copies[tag][2]))
        return unpacked

    def first_weights(after):
        after, packed["mixer"], packed["ffn2"] = lax.optimization_barrier((after, packed["mixer"], packed["ffn2"]))
        return then_start(arrived("first", after), "mixer", lambda s: _unpack_weights(s, FFN1), FFN1[0])

    def mixer_weights(after):
        def unpack(slots):
            rest = _unpack_weights(slots, MIXER, conv=True)
            rest["w_in_p"], rest["w_dt"] = _split_w_in(rest.pop("w_in"))
            return rest
        return then_start(arrived("mixer", after), "ffn2", unpack, "w_in_p")

    def second_weights(after):
        return _unpack_weights(arrived("ffn2", after), FFN2)

    pending = {}

    def push(k, group_grads):
        gp = _pack_grads(group_grads, REDUCE_GROUPS[k])
        pa = _rs_add(gp, _rs_swap(gp, k), core.reshape(1), k)
        pending[k], token = _scatter_start(pa, k)
        return token[0:1, 0:1]

    def reduced(k, after):
        pa, land = _scatter_wait(*pending[k], after, k)
        gsum = _rs_share(_rs_sum(pa, land, jnp.stack([chip, core]), k), k)
        rows = gsum.reshape(-1, LANES)
        out, r = {}, 0
        for name in REDUCE_GROUPS[k]:
            n = _shard_rows_of(name)
            a, b = given[name].shape[1:]
            out[name] = rows[r:r + n].reshape(b, a).T if BIG[name][1] == 1 else rows[r:r + n].reshape(a, b)
            r += n
        return out

    loss_part, grad_x, g = _local_step(x[0], mem[0], loss_target[0], w, first_weights, mixer_weights, second_weights, push)

    grads, deltas, new_m, new_v = {}, {}, {}, {}

    def update(k, after):
        for name, gl in reduced(k, after).items():
            d, nm, nv = _adamw(given[name][0], gl, given["m_" + name][0], given["v_" + name][0], f"adamw_{name}")
            grads[name], deltas[name], new_m[name], new_v[name] = (a[None] for a in (gl, d, nm, nv))

    update(0, grad_x)
    update(1, deltas[REDUCE_GROUPS[0][-1]])

    small_vals = {k: g[k] for k, _ in SMALL if k != "heads"}
    small_vals.update({k: g[k] for k in HEAD_VECS})
    red = _unpack_small(_allreduce_small(_pack_small(small_vals, loss=loss_part[0, 0])))
    update(2, deltas[REDUCE_GROUPS[1][-1]])
    conv_g = lax.dynamic_slice_in_dim(red["ssd_conv_w"].reshape(SSD_CONV, XBC), chip * conv_cols, conv_cols, axis=1)
    d, nm, nv = _adamw(given["ssd_conv_w"][0], conv_g, given["m_ssd_conv_w"][0], given["v_ssd_conv_w"][0], "adamw_conv_w")
    grads["ssd_conv_w"], deltas["ssd_conv_w"], new_m["ssd_conv_w"], new_v["ssd_conv_w"] = (a[None] for a in (conv_g, d, nm, nv))
    for k in [k for k, _ in SMALL if k not in ("heads", "ssd_conv_w")] + list(HEAD_VECS):
        shape = given[k].shape
        as2d = lambda a: a.reshape(-1, shape[-1])
        d, nm, nv = _adamw(as2d(given[k]), as2d(red[k]), as2d(given["m_" + k]), as2d(given["v_" + k]), f"adamw_{k}")
        grads[k], deltas[k], new_m[k], new_v[k] = (a.reshape(shape) for a in (red[k], d, nm, nv))

    return (red["loss"], grad_x[None], *[grads[n] for n in weights], *[deltas[n] for n in weights],
            *[new_m[n] for n in weights], *[new_v[n] for n in weights])
```
